```python
import jax, jax.numpy as jnp
from jax import lax
import numpy as np

D_MODEL = 1024
BATCH = 8
SEQ = 8192
DEPTH = 2

N_EVEN = (DEPTH + 1) // 2
N_ODD = DEPTH // 2
BLOCK = 128
D_FF = 2816
EPS = 1e-6
SB_HEADS = 8
SB_HEAD_DIM = 64
SB_WIDTH = SB_HEADS * SB_HEAD_DIM
SG_GROUPS = 8
SG_GROUP_DIM = 64
SG_WIDTH = SG_GROUPS * SG_GROUP_DIM
SG_CHUNK = 128
EVEN_IN = 3 * SB_WIDTH + 2 * SG_WIDTH
EVEN_MIX = SB_WIDTH + SG_WIDTH
MLA_HEADS = 16
MLA_NOPE = 64
MLA_ROPE = 32
MLA_QK = MLA_NOPE + MLA_ROPE
MLA_V = 64
MLA_Q_LORA = 512
MLA_KV_LORA = 256
MLA_IN = MLA_Q_LORA + MLA_KV_LORA + MLA_ROPE
MLA_MIX = MLA_HEADS * MLA_V
ROPE_THETA = 10000.0
MEM_TOKENS = 256
MEM_HEADS = 4
MEM_HEAD_DIM = D_MODEL // MEM_HEADS

kernel_name = 'hybrid_sb_gmlp_mla_macaron_trunk'


def _rmsnorm(x, g):
    xf = x.astype(jnp.float32)
    y = xf * lax.rsqrt(jnp.mean(xf * xf, axis=-1, keepdims=True) + EPS)
    return (y * g.astype(jnp.float32)).astype(x.dtype)


def _layernorm(x, g, b):
    xf = x.astype(jnp.float32)
    mu = jnp.mean(xf, axis=-1, keepdims=True)
    var = jnp.mean(jnp.square(xf - mu), axis=-1, keepdims=True)
    y = (xf - mu) * lax.rsqrt(var + EPS)
    return (y * g.astype(jnp.float32) + b.astype(jnp.float32)).astype(x.dtype)


def _swiglu(h, w_gu, w_down):
    gate, up = jnp.split(h @ w_gu, 2, axis=-1)
    return (jax.nn.silu(gate) * up) @ w_down


def _to_blocks(t):
    b, s, h, d = t.shape
    return t.reshape(b, s // BLOCK, BLOCK, h, d).transpose(1, 0, 2, 3, 4)


def _from_blocks(t):
    nb, b, l, h, d = t.shape
    return t.transpose(1, 0, 2, 3, 4).reshape(b, nb * l, h, d)


def _rope(t, positions):
    half = t.shape[-1] // 2
    inv_freq = ROPE_THETA ** (-jnp.arange(half, dtype=jnp.float32) / half)
    ang = positions.astype(jnp.float32)[:, :, None, None] * inv_freq
    cos, sin = jnp.cos(ang), jnp.sin(ang)
    tf = t.astype(jnp.float32)
    t1, t2 = tf[..., :half], tf[..., half:]
    return jnp.concatenate([t1 * cos - t2 * sin, t1 * sin + t2 * cos], axis=-1).astype(t.dtype)


def _stick_breaking_attention(q, k, v):
    s_len = q.shape[1]
    scale = SB_HEAD_DIM ** -0.5
    k_pos = jnp.arange(s_len)

    def block(args):
        q_blk, i = args
        z = jnp.einsum('bqhd,bkhd->bhqk', q_blk, k).astype(jnp.float32) * scale
        q_pos = i * BLOCK + jnp.arange(BLOCK)
        strict = (k_pos[None, :] < q_pos[:, None])[None, None]
        log_stay = jnp.where(strict, jax.nn.log_sigmoid(-z), 0.0)
        log_rest = lax.cumsum(log_stay, axis=3, reverse=True) - log_stay
        w = jnp.where(strict, jnp.exp(jax.nn.log_sigmoid(z) + log_rest), 0.0)
        return jnp.einsum('bhqk,bkhd->bqhd', w.astype(v.dtype), v)

    out = lax.map(block, (_to_blocks(q), jnp.arange(s_len // BLOCK)))
    return _from_blocks(out)


def _causal_softmax_attention(q, k, v):
    s_len = q.shape[1]
    scale = q.shape[-1] ** -0.5
    k_pos = jnp.arange(s_len)

    def block(args):
        q_blk, i = args
        sc = jnp.einsum('bqhd,bkhd->bhqk', q_blk, k).astype(jnp.float32) * scale
        q_pos = i * BLOCK + jnp.arange(BLOCK)
        causal = (k_pos[None, :] <= q_pos[:, None])[None, None]
        p = jax.nn.softmax(jnp.where(causal, sc, -jnp.inf), axis=-1)
        return jnp.einsum('bhqk,bkhd->bqhd', p.astype(v.dtype), v)

    out = lax.map(block, (_to_blocks(q), jnp.arange(s_len // BLOCK)))
    return _from_blocks(out)


def _even_mixer(h, w_in, ln_g, ln_b, sgu_w, sgu_b, w_out):
    b, s, _ = h.shape
    q, k, v, z = jnp.split(h @ w_in, [SB_WIDTH, 2 * SB_WIDTH, 3 * SB_WIDTH], axis=-1)
    heads = lambda t: t.reshape(b, s, SB_HEADS, SB_HEAD_DIM)
    o_sb = _stick_breaking_attention(heads(q), heads(k), heads(v)).reshape(b, s, SB_WIDTH)
    u, g = jnp.split(jax.nn.gelu(z), 2, axis=-1)
    g = _layernorm(g, ln_g, ln_b).reshape(b, s // SG_CHUNK, SG_CHUNK, SG_GROUPS, SG_GROUP_DIM)
    tri = jnp.tril(jnp.ones((SG_CHUNK, SG_CHUNK), dtype=sgu_w.dtype))
    mixed = jnp.einsum('gts,bcsgd->bctgd', sgu_w * tri, g) + sgu_b.T[None, None, :, :, None]
    o_sg = u * mixed.reshape(b, s, SG_WIDTH)
    return jnp.concatenate([o_sb, o_sg], axis=-1) @ w_out


def _mla_mixer(h, positions, w_in, q_lora_g, kv_lora_g, w_uq, w_ukv, q_g, k_g, w_out):
    b, s, _ = h.shape
    c_q, c_kv, k_r = jnp.split(h @ w_in, [MLA_Q_LORA, MLA_Q_LORA + MLA_KV_LORA], axis=-1)
    q = (_rmsnorm(c_q, q_lora_g) @ w_uq).reshape(b, s, MLA_HEADS, MLA_QK)
    kv = (_rmsnorm(c_kv, kv_lora_g) @ w_ukv).reshape(b, s, MLA_HEADS, MLA_NOPE + MLA_V)
    k_nope, v = kv[..., :MLA_NOPE], kv[..., MLA_NOPE:]
    k_r = jnp.broadcast_to(k_r[:, :, None, :], (b, s, MLA_HEADS, MLA_ROPE))
    k = jnp.concatenate([k_nope, k_r], axis=-1)
    q = _rmsnorm(q, q_g)
    k = _rmsnorm(k, k_g)
    q = jnp.concatenate([q[..., :MLA_NOPE], _rope(q[..., MLA_NOPE:], positions)], axis=-1)
    k = jnp.concatenate([k[..., :MLA_NOPE], _rope(k[..., MLA_NOPE:], positions)], axis=-1)
    o = _causal_softmax_attention(q, k, v).reshape(b, s, MLA_MIX)
    return o @ w_out


def _memory_cross_attention(hq, hm, wq, wkv, q_g, k_g, wo):
    b, s, _ = hq.shape
    m = hm.shape[1]
    q = _rmsnorm((hq @ wq).reshape(b, s, MEM_HEADS, MEM_HEAD_DIM), q_g)
    k, v = jnp.split((hm @ wkv).reshape(b, m, MEM_HEADS, 2 * MEM_HEAD_DIM), 2, axis=-1)
    k = _rmsnorm(k, k_g)
    sc = jnp.einsum('bqhd,bmhd->bhqm', q, k).astype(jnp.float32) * (MEM_HEAD_DIM ** -0.5)
    p = jax.nn.softmax(sc, axis=-1)
    o = jnp.einsum('bhqm,bmhd->bqhd', p.astype(v.dtype), v).reshape(b, s, D_MODEL)
    return o @ wo


def _w(k, shape, fan_in):
    return jax.random.normal(k, shape, jnp.float32) * (fan_in ** -0.5)


def _gain(k, shape):
    return 1.0 + 0.02 * jax.random.normal(k, shape, jnp.float32)


def _fwd_setup_inputs(seed: int = 0) -> dict:
    key = jax.random.key(seed)
    ks = list(jax.random.split(key, 32))
    nk = ks.pop
    inp = {}
    inp['x'] = jax.random.normal(nk(), (BATCH, SEQ, D_MODEL), jnp.float32)
    inp['mem'] = jax.random.normal(nk(), (BATCH, MEM_TOKENS, D_MODEL), jnp.float32)
    inp['positions'] = jnp.broadcast_to(jnp.arange(SEQ, dtype=jnp.int32)[None, :], (BATCH, SEQ))
    inp['ffn_pre_norm'] = _gain(nk(), (DEPTH, D_MODEL))
    inp['ffn_pre_w_gu'] = _w(nk(), (DEPTH, D_MODEL, 2 * D_FF), D_MODEL)
    inp['ffn_pre_w_down'] = _w(nk(), (DEPTH, D_FF, D_MODEL), D_FF)
    inp['mix_norm'] = _gain(nk(), (DEPTH, D_MODEL))
    inp['sbg_w_in'] = _w(nk(), (N_EVEN, D_MODEL, EVEN_IN), D_MODEL)
    inp['sgu_ln_gain'] = _gain(nk(), (N_EVEN, SG_WIDTH))
    inp['sgu_ln_bias'] = 0.02 * jax.random.normal(nk(), (N_EVEN, SG_WIDTH), jnp.float32)
    inp['sgu_w'] = _w(nk(), (N_EVEN, SG_GROUPS, SG_CHUNK, SG_CHUNK), SG_CHUNK)
    inp['sgu_b'] = 1.0 + 0.1 * jax.random.normal(nk(), (N_EVEN, SG_GROUPS, SG_CHUNK), jnp.float32)
    inp['sbg_w_out'] = _w(nk(), (N_EVEN, EVEN_MIX, D_MODEL), EVEN_MIX)
    inp['mla_w_in'] = _w(nk(), (N_ODD, D_MODEL, MLA_IN), D_MODEL)
    inp['mla_q_lora_gain'] = _gain(nk(), (N_ODD, MLA_Q_LORA))
    inp['mla_kv_lora_gain'] = _gain(nk(), (N_ODD, MLA_KV_LORA))
    inp['mla_w_uq'] = _w(nk(), (N_ODD, MLA_Q_LORA, MLA_HEADS * MLA_QK), MLA_Q_LORA)
    inp['mla_w_ukv'] = _w(nk(), (N_ODD, MLA_KV_LORA, MLA_HEADS * (MLA_NOPE + MLA_V)), MLA_KV_LORA)
    inp['mla_q_gain'] = _gain(nk(), (N_ODD, MLA_QK))
    inp['mla_k_gain'] = _gain(nk(), (N_ODD, MLA_QK))
    inp['mla_w_out'] = _w(nk(), (N_ODD, MLA_MIX, D_MODEL), MLA_MIX)
    inp['xmem_norm'] = _gain(nk(), (DEPTH, D_MODEL))
    inp['xmem_mem_norm'] = _gain(nk(), (DEPTH, D_MODEL))
    inp['xmem_wq'] = _w(nk(), (DEPTH, D_MODEL, D_MODEL), D_MODEL)
    inp['xmem_wkv'] = _w(nk(), (DEPTH, D_MODEL, 2 * D_MODEL), D_MODEL)
    inp['xmem_q_gain'] = _gain(nk(), (DEPTH, MEM_HEAD_DIM))
    inp['xmem_k_gain'] = _gain(nk(), (DEPTH, MEM_HEAD_DIM))
    inp['xmem_wo'] = _w(nk(), (DEPTH, D_MODEL, D_MODEL), D_MODEL)
    inp['ffn_post_norm'] = _gain(nk(), (DEPTH, D_MODEL))
    inp['ffn_post_w_gu'] = _w(nk(), (DEPTH, D_MODEL, 2 * D_FF), D_MODEL)
    inp['ffn_post_w_down'] = _w(nk(), (DEPTH, D_FF, D_MODEL), D_FF)
    return inp


def _fwd_reference(x, mem, positions,
              ffn_pre_norm, ffn_pre_w_gu, ffn_pre_w_down,
              mix_norm,
              sbg_w_in, sgu_ln_gain, sgu_ln_bias, sgu_w, sgu_b, sbg_w_out,
              mla_w_in, mla_q_lora_gain, mla_kv_lora_gain, mla_w_uq, mla_w_ukv,
              mla_q_gain, mla_k_gain, mla_w_out,
              xmem_norm, xmem_mem_norm, xmem_wq, xmem_wkv, xmem_q_gain, xmem_k_gain, xmem_wo,
              ffn_post_norm, ffn_post_w_gu, ffn_post_w_down):
    for layer in range(DEPTH):
        x = x + 0.5 * _swiglu(_rmsnorm(x, ffn_pre_norm[layer]),
                              ffn_pre_w_gu[layer], ffn_pre_w_down[layer])
        h = _rmsnorm(x, mix_norm[layer])
        if layer % 2 == 0:
            e = layer // 2
            x = x + _even_mixer(h, sbg_w_in[e], sgu_ln_gain[e], sgu_ln_bias[e],
                                sgu_w[e], sgu_b[e], sbg_w_out[e])
        else:
            o = layer // 2
            x = x + _mla_mixer(h, positions, mla_w_in[o], mla_q_lora_gain[o],
                               mla_kv_lora_gain[o], mla_w_uq[o], mla_w_ukv[o],
                               mla_q_gain[o], mla_k_gain[o], mla_w_out[o])
        x = x + _memory_cross_attention(_rmsnorm(x, xmem_norm[layer]),
                                        _rmsnorm(mem, xmem_mem_norm[layer]),
                                        xmem_wq[layer], xmem_wkv[layer],
                                        xmem_q_gain[layer], xmem_k_gain[layer], xmem_wo[layer])
        x = x + 0.5 * _swiglu(_rmsnorm(x, ffn_post_norm[layer]),
                              ffn_post_w_gu[layer], ffn_post_w_down[layer])
    return x


import jax as _jax
import jax.numpy as _jnp

TWIN_FORMAT = 'train_step'
FWD_PARAMS = ['x', 'mem', 'positions', 'ffn_pre_norm', 'ffn_pre_w_gu', 'ffn_pre_w_down', 'mix_norm', 'sbg_w_in', 'sgu_ln_gain', 'sgu_ln_bias', 'sgu_w', 'sgu_b', 'sbg_w_out', 'mla_w_in', 'mla_q_lora_gain', 'mla_kv_lora_gain', 'mla_w_uq', 'mla_w_ukv', 'mla_q_gain', 'mla_k_gain', 'mla_w_out', 'xmem_norm', 'xmem_mem_norm', 'xmem_wq', 'xmem_wkv', 'xmem_q_gain', 'xmem_k_gain', 'xmem_wo', 'ffn_post_norm', 'ffn_post_w_gu', 'ffn_post_w_down']
TWIN_WEIGHTS = ['ffn_pre_norm', 'ffn_pre_w_gu', 'ffn_pre_w_down', 'mix_norm', 'sbg_w_in', 'sgu_ln_gain', 'sgu_ln_bias', 'sgu_w', 'sgu_b', 'sbg_w_out', 'mla_w_in', 'mla_q_lora_gain', 'mla_kv_lora_gain', 'mla_w_uq', 'mla_w_ukv', 'mla_q_gain', 'mla_k_gain', 'mla_w_out', 'xmem_norm', 'xmem_mem_norm', 'xmem_wq', 'xmem_wkv', 'xmem_q_gain', 'xmem_k_gain', 'xmem_wo', 'ffn_post_norm', 'ffn_post_w_gu', 'ffn_post_w_down']
TWIN_DIFF_INPUT = 'x'
TWIN_INPUTS = ['x', 'mem', 'positions', 'ffn_pre_norm', 'ffn_pre_w_gu', 'ffn_pre_w_down', 'mix_norm', 'sbg_w_in', 'sgu_ln_gain', 'sgu_ln_bias', 'sgu_w', 'sgu_b', 'sbg_w_out', 'mla_w_in', 'mla_q_lora_gain', 'mla_kv_lora_gain', 'mla_w_uq', 'mla_w_ukv', 'mla_q_gain', 'mla_k_gain', 'mla_w_out', 'xmem_norm', 'xmem_mem_norm', 'xmem_wq', 'xmem_wkv', 'xmem_q_gain', 'xmem_k_gain', 'xmem_wo', 'ffn_post_norm', 'ffn_post_w_gu', 'ffn_post_w_down', 'loss_target', 'm_ffn_pre_norm', 'm_ffn_pre_w_gu', 'm_ffn_pre_w_down', 'm_mix_norm', 'm_sbg_w_in', 'm_sgu_ln_gain', 'm_sgu_ln_bias', 'm_sgu_w', 'm_sgu_b', 'm_sbg_w_out', 'm_mla_w_in', 'm_mla_q_lora_gain', 'm_mla_kv_lora_gain', 'm_mla_w_uq', 'm_mla_w_ukv', 'm_mla_q_gain', 'm_mla_k_gain', 'm_mla_w_out', 'm_xmem_norm', 'm_xmem_mem_norm', 'm_xmem_wq', 'm_xmem_wkv', 'm_xmem_q_gain', 'm_xmem_k_gain', 'm_xmem_wo', 'm_ffn_post_norm', 'm_ffn_post_w_gu', 'm_ffn_post_w_down', 'v_ffn_pre_norm', 'v_ffn_pre_w_gu', 'v_ffn_pre_w_down', 'v_mix_norm', 'v_sbg_w_in', 'v_sgu_ln_gain', 'v_sgu_ln_bias', 'v_sgu_w', 'v_sgu_b', 'v_sbg_w_out', 'v_mla_w_in', 'v_mla_q_lora_gain', 'v_mla_kv_lora_gain', 'v_mla_w_uq', 'v_mla_w_ukv', 'v_mla_q_gain', 'v_mla_k_gain', 'v_mla_w_out', 'v_xmem_norm', 'v_xmem_mem_norm', 'v_xmem_wq', 'v_xmem_wkv', 'v_xmem_q_gain', 'v_xmem_k_gain', 'v_xmem_wo', 'v_ffn_post_norm', 'v_ffn_post_w_gu', 'v_ffn_post_w_down']
TWIN_OUTPUTS = ['loss', 'grad_x', 'grad_ffn_pre_norm', 'grad_ffn_pre_w_gu', 'grad_ffn_pre_w_down', 'grad_mix_norm', 'grad_sbg_w_in', 'grad_sgu_ln_gain', 'grad_sgu_ln_bias', 'grad_sgu_w', 'grad_sgu_b', 'grad_sbg_w_out', 'grad_mla_w_in', 'grad_mla_q_lora_gain', 'grad_mla_kv_lora_gain', 'grad_mla_w_uq', 'grad_mla_w_ukv', 'grad_mla_q_gain', 'grad_mla_k_gain', 'grad_mla_w_out', 'grad_xmem_norm', 'grad_xmem_mem_norm', 'grad_xmem_wq', 'grad_xmem_wkv', 'grad_xmem_q_gain', 'grad_xmem_k_gain', 'grad_xmem_wo', 'grad_ffn_post_norm', 'grad_ffn_post_w_gu', 'grad_ffn_post_w_down', 'delta_ffn_pre_norm', 'delta_ffn_pre_w_gu', 'delta_ffn_pre_w_down', 'delta_mix_norm', 'delta_sbg_w_in', 'delta_sgu_ln_gain', 'delta_sgu_ln_bias', 'delta_sgu_w', 'delta_sgu_b', 'delta_sbg_w_out', 'delta_mla_w_in', 'delta_mla_q_lora_gain', 'delta_mla_kv_lora_gain', 'delta_mla_w_uq', 'delta_mla_w_ukv', 'delta_mla_q_gain', 'delta_mla_k_gain', 'delta_mla_w_out', 'delta_xmem_norm', 'delta_xmem_mem_norm', 'delta_xmem_wq', 'delta_xmem_wkv', 'delta_xmem_q_gain', 'delta_xmem_k_gain', 'delta_xmem_wo', 'delta_ffn_post_norm', 'delta_ffn_post_w_gu', 'delta_ffn_post_w_down', 'new_m_ffn_pre_norm', 'new_m_ffn_pre_w_gu', 'new_m_ffn_pre_w_down', 'new_m_mix_norm', 'new_m_sbg_w_in', 'new_m_sgu_ln_gain', 'new_m_sgu_ln_bias', 'new_m_sgu_w', 'new_m_sgu_b', 'new_m_sbg_w_out', 'new_m_mla_w_in', 'new_m_mla_q_lora_gain', 'new_m_mla_kv_lora_gain', 'new_m_mla_w_uq', 'new_m_mla_w_ukv', 'new_m_mla_q_gain', 'new_m_mla_k_gain', 'new_m_mla_w_out', 'new_m_xmem_norm', 'new_m_xmem_mem_norm', 'new_m_xmem_wq', 'new_m_xmem_wkv', 'new_m_xmem_q_gain', 'new_m_xmem_k_gain', 'new_m_xmem_wo', 'new_m_ffn_post_norm', 'new_m_ffn_post_w_gu', 'new_m_ffn_post_w_down', 'new_v_ffn_pre_norm', 'new_v_ffn_pre_w_gu', 'new_v_ffn_pre_w_down', 'new_v_mix_norm', 'new_v_sbg_w_in', 'new_v_sgu_ln_gain', 'new_v_sgu_ln_bias', 'new_v_sgu_w', 'new_v_sgu_b', 'new_v_sbg_w_out', 'new_v_mla_w_in', 'new_v_mla_q_lora_gain', 'new_v_mla_kv_lora_gain', 'new_v_mla_w_uq', 'new_v_mla_w_ukv', 'new_v_mla_q_gain', 'new_v_mla_k_gain', 'new_v_mla_w_out', 'new_v_xmem_norm', 'new_v_xmem_mem_norm', 'new_v_xmem_wq', 'new_v_xmem_wkv', 'new_v_xmem_q_gain', 'new_v_xmem_k_gain', 'new_v_xmem_wo', 'new_v_ffn_post_norm', 'new_v_ffn_post_w_gu', 'new_v_ffn_post_w_down']
TWIN_LEAF_KINDS = {'loss': 'loss', 'grad_x': 'grad_x', 'grad_ffn_pre_norm': 'grad_w', 'grad_ffn_pre_w_gu': 'grad_w', 'grad_ffn_pre_w_down': 'grad_w', 'grad_mix_norm': 'grad_w', 'grad_sbg_w_in': 'grad_w', 'grad_sgu_ln_gain': 'grad_w', 'grad_sgu_ln_bias': 'grad_w', 'grad_sgu_w': 'grad_w', 'grad_sgu_b': 'grad_w', 'grad_sbg_w_out': 'grad_w', 'grad_mla_w_in': 'grad_w', 'grad_mla_q_lora_gain': 'grad_w', 'grad_mla_kv_lora_gain': 'grad_w', 'grad_mla_w_uq': 'grad_w', 'grad_mla_w_ukv': 'grad_w', 'grad_mla_q_gain': 'grad_w', 'grad_mla_k_gain': 'grad_w', 'grad_mla_w_out': 'grad_w', 'grad_xmem_norm': 'grad_w', 'grad_xmem_mem_norm': 'grad_w', 'grad_xmem_wq': 'grad_w', 'grad_xmem_wkv': 'grad_w', 'grad_xmem_q_gain': 'grad_w', 'grad_xmem_k_gain': 'grad_w', 'grad_xmem_wo': 'grad_w', 'grad_ffn_post_norm': 'grad_w', 'grad_ffn_post_w_gu': 'grad_w', 'grad_ffn_post_w_down': 'grad_w', 'delta_ffn_pre_norm': 'delta_w', 'delta_ffn_pre_w_gu': 'delta_w', 'delta_ffn_pre_w_down': 'delta_w', 'delta_mix_norm': 'delta_w', 'delta_sbg_w_in': 'delta_w', 'delta_sgu_ln_gain': 'delta_w', 'delta_sgu_ln_bias': 'delta_w', 'delta_sgu_w': 'delta_w', 'delta_sgu_b': 'delta_w', 'delta_sbg_w_out': 'delta_w', 'delta_mla_w_in': 'delta_w', 'delta_mla_q_lora_gain': 'delta_w', 'delta_mla_kv_lora_gain': 'delta_w', 'delta_mla_w_uq': 'delta_w', 'delta_mla_w_ukv': 'delta_w', 'delta_mla_q_gain': 'delta_w', 'delta_mla_k_gain': 'delta_w', 'delta_mla_w_out': 'delta_w', 'delta_xmem_norm': 'delta_w', 'delta_xmem_mem_norm': 'delta_w', 'delta_xmem_wq': 'delta_w', 'delta_xmem_wkv': 'delta_w', 'delta_xmem_q_gain': 'delta_w', 'delta_xmem_k_gain': 'delta_w', 'delta_xmem_wo': 'delta_w', 'delta_ffn_post_norm': 'delta_w', 'delta_ffn_post_w_gu': 'delta_w', 'delta_ffn_post_w_down': 'delta_w', 'new_m_ffn_pre_norm': 'new_m', 'new_m_ffn_pre_w_gu': 'new_m', 'new_m_ffn_pre_w_down': 'new_m', 'new_m_mix_norm': 'new_m', 'new_m_sbg_w_in': 'new_m', 'new_m_sgu_ln_gain': 'new_m', 'new_m_sgu_ln_bias': 'new_m', 'new_m_sgu_w': 'new_m', 'new_m_sgu_b': 'new_m', 'new_m_sbg_w_out': 'new_m', 'new_m_mla_w_in': 'new_m', 'new_m_mla_q_lora_gain': 'new_m', 'new_m_mla_kv_lora_gain': 'new_m', 'new_m_mla_w_uq': 'new_m', 'new_m_mla_w_ukv': 'new_m', 'new_m_mla_q_gain': 'new_m', 'new_m_mla_k_gain': 'new_m', 'new_m_mla_w_out': 'new_m', 'new_m_xmem_norm': 'new_m', 'new_m_xmem_mem_norm': 'new_m', 'new_m_xmem_wq': 'new_m', 'new_m_xmem_wkv': 'new_m', 'new_m_xmem_q_gain': 'new_m', 'new_m_xmem_k_gain': 'new_m', 'new_m_xmem_wo': 'new_m', 'new_m_ffn_post_norm': 'new_m', 'new_m_ffn_post_w_gu': 'new_m', 'new_m_ffn_post_w_down': 'new_m', 'new_v_ffn_pre_norm': 'new_v', 'new_v_ffn_pre_w_gu': 'new_v', 'new_v_ffn_pre_w_down': 'new_v', 'new_v_mix_norm': 'new_v', 'new_v_sbg_w_in': 'new_v', 'new_v_sgu_ln_gain': 'new_v', 'new_v_sgu_ln_bias': 'new_v', 'new_v_sgu_w': 'new_v', 'new_v_sgu_b': 'new_v', 'new_v_sbg_w_out': 'new_v', 'new_v_mla_w_in': 'new_v', 'new_v_mla_q_lora_gain': 'new_v', 'new_v_mla_kv_lora_gain': 'new_v', 'new_v_mla_w_uq': 'new_v', 'new_v_mla_w_ukv': 'new_v', 'new_v_mla_q_gain': 'new_v', 'new_v_mla_k_gain': 'new_v', 'new_v_mla_w_out': 'new_v', 'new_v_xmem_norm': 'new_v', 'new_v_xmem_mem_norm': 'new_v', 'new_v_xmem_wq': 'new_v', 'new_v_xmem_wkv': 'new_v', 'new_v_xmem_q_gain': 'new_v', 'new_v_xmem_k_gain': 'new_v', 'new_v_xmem_wo': 'new_v', 'new_v_ffn_post_norm': 'new_v', 'new_v_ffn_post_w_gu': 'new_v', 'new_v_ffn_post_w_down': 'new_v'}


def _forward(args):
    return _fwd_reference(*[args[k] for k in FWD_PARAMS])


def _output_shape():
    def fwd():
        inp = _fwd_setup_inputs(0)
        return _fwd_reference(*[inp[k] for k in FWD_PARAMS])
    out = _jax.eval_shape(fwd)
    return out.shape, out.dtype

N_MICROBATCH = 1
ADAM_LR = 0.001
ADAM_B1 = 0.9
ADAM_B2 = 0.999
ADAM_EPS = 1e-08
ADAM_WD = 0.01
ADAM_STEP = 10
PER_EXAMPLE_BATCH_AXIS = {'x': 0, 'mem': 0, 'positions': 0, 'loss_target': 0}
SHARED_INPUTS = []
_WEIGHT_DTYPES = {'ffn_pre_norm': _jnp.float32, 'ffn_pre_w_gu': _jnp.float32, 'ffn_pre_w_down': _jnp.float32, 'mix_norm': _jnp.float32, 'sbg_w_in': _jnp.float32, 'sgu_ln_gain': _jnp.float32, 'sgu_ln_bias': _jnp.float32, 'sgu_w': _jnp.float32, 'sgu_b': _jnp.float32, 'sbg_w_out': _jnp.float32, 'mla_w_in': _jnp.float32, 'mla_q_lora_gain': _jnp.float32, 'mla_kv_lora_gain': _jnp.float32, 'mla_w_uq': _jnp.float32, 'mla_w_ukv': _jnp.float32, 'mla_q_gain': _jnp.float32, 'mla_k_gain': _jnp.float32, 'mla_w_out': _jnp.float32, 'xmem_norm': _jnp.float32, 'xmem_mem_norm': _jnp.float32, 'xmem_wq': _jnp.float32, 'xmem_wkv': _jnp.float32, 'xmem_q_gain': _jnp.float32, 'xmem_k_gain': _jnp.float32, 'xmem_wo': _jnp.float32, 'ffn_post_norm': _jnp.float32, 'ffn_post_w_gu': _jnp.float32, 'ffn_post_w_down': _jnp.float32}
MOMENT_SCALE = {'ffn_pre_norm': 1.221802e+01, 'ffn_pre_w_gu': 2.136957e-01, 'ffn_pre_w_down': 3.752222e-01, 'mix_norm': 2.972993e+01, 'sbg_w_in': 7.129344e-01, 'sgu_ln_gain': 1.340296e+01, 'sgu_ln_bias': 8.103737e-01, 'sgu_w': 6.052611e-01, 'sgu_b': 1.444409e+01, 'sbg_w_out': 5.519516e+00, 'mla_w_in': 3.736894e+00, 'mla_q_lora_gain': 1.522262e-01, 'mla_kv_lora_gain': 6.890709e+00, 'mla_w_uq': 8.274505e-02, 'mla_w_ukv': 1.591969e+00, 'mla_q_gain': 3.176301e+00, 'mla_k_gain': 3.174000e+00, 'mla_w_out': 1.974499e+00, 'xmem_norm': 1.289197e-01, 'xmem_mem_norm': 9.428769e-01, 'xmem_wq': 1.263677e-01, 'xmem_wkv': 3.565372e-01, 'xmem_q_gain': 2.578955e+00, 'xmem_k_gain': 2.571577e+00, 'xmem_wo': 4.936548e-01, 'ffn_post_norm': 1.239407e+01, 'ffn_post_w_gu': 2.483761e-01, 'ffn_post_w_down': 4.104378e-01}


def _to_microbatches(a, axis):
    t = _jnp.moveaxis(a, axis, 0)
    t = t.reshape((N_MICROBATCH, t.shape[0] // N_MICROBATCH) + t.shape[1:])
    return _jnp.moveaxis(t, 1, axis + 1)


def setup_inputs(seed: int = 0) -> dict:
    inp = _fwd_setup_inputs(seed)
    key = _jax.random.fold_in(_jax.random.key(seed), 7919)
    shape, _ = _output_shape()
    out = dict(inp)
    out["loss_target"] = _jax.random.normal(_jax.random.fold_in(key, 0), shape, _jnp.float32)
    for i, name in enumerate(TWIN_WEIGHTS):
        w = inp[name].astype(_jnp.float32)
        if MOMENT_SCALE is None:
            s = _jnp.sqrt(_jnp.mean(_jnp.square(w)) + 1e-30)
        else:
            s = MOMENT_SCALE[name]
        km, kv = _jax.random.split(_jax.random.fold_in(key, i + 1))
        out[name] = w
        out["m_" + name] = s * _jax.random.normal(km, w.shape, _jnp.float32)
        out["v_" + name] = (s * s) * _jax.random.uniform(kv, w.shape, _jnp.float32, 0.5, 1.5)
    if N_MICROBATCH > 1:
        for name, axis in PER_EXAMPLE_BATCH_AXIS.items():
            out[name] = _to_microbatches(out[name], axis)
    return {'x': out['x'], 'mem': out['mem'], 'positions': out['positions'], 'ffn_pre_norm': out['ffn_pre_norm'], 'ffn_pre_w_gu': out['ffn_pre_w_gu'], 'ffn_pre_w_down': out['ffn_pre_w_down'], 'mix_norm': out['mix_norm'], 'sbg_w_in': out['sbg_w_in'], 'sgu_ln_gain': out['sgu_ln_gain'], 'sgu_ln_bias': out['sgu_ln_bias'], 'sgu_w': out['sgu_w'], 'sgu_b': out['sgu_b'], 'sbg_w_out': out['sbg_w_out'], 'mla_w_in': out['mla_w_in'], 'mla_q_lora_gain': out['mla_q_lora_gain'], 'mla_kv_lora_gain': out['mla_kv_lora_gain'], 'mla_w_uq': out['mla_w_uq'], 'mla_w_ukv': out['mla_w_ukv'], 'mla_q_gain': out['mla_q_gain'], 'mla_k_gain': out['mla_k_gain'], 'mla_w_out': out['mla_w_out'], 'xmem_norm': out['xmem_norm'], 'xmem_mem_norm': out['xmem_mem_norm'], 'xmem_wq': out['xmem_wq'], 'xmem_wkv': out['xmem_wkv'], 'xmem_q_gain': out['xmem_q_gain'], 'xmem_k_gain': out['xmem_k_gain'], 'xmem_wo': out['xmem_wo'], 'ffn_post_norm': out['ffn_post_norm'], 'ffn_post_w_gu': out['ffn_post_w_gu'], 'ffn_post_w_down': out['ffn_post_w_down'], 'loss_target': out['loss_target'], 'm_ffn_pre_norm': out['m_ffn_pre_norm'], 'm_ffn_pre_w_gu': out['m_ffn_pre_w_gu'], 'm_ffn_pre_w_down': out['m_ffn_pre_w_down'], 'm_mix_norm': out['m_mix_norm'], 'm_sbg_w_in': out['m_sbg_w_in'], 'm_sgu_ln_gain': out['m_sgu_ln_gain'], 'm_sgu_ln_bias': out['m_sgu_ln_bias'], 'm_sgu_w': out['m_sgu_w'], 'm_sgu_b': out['m_sgu_b'], 'm_sbg_w_out': out['m_sbg_w_out'], 'm_mla_w_in': out['m_mla_w_in'], 'm_mla_q_lora_gain': out['m_mla_q_lora_gain'], 'm_mla_kv_lora_gain': out['m_mla_kv_lora_gain'], 'm_mla_w_uq': out['m_mla_w_uq'], 'm_mla_w_ukv': out['m_mla_w_ukv'], 'm_mla_q_gain': out['m_mla_q_gain'], 'm_mla_k_gain': out['m_mla_k_gain'], 'm_mla_w_out': out['m_mla_w_out'], 'm_xmem_norm': out['m_xmem_norm'], 'm_xmem_mem_norm': out['m_xmem_mem_norm'], 'm_xmem_wq': out['m_xmem_wq'], 'm_xmem_wkv': out['m_xmem_wkv'], 'm_xmem_q_gain': out['m_xmem_q_gain'], 'm_xmem_k_gain': out['m_xmem_k_gain'], 'm_xmem_wo': out['m_xmem_wo'], 'm_ffn_post_norm': out['m_ffn_post_norm'], 'm_ffn_post_w_gu': out['m_ffn_post_w_gu'], 'm_ffn_post_w_down': out['m_ffn_post_w_down'], 'v_ffn_pre_norm': out['v_ffn_pre_norm'], 'v_ffn_pre_w_gu': out['v_ffn_pre_w_gu'], 'v_ffn_pre_w_down': out['v_ffn_pre_w_down'], 'v_mix_norm': out['v_mix_norm'], 'v_sbg_w_in': out['v_sbg_w_in'], 'v_sgu_ln_gain': out['v_sgu_ln_gain'], 'v_sgu_ln_bias': out['v_sgu_ln_bias'], 'v_sgu_w': out['v_sgu_w'], 'v_sgu_b': out['v_sgu_b'], 'v_sbg_w_out': out['v_sbg_w_out'], 'v_mla_w_in': out['v_mla_w_in'], 'v_mla_q_lora_gain': out['v_mla_q_lora_gain'], 'v_mla_kv_lora_gain': out['v_mla_kv_lora_gain'], 'v_mla_w_uq': out['v_mla_w_uq'], 'v_mla_w_ukv': out['v_mla_w_ukv'], 'v_mla_q_gain': out['v_mla_q_gain'], 'v_mla_k_gain': out['v_mla_k_gain'], 'v_mla_w_out': out['v_mla_w_out'], 'v_xmem_norm': out['v_xmem_norm'], 'v_xmem_mem_norm': out['v_xmem_mem_norm'], 'v_xmem_wq': out['v_xmem_wq'], 'v_xmem_wkv': out['v_xmem_wkv'], 'v_xmem_q_gain': out['v_xmem_q_gain'], 'v_xmem_k_gain': out['v_xmem_k_gain'], 'v_xmem_wo': out['v_xmem_wo'], 'v_ffn_post_norm': out['v_ffn_post_norm'], 'v_ffn_post_w_gu': out['v_ffn_post_w_gu'], 'v_ffn_post_w_down': out['v_ffn_post_w_down']}


def _loss(weights, diff, rest, loss_target):
    with _jax.named_scope("forward"):
        args = {**rest, TWIN_DIFF_INPUT: diff, **{k: w.astype(_WEIGHT_DTYPES[k]) for k, w in weights.items()}}
        y = _forward(args)
    with _jax.named_scope("loss_head"):
        err = _jnp.square(y.astype(_jnp.float32) - loss_target)
        return 0.5 * _jnp.sum(_jnp.mean(err, axis=-1)) if err.ndim else 0.5 * err


def _adamw(w, g, m, v):
    m = ADAM_B1 * m + (1.0 - ADAM_B1) * g
    v = ADAM_B2 * v + (1.0 - ADAM_B2) * _jnp.square(g)
    m_hat = m / (1.0 - ADAM_B1 ** ADAM_STEP)
    v_hat = v / (1.0 - ADAM_B2 ** ADAM_STEP)
    delta = -ADAM_LR * (m_hat / (_jnp.sqrt(v_hat) + ADAM_EPS) + ADAM_WD * w)
    return delta, m, v


def reference(x, mem, positions, ffn_pre_norm, ffn_pre_w_gu, ffn_pre_w_down, mix_norm, sbg_w_in, sgu_ln_gain, sgu_ln_bias, sgu_w, sgu_b, sbg_w_out, mla_w_in, mla_q_lora_gain, mla_kv_lora_gain, mla_w_uq, mla_w_ukv, mla_q_gain, mla_k_gain, mla_w_out, xmem_norm, xmem_mem_norm, xmem_wq, xmem_wkv, xmem_q_gain, xmem_k_gain, xmem_wo, ffn_post_norm, ffn_post_w_gu, ffn_post_w_down, loss_target, m_ffn_pre_norm, m_ffn_pre_w_gu, m_ffn_pre_w_down, m_mix_norm, m_sbg_w_in, m_sgu_ln_gain, m_sgu_ln_bias, m_sgu_w, m_sgu_b, m_sbg_w_out, m_mla_w_in, m_mla_q_lora_gain, m_mla_kv_lora_gain, m_mla_w_uq, m_mla_w_ukv, m_mla_q_gain, m_mla_k_gain, m_mla_w_out, m_xmem_norm, m_xmem_mem_norm, m_xmem_wq, m_xmem_wkv, m_xmem_q_gain, m_xmem_k_gain, m_xmem_wo, m_ffn_post_norm, m_ffn_post_w_gu, m_ffn_post_w_down, v_ffn_pre_norm, v_ffn_pre_w_gu, v_ffn_pre_w_down, v_mix_norm, v_sbg_w_in, v_sgu_ln_gain, v_sgu_ln_bias, v_sgu_w, v_sgu_b, v_sbg_w_out, v_mla_w_in, v_mla_q_lora_gain, v_mla_kv_lora_gain, v_mla_w_uq, v_mla_w_ukv, v_mla_q_gain, v_mla_k_gain, v_mla_w_out, v_xmem_norm, v_xmem_mem_norm, v_xmem_wq, v_xmem_wkv, v_xmem_q_gain, v_xmem_k_gain, v_xmem_wo, v_ffn_post_norm, v_ffn_post_w_gu, v_ffn_post_w_down):
    given = dict(x=x, mem=mem, positions=positions, ffn_pre_norm=ffn_pre_norm, ffn_pre_w_gu=ffn_pre_w_gu, ffn_pre_w_down=ffn_pre_w_down, mix_norm=mix_norm, sbg_w_in=sbg_w_in, sgu_ln_gain=sgu_ln_gain, sgu_ln_bias=sgu_ln_bias, sgu_w=sgu_w, sgu_b=sgu_b, sbg_w_out=sbg_w_out, mla_w_in=mla_w_in, mla_q_lora_gain=mla_q_lora_gain, mla_kv_lora_gain=mla_kv_lora_gain, mla_w_uq=mla_w_uq, mla_w_ukv=mla_w_ukv, mla_q_gain=mla_q_gain, mla_k_gain=mla_k_gain, mla_w_out=mla_w_out, xmem_norm=xmem_norm, xmem_mem_norm=xmem_mem_norm, xmem_wq=xmem_wq, xmem_wkv=xmem_wkv, xmem_q_gain=xmem_q_gain, xmem_k_gain=xmem_k_gain, xmem_wo=xmem_wo, ffn_post_norm=ffn_post_norm, ffn_post_w_gu=ffn_post_w_gu, ffn_post_w_down=ffn_post_w_down, loss_target=loss_target, m_ffn_pre_norm=m_ffn_pre_norm, m_ffn_pre_w_gu=m_ffn_pre_w_gu, m_ffn_pre_w_down=m_ffn_pre_w_down, m_mix_norm=m_mix_norm, m_sbg_w_in=m_sbg_w_in, m_sgu_ln_gain=m_sgu_ln_gain, m_sgu_ln_bias=m_sgu_ln_bias, m_sgu_w=m_sgu_w, m_sgu_b=m_sgu_b, m_sbg_w_out=m_sbg_w_out, m_mla_w_in=m_mla_w_in, m_mla_q_lora_gain=m_mla_q_lora_gain, m_mla_kv_lora_gain=m_mla_kv_lora_gain, m_mla_w_uq=m_mla_w_uq, m_mla_w_ukv=m_mla_w_ukv, m_mla_q_gain=m_mla_q_gain, m_mla_k_gain=m_mla_k_gain, m_mla_w_out=m_mla_w_out, m_xmem_norm=m_xmem_norm, m_xmem_mem_norm=m_xmem_mem_norm, m_xmem_wq=m_xmem_wq, m_xmem_wkv=m_xmem_wkv, m_xmem_q_gain=m_xmem_q_gain, m_xmem_k_gain=m_xmem_k_gain, m_xmem_wo=m_xmem_wo, m_ffn_post_norm=m_ffn_post_norm, m_ffn_post_w_gu=m_ffn_post_w_gu, m_ffn_post_w_down=m_ffn_post_w_down, v_ffn_pre_norm=v_ffn_pre_norm, v_ffn_pre_w_gu=v_ffn_pre_w_gu, v_ffn_pre_w_down=v_ffn_pre_w_down, v_mix_norm=v_mix_norm, v_sbg_w_in=v_sbg_w_in, v_sgu_ln_gain=v_sgu_ln_gain, v_sgu_ln_bias=v_sgu_ln_bias, v_sgu_w=v_sgu_w, v_sgu_b=v_sgu_b, v_sbg_w_out=v_sbg_w_out, v_mla_w_in=v_mla_w_in, v_mla_q_lora_gain=v_mla_q_lora_gain, v_mla_kv_lora_gain=v_mla_kv_lora_gain, v_mla_w_uq=v_mla_w_uq, v_mla_w_ukv=v_mla_w_ukv, v_mla_q_gain=v_mla_q_gain, v_mla_k_gain=v_mla_k_gain, v_mla_w_out=v_mla_w_out, v_xmem_norm=v_xmem_norm, v_xmem_mem_norm=v_xmem_mem_norm, v_xmem_wq=v_xmem_wq, v_xmem_wkv=v_xmem_wkv, v_xmem_q_gain=v_xmem_q_gain, v_xmem_k_gain=v_xmem_k_gain, v_xmem_wo=v_xmem_wo, v_ffn_post_norm=v_ffn_post_norm, v_ffn_post_w_gu=v_ffn_post_w_gu, v_ffn_post_w_down=v_ffn_post_w_down)
    weights = {n: given[n] for n in TWIN_WEIGHTS}
    shared = {n: given[n] for n in SHARED_INPUTS}
    per_example = {n: given[n] for n in ['x', 'mem', 'positions']}
    grad_fn = _jax.value_and_grad(_loss, argnums=(0, 1))

    def one_microbatch(ex, loss_target):
        ex = dict(ex)
        diff = ex.pop(TWIN_DIFF_INPUT)
        return grad_fn(weights, diff, {**shared, **ex}, loss_target)

    if N_MICROBATCH == 1:
        loss, (grad_w, grad_x) = one_microbatch(per_example, given["loss_target"])
    else:
        def body(carry, xs):
            loss_sum, grad_sum = carry
            l_k, (gw_k, gx_k) = one_microbatch(xs[0], xs[1])
            with _jax.named_scope("update"):
                return (loss_sum + l_k, _jax.tree.map(_jnp.add, grad_sum, gw_k)), gx_k

        init = (_jnp.zeros((), _jnp.float32), _jax.tree.map(_jnp.zeros_like, weights))
        (loss, grad_w), grad_x = _jax.lax.scan(body, init, (per_example, given["loss_target"]))
    with _jax.named_scope("update"):
        delta_w, new_m, new_v = {}, {}, {}
        for n in TWIN_WEIGHTS:
            delta_w[n], new_m[n], new_v[n] = _adamw(weights[n], grad_w[n], given["m_" + n], given["v_" + n])
    return (loss, grad_x, *[grad_w[n] for n in TWIN_WEIGHTS], *[delta_w[n] for n in TWIN_WEIGHTS],
            *[new_m[n] for n in TWIN_WEIGHTS], *[new_v[n] for n in TWIN_WEIGHTS])
```

```python
import functools

import jax
import jax.numpy as jnp
from jax import lax
from jax.experimental import pallas as pl
from jax.experimental.pallas import tpu as pltpu

F32, BF16 = jnp.float32, jnp.bfloat16
LANE = 128
VMEM_LIMIT = 56 * 1024 * 1024
EPS = 1e-6
D_FF = 2816
SB_HEADS, SB_HD = 8, 64
SG_GROUPS, SG_GD, SG_CHUNK = 8, 64, 128
SB_W, SG_W = SB_HEADS * SB_HD, SG_GROUPS * SG_GD
MLA_HEADS, MLA_NOPE, MLA_ROPE, MLA_V = 16, 64, 32, 64
MLA_QK = MLA_NOPE + MLA_ROPE
MLA_QL, MLA_KVL = 512, 256
ROPE_THETA = 10000.0
MEM_HEADS, MEM_HD = 4, 256
ADAM_LR, ADAM_B1, ADAM_B2, ADAM_EPS, ADAM_WD, ADAM_STEP = 0.001, 0.9, 0.999, 1e-08, 0.01, 10
MESH = pl.DeviceIdType.MESH
ANY = pl.BlockSpec(memory_space=pl.ANY)


def _params(n_axes):
    return pltpu.CompilerParams(dimension_semantics=("arbitrary",) * n_axes, vmem_limit_bytes=VMEM_LIMIT)


def _tile(dim, cap):
    if dim <= cap:
        return dim
    best = max(t for t in range(LANE, cap + 1, LANE) if dim % t == 0)
    return best


def _mm(a, b, *, ta=False, tb=False, out_dtype=F32, scale=1.0, residual=None, a_off=(0, 0), b_off=(0, 0),
        m=None, n=None, k=None, name):
    am, ak = (a.shape[1], a.shape[0]) if ta else a.shape
    bk, bn = (b.shape[1], b.shape[0]) if tb else b.shape
    M, N, K = m or am, n or bn, k or ak
    tm, tn, tk = _tile(M, 1024), _tile(N, 1408), _tile(K, 512)
    nm, nn, nk = M // tm, N // tn, K // tk
    dims = (((0 if ta else 1,), (1 if tb else 0,)), ((), ()))

    def body(*refs):
        a_ref, b_ref = refs[0], refs[1]
        o_ref, acc_ref = refs[-2], refs[-1]
        kk = pl.program_id(2)

        @pl.when(kk == 0)
        def _():
            acc_ref[...] = jnp.zeros_like(acc_ref)

        acc_ref[...] += lax.dot_general(a_ref[...].astype(BF16), b_ref[...].astype(BF16), dims,
                                        preferred_element_type=F32)

        @pl.when(kk == nk - 1)
        def _():
            out = acc_ref[...] * scale
            if residual is not None:
                out = out + refs[2][...].astype(F32)
            o_ref[...] = out.astype(o_ref.dtype)

    (ao0, ao1), (bo0, bo1) = a_off, b_off
    a_spec = (pl.BlockSpec((tk, tm), lambda i, j, kk: (kk + ao0, i + ao1)) if ta
              else pl.BlockSpec((tm, tk), lambda i, j, kk: (i + ao0, kk + ao1)))
    b_spec = (pl.BlockSpec((tn, tk), lambda i, j, kk: (j + bo0, kk + bo1)) if tb
              else pl.BlockSpec((tk, tn), lambda i, j, kk: (kk + bo0, j + bo1)))
    o_spec = pl.BlockSpec((tm, tn), lambda i, j, kk: (i, j))
    ins, in_specs = [a, b], [a_spec, b_spec]
    if residual is not None:
        ins.append(residual)
        in_specs.append(o_spec)
    return pl.pallas_call(
        body, name=name, grid=(nm, nn, nk), in_specs=in_specs, out_specs=o_spec,
        out_shape=jax.ShapeDtypeStruct((M, N), out_dtype),
        scratch_shapes=[pltpu.VMEM((tm, tn), F32)], compiler_params=_params(3))(*ins)


def _mm_swiglu(h, wgu, *, name):
    M, K = h.shape
    F = wgu.shape[1] // 2
    tm, tn, tk = _tile(M, 1024), _tile(F, 1408), _tile(K, 512)
    nm, nf, nk = M // tm, F // tn, K // tk

    def body(h_ref, wg_ref, wu_ref, g_ref, u_ref, a_ref, accg, accu):
        kk = pl.program_id(2)

        @pl.when(kk == 0)
        def _():
            accg[...] = jnp.zeros_like(accg)
            accu[...] = jnp.zeros_like(accu)

        hb = h_ref[...]
        accg[...] += jnp.dot(hb, wg_ref[...], preferred_element_type=F32)
        accu[...] += jnp.dot(hb, wu_ref[...], preferred_element_type=F32)

        @pl.when(kk == nk - 1)
        def _():
            g, u = accg[...], accu[...]
            g_ref[...] = g.astype(BF16)
            u_ref[...] = u.astype(BF16)
            a_ref[...] = (g * jax.nn.sigmoid(g) * u).astype(BF16)

    o_spec = pl.BlockSpec((tm, tn), lambda i, j, kk: (i, j))
    shp = jax.ShapeDtypeStruct((M, F), BF16)
    return pl.pallas_call(
        body, name=name, grid=(nm, nf, nk),
        in_specs=[pl.BlockSpec((tm, tk), lambda i, j, kk: (i, kk)),
                  pl.BlockSpec((tk, tn), lambda i, j, kk: (kk, j)),
                  pl.BlockSpec((tk, tn), lambda i, j, kk: (kk, j + nf))],
        out_specs=[o_spec, o_spec, o_spec], out_shape=[shp, shp, shp],
        scratch_shapes=[pltpu.VMEM((tm, tn), F32), pltpu.VMEM((tm, tn), F32)],
        compiler_params=_params(3))(h, wgu, wgu)


def _mm_dswiglu(dy, wd, gate, up, *, scale, name):
    M, K = dy.shape
    F = wd.shape[0]
    tm, tn, tk = _tile(M, 1024), _tile(F, 1408), _tile(K, 512)
    nm, nf, nk = M // tm, F // tn, K // tk

    def body(dy_ref, wd_ref, g_ref, u_ref, dg_ref, du_ref, acc):
        kk = pl.program_id(2)

        @pl.when(kk == 0)
        def _():
            acc[...] = jnp.zeros_like(acc)

        acc[...] += lax.dot_general(dy_ref[...].astype(BF16), wd_ref[...], (((1,), (1,)), ((), ())),
                                    preferred_element_type=F32)

        @pl.when(kk == nk - 1)
        def _():
            da = acc[...] * scale
            g, u = g_ref[...].astype(F32), u_ref[...].astype(F32)
            sg = jax.nn.sigmoid(g)
            du_ref[...] = (da * g * sg).astype(BF16)
            dg_ref[...] = (da * u * sg * (1.0 + g * (1.0 - sg))).astype(BF16)

    o_spec = pl.BlockSpec((tm, tn), lambda i, j, kk: (i, j))
    shp = jax.ShapeDtypeStruct((M, F), BF16)
    return pl.pallas_call(
        body, name=name, grid=(nm, nf, nk),
        in_specs=[pl.BlockSpec((tm, tk), lambda i, j, kk: (i, kk)),
                  pl.BlockSpec((tn, tk), lambda i, j, kk: (j, kk)), o_spec, o_spec],
        out_specs=[o_spec, o_spec], out_shape=[shp, shp],
        scratch_shapes=[pltpu.VMEM((tm, tn), F32)], compiler_params=_params(3))(dy, wd, gate, up)


def _row_tile(rows, cap):
    t = cap
    while t >= 8:
        if rows % t == 0:
            return t
        t //= 2
    return rows


def _rowwise(fn, rows, consts, outs, sums=(), hsums=(), *, heads=None, tm=256, name):
    rows = [r if isinstance(r, tuple) else (r, r.shape[1], None) for r in rows]
    S = rows[0][0].shape[0]
    tm = _row_tile(S, tm)
    nh = heads or 1
    n_r, n_c, n_o, n_h, n_s = len(rows), len(consts), len(outs), len(hsums), len(sums)

    def body(*refs):
        r = [x[...] for x in refs[:n_r]]
        c = [x[...] for x in refs[n_r:n_r + n_c]]
        o_refs = refs[n_r + n_c:n_r + n_c + n_o]
        h_refs = refs[n_r + n_c + n_o:n_r + n_c + n_o + n_h]
        s_refs = refs[n_r + n_c + n_o + n_h:]
        res = fn(*r, *c)
        res = res if isinstance(res, (tuple, list)) else (res,)
        for ref, val in zip(o_refs, res[:n_o]):
            ref[...] = val.astype(ref.dtype)
        if n_h:
            @pl.when(pl.program_id(1) == 0)
            def _():
                for ref in h_refs:
                    ref[...] = jnp.zeros_like(ref)
            for ref, val in zip(h_refs, res[n_o:n_o + n_h]):
                ref[...] += val
        if n_s:
            @pl.when((pl.program_id(0) == 0) & (pl.program_id(1) == 0))
            def _():
                for ref in s_refs:
                    ref[...] = jnp.zeros_like(ref)
            for ref, val in zip(s_refs, res[n_o + n_h:]):
                ref[...] += val

    def col(colfn):
        return (lambda i, h: (i, 0)) if colfn is None else (lambda i, h: (i, colfn(h)))

    in_specs = [pl.BlockSpec((tm, w), col(cf)) for _, w, cf in rows]
    in_specs += [pl.BlockSpec(a.shape, lambda i, h, nd=a.ndim: (0,) * nd) for a in consts]
    out_specs = [pl.BlockSpec((tm, w // nh), (lambda i, h: (i, h)) if heads else (lambda i, h: (i, 0))) for w, _ in outs]
    out_specs += [pl.BlockSpec((tm, w), lambda i, h: (i, 0)) for w in hsums]
    out_specs += [pl.BlockSpec(sh, lambda i, h, nd=len(sh): (0,) * nd) for sh in sums]
    out_shape = [jax.ShapeDtypeStruct((S, w), dt) for w, dt in outs]
    out_shape += [jax.ShapeDtypeStruct((S, w), F32) for w in hsums]
    out_shape += [jax.ShapeDtypeStruct(sh, F32) for sh in sums]
    return pl.pallas_call(body, name=name, grid=(S // tm, nh), in_specs=in_specs, out_specs=out_specs,
                          out_shape=out_shape, compiler_params=_params(2))(*[a for a, _, _ in rows], *consts)


def _rms(x, width=None):
    width = width or x.shape[-1]
    return lax.rsqrt(jnp.sum(x * x, axis=-1, keepdims=True) * (1.0 / width) + EPS)


def _rmsnorm_fwd(x, g, width=None):
    return x * _rms(x, width) * g


def _rmsnorm_bwd(dy, x, g, width=None):
    width = width or x.shape[-1]
    r = _rms(x, width)
    xn = x * r
    dxn = dy * g
    dx = r * (dxn - xn * (jnp.sum(dxn * xn, axis=-1, keepdims=True) * (1.0 / width)))
    return dx, jnp.sum(dy * xn, axis=0, keepdims=True)


def _norm_rows(x, g, *, name, out_dtype=BF16):
    D = x.shape[1]
    return _rowwise(lambda xv, gv: _rmsnorm_fwd(xv.astype(F32), gv), [x], [g.reshape(1, D)], [(D, out_dtype)],
                    name=name)[0]


def _norm_rows_bwd(dh, x, g, dres, *, name):
    D = x.shape[1]

    def fn(dhv, xv, *rest):
        dx, dg = _rmsnorm_bwd(dhv.astype(F32), xv, rest[-1])
        return (dx + rest[0] if dres is not None else dx), dg

    rows = [dh, x] + ([dres] if dres is not None else [])
    return _rowwise(fn, rows, [g.reshape(1, D)], [(D, F32)], [(1, D)], name=name)


def _softplus(z):
    return jnp.maximum(z, 0.0) + jnp.log(1.0 + jnp.exp(-jnp.abs(z)))


def _running_sum(v, u):
    hi = v.astype(BF16)
    lo = (v - hi.astype(F32)).astype(BF16)
    return jnp.dot(hi, u, preferred_element_type=F32) + jnp.dot(lo, u, preferred_element_type=F32)


def _triangle(tk, inclusive_prefix):
    j, s = lax.broadcasted_iota(jnp.int32, (tk, tk), 0), lax.broadcasted_iota(jnp.int32, (tk, tk), 1)
    return ((j <= s) if inclusive_prefix else (j > s)).astype(BF16)


def _nt(a, b):
    return lax.dot_general(a, b, (((1,), (1,)), ((), ())), preferred_element_type=F32)


def _tn(a, b):
    return lax.dot_general(a, b, (((0,), (0,)), ((), ())), preferred_element_type=F32)


def _attn_fwd(q, k, v, *, sb, causal, heads, dq, dv, scale, kcol=None, vcol=None, name):
    S, Sk = q.shape[0], k.shape[0]
    tq, tk = min(256, S), min(256, Sk)
    nq = S // tq
    kcol = kcol or (lambda h: h)
    vcol = vcol or (lambda h: h)

    def body(*refs):
        if sb:
            q_ref, k_ref, v_ref, u_ref, o_ref, lse_ref, acc_ref, r_ref = refs
        else:
            q_ref, k_ref, v_ref, o_ref, lse_ref, acc_ref, m_ref, l_ref = refs
            m_ref[...] = jnp.full_like(m_ref, -1e30)
            l_ref[...] = jnp.zeros_like(l_ref)
        i = pl.program_id(1)
        qb = q_ref[...]
        acc_ref[...] = jnp.zeros_like(acc_ref)
        if sb:
            r_ref[...] = jnp.zeros_like(r_ref)
        nblk = ((i + 1) * tq + tk - 1) // tk if causal else Sk // tk

        def step(t, carry):
            off = pl.multiple_of((nblk - 1 - t) * tk, tk)
            kb, vb = k_ref[pl.ds(off, tk), :], v_ref[pl.ds(off, tk), :]
            s = _nt(qb, kb) * scale
            if causal:
                qpos = i * tq + lax.broadcasted_iota(jnp.int32, (tq, tk), 0)
                kpos = off + lax.broadcasted_iota(jnp.int32, (tq, tk), 1)
                valid = (kpos < qpos) if sb else (kpos <= qpos)
            if sb:
                sp = _softplus(s)
                ls = jnp.where(valid, -sp, 0.0)
                rest = r_ref[...] + _running_sum(ls, u_ref[...])
                w = jnp.where(valid, jnp.exp(s - sp + rest), 0.0)
                acc_ref[...] += jnp.dot(w.astype(BF16), vb, preferred_element_type=F32)
                r_ref[...] += jnp.sum(ls, axis=1, keepdims=True)
            else:
                if causal:
                    s = jnp.where(valid, s, -1e30)
                m_old = m_ref[...]
                m_new = jnp.maximum(m_old, jnp.max(s, axis=1, keepdims=True))
                p = jnp.exp(s - m_new)
                alpha = jnp.exp(m_old - m_new)
                l_ref[...] = alpha * l_ref[...] + jnp.sum(p, axis=1, keepdims=True)
                acc_ref[...] = alpha * acc_ref[...] + jnp.dot(p.astype(BF16), vb, preferred_element_type=F32)
                m_ref[...] = m_new
            return carry

        lax.fori_loop(0, nblk, step, 0)
        if sb:
            o_ref[...] = acc_ref[...]
            lse_ref[0] = r_ref[...]
        else:
            o_ref[...] = acc_ref[...] / l_ref[...]
            lse_ref[0] = m_ref[...] + jnp.log(l_ref[...])

    in_specs = [pl.BlockSpec((tq, dq), lambda h, i: (i, h)),
                pl.BlockSpec((Sk, dq), lambda h, i: (0, kcol(h))),
                pl.BlockSpec((Sk, dv), lambda h, i: (0, vcol(h)))]
    ins = [q, k, v]
    scratch = [pltpu.VMEM((tq, dv), F32), pltpu.VMEM((tq, 1), F32)]
    if sb:
        ins.append(_triangle(tk, inclusive_prefix=False))
        in_specs.append(pl.BlockSpec((tk, tk), lambda h, i: (0, 0)))
    else:
        scratch.append(pltpu.VMEM((tq, 1), F32))
    out_specs = [pl.BlockSpec((tq, dv), lambda h, i: (i, h)), pl.BlockSpec((1, tq, 1), lambda h, i: (h, i, 0))]
    out_shape = [jax.ShapeDtypeStruct((S, heads * dv), F32), jax.ShapeDtypeStruct((heads, S, 1), F32)]
    return pl.pallas_call(body, name=name, grid=(heads, nq), in_specs=in_specs, out_specs=out_specs,
                          out_shape=out_shape, scratch_shapes=scratch, compiler_params=_params(2))(*ins)


def _attn_bwd(q, k, v, o, do, lse, *, sb, causal, heads, dq, dv, scale, kcol=None, vcol=None, name):
    S, Sk = q.shape[0], k.shape[0]
    tq, tk = min(256, S), min(256, Sk)
    nq = S // tq
    kcol = kcol or (lambda h: h)
    vcol = vcol or (lambda h: h)

    def body(*refs):
        if sb:
            q_ref, k_ref, v_ref, o_ref, do_ref, lse_ref, u_ref, dq_ref, dk_ref, dv_ref, acc_ref, r_ref, re_ref = refs
            r_ref[...] = jnp.zeros_like(r_ref)
            re_ref[...] = jnp.zeros_like(re_ref)
        else:
            q_ref, k_ref, v_ref, o_ref, do_ref, lse_ref, dq_ref, dk_ref, dv_ref, acc_ref = refs
        i = pl.program_id(1)

        @pl.when(i == 0)
        def _():
            dk_ref[...] = jnp.zeros_like(dk_ref)
            dv_ref[...] = jnp.zeros_like(dv_ref)

        qb = q_ref[...]
        dof = do_ref[...].astype(F32)
        dob = dof.astype(BF16)
        if not sb:
            dlt = jnp.sum(dof * o_ref[...], axis=1, keepdims=True)
        acc_ref[...] = jnp.zeros_like(acc_ref)
        nblk = ((i + 1) * tq + tk - 1) // tk if causal else Sk // tk

        def step(t, carry):
            off = pl.multiple_of(t * tk, tk)
            kb, vb = k_ref[pl.ds(off, tk), :], v_ref[pl.ds(off, tk), :]
            s = _nt(qb, kb) * scale
            dp = _nt(dob, vb)
            if causal:
                qpos = i * tq + lax.broadcasted_iota(jnp.int32, (tq, tk), 0)
                kpos = off + lax.broadcasted_iota(jnp.int32, (tq, tk), 1)
                valid = (kpos < qpos) if sb else (kpos <= qpos)
            if sb:
                u = u_ref[...]
                sp = _softplus(s)
                ls = jnp.where(valid, -sp, 0.0)
                lb = s - sp
                rest = lse_ref[0] - (r_ref[...] + _running_sum(ls, u))
                w = jnp.where(valid, jnp.exp(lb + rest), 0.0)
                e = dp * w
                upto = re_ref[...] + _running_sum(e, u)
                ds = jnp.where(valid, e - jnp.exp(lb) * upto, 0.0)
                r_ref[...] += jnp.sum(ls, axis=1, keepdims=True)
                re_ref[...] += jnp.sum(e, axis=1, keepdims=True)
            else:
                w = jnp.exp(s - lse_ref[0])
                if causal:
                    w = jnp.where(valid, w, 0.0)
                ds = w * (dp - dlt)
            dsb = (ds * scale).astype(BF16)
            dv_ref[pl.ds(off, tk), :] += _tn(w.astype(BF16), dob)
            dk_ref[pl.ds(off, tk), :] += _tn(dsb, qb)
            acc_ref[...] += jnp.dot(dsb, kb, preferred_element_type=F32)
            return carry

        lax.fori_loop(0, nblk, step, 0)
        dq_ref[...] = acc_ref[...]

    ins = [q, k, v, o, do]
    in_specs = [pl.BlockSpec((tq, dq), lambda h, i: (i, h)),
                pl.BlockSpec((Sk, dq), lambda h, i: (0, kcol(h))),
                pl.BlockSpec((Sk, dv), lambda h, i: (0, vcol(h))),
                pl.BlockSpec((tq, dv), lambda h, i: (i, h)),
                pl.BlockSpec((tq, dv), lambda h, i: (i, h))]
    scratch = [pltpu.VMEM((tq, dq), F32)]
    ins.append(lse)
    in_specs.append(pl.BlockSpec((1, tq, 1), lambda h, i: (h, i, 0)))
    if sb:
        ins.append(_triangle(tk, inclusive_prefix=True))
        in_specs.append(pl.BlockSpec((tk, tk), lambda h, i: (0, 0)))
        scratch += [pltpu.VMEM((tq, 1), F32), pltpu.VMEM((tq, 1), F32)]
    out_specs = [pl.BlockSpec((tq, dq), lambda h, i: (i, h)),
                 pl.BlockSpec((Sk, dq), lambda h, i: (0, h)),
                 pl.BlockSpec((Sk, dv), lambda h, i: (0, h))]
    out_shape = [jax.ShapeDtypeStruct((S, heads * dq), F32), jax.ShapeDtypeStruct((Sk, heads * dq), F32),
                 jax.ShapeDtypeStruct((Sk, heads * dv), F32)]
    return pl.pallas_call(body, name=name, grid=(heads, nq), in_specs=in_specs, out_specs=out_specs,
                          out_shape=out_shape, scratch_shapes=scratch, compiler_params=_params(2))(*ins)


GELU_C = 0.7978845608028654
assert 2 * SG_GD == LANE and SG_CHUNK == LANE


def _gelu(z):
    t = jnp.tanh(GELU_C * (z + 0.044715 * z * z * z))
    return 0.5 * z * (1.0 + t), t


def _gelu_grad(z, t):
    return 0.5 * (1.0 + t) + 0.5 * z * (1.0 - t * t) * GELU_C * (1.0 + 3.0 * 0.044715 * z * z)


def _layernorm_parts(g):
    d = g - jnp.mean(g, axis=-1, keepdims=True)
    rstd = lax.rsqrt(jnp.mean(d * d, axis=-1, keepdims=True) + EPS)
    return d * rstd, rstd


def _gelu_ln(z, gain, bias, *, name):
    def fn(zv, gn, bs):
        a, _ = _gelu(zv)
        y, _ = _layernorm_parts(a[:, SG_W:])
        return a[:, :SG_W], y * gn + bs

    return _rowwise(fn, [z], [gain.reshape(1, SG_W), bias.reshape(1, SG_W)], [(SG_W, F32), (SG_W, BF16)], name=name)


def _gelu_ln_bwd(z, du, dgl, gain, *, name):
    def fn(zv, duv, dglv, gn):
        a, t = _gelu(zv)
        y, rstd = _layernorm_parts(a[:, SG_W:])
        dy = dglv * gn
        dgg = rstd * (dy - jnp.mean(dy, axis=-1, keepdims=True) - y * jnp.mean(dy * y, axis=-1, keepdims=True))
        dz = jnp.concatenate([duv, dgg], axis=1) * _gelu_grad(zv, t)
        return dz, jnp.sum(dglv * y, axis=0, keepdims=True), jnp.sum(dglv, axis=0, keepdims=True)

    return _rowwise(fn, [z, du, dgl], [gain.reshape(1, SG_W)], [(2 * SG_W, BF16)], [(1, SG_W), (1, SG_W)], name=name)


def _sg_masks():
    tri = lax.broadcasted_iota(jnp.int32, (SG_CHUNK, SG_CHUNK), 0) >= lax.broadcasted_iota(jnp.int32, (SG_CHUNK, SG_CHUNK), 1)
    first = lax.broadcasted_iota(jnp.int32, (SG_CHUNK, LANE), 1) < SG_GD
    return tri, first


def _spatial(gl, u, w, bt, *, name):
    S = gl.shape[0]
    tm = _row_tile(S, 512)
    nch = tm // SG_CHUNK

    def body(gl_ref, u_ref, w_ref, bt_ref, o_ref):
        tri, first = _sg_masks()
        for p in range(SG_W // LANE):
            cols = slice(p * LANE, (p + 1) * LANE)
            wa = jnp.where(tri, w_ref[2 * p], 0.0).astype(BF16)
            wb = jnp.where(tri, w_ref[2 * p + 1], 0.0).astype(BF16)
            for ci in range(nch):
                rws = slice(ci * SG_CHUNK, (ci + 1) * SG_CHUNK)
                g = gl_ref[rws, cols]
                zero = jnp.zeros_like(g)
                mixed = (jnp.dot(wa, jnp.where(first, g, zero), preferred_element_type=F32)
                         + jnp.dot(wb, jnp.where(first, zero, g), preferred_element_type=F32) + bt_ref[:, cols])
                o_ref[rws, cols] = u_ref[rws, cols] * mixed

    row = pl.BlockSpec((tm, SG_W), lambda i: (i, 0))
    return pl.pallas_call(
        body, name=name, grid=(S // tm,),
        in_specs=[row, row, pl.BlockSpec(w.shape, lambda i: (0, 0, 0)), pl.BlockSpec(bt.shape, lambda i: (0, 0))],
        out_specs=row, out_shape=jax.ShapeDtypeStruct((S, SG_W), F32), compiler_params=_params(1))(gl, u, w, bt)


def _spatial_bwd(d_o, gl, u, w, bt, *, name):
    S = gl.shape[0]
    tm = _row_tile(S, 512)
    nch = tm // SG_CHUNK
    nsteps = S // tm

    def body(do_ref, gl_ref, u_ref, w_ref, bt_ref, du_ref, dgl_ref, dw_ref, db_ref, dbt_ref):
        tri, first = _sg_masks()
        step = pl.program_id(0)

        @pl.when(step == 0)
        def _():
            dw_ref[...] = jnp.zeros_like(dw_ref)
            dbt_ref[...] = jnp.zeros_like(dbt_ref)

        for p in range(SG_W // LANE):
            cols = slice(p * LANE, (p + 1) * LANE)
            wa = jnp.where(tri, w_ref[2 * p], 0.0).astype(BF16)
            wb = jnp.where(tri, w_ref[2 * p + 1], 0.0).astype(BF16)
            for ci in range(nch):
                rws = slice(ci * SG_CHUNK, (ci + 1) * SG_CHUNK)
                g = gl_ref[rws, cols]
                zero = jnp.zeros_like(g)
                mixed = (jnp.dot(wa, jnp.where(first, g, zero), preferred_element_type=F32)
                         + jnp.dot(wb, jnp.where(first, zero, g), preferred_element_type=F32) + bt_ref[:, cols])
                dov = do_ref[rws, cols]
                du_ref[rws, cols] = dov * mixed
                dm = dov * u_ref[rws, cols]
                dbt_ref[:, cols] += dm
                dma = jnp.where(first, dm, 0.0).astype(BF16)
                dmb = jnp.where(first, 0.0, dm).astype(BF16)
                dw_ref[2 * p] += jnp.where(tri, _nt(dma, g), 0.0)
                dw_ref[2 * p + 1] += jnp.where(tri, _nt(dmb, g), 0.0)
                dgl_ref[rws, cols] = _tn(wa, dma) + _tn(wb, dmb)

        @pl.when(step == nsteps - 1)
        def _():
            lane = lax.broadcasted_iota(jnp.int32, (SG_CHUNK, LANE), 1)
            acc = jnp.zeros((SG_CHUNK, LANE), F32)
            for p in range(SG_W // LANE):
                blk = dbt_ref[:, p * LANE:(p + 1) * LANE]
                sa = jnp.sum(jnp.where(first, blk, 0.0), axis=1, keepdims=True)
                sb_ = jnp.sum(jnp.where(first, 0.0, blk), axis=1, keepdims=True)
                acc = acc + jnp.where(lane == 2 * p, sa, 0.0) + jnp.where(lane == 2 * p + 1, sb_, 0.0)
            db_ref[...] = acc

    row = pl.BlockSpec((tm, SG_W), lambda i: (i, 0))
    return pl.pallas_call(
        body, name=name, grid=(nsteps,),
        in_specs=[row, row, row, pl.BlockSpec(w.shape, lambda i: (0, 0, 0)), pl.BlockSpec(bt.shape, lambda i: (0, 0))],
        out_specs=[row, row, pl.BlockSpec(w.shape, lambda i: (0, 0, 0)), pl.BlockSpec((SG_CHUNK, LANE), lambda i: (0, 0))],
        out_shape=[jax.ShapeDtypeStruct((S, SG_W), F32), jax.ShapeDtypeStruct((S, SG_W), F32),
                   jax.ShapeDtypeStruct(w.shape, F32), jax.ShapeDtypeStruct((SG_CHUNK, LANE), F32)],
        scratch_shapes=[pltpu.VMEM((SG_CHUNK, SG_W), F32)], compiler_params=_params(1))(d_o, gl, u, w, bt)


ROPE_HALF = MLA_ROPE // 2
KR_COL = (MLA_QL + MLA_KVL) // LANE
MLA_IN_PAD = MLA_QL + MLA_KVL + LANE


def _rope_tables(positions):
    inv_freq = ROPE_THETA ** (-jnp.arange(ROPE_HALF, dtype=F32) / ROPE_HALF)
    ang = positions.astype(F32)[:, None] * inv_freq
    cos, sin = jnp.cos(ang), jnp.sin(ang)
    S = positions.shape[0]
    z16, tail = jnp.zeros((S, ROPE_HALF), F32), jnp.zeros((S, LANE - MLA_QK), F32)
    ones = jnp.ones((S, MLA_NOPE), F32)
    zeros = jnp.zeros((S, MLA_NOPE), F32)
    return (jnp.concatenate([ones, cos, cos, tail], axis=1), jnp.concatenate([zeros, z16, sin, tail], axis=1),
            jnp.concatenate([zeros, -sin, z16, tail], axis=1))


def _rope(x, cos, sa, sb):
    return x * cos + pltpu.roll(x, ROPE_HALF, 1) * sa + pltpu.roll(x, LANE - ROPE_HALF, 1) * sb


def _rope_t(dy, cos, sa, sb):
    return dy * cos + pltpu.roll(dy * sa, LANE - ROPE_HALF, 1) + pltpu.roll(dy * sb, ROPE_HALF, 1)


def _mla_lora(P, qlg, kvlg, *, name):
    def fn(pv, a, b):
        return _rmsnorm_fwd(pv[:, :MLA_QL], a), _rmsnorm_fwd(pv[:, MLA_QL:MLA_QL + MLA_KVL], b)

    return _rowwise(fn, [P], [qlg.reshape(1, MLA_QL), kvlg.reshape(1, MLA_KVL)], [(MLA_QL, BF16), (MLA_KVL, BF16)], name=name)


def _mla_lora_bwd(dcq, dckv, dkr, P, qlg, kvlg, *, name):
    def fn(d1, d2, d3, pv, a, b):
        x1, g1 = _rmsnorm_bwd(d1, pv[:, :MLA_QL], a)
        x2, g2 = _rmsnorm_bwd(d2, pv[:, MLA_QL:MLA_QL + MLA_KVL], b)
        return jnp.concatenate([x1, x2, d3], axis=1), g1, g2

    return _rowwise(fn, [dcq, dckv, dkr, P], [qlg.reshape(1, MLA_QL), kvlg.reshape(1, MLA_KVL)], [(MLA_IN_PAD, BF16)],
                    [(1, MLA_QL), (1, MLA_KVL)], name=name)


def _mla_qk(q_pre, k_pre, P, tabs, qg, kg, *, name):
    def fn(qp, kp, kr, c, a, b, qgv, kgv):
        return (_rope(_rmsnorm_fwd(qp, qgv, MLA_QK), c, a, b), _rope(_rmsnorm_fwd(kp + kr, kgv, MLA_QK), c, a, b))

    hcol = lambda h: h
    rows = [(q_pre, LANE, hcol), (k_pre, LANE, hcol), (P, LANE, lambda h: KR_COL), *tabs]
    w = MLA_HEADS * LANE
    return _rowwise(fn, rows, [qg, kg], [(w, BF16), (w, BF16)], heads=MLA_HEADS, name=name)


def _mla_qk_bwd(dq, dk, q_pre, k_pre, P, tabs, qg, kg, *, name):
    def fn(dqv, dkv, qp, kp, kr, c, a, b, qgv, kgv):
        dqp, dqg = _rmsnorm_bwd(_rope_t(dqv, c, a, b), qp, qgv, MLA_QK)
        dkp, dkg = _rmsnorm_bwd(_rope_t(dkv, c, a, b), kp + kr, kgv, MLA_QK)
        lane = lax.broadcasted_iota(jnp.int32, (1, LANE), 1)
        return dqp, dkp, jnp.where((lane >= MLA_NOPE) & (lane < MLA_QK), dkp, 0.0), dqg, dkg

    hcol = lambda h: h
    rows = [(dq, LANE, hcol), (dk, LANE, hcol), (q_pre, LANE, hcol), (k_pre, LANE, hcol), (P, LANE, lambda h: KR_COL), *tabs]
    w = MLA_HEADS * LANE
    return _rowwise(fn, rows, [qg, kg], [(w, BF16), (w, BF16)], [(1, LANE), (1, LANE)], [LANE], heads=MLA_HEADS, name=name)


def _head_norm(x, g, *, heads, width, colfn=None, name):
    return _rowwise(lambda xv, gv: _rmsnorm_fwd(xv, gv), [(x, width, colfn or (lambda h: h))], [g.reshape(1, width)],
                    [(heads * width, BF16)], heads=heads, name=name)[0]


def _head_norm_bwd(dy, x, g, *, heads, width, colfn=None, out_dtype, name):
    return _rowwise(lambda dv_, xv, gv: _rmsnorm_bwd(dv_, xv, gv), [(dy, width, lambda h: h), (x, width, colfn or (lambda h: h))],
                    [g.reshape(1, width)], [(heads * width, out_dtype)], [(1, width)], heads=heads, name=name)


def _loss_grad(y, tgt, *, name):
    D = y.shape[1]

    def fn(yv, tv):
        d = yv - tv
        return d * (1.0 / D), jnp.sum(d * d, axis=0, keepdims=True) * (0.5 / D)

    dy, part = _rowwise(fn, [y, tgt], [], [(D, F32)], [(1, D)], name=name)
    return jnp.sum(part), dy


def _adamw(w, g, m, v, *, name):
    shape = w.shape
    two_d = (-1, shape[-1])

    def fn(wv, gv, mv, vv):
        m2 = ADAM_B1 * mv + (1.0 - ADAM_B1) * gv
        v2 = ADAM_B2 * vv + (1.0 - ADAM_B2) * (gv * gv)
        m_hat = m2 / (1.0 - ADAM_B1 ** ADAM_STEP)
        v_hat = v2 / (1.0 - ADAM_B2 ** ADAM_STEP)
        return -ADAM_LR * (m_hat / (jnp.sqrt(v_hat) + ADAM_EPS) + ADAM_WD * wv), m2, v2

    outs = _rowwise(fn, [t.reshape(two_d) for t in (w, g, m, v)], [], [(shape[-1], F32)] * 3, name=name)
    return [o.reshape(shape) for o in outs]


def _pad_cols(w, heads, hd):
    k = w.shape[0]
    return jnp.pad(w.reshape(k, heads, hd), ((0, 0), (0, 0), (0, LANE - hd))).reshape(k, heads * LANE)


def _unpad_cols(w, heads, hd):
    k = w.shape[0]
    return w.reshape(k, heads, LANE)[:, :, :hd].reshape(k, heads * hd)


def _pad_rows(w, heads, hd):
    n = w.shape[1]
    return jnp.pad(w.reshape(heads, hd, n), ((0, 0), (0, LANE - hd), (0, 0))).reshape(heads * LANE, n)


def _unpad_rows(w, heads, hd):
    n = w.shape[1]
    return w.reshape(heads, LANE, n)[:, :hd, :].reshape(heads * hd, n)


def _ffn_fwd(x, g, wgu, wd, tag):
    h = _norm_rows(x, g, name=tag + "_norm")
    gate, up, act = _mm_swiglu(h, wgu, name=tag + "_gu")
    y = _mm(act, wd, scale=0.5, residual=x, name=tag + "_down")
    return y, (x, h, gate, up, act)


def _ffn_bwd(dy, saved, g, wgu, wd, tag):
    x, h, gate, up, act = saved
    F = wd.shape[0]
    dwd = _mm(act, dy, ta=True, scale=0.5, name=tag + "_dwd")
    dgate, dup = _mm_dswiglu(dy, wd, gate, up, scale=0.5, name=tag + "_dact")
    dh = _mm(dgate, wgu, tb=True, name=tag + "_dh_g")
    dh = _mm(dup, wgu, tb=True, b_off=(0, F // _tile(F, 512)), residual=dh, name=tag + "_dh_u")
    dwgu = jnp.concatenate([_mm(h, dgate, ta=True, name=tag + "_dwg"), _mm(h, dup, ta=True, name=tag + "_dwu")], axis=1)
    dx, dg = _norm_rows_bwd(dh, x, g, dy, name=tag + "_dnorm")
    return dx, dg, dwgu, dwd


def _even_weights(w_in, w_out):
    wqkv = jnp.concatenate([_pad_cols(w_in[:, i * SB_W:(i + 1) * SB_W], SB_HEADS, SB_HD) for i in range(3)], axis=1)
    return wqkv, w_in[:, 3 * SB_W:], _pad_rows(w_out[:SB_W], SB_HEADS, SB_HD), w_out[SB_W:]


def _even_fwd(x, g, wts, ln_g, ln_b, sgu_w, bt, tag):
    wqkv, wz, wo_sb, wo_sg = wts
    h = _norm_rows(x, g, name=tag + "_norm")
    qkv = _mm(h, wqkv, out_dtype=BF16, name=tag + "_qkv")
    z = _mm(h, wz, name=tag + "_z")
    o_sb, tot = _attn_fwd(qkv, qkv, qkv, sb=True, causal=True, heads=SB_HEADS, dq=LANE, dv=LANE, scale=SB_HD ** -0.5,
                          kcol=lambda hh: SB_HEADS + hh, vcol=lambda hh: 2 * SB_HEADS + hh, name=tag + "_sb")
    u, gl = _gelu_ln(z, ln_g, ln_b, name=tag + "_geluln")
    o_sg = _spatial(gl, u, sgu_w, bt, name=tag + "_sgu")
    y = _mm(o_sb, wo_sb, residual=x, name=tag + "_out_sb")
    y = _mm(o_sg, wo_sg, residual=y, name=tag + "_out_sg")
    return y, (x, h, qkv, z, o_sb, tot, u, gl, o_sg)


def _even_bwd(dy, saved, g, wts, ln_g, sgu_w, bt, tag):
    wqkv, wz, wo_sb, wo_sg = wts
    x, h, qkv, z, o_sb, tot, u, gl, o_sg = saved
    do_sb = _mm(dy, wo_sb, tb=True, name=tag + "_do_sb")
    do_sg = _mm(dy, wo_sg, tb=True, name=tag + "_do_sg")
    dwo = jnp.concatenate([_unpad_rows(_mm(o_sb, dy, ta=True, name=tag + "_dwo_sb"), SB_HEADS, SB_HD),
                           _mm(o_sg, dy, ta=True, name=tag + "_dwo_sg")], axis=0)
    dq, dk, dv = _attn_bwd(qkv, qkv, qkv, o_sb, do_sb, tot, sb=True, causal=True, heads=SB_HEADS, dq=LANE, dv=LANE,
                           scale=SB_HD ** -0.5, kcol=lambda hh: SB_HEADS + hh, vcol=lambda hh: 2 * SB_HEADS + hh,
                           name=tag + "_sb_bwd")
    du, dgl, dsgu_w, db_t = _spatial_bwd(do_sg, gl, u, sgu_w, bt, name=tag + "_sgu_bwd")
    dz, dln_g, dln_b = _gelu_ln_bwd(z, du, dgl, ln_g, name=tag + "_geluln_bwd")
    kb = (SB_HEADS * LANE) // _tile(SB_HEADS * LANE, 512)
    dh = _mm(dz, wz, tb=True, name=tag + "_dh_z")
    dws = []
    for i, (d, nm) in enumerate(((dq, "q"), (dk, "k"), (dv, "v"))):
        dh = _mm(d, wqkv, tb=True, b_off=(0, i * kb), residual=dh, name=tag + "_dh_" + nm)
        dws.append(_unpad_cols(_mm(h, d, ta=True, name=tag + "_dw_" + nm), SB_HEADS, SB_HD))
    dws.append(_mm(h, dz, ta=True, name=tag + "_dw_z"))
    dx, dg = _norm_rows_bwd(dh, x, g, dy, name=tag + "_dnorm")
    return dx, dict(mix_norm=dg, sbg_w_in=jnp.concatenate(dws, axis=1), sgu_ln_gain=dln_g, sgu_ln_bias=dln_b,
                    sgu_w=dsgu_w, sgu_b=db_t[:, :SG_GROUPS].T, sbg_w_out=dwo)


def _mla_weights(w_in, w_uq, w_ukv, w_out, q_gain, k_gain):
    d = w_in.shape[0]
    lat = MLA_QL + MLA_KVL
    w_in_ext = jnp.concatenate([w_in[:, :lat], jnp.zeros((d, MLA_NOPE), w_in.dtype), w_in[:, lat:],
                                jnp.zeros((d, LANE - MLA_QK), w_in.dtype)], axis=1)
    kv = w_ukv.reshape(MLA_KVL, MLA_HEADS, MLA_NOPE + MLA_V)
    wk = _pad_cols(kv[:, :, :MLA_NOPE].reshape(MLA_KVL, -1), MLA_HEADS, MLA_NOPE)
    wv = _pad_cols(kv[:, :, MLA_NOPE:].reshape(MLA_KVL, -1), MLA_HEADS, MLA_V)
    pad_gain = lambda gn: jnp.pad(gn.reshape(1, MLA_QK), ((0, 0), (0, LANE - MLA_QK)))
    return (w_in_ext, _pad_cols(w_uq, MLA_HEADS, MLA_QK), wk, wv, _pad_rows(w_out, MLA_HEADS, MLA_V),
            pad_gain(q_gain), pad_gain(k_gain))


def _mla_fwd(x, g, wts, qlg, kvlg, tabs, tag):
    w_in, w_uq, wk, wv, w_out, qg, kg = wts
    h = _norm_rows(x, g, name=tag + "_norm")
    P = _mm(h, w_in, name=tag + "_in")
    cqn, ckvn = _mla_lora(P, qlg, kvlg, name=tag + "_lora")
    q_pre = _mm(cqn, w_uq, name=tag + "_uq")
    k_pre = _mm(ckvn, wk, name=tag + "_uk")
    v = _mm(ckvn, wv, out_dtype=BF16, name=tag + "_uv")
    q, k = _mla_qk(q_pre, k_pre, P, tabs, qg, kg, name=tag + "_qk")
    o, lse = _attn_fwd(q, k, v, sb=False, causal=True, heads=MLA_HEADS, dq=LANE, dv=LANE, scale=MLA_QK ** -0.5,
                       name=tag + "_attn")
    y = _mm(o, w_out, residual=x, name=tag + "_out")
    return y, (x, h, P, cqn, ckvn, q_pre, k_pre, q, k, v, o, lse)


def _mla_bwd(dy, saved, g, wts, qlg, kvlg, tabs, tag):
    w_in, w_uq, wk, wv, w_out, qg, kg = wts
    x, h, P, cqn, ckvn, q_pre, k_pre, q, k, v, o, lse = saved
    do = _mm(dy, w_out, tb=True, name=tag + "_do")
    dw_out = _unpad_rows(_mm(o, dy, ta=True, name=tag + "_dwo"), MLA_HEADS, MLA_V)
    dq, dk, dv = _attn_bwd(q, k, v, o, do, lse, sb=False, causal=True, heads=MLA_HEADS, dq=LANE, dv=LANE,
                           scale=MLA_QK ** -0.5, name=tag + "_attn_bwd")
    dq_pre, dk_pre, dkr, dqg, dkg = _mla_qk_bwd(dq, dk, q_pre, k_pre, P, tabs, qg, kg, name=tag + "_qk_bwd")
    dcqn = _mm(dq_pre, w_uq, tb=True, name=tag + "_dcq")
    dckvn = _mm(dk_pre, wk, tb=True, name=tag + "_dckv_k")
    dckvn = _mm(dv, wv, tb=True, residual=dckvn, name=tag + "_dckv_v")
    dw_uq = _unpad_cols(_mm(cqn, dq_pre, ta=True, name=tag + "_dwuq"), MLA_HEADS, MLA_QK)
    dwk = _unpad_cols(_mm(ckvn, dk_pre, ta=True, name=tag + "_dwk"), MLA_HEADS, MLA_NOPE)
    dwv = _unpad_cols(_mm(ckvn, dv, ta=True, name=tag + "_dwv"), MLA_HEADS, MLA_V)
    dw_ukv = jnp.concatenate([dwk.reshape(MLA_KVL, MLA_HEADS, MLA_NOPE), dwv.reshape(MLA_KVL, MLA_HEADS, MLA_V)],
                             axis=2).reshape(MLA_KVL, -1)
    dP, dqlg, dkvlg = _mla_lora_bwd(dcqn, dckvn, dkr, P, qlg, kvlg, name=tag + "_lora_bwd")
    dh = _mm(dP, w_in, tb=True, name=tag + "_dh")
    dw_in_ext = _mm(h, dP, ta=True, name=tag + "_dwin")
    lat = MLA_QL + MLA_KVL
    dw_in = jnp.concatenate([dw_in_ext[:, :lat], dw_in_ext[:, lat + MLA_NOPE:lat + MLA_QK]], axis=1)
    dx, dg = _norm_rows_bwd(dh, x, g, dy, name=tag + "_dnorm")
    return dx, dict(mix_norm=dg, mla_w_in=dw_in, mla_q_lora_gain=dqlg, mla_kv_lora_gain=dkvlg, mla_w_uq=dw_uq,
                    mla_w_ukv=dw_ukv, mla_q_gain=dqg[:, :MLA_QK], mla_k_gain=dkg[:, :MLA_QK], mla_w_out=dw_out)


def _xmem_fwd(x, mem, g, gm, wq, wkv, qg, kg, wo, tag):
    hq = _norm_rows(x, g, name=tag + "_norm")
    hm = _norm_rows(mem, gm, name=tag + "_mnorm")
    qp = _mm(hq, wq, name=tag + "_q")
    kv = _mm(hm, wkv, name=tag + "_kv")
    q = _head_norm(qp, qg, heads=MEM_HEADS, width=MEM_HD, name=tag + "_qn")
    kn = _head_norm(kv, kg, heads=MEM_HEADS, width=MEM_HD, colfn=lambda hh: 2 * hh, name=tag + "_kn")
    kvb = kv.astype(BF16)
    o, lse = _attn_fwd(q, kn, kvb, sb=False, causal=False, heads=MEM_HEADS, dq=MEM_HD, dv=MEM_HD, scale=MEM_HD ** -0.5,
                       vcol=lambda hh: 2 * hh + 1, name=tag + "_attn")
    y = _mm(o, wo, residual=x, name=tag + "_out")
    return y, (x, hq, hm, qp, kv, q, kn, kvb, o, lse)


def _xmem_bwd(dy, saved, mem, g, gm, wq, wkv, qg, kg, wo, tag):
    x, hq, hm, qp, kv, q, kn, kvb, o, lse = saved
    m = mem.shape[0]
    do = _mm(dy, wo, tb=True, name=tag + "_do")
    dwo = _mm(o, dy, ta=True, name=tag + "_dwo")
    dq, dk, dv = _attn_bwd(q, kn, kvb, o, do, lse, sb=False, causal=False, heads=MEM_HEADS, dq=MEM_HD, dv=MEM_HD,
                           scale=MEM_HD ** -0.5, vcol=lambda hh: 2 * hh + 1, name=tag + "_attn_bwd")
    dqp, dqg = _head_norm_bwd(dq, qp, qg, heads=MEM_HEADS, width=MEM_HD, out_dtype=BF16, name=tag + "_qn_bwd")
    dkp, dkg = _head_norm_bwd(dk, kv, kg, heads=MEM_HEADS, width=MEM_HD, colfn=lambda hh: 2 * hh, out_dtype=F32,
                              name=tag + "_kn_bwd")
    dkv = jnp.concatenate([dkp.reshape(m, MEM_HEADS, MEM_HD), dv.reshape(m, MEM_HEADS, MEM_HD)], axis=2).reshape(m, -1)
    dwkv = _mm(hm, dkv, ta=True, name=tag + "_dwkv")
    dhm = _mm(dkv, wkv, tb=True, name=tag + "_dhm")
    _, dgm = _norm_rows_bwd(dhm, mem, gm, None, name=tag + "_dmnorm")
    dwq = _mm(hq, dqp, ta=True, name=tag + "_dwq")
    dhq = _mm(dqp, wq, tb=True, name=tag + "_dhq")
    dx, dg = _norm_rows_bwd(dhq, x, g, dy, name=tag + "_dnorm")
    return dx, dict(xmem_norm=dg, xmem_mem_norm=dgm, xmem_wq=dwq, xmem_wkv=dwkv, xmem_q_gain=dqg, xmem_k_gain=dkg,
                    xmem_wo=dwo)


def _local_step(x, mem, positions, tgt, w):
    tabs = _rope_tables(positions)
    even = _even_weights(w["sbg_w_in"][0], w["sbg_w_out"][0])
    mla = _mla_weights(w["mla_w_in"][0], w["mla_w_uq"][0], w["mla_w_ukv"][0], w["mla_w_out"][0], w["mla_q_gain"][0],
                       w["mla_k_gain"][0])
    bt = jnp.repeat(w["sgu_b"][0].T, SG_GD, axis=1)
    saved = []
    for l in range(2):
        t = f"l{l}"
        x, s_pre = _ffn_fwd(x, w["ffn_pre_norm"][l], w["ffn_pre_w_gu"][l], w["ffn_pre_w_down"][l], t + "_pre")
        if l == 0:
            x, s_mix = _even_fwd(x, w["mix_norm"][0], even, w["sgu_ln_gain"][0], w["sgu_ln_bias"][0], w["sgu_w"][0], bt,
                                 t + "_even")
        else:
            x, s_mix = _mla_fwd(x, w["mix_norm"][1], mla, w["mla_q_lora_gain"][0], w["mla_kv_lora_gain"][0], tabs,
                                t + "_mla")
        x, s_xm = _xmem_fwd(x, mem, w["xmem_norm"][l], w["xmem_mem_norm"][l], w["xmem_wq"][l], w["xmem_wkv"][l],
                            w["xmem_q_gain"][l], w["xmem_k_gain"][l], w["xmem_wo"][l], t + "_xm")
        x, s_post = _ffn_fwd(x, w["ffn_post_norm"][l], w["ffn_post_w_gu"][l], w["ffn_post_w_down"][l], t + "_post")
        saved.append((s_pre, s_mix, s_xm, s_post))
    loss, dx = _loss_grad(x, tgt, name="loss")
    grads = {}

    def put(name, l, val):
        grads.setdefault(name, {})[l] = val

    for l in (1, 0):
        t = f"l{l}"
        s_pre, s_mix, s_xm, s_post = saved[l]
        dx, dg, dwgu, dwd = _ffn_bwd(dx, s_post, w["ffn_post_norm"][l], w["ffn_post_w_gu"][l], w["ffn_post_w_down"][l],
                                     t + "_post")
        put("ffn_post_norm", l, dg), put("ffn_post_w_gu", l, dwgu), put("ffn_post_w_down", l, dwd)
        dx, gx = _xmem_bwd(dx, s_xm, mem, w["xmem_norm"][l], w["xmem_mem_norm"][l], w["xmem_wq"][l], w["xmem_wkv"][l],
                           w["xmem_q_gain"][l], w["xmem_k_gain"][l], w["xmem_wo"][l], t + "_xm")
        for k_, v_ in gx.items():
            put(k_, l, v_)
        if l == 0:
            dx, gm = _even_bwd(dx, s_mix, w["mix_norm"][0], even, w["sgu_ln_gain"][0], w["sgu_w"][0], bt, t + "_even")
        else:
            dx, gm = _mla_bwd(dx, s_mix, w["mix_norm"][1], mla, w["mla_q_lora_gain"][0], w["mla_kv_lora_gain"][0], tabs,
                              t + "_mla")
        for k_, v_ in gm.items():
            put(k_, l if k_ == "mix_norm" else 0, v_)
        dx, dg, dwgu, dwd = _ffn_bwd(dx, s_pre, w["ffn_pre_norm"][l], w["ffn_pre_w_gu"][l], w["ffn_pre_w_down"][l],
                                     t + "_pre")
        put("ffn_pre_norm", l, dg), put("ffn_pre_w_gu", l, dwgu), put("ffn_pre_w_down", l, dwd)
    return loss, dx, {k_: [v_[l] for l in sorted(v_)] for k_, v_ in grads.items()}


N_CHIPS = 4
PACK_COLS = 1024
PACK_ROW_MULTIPLE = 512


def _place():
    x, y, c = lax.axis_index("x"), lax.axis_index("y"), lax.axis_index("c")
    return x, y, c, [(1 - x, y), (x, 1 - y), (1 - x, 1 - y)]


def _gather_chips(shard):
    R, C = shard.shape
    Rh = R // 2

    def body(x_ref, out_ref, send_sems, recv_sems, local_sem):
        x, y, c, chips = _place()
        me = 2 * x + y

        def half(chip, core):
            return out_ref.at[chip, pl.ds(core * Rh, Rh), :]

        def copy(k, chip, core, to, src=None):
            return pltpu.make_async_remote_copy(src_ref=half(chip, core) if src is None else src, dst_ref=half(chip, core),
                                                send_sem=send_sems.at[k], recv_sem=recv_sems.at[k], device_id=to,
                                                device_id_type=MESH)

        mine = pltpu.make_async_copy(x_ref, out_ref.at[me], local_sem)
        mine.start()
        first = [copy(j, me, c, (qx, qy, c), src=x_ref.at[pl.ds(c * Rh, Rh), :]) for j, (qx, qy) in enumerate(chips)]
        for cp in first:
            cp.start()
        passed = [copy(3 + j, 2 * qx + qy, c, (x, y, 1 - c)) for j, (qx, qy) in enumerate(chips)]
        for j, (qx, qy) in enumerate(chips):
            copy(j, 2 * qx + qy, c, (x, y, c)).wait_recv()
            passed[j].start()
        for j, (qx, qy) in enumerate(chips):
            copy(3 + j, 2 * qx + qy, 1 - c, (x, y, c)).wait_recv()
        for cp in first + passed:
            cp.wait_send()
        mine.wait()

    return pl.pallas_call(
        body, name="gather_weights", out_shape=jax.ShapeDtypeStruct((N_CHIPS, R, C), shard.dtype),
        in_specs=[ANY], out_specs=ANY,
        scratch_shapes=[pltpu.SemaphoreType.DMA((6,)), pltpu.SemaphoreType.DMA((6,)), pltpu.SemaphoreType.DMA])(shard)


def _gather_devices(block):
    M, N = block.shape

    def body(x_ref, out_ref, send_sems, recv_sems, local_sem):
        x, y, c, chips = _place()
        me, sibling = (x, y, c), (x, y, 1 - c)

        def rows(px, py, pc):
            return out_ref.at[pl.ds((4 * px + 2 * py + pc) * M, M), :]

        def copy(k, blk, to, src=None):
            return pltpu.make_async_remote_copy(src_ref=rows(*blk) if src is None else src, dst_ref=rows(*blk),
                                                send_sem=send_sems.at[k], recv_sem=recv_sems.at[k], device_id=to,
                                                device_id_type=MESH)

        mine = pltpu.make_async_copy(x_ref, rows(*me), local_sem)
        mine.start()
        first = [copy(0, me, sibling, src=x_ref)]
        first += [copy(1 + j, me, (*chip, c), src=x_ref) for j, chip in enumerate(chips)]
        for cp in first:
            cp.start()
        passed = [copy(4 + j, (*chip, c), sibling) for j, chip in enumerate(chips)]
        for j, chip in enumerate(chips):
            copy(1 + j, (*chip, c), me).wait_recv()
            passed[j].start()
        copy(0, sibling, me).wait_recv()
        for j, chip in enumerate(chips):
            copy(4 + j, (*chip, 1 - c), me).wait_recv()
        for cp in first + passed:
            cp.wait_send()
        mine.wait()

    vmem = pl.BlockSpec(memory_space=pltpu.VMEM)
    return pl.pallas_call(
        body, name=f"gather_devices_{M}", out_shape=jax.ShapeDtypeStruct((8 * M, N), block.dtype),
        in_specs=[vmem], out_specs=vmem,
        scratch_shapes=[pltpu.SemaphoreType.DMA((7,)), pltpu.SemaphoreType.DMA((7,)), pltpu.SemaphoreType.DMA],
        compiler_params=pltpu.CompilerParams(vmem_limit_bytes=VMEM_LIMIT))(block)


def _swap_halves(g):
    n, R, C = g.shape
    Rh = R // 2

    def body(g_ref, a_ref, send_sem, recv_sem):
        x, y, c, _ = _place()
        cp = pltpu.make_async_remote_copy(src_ref=g_ref.at[:, pl.ds((1 - c) * Rh, Rh), :], dst_ref=a_ref,
                                          send_sem=send_sem, recv_sem=recv_sem, device_id=(x, y, 1 - c),
                                          device_id_type=MESH)
        cp.start()
        cp.wait()

    return pl.pallas_call(body, name="grad_swap_halves", out_shape=jax.ShapeDtypeStruct((n, Rh, C), g.dtype),
                          in_specs=[ANY], out_specs=ANY,
                          scratch_shapes=[pltpu.SemaphoreType.DMA, pltpu.SemaphoreType.DMA])(g)


def _add_own_half(g, a, core):
    n, R, C = g.shape
    Rh = R // 2
    tr = _row_tile(Rh, 512)
    nt = Rh // tr

    def body(c_ref, g_ref, a_ref, o_ref):
        o_ref[...] = g_ref[...] + a_ref[...]

    blk = pl.BlockSpec((1, tr, C), lambda q, i, c_ref: (q, i, 0))
    spec = pltpu.PrefetchScalarGridSpec(
        num_scalar_prefetch=1, grid=(n, nt),
        in_specs=[pl.BlockSpec((1, tr, C), lambda q, i, c_ref: (q, c_ref[0] * nt + i, 0)), blk], out_specs=blk)
    return pl.pallas_call(body, name="grad_add_own_half", grid_spec=spec, out_shape=jax.ShapeDtypeStruct((n, Rh, C), F32),
                          compiler_params=_params(2))(core.reshape(1).astype(jnp.int32), g, a)


def _swap_chips(p):
    n, Rh, C = p.shape

    def body(p_ref, b_ref, send_sems, recv_sems, local_sem):
        x, y, c, chips = _place()
        me = 2 * x + y
        mine = pltpu.make_async_copy(p_ref.at[me], b_ref.at[me], local_sem)
        mine.start()
        sends = []
        for j, (qx, qy) in enumerate(chips):
            cp = pltpu.make_async_remote_copy(src_ref=p_ref.at[2 * qx + qy], dst_ref=b_ref.at[me], send_sem=send_sems.at[j],
                                              recv_sem=recv_sems.at[j], device_id=(qx, qy, c), device_id_type=MESH)
            cp.start()
            sends.append(cp)
        for j, (qx, qy) in enumerate(chips):
            pltpu.make_async_remote_copy(src_ref=p_ref.at[me], dst_ref=b_ref.at[2 * qx + qy], send_sem=send_sems.at[j],
                                         recv_sem=recv_sems.at[j], device_id=(qx, qy, c), device_id_type=MESH).wait_recv()
        for cp in sends:
            cp.wait_send()
        mine.wait()

    return pl.pallas_call(
        body, name="grad_swap_chips", out_shape=jax.ShapeDtypeStruct((n, Rh, C), p.dtype), in_specs=[ANY], out_specs=ANY,
        scratch_shapes=[pltpu.SemaphoreType.DMA((3,)), pltpu.SemaphoreType.DMA((3,)), pltpu.SemaphoreType.DMA])(p)


def _sum_slots(b, *, name):
    n, R, C = b.shape
    tr = _row_tile(R, 512)

    def body(b_ref, o_ref):
        acc = b_ref[0]
        for q in range(1, n):
            acc = acc + b_ref[q]
        o_ref[...] = acc

    return pl.pallas_call(body, name=name, grid=(R // tr,), in_specs=[pl.BlockSpec((n, tr, C), lambda i: (0, i, 0))],
                          out_specs=pl.BlockSpec((tr, C), lambda i: (i, 0)), out_shape=jax.ShapeDtypeStruct((R, C), F32),
                          compiler_params=_params(1))(b)


def _join_halves(r):
    Rh, C = r.shape

    def body(r_ref, o_ref, send_sem, recv_sem, local_sem):
        x, y, c, _ = _place()
        own, other = o_ref.at[pl.ds(c * Rh, Rh), :], o_ref.at[pl.ds((1 - c) * Rh, Rh), :]
        mine = pltpu.make_async_copy(r_ref, own, local_sem)
        mine.start()
        cp = pltpu.make_async_remote_copy(src_ref=r_ref, dst_ref=own, send_sem=send_sem, recv_sem=recv_sem,
                                          device_id=(x, y, 1 - c), device_id_type=MESH)
        cp.start()
        pltpu.make_async_remote_copy(src_ref=r_ref, dst_ref=other, send_sem=send_sem, recv_sem=recv_sem,
                                     device_id=(x, y, 1 - c), device_id_type=MESH).wait_recv()
        cp.wait_send()
        mine.wait()

    return pl.pallas_call(
        body, name="grad_join_halves", out_shape=jax.ShapeDtypeStruct((2 * Rh, C), r.dtype), in_specs=[ANY], out_specs=ANY,
        scratch_shapes=[pltpu.SemaphoreType.DMA, pltpu.SemaphoreType.DMA, pltpu.SemaphoreType.DMA])(r)


def _pack(pieces, cols, row_multiple, dtype):
    flat = jnp.concatenate([p.reshape(-1).astype(dtype) for p in pieces])
    per = cols * row_multiple
    total = -(-flat.shape[0] // per) * per
    return jnp.pad(flat, (0, total - flat.shape[0])).reshape(total // cols, cols)


def _unpack(buf, shapes):
    flat, out, at = buf.reshape(-1), [], 0
    for shp in shapes:
        size = 1
        for d in shp:
            size *= d
        out.append(flat[at:at + size].reshape(shp))
        at += size
    return out


SHARDED = (("ffn_pre_w_gu", 2), ("ffn_pre_w_down", 1), ("sbg_w_in", 2), ("sbg_w_out", 1), ("mla_w_in", 1),
           ("mla_w_uq", 2), ("mla_w_ukv", 2), ("mla_w_out", 1), ("xmem_wq", 1), ("xmem_wkv", 2), ("xmem_wo", 1),
           ("ffn_post_w_gu", 2), ("ffn_post_w_down", 1))
LORA_GAINS = ("mla_q_lora_gain", "mla_kv_lora_gain")
REPLICATED = ("ffn_pre_norm", "mix_norm", "sgu_ln_gain", "sgu_ln_bias", "sgu_w", "sgu_b", "mla_q_gain", "mla_k_gain",
              "xmem_norm", "xmem_mem_norm", "xmem_q_gain", "xmem_k_gain", "ffn_post_norm")
WEIGHTS = ("ffn_pre_norm", "ffn_pre_w_gu", "ffn_pre_w_down", "mix_norm", "sbg_w_in", "sgu_ln_gain", "sgu_ln_bias", "sgu_w",
           "sgu_b", "sbg_w_out", "mla_w_in", "mla_q_lora_gain", "mla_kv_lora_gain", "mla_w_uq", "mla_w_ukv", "mla_q_gain",
           "mla_k_gain", "mla_w_out", "xmem_norm", "xmem_mem_norm", "xmem_wq", "xmem_wkv", "xmem_q_gain", "xmem_k_gain",
           "xmem_wo", "ffn_post_norm", "ffn_post_w_gu", "ffn_post_w_down")
INPUTS = ("x", "mem", "positions") + WEIGHTS + ("loss_target",) + tuple("m_" + n for n in WEIGHTS) + tuple(
    "v_" + n for n in WEIGHTS)


def _step(a):
    x, y, c, _ = _place()
    chip = 2 * x + y
    shard_shapes = [a[n].shape for n, _ in SHARDED]

    gathered = _gather_chips(_pack([a[n] for n, _ in SHARDED], PACK_COLS, PACK_ROW_MULTIPLE, BF16))
    parts = [_unpack(gathered[q], shard_shapes) for q in range(N_CHIPS)]
    w = {n: jnp.concatenate([parts[q][i] for q in range(N_CHIPS)], axis=ax) for i, (n, ax) in enumerate(SHARDED)}
    gains = jnp.zeros((8, LANE), F32)
    for r, n in enumerate(LORA_GAINS):
        gains = gains.at[r, :a[n].shape[1]].set(a[n][0])
    gains = _gather_devices(gains)
    for r, n in enumerate(LORA_GAINS):
        w[n] = jnp.concatenate([gains[16 * q + r, :a[n].shape[1]] for q in range(N_CHIPS)])[None, :]
    for n in REPLICATED:
        w[n] = a[n]

    loss, dx, grads = _local_step(a["x"][0], a["mem"][0], a["positions"][0], a["loss_target"][0], w)
    loss = lax.psum(loss, ("x", "y", "c"))
    full = {n: jnp.stack(grads[n]).reshape(w[n].shape) for n in WEIGHTS}

    def cut(n, ax, q):
        size = full[n].shape[ax] // N_CHIPS
        return lax.slice_in_dim(full[n], q * size, (q + 1) * size, axis=ax)

    g = jnp.stack([_pack([cut(n, ax, q) for n, ax in SHARDED], PACK_COLS, PACK_ROW_MULTIPLE, F32) for q in range(N_CHIPS)])
    partial = _add_own_half(g, _swap_halves(g), c)
    reduced = _join_halves(_sum_slots(_swap_chips(partial), name="grad_sum_chips"))
    gw = dict(zip([n for n, _ in SHARDED], _unpack(reduced, shard_shapes)))

    small_names = REPLICATED + LORA_GAINS
    small = _pack([full[n] for n in small_names], LANE, 256, F32)
    rows = small.shape[0]
    summed = _sum_slots(_gather_devices(small).reshape(8, rows, LANE), name="grad_sum_devices")
    for n, val in zip(small_names, _unpack(summed, [full[n].shape for n in small_names])):
        if n in LORA_GAINS:
            size = a[n].shape[1]
            val = lax.dynamic_slice_in_dim(val, chip * size, size, axis=1)
        gw[n] = val

    upd = {n: _adamw(a[n], gw[n], a["m_" + n], a["v_" + n], name="adamw_" + n) for n in WEIGHTS}
    return (loss, dx[None], *[gw[n] for n in WEIGHTS], *[upd[n][0] for n in WEIGHTS], *[upd[n][1] for n in WEIGHTS],
            *[upd[n][2] for n in WEIGHTS])


def kernel(x, mem, positions, ffn_pre_norm, ffn_pre_w_gu, ffn_pre_w_down, mix_norm, sbg_w_in, sgu_ln_gain,
           sgu_ln_bias, sgu_w, sgu_b, sbg_w_out, mla_w_in, mla_q_lora_gain, mla_kv_lora_gain, mla_w_uq, mla_w_ukv,
           mla_q_gain, mla_k_gain, mla_w_out, xmem_norm, xmem_mem_norm, xmem_wq, xmem_wkv, xmem_q_gain, xmem_k_gain,
           xmem_wo, ffn_post_norm, ffn_post_w_gu, ffn_post_w_down, loss_target, m_ffn_pre_norm, m_ffn_pre_w_gu,
           m_ffn_pre_w_down, m_mix_norm, m_sbg_w_in, m_sgu_ln_gain, m_sgu_ln_bias, m_sgu_w, m_sgu_b, m_sbg_w_out,
           m_mla_w_in, m_mla_q_lora_gain, m_mla_kv_lora_gain, m_mla_w_uq, m_mla_w_ukv, m_mla_q_gain, m_mla_k_gain,
           m_mla_w_out, m_xmem_norm, m_xmem_mem_norm, m_xmem_wq, m_xmem_wkv, m_xmem_q_gain, m_xmem_k_gain,
           m_xmem_wo, m_ffn_post_norm, m_ffn_post_w_gu, m_ffn_post_w_down, v_ffn_pre_norm, v_ffn_pre_w_gu,
           v_ffn_pre_w_down, v_mix_norm, v_sbg_w_in, v_sgu_ln_gain, v_sgu_ln_bias, v_sgu_w, v_sgu_b, v_sbg_w_out,
           v_mla_w_in, v_mla_q_lora_gain, v_mla_kv_lora_gain, v_mla_w_uq, v_mla_w_ukv, v_mla_q_gain, v_mla_k_gain,
           v_mla_w_out, v_xmem_norm, v_xmem_mem_norm, v_xmem_wq, v_xmem_wkv, v_xmem_q_gain, v_xmem_k_gain,
           v_xmem_wo, v_ffn_post_norm, v_ffn_post_w_gu, v_ffn_post_w_down):
    given = locals()
    return _step({n: given[n] for n in INPUTS})
```

```python
import functools

import jax
import jax.numpy as jnp
from jax import lax
from jax.experimental import pallas as pl
from jax.experimental.pallas import tpu as pltpu

F32, BF16 = jnp.float32, jnp.bfloat16
LANE = 128
VMEM_LIMIT = 56 * 1024 * 1024
EPS = 1e-6
D_FF = 2816
SB_HEADS, SB_HD = 8, 64
SG_GROUPS, SG_GD, SG_CHUNK = 8, 64, 128
SB_W, SG_W = SB_HEADS * SB_HD, SG_GROUPS * SG_GD
MLA_HEADS, MLA_NOPE, MLA_ROPE, MLA_V = 16, 64, 32, 64
MLA_QK = MLA_NOPE + MLA_ROPE
MLA_QL, MLA_KVL = 512, 256
ROPE_THETA = 10000.0
MEM_HEADS, MEM_HD = 4, 256
SB_SCALE, MLA_SCALE, MEM_SCALE = SB_HD ** -0.5, MLA_QK ** -0.5, MEM_HD ** -0.5
ADAM_LR, ADAM_B1, ADAM_B2, ADAM_EPS, ADAM_WD, ADAM_STEP = 0.001, 0.9, 0.999, 1e-08, 0.01, 10
MESH = pl.DeviceIdType.MESH
ANY = pl.BlockSpec(memory_space=pl.ANY)


def _params(n_axes):
    return pltpu.CompilerParams(dimension_semantics=("arbitrary",) * n_axes, vmem_limit_bytes=VMEM_LIMIT)


MM_TILE_CAP = 1408
MM_VMEM_BUDGET = 40 * 1024 * 1024


def _tile(dim, cap):
    if dim <= cap:
        return dim
    best = max(t for t in range(LANE, cap + 1, LANE) if dim % t == 0)
    return best


def _mm(a, b, *, ta=False, tb=False, out_dtype=F32, scale=1.0, residual=None, bias=None, a_off=(0, 0), b_off=(0, 0),
        m=None, n=None, k=None, name):
    am, ak = (a.shape[1], a.shape[0]) if ta else a.shape
    bk, bn = (b.shape[1], b.shape[0]) if tb else b.shape
    M, N, K = m or am, n or bn, k or ak
    tm, tn = _tile(M, MM_TILE_CAP), _tile(N, MM_TILE_CAP)
    fixed = tm * tn * (4 + 2 * jnp.dtype(out_dtype).itemsize + (2 * residual.dtype.itemsize if residual is not None else 0))
    per_k = (tm * (2 * a.dtype.itemsize + 2) + tn * (2 * b.dtype.itemsize + 2))
    tk = _tile(K, max(LANE, (MM_VMEM_BUDGET - fixed) // per_k))
    nm, nn, nk = M // tm, N // tn, K // tk
    a_off = (a_off[0] // (tk if ta else tm), a_off[1] // (tm if ta else tk))
    b_off = (b_off[0] // (tn if tb else tk), b_off[1] // (tk if tb else tn))
    dims = (((0 if ta else 1,), (1 if tb else 0,)), ((), ()))

    def body(*refs):
        a_ref, b_ref = refs[0], refs[1]
        o_ref, acc_ref = refs[-2], refs[-1]
        kk = pl.program_id(2)

        @pl.when(kk == 0)
        def _():
            acc_ref[...] = jnp.zeros_like(acc_ref)

        acc_ref[...] += lax.dot_general(a_ref[...].astype(BF16), b_ref[...].astype(BF16), dims,
                                        preferred_element_type=F32)

        @pl.when(kk == nk - 1)
        def _():
            out = acc_ref[...] * scale
            for extra in refs[2:-2]:
                out = out + extra[...].astype(F32)
            o_ref[...] = out.astype(o_ref.dtype)

    (ao0, ao1), (bo0, bo1) = a_off, b_off
    a_spec = (pl.BlockSpec((tk, tm), lambda i, j, kk: (kk + ao0, i + ao1)) if ta
              else pl.BlockSpec((tm, tk), lambda i, j, kk: (i + ao0, kk + ao1)))
    b_spec = (pl.BlockSpec((tn, tk), lambda i, j, kk: (j + bo0, kk + bo1)) if tb
              else pl.BlockSpec((tk, tn), lambda i, j, kk: (kk + bo0, j + bo1)))
    o_spec = pl.BlockSpec((tm, tn), lambda i, j, kk: (i, j))
    ins, in_specs = [a, b], [a_spec, b_spec]
    if residual is not None:
        ins.append(residual)
        in_specs.append(o_spec)
    if bias is not None:
        ins.append(bias)
        in_specs.append(pl.BlockSpec((1, tn), lambda i, j, kk: (0, j)))
    return pl.pallas_call(
        body, name=name, grid=(nm, nn, nk), in_specs=in_specs, out_specs=o_spec,
        out_shape=jax.ShapeDtypeStruct((M, N), out_dtype),
        scratch_shapes=[pltpu.VMEM((tm, tn), F32)], compiler_params=_params(3))(*ins)


def _mm_swiglu(h, wgu, *, name):
    M, K = h.shape
    F = wgu.shape[1] // 2
    tm, tn, tk = _tile(M, 512), _tile(F, MM_TILE_CAP), _tile(K, 1024)
    nm, nf, nk = M // tm, F // tn, K // tk

    def body(h_ref, wg_ref, wu_ref, g_ref, u_ref, a_ref, accg, accu):
        kk = pl.program_id(2)

        @pl.when(kk == 0)
        def _():
            accg[...] = jnp.zeros_like(accg)
            accu[...] = jnp.zeros_like(accu)

        hb = h_ref[...]
        accg[...] += jnp.dot(hb, wg_ref[...], preferred_element_type=F32)
        accu[...] += jnp.dot(hb, wu_ref[...], preferred_element_type=F32)

        @pl.when(kk == nk - 1)
        def _():
            g, u = accg[...], accu[...]
            g_ref[...] = g.astype(BF16)
            u_ref[...] = u.astype(BF16)
            a_ref[...] = (g * jax.nn.sigmoid(g) * u).astype(BF16)

    o_spec = pl.BlockSpec((tm, tn), lambda i, j, kk: (i, j))
    shp = jax.ShapeDtypeStruct((M, F), BF16)
    return pl.pallas_call(
        body, name=name, grid=(nm, nf, nk),
        in_specs=[pl.BlockSpec((tm, tk), lambda i, j, kk: (i, kk)),
                  pl.BlockSpec((tk, tn), lambda i, j, kk: (kk, j)),
                  pl.BlockSpec((tk, tn), lambda i, j, kk: (kk, j + nf))],
        out_specs=[o_spec, o_spec, o_spec], out_shape=[shp, shp, shp],
        scratch_shapes=[pltpu.VMEM((tm, tn), F32), pltpu.VMEM((tm, tn), F32)],
        compiler_params=_params(3))(h, wgu, wgu)


def _mm_dswiglu(dy, wd, gate, up, *, scale, name):
    M, K = dy.shape
    F = wd.shape[0]
    tm, tn, tk = _tile(M, 512), _tile(F, MM_TILE_CAP), _tile(K, 1024)
    nm, nf, nk = M // tm, F // tn, K // tk

    def body(dy_ref, wd_ref, g_ref, u_ref, dg_ref, du_ref, acc):
        kk = pl.program_id(2)

        @pl.when(kk == 0)
        def _():
            acc[...] = jnp.zeros_like(acc)

        acc[...] += lax.dot_general(dy_ref[...].astype(BF16), wd_ref[...], (((1,), (1,)), ((), ())),
                                    preferred_element_type=F32)

        @pl.when(kk == nk - 1)
        def _():
            da = acc[...] * scale
            g, u = g_ref[...].astype(F32), u_ref[...].astype(F32)
            sg = jax.nn.sigmoid(g)
            du_ref[...] = (da * g * sg).astype(BF16)
            dg_ref[...] = (da * u * sg * (1.0 + g * (1.0 - sg))).astype(BF16)

    o_spec = pl.BlockSpec((tm, tn), lambda i, j, kk: (i, j))
    shp = jax.ShapeDtypeStruct((M, F), BF16)
    return pl.pallas_call(
        body, name=name, grid=(nm, nf, nk),
        in_specs=[pl.BlockSpec((tm, tk), lambda i, j, kk: (i, kk)),
                  pl.BlockSpec((tn, tk), lambda i, j, kk: (j, kk)), o_spec, o_spec],
        out_specs=[o_spec, o_spec], out_shape=[shp, shp],
        scratch_shapes=[pltpu.VMEM((tm, tn), F32)], compiler_params=_params(3))(dy, wd, gate, up)


HEAD_ROWS = 1024


def _row_tile(rows, cap):
    t = cap
    while t >= 8:
        if rows % t == 0:
            return t
        t //= 2
    return rows


def _rowwise(fn, rows, consts, outs, sums=(), hsums=(), *, heads=None, tm=256, name):
    rows = [r if isinstance(r, tuple) else (r, r.shape[1], None) for r in rows]
    S = rows[0][0].shape[0]
    tm = _row_tile(S, tm)
    nh = heads or 1
    n_r, n_c, n_o, n_h, n_s = len(rows), len(consts), len(outs), len(hsums), len(sums)

    def body(*refs):
        r = [x[...] for x in refs[:n_r]]
        c = [x[...] for x in refs[n_r:n_r + n_c]]
        o_refs = refs[n_r + n_c:n_r + n_c + n_o]
        h_refs = refs[n_r + n_c + n_o:n_r + n_c + n_o + n_h]
        s_refs = refs[n_r + n_c + n_o + n_h:]
        res = fn(*r, *c)
        res = res if isinstance(res, (tuple, list)) else (res,)
        for ref, val in zip(o_refs, res[:n_o]):
            ref[...] = val.astype(ref.dtype)
        if n_h:
            @pl.when(pl.program_id(1) == 0)
            def _():
                for ref in h_refs:
                    ref[...] = jnp.zeros_like(ref)
            for ref, val in zip(h_refs, res[n_o:n_o + n_h]):
                ref[...] += val
        if n_s:
            @pl.when((pl.program_id(0) == 0) & (pl.program_id(1) == 0))
            def _():
                for ref in s_refs:
                    ref[...] = jnp.zeros_like(ref)
            for ref, val in zip(s_refs, res[n_o + n_h:]):
                ref[...] += val

    def col(colfn):
        return (lambda i, h: (i, 0)) if colfn is None else (lambda i, h: (i, colfn(h)))

    in_specs = [pl.BlockSpec((tm, w), col(cf)) for _, w, cf in rows]
    in_specs += [pl.BlockSpec(a.shape, lambda i, h, nd=a.ndim: (0,) * nd) for a in consts]
    out_specs = [pl.BlockSpec((tm, w // nh), (lambda i, h: (i, h)) if heads else (lambda i, h: (i, 0))) for w, _ in outs]
    out_specs += [pl.BlockSpec((tm, w), lambda i, h: (i, 0)) for w in hsums]
    out_specs += [pl.BlockSpec(sh, lambda i, h, nd=len(sh): (0,) * nd) for sh in sums]
    out_shape = [jax.ShapeDtypeStruct((S, w), dt) for w, dt in outs]
    out_shape += [jax.ShapeDtypeStruct((S, w), F32) for w in hsums]
    out_shape += [jax.ShapeDtypeStruct(sh, F32) for sh in sums]
    return pl.pallas_call(body, name=name, grid=(S // tm, nh), in_specs=in_specs, out_specs=out_specs,
                          out_shape=out_shape, compiler_params=_params(2))(*[a for a, _, _ in rows], *consts)


def _rms(x, width=None):
    width = width or x.shape[-1]
    return lax.rsqrt(jnp.sum(x * x, axis=-1, keepdims=True) * (1.0 / width) + EPS)


def _rmsnorm_fwd(x, g, width=None):
    return x * _rms(x, width) * g


def _rmsnorm_bwd(dy, x, g, width=None):
    width = width or x.shape[-1]
    r = _rms(x, width)
    xn = x * r
    dxn = dy * g
    dx = r * (dxn - xn * (jnp.sum(dxn * xn, axis=-1, keepdims=True) * (1.0 / width)))
    return dx, jnp.sum(dy * xn, axis=0, keepdims=True)


def _norm_rows(x, g, *, name, out_dtype=BF16):
    D = x.shape[1]
    return _rowwise(lambda xv, gv: _rmsnorm_fwd(xv.astype(F32), gv), [x], [g.reshape(1, D)], [(D, out_dtype)],
                    name=name)[0]


def _norm_rows_bwd(dh, x, g, dres, *, name):
    D = x.shape[1]

    def fn(dhv, xv, *rest):
        dx, dg = _rmsnorm_bwd(dhv.astype(F32), xv, rest[-1])
        return (dx + rest[0] if dres is not None else dx), dg

    rows = [dh, x] + ([dres] if dres is not None else [])
    return _rowwise(fn, rows, [g.reshape(1, D)], [(D, F32)], [(1, D)], name=name)


def _softplus(z):
    return jnp.maximum(z, 0.0) + jnp.log(1.0 + jnp.exp(-jnp.abs(z)))


def _running_sum(v, u, split=True):
    hi = v.astype(BF16)
    out = jnp.dot(hi, u, preferred_element_type=F32)
    if split:
        out = out + jnp.dot((v - hi.astype(F32)).astype(BF16), u, preferred_element_type=F32)
    return out


def _triangle(tk, inclusive_prefix):
    j, s = lax.broadcasted_iota(jnp.int32, (tk, tk), 0), lax.broadcasted_iota(jnp.int32, (tk, tk), 1)
    return ((j <= s) if inclusive_prefix else (j > s)).astype(BF16)


def _nt(a, b):
    return lax.dot_general(a, b, (((1,), (1,)), ((), ())), preferred_element_type=F32)


def _tn(a, b):
    return lax.dot_general(a, b, (((0,), (0,)), ((), ())), preferred_element_type=F32)


ATT_TQ, ATT_TK = 512, 512
SB_SUB = 256


def _attn_fwd(q, k, v, *, sb, causal, heads, dq, dv, kcol=None, vcol=None, sum_lane=None, name):
    S, Sk = q.shape[0], k.shape[0]
    tq, tk = min(ATT_TQ, S), min(ATT_TK, Sk)
    assert tq == tk or not causal
    sub = min(SB_SUB, tk) if sb else tk
    nsub = tk // sub
    kcol = kcol or (lambda h: h)
    vcol = vcol or (lambda h: h)

    def body(*refs):
        if sb:
            q_ref, k_ref, v_ref, u_ref, o_ref, lse_ref, acc_ref, r_ref = refs
            r_ref[...] = jnp.zeros_like(r_ref)
        else:
            q_ref, k_ref, v_ref, o_ref, lse_ref, acc_ref, m_ref, l_ref = refs
            m_ref[...] = jnp.full_like(m_ref, -1e30)
            l_ref[...] = jnp.zeros_like(l_ref)
        first_row = pl.program_id(1) * tq
        qb = q_ref[...]
        acc_ref[...] = jnp.zeros_like(acc_ref)
        nblk = (first_row + tq + tk - 1) // tk if causal else Sk // tk
        nfull = (first_row + (0 if sb else 1)) // tk if causal else nblk

        def piece(off, masked):
            kb, vb = k_ref[pl.ds(off, sub), :], v_ref[pl.ds(off, sub), :]
            s = _nt(qb, kb)
            sp = _softplus(s)
            if masked:
                valid = (off + lax.broadcasted_iota(jnp.int32, (tq, sub), 1)
                         < first_row + lax.broadcasted_iota(jnp.int32, (tq, sub), 0))
            ls = jnp.where(valid, -sp, 0.0) if masked else -sp
            w = jnp.exp(s - sp + r_ref[...] + _running_sum(ls, u_ref[...]))
            if masked:
                w = jnp.where(valid, w, 0.0)
            acc_ref[...] += jnp.dot(w.astype(BF16), vb, preferred_element_type=F32)
            r_ref[...] += jnp.sum(ls, axis=1, keepdims=True)

        def tile(jj, masked):
            for cidx in reversed(range(nsub)):
                off = pl.multiple_of(jj * tk + cidx * sub, sub)
                if masked and nsub > 1:
                    @pl.when(off < first_row + tq)
                    def _():
                        piece(off, True)
                else:
                    piece(off, masked)

        def sweep(lo, hi, masked):
            def step(t, carry):
                tile(hi - 1 - t, masked)
                return carry
            lax.fori_loop(0, hi - lo, step, 0)

        def scores(jj):
            return _nt(qb, k_ref[pl.ds(pl.multiple_of(jj * tk, tk), tk), :])

        def softmax_tile(jj, s, masked):
            off = pl.multiple_of(jj * tk, tk)
            vb = v_ref[pl.ds(off, tk), :]
            if masked:
                qpos = first_row + lax.broadcasted_iota(jnp.int32, (tq, tk), 0)
                kpos = off + lax.broadcasted_iota(jnp.int32, (tq, tk), 1)
                s = jnp.where(kpos <= qpos, s, -1e30)
            m_old = m_ref[...]
            m_new = jnp.maximum(m_old, jnp.max(s, axis=1, keepdims=True))
            p = jnp.exp(s - m_new)
            alpha = jnp.exp(m_old - m_new)
            if sum_lane is None:
                l_ref[...] = alpha * l_ref[...] + jnp.sum(p, axis=1, keepdims=True)
            acc_ref[...] = alpha * acc_ref[...] + jnp.dot(p.astype(BF16), vb, preferred_element_type=F32)
            m_ref[...] = m_new

        if sb:
            sweep(nfull, nblk, True)
            sweep(0, nfull, False)
        else:
            n_loop = nfull if causal else nblk - 1

            def step(t, s_cur):
                s_next = scores(jnp.minimum(t + 1, nblk - 1))
                softmax_tile(t, s_cur, False)
                return s_next

            softmax_tile(n_loop, lax.fori_loop(0, n_loop, step, scores(0)), causal)
        if sb:
            o_ref[...] = acc_ref[...]
            lse_ref[0] = r_ref[...]
        else:
            acc = acc_ref[...]
            l = l_ref[...] if sum_lane is None else acc[:, sum_lane:sum_lane + 1]
            o_ref[...] = acc / l
            lse_ref[0] = m_ref[...] + jnp.log(l)

    in_specs = [pl.BlockSpec((tq, dq), lambda h, i: (i, h)),
                pl.BlockSpec((Sk, dq), lambda h, i: (0, kcol(h))),
                pl.BlockSpec((Sk, dv), lambda h, i: (0, vcol(h)))]
    ins = [q, k, v]
    scratch = [pltpu.VMEM((tq, dv), F32), pltpu.VMEM((tq, 1), F32)]
    if sb:
        ins.append(_triangle(sub, inclusive_prefix=False))
        in_specs.append(pl.BlockSpec((sub, sub), lambda h, i: (0, 0)))
    else:
        scratch.append(pltpu.VMEM((tq, 1), F32))
    out_specs = [pl.BlockSpec((tq, dv), lambda h, i: (i, h)), pl.BlockSpec((1, tq, 1), lambda h, i: (h, i, 0))]
    out_shape = [jax.ShapeDtypeStruct((S, heads * dv), F32), jax.ShapeDtypeStruct((heads, S, 1), F32)]
    return pl.pallas_call(body, name=name, grid=(heads, S // tq), in_specs=in_specs, out_specs=out_specs,
                          out_shape=out_shape, scratch_shapes=scratch, compiler_params=_params(2))(*ins)


def _attn_bwd(q, k, v, o, do, lse, *, sb, causal, heads, dq, dv, kcol=None, vcol=None, name):
    S, Sk = q.shape[0], k.shape[0]
    tq, tk = min(ATT_TQ, S), min(ATT_TK, Sk)
    sub = min(SB_SUB, tk) if sb else tk
    nsub = tk // sub
    nq = S // tq
    kcol = kcol or (lambda h: h)
    vcol = vcol or (lambda h: h)

    def body(*refs):
        if sb:
            q_ref, k_ref, v_ref, o_ref, do_ref, lse_ref, u_ref, dq_ref, dk_ref, dv_ref, acc_ref, r_ref, re_ref = refs
            r_ref[...] = jnp.zeros_like(r_ref)
            re_ref[...] = jnp.zeros_like(re_ref)
        else:
            q_ref, k_ref, v_ref, o_ref, do_ref, lse_ref, dq_ref, dk_ref, dv_ref, acc_ref = refs
        first_row = pl.program_id(1) * tq

        @pl.when(first_row == 0)
        def _():
            dk_ref[...] = jnp.zeros_like(dk_ref)
            dv_ref[...] = jnp.zeros_like(dv_ref)

        qb = q_ref[...]
        dof = do_ref[...].astype(F32)
        dob = dof.astype(BF16)
        if not sb:
            dlt = jnp.sum(dof * o_ref[...], axis=1, keepdims=True)
        acc_ref[...] = jnp.zeros_like(acc_ref)
        nblk = (first_row + tq + tk - 1) // tk if causal else Sk // tk
        nfull = (first_row + (0 if sb else 1)) // tk if causal else nblk

        def piece(off, masked):
            kb, vb = k_ref[pl.ds(off, sub), :], v_ref[pl.ds(off, sub), :]
            s = _nt(qb, kb)
            dp = _nt(dob, vb)
            if masked:
                qpos = first_row + lax.broadcasted_iota(jnp.int32, (tq, sub), 0)
                kpos = off + lax.broadcasted_iota(jnp.int32, (tq, sub), 1)
                valid = (kpos < qpos) if sb else (kpos <= qpos)
            if sb:
                u = u_ref[...]
                sp = _softplus(s)
                ls = jnp.where(valid, -sp, 0.0) if masked else -sp
                lb = s - sp
                w = jnp.exp(lb + (lse_ref[0] - (r_ref[...] + _running_sum(ls, u))))
                if masked:
                    w = jnp.where(valid, w, 0.0)
                e = dp * w
                ds = e - jnp.exp(lb) * (re_ref[...] + _running_sum(e, u, split=False))
                if masked:
                    ds = jnp.where(valid, ds, 0.0)
                r_ref[...] += jnp.sum(ls, axis=1, keepdims=True)
                re_ref[...] += jnp.sum(e, axis=1, keepdims=True)
            else:
                w = jnp.exp(s - lse_ref[0])
                if masked:
                    w = jnp.where(valid, w, 0.0)
                ds = w * (dp - dlt)
            dsb = ds.astype(BF16)
            dv_ref[pl.ds(off, sub), :] += _tn(w.astype(BF16), dob)
            dk_ref[pl.ds(off, sub), :] += _tn(dsb, qb)
            acc_ref[...] += jnp.dot(dsb, kb, preferred_element_type=F32)

        def tile(jj, masked):
            for cidx in range(nsub):
                off = pl.multiple_of(jj * tk + cidx * sub, sub)
                if masked and nsub > 1:
                    @pl.when(off < first_row + tq)
                    def _():
                        piece(off, True)
                else:
                    piece(off, masked)

        def sweep(lo, hi, masked):
            def step(t, carry):
                tile(lo + t, masked)
                return carry
            lax.fori_loop(0, hi - lo, step, 0)

        sweep(0, nfull, False)
        if causal:
            sweep(nfull, nblk, True)
        dq_ref[...] = acc_ref[...]

    ins = [q, k, v, o, do]
    in_specs = [pl.BlockSpec((tq, dq), lambda h, i: (i, h)),
                pl.BlockSpec((Sk, dq), lambda h, i: (0, kcol(h))),
                pl.BlockSpec((Sk, dv), lambda h, i: (0, vcol(h))),
                pl.BlockSpec((tq, dv), lambda h, i: (i, h)),
                pl.BlockSpec((tq, dv), lambda h, i: (i, h))]
    scratch = [pltpu.VMEM((tq, dq), F32)]
    ins.append(lse)
    in_specs.append(pl.BlockSpec((1, tq, 1), lambda h, i: (h, i, 0)))
    if sb:
        ins.append(_triangle(sub, inclusive_prefix=True))
        in_specs.append(pl.BlockSpec((sub, sub), lambda h, i: (0, 0)))
        scratch += [pltpu.VMEM((tq, 1), F32), pltpu.VMEM((tq, 1), F32)]
    out_specs = [pl.BlockSpec((tq, dq), lambda h, i: (i, h)),
                 pl.BlockSpec((Sk, dq), lambda h, i: (0, h)),
                 pl.BlockSpec((Sk, dv), lambda h, i: (0, h))]
    out_shape = [jax.ShapeDtypeStruct((S, heads * dq), F32), jax.ShapeDtypeStruct((Sk, heads * dq), F32),
                 jax.ShapeDtypeStruct((Sk, heads * dv), F32)]
    return pl.pallas_call(body, name=name, grid=(heads, nq), in_specs=in_specs, out_specs=out_specs,
                          out_shape=out_shape, scratch_shapes=scratch, compiler_params=_params(2))(*ins)


GELU_C = 0.7978845608028654
assert 2 * SG_GD == LANE and SG_CHUNK == LANE


def _gelu(z):
    t = jnp.tanh(GELU_C * (z + 0.044715 * z * z * z))
    return 0.5 * z * (1.0 + t), t


def _gelu_grad(z, t):
    return 0.5 * (1.0 + t) + 0.5 * z * (1.0 - t * t) * GELU_C * (1.0 + 3.0 * 0.044715 * z * z)


def _layernorm_parts(g):
    d = g - jnp.mean(g, axis=-1, keepdims=True)
    rstd = lax.rsqrt(jnp.mean(d * d, axis=-1, keepdims=True) + EPS)
    return d * rstd, rstd


def _gelu_ln(z, gain, bias, *, name):
    def fn(zv, gn, bs):
        a, _ = _gelu(zv)
        y, _ = _layernorm_parts(a[:, SG_W:])
        return a[:, :SG_W], y * gn + bs

    return _rowwise(fn, [z], [gain.reshape(1, SG_W), bias.reshape(1, SG_W)], [(SG_W, F32), (SG_W, BF16)], name=name)


def _gelu_ln_bwd(z, du, dgl, gain, *, name):
    def fn(zv, duv, dglv, gn):
        a, t = _gelu(zv)
        y, rstd = _layernorm_parts(a[:, SG_W:])
        dy = dglv * gn
        dgg = rstd * (dy - jnp.mean(dy, axis=-1, keepdims=True) - y * jnp.mean(dy * y, axis=-1, keepdims=True))
        dz = jnp.concatenate([duv, dgg], axis=1) * _gelu_grad(zv, t)
        return dz, jnp.sum(dglv * y, axis=0, keepdims=True), jnp.sum(dglv, axis=0, keepdims=True)

    return _rowwise(fn, [z, du, dgl], [gain.reshape(1, SG_W)], [(2 * SG_W, BF16)], [(1, SG_W), (1, SG_W)], name=name)


def _sg_masks():
    tri = lax.broadcasted_iota(jnp.int32, (SG_CHUNK, SG_CHUNK), 0) >= lax.broadcasted_iota(jnp.int32, (SG_CHUNK, SG_CHUNK), 1)
    first = lax.broadcasted_iota(jnp.int32, (SG_CHUNK, LANE), 1) < SG_GD
    return tri, first


def _spatial(gl, u, w, bt, *, name):
    S = gl.shape[0]
    tm = _row_tile(S, 512)
    nch = tm // SG_CHUNK

    def body(gl_ref, u_ref, w_ref, bt_ref, o_ref):
        tri, first = _sg_masks()
        for p in range(SG_W // LANE):
            cols = slice(p * LANE, (p + 1) * LANE)
            wa = jnp.where(tri, w_ref[2 * p], 0.0).astype(BF16)
            wb = jnp.where(tri, w_ref[2 * p + 1], 0.0).astype(BF16)
            for ci in range(nch):
                rws = slice(ci * SG_CHUNK, (ci + 1) * SG_CHUNK)
                g = gl_ref[rws, cols]
                zero = jnp.zeros_like(g)
                mixed = (jnp.dot(wa, jnp.where(first, g, zero), preferred_element_type=F32)
                         + jnp.dot(wb, jnp.where(first, zero, g), preferred_element_type=F32) + bt_ref[:, cols])
                o_ref[rws, cols] = u_ref[rws, cols] * mixed

    row = pl.BlockSpec((tm, SG_W), lambda i: (i, 0))
    return pl.pallas_call(
        body, name=name, grid=(S // tm,),
        in_specs=[row, row, pl.BlockSpec(w.shape, lambda i: (0, 0, 0)), pl.BlockSpec(bt.shape, lambda i: (0, 0))],
        out_specs=row, out_shape=jax.ShapeDtypeStruct((S, SG_W), F32), compiler_params=_params(1))(gl, u, w, bt)


def _spatial_bwd(d_o, gl, u, w, bt, *, name):
    S = gl.shape[0]
    tm = _row_tile(S, 512)
    nch = tm // SG_CHUNK
    nsteps = S // tm

    def body(do_ref, gl_ref, u_ref, w_ref, bt_ref, du_ref, dgl_ref, dw_ref, db_ref, dbt_ref):
        tri, first = _sg_masks()
        step = pl.program_id(0)

        @pl.when(step == 0)
        def _():
            dw_ref[...] = jnp.zeros_like(dw_ref)
            dbt_ref[...] = jnp.zeros_like(dbt_ref)

        for p in range(SG_W // LANE):
            cols = slice(p * LANE, (p + 1) * LANE)
            wa = jnp.where(tri, w_ref[2 * p], 0.0).astype(BF16)
            wb = jnp.where(tri, w_ref[2 * p + 1], 0.0).astype(BF16)
            for ci in range(nch):
                rws = slice(ci * SG_CHUNK, (ci + 1) * SG_CHUNK)
                g = gl_ref[rws, cols]
                zero = jnp.zeros_like(g)
                mixed = (jnp.dot(wa, jnp.where(first, g, zero), preferred_element_type=F32)
                         + jnp.dot(wb, jnp.where(first, zero, g), preferred_element_type=F32) + bt_ref[:, cols])
                dov = do_ref[rws, cols]
                du_ref[rws, cols] = dov * mixed
                dm = dov * u_ref[rws, cols]
                dbt_ref[:, cols] += dm
                dma = jnp.where(first, dm, 0.0).astype(BF16)
                dmb = jnp.where(first, 0.0, dm).astype(BF16)
                dw_ref[2 * p] += jnp.where(tri, _nt(dma, g), 0.0)
                dw_ref[2 * p + 1] += jnp.where(tri, _nt(dmb, g), 0.0)
                dgl_ref[rws, cols] = _tn(wa, dma) + _tn(wb, dmb)

        @pl.when(step == nsteps - 1)
        def _():
            lane = lax.broadcasted_iota(jnp.int32, (SG_CHUNK, LANE), 1)
            acc = jnp.zeros((SG_CHUNK, LANE), F32)
            for p in range(SG_W // LANE):
                blk = dbt_ref[:, p * LANE:(p + 1) * LANE]
                sa = jnp.sum(jnp.where(first, blk, 0.0), axis=1, keepdims=True)
                sb_ = jnp.sum(jnp.where(first, 0.0, blk), axis=1, keepdims=True)
                acc = acc + jnp.where(lane == 2 * p, sa, 0.0) + jnp.where(lane == 2 * p + 1, sb_, 0.0)
            db_ref[...] = acc

    row = pl.BlockSpec((tm, SG_W), lambda i: (i, 0))
    return pl.pallas_call(
        body, name=name, grid=(nsteps,),
        in_specs=[row, row, row, pl.BlockSpec(w.shape, lambda i: (0, 0, 0)), pl.BlockSpec(bt.shape, lambda i: (0, 0))],
        out_specs=[row, row, pl.BlockSpec(w.shape, lambda i: (0, 0, 0)), pl.BlockSpec((SG_CHUNK, LANE), lambda i: (0, 0))],
        out_shape=[jax.ShapeDtypeStruct((S, SG_W), F32), jax.ShapeDtypeStruct((S, SG_W), F32),
                   jax.ShapeDtypeStruct(w.shape, F32), jax.ShapeDtypeStruct((SG_CHUNK, LANE), F32)],
        scratch_shapes=[pltpu.VMEM((SG_CHUNK, SG_W), F32)], compiler_params=_params(1))(d_o, gl, u, w, bt)


ROPE_HALF = MLA_ROPE // 2
KR_COL = (MLA_QL + MLA_KVL) // LANE
MLA_IN_PAD = MLA_QL + MLA_KVL + LANE


def _rope_tables(positions):
    inv_freq = ROPE_THETA ** (-jnp.arange(ROPE_HALF, dtype=F32) / ROPE_HALF)
    ang = positions.astype(F32)[:, None] * inv_freq
    cos, sin = jnp.cos(ang), jnp.sin(ang)
    S = positions.shape[0]
    z16, tail = jnp.zeros((S, ROPE_HALF), F32), jnp.zeros((S, LANE - MLA_QK), F32)
    ones = jnp.ones((S, MLA_NOPE), F32)
    zeros = jnp.zeros((S, MLA_NOPE), F32)
    return (jnp.concatenate([ones, cos, cos, tail], axis=1), jnp.concatenate([zeros, z16, sin, tail], axis=1),
            jnp.concatenate([zeros, -sin, z16, tail], axis=1))


def _rope(x, cos, sa, sb):
    return x * cos + pltpu.roll(x, ROPE_HALF, 1) * sa + pltpu.roll(x, LANE - ROPE_HALF, 1) * sb


def _rope_t(dy, cos, sa, sb):
    return dy * cos + pltpu.roll(dy * sa, LANE - ROPE_HALF, 1) + pltpu.roll(dy * sb, ROPE_HALF, 1)


def _mla_lora(P, qlg, kvlg, *, name):
    def fn(pv, a, b):
        return _rmsnorm_fwd(pv[:, :MLA_QL], a), _rmsnorm_fwd(pv[:, MLA_QL:MLA_QL + MLA_KVL], b)

    return _rowwise(fn, [P], [qlg.reshape(1, MLA_QL), kvlg.reshape(1, MLA_KVL)], [(MLA_QL, BF16), (MLA_KVL, BF16)], name=name)


def _mla_lora_bwd(dcq, dckv, dkr, P, qlg, kvlg, *, name):
    def fn(d1, d2, d3, pv, a, b):
        x1, g1 = _rmsnorm_bwd(d1, pv[:, :MLA_QL], a)
        x2, g2 = _rmsnorm_bwd(d2, pv[:, MLA_QL:MLA_QL + MLA_KVL], b)
        return jnp.concatenate([x1, x2, d3], axis=1), g1, g2

    return _rowwise(fn, [dcq, dckv, dkr, P], [qlg.reshape(1, MLA_QL), kvlg.reshape(1, MLA_KVL)], [(MLA_IN_PAD, BF16)],
                    [(1, MLA_QL), (1, MLA_KVL)], name=name)


def _mla_qk(q_pre, k_pre, P, tabs, qg, kg, *, name):
    def fn(qp, kp, kr, c, a, b, qgv, kgv):
        return (_rope(_rmsnorm_fwd(qp, qgv, MLA_QK), c, a, b) * MLA_SCALE,
                _rope(_rmsnorm_fwd(kp + kr, kgv, MLA_QK), c, a, b))

    hcol = lambda h: h
    rows = [(q_pre, LANE, hcol), (k_pre, LANE, hcol), (P, LANE, lambda h: KR_COL), *tabs]
    w = MLA_HEADS * LANE
    return _rowwise(fn, rows, [qg, kg], [(w, BF16), (w, BF16)], heads=MLA_HEADS, tm=HEAD_ROWS, name=name)


def _mla_qk_bwd(dq, dk, q_pre, k_pre, P, tabs, qg, kg, *, name):
    def fn(dqv, dkv, qp, kp, kr, c, a, b, qgv, kgv):
        dqp, dqg = _rmsnorm_bwd(_rope_t(dqv * MLA_SCALE, c, a, b), qp, qgv, MLA_QK)
        dkp, dkg = _rmsnorm_bwd(_rope_t(dkv, c, a, b), kp + kr, kgv, MLA_QK)
        lane = lax.broadcasted_iota(jnp.int32, (1, LANE), 1)
        return dqp, dkp, jnp.where((lane >= MLA_NOPE) & (lane < MLA_QK), dkp, 0.0), dqg, dkg

    hcol = lambda h: h
    rows = [(dq, LANE, hcol), (dk, LANE, hcol), (q_pre, LANE, hcol), (k_pre, LANE, hcol), (P, LANE, lambda h: KR_COL), *tabs]
    w = MLA_HEADS * LANE
    return _rowwise(fn, rows, [qg, kg], [(w, BF16), (w, BF16)], [(1, LANE), (1, LANE)], [LANE], heads=MLA_HEADS,
                    tm=HEAD_ROWS, name=name)


def _head_norm(x, g, *, heads, width, colfn=None, scale=1.0, name):
    return _rowwise(lambda xv, gv: _rmsnorm_fwd(xv, gv) * scale, [(x, width, colfn or (lambda h: h))],
                    [g.reshape(1, width)], [(heads * width, BF16)], heads=heads, tm=HEAD_ROWS, name=name)[0]


def _head_norm_bwd(dy, x, g, *, heads, width, colfn=None, scale=1.0, out_dtype, name):
    return _rowwise(lambda dv_, xv, gv: _rmsnorm_bwd(dv_ * scale, xv, gv),
                    [(dy, width, lambda h: h), (x, width, colfn or (lambda h: h))],
                    [g.reshape(1, width)], [(heads * width, out_dtype)], [(1, width)], heads=heads, tm=HEAD_ROWS,
                    name=name)


def _loss_grad(y, tgt, *, name):
    D = y.shape[1]

    def fn(yv, tv):
        d = yv - tv
        return d * (1.0 / D), jnp.sum(d * d, axis=0, keepdims=True) * (0.5 / D)

    dy, part = _rowwise(fn, [y, tgt], [], [(D, F32)], [(1, D)], name=name)
    return jnp.sum(part), dy


def _adamw(w, g, m, v, *, name):
    shape = w.shape
    two_d = (-1, shape[-1])

    def fn(wv, gv, mv, vv):
        m2 = ADAM_B1 * mv + (1.0 - ADAM_B1) * gv
        v2 = ADAM_B2 * vv + (1.0 - ADAM_B2) * (gv * gv)
        m_hat = m2 / (1.0 - ADAM_B1 ** ADAM_STEP)
        v_hat = v2 / (1.0 - ADAM_B2 ** ADAM_STEP)
        return -ADAM_LR * (m_hat / (jnp.sqrt(v_hat) + ADAM_EPS) + ADAM_WD * wv), m2, v2

    outs = _rowwise(fn, [t.reshape(two_d) for t in (w, g, m, v)], [], [(shape[-1], F32)] * 3, name=name)
    return [o.reshape(shape) for o in outs]


def _pad_cols(w, heads, hd):
    k = w.shape[0]
    return jnp.pad(w.reshape(k, heads, hd), ((0, 0), (0, 0), (0, LANE - hd))).reshape(k, heads * LANE)


def _unpad_cols(w, heads, hd):
    k = w.shape[0]
    return w.reshape(k, heads, LANE)[:, :, :hd].reshape(k, heads * hd)


def _pad_rows(w, heads, hd):
    n = w.shape[1]
    return jnp.pad(w.reshape(heads, hd, n), ((0, 0), (0, LANE - hd), (0, 0))).reshape(heads * LANE, n)


def _unpad_rows(w, heads, hd):
    n = w.shape[1]
    return w.reshape(heads, LANE, n)[:, :hd, :].reshape(heads * hd, n)


def _ffn_fwd(x, g, wgu, wd, tag):
    h = _norm_rows(x, g, name=tag + "_norm")
    gate, up, act = _mm_swiglu(h, wgu, name=tag + "_gu")
    y = _mm(act, wd, scale=0.5, residual=x, name=tag + "_down")
    return y, (x, h, gate, up, act)


def _ffn_bwd(dy, saved, g, wgu, wd, tag):
    x, h, gate, up, act = saved
    F = wd.shape[0]
    dwd = _mm(act, dy, ta=True, scale=0.5, name=tag + "_dwd")
    dgate, dup = _mm_dswiglu(dy, wd, gate, up, scale=0.5, name=tag + "_dact")
    dh = _mm(dgate, wgu, tb=True, name=tag + "_dh_g")
    dh = _mm(dup, wgu, tb=True, b_off=(0, F), residual=dh, name=tag + "_dh_u")
    dwgu = jnp.concatenate([_mm(h, dgate, ta=True, name=tag + "_dwg"), _mm(h, dup, ta=True, name=tag + "_dwu")], axis=1)
    dx, dg = _norm_rows_bwd(dh, x, g, dy, name=tag + "_dnorm")
    return dx, dg, dwgu, dwd


def _even_weights(w_in, w_out):
    parts = [w_in[:, :SB_W] * SB_SCALE, w_in[:, SB_W:2 * SB_W], w_in[:, 2 * SB_W:3 * SB_W]]
    wqkv = jnp.concatenate([_pad_cols(p, SB_HEADS, SB_HD) for p in parts], axis=1)
    return wqkv, w_in[:, 3 * SB_W:], _pad_rows(w_out[:SB_W], SB_HEADS, SB_HD), w_out[SB_W:]


def _even_fwd(x, g, wts, ln_g, ln_b, sgu_w, bt, tag):
    wqkv, wz, wo_sb, wo_sg = wts
    h = _norm_rows(x, g, name=tag + "_norm")
    qkv = _mm(h, wqkv, out_dtype=BF16, name=tag + "_qkv")
    z = _mm(h, wz, name=tag + "_z")
    o_sb, tot = _attn_fwd(qkv, qkv, qkv, sb=True, causal=True, heads=SB_HEADS, dq=LANE, dv=LANE,
                          kcol=lambda hh: SB_HEADS + hh, vcol=lambda hh: 2 * SB_HEADS + hh, name=tag + "_sb")
    u, gl = _gelu_ln(z, ln_g, ln_b, name=tag + "_geluln")
    o_sg = _spatial(gl, u, sgu_w, bt, name=tag + "_sgu")
    y = _mm(o_sb, wo_sb, residual=x, name=tag + "_out_sb")
    y = _mm(o_sg, wo_sg, residual=y, name=tag + "_out_sg")
    return y, (x, h, qkv, z, o_sb, tot, u, gl, o_sg)


def _even_bwd(dy, saved, g, wts, ln_g, sgu_w, bt, tag):
    wqkv, wz, wo_sb, wo_sg = wts
    x, h, qkv, z, o_sb, tot, u, gl, o_sg = saved
    do_sb = _mm(dy, wo_sb, tb=True, name=tag + "_do_sb")
    do_sg = _mm(dy, wo_sg, tb=True, name=tag + "_do_sg")
    dwo = jnp.concatenate([_unpad_rows(_mm(o_sb, dy, ta=True, name=tag + "_dwo_sb"), SB_HEADS, SB_HD),
                           _mm(o_sg, dy, ta=True, name=tag + "_dwo_sg")], axis=0)
    dq, dk, dv = _attn_bwd(qkv, qkv, qkv, o_sb, do_sb, tot, sb=True, causal=True, heads=SB_HEADS, dq=LANE, dv=LANE,
                           kcol=lambda hh: SB_HEADS + hh, vcol=lambda hh: 2 * SB_HEADS + hh, name=tag + "_sb_bwd")
    du, dgl, dsgu_w, db_t = _spatial_bwd(do_sg, gl, u, sgu_w, bt, name=tag + "_sgu_bwd")
    dz, dln_g, dln_b = _gelu_ln_bwd(z, du, dgl, ln_g, name=tag + "_geluln_bwd")
    dh = _mm(dz, wz, tb=True, name=tag + "_dh_z")
    dws = []
    for i, (d, nm) in enumerate(((dq, "q"), (dk, "k"), (dv, "v"))):
        dh = _mm(d, wqkv, tb=True, b_off=(0, i * SB_HEADS * LANE), residual=dh, name=tag + "_dh_" + nm)
        dws.append(_unpad_cols(_mm(h, d, ta=True, scale=SB_SCALE if nm == "q" else 1.0, name=tag + "_dw_" + nm),
                               SB_HEADS, SB_HD))
    dws.append(_mm(h, dz, ta=True, name=tag + "_dw_z"))
    dx, dg = _norm_rows_bwd(dh, x, g, dy, name=tag + "_dnorm")
    return dx, dict(mix_norm=dg, sbg_w_in=jnp.concatenate(dws, axis=1), sgu_ln_gain=dln_g, sgu_ln_bias=dln_b,
                    sgu_w=dsgu_w, sgu_b=db_t[:, :SG_GROUPS].T, sbg_w_out=dwo)


def _mla_weights(w_in, w_uq, w_ukv, w_out, q_gain, k_gain):
    d = w_in.shape[0]
    lat = MLA_QL + MLA_KVL
    w_in_ext = jnp.concatenate([w_in[:, :lat], jnp.zeros((d, MLA_NOPE), w_in.dtype), w_in[:, lat:],
                                jnp.zeros((d, LANE - MLA_QK), w_in.dtype)], axis=1)
    kv = w_ukv.reshape(MLA_KVL, MLA_HEADS, MLA_NOPE + MLA_V)
    wk = _pad_cols(kv[:, :, :MLA_NOPE].reshape(MLA_KVL, -1), MLA_HEADS, MLA_NOPE)
    wv = _pad_cols(kv[:, :, MLA_NOPE:].reshape(MLA_KVL, -1), MLA_HEADS, MLA_V)
    pad_gain = lambda gn: jnp.pad(gn.reshape(1, MLA_QK), ((0, 0), (0, LANE - MLA_QK)))
    return (w_in_ext, _pad_cols(w_uq, MLA_HEADS, MLA_QK), wk, wv, _pad_rows(w_out, MLA_HEADS, MLA_V),
            pad_gain(q_gain), pad_gain(k_gain))


def _mla_fwd(x, g, wts, qlg, kvlg, tabs, tag):
    w_in, w_uq, wk, wv, w_out, qg, kg = wts
    h = _norm_rows(x, g, name=tag + "_norm")
    P = _mm(h, w_in, name=tag + "_in")
    cqn, ckvn = _mla_lora(P, qlg, kvlg, name=tag + "_lora")
    q_pre = _mm(cqn, w_uq, name=tag + "_uq")
    k_pre = _mm(ckvn, wk, name=tag + "_uk")
    ones_lane = jnp.tile((jnp.arange(LANE) == MLA_V).astype(F32), MLA_HEADS)[None, :]
    v = _mm(ckvn, wv, out_dtype=BF16, bias=ones_lane, name=tag + "_uv")
    q, k = _mla_qk(q_pre, k_pre, P, tabs, qg, kg, name=tag + "_qk")
    o, lse = _attn_fwd(q, k, v, sb=False, causal=True, heads=MLA_HEADS, dq=LANE, dv=LANE, sum_lane=MLA_V,
                       name=tag + "_attn")
    y = _mm(o, w_out, residual=x, name=tag + "_out")
    return y, (x, h, P, cqn, ckvn, q_pre, k_pre, q, k, v, o, lse)


def _mla_bwd(dy, saved, g, wts, qlg, kvlg, tabs, tag):
    w_in, w_uq, wk, wv, w_out, qg, kg = wts
    x, h, P, cqn, ckvn, q_pre, k_pre, q, k, v, o, lse = saved
    do = _mm(dy, w_out, tb=True, name=tag + "_do")
    dw_out = _unpad_rows(_mm(o, dy, ta=True, name=tag + "_dwo"), MLA_HEADS, MLA_V)
    dq, dk, dv = _attn_bwd(q, k, v, o, do, lse, sb=False, causal=True, heads=MLA_HEADS, dq=LANE, dv=LANE,
                           name=tag + "_attn_bwd")
    dq_pre, dk_pre, dkr, dqg, dkg = _mla_qk_bwd(dq, dk, q_pre, k_pre, P, tabs, qg, kg, name=tag + "_qk_bwd")
    dcqn = _mm(dq_pre, w_uq, tb=True, name=tag + "_dcq")
    dckvn = _mm(dk_pre, wk, tb=True, name=tag + "_dckv_k")
    dckvn = _mm(dv, wv, tb=True, residual=dckvn, name=tag + "_dckv_v")
    dw_uq = _unpad_cols(_mm(cqn, dq_pre, ta=True, name=tag + "_dwuq"), MLA_HEADS, MLA_QK)
    dwk = _unpad_cols(_mm(ckvn, dk_pre, ta=True, name=tag + "_dwk"), MLA_HEADS, MLA_NOPE)
    dwv = _unpad_cols(_mm(ckvn, dv, ta=True, name=tag + "_dwv"), MLA_HEADS, MLA_V)
    dw_ukv = jnp.concatenate([dwk.reshape(MLA_KVL, MLA_HEADS, MLA_NOPE), dwv.reshape(MLA_KVL, MLA_HEADS, MLA_V)],
                             axis=2).reshape(MLA_KVL, -1)
    dP, dqlg, dkvlg = _mla_lora_bwd(dcqn, dckvn, dkr, P, qlg, kvlg, name=tag + "_lora_bwd")
    dh = _mm(dP, w_in, tb=True, name=tag + "_dh")
    dw_in_ext = _mm(h, dP, ta=True, name=tag + "_dwin")
    lat = MLA_QL + MLA_KVL
    dw_in = jnp.concatenate([dw_in_ext[:, :lat], dw_in_ext[:, lat + MLA_NOPE:lat + MLA_QK]], axis=1)
    dx, dg = _norm_rows_bwd(dh, x, g, dy, name=tag + "_dnorm")
    return dx, dict(mix_norm=dg, mla_w_in=dw_in, mla_q_lora_gain=dqlg, mla_kv_lora_gain=dkvlg, mla_w_uq=dw_uq,
                    mla_w_ukv=dw_ukv, mla_q_gain=dqg[:, :MLA_QK], mla_k_gain=dkg[:, :MLA_QK], mla_w_out=dw_out)


def _xmem_fwd(x, mem, g, gm, wq, wkv, qg, kg, wo, tag):
    hq = _norm_rows(x, g, name=tag + "_norm")
    hm = _norm_rows(mem, gm, name=tag + "_mnorm")
    qp = _mm(hq, wq, name=tag + "_q")
    kv = _mm(hm, wkv, name=tag + "_kv")
    q = _head_norm(qp, qg, heads=MEM_HEADS, width=MEM_HD, scale=MEM_SCALE, name=tag + "_qn")
    kn = _head_norm(kv, kg, heads=MEM_HEADS, width=MEM_HD, colfn=lambda hh: 2 * hh, name=tag + "_kn")
    kvb = kv.astype(BF16)
    o, lse = _attn_fwd(q, kn, kvb, sb=False, causal=False, heads=MEM_HEADS, dq=MEM_HD, dv=MEM_HD,
                       vcol=lambda hh: 2 * hh + 1, name=tag + "_attn")
    y = _mm(o, wo, residual=x, name=tag + "_out")
    return y, (x, hq, hm, qp, kv, q, kn, kvb, o, lse)


def _xmem_bwd(dy, saved, mem, g, gm, wq, wkv, qg, kg, wo, tag):
    x, hq, hm, qp, kv, q, kn, kvb, o, lse = saved
    m = mem.shape[0]
    do = _mm(dy, wo, tb=True, name=tag + "_do")
    dwo = _mm(o, dy, ta=True, name=tag + "_dwo")
    dq, dk, dv = _attn_bwd(q, kn, kvb, o, do, lse, sb=False, causal=False, heads=MEM_HEADS, dq=MEM_HD, dv=MEM_HD,
                           vcol=lambda hh: 2 * hh + 1, name=tag + "_attn_bwd")
    dqp, dqg = _head_norm_bwd(dq, qp, qg, heads=MEM_HEADS, width=MEM_HD, scale=MEM_SCALE, out_dtype=BF16,
                              name=tag + "_qn_bwd")
    dkp, dkg = _head_norm_bwd(dk, kv, kg, heads=MEM_HEADS, width=MEM_HD, colfn=lambda hh: 2 * hh, out_dtype=F32,
                              name=tag + "_kn_bwd")
    dkv = jnp.concatenate([dkp.reshape(m, MEM_HEADS, MEM_HD), dv.reshape(m, MEM_HEADS, MEM_HD)], axis=2).reshape(m, -1)
    dwkv = _mm(hm, dkv, ta=True, name=tag + "_dwkv")
    dhm = _mm(dkv, wkv, tb=True, name=tag + "_dhm")
    _, dgm = _norm_rows_bwd(dhm, mem, gm, None, name=tag + "_dmnorm")
    dwq = _mm(hq, dqp, ta=True, name=tag + "_dwq")
    dhq = _mm(dqp, wq, tb=True, name=tag + "_dhq")
    dx, dg = _norm_rows_bwd(dhq, x, g, dy, name=tag + "_dnorm")
    return dx, dict(xmem_norm=dg, xmem_mem_norm=dgm, xmem_wq=dwq, xmem_wkv=dwkv, xmem_q_gain=dqg, xmem_k_gain=dkg,
                    xmem_wo=dwo)


def _local_step(x, mem, positions, tgt, w):
    tabs = _rope_tables(positions)
    even = _even_weights(w["sbg_w_in"][0], w["sbg_w_out"][0])
    mla = _mla_weights(w["mla_w_in"][0], w["mla_w_uq"][0], w["mla_w_ukv"][0], w["mla_w_out"][0], w["mla_q_gain"][0],
                       w["mla_k_gain"][0])
    bt = jnp.repeat(w["sgu_b"][0].T, SG_GD, axis=1)
    saved = []
    for l in range(2):
        t = f"l{l}"
        x, s_pre = _ffn_fwd(x, w["ffn_pre_norm"][l], w["ffn_pre_w_gu"][l], w["ffn_pre_w_down"][l], t + "_pre")
        if l == 0:
            x, s_mix = _even_fwd(x, w["mix_norm"][0], even, w["sgu_ln_gain"][0], w["sgu_ln_bias"][0], w["sgu_w"][0], bt,
                                 t + "_even")
        else:
            x, s_mix = _mla_fwd(x, w["mix_norm"][1], mla, w["mla_q_lora_gain"][0], w["mla_kv_lora_gain"][0], tabs,
                                t + "_mla")
        x, s_xm = _xmem_fwd(x, mem, w["xmem_norm"][l], w["xmem_mem_norm"][l], w["xmem_wq"][l], w["xmem_wkv"][l],
                            w["xmem_q_gain"][l], w["xmem_k_gain"][l], w["xmem_wo"][l], t + "_xm")
        x, s_post = _ffn_fwd(x, w["ffn_post_norm"][l], w["ffn_post_w_gu"][l], w["ffn_post_w_down"][l], t + "_post")
        saved.append((s_pre, s_mix, s_xm, s_post))
    loss, dx = _loss_grad(x, tgt, name="loss")
    grads = {}

    def put(name, l, val):
        grads.setdefault(name, {})[l] = val

    for l in (1, 0):
        t = f"l{l}"
        s_pre, s_mix, s_xm, s_post = saved[l]
        dx, dg, dwgu, dwd = _ffn_bwd(dx, s_post, w["ffn_post_norm"][l], w["ffn_post_w_gu"][l], w["ffn_post_w_down"][l],
                                     t + "_post")
        put("ffn_post_norm", l, dg), put("ffn_post_w_gu", l, dwgu), put("ffn_post_w_down", l, dwd)
        dx, gx = _xmem_bwd(dx, s_xm, mem, w["xmem_norm"][l], w["xmem_mem_norm"][l], w["xmem_wq"][l], w["xmem_wkv"][l],
                           w["xmem_q_gain"][l], w["xmem_k_gain"][l], w["xmem_wo"][l], t + "_xm")
        for k_, v_ in gx.items():
            put(k_, l, v_)
        if l == 0:
            dx, gm = _even_bwd(dx, s_mix, w["mix_norm"][0], even, w["sgu_ln_gain"][0], w["sgu_w"][0], bt, t + "_even")
        else:
            dx, gm = _mla_bwd(dx, s_mix, w["mix_norm"][1], mla, w["mla_q_lora_gain"][0], w["mla_kv_lora_gain"][0], tabs,
                              t + "_mla")
        for k_, v_ in gm.items():
            put(k_, l if k_ == "mix_norm" else 0, v_)
        dx, dg, dwgu, dwd = _ffn_bwd(dx, s_pre, w["ffn_pre_norm"][l], w["ffn_pre_w_gu"][l], w["ffn_pre_w_down"][l],
                                     t + "_pre")
        put("ffn_pre_norm", l, dg), put("ffn_pre_w_gu", l, dwgu), put("ffn_pre_w_down", l, dwd)
    return loss, dx, {k_: [v_[l] for l in sorted(v_)] for k_, v_ in grads.items()}


N_CHIPS = 4
PACK_COLS = 1024
PACK_ROW_MULTIPLE = 512


def _place():
    x, y, c = lax.axis_index("x"), lax.axis_index("y"), lax.axis_index("c")
    return x, y, c, [(1 - x, y), (x, 1 - y), (1 - x, 1 - y)]


def _gather_chips(shard):
    R, C = shard.shape
    Rh = R // 2

    def body(x_ref, out_ref, send_sems, recv_sems, local_sem):
        x, y, c, chips = _place()
        me = 2 * x + y

        def half(chip, core):
            return out_ref.at[chip, pl.ds(core * Rh, Rh), :]

        def copy(k, chip, core, to, src=None):
            return pltpu.make_async_remote_copy(src_ref=half(chip, core) if src is None else src, dst_ref=half(chip, core),
                                                send_sem=send_sems.at[k], recv_sem=recv_sems.at[k], device_id=to,
                                                device_id_type=MESH)

        mine = pltpu.make_async_copy(x_ref, out_ref.at[me], local_sem)
        mine.start()
        first = [copy(j, me, c, (qx, qy, c), src=x_ref.at[pl.ds(c * Rh, Rh), :]) for j, (qx, qy) in enumerate(chips)]
        for cp in first:
            cp.start()
        passed = [copy(3 + j, 2 * qx + qy, c, (x, y, 1 - c)) for j, (qx, qy) in enumerate(chips)]
        for j, (qx, qy) in enumerate(chips):
            copy(j, 2 * qx + qy, c, (x, y, c)).wait_recv()
            passed[j].start()
        for j, (qx, qy) in enumerate(chips):
            copy(3 + j, 2 * qx + qy, 1 - c, (x, y, c)).wait_recv()
        for cp in first + passed:
            cp.wait_send()
        mine.wait()

    return pl.pallas_call(
        body, name="gather_weights", out_shape=jax.ShapeDtypeStruct((N_CHIPS, R, C), shard.dtype),
        in_specs=[ANY], out_specs=ANY,
        scratch_shapes=[pltpu.SemaphoreType.DMA((6,)), pltpu.SemaphoreType.DMA((6,)), pltpu.SemaphoreType.DMA])(shard)


def _gather_devices(block):
    M, N = block.shape

    def body(x_ref, out_ref, send_sems, recv_sems, local_sem):
        x, y, c, chips = _place()
        me, sibling = (x, y, c), (x, y, 1 - c)

        def rows(px, py, pc):
            return out_ref.at[pl.ds((4 * px + 2 * py + pc) * M, M), :]

        def copy(k, blk, to, src=None):
            return pltpu.make_async_remote_copy(src_ref=rows(*blk) if src is None else src, dst_ref=rows(*blk),
                                                send_sem=send_sems.at[k], recv_sem=recv_sems.at[k], device_id=to,
                                                device_id_type=MESH)

        mine = pltpu.make_async_copy(x_ref, rows(*me), local_sem)
        mine.start()
        first = [copy(0, me, sibling, src=x_ref)]
        first += [copy(1 + j, me, (*chip, c), src=x_ref) for j, chip in enumerate(chips)]
        for cp in first:
            cp.start()
        passed = [copy(4 + j, (*chip, c), sibling) for j, chip in enumerate(chips)]
        for j, chip in enumerate(chips):
            copy(1 + j, (*chip, c), me).wait_recv()
            passed[j].start()
        copy(0, sibling, me).wait_recv()
        for j, chip in enumerate(chips):
            copy(4 + j, (*chip, 1 - c), me).wait_recv()
        for cp in first + passed:
            cp.wait_send()
        mine.wait()

    vmem = pl.BlockSpec(memory_space=pltpu.VMEM)
    return pl.pallas_call(
        body, name=f"gather_devices_{M}", out_shape=jax.ShapeDtypeStruct((8 * M, N), block.dtype),
        in_specs=[vmem], out_specs=vmem,
        scratch_shapes=[pltpu.SemaphoreType.DMA((7,)), pltpu.SemaphoreType.DMA((7,)), pltpu.SemaphoreType.DMA],
        compiler_params=pltpu.CompilerParams(vmem_limit_bytes=VMEM_LIMIT))(block)


def _swap_halves(g):
    n, R, C = g.shape
    Rh = R // 2

    def body(g_ref, a_ref, send_sem, recv_sem):
        x, y, c, _ = _place()
        cp = pltpu.make_async_remote_copy(src_ref=g_ref.at[:, pl.ds((1 - c) * Rh, Rh), :], dst_ref=a_ref,
                                          send_sem=send_sem, recv_sem=recv_sem, device_id=(x, y, 1 - c),
                                          device_id_type=MESH)
        cp.start()
        cp.wait()

    return pl.pallas_call(body, name="grad_swap_halves", out_shape=jax.ShapeDtypeStruct((n, Rh, C), g.dtype),
                          in_specs=[ANY], out_specs=ANY,
                          scratch_shapes=[pltpu.SemaphoreType.DMA, pltpu.SemaphoreType.DMA])(g)


def _add_own_half(g, a, core):
    n, R, C = g.shape
    Rh = R // 2
    tr = _row_tile(Rh, 512)
    nt = Rh // tr

    def body(c_ref, g_ref, a_ref, o_ref):
        o_ref[...] = g_ref[...] + a_ref[...]

    blk = pl.BlockSpec((1, tr, C), lambda q, i, c_ref: (q, i, 0))
    spec = pltpu.PrefetchScalarGridSpec(
        num_scalar_prefetch=1, grid=(n, nt),
        in_specs=[pl.BlockSpec((1, tr, C), lambda q, i, c_ref: (q, c_ref[0] * nt + i, 0)), blk], out_specs=blk)
    return pl.pallas_call(body, name="grad_add_own_half", grid_spec=spec, out_shape=jax.ShapeDtypeStruct((n, Rh, C), F32),
                          compiler_params=_params(2))(core.reshape(1).astype(jnp.int32), g, a)


def _swap_chips(p):
    n, Rh, C = p.shape

    def body(p_ref, b_ref, send_sems, recv_sems, local_sem):
        x, y, c, chips = _place()
        me = 2 * x + y
        mine = pltpu.make_async_copy(p_ref.at[me], b_ref.at[me], local_sem)
        mine.start()
        sends = []
        for j, (qx, qy) in enumerate(chips):
            cp = pltpu.make_async_remote_copy(src_ref=p_ref.at[2 * qx + qy], dst_ref=b_ref.at[me], send_sem=send_sems.at[j],
                                              recv_sem=recv_sems.at[j], device_id=(qx, qy, c), device_id_type=MESH)
            cp.start()
            sends.append(cp)
        for j, (qx, qy) in enumerate(chips):
            pltpu.make_async_remote_copy(src_ref=p_ref.at[me], dst_ref=b_ref.at[2 * qx + qy], send_sem=send_sems.at[j],
                                         recv_sem=recv_sems.at[j], device_id=(qx, qy, c), device_id_type=MESH).wait_recv()
        for cp in sends:
            cp.wait_send()
        mine.wait()

    return pl.pallas_call(
        body, name="grad_swap_chips", out_shape=jax.ShapeDtypeStruct((n, Rh, C), p.dtype), in_specs=[ANY], out_specs=ANY,
        scratch_shapes=[pltpu.SemaphoreType.DMA((3,)), pltpu.SemaphoreType.DMA((3,)), pltpu.SemaphoreType.DMA])(p)


def _sum_slots(b, *, name):
    n, R, C = b.shape
    tr = _row_tile(R, 512)

    def body(b_ref, o_ref):
        acc = b_ref[0]
        for q in range(1, n):
            acc = acc + b_ref[q]
        o_ref[...] = acc

    return pl.pallas_call(body, name=name, grid=(R // tr,), in_specs=[pl.BlockSpec((n, tr, C), lambda i: (0, i, 0))],
                          out_specs=pl.BlockSpec((tr, C), lambda i: (i, 0)), out_shape=jax.ShapeDtypeStruct((R, C), F32),
                          compiler_params=_params(1))(b)


def _join_halves(r):
    Rh, C = r.shape

    def body(r_ref, o_ref, send_sem, recv_sem, local_sem):
        x, y, c, _ = _place()
        own, other = o_ref.at[pl.ds(c * Rh, Rh), :], o_ref.at[pl.ds((1 - c) * Rh, Rh), :]
        mine = pltpu.make_async_copy(r_ref, own, local_sem)
        mine.start()
        cp = pltpu.make_async_remote_copy(src_ref=r_ref, dst_ref=own, send_sem=send_sem, recv_sem=recv_sem,
                                          device_id=(x, y, 1 - c), device_id_type=MESH)
        cp.start()
        pltpu.make_async_remote_copy(src_ref=r_ref, dst_ref=other, send_sem=send_sem, recv_sem=recv_sem,
                                     device_id=(x, y, 1 - c), device_id_type=MESH).wait_recv()
        cp.wait_send()
        mine.wait()

    return pl.pallas_call(
        body, name="grad_join_halves", out_shape=jax.ShapeDtypeStruct((2 * Rh, C), r.dtype), in_specs=[ANY], out_specs=ANY,
        scratch_shapes=[pltpu.SemaphoreType.DMA, pltpu.SemaphoreType.DMA, pltpu.SemaphoreType.DMA])(r)


def _pack(pieces, cols, row_multiple, dtype):
    flat = jnp.concatenate([p.reshape(-1).astype(dtype) for p in pieces])
    per = cols * row_multiple
    total = -(-flat.shape[0] // per) * per
    return jnp.pad(flat, (0, total - flat.shape[0])).reshape(total // cols, cols)


def _unpack(buf, shapes):
    flat, out, at = buf.reshape(-1), [], 0
    for shp in shapes:
        size = 1
        for d in shp:
            size *= d
        out.append(flat[at:at + size].reshape(shp))
        at += size
    return out


SHARDED = (("ffn_pre_w_gu", 2), ("ffn_pre_w_down", 1), ("sbg_w_in", 2), ("sbg_w_out", 1), ("mla_w_in", 1),
           ("mla_w_uq", 2), ("mla_w_ukv", 2), ("mla_w_out", 1), ("xmem_wq", 1), ("xmem_wkv", 2), ("xmem_wo", 1),
           ("ffn_post_w_gu", 2), ("ffn_post_w_down", 1))
LORA_GAINS = ("mla_q_lora_gain", "mla_kv_lora_gain")
REPLICATED = ("ffn_pre_norm", "mix_norm", "sgu_ln_gain", "sgu_ln_bias", "sgu_w", "sgu_b", "mla_q_gain", "mla_k_gain",
              "xmem_norm", "xmem_mem_norm", "xmem_q_gain", "xmem_k_gain", "ffn_post_norm")
WEIGHTS = ("ffn_pre_norm", "ffn_pre_w_gu", "ffn_pre_w_down", "mix_norm", "sbg_w_in", "sgu_ln_gain", "sgu_ln_bias", "sgu_w",
           "sgu_b", "sbg_w_out", "mla_w_in", "mla_q_lora_gain", "mla_kv_lora_gain", "mla_w_uq", "mla_w_ukv", "mla_q_gain",
           "mla_k_gain", "mla_w_out", "xmem_norm", "xmem_mem_norm", "xmem_wq", "xmem_wkv", "xmem_q_gain", "xmem_k_gain",
           "xmem_wo", "ffn_post_norm", "ffn_post_w_gu", "ffn_post_w_down")
INPUTS = ("x", "mem", "positions") + WEIGHTS + ("loss_target",) + tuple("m_" + n for n in WEIGHTS) + tuple(
    "v_" + n for n in WEIGHTS)


def _step(a):
    x, y, c, _ = _place()
    chip = 2 * x + y
    shard_shapes = [a[n].shape for n, _ in SHARDED]

    gathered = _gather_chips(_pack([a[n] for n, _ in SHARDED], PACK_COLS, PACK_ROW_MULTIPLE, BF16))
    parts = [_unpack(gathered[q], shard_shapes) for q in range(N_CHIPS)]
    w = {n: jnp.concatenate([parts[q][i] for q in range(N_CHIPS)], axis=ax) for i, (n, ax) in enumerate(SHARDED)}
    gains = jnp.zeros((8, LANE), F32)
    for r, n in enumerate(LORA_GAINS):
        gains = gains.at[r, :a[n].shape[1]].set(a[n][0])
    gains = _gather_devices(gains)
    for r, n in enumerate(LORA_GAINS):
        w[n] = jnp.concatenate([gains[16 * q + r, :a[n].shape[1]] for q in range(N_CHIPS)])[None, :]
    for n in REPLICATED:
        w[n] = a[n]

    loss, dx, grads = _local_step(a["x"][0], a["mem"][0], a["positions"][0], a["loss_target"][0], w)
    loss = lax.psum(loss, ("x", "y", "c"))
    full = {n: jnp.stack(grads[n]).reshape(w[n].shape) for n in WEIGHTS}

    def cut(n, ax, q):
        size = full[n].shape[ax] // N_CHIPS
        return lax.slice_in_dim(full[n], q * size, (q + 1) * size, axis=ax)

    g = jnp.stack([_pack([cut(n, ax, q) for n, ax in SHARDED], PACK_COLS, PACK_ROW_MULTIPLE, F32) for q in range(N_CHIPS)])
    partial = _add_own_half(g, _swap_halves(g), c)
    reduced = _join_halves(_sum_slots(_swap_chips(partial), name="grad_sum_chips"))
    gw = dict(zip([n for n, _ in SHARDED], _unpack(reduced, shard_shapes)))

    small_names = REPLICATED + LORA_GAINS
    small = _pack([full[n] for n in small_names], LANE, 256, F32)
    rows = small.shape[0]
    summed = _sum_slots(_gather_devices(small).reshape(8, rows, LANE), name="grad_sum_devices")
    for n, val in zip(small_names, _unpack(summed, [full[n].shape for n in small_names])):
        if n in LORA_GAINS:
            size = a[n].shape[1]
            val = lax.dynamic_slice_in_dim(val, chip * size, size, axis=1)
        gw[n] = val

    upd = {n: _adamw(a[n], gw[n], a["m_" + n], a["v_" + n], name="adamw_" + n) for n in WEIGHTS}
    return (loss, dx[None], *[gw[n] for n in WEIGHTS], *[upd[n][0] for n in WEIGHTS], *[upd[n][1] for n in WEIGHTS],
            *[upd[n][2] for n in WEIGHTS])


def kernel(x, mem, positions, ffn_pre_norm, ffn_pre_w_gu, ffn_pre_w_down, mix_norm, sbg_w_in, sgu_ln_gain,
           sgu_ln_bias, sgu_w, sgu_b, sbg_w_out, mla_w_in, mla_q_lora_gain, mla_kv_lora_gain, mla_w_uq, mla_w_ukv,
           mla_q_gain, mla_k_gain, mla_w_out, xmem_norm, xmem_mem_norm, xmem_wq, xmem_wkv, xmem_q_gain, xmem_k_gain,
           xmem_wo, ffn_post_norm, ffn_post_w_gu, ffn_post_w_down, loss_target, m_ffn_pre_norm, m_ffn_pre_w_gu,
           m_ffn_pre_w_down, m_mix_norm, m_sbg_w_in, m_sgu_ln_gain, m_sgu_ln_bias, m_sgu_w, m_sgu_b, m_sbg_w_out,
           m_mla_w_in, m_mla_q_lora_gain, m_mla_kv_lora_gain, m_mla_w_uq, m_mla_w_ukv, m_mla_q_gain, m_mla_k_gain,
           m_mla_w_out, m_xmem_norm, m_xmem_mem_norm, m_xmem_wq, m_xmem_wkv, m_xmem_q_gain, m_xmem_k_gain,
           m_xmem_wo, m_ffn_post_norm, m_ffn_post_w_gu, m_ffn_post_w_down, v_ffn_pre_norm, v_ffn_pre_w_gu,
           v_ffn_pre_w_down, v_mix_norm, v_sbg_w_in, v_sgu_ln_gain, v_sgu_ln_bias, v_sgu_w, v_sgu_b, v_sbg_w_out,
           v_mla_w_in, v_mla_q_lora_gain, v_mla_kv_lora_gain, v_mla_w_uq, v_mla_w_ukv, v_mla_q_gain, v_mla_k_gain,
           v_mla_w_out, v_xmem_norm, v_xmem_mem_norm, v_xmem_wq, v_xmem_wkv, v_xmem_q_gain, v_xmem_k_gain,
           v_xmem_wo, v_ffn_post_norm, v_ffn_post_w_gu, v_ffn_post_w_down):
    given = locals()
    return _step({n: given[n] for n in INPUTS})
```

```python
import functools

import jax
import jax.numpy as jnp
from jax import lax
from jax.experimental import pallas as pl
from jax.experimental.pallas import tpu as pltpu

F32, BF16 = jnp.float32, jnp.bfloat16
LANE = 128
VMEM_LIMIT = 56 * 1024 * 1024
EPS = 1e-6
D_FF = 2816
SB_HEADS, SB_HD = 8, 64
SG_GROUPS, SG_GD, SG_CHUNK = 8, 64, 128
SB_W, SG_W = SB_HEADS * SB_HD, SG_GROUPS * SG_GD
MLA_HEADS, MLA_NOPE, MLA_ROPE, MLA_V = 16, 64, 32, 64
MLA_QK = MLA_NOPE + MLA_ROPE
MLA_QL, MLA_KVL = 512, 256
ROPE_THETA = 10000.0
MEM_HEADS, MEM_HD = 4, 256
SB_SCALE, MLA_SCALE, MEM_SCALE = SB_HD ** -0.5, MLA_QK ** -0.5, MEM_HD ** -0.5
ADAM_LR, ADAM_B1, ADAM_B2, ADAM_EPS, ADAM_WD, ADAM_STEP = 0.001, 0.9, 0.999, 1e-08, 0.01, 10
MESH = pl.DeviceIdType.MESH
ANY = pl.BlockSpec(memory_space=pl.ANY)


def _params(n_axes):
    return pltpu.CompilerParams(dimension_semantics=("arbitrary",) * n_axes, vmem_limit_bytes=VMEM_LIMIT)


MM_TILE_CAP = 1408
MM_VMEM_BUDGET = 40 * 1024 * 1024


def _tile(dim, cap):
    if dim <= cap:
        return dim
    best = max(t for t in range(LANE, cap + 1, LANE) if dim % t == 0)
    return best


def _mm(a, b, *, ta=False, tb=False, out_dtype=F32, scale=1.0, residual=None, bias=None, a_off=(0, 0), b_off=(0, 0),
        m=None, n=None, k=None, name):
    am, ak = (a.shape[1], a.shape[0]) if ta else a.shape
    bk, bn = (b.shape[1], b.shape[0]) if tb else b.shape
    M, N, K = m or am, n or bn, k or ak
    tm, tn = _tile(M, MM_TILE_CAP), _tile(N, MM_TILE_CAP)
    fixed = tm * tn * (4 + 2 * jnp.dtype(out_dtype).itemsize + (2 * residual.dtype.itemsize if residual is not None else 0))
    per_k = (tm * (2 * a.dtype.itemsize + 2) + tn * (2 * b.dtype.itemsize + 2))
    tk = _tile(K, max(LANE, (MM_VMEM_BUDGET - fixed) // per_k))
    nm, nn, nk = M // tm, N // tn, K // tk
    a_off = (a_off[0] // (tk if ta else tm), a_off[1] // (tm if ta else tk))
    b_off = (b_off[0] // (tn if tb else tk), b_off[1] // (tk if tb else tn))
    dims = (((0 if ta else 1,), (1 if tb else 0,)), ((), ()))

    def body(*refs):
        a_ref, b_ref = refs[0], refs[1]
        o_ref, acc_ref = refs[-2], refs[-1]
        kk = pl.program_id(2)

        @pl.when(kk == 0)
        def _():
            acc_ref[...] = jnp.zeros_like(acc_ref)

        acc_ref[...] += lax.dot_general(a_ref[...].astype(BF16), b_ref[...].astype(BF16), dims,
                                        preferred_element_type=F32)

        @pl.when(kk == nk - 1)
        def _():
            out = acc_ref[...] * scale
            for extra in refs[2:-2]:
                out = out + extra[...].astype(F32)
            o_ref[...] = out.astype(o_ref.dtype)

    (ao0, ao1), (bo0, bo1) = a_off, b_off
    a_spec = (pl.BlockSpec((tk, tm), lambda i, j, kk: (kk + ao0, i + ao1)) if ta
              else pl.BlockSpec((tm, tk), lambda i, j, kk: (i + ao0, kk + ao1)))
    b_spec = (pl.BlockSpec((tn, tk), lambda i, j, kk: (j + bo0, kk + bo1)) if tb
              else pl.BlockSpec((tk, tn), lambda i, j, kk: (kk + bo0, j + bo1)))
    o_spec = pl.BlockSpec((tm, tn), lambda i, j, kk: (i, j))
    ins, in_specs = [a, b], [a_spec, b_spec]
    if residual is not None:
        ins.append(residual)
        in_specs.append(o_spec)
    if bias is not None:
        ins.append(bias)
        in_specs.append(pl.BlockSpec((1, tn), lambda i, j, kk: (0, j)))
    return pl.pallas_call(
        body, name=name, grid=(nm, nn, nk), in_specs=in_specs, out_specs=o_spec,
        out_shape=jax.ShapeDtypeStruct((M, N), out_dtype),
        scratch_shapes=[pltpu.VMEM((tm, tn), F32)], compiler_params=_params(3))(*ins)


def _mm_swiglu(h, wgu, *, name):
    M, K = h.shape
    F = wgu.shape[1] // 2
    tm, tn, tk = _tile(M, 512), _tile(F, MM_TILE_CAP), _tile(K, 1024)
    nm, nf, nk = M // tm, F // tn, K // tk

    def body(h_ref, wg_ref, wu_ref, g_ref, u_ref, a_ref, accg, accu):
        kk = pl.program_id(2)

        @pl.when(kk == 0)
        def _():
            accg[...] = jnp.zeros_like(accg)
            accu[...] = jnp.zeros_like(accu)

        hb = h_ref[...]
        accg[...] += jnp.dot(hb, wg_ref[...], preferred_element_type=F32)
        accu[...] += jnp.dot(hb, wu_ref[...], preferred_element_type=F32)

        @pl.when(kk == nk - 1)
        def _():
            g, u = accg[...], accu[...]
            g_ref[...] = g.astype(BF16)
            u_ref[...] = u.astype(BF16)
            a_ref[...] = (g * jax.nn.sigmoid(g) * u).astype(BF16)

    o_spec = pl.BlockSpec((tm, tn), lambda i, j, kk: (i, j))
    shp = jax.ShapeDtypeStruct((M, F), BF16)
    return pl.pallas_call(
        body, name=name, grid=(nm, nf, nk),
        in_specs=[pl.BlockSpec((tm, tk), lambda i, j, kk: (i, kk)),
                  pl.BlockSpec((tk, tn), lambda i, j, kk: (kk, j)),
                  pl.BlockSpec((tk, tn), lambda i, j, kk: (kk, j + nf))],
        out_specs=[o_spec, o_spec, o_spec], out_shape=[shp, shp, shp],
        scratch_shapes=[pltpu.VMEM((tm, tn), F32), pltpu.VMEM((tm, tn), F32)],
        compiler_params=_params(3))(h, wgu, wgu)


def _mm_dswiglu(dy, wd, gate, up, *, scale, name):
    M, K = dy.shape
    F = wd.shape[0]
    tm, tn, tk = _tile(M, 512), _tile(F, MM_TILE_CAP), _tile(K, 1024)
    nm, nf, nk = M // tm, F // tn, K // tk

    def body(dy_ref, wd_ref, g_ref, u_ref, dg_ref, du_ref, acc):
        kk = pl.program_id(2)

        @pl.when(kk == 0)
        def _():
            acc[...] = jnp.zeros_like(acc)

        acc[...] += lax.dot_general(dy_ref[...].astype(BF16), wd_ref[...], (((1,), (1,)), ((), ())),
                                    preferred_element_type=F32)

        @pl.when(kk == nk - 1)
        def _():
            da = acc[...] * scale
            g, u = g_ref[...].astype(F32), u_ref[...].astype(F32)
            sg = jax.nn.sigmoid(g)
            du_ref[...] = (da * g * sg).astype(BF16)
            dg_ref[...] = (da * u * sg * (1.0 + g * (1.0 - sg))).astype(BF16)

    o_spec = pl.BlockSpec((tm, tn), lambda i, j, kk: (i, j))
    shp = jax.ShapeDtypeStruct((M, F), BF16)
    return pl.pallas_call(
        body, name=name, grid=(nm, nf, nk),
        in_specs=[pl.BlockSpec((tm, tk), lambda i, j, kk: (i, kk)),
                  pl.BlockSpec((tn, tk), lambda i, j, kk: (j, kk)), o_spec, o_spec],
        out_specs=[o_spec, o_spec], out_shape=[shp, shp],
        scratch_shapes=[pltpu.VMEM((tm, tn), F32)], compiler_params=_params(3))(dy, wd, gate, up)


HEAD_ROWS = 1024


def _row_tile(rows, cap):
    t = cap
    while t >= 8:
        if rows % t == 0:
            return t
        t //= 2
    return rows


def _rowwise(fn, rows, consts, outs, sums=(), hsums=(), *, heads=None, tm=256, name):
    rows = [r if isinstance(r, tuple) else (r, r.shape[1], None) for r in rows]
    S = rows[0][0].shape[0]
    tm = _row_tile(S, tm)
    nh = heads or 1
    n_r, n_c, n_o, n_h, n_s = len(rows), len(consts), len(outs), len(hsums), len(sums)

    def body(*refs):
        r = [x[...] for x in refs[:n_r]]
        c = [x[...] for x in refs[n_r:n_r + n_c]]
        o_refs = refs[n_r + n_c:n_r + n_c + n_o]
        h_refs = refs[n_r + n_c + n_o:n_r + n_c + n_o + n_h]
        s_refs = refs[n_r + n_c + n_o + n_h:]
        res = fn(*r, *c)
        res = res if isinstance(res, (tuple, list)) else (res,)
        for ref, val in zip(o_refs, res[:n_o]):
            ref[...] = val.astype(ref.dtype)
        if n_h:
            @pl.when(pl.program_id(1) == 0)
            def _():
                for ref in h_refs:
                    ref[...] = jnp.zeros_like(ref)
            for ref, val in zip(h_refs, res[n_o:n_o + n_h]):
                ref[...] += val
        if n_s:
            @pl.when((pl.program_id(0) == 0) & (pl.program_id(1) == 0))
            def _():
                for ref in s_refs:
                    ref[...] = jnp.zeros_like(ref)
            for ref, val in zip(s_refs, res[n_o + n_h:]):
                ref[...] += val

    def col(colfn):
        return (lambda i, h: (i, 0)) if colfn is None else (lambda i, h: (i, colfn(h)))

    in_specs = [pl.BlockSpec((tm, w), col(cf)) for _, w, cf in rows]
    in_specs += [pl.BlockSpec(a.shape, lambda i, h, nd=a.ndim: (0,) * nd) for a in consts]
    out_specs = [pl.BlockSpec((tm, w // nh), (lambda i, h: (i, h)) if heads else (lambda i, h: (i, 0))) for w, _ in outs]
    out_specs += [pl.BlockSpec((tm, w), lambda i, h: (i, 0)) for w in hsums]
    out_specs += [pl.BlockSpec(sh, lambda i, h, nd=len(sh): (0,) * nd) for sh in sums]
    out_shape = [jax.ShapeDtypeStruct((S, w), dt) for w, dt in outs]
    out_shape += [jax.ShapeDtypeStruct((S, w), F32) for w in hsums]
    out_shape += [jax.ShapeDtypeStruct(sh, F32) for sh in sums]
    return pl.pallas_call(body, name=name, grid=(S // tm, nh), in_specs=in_specs, out_specs=out_specs,
                          out_shape=out_shape, compiler_params=_params(2))(*[a for a, _, _ in rows], *consts)


def _rms(x, width=None):
    width = width or x.shape[-1]
    return lax.rsqrt(jnp.sum(x * x, axis=-1, keepdims=True) * (1.0 / width) + EPS)


def _rmsnorm_fwd(x, g, width=None):
    return x * _rms(x, width) * g


def _rmsnorm_bwd(dy, x, g, width=None):
    width = width or x.shape[-1]
    r = _rms(x, width)
    xn = x * r
    dxn = dy * g
    dx = r * (dxn - xn * (jnp.sum(dxn * xn, axis=-1, keepdims=True) * (1.0 / width)))
    return dx, jnp.sum(dy * xn, axis=0, keepdims=True)


def _norm_rows(x, g, *, name, out_dtype=BF16):
    D = x.shape[1]
    return _rowwise(lambda xv, gv: _rmsnorm_fwd(xv.astype(F32), gv), [x], [g.reshape(1, D)], [(D, out_dtype)],
                    name=name)[0]


def _norm_rows_bwd(dh, x, g, dres, *, name):
    D = x.shape[1]

    def fn(dhv, xv, *rest):
        dx, dg = _rmsnorm_bwd(dhv.astype(F32), xv, rest[-1])
        return (dx + rest[0] if dres is not None else dx), dg

    rows = [dh, x] + ([dres] if dres is not None else [])
    return _rowwise(fn, rows, [g.reshape(1, D)], [(D, F32)], [(1, D)], name=name)


def _softplus(z):
    return jnp.maximum(z, 0.0) + jnp.log(1.0 + jnp.exp(-jnp.abs(z)))


def _running_sum(v, u, split=True):
    hi = v.astype(BF16)
    out = jnp.dot(hi, u, preferred_element_type=F32)
    if split:
        out = out + jnp.dot((v - hi.astype(F32)).astype(BF16), u, preferred_element_type=F32)
    return out


def _triangle(tk, inclusive_prefix):
    j, s = lax.broadcasted_iota(jnp.int32, (tk, tk), 0), lax.broadcasted_iota(jnp.int32, (tk, tk), 1)
    return ((j <= s) if inclusive_prefix else (j > s)).astype(BF16)


def _nt(a, b):
    return lax.dot_general(a, b, (((1,), (1,)), ((), ())), preferred_element_type=F32)


def _tn(a, b):
    return lax.dot_general(a, b, (((0,), (0,)), ((), ())), preferred_element_type=F32)


ATT_TQ, ATT_TK = 512, 512
SB_SUB = 256


def _attn_fwd(q, k, v, *, sb, causal, heads, dq, dv, kcol=None, vcol=None, sum_lane=None, name):
    S, Sk = q.shape[0], k.shape[0]
    tq, tk = min(ATT_TQ, S), min(ATT_TK, Sk)
    assert tq == tk or not causal
    sub = min(SB_SUB, tk) if sb else tk
    nsub = tk // sub
    kcol = kcol or (lambda h: h)
    vcol = vcol or (lambda h: h)

    def body(*refs):
        if sb:
            q_ref, k_ref, v_ref, u_ref, o_ref, lse_ref, acc_ref, r_ref = refs
            r_ref[...] = jnp.zeros_like(r_ref)
        else:
            q_ref, k_ref, v_ref, o_ref, lse_ref, acc_ref, m_ref, l_ref = refs
            m_ref[...] = jnp.full_like(m_ref, -1e30)
            l_ref[...] = jnp.zeros_like(l_ref)
        first_row = pl.program_id(1) * tq
        qb = q_ref[...]
        acc_ref[...] = jnp.zeros_like(acc_ref)
        nblk = (first_row + tq + tk - 1) // tk if causal else Sk // tk
        nfull = (first_row + (0 if sb else 1)) // tk if causal else nblk

        def piece(off, masked):
            kb, vb = k_ref[pl.ds(off, sub), :], v_ref[pl.ds(off, sub), :]
            s = _nt(qb, kb)
            sp = _softplus(s)
            if masked:
                valid = (off + lax.broadcasted_iota(jnp.int32, (tq, sub), 1)
                         < first_row + lax.broadcasted_iota(jnp.int32, (tq, sub), 0))
            ls = jnp.where(valid, -sp, 0.0) if masked else -sp
            w = jnp.exp(s - sp + r_ref[...] + _running_sum(ls, u_ref[...]))
            if masked:
                w = jnp.where(valid, w, 0.0)
            acc_ref[...] += jnp.dot(w.astype(BF16), vb, preferred_element_type=F32)
            r_ref[...] += jnp.sum(ls, axis=1, keepdims=True)

        def tile(jj, masked):
            for cidx in reversed(range(nsub)):
                off = pl.multiple_of(jj * tk + cidx * sub, sub)
                if masked and nsub > 1:
                    @pl.when(off < first_row + tq)
                    def _():
                        piece(off, True)
                else:
                    piece(off, masked)

        def sweep(lo, hi, masked):
            def step(t, carry):
                tile(hi - 1 - t, masked)
                return carry
            lax.fori_loop(0, hi - lo, step, 0)

        def scores(jj):
            return _nt(qb, k_ref[pl.ds(pl.multiple_of(jj * tk, tk), tk), :])

        def softmax_tile(jj, s, masked):
            off = pl.multiple_of(jj * tk, tk)
            vb = v_ref[pl.ds(off, tk), :]
            if masked:
                qpos = first_row + lax.broadcasted_iota(jnp.int32, (tq, tk), 0)
                kpos = off + lax.broadcasted_iota(jnp.int32, (tq, tk), 1)
                s = jnp.where(kpos <= qpos, s, -1e30)
            m_old = m_ref[...]
            m_new = jnp.maximum(m_old, jnp.max(s, axis=1, keepdims=True))
            p = jnp.exp(s - m_new)
            alpha = jnp.exp(m_old - m_new)
            if sum_lane is None:
                l_ref[...] = alpha * l_ref[...] + jnp.sum(p, axis=1, keepdims=True)
            acc_ref[...] = alpha * acc_ref[...] + jnp.dot(p.astype(BF16), vb, preferred_element_type=F32)
            m_ref[...] = m_new

        if sb:
            sweep(nfull, nblk, True)
            sweep(0, nfull, False)
        else:
            n_loop = nfull if causal else nblk - 1

            def step(t, s_cur):
                s_next = scores(jnp.minimum(t + 1, nblk - 1))
                softmax_tile(t, s_cur, False)
                return s_next

            softmax_tile(n_loop, lax.fori_loop(0, n_loop, step, scores(0)), causal)
        if sb:
            o_ref[...] = acc_ref[...]
            lse_ref[0] = r_ref[...]
        else:
            acc = acc_ref[...]
            l = l_ref[...] if sum_lane is None else acc[:, sum_lane:sum_lane + 1]
            o_ref[...] = acc / l
            lse_ref[0] = m_ref[...] + jnp.log(l)

    in_specs = [pl.BlockSpec((tq, dq), lambda h, i: (i, h)),
                pl.BlockSpec((Sk, dq), lambda h, i: (0, kcol(h))),
                pl.BlockSpec((Sk, dv), lambda h, i: (0, vcol(h)))]
    ins = [q, k, v]
    scratch = [pltpu.VMEM((tq, dv), F32), pltpu.VMEM((tq, 1), F32)]
    if sb:
        ins.append(_triangle(sub, inclusive_prefix=False))
        in_specs.append(pl.BlockSpec((sub, sub), lambda h, i: (0, 0)))
    else:
        scratch.append(pltpu.VMEM((tq, 1), F32))
    out_specs = [pl.BlockSpec((tq, dv), lambda h, i: (i, h)), pl.BlockSpec((1, tq, 1), lambda h, i: (h, i, 0))]
    out_shape = [jax.ShapeDtypeStruct((S, heads * dv), F32), jax.ShapeDtypeStruct((heads, S, 1), F32)]
    return pl.pallas_call(body, name=name, grid=(heads, S // tq), in_specs=in_specs, out_specs=out_specs,
                          out_shape=out_shape, scratch_shapes=scratch, compiler_params=_params(2))(*ins)


def _attn_bwd(q, k, v, o, do, lse, *, sb, causal, heads, dq, dv, kcol=None, vcol=None, name):
    S, Sk = q.shape[0], k.shape[0]
    tq, tk = min(ATT_TQ, S), min(ATT_TK, Sk)
    sub = min(SB_SUB, tk) if sb else tk
    nsub = tk // sub
    nq = S // tq
    kcol = kcol or (lambda h: h)
    vcol = vcol or (lambda h: h)

    def body(*refs):
        if sb:
            q_ref, k_ref, v_ref, o_ref, do_ref, lse_ref, u_ref, dq_ref, dk_ref, dv_ref, acc_ref, r_ref, re_ref = refs
            r_ref[...] = jnp.zeros_like(r_ref)
            re_ref[...] = jnp.zeros_like(re_ref)
        else:
            q_ref, k_ref, v_ref, o_ref, do_ref, lse_ref, dq_ref, dk_ref, dv_ref, acc_ref = refs
        first_row = pl.program_id(1) * tq

        @pl.when(first_row == 0)
        def _():
            dk_ref[...] = jnp.zeros_like(dk_ref)
            dv_ref[...] = jnp.zeros_like(dv_ref)

        qb = q_ref[...]
        dof = do_ref[...].astype(F32)
        dob = dof.astype(BF16)
        if not sb:
            dlt = jnp.sum(dof * o_ref[...], axis=1, keepdims=True)
        acc_ref[...] = jnp.zeros_like(acc_ref)
        nblk = (first_row + tq + tk - 1) // tk if causal else Sk // tk
        nfull = (first_row + (0 if sb else 1)) // tk if causal else nblk

        def piece(off, masked):
            kb, vb = k_ref[pl.ds(off, sub), :], v_ref[pl.ds(off, sub), :]
            s = _nt(qb, kb)
            dp = _nt(dob, vb)
            if masked:
                qpos = first_row + lax.broadcasted_iota(jnp.int32, (tq, sub), 0)
                kpos = off + lax.broadcasted_iota(jnp.int32, (tq, sub), 1)
                valid = (kpos < qpos) if sb else (kpos <= qpos)
            if sb:
                u = u_ref[...]
                sp = _softplus(s)
                ls = jnp.where(valid, -sp, 0.0) if masked else -sp
                lb = s - sp
                w = jnp.exp(lb + (lse_ref[0] - (r_ref[...] + _running_sum(ls, u))))
                if masked:
                    w = jnp.where(valid, w, 0.0)
                e = dp * w
                ds = e - jnp.exp(lb) * (re_ref[...] + _running_sum(e, u, split=False))
                if masked:
                    ds = jnp.where(valid, ds, 0.0)
                r_ref[...] += jnp.sum(ls, axis=1, keepdims=True)
                re_ref[...] += jnp.sum(e, axis=1, keepdims=True)
            else:
                w = jnp.exp(s - lse_ref[0])
                if masked:
                    w = jnp.where(valid, w, 0.0)
                ds = w * (dp - dlt)
            dsb = ds.astype(BF16)
            dv_ref[pl.ds(off, sub), :] += _tn(w.astype(BF16), dob)
            dk_ref[pl.ds(off, sub), :] += _tn(dsb, qb)
            acc_ref[...] += jnp.dot(dsb, kb, preferred_element_type=F32)

        def tile(jj, masked):
            for cidx in range(nsub):
                off = pl.multiple_of(jj * tk + cidx * sub, sub)
                if masked and nsub > 1:
                    @pl.when(off < first_row + tq)
                    def _():
                        piece(off, True)
                else:
                    piece(off, masked)

        def sweep(lo, hi, masked):
            def step(t, carry):
                tile(lo + t, masked)
                return carry
            lax.fori_loop(0, hi - lo, step, 0)

        sweep(0, nfull, False)
        if causal:
            sweep(nfull, nblk, True)
        dq_ref[...] = acc_ref[...]

    ins = [q, k, v, o, do]
    in_specs = [pl.BlockSpec((tq, dq), lambda h, i: (i, h)),
                pl.BlockSpec((Sk, dq), lambda h, i: (0, kcol(h))),
                pl.BlockSpec((Sk, dv), lambda h, i: (0, vcol(h))),
                pl.BlockSpec((tq, dv), lambda h, i: (i, h)),
                pl.BlockSpec((tq, dv), lambda h, i: (i, h))]
    scratch = [pltpu.VMEM((tq, dq), F32)]
    ins.append(lse)
    in_specs.append(pl.BlockSpec((1, tq, 1), lambda h, i: (h, i, 0)))
    if sb:
        ins.append(_triangle(sub, inclusive_prefix=True))
        in_specs.append(pl.BlockSpec((sub, sub), lambda h, i: (0, 0)))
        scratch += [pltpu.VMEM((tq, 1), F32), pltpu.VMEM((tq, 1), F32)]
    out_specs = [pl.BlockSpec((tq, dq), lambda h, i: (i, h)),
                 pl.BlockSpec((Sk, dq), lambda h, i: (0, h)),
                 pl.BlockSpec((Sk, dv), lambda h, i: (0, h))]
    out_shape = [jax.ShapeDtypeStruct((S, heads * dq), F32), jax.ShapeDtypeStruct((Sk, heads * dq), F32),
                 jax.ShapeDtypeStruct((Sk, heads * dv), F32)]
    return pl.pallas_call(body, name=name, grid=(heads, nq), in_specs=in_specs, out_specs=out_specs,
                          out_shape=out_shape, scratch_shapes=scratch, compiler_params=_params(2))(*ins)


GELU_C = 0.7978845608028654
assert 2 * SG_GD == LANE and SG_CHUNK == LANE


def _gelu(z):
    t = jnp.tanh(GELU_C * (z + 0.044715 * z * z * z))
    return 0.5 * z * (1.0 + t), t


def _gelu_grad(z, t):
    return 0.5 * (1.0 + t) + 0.5 * z * (1.0 - t * t) * GELU_C * (1.0 + 3.0 * 0.044715 * z * z)


def _layernorm_parts(g):
    d = g - jnp.mean(g, axis=-1, keepdims=True)
    rstd = lax.rsqrt(jnp.mean(d * d, axis=-1, keepdims=True) + EPS)
    return d * rstd, rstd


def _gelu_ln(z, gain, bias, *, name):
    def fn(zv, gn, bs):
        a, _ = _gelu(zv)
        y, _ = _layernorm_parts(a[:, SG_W:])
        return a[:, :SG_W], y * gn + bs

    return _rowwise(fn, [z], [gain.reshape(1, SG_W), bias.reshape(1, SG_W)], [(SG_W, F32), (SG_W, BF16)], name=name)


def _gelu_ln_bwd(z, du, dgl, gain, *, name):
    def fn(zv, duv, dglv, gn):
        a, t = _gelu(zv)
        y, rstd = _layernorm_parts(a[:, SG_W:])
        dy = dglv * gn
        dgg = rstd * (dy - jnp.mean(dy, axis=-1, keepdims=True) - y * jnp.mean(dy * y, axis=-1, keepdims=True))
        dz = jnp.concatenate([duv, dgg], axis=1) * _gelu_grad(zv, t)
        return dz, jnp.sum(dglv * y, axis=0, keepdims=True), jnp.sum(dglv, axis=0, keepdims=True)

    return _rowwise(fn, [z, du, dgl], [gain.reshape(1, SG_W)], [(2 * SG_W, BF16)], [(1, SG_W), (1, SG_W)], name=name)


def _sg_masks():
    tri = lax.broadcasted_iota(jnp.int32, (SG_CHUNK, SG_CHUNK), 0) >= lax.broadcasted_iota(jnp.int32, (SG_CHUNK, SG_CHUNK), 1)
    first = lax.broadcasted_iota(jnp.int32, (SG_CHUNK, LANE), 1) < SG_GD
    return tri, first


def _spatial(gl, u, w, bt, *, name):
    S = gl.shape[0]
    tm = _row_tile(S, 512)
    nch = tm // SG_CHUNK

    def body(gl_ref, u_ref, w_ref, bt_ref, o_ref):
        tri, first = _sg_masks()
        for p in range(SG_W // LANE):
            cols = slice(p * LANE, (p + 1) * LANE)
            wa = jnp.where(tri, w_ref[2 * p], 0.0).astype(BF16)
            wb = jnp.where(tri, w_ref[2 * p + 1], 0.0).astype(BF16)
            for ci in range(nch):
                rws = slice(ci * SG_CHUNK, (ci + 1) * SG_CHUNK)
                g = gl_ref[rws, cols]
                zero = jnp.zeros_like(g)
                mixed = (jnp.dot(wa, jnp.where(first, g, zero), preferred_element_type=F32)
                         + jnp.dot(wb, jnp.where(first, zero, g), preferred_element_type=F32) + bt_ref[:, cols])
                o_ref[rws, cols] = u_ref[rws, cols] * mixed

    row = pl.BlockSpec((tm, SG_W), lambda i: (i, 0))
    return pl.pallas_call(
        body, name=name, grid=(S // tm,),
        in_specs=[row, row, pl.BlockSpec(w.shape, lambda i: (0, 0, 0)), pl.BlockSpec(bt.shape, lambda i: (0, 0))],
        out_specs=row, out_shape=jax.ShapeDtypeStruct((S, SG_W), F32), compiler_params=_params(1))(gl, u, w, bt)


def _spatial_bwd(d_o, gl, u, w, bt, *, name):
    S = gl.shape[0]
    tm = _row_tile(S, 512)
    nch = tm // SG_CHUNK
    nsteps = S // tm

    def body(do_ref, gl_ref, u_ref, w_ref, bt_ref, du_ref, dgl_ref, dw_ref, db_ref, dbt_ref):
        tri, first = _sg_masks()
        step = pl.program_id(0)

        @pl.when(step == 0)
        def _():
            dw_ref[...] = jnp.zeros_like(dw_ref)
            dbt_ref[...] = jnp.zeros_like(dbt_ref)

        for p in range(SG_W // LANE):
            cols = slice(p * LANE, (p + 1) * LANE)
            wa = jnp.where(tri, w_ref[2 * p], 0.0).astype(BF16)
            wb = jnp.where(tri, w_ref[2 * p + 1], 0.0).astype(BF16)
            for ci in range(nch):
                rws = slice(ci * SG_CHUNK, (ci + 1) * SG_CHUNK)
                g = gl_ref[rws, cols]
                zero = jnp.zeros_like(g)
                mixed = (jnp.dot(wa, jnp.where(first, g, zero), preferred_element_type=F32)
                         + jnp.dot(wb, jnp.where(first, zero, g), preferred_element_type=F32) + bt_ref[:, cols])
                dov = do_ref[rws, cols]
                du_ref[rws, cols] = dov * mixed
                dm = dov * u_ref[rws, cols]
                dbt_ref[:, cols] += dm
                dma = jnp.where(first, dm, 0.0).astype(BF16)
                dmb = jnp.where(first, 0.0, dm).astype(BF16)
                dw_ref[2 * p] += jnp.where(tri, _nt(dma, g), 0.0)
                dw_ref[2 * p + 1] += jnp.where(tri, _nt(dmb, g), 0.0)
                dgl_ref[rws, cols] = _tn(wa, dma) + _tn(wb, dmb)

        @pl.when(step == nsteps - 1)
        def _():
            lane = lax.broadcasted_iota(jnp.int32, (SG_CHUNK, LANE), 1)
            acc = jnp.zeros((SG_CHUNK, LANE), F32)
            for p in range(SG_W // LANE):
                blk = dbt_ref[:, p * LANE:(p + 1) * LANE]
                sa = jnp.sum(jnp.where(first, blk, 0.0), axis=1, keepdims=True)
                sb_ = jnp.sum(jnp.where(first, 0.0, blk), axis=1, keepdims=True)
                acc = acc + jnp.where(lane == 2 * p, sa, 0.0) + jnp.where(lane == 2 * p + 1, sb_, 0.0)
            db_ref[...] = acc

    row = pl.BlockSpec((tm, SG_W), lambda i: (i, 0))
    return pl.pallas_call(
        body, name=name, grid=(nsteps,),
        in_specs=[row, row, row, pl.BlockSpec(w.shape, lambda i: (0, 0, 0)), pl.BlockSpec(bt.shape, lambda i: (0, 0))],
        out_specs=[row, row, pl.BlockSpec(w.shape, lambda i: (0, 0, 0)), pl.BlockSpec((SG_CHUNK, LANE), lambda i: (0, 0))],
        out_shape=[jax.ShapeDtypeStruct((S, SG_W), F32), jax.ShapeDtypeStruct((S, SG_W), F32),
                   jax.ShapeDtypeStruct(w.shape, F32), jax.ShapeDtypeStruct((SG_CHUNK, LANE), F32)],
        scratch_shapes=[pltpu.VMEM((SG_CHUNK, SG_W), F32)], compiler_params=_params(1))(d_o, gl, u, w, bt)


ROPE_HALF = MLA_ROPE // 2
KR_COL = (MLA_QL + MLA_KVL) // LANE
MLA_IN_PAD = MLA_QL + MLA_KVL + LANE


def _rope_tables(positions):
    inv_freq = ROPE_THETA ** (-jnp.arange(ROPE_HALF, dtype=F32) / ROPE_HALF)
    ang = positions.astype(F32)[:, None] * inv_freq
    cos, sin = jnp.cos(ang), jnp.sin(ang)
    S = positions.shape[0]
    z16, tail = jnp.zeros((S, ROPE_HALF), F32), jnp.zeros((S, LANE - MLA_QK), F32)
    ones = jnp.ones((S, MLA_NOPE), F32)
    zeros = jnp.zeros((S, MLA_NOPE), F32)
    return (jnp.concatenate([ones, cos, cos, tail], axis=1), jnp.concatenate([zeros, z16, sin, tail], axis=1),
            jnp.concatenate([zeros, -sin, z16, tail], axis=1))


def _rope(x, cos, sa, sb):
    return x * cos + pltpu.roll(x, ROPE_HALF, 1) * sa + pltpu.roll(x, LANE - ROPE_HALF, 1) * sb


def _rope_t(dy, cos, sa, sb):
    return dy * cos + pltpu.roll(dy * sa, LANE - ROPE_HALF, 1) + pltpu.roll(dy * sb, ROPE_HALF, 1)


def _mla_lora(P, qlg, kvlg, *, name):
    def fn(pv, a, b):
        return _rmsnorm_fwd(pv[:, :MLA_QL], a), _rmsnorm_fwd(pv[:, MLA_QL:MLA_QL + MLA_KVL], b)

    return _rowwise(fn, [P], [qlg.reshape(1, MLA_QL), kvlg.reshape(1, MLA_KVL)], [(MLA_QL, BF16), (MLA_KVL, BF16)], name=name)


def _mla_lora_bwd(dcq, dckv, dkr, P, qlg, kvlg, *, name):
    def fn(d1, d2, d3, pv, a, b):
        x1, g1 = _rmsnorm_bwd(d1, pv[:, :MLA_QL], a)
        x2, g2 = _rmsnorm_bwd(d2, pv[:, MLA_QL:MLA_QL + MLA_KVL], b)
        return jnp.concatenate([x1, x2, d3], axis=1), g1, g2

    return _rowwise(fn, [dcq, dckv, dkr, P], [qlg.reshape(1, MLA_QL), kvlg.reshape(1, MLA_KVL)], [(MLA_IN_PAD, BF16)],
                    [(1, MLA_QL), (1, MLA_KVL)], name=name)


def _mla_qk(q_pre, k_pre, P, tabs, qg, kg, *, name):
    def fn(qp, kp, kr, c, a, b, qgv, kgv):
        return (_rope(_rmsnorm_fwd(qp, qgv, MLA_QK), c, a, b) * MLA_SCALE,
                _rope(_rmsnorm_fwd(kp + kr, kgv, MLA_QK), c, a, b))

    hcol = lambda h: h
    rows = [(q_pre, LANE, hcol), (k_pre, LANE, hcol), (P, LANE, lambda h: KR_COL), *tabs]
    w = MLA_HEADS * LANE
    return _rowwise(fn, rows, [qg, kg], [(w, BF16), (w, BF16)], heads=MLA_HEADS, tm=HEAD_ROWS, name=name)


def _mla_qk_bwd(dq, dk, q_pre, k_pre, P, tabs, qg, kg, *, name):
    def fn(dqv, dkv, qp, kp, kr, c, a, b, qgv, kgv):
        dqp, dqg = _rmsnorm_bwd(_rope_t(dqv * MLA_SCALE, c, a, b), qp, qgv, MLA_QK)
        dkp, dkg = _rmsnorm_bwd(_rope_t(dkv, c, a, b), kp + kr, kgv, MLA_QK)
        lane = lax.broadcasted_iota(jnp.int32, (1, LANE), 1)
        return dqp, dkp, jnp.where((lane >= MLA_NOPE) & (lane < MLA_QK), dkp, 0.0), dqg, dkg

    hcol = lambda h: h
    rows = [(dq, LANE, hcol), (dk, LANE, hcol), (q_pre, LANE, hcol), (k_pre, LANE, hcol), (P, LANE, lambda h: KR_COL), *tabs]
    w = MLA_HEADS * LANE
    return _rowwise(fn, rows, [qg, kg], [(w, BF16), (w, BF16)], [(1, LANE), (1, LANE)], [LANE], heads=MLA_HEADS,
                    tm=HEAD_ROWS, name=name)


def _head_norm(x, g, *, heads, width, colfn=None, scale=1.0, name):
    return _rowwise(lambda xv, gv: _rmsnorm_fwd(xv, gv) * scale, [(x, width, colfn or (lambda h: h))],
                    [g.reshape(1, width)], [(heads * width, BF16)], heads=heads, tm=HEAD_ROWS, name=name)[0]


def _head_norm_bwd(dy, x, g, *, heads, width, colfn=None, scale=1.0, out_dtype, name):
    return _rowwise(lambda dv_, xv, gv: _rmsnorm_bwd(dv_ * scale, xv, gv),
                    [(dy, width, lambda h: h), (x, width, colfn or (lambda h: h))],
                    [g.reshape(1, width)], [(heads * width, out_dtype)], [(1, width)], heads=heads, tm=HEAD_ROWS,
                    name=name)


def _loss_grad(y, tgt, *, name):
    D = y.shape[1]

    def fn(yv, tv):
        d = yv - tv
        return d * (1.0 / D), jnp.sum(d * d, axis=0, keepdims=True) * (0.5 / D)

    dy, part = _rowwise(fn, [y, tgt], [], [(D, F32)], [(1, D)], name=name)
    return jnp.sum(part), dy


def _adamw(w, g, m, v, *, name):
    shape = w.shape
    two_d = (-1, shape[-1])

    def fn(wv, gv, mv, vv):
        m2 = ADAM_B1 * mv + (1.0 - ADAM_B1) * gv
        v2 = ADAM_B2 * vv + (1.0 - ADAM_B2) * (gv * gv)
        m_hat = m2 / (1.0 - ADAM_B1 ** ADAM_STEP)
        v_hat = v2 / (1.0 - ADAM_B2 ** ADAM_STEP)
        return -ADAM_LR * (m_hat / (jnp.sqrt(v_hat) + ADAM_EPS) + ADAM_WD * wv), m2, v2

    outs = _rowwise(fn, [t.reshape(two_d) for t in (w, g, m, v)], [], [(shape[-1], F32)] * 3, name=name)
    return [o.reshape(shape) for o in outs]


def _pad_cols(w, heads, hd):
    k = w.shape[0]
    return jnp.pad(w.reshape(k, heads, hd), ((0, 0), (0, 0), (0, LANE - hd))).reshape(k, heads * LANE)


def _unpad_cols(w, heads, hd):
    k = w.shape[0]
    return w.reshape(k, heads, LANE)[:, :, :hd].reshape(k, heads * hd)


def _pad_rows(w, heads, hd):
    n = w.shape[1]
    return jnp.pad(w.reshape(heads, hd, n), ((0, 0), (0, LANE - hd), (0, 0))).reshape(heads * LANE, n)


def _unpad_rows(w, heads, hd):
    n = w.shape[1]
    return w.reshape(heads, LANE, n)[:, :hd, :].reshape(heads * hd, n)


def _ffn_fwd(x, g, wgu, wd, tag):
    h = _norm_rows(x, g, name=tag + "_norm")
    gate, up, act = _mm_swiglu(h, wgu, name=tag + "_gu")
    y = _mm(act, wd, scale=0.5, residual=x, name=tag + "_down")
    return y, (x, h, gate, up, act)


def _ffn_bwd(dy, saved, g, wgu, wd, tag):
    x, h, gate, up, act = saved
    F = wd.shape[0]
    dwd = _mm(act, dy, ta=True, scale=0.5, name=tag + "_dwd")
    dgate, dup = _mm_dswiglu(dy, wd, gate, up, scale=0.5, name=tag + "_dact")
    dh = _mm(dgate, wgu, tb=True, name=tag + "_dh_g")
    dh = _mm(dup, wgu, tb=True, b_off=(0, F), residual=dh, name=tag + "_dh_u")
    dwgu = jnp.concatenate([_mm(h, dgate, ta=True, name=tag + "_dwg"), _mm(h, dup, ta=True, name=tag + "_dwu")], axis=1)
    dx, dg = _norm_rows_bwd(dh, x, g, dy, name=tag + "_dnorm")
    return dx, dg, dwgu, dwd


def _even_weights(w_in, w_out):
    parts = [w_in[:, :SB_W] * SB_SCALE, w_in[:, SB_W:2 * SB_W], w_in[:, 2 * SB_W:3 * SB_W]]
    wqkv = jnp.concatenate([_pad_cols(p, SB_HEADS, SB_HD) for p in parts], axis=1)
    return wqkv, w_in[:, 3 * SB_W:], _pad_rows(w_out[:SB_W], SB_HEADS, SB_HD), w_out[SB_W:]


def _even_fwd(x, g, wts, ln_g, ln_b, sgu_w, bt, tag):
    wqkv, wz, wo_sb, wo_sg = wts
    h = _norm_rows(x, g, name=tag + "_norm")
    qkv = _mm(h, wqkv, out_dtype=BF16, name=tag + "_qkv")
    z = _mm(h, wz, name=tag + "_z")
    o_sb, tot = _attn_fwd(qkv, qkv, qkv, sb=True, causal=True, heads=SB_HEADS, dq=LANE, dv=LANE,
                          kcol=lambda hh: SB_HEADS + hh, vcol=lambda hh: 2 * SB_HEADS + hh, name=tag + "_sb")
    u, gl = _gelu_ln(z, ln_g, ln_b, name=tag + "_geluln")
    o_sg = _spatial(gl, u, sgu_w, bt, name=tag + "_sgu")
    y = _mm(o_sb, wo_sb, residual=x, name=tag + "_out_sb")
    y = _mm(o_sg, wo_sg, residual=y, name=tag + "_out_sg")
    return y, (x, h, qkv, z, o_sb, tot, u, gl, o_sg)


def _even_bwd(dy, saved, g, wts, ln_g, sgu_w, bt, tag):
    wqkv, wz, wo_sb, wo_sg = wts
    x, h, qkv, z, o_sb, tot, u, gl, o_sg = saved
    do_sb = _mm(dy, wo_sb, tb=True, name=tag + "_do_sb")
    do_sg = _mm(dy, wo_sg, tb=True, name=tag + "_do_sg")
    dwo = jnp.concatenate([_unpad_rows(_mm(o_sb, dy, ta=True, name=tag + "_dwo_sb"), SB_HEADS, SB_HD),
                           _mm(o_sg, dy, ta=True, name=tag + "_dwo_sg")], axis=0)
    dq, dk, dv = _attn_bwd(qkv, qkv, qkv, o_sb, do_sb, tot, sb=True, causal=True, heads=SB_HEADS, dq=LANE, dv=LANE,
                           kcol=lambda hh: SB_HEADS + hh, vcol=lambda hh: 2 * SB_HEADS + hh, name=tag + "_sb_bwd")
    du, dgl, dsgu_w, db_t = _spatial_bwd(do_sg, gl, u, sgu_w, bt, name=tag + "_sgu_bwd")
    dz, dln_g, dln_b = _gelu_ln_bwd(z, du, dgl, ln_g, name=tag + "_geluln_bwd")
    dh = _mm(dz, wz, tb=True, name=tag + "_dh_z")
    dws = []
    for i, (d, nm) in enumerate(((dq, "q"), (dk, "k"), (dv, "v"))):
        dh = _mm(d, wqkv, tb=True, b_off=(0, i * SB_HEADS * LANE), residual=dh, name=tag + "_dh_" + nm)
        dws.append(_unpad_cols(_mm(h, d, ta=True, scale=SB_SCALE if nm == "q" else 1.0, name=tag + "_dw_" + nm),
                               SB_HEADS, SB_HD))
    dws.append(_mm(h, dz, ta=True, name=tag + "_dw_z"))
    dx, dg = _norm_rows_bwd(dh, x, g, dy, name=tag + "_dnorm")
    return dx, dict(mix_norm=dg, sbg_w_in=jnp.concatenate(dws, axis=1), sgu_ln_gain=dln_g, sgu_ln_bias=dln_b,
                    sgu_w=dsgu_w, sgu_b=db_t[:, :SG_GROUPS].T, sbg_w_out=dwo)


def _mla_weights(w_in, w_uq, w_ukv, w_out, q_gain, k_gain):
    d = w_in.shape[0]
    lat = MLA_QL + MLA_KVL
    w_in_ext = jnp.concatenate([w_in[:, :lat], jnp.zeros((d, MLA_NOPE), w_in.dtype), w_in[:, lat:],
                                jnp.zeros((d, LANE - MLA_QK), w_in.dtype)], axis=1)
    kv = w_ukv.reshape(MLA_KVL, MLA_HEADS, MLA_NOPE + MLA_V)
    wk = _pad_cols(kv[:, :, :MLA_NOPE].reshape(MLA_KVL, -1), MLA_HEADS, MLA_NOPE)
    wv = _pad_cols(kv[:, :, MLA_NOPE:].reshape(MLA_KVL, -1), MLA_HEADS, MLA_V)
    pad_gain = lambda gn: jnp.pad(gn.reshape(1, MLA_QK), ((0, 0), (0, LANE - MLA_QK)))
    return (w_in_ext, _pad_cols(w_uq, MLA_HEADS, MLA_QK), wk, wv, _pad_rows(w_out, MLA_HEADS, MLA_V),
            pad_gain(q_gain), pad_gain(k_gain))


def _mla_fwd(x, g, wts, qlg, kvlg, tabs, tag):
    w_in, w_uq, wk, wv, w_out, qg, kg = wts
    h = _norm_rows(x, g, name=tag + "_norm")
    P = _mm(h, w_in, name=tag + "_in")
    cqn, ckvn = _mla_lora(P, qlg, kvlg, name=tag + "_lora")
    q_pre = _mm(cqn, w_uq, name=tag + "_uq")
    k_pre = _mm(ckvn, wk, name=tag + "_uk")
    ones_lane = jnp.tile((jnp.arange(LANE) == MLA_V).astype(F32), MLA_HEADS)[None, :]
    v = _mm(ckvn, wv, out_dtype=BF16, bias=ones_lane, name=tag + "_uv")
    q, k = _mla_qk(q_pre, k_pre, P, tabs, qg, kg, name=tag + "_qk")
    o, lse = _attn_fwd(q, k, v, sb=False, causal=True, heads=MLA_HEADS, dq=LANE, dv=LANE, sum_lane=MLA_V,
                       name=tag + "_attn")
    y = _mm(o, w_out, residual=x, name=tag + "_out")
    return y, (x, h, P, cqn, ckvn, q_pre, k_pre, q, k, v, o, lse)


def _mla_bwd(dy, saved, g, wts, qlg, kvlg, tabs, tag):
    w_in, w_uq, wk, wv, w_out, qg, kg = wts
    x, h, P, cqn, ckvn, q_pre, k_pre, q, k, v, o, lse = saved
    do = _mm(dy, w_out, tb=True, name=tag + "_do")
    dw_out = _unpad_rows(_mm(o, dy, ta=True, name=tag + "_dwo"), MLA_HEADS, MLA_V)
    dq, dk, dv = _attn_bwd(q, k, v, o, do, lse, sb=False, causal=True, heads=MLA_HEADS, dq=LANE, dv=LANE,
                           name=tag + "_attn_bwd")
    dq_pre, dk_pre, dkr, dqg, dkg = _mla_qk_bwd(dq, dk, q_pre, k_pre, P, tabs, qg, kg, name=tag + "_qk_bwd")
    dcqn = _mm(dq_pre, w_uq, tb=True, name=tag + "_dcq")
    dckvn = _mm(dk_pre, wk, tb=True, name=tag + "_dckv_k")
    dckvn = _mm(dv, wv, tb=True, residual=dckvn, name=tag + "_dckv_v")
    dw_uq = _unpad_cols(_mm(cqn, dq_pre, ta=True, name=tag + "_dwuq"), MLA_HEADS, MLA_QK)
    dwk = _unpad_cols(_mm(ckvn, dk_pre, ta=True, name=tag + "_dwk"), MLA_HEADS, MLA_NOPE)
    dwv = _unpad_cols(_mm(ckvn, dv, ta=True, name=tag + "_dwv"), MLA_HEADS, MLA_V)
    dw_ukv = jnp.concatenate([dwk.reshape(MLA_KVL, MLA_HEADS, MLA_NOPE), dwv.reshape(MLA_KVL, MLA_HEADS, MLA_V)],
                             axis=2).reshape(MLA_KVL, -1)
    dP, dqlg, dkvlg = _mla_lora_bwd(dcqn, dckvn, dkr, P, qlg, kvlg, name=tag + "_lora_bwd")
    dh = _mm(dP, w_in, tb=True, name=tag + "_dh")
    dw_in_ext = _mm(h, dP, ta=True, name=tag + "_dwin")
    lat = MLA_QL + MLA_KVL
    dw_in = jnp.concatenate([dw_in_ext[:, :lat], dw_in_ext[:, lat + MLA_NOPE:lat + MLA_QK]], axis=1)
    dx, dg = _norm_rows_bwd(dh, x, g, dy, name=tag + "_dnorm")
    return dx, dict(mix_norm=dg, mla_w_in=dw_in, mla_q_lora_gain=dqlg, mla_kv_lora_gain=dkvlg, mla_w_uq=dw_uq,
                    mla_w_ukv=dw_ukv, mla_q_gain=dqg[:, :MLA_QK], mla_k_gain=dkg[:, :MLA_QK], mla_w_out=dw_out)


def _xmem_fwd(x, mem, g, gm, wq, wkv, qg, kg, wo, tag):
    hq = _norm_rows(x, g, name=tag + "_norm")
    hm = _norm_rows(mem, gm, name=tag + "_mnorm")
    qp = _mm(hq, wq, name=tag + "_q")
    kv = _mm(hm, wkv, name=tag + "_kv")
    q = _head_norm(qp, qg, heads=MEM_HEADS, width=MEM_HD, scale=MEM_SCALE, name=tag + "_qn")
    kn = _head_norm(kv, kg, heads=MEM_HEADS, width=MEM_HD, colfn=lambda hh: 2 * hh, name=tag + "_kn")
    kvb = kv.astype(BF16)
    o, lse = _attn_fwd(q, kn, kvb, sb=False, causal=False, heads=MEM_HEADS, dq=MEM_HD, dv=MEM_HD,
                       vcol=lambda hh: 2 * hh + 1, name=tag + "_attn")
    y = _mm(o, wo, residual=x, name=tag + "_out")
    return y, (x, hq, hm, qp, kv, q, kn, kvb, o, lse)


def _xmem_bwd(dy, saved, mem, g, gm, wq, wkv, qg, kg, wo, tag):
    x, hq, hm, qp, kv, q, kn, kvb, o, lse = saved
    m = mem.shape[0]
    do = _mm(dy, wo, tb=True, name=tag + "_do")
    dwo = _mm(o, dy, ta=True, name=tag + "_dwo")
    dq, dk, dv = _attn_bwd(q, kn, kvb, o, do, lse, sb=False, causal=False, heads=MEM_HEADS, dq=MEM_HD, dv=MEM_HD,
                           vcol=lambda hh: 2 * hh + 1, name=tag + "_attn_bwd")
    dqp, dqg = _head_norm_bwd(dq, qp, qg, heads=MEM_HEADS, width=MEM_HD, scale=MEM_SCALE, out_dtype=BF16,
                              name=tag + "_qn_bwd")
    dkp, dkg = _head_norm_bwd(dk, kv, kg, heads=MEM_HEADS, width=MEM_HD, colfn=lambda hh: 2 * hh, out_dtype=F32,
                              name=tag + "_kn_bwd")
    dkv = jnp.concatenate([dkp.reshape(m, MEM_HEADS, MEM_HD), dv.reshape(m, MEM_HEADS, MEM_HD)], axis=2).reshape(m, -1)
    dwkv = _mm(hm, dkv, ta=True, name=tag + "_dwkv")
    dhm = _mm(dkv, wkv, tb=True, name=tag + "_dhm")
    _, dgm = _norm_rows_bwd(dhm, mem, gm, None, name=tag + "_dmnorm")
    dwq = _mm(hq, dqp, ta=True, name=tag + "_dwq")
    dhq = _mm(dqp, wq, tb=True, name=tag + "_dhq")
    dx, dg = _norm_rows_bwd(dhq, x, g, dy, name=tag + "_dnorm")
    return dx, dict(xmem_norm=dg, xmem_mem_norm=dgm, xmem_wq=dwq, xmem_wkv=dwkv, xmem_q_gain=dqg, xmem_k_gain=dkg,
                    xmem_wo=dwo)


def _local_step(x, mem, positions, tgt, w):
    tabs = _rope_tables(positions)
    even = _even_weights(w["sbg_w_in"][0], w["sbg_w_out"][0])
    mla = _mla_weights(w["mla_w_in"][0], w["mla_w_uq"][0], w["mla_w_ukv"][0], w["mla_w_out"][0], w["mla_q_gain"][0],
                       w["mla_k_gain"][0])
    bt = jnp.repeat(w["sgu_b"][0].T, SG_GD, axis=1)
    saved = []
    for l in range(2):
        t = f"l{l}"
        x, s_pre = _ffn_fwd(x, w["ffn_pre_norm"][l], w["ffn_pre_w_gu"][l], w["ffn_pre_w_down"][l], t + "_pre")
        if l == 0:
            x, s_mix = _even_fwd(x, w["mix_norm"][0], even, w["sgu_ln_gain"][0], w["sgu_ln_bias"][0], w["sgu_w"][0], bt,
                                 t + "_even")
        else:
            x, s_mix = _mla_fwd(x, w["mix_norm"][1], mla, w["mla_q_lora_gain"][0], w["mla_kv_lora_gain"][0], tabs,
                                t + "_mla")
        x, s_xm = _xmem_fwd(x, mem, w["xmem_norm"][l], w["xmem_mem_norm"][l], w["xmem_wq"][l], w["xmem_wkv"][l],
                            w["xmem_q_gain"][l], w["xmem_k_gain"][l], w["xmem_wo"][l], t + "_xm")
        x, s_post = _ffn_fwd(x, w["ffn_post_norm"][l], w["ffn_post_w_gu"][l], w["ffn_post_w_down"][l], t + "_post")
        saved.append((s_pre, s_mix, s_xm, s_post))
    loss, dx = _loss_grad(x, tgt, name="loss")
    grads = {}

    def put(name, l, val):
        grads.setdefault(name, {})[l] = val

    for l in (1, 0):
        t = f"l{l}"
        s_pre, s_mix, s_xm, s_post = saved[l]
        dx, dg, dwgu, dwd = _ffn_bwd(dx, s_post, w["ffn_post_norm"][l], w["ffn_post_w_gu"][l], w["ffn_post_w_down"][l],
                                     t + "_post")
        put("ffn_post_norm", l, dg), put("ffn_post_w_gu", l, dwgu), put("ffn_post_w_down", l, dwd)
        dx, gx = _xmem_bwd(dx, s_xm, mem, w["xmem_norm"][l], w["xmem_mem_norm"][l], w["xmem_wq"][l], w["xmem_wkv"][l],
                           w["xmem_q_gain"][l], w["xmem_k_gain"][l], w["xmem_wo"][l], t + "_xm")
        for k_, v_ in gx.items():
            put(k_, l, v_)
        if l == 0:
            dx, gm = _even_bwd(dx, s_mix, w["mix_norm"][0], even, w["sgu_ln_gain"][0], w["sgu_w"][0], bt, t + "_even")
        else:
            dx, gm = _mla_bwd(dx, s_mix, w["mix_norm"][1], mla, w["mla_q_lora_gain"][0], w["mla_kv_lora_gain"][0], tabs,
                              t + "_mla")
        for k_, v_ in gm.items():
            put(k_, l if k_ == "mix_norm" else 0, v_)
        dx, dg, dwgu, dwd = _ffn_bwd(dx, s_pre, w["ffn_pre_norm"][l], w["ffn_pre_w_gu"][l], w["ffn_pre_w_down"][l],
                                     t + "_pre")
        put("ffn_pre_norm", l, dg), put("ffn_pre_w_gu", l, dwgu), put("ffn_pre_w_down", l, dwd)
    return loss, dx, {k_: [v_[l] for l in sorted(v_)] for k_, v_ in grads.items()}


N_CHIPS = 4
PACK_COLS = 1024
PACK_ROW_MULTIPLE = 512


def _place():
    x, y, c = lax.axis_index("x"), lax.axis_index("y"), lax.axis_index("c")
    return x, y, c, [(1 - x, y), (x, 1 - y), (1 - x, 1 - y)]


def _hops(x, y, c):
    return ((x + 1 - c) % 2, (y + c) % 2), ((x + c) % 2, (y + 1 - c) % 2), (1 - x, 1 - y)


def _gather_chips(shard):
    R, C = shard.shape
    Rh = R // 2

    def body(x_ref, out_ref, send_sems, recv_sems, local_sem):
        x, y, c = lax.axis_index("x"), lax.axis_index("y"), lax.axis_index("c")
        n1, n2, nd = _hops(x, y, c)
        me, q1, q2, qd = 2 * x + y, 2 * n1[0] + n1[1], 2 * n2[0] + n2[1], 2 * nd[0] + nd[1]

        def half(chip, core):
            return out_ref.at[chip, pl.ds(core * Rh, Rh), :]

        def copy(k, chip, core, to, src=None):
            return pltpu.make_async_remote_copy(src_ref=half(chip, core) if src is None else src, dst_ref=half(chip, core),
                                                send_sem=send_sems.at[k], recv_sem=recv_sems.at[k], device_id=to,
                                                device_id_type=MESH)

        own = x_ref.at[pl.ds(c * Rh, Rh), :]
        sibling = (x, y, 1 - c)
        mine = pltpu.make_async_copy(x_ref, out_ref.at[me], local_sem)
        mine.start()
        sends = [copy(0, me, c, (*n1, c), src=own), copy(1, me, c, (*n2, c), src=own)]
        sends[0].start()
        sends[1].start()
        copy(0, q1, c, sibling).wait_recv()
        sends += [copy(2, q1, c, (*n2, c)), copy(3, q1, c, sibling)]
        sends[2].start()
        sends[3].start()
        copy(1, q2, c, sibling).wait_recv()
        sends.append(copy(4, q2, c, sibling))
        sends[4].start()
        copy(2, qd, c, sibling).wait_recv()
        sends.append(copy(5, qd, c, sibling))
        sends[5].start()
        copy(3, q2, 1 - c, sibling).wait_recv()
        copy(4, q1, 1 - c, sibling).wait_recv()
        copy(5, qd, 1 - c, sibling).wait_recv()
        for cp in sends:
            cp.wait_send()
        mine.wait()

    return pl.pallas_call(
        body, name="gather_weights", out_shape=jax.ShapeDtypeStruct((N_CHIPS, R, C), shard.dtype),
        in_specs=[ANY], out_specs=ANY,
        scratch_shapes=[pltpu.SemaphoreType.DMA((6,)), pltpu.SemaphoreType.DMA((6,)), pltpu.SemaphoreType.DMA])(shard)


def _gather_devices(block):
    M, N = block.shape

    def body(x_ref, out_ref, send_sems, recv_sems, local_sem):
        x, y, c, chips = _place()
        me, sibling = (x, y, c), (x, y, 1 - c)

        def rows(px, py, pc):
            return out_ref.at[pl.ds((4 * px + 2 * py + pc) * M, M), :]

        def copy(k, blk, to, src=None):
            return pltpu.make_async_remote_copy(src_ref=rows(*blk) if src is None else src, dst_ref=rows(*blk),
                                                send_sem=send_sems.at[k], recv_sem=recv_sems.at[k], device_id=to,
                                                device_id_type=MESH)

        mine = pltpu.make_async_copy(x_ref, rows(*me), local_sem)
        mine.start()
        first = [copy(0, me, sibling, src=x_ref)]
        first += [copy(1 + j, me, (*chip, c), src=x_ref) for j, chip in enumerate(chips)]
        for cp in first:
            cp.start()
        passed = [copy(4 + j, (*chip, c), sibling) for j, chip in enumerate(chips)]
        for j, chip in enumerate(chips):
            copy(1 + j, (*chip, c), me).wait_recv()
            passed[j].start()
        copy(0, sibling, me).wait_recv()
        for j, chip in enumerate(chips):
            copy(4 + j, (*chip, 1 - c), me).wait_recv()
        for cp in first + passed:
            cp.wait_send()
        mine.wait()

    vmem = pl.BlockSpec(memory_space=pltpu.VMEM)
    return pl.pallas_call(
        body, name=f"gather_devices_{M}", out_shape=jax.ShapeDtypeStruct((8 * M, N), block.dtype),
        in_specs=[vmem], out_specs=vmem,
        scratch_shapes=[pltpu.SemaphoreType.DMA((7,)), pltpu.SemaphoreType.DMA((7,)), pltpu.SemaphoreType.DMA],
        compiler_params=pltpu.CompilerParams(vmem_limit_bytes=VMEM_LIMIT))(block)


def _swap_halves(g):
    n, R, C = g.shape
    Rh = R // 2

    def body(g_ref, a_ref, send_sem, recv_sem):
        x, y, c, _ = _place()
        cp = pltpu.make_async_remote_copy(src_ref=g_ref.at[:, pl.ds((1 - c) * Rh, Rh), :], dst_ref=a_ref,
                                          send_sem=send_sem, recv_sem=recv_sem, device_id=(x, y, 1 - c),
                                          device_id_type=MESH)
        cp.start()
        cp.wait()

    return pl.pallas_call(body, name="grad_swap_halves", out_shape=jax.ShapeDtypeStruct((n, Rh, C), g.dtype),
                          in_specs=[ANY], out_specs=ANY,
                          scratch_shapes=[pltpu.SemaphoreType.DMA, pltpu.SemaphoreType.DMA])(g)


def _add_picked(a, b, picks, *, a_row_half=None, out_dtype, name):
    n_out = picks.shape[0]
    _, rows, C = b.shape
    tr = _row_tile(rows, 512)
    nt = rows // tr
    half = jnp.zeros((1,), jnp.int32) if a_row_half is None else a_row_half

    def body(pick_ref, half_ref, a_ref, b_ref, o_ref):
        o_ref[...] = (a_ref[...].astype(F32) + b_ref[...].astype(F32)).astype(o_ref.dtype)

    spec = pltpu.PrefetchScalarGridSpec(
        num_scalar_prefetch=2, grid=(n_out, nt),
        in_specs=[pl.BlockSpec((1, tr, C), lambda j, i, pick, hf: (pick[j], hf[0] * nt + i, 0)),
                  pl.BlockSpec((1, tr, C), lambda j, i, pick, hf: (pick[j], i, 0))],
        out_specs=pl.BlockSpec((1, tr, C), lambda j, i, pick, hf: (j, i, 0)))
    return pl.pallas_call(body, name=name, grid_spec=spec, out_shape=jax.ShapeDtypeStruct((n_out, rows, C), out_dtype),
                          compiler_params=_params(2))(picks.astype(jnp.int32), half.astype(jnp.int32), a, b)


def _hop_exchange(src, hop, *, name):
    def body(s_ref, d_ref, send_sem, recv_sem):
        x, y, c = lax.axis_index("x"), lax.axis_index("y"), lax.axis_index("c")
        cp = pltpu.make_async_remote_copy(src_ref=s_ref, dst_ref=d_ref, send_sem=send_sem, recv_sem=recv_sem,
                                          device_id=(*_hops(x, y, c)[hop], c), device_id_type=MESH)
        cp.start()
        cp.wait()

    return pl.pallas_call(body, name=name, out_shape=jax.ShapeDtypeStruct(src.shape, src.dtype), in_specs=[ANY],
                          out_specs=ANY, scratch_shapes=[pltpu.SemaphoreType.DMA, pltpu.SemaphoreType.DMA])(src)


def _reduce_over_chips(g):
    x, y, c = lax.axis_index("x"), lax.axis_index("y"), lax.axis_index("c")
    n1, n2, _ = _hops(x, y, c)
    chip = lambda p: 2 * p[0] + p[1]
    near = jnp.stack([chip((x, y)), chip(n2)])
    far = jnp.stack([chip(n1), chip((1 - x, 1 - y))])
    half = c.reshape(1)
    sib = _swap_halves(g)
    kept = _add_picked(g, sib, near, a_row_half=half, out_dtype=F32, name="grad_add_near")
    sent = _add_picked(g, sib, far, a_row_half=half, out_dtype=BF16, name="grad_add_far")
    got = _hop_exchange(sent, 0, name="grad_hop_first")
    mine = _add_picked(kept, got, jnp.zeros((1,), jnp.int32), out_dtype=F32, name="grad_add_mine")
    theirs = _add_picked(kept, got, jnp.ones((1,), jnp.int32), out_dtype=BF16, name="grad_add_theirs")
    got = _hop_exchange(theirs, 1, name="grad_hop_second")
    total = _add_picked(mine, got, jnp.zeros((1,), jnp.int32), out_dtype=F32, name="grad_add_total")
    return _join_halves(total[0])


def _sum_slots(b, *, name):
    n, R, C = b.shape
    tr = _row_tile(R, 512)

    def body(b_ref, o_ref):
        acc = b_ref[0]
        for q in range(1, n):
            acc = acc + b_ref[q]
        o_ref[...] = acc

    return pl.pallas_call(body, name=name, grid=(R // tr,), in_specs=[pl.BlockSpec((n, tr, C), lambda i: (0, i, 0))],
                          out_specs=pl.BlockSpec((tr, C), lambda i: (i, 0)), out_shape=jax.ShapeDtypeStruct((R, C), F32),
                          compiler_params=_params(1))(b)


def _join_halves(r):
    Rh, C = r.shape

    def body(r_ref, o_ref, send_sem, recv_sem, local_sem):
        x, y, c, _ = _place()
        own, other = o_ref.at[pl.ds(c * Rh, Rh), :], o_ref.at[pl.ds((1 - c) * Rh, Rh), :]
        mine = pltpu.make_async_copy(r_ref, own, local_sem)
        mine.start()
        cp = pltpu.make_async_remote_copy(src_ref=r_ref, dst_ref=own, send_sem=send_sem, recv_sem=recv_sem,
                                          device_id=(x, y, 1 - c), device_id_type=MESH)
        cp.start()
        pltpu.make_async_remote_copy(src_ref=r_ref, dst_ref=other, send_sem=send_sem, recv_sem=recv_sem,
                                     device_id=(x, y, 1 - c), device_id_type=MESH).wait_recv()
        cp.wait_send()
        mine.wait()

    return pl.pallas_call(
        body, name="grad_join_halves", out_shape=jax.ShapeDtypeStruct((2 * Rh, C), r.dtype), in_specs=[ANY], out_specs=ANY,
        scratch_shapes=[pltpu.SemaphoreType.DMA, pltpu.SemaphoreType.DMA, pltpu.SemaphoreType.DMA])(r)


def _pack(pieces, cols, row_multiple, dtype):
    flat = jnp.concatenate([p.reshape(-1).astype(dtype) for p in pieces])
    per = cols * row_multiple
    total = -(-flat.shape[0] // per) * per
    return jnp.pad(flat, (0, total - flat.shape[0])).reshape(total // cols, cols)


def _unpack(buf, shapes):
    flat, out, at = buf.reshape(-1), [], 0
    for shp in shapes:
        size = 1
        for d in shp:
            size *= d
        out.append(flat[at:at + size].reshape(shp))
        at += size
    return out


SHARDED = (("ffn_pre_w_gu", 2), ("ffn_pre_w_down", 1), ("sbg_w_in", 2), ("sbg_w_out", 1), ("mla_w_in", 1),
           ("mla_w_uq", 2), ("mla_w_ukv", 2), ("mla_w_out", 1), ("xmem_wq", 1), ("xmem_wkv", 2), ("xmem_wo", 1),
           ("ffn_post_w_gu", 2), ("ffn_post_w_down", 1))
LORA_GAINS = ("mla_q_lora_gain", "mla_kv_lora_gain")
REPLICATED = ("ffn_pre_norm", "mix_norm", "sgu_ln_gain", "sgu_ln_bias", "sgu_w", "sgu_b", "mla_q_gain", "mla_k_gain",
              "xmem_norm", "xmem_mem_norm", "xmem_q_gain", "xmem_k_gain", "ffn_post_norm")
WEIGHTS = ("ffn_pre_norm", "ffn_pre_w_gu", "ffn_pre_w_down", "mix_norm", "sbg_w_in", "sgu_ln_gain", "sgu_ln_bias", "sgu_w",
           "sgu_b", "sbg_w_out", "mla_w_in", "mla_q_lora_gain", "mla_kv_lora_gain", "mla_w_uq", "mla_w_ukv", "mla_q_gain",
           "mla_k_gain", "mla_w_out", "xmem_norm", "xmem_mem_norm", "xmem_wq", "xmem_wkv", "xmem_q_gain", "xmem_k_gain",
           "xmem_wo", "ffn_post_norm", "ffn_post_w_gu", "ffn_post_w_down")
INPUTS = ("x", "mem", "positions") + WEIGHTS + ("loss_target",) + tuple("m_" + n for n in WEIGHTS) + tuple(
    "v_" + n for n in WEIGHTS)


def _step(a):
    x, y, c, _ = _place()
    chip = 2 * x + y
    shard_shapes = [a[n].shape for n, _ in SHARDED]

    gathered = _gather_chips(_pack([a[n] for n, _ in SHARDED], PACK_COLS, PACK_ROW_MULTIPLE, BF16))
    parts = [_unpack(gathered[q], shard_shapes) for q in range(N_CHIPS)]
    w = {n: jnp.concatenate([parts[q][i] for q in range(N_CHIPS)], axis=ax) for i, (n, ax) in enumerate(SHARDED)}
    gains = jnp.zeros((8, LANE), F32)
    for r, n in enumerate(LORA_GAINS):
        gains = gains.at[r, :a[n].shape[1]].set(a[n][0])
    gains = _gather_devices(gains)
    for r, n in enumerate(LORA_GAINS):
        w[n] = jnp.concatenate([gains[16 * q + r, :a[n].shape[1]] for q in range(N_CHIPS)])[None, :]
    for n in REPLICATED:
        w[n] = a[n]

    loss, dx, grads = _local_step(a["x"][0], a["mem"][0], a["positions"][0], a["loss_target"][0], w)
    loss = lax.psum(loss, ("x", "y", "c"))
    small_names = REPLICATED + LORA_GAINS
    full = {n: jnp.stack(grads[n]).reshape(w[n].shape) for n in small_names}

    def cut(n, ax, q):
        size = w[n].shape[ax] // N_CHIPS
        return [lax.slice_in_dim(gl, q * size, (q + 1) * size, axis=ax - 1) for gl in grads[n]]

    g = jnp.stack([_pack([p for n, ax in SHARDED for p in cut(n, ax, q)], PACK_COLS, PACK_ROW_MULTIPLE, F32)
                   for q in range(N_CHIPS)])
    reduced = _reduce_over_chips(g)
    gw = dict(zip([n for n, _ in SHARDED], _unpack(reduced, shard_shapes)))

    small = _pack([full[n] for n in small_names], LANE, 256, F32)
    rows = small.shape[0]
    summed = _sum_slots(_gather_devices(small).reshape(8, rows, LANE), name="grad_sum_devices")
    for n, val in zip(small_names, _unpack(summed, [full[n].shape for n in small_names])):
        if n in LORA_GAINS:
            size = a[n].shape[1]
            val = lax.dynamic_slice_in_dim(val, chip * size, size, axis=1)
        gw[n] = val

    upd = {n: _adamw(a[n], gw[n], a["m_" + n], a["v_" + n], name="adamw_" + n) for n in WEIGHTS}
    return (loss, dx[None], *[gw[n] for n in WEIGHTS], *[upd[n][0] for n in WEIGHTS], *[upd[n][1] for n in WEIGHTS],
            *[upd[n][2] for n in WEIGHTS])


def kernel(x, mem, positions, ffn_pre_norm, ffn_pre_w_gu, ffn_pre_w_down, mix_norm, sbg_w_in, sgu_ln_gain,
           sgu_ln_bias, sgu_w, sgu_b, sbg_w_out, mla_w_in, mla_q_lora_gain, mla_kv_lora_gain, mla_w_uq, mla_w_ukv,
           mla_q_gain, mla_k_gain, mla_w_out, xmem_norm, xmem_mem_norm, xmem_wq, xmem_wkv, xmem_q_gain, xmem_k_gain,
           xmem_wo, ffn_post_norm, ffn_post_w_gu, ffn_post_w_down, loss_target, m_ffn_pre_norm, m_ffn_pre_w_gu,
           m_ffn_pre_w_down, m_mix_norm, m_sbg_w_in, m_sgu_ln_gain, m_sgu_ln_bias, m_sgu_w, m_sgu_b, m_sbg_w_out,
           m_mla_w_in, m_mla_q_lora_gain, m_mla_kv_lora_gain, m_mla_w_uq, m_mla_w_ukv, m_mla_q_gain, m_mla_k_gain,
           m_mla_w_out, m_xmem_norm, m_xmem_mem_norm, m_xmem_wq, m_xmem_wkv, m_xmem_q_gain, m_xmem_k_gain,
           m_xmem_wo, m_ffn_post_norm, m_ffn_post_w_gu, m_ffn_post_w_down, v_ffn_pre_norm, v_ffn_pre_w_gu,
           v_ffn_pre_w_down, v_mix_norm, v_sbg_w_in, v_sgu_ln_gain, v_sgu_ln_bias, v_sgu_w, v_sgu_b, v_sbg_w_out,
           v_mla_w_in, v_mla_q_lora_gain, v_mla_kv_lora_gain, v_mla_w_uq, v_mla_w_ukv, v_mla_q_gain, v_mla_k_gain,
           v_mla_w_out, v_xmem_norm, v_xmem_mem_norm, v_xmem_wq, v_xmem_wkv, v_xmem_q_gain, v_xmem_k_gain,
           v_xmem_wo, v_ffn_post_norm, v_ffn_post_w_gu, v_ffn_post_w_down):
    given = locals()
    return _step({n: given[n] for n in INPUTS})
```

```python
import functools

import jax
import jax.numpy as jnp
from jax import lax
from jax.experimental import pallas as pl
from jax.experimental.pallas import tpu as pltpu

F32, BF16 = jnp.float32, jnp.bfloat16
LANE = 128
VMEM_LIMIT = 56 * 1024 * 1024
EPS = 1e-6
D_FF = 2816
SB_HEADS, SB_HD = 8, 64
SG_GROUPS, SG_GD, SG_CHUNK = 8, 64, 128
SB_W, SG_W = SB_HEADS * SB_HD, SG_GROUPS * SG_GD
MLA_HEADS, MLA_NOPE, MLA_ROPE, MLA_V = 16, 64, 32, 64
MLA_QK = MLA_NOPE + MLA_ROPE
MLA_QL, MLA_KVL = 512, 256
ROPE_THETA = 10000.0
MEM_HEADS, MEM_HD = 4, 256
SB_SCALE, MLA_SCALE, MEM_SCALE = SB_HD ** -0.5, MLA_QK ** -0.5, MEM_HD ** -0.5
ADAM_LR, ADAM_B1, ADAM_B2, ADAM_EPS, ADAM_WD, ADAM_STEP = 0.001, 0.9, 0.999, 1e-08, 0.01, 10
MESH = pl.DeviceIdType.MESH
ANY = pl.BlockSpec(memory_space=pl.ANY)


def _params(n_axes):
    return pltpu.CompilerParams(dimension_semantics=("arbitrary",) * n_axes, vmem_limit_bytes=VMEM_LIMIT)


MM_TILE_CAP = 1408
MM_VMEM_BUDGET = 40 * 1024 * 1024


def _tile(dim, cap):
    if dim <= cap:
        return dim
    best = max(t for t in range(LANE, cap + 1, LANE) if dim % t == 0)
    return best


def _mm(a, b, *, ta=False, tb=False, out_dtype=F32, scale=1.0, residual=None, bias=None, a_off=(0, 0), b_off=(0, 0),
        m=None, n=None, k=None, name):
    am, ak = (a.shape[1], a.shape[0]) if ta else a.shape
    bk, bn = (b.shape[1], b.shape[0]) if tb else b.shape
    M, N, K = m or am, n or bn, k or ak
    tm, tn = _tile(M, MM_TILE_CAP), _tile(N, MM_TILE_CAP)
    fixed = tm * tn * (4 + 2 * jnp.dtype(out_dtype).itemsize + (2 * residual.dtype.itemsize if residual is not None else 0))
    per_k = (tm * (2 * a.dtype.itemsize + 2) + tn * (2 * b.dtype.itemsize + 2))
    tk = _tile(K, max(LANE, (MM_VMEM_BUDGET - fixed) // per_k))
    nm, nn, nk = M // tm, N // tn, K // tk
    a_off = (a_off[0] // (tk if ta else tm), a_off[1] // (tm if ta else tk))
    b_off = (b_off[0] // (tn if tb else tk), b_off[1] // (tk if tb else tn))
    dims = (((0 if ta else 1,), (1 if tb else 0,)), ((), ()))

    def body(*refs):
        a_ref, b_ref = refs[0], refs[1]
        o_ref, acc_ref = refs[-2], refs[-1]
        kk = pl.program_id(2)

        @pl.when(kk == 0)
        def _():
            acc_ref[...] = jnp.zeros_like(acc_ref)

        acc_ref[...] += lax.dot_general(a_ref[...].astype(BF16), b_ref[...].astype(BF16), dims,
                                        preferred_element_type=F32)

        @pl.when(kk == nk - 1)
        def _():
            out = acc_ref[...] * scale
            for extra in refs[2:-2]:
                out = out + extra[...].astype(F32)
            o_ref[...] = out.astype(o_ref.dtype)

    (ao0, ao1), (bo0, bo1) = a_off, b_off
    a_spec = (pl.BlockSpec((tk, tm), lambda i, j, kk: (kk + ao0, i + ao1)) if ta
              else pl.BlockSpec((tm, tk), lambda i, j, kk: (i + ao0, kk + ao1)))
    b_spec = (pl.BlockSpec((tn, tk), lambda i, j, kk: (j + bo0, kk + bo1)) if tb
              else pl.BlockSpec((tk, tn), lambda i, j, kk: (kk + bo0, j + bo1)))
    o_spec = pl.BlockSpec((tm, tn), lambda i, j, kk: (i, j))
    ins, in_specs = [a, b], [a_spec, b_spec]
    if residual is not None:
        ins.append(residual)
        in_specs.append(o_spec)
    if bias is not None:
        ins.append(bias)
        in_specs.append(pl.BlockSpec((1, tn), lambda i, j, kk: (0, j)))
    return pl.pallas_call(
        body, name=name, grid=(nm, nn, nk), in_specs=in_specs, out_specs=o_spec,
        out_shape=jax.ShapeDtypeStruct((M, N), out_dtype),
        scratch_shapes=[pltpu.VMEM((tm, tn), F32)], compiler_params=_params(3))(*ins)


def _mm_swiglu(h, wgu, *, name):
    M, K = h.shape
    F = wgu.shape[1] // 2
    tm, tn, tk = _tile(M, 512), _tile(F, MM_TILE_CAP), _tile(K, 1024)
    nm, nf, nk = M // tm, F // tn, K // tk

    def body(h_ref, wg_ref, wu_ref, g_ref, u_ref, a_ref, accg, accu):
        kk = pl.program_id(2)

        @pl.when(kk == 0)
        def _():
            accg[...] = jnp.zeros_like(accg)
            accu[...] = jnp.zeros_like(accu)

        hb = h_ref[...]
        accg[...] += jnp.dot(hb, wg_ref[...], preferred_element_type=F32)
        accu[...] += jnp.dot(hb, wu_ref[...], preferred_element_type=F32)

        @pl.when(kk == nk - 1)
        def _():
            g, u = accg[...], accu[...]
            g_ref[...] = g.astype(BF16)
            u_ref[...] = u.astype(BF16)
            a_ref[...] = (g * jax.nn.sigmoid(g) * u).astype(BF16)

    o_spec = pl.BlockSpec((tm, tn), lambda i, j, kk: (i, j))
    shp = jax.ShapeDtypeStruct((M, F), BF16)
    return pl.pallas_call(
        body, name=name, grid=(nm, nf, nk),
        in_specs=[pl.BlockSpec((tm, tk), lambda i, j, kk: (i, kk)),
                  pl.BlockSpec((tk, tn), lambda i, j, kk: (kk, j)),
                  pl.BlockSpec((tk, tn), lambda i, j, kk: (kk, j + nf))],
        out_specs=[o_spec, o_spec, o_spec], out_shape=[shp, shp, shp],
        scratch_shapes=[pltpu.VMEM((tm, tn), F32), pltpu.VMEM((tm, tn), F32)],
        compiler_params=_params(3))(h, wgu, wgu)


def _mm_dswiglu(dy, wd, gate, up, *, scale, name):
    M, K = dy.shape
    F = wd.shape[0]
    tm, tn, tk = _tile(M, 512), _tile(F, MM_TILE_CAP), _tile(K, 1024)
    nm, nf, nk = M // tm, F // tn, K // tk

    def body(dy_ref, wd_ref, g_ref, u_ref, dg_ref, du_ref, acc):
        kk = pl.program_id(2)

        @pl.when(kk == 0)
        def _():
            acc[...] = jnp.zeros_like(acc)

        acc[...] += lax.dot_general(dy_ref[...].astype(BF16), wd_ref[...], (((1,), (1,)), ((), ())),
                                    preferred_element_type=F32)

        @pl.when(kk == nk - 1)
        def _():
            da = acc[...] * scale
            g, u = g_ref[...].astype(F32), u_ref[...].astype(F32)
            sg = jax.nn.sigmoid(g)
            du_ref[...] = (da * g * sg).astype(BF16)
            dg_ref[...] = (da * u * sg * (1.0 + g * (1.0 - sg))).astype(BF16)

    o_spec = pl.BlockSpec((tm, tn), lambda i, j, kk: (i, j))
    shp = jax.ShapeDtypeStruct((M, F), BF16)
    return pl.pallas_call(
        body, name=name, grid=(nm, nf, nk),
        in_specs=[pl.BlockSpec((tm, tk), lambda i, j, kk: (i, kk)),
                  pl.BlockSpec((tn, tk), lambda i, j, kk: (j, kk)), o_spec, o_spec],
        out_specs=[o_spec, o_spec], out_shape=[shp, shp],
        scratch_shapes=[pltpu.VMEM((tm, tn), F32)], compiler_params=_params(3))(dy, wd, gate, up)


HEAD_ROWS = 1024


def _row_tile(rows, cap):
    t = cap
    while t >= 8:
        if rows % t == 0:
            return t
        t //= 2
    return rows


def _rowwise(fn, rows, consts, outs, sums=(), hsums=(), *, heads=None, tm=256, name):
    rows = [r if isinstance(r, tuple) else (r, r.shape[1], None) for r in rows]
    S = rows[0][0].shape[0]
    tm = _row_tile(S, tm)
    nh = heads or 1
    n_r, n_c, n_o, n_h, n_s = len(rows), len(consts), len(outs), len(hsums), len(sums)

    def body(*refs):
        r = [x[...] for x in refs[:n_r]]
        c = [x[...] for x in refs[n_r:n_r + n_c]]
        o_refs = refs[n_r + n_c:n_r + n_c + n_o]
        h_refs = refs[n_r + n_c + n_o:n_r + n_c + n_o + n_h]
        s_refs = refs[n_r + n_c + n_o + n_h:]
        res = fn(*r, *c)
        res = res if isinstance(res, (tuple, list)) else (res,)
        for ref, val in zip(o_refs, res[:n_o]):
            ref[...] = val.astype(ref.dtype)
        if n_h:
            @pl.when(pl.program_id(1) == 0)
            def _():
                for ref in h_refs:
                    ref[...] = jnp.zeros_like(ref)
            for ref, val in zip(h_refs, res[n_o:n_o + n_h]):
                ref[...] += val
        if n_s:
            @pl.when((pl.program_id(0) == 0) & (pl.program_id(1) == 0))
            def _():
                for ref in s_refs:
                    ref[...] = jnp.zeros_like(ref)
            for ref, val in zip(s_refs, res[n_o + n_h:]):
                ref[...] += val

    def col(colfn):
        return (lambda i, h: (i, 0)) if colfn is None else (lambda i, h: (i, colfn(h)))

    in_specs = [pl.BlockSpec((tm, w), col(cf)) for _, w, cf in rows]
    in_specs += [pl.BlockSpec(a.shape, lambda i, h, nd=a.ndim: (0,) * nd) for a in consts]
    out_specs = [pl.BlockSpec((tm, w // nh), (lambda i, h: (i, h)) if heads else (lambda i, h: (i, 0))) for w, _ in outs]
    out_specs += [pl.BlockSpec((tm, w), lambda i, h: (i, 0)) for w in hsums]
    out_specs += [pl.BlockSpec(sh, lambda i, h, nd=len(sh): (0,) * nd) for sh in sums]
    out_shape = [jax.ShapeDtypeStruct((S, w), dt) for w, dt in outs]
    out_shape += [jax.ShapeDtypeStruct((S, w), F32) for w in hsums]
    out_shape += [jax.ShapeDtypeStruct(sh, F32) for sh in sums]
    return pl.pallas_call(body, name=name, grid=(S // tm, nh), in_specs=in_specs, out_specs=out_specs,
                          out_shape=out_shape, compiler_params=_params(2))(*[a for a, _, _ in rows], *consts)


def _rms(x, width=None):
    width = width or x.shape[-1]
    return lax.rsqrt(jnp.sum(x * x, axis=-1, keepdims=True) * (1.0 / width) + EPS)


def _rmsnorm_fwd(x, g, width=None):
    return x * _rms(x, width) * g


def _rmsnorm_bwd(dy, x, g, width=None):
    width = width or x.shape[-1]
    r = _rms(x, width)
    xn = x * r
    dxn = dy * g
    dx = r * (dxn - xn * (jnp.sum(dxn * xn, axis=-1, keepdims=True) * (1.0 / width)))
    return dx, jnp.sum(dy * xn, axis=0, keepdims=True)


def _norm_rows(x, g, *, name, out_dtype=BF16):
    D = x.shape[1]
    return _rowwise(lambda xv, gv: _rmsnorm_fwd(xv.astype(F32), gv), [x], [g.reshape(1, D)], [(D, out_dtype)],
                    name=name)[0]


def _norm_rows_bwd(dh, x, g, dres, *, name):
    D = x.shape[1]

    def fn(dhv, xv, *rest):
        dx, dg = _rmsnorm_bwd(dhv.astype(F32), xv, rest[-1])
        return (dx + rest[0] if dres is not None else dx), dg

    rows = [dh, x] + ([dres] if dres is not None else [])
    return _rowwise(fn, rows, [g.reshape(1, D)], [(D, F32)], [(1, D)], name=name)


def _softplus(z):
    return jnp.maximum(z, 0.0) + jnp.log(1.0 + jnp.exp(-jnp.abs(z)))


def _running_sum(v, u, split=True):
    hi = v.astype(BF16)
    out = jnp.dot(hi, u, preferred_element_type=F32)
    if split:
        out = out + jnp.dot((v - hi.astype(F32)).astype(BF16), u, preferred_element_type=F32)
    return out


def _triangle(tk, inclusive_prefix):
    j, s = lax.broadcasted_iota(jnp.int32, (tk, tk), 0), lax.broadcasted_iota(jnp.int32, (tk, tk), 1)
    return ((j <= s) if inclusive_prefix else (j > s)).astype(BF16)


def _nt(a, b):
    return lax.dot_general(a, b, (((1,), (1,)), ((), ())), preferred_element_type=F32)


def _tn(a, b):
    return lax.dot_general(a, b, (((0,), (0,)), ((), ())), preferred_element_type=F32)


ATT_TQ, ATT_TK = 512, 512
SB_SUB = 256


def _attn_fwd(q, k, v, *, sb, causal, heads, dq, dv, kcol=None, vcol=None, sum_lane=None, name):
    S, Sk = q.shape[0], k.shape[0]
    tq, tk = min(ATT_TQ, S), min(ATT_TK, Sk)
    assert tq == tk or not causal
    sub = min(SB_SUB, tk) if sb else tk
    nsub = tk // sub
    kcol = kcol or (lambda h: h)
    vcol = vcol or (lambda h: h)

    def body(*refs):
        if sb:
            q_ref, k_ref, v_ref, u_ref, o_ref, lse_ref, acc_ref, r_ref = refs
            r_ref[...] = jnp.zeros_like(r_ref)
        else:
            q_ref, k_ref, v_ref, o_ref, lse_ref, acc_ref, m_ref, l_ref = refs
            m_ref[...] = jnp.full_like(m_ref, -1e30)
            l_ref[...] = jnp.zeros_like(l_ref)
        first_row = pl.program_id(1) * tq
        qb = q_ref[...]
        acc_ref[...] = jnp.zeros_like(acc_ref)
        nblk = (first_row + tq + tk - 1) // tk if causal else Sk // tk
        nfull = (first_row + (0 if sb else 1)) // tk if causal else nblk

        def piece(off, masked):
            kb, vb = k_ref[pl.ds(off, sub), :], v_ref[pl.ds(off, sub), :]
            s = _nt(qb, kb)
            sp = _softplus(s)
            if masked:
                valid = (off + lax.broadcasted_iota(jnp.int32, (tq, sub), 1)
                         < first_row + lax.broadcasted_iota(jnp.int32, (tq, sub), 0))
            ls = jnp.where(valid, -sp, 0.0) if masked else -sp
            w = jnp.exp(s - sp + r_ref[...] + _running_sum(ls, u_ref[...]))
            if masked:
                w = jnp.where(valid, w, 0.0)
            acc_ref[...] += jnp.dot(w.astype(BF16), vb, preferred_element_type=F32)
            r_ref[...] += jnp.sum(ls, axis=1, keepdims=True)

        def tile(jj, masked):
            for cidx in reversed(range(nsub)):
                off = pl.multiple_of(jj * tk + cidx * sub, sub)
                if masked and nsub > 1:
                    @pl.when(off < first_row + tq)
                    def _():
                        piece(off, True)
                else:
                    piece(off, masked)

        def sweep(lo, hi, masked):
            def step(t, carry):
                tile(hi - 1 - t, masked)
                return carry
            lax.fori_loop(0, hi - lo, step, 0)

        def scores(jj):
            return _nt(qb, k_ref[pl.ds(pl.multiple_of(jj * tk, tk), tk), :])

        def softmax_tile(jj, s, masked):
            off = pl.multiple_of(jj * tk, tk)
            vb = v_ref[pl.ds(off, tk), :]
            if masked:
                qpos = first_row + lax.broadcasted_iota(jnp.int32, (tq, tk), 0)
                kpos = off + lax.broadcasted_iota(jnp.int32, (tq, tk), 1)
                s = jnp.where(kpos <= qpos, s, -1e30)
            m_old = m_ref[...]
            m_new = jnp.maximum(m_old, jnp.max(s, axis=1, keepdims=True))
            p = jnp.exp(s - m_new)
            alpha = jnp.exp(m_old - m_new)
            if sum_lane is None:
                l_ref[...] = alpha * l_ref[...] + jnp.sum(p, axis=1, keepdims=True)
            acc_ref[...] = alpha * acc_ref[...] + jnp.dot(p.astype(BF16), vb, preferred_element_type=F32)
            m_ref[...] = m_new

        if sb:
            sweep(nfull, nblk, True)
            sweep(0, nfull, False)
        else:
            n_loop = nfull if causal else nblk - 1

            def step(t, s_cur):
                s_next = scores(jnp.minimum(t + 1, nblk - 1))
                softmax_tile(t, s_cur, False)
                return s_next

            softmax_tile(n_loop, lax.fori_loop(0, n_loop, step, scores(0)), causal)
        if sb:
            o_ref[...] = acc_ref[...]
            lse_ref[0] = r_ref[...]
        else:
            acc = acc_ref[...]
            l = l_ref[...] if sum_lane is None else acc[:, sum_lane:sum_lane + 1]
            o_ref[...] = acc / l
            lse_ref[0] = m_ref[...] + jnp.log(l)

    in_specs = [pl.BlockSpec((tq, dq), lambda h, i: (i, h)),
                pl.BlockSpec((Sk, dq), lambda h, i: (0, kcol(h))),
                pl.BlockSpec((Sk, dv), lambda h, i: (0, vcol(h)))]
    ins = [q, k, v]
    scratch = [pltpu.VMEM((tq, dv), F32), pltpu.VMEM((tq, 1), F32)]
    if sb:
        ins.append(_triangle(sub, inclusive_prefix=False))
        in_specs.append(pl.BlockSpec((sub, sub), lambda h, i: (0, 0)))
    else:
        scratch.append(pltpu.VMEM((tq, 1), F32))
    out_specs = [pl.BlockSpec((tq, dv), lambda h, i: (i, h)), pl.BlockSpec((1, tq, 1), lambda h, i: (h, i, 0))]
    out_shape = [jax.ShapeDtypeStruct((S, heads * dv), F32), jax.ShapeDtypeStruct((heads, S, 1), F32)]
    return pl.pallas_call(body, name=name, grid=(heads, S // tq), in_specs=in_specs, out_specs=out_specs,
                          out_shape=out_shape, scratch_shapes=scratch, compiler_params=_params(2))(*ins)


def _attn_bwd(q, k, v, o, do, lse, *, sb, causal, heads, dq, dv, kcol=None, vcol=None, name):
    S, Sk = q.shape[0], k.shape[0]
    tq, tk = min(ATT_TQ, S), min(ATT_TK, Sk)
    sub = min(SB_SUB, tk) if sb else tk
    nsub = tk // sub
    nq = S // tq
    kcol = kcol or (lambda h: h)
    vcol = vcol or (lambda h: h)

    def body(*refs):
        if sb:
            q_ref, k_ref, v_ref, o_ref, do_ref, lse_ref, u_ref, dq_ref, dk_ref, dv_ref, acc_ref, r_ref, re_ref = refs
            r_ref[...] = jnp.zeros_like(r_ref)
            re_ref[...] = jnp.zeros_like(re_ref)
        else:
            q_ref, k_ref, v_ref, o_ref, do_ref, lse_ref, dq_ref, dk_ref, dv_ref, acc_ref = refs
        first_row = pl.program_id(1) * tq

        @pl.when(first_row == 0)
        def _():
            dk_ref[...] = jnp.zeros_like(dk_ref)
            dv_ref[...] = jnp.zeros_like(dv_ref)

        qb = q_ref[...]
        dof = do_ref[...].astype(F32)
        dob = dof.astype(BF16)
        if not sb:
            dlt = jnp.sum(dof * o_ref[...], axis=1, keepdims=True)
        acc_ref[...] = jnp.zeros_like(acc_ref)
        nblk = (first_row + tq + tk - 1) // tk if causal else Sk // tk
        nfull = (first_row + (0 if sb else 1)) // tk if causal else nblk

        def piece(off, masked):
            kb, vb = k_ref[pl.ds(off, sub), :], v_ref[pl.ds(off, sub), :]
            s = _nt(qb, kb)
            dp = _nt(dob, vb)
            if masked:
                qpos = first_row + lax.broadcasted_iota(jnp.int32, (tq, sub), 0)
                kpos = off + lax.broadcasted_iota(jnp.int32, (tq, sub), 1)
                valid = (kpos < qpos) if sb else (kpos <= qpos)
            if sb:
                u = u_ref[...]
                sp = _softplus(s)
                ls = jnp.where(valid, -sp, 0.0) if masked else -sp
                lb = s - sp
                w = jnp.exp(lb + (lse_ref[0] - (r_ref[...] + _running_sum(ls, u))))
                if masked:
                    w = jnp.where(valid, w, 0.0)
                e = dp * w
                ds = e - jnp.exp(lb) * (re_ref[...] + _running_sum(e, u, split=False))
                if masked:
                    ds = jnp.where(valid, ds, 0.0)
                r_ref[...] += jnp.sum(ls, axis=1, keepdims=True)
                re_ref[...] += jnp.sum(e, axis=1, keepdims=True)
            else:
                w = jnp.exp(s - lse_ref[0])
                if masked:
                    w = jnp.where(valid, w, 0.0)
                ds = w * (dp - dlt)
            dsb = ds.astype(BF16)
            dv_ref[pl.ds(off, sub), :] += _tn(w.astype(BF16), dob)
            dk_ref[pl.ds(off, sub), :] += _tn(dsb, qb)
            acc_ref[...] += jnp.dot(dsb, kb, preferred_element_type=F32)

        def tile(jj, masked):
            for cidx in range(nsub):
                off = pl.multiple_of(jj * tk + cidx * sub, sub)
                if masked and nsub > 1:
                    @pl.when(off < first_row + tq)
                    def _():
                        piece(off, True)
                else:
                    piece(off, masked)

        def sweep(lo, hi, masked):
            def step(t, carry):
                tile(lo + t, masked)
                return carry
            lax.fori_loop(0, hi - lo, step, 0)

        sweep(0, nfull, False)
        if causal:
            sweep(nfull, nblk, True)
        dq_ref[...] = acc_ref[...]

    ins = [q, k, v, o, do]
    in_specs = [pl.BlockSpec((tq, dq), lambda h, i: (i, h)),
                pl.BlockSpec((Sk, dq), lambda h, i: (0, kcol(h))),
                pl.BlockSpec((Sk, dv), lambda h, i: (0, vcol(h))),
                pl.BlockSpec((tq, dv), lambda h, i: (i, h)),
                pl.BlockSpec((tq, dv), lambda h, i: (i, h))]
    scratch = [pltpu.VMEM((tq, dq), F32)]
    ins.append(lse)
    in_specs.append(pl.BlockSpec((1, tq, 1), lambda h, i: (h, i, 0)))
    if sb:
        ins.append(_triangle(sub, inclusive_prefix=True))
        in_specs.append(pl.BlockSpec((sub, sub), lambda h, i: (0, 0)))
        scratch += [pltpu.VMEM((tq, 1), F32), pltpu.VMEM((tq, 1), F32)]
    out_specs = [pl.BlockSpec((tq, dq), lambda h, i: (i, h)),
                 pl.BlockSpec((Sk, dq), lambda h, i: (0, h)),
                 pl.BlockSpec((Sk, dv), lambda h, i: (0, h))]
    out_shape = [jax.ShapeDtypeStruct((S, heads * dq), F32), jax.ShapeDtypeStruct((Sk, heads * dq), F32),
                 jax.ShapeDtypeStruct((Sk, heads * dv), F32)]
    return pl.pallas_call(body, name=name, grid=(heads, nq), in_specs=in_specs, out_specs=out_specs,
                          out_shape=out_shape, scratch_shapes=scratch, compiler_params=_params(2))(*ins)


GELU_C = 0.7978845608028654
assert 2 * SG_GD == LANE and SG_CHUNK == LANE


def _gelu(z):
    t = jnp.tanh(GELU_C * (z + 0.044715 * z * z * z))
    return 0.5 * z * (1.0 + t), t


def _gelu_grad(z, t):
    return 0.5 * (1.0 + t) + 0.5 * z * (1.0 - t * t) * GELU_C * (1.0 + 3.0 * 0.044715 * z * z)


def _layernorm_parts(g):
    d = g - jnp.mean(g, axis=-1, keepdims=True)
    rstd = lax.rsqrt(jnp.mean(d * d, axis=-1, keepdims=True) + EPS)
    return d * rstd, rstd


def _gelu_ln(z, gain, bias, *, name):
    def fn(zv, gn, bs):
        a, _ = _gelu(zv)
        y, _ = _layernorm_parts(a[:, SG_W:])
        return a[:, :SG_W], y * gn + bs

    return _rowwise(fn, [z], [gain.reshape(1, SG_W), bias.reshape(1, SG_W)], [(SG_W, F32), (SG_W, BF16)], name=name)


def _gelu_ln_bwd(z, du, dgl, gain, *, name):
    def fn(zv, duv, dglv, gn):
        a, t = _gelu(zv)
        y, rstd = _layernorm_parts(a[:, SG_W:])
        dy = dglv * gn
        dgg = rstd * (dy - jnp.mean(dy, axis=-1, keepdims=True) - y * jnp.mean(dy * y, axis=-1, keepdims=True))
        dz = jnp.concatenate([duv, dgg], axis=1) * _gelu_grad(zv, t)
        return dz, jnp.sum(dglv * y, axis=0, keepdims=True), jnp.sum(dglv, axis=0, keepdims=True)

    return _rowwise(fn, [z, du, dgl], [gain.reshape(1, SG_W)], [(2 * SG_W, BF16)], [(1, SG_W), (1, SG_W)], name=name)


def _sg_masks():
    tri = lax.broadcasted_iota(jnp.int32, (SG_CHUNK, SG_CHUNK), 0) >= lax.broadcasted_iota(jnp.int32, (SG_CHUNK, SG_CHUNK), 1)
    first = lax.broadcasted_iota(jnp.int32, (SG_CHUNK, LANE), 1) < SG_GD
    return tri, first


def _spatial(gl, u, w, bt, *, name):
    S = gl.shape[0]
    tm = _row_tile(S, 512)
    nch = tm // SG_CHUNK

    def body(gl_ref, u_ref, w_ref, bt_ref, o_ref):
        tri, first = _sg_masks()
        for p in range(SG_W // LANE):
            cols = slice(p * LANE, (p + 1) * LANE)
            wa = jnp.where(tri, w_ref[2 * p], 0.0).astype(BF16)
            wb = jnp.where(tri, w_ref[2 * p + 1], 0.0).astype(BF16)
            for ci in range(nch):
                rws = slice(ci * SG_CHUNK, (ci + 1) * SG_CHUNK)
                g = gl_ref[rws, cols]
                zero = jnp.zeros_like(g)
                mixed = (jnp.dot(wa, jnp.where(first, g, zero), preferred_element_type=F32)
                         + jnp.dot(wb, jnp.where(first, zero, g), preferred_element_type=F32) + bt_ref[:, cols])
                o_ref[rws, cols] = u_ref[rws, cols] * mixed

    row = pl.BlockSpec((tm, SG_W), lambda i: (i, 0))
    return pl.pallas_call(
        body, name=name, grid=(S // tm,),
        in_specs=[row, row, pl.BlockSpec(w.shape, lambda i: (0, 0, 0)), pl.BlockSpec(bt.shape, lambda i: (0, 0))],
        out_specs=row, out_shape=jax.ShapeDtypeStruct((S, SG_W), F32), compiler_params=_params(1))(gl, u, w, bt)


def _spatial_bwd(d_o, gl, u, w, bt, *, name):
    S = gl.shape[0]
    tm = _row_tile(S, 512)
    nch = tm // SG_CHUNK
    nsteps = S // tm

    def body(do_ref, gl_ref, u_ref, w_ref, bt_ref, du_ref, dgl_ref, dw_ref, db_ref, dbt_ref):
        tri, first = _sg_masks()
        step = pl.program_id(0)

        @pl.when(step == 0)
        def _():
            dw_ref[...] = jnp.zeros_like(dw_ref)
            dbt_ref[...] = jnp.zeros_like(dbt_ref)

        for p in range(SG_W // LANE):
            cols = slice(p * LANE, (p + 1) * LANE)
            wa = jnp.where(tri, w_ref[2 * p], 0.0).astype(BF16)
            wb = jnp.where(tri, w_ref[2 * p + 1], 0.0).astype(BF16)
            for ci in range(nch):
                rws = slice(ci * SG_CHUNK, (ci + 1) * SG_CHUNK)
                g = gl_ref[rws, cols]
                zero = jnp.zeros_like(g)
                mixed = (jnp.dot(wa, jnp.where(first, g, zero), preferred_element_type=F32)
                         + jnp.dot(wb, jnp.where(first, zero, g), preferred_element_type=F32) + bt_ref[:, cols])
                dov = do_ref[rws, cols]
                du_ref[rws, cols] = dov * mixed
                dm = dov * u_ref[rws, cols]
                dbt_ref[:, cols] += dm
                dma = jnp.where(first, dm, 0.0).astype(BF16)
                dmb = jnp.where(first, 0.0, dm).astype(BF16)
                dw_ref[2 * p] += jnp.where(tri, _nt(dma, g), 0.0)
                dw_ref[2 * p + 1] += jnp.where(tri, _nt(dmb, g), 0.0)
                dgl_ref[rws, cols] = _tn(wa, dma) + _tn(wb, dmb)

        @pl.when(step == nsteps - 1)
        def _():
            lane = lax.broadcasted_iota(jnp.int32, (SG_CHUNK, LANE), 1)
            acc = jnp.zeros((SG_CHUNK, LANE), F32)
            for p in range(SG_W // LANE):
                blk = dbt_ref[:, p * LANE:(p + 1) * LANE]
                sa = jnp.sum(jnp.where(first, blk, 0.0), axis=1, keepdims=True)
                sb_ = jnp.sum(jnp.where(first, 0.0, blk), axis=1, keepdims=True)
                acc = acc + jnp.where(lane == 2 * p, sa, 0.0) + jnp.where(lane == 2 * p + 1, sb_, 0.0)
            db_ref[...] = acc

    row = pl.BlockSpec((tm, SG_W), lambda i: (i, 0))
    return pl.pallas_call(
        body, name=name, grid=(nsteps,),
        in_specs=[row, row, row, pl.BlockSpec(w.shape, lambda i: (0, 0, 0)), pl.BlockSpec(bt.shape, lambda i: (0, 0))],
        out_specs=[row, row, pl.BlockSpec(w.shape, lambda i: (0, 0, 0)), pl.BlockSpec((SG_CHUNK, LANE), lambda i: (0, 0))],
        out_shape=[jax.ShapeDtypeStruct((S, SG_W), F32), jax.ShapeDtypeStruct((S, SG_W), F32),
                   jax.ShapeDtypeStruct(w.shape, F32), jax.ShapeDtypeStruct((SG_CHUNK, LANE), F32)],
        scratch_shapes=[pltpu.VMEM((SG_CHUNK, SG_W), F32)], compiler_params=_params(1))(d_o, gl, u, w, bt)


ROPE_HALF = MLA_ROPE // 2
KR_COL = (MLA_QL + MLA_KVL) // LANE
MLA_IN_PAD = MLA_QL + MLA_KVL + LANE


def _rope_tables(positions):
    inv_freq = ROPE_THETA ** (-jnp.arange(ROPE_HALF, dtype=F32) / ROPE_HALF)
    ang = positions.astype(F32)[:, None] * inv_freq
    cos, sin = jnp.cos(ang), jnp.sin(ang)
    S = positions.shape[0]
    z16, tail = jnp.zeros((S, ROPE_HALF), F32), jnp.zeros((S, LANE - MLA_QK), F32)
    ones = jnp.ones((S, MLA_NOPE), F32)
    zeros = jnp.zeros((S, MLA_NOPE), F32)
    return (jnp.concatenate([ones, cos, cos, tail], axis=1), jnp.concatenate([zeros, z16, sin, tail], axis=1),
            jnp.concatenate([zeros, -sin, z16, tail], axis=1))


def _rope(x, cos, sa, sb):
    return x * cos + pltpu.roll(x, ROPE_HALF, 1) * sa + pltpu.roll(x, LANE - ROPE_HALF, 1) * sb


def _rope_t(dy, cos, sa, sb):
    return dy * cos + pltpu.roll(dy * sa, LANE - ROPE_HALF, 1) + pltpu.roll(dy * sb, ROPE_HALF, 1)


def _mla_lora(P, qlg, kvlg, *, name):
    def fn(pv, a, b):
        return _rmsnorm_fwd(pv[:, :MLA_QL], a), _rmsnorm_fwd(pv[:, MLA_QL:MLA_QL + MLA_KVL], b)

    return _rowwise(fn, [P], [qlg.reshape(1, MLA_QL), kvlg.reshape(1, MLA_KVL)], [(MLA_QL, BF16), (MLA_KVL, BF16)], name=name)


def _mla_lora_bwd(dcq, dckv, dkr, P, qlg, kvlg, *, name):
    def fn(d1, d2, d3, pv, a, b):
        x1, g1 = _rmsnorm_bwd(d1, pv[:, :MLA_QL], a)
        x2, g2 = _rmsnorm_bwd(d2, pv[:, MLA_QL:MLA_QL + MLA_KVL], b)
        return jnp.concatenate([x1, x2, d3], axis=1), g1, g2

    return _rowwise(fn, [dcq, dckv, dkr, P], [qlg.reshape(1, MLA_QL), kvlg.reshape(1, MLA_KVL)], [(MLA_IN_PAD, BF16)],
                    [(1, MLA_QL), (1, MLA_KVL)], name=name)


def _mla_qk(q_pre, k_pre, P, tabs, qg, kg, *, name):
    def fn(qp, kp, kr, c, a, b, qgv, kgv):
        return (_rope(_rmsnorm_fwd(qp, qgv, MLA_QK), c, a, b) * MLA_SCALE,
                _rope(_rmsnorm_fwd(kp + kr, kgv, MLA_QK), c, a, b))

    hcol = lambda h: h
    rows = [(q_pre, LANE, hcol), (k_pre, LANE, hcol), (P, LANE, lambda h: KR_COL), *tabs]
    w = MLA_HEADS * LANE
    return _rowwise(fn, rows, [qg, kg], [(w, BF16), (w, BF16)], heads=MLA_HEADS, tm=HEAD_ROWS, name=name)


def _mla_qk_bwd(dq, dk, q_pre, k_pre, P, tabs, qg, kg, *, name):
    def fn(dqv, dkv, qp, kp, kr, c, a, b, qgv, kgv):
        dqp, dqg = _rmsnorm_bwd(_rope_t(dqv * MLA_SCALE, c, a, b), qp, qgv, MLA_QK)
        dkp, dkg = _rmsnorm_bwd(_rope_t(dkv, c, a, b), kp + kr, kgv, MLA_QK)
        lane = lax.broadcasted_iota(jnp.int32, (1, LANE), 1)
        return dqp, dkp, jnp.where((lane >= MLA_NOPE) & (lane < MLA_QK), dkp, 0.0), dqg, dkg

    hcol = lambda h: h
    rows = [(dq, LANE, hcol), (dk, LANE, hcol), (q_pre, LANE, hcol), (k_pre, LANE, hcol), (P, LANE, lambda h: KR_COL), *tabs]
    w = MLA_HEADS * LANE
    return _rowwise(fn, rows, [qg, kg], [(w, BF16), (w, BF16)], [(1, LANE), (1, LANE)], [LANE], heads=MLA_HEADS,
                    tm=HEAD_ROWS, name=name)


def _head_norm(x, g, *, heads, width, colfn=None, scale=1.0, name):
    return _rowwise(lambda xv, gv: _rmsnorm_fwd(xv, gv) * scale, [(x, width, colfn or (lambda h: h))],
                    [g.reshape(1, width)], [(heads * width, BF16)], heads=heads, tm=HEAD_ROWS, name=name)[0]


def _head_norm_bwd(dy, x, g, *, heads, width, colfn=None, scale=1.0, out_dtype, name):
    return _rowwise(lambda dv_, xv, gv: _rmsnorm_bwd(dv_ * scale, xv, gv),
                    [(dy, width, lambda h: h), (x, width, colfn or (lambda h: h))],
                    [g.reshape(1, width)], [(heads * width, out_dtype)], [(1, width)], heads=heads, tm=HEAD_ROWS,
                    name=name)


def _loss_grad(y, tgt, *, name):
    D = y.shape[1]

    def fn(yv, tv):
        d = yv - tv
        return d * (1.0 / D), jnp.sum(d * d, axis=0, keepdims=True) * (0.5 / D)

    dy, part = _rowwise(fn, [y, tgt], [], [(D, F32)], [(1, D)], name=name)
    return jnp.sum(part), dy


def _adamw(w, g, m, v, *, name):
    shape = w.shape
    two_d = (-1, shape[-1])

    def fn(wv, gv, mv, vv):
        m2 = ADAM_B1 * mv + (1.0 - ADAM_B1) * gv
        v2 = ADAM_B2 * vv + (1.0 - ADAM_B2) * (gv * gv)
        m_hat = m2 / (1.0 - ADAM_B1 ** ADAM_STEP)
        v_hat = v2 / (1.0 - ADAM_B2 ** ADAM_STEP)
        return -ADAM_LR * (m_hat / (jnp.sqrt(v_hat) + ADAM_EPS) + ADAM_WD * wv), m2, v2

    outs = _rowwise(fn, [t.reshape(two_d) for t in (w, g, m, v)], [], [(shape[-1], F32)] * 3, name=name)
    return [o.reshape(shape) for o in outs]


def _pad_cols(w, heads, hd):
    k = w.shape[0]
    return jnp.pad(w.reshape(k, heads, hd), ((0, 0), (0, 0), (0, LANE - hd))).reshape(k, heads * LANE)


def _unpad_cols(w, heads, hd):
    k = w.shape[0]
    return w.reshape(k, heads, LANE)[:, :, :hd].reshape(k, heads * hd)


def _pad_rows(w, heads, hd):
    n = w.shape[1]
    return jnp.pad(w.reshape(heads, hd, n), ((0, 0), (0, LANE - hd), (0, 0))).reshape(heads * LANE, n)


def _unpad_rows(w, heads, hd):
    n = w.shape[1]
    return w.reshape(heads, LANE, n)[:, :hd, :].reshape(heads * hd, n)


def _ffn_fwd(x, g, wgu, wd, tag):
    h = _norm_rows(x, g, name=tag + "_norm")
    gate, up, act = _mm_swiglu(h, wgu, name=tag + "_gu")
    y = _mm(act, wd, scale=0.5, residual=x, name=tag + "_down")
    return y, (x, h, gate, up, act)


def _ffn_bwd(dy, saved, g, wgu, wd, tag):
    x, h, gate, up, act = saved
    F = wd.shape[0]
    dwd = _mm(act, dy, ta=True, scale=0.5, name=tag + "_dwd")
    dgate, dup = _mm_dswiglu(dy, wd, gate, up, scale=0.5, name=tag + "_dact")
    dh = _mm(dgate, wgu, tb=True, name=tag + "_dh_g")
    dh = _mm(dup, wgu, tb=True, b_off=(0, F), residual=dh, name=tag + "_dh_u")
    dwgu = jnp.concatenate([_mm(h, dgate, ta=True, name=tag + "_dwg"), _mm(h, dup, ta=True, name=tag + "_dwu")], axis=1)
    dx, dg = _norm_rows_bwd(dh, x, g, dy, name=tag + "_dnorm")
    return dx, dg, dwgu, dwd


def _even_weights(w_in, w_out):
    parts = [w_in[:, :SB_W] * SB_SCALE, w_in[:, SB_W:2 * SB_W], w_in[:, 2 * SB_W:3 * SB_W]]
    wqkv = jnp.concatenate([_pad_cols(p, SB_HEADS, SB_HD) for p in parts], axis=1)
    return wqkv, w_in[:, 3 * SB_W:], _pad_rows(w_out[:SB_W], SB_HEADS, SB_HD), w_out[SB_W:]


def _even_fwd(x, g, wts, ln_g, ln_b, sgu_w, bt, tag):
    wqkv, wz, wo_sb, wo_sg = wts
    h = _norm_rows(x, g, name=tag + "_norm")
    qkv = _mm(h, wqkv, out_dtype=BF16, name=tag + "_qkv")
    z = _mm(h, wz, name=tag + "_z")
    o_sb, tot = _attn_fwd(qkv, qkv, qkv, sb=True, causal=True, heads=SB_HEADS, dq=LANE, dv=LANE,
                          kcol=lambda hh: SB_HEADS + hh, vcol=lambda hh: 2 * SB_HEADS + hh, name=tag + "_sb")
    u, gl = _gelu_ln(z, ln_g, ln_b, name=tag + "_geluln")
    o_sg = _spatial(gl, u, sgu_w, bt, name=tag + "_sgu")
    y = _mm(o_sb, wo_sb, residual=x, name=tag + "_out_sb")
    y = _mm(o_sg, wo_sg, residual=y, name=tag + "_out_sg")
    return y, (x, h, qkv, z, o_sb, tot, u, gl, o_sg)


def _even_bwd(dy, saved, g, wts, ln_g, sgu_w, bt, tag):
    wqkv, wz, wo_sb, wo_sg = wts
    x, h, qkv, z, o_sb, tot, u, gl, o_sg = saved
    do_sb = _mm(dy, wo_sb, tb=True, name=tag + "_do_sb")
    do_sg = _mm(dy, wo_sg, tb=True, name=tag + "_do_sg")
    dwo = jnp.concatenate([_unpad_rows(_mm(o_sb, dy, ta=True, name=tag + "_dwo_sb"), SB_HEADS, SB_HD),
                           _mm(o_sg, dy, ta=True, name=tag + "_dwo_sg")], axis=0)
    dq, dk, dv = _attn_bwd(qkv, qkv, qkv, o_sb, do_sb, tot, sb=True, causal=True, heads=SB_HEADS, dq=LANE, dv=LANE,
                           kcol=lambda hh: SB_HEADS + hh, vcol=lambda hh: 2 * SB_HEADS + hh, name=tag + "_sb_bwd")
    du, dgl, dsgu_w, db_t = _spatial_bwd(do_sg, gl, u, sgu_w, bt, name=tag + "_sgu_bwd")
    dz, dln_g, dln_b = _gelu_ln_bwd(z, du, dgl, ln_g, name=tag + "_geluln_bwd")
    dh = _mm(dz, wz, tb=True, name=tag + "_dh_z")
    dws = []
    for i, (d, nm) in enumerate(((dq, "q"), (dk, "k"), (dv, "v"))):
        dh = _mm(d, wqkv, tb=True, b_off=(0, i * SB_HEADS * LANE), residual=dh, name=tag + "_dh_" + nm)
        dws.append(_unpad_cols(_mm(h, d, ta=True, scale=SB_SCALE if nm == "q" else 1.0, name=tag + "_dw_" + nm),
                               SB_HEADS, SB_HD))
    dws.append(_mm(h, dz, ta=True, name=tag + "_dw_z"))
    dx, dg = _norm_rows_bwd(dh, x, g, dy, name=tag + "_dnorm")
    return dx, dict(mix_norm=dg, sbg_w_in=jnp.concatenate(dws, axis=1), sgu_ln_gain=dln_g, sgu_ln_bias=dln_b,
                    sgu_w=dsgu_w, sgu_b=db_t[:, :SG_GROUPS].T, sbg_w_out=dwo)


def _mla_weights(w_in, w_uq, w_ukv, w_out, q_gain, k_gain):
    d = w_in.shape[0]
    lat = MLA_QL + MLA_KVL
    w_in_ext = jnp.concatenate([w_in[:, :lat], jnp.zeros((d, MLA_NOPE), w_in.dtype), w_in[:, lat:],
                                jnp.zeros((d, LANE - MLA_QK), w_in.dtype)], axis=1)
    kv = w_ukv.reshape(MLA_KVL, MLA_HEADS, MLA_NOPE + MLA_V)
    wk = _pad_cols(kv[:, :, :MLA_NOPE].reshape(MLA_KVL, -1), MLA_HEADS, MLA_NOPE)
    wv = _pad_cols(kv[:, :, MLA_NOPE:].reshape(MLA_KVL, -1), MLA_HEADS, MLA_V)
    pad_gain = lambda gn: jnp.pad(gn.reshape(1, MLA_QK), ((0, 0), (0, LANE - MLA_QK)))
    return (w_in_ext, _pad_cols(w_uq, MLA_HEADS, MLA_QK), wk, wv, _pad_rows(w_out, MLA_HEADS, MLA_V),
            pad_gain(q_gain), pad_gain(k_gain))


def _mla_fwd(x, g, wts, qlg, kvlg, tabs, tag):
    w_in, w_uq, wk, wv, w_out, qg, kg = wts
    h = _norm_rows(x, g, name=tag + "_norm")
    P = _mm(h, w_in, name=tag + "_in")
    cqn, ckvn = _mla_lora(P, qlg, kvlg, name=tag + "_lora")
    q_pre = _mm(cqn, w_uq, name=tag + "_uq")
    k_pre = _mm(ckvn, wk, name=tag + "_uk")
    ones_lane = jnp.tile((jnp.arange(LANE) == MLA_V).astype(F32), MLA_HEADS)[None, :]
    v = _mm(ckvn, wv, out_dtype=BF16, bias=ones_lane, name=tag + "_uv")
    q, k = _mla_qk(q_pre, k_pre, P, tabs, qg, kg, name=tag + "_qk")
    o, lse = _attn_fwd(q, k, v, sb=False, causal=True, heads=MLA_HEADS, dq=LANE, dv=LANE, sum_lane=MLA_V,
                       name=tag + "_attn")
    y = _mm(o, w_out, residual=x, name=tag + "_out")
    return y, (x, h, P, cqn, ckvn, q_pre, k_pre, q, k, v, o, lse)


def _mla_bwd(dy, saved, g, wts, qlg, kvlg, tabs, tag):
    w_in, w_uq, wk, wv, w_out, qg, kg = wts
    x, h, P, cqn, ckvn, q_pre, k_pre, q, k, v, o, lse = saved
    do = _mm(dy, w_out, tb=True, name=tag + "_do")
    dw_out = _unpad_rows(_mm(o, dy, ta=True, name=tag + "_dwo"), MLA_HEADS, MLA_V)
    dq, dk, dv = _attn_bwd(q, k, v, o, do, lse, sb=False, causal=True, heads=MLA_HEADS, dq=LANE, dv=LANE,
                           name=tag + "_attn_bwd")
    dq_pre, dk_pre, dkr, dqg, dkg = _mla_qk_bwd(dq, dk, q_pre, k_pre, P, tabs, qg, kg, name=tag + "_qk_bwd")
    dcqn = _mm(dq_pre, w_uq, tb=True, name=tag + "_dcq")
    dckvn = _mm(dk_pre, wk, tb=True, name=tag + "_dckv_k")
    dckvn = _mm(dv, wv, tb=True, residual=dckvn, name=tag + "_dckv_v")
    dw_uq = _unpad_cols(_mm(cqn, dq_pre, ta=True, name=tag + "_dwuq"), MLA_HEADS, MLA_QK)
    dwk = _unpad_cols(_mm(ckvn, dk_pre, ta=True, name=tag + "_dwk"), MLA_HEADS, MLA_NOPE)
    dwv = _unpad_cols(_mm(ckvn, dv, ta=True, name=tag + "_dwv"), MLA_HEADS, MLA_V)
    dw_ukv = jnp.concatenate([dwk.reshape(MLA_KVL, MLA_HEADS, MLA_NOPE), dwv.reshape(MLA_KVL, MLA_HEADS, MLA_V)],
                             axis=2).reshape(MLA_KVL, -1)
    dP, dqlg, dkvlg = _mla_lora_bwd(dcqn, dckvn, dkr, P, qlg, kvlg, name=tag + "_lora_bwd")
    dh = _mm(dP, w_in, tb=True, name=tag + "_dh")
    dw_in_ext = _mm(h, dP, ta=True, name=tag + "_dwin")
    lat = MLA_QL + MLA_KVL
    dw_in = jnp.concatenate([dw_in_ext[:, :lat], dw_in_ext[:, lat + MLA_NOPE:lat + MLA_QK]], axis=1)
    dx, dg = _norm_rows_bwd(dh, x, g, dy, name=tag + "_dnorm")
    return dx, dict(mix_norm=dg, mla_w_in=dw_in, mla_q_lora_gain=dqlg, mla_kv_lora_gain=dkvlg, mla_w_uq=dw_uq,
                    mla_w_ukv=dw_ukv, mla_q_gain=dqg[:, :MLA_QK], mla_k_gain=dkg[:, :MLA_QK], mla_w_out=dw_out)


def _xmem_fwd(x, mem, g, gm, wq, wkv, qg, kg, wo, tag):
    hq = _norm_rows(x, g, name=tag + "_norm")
    hm = _norm_rows(mem, gm, name=tag + "_mnorm")
    qp = _mm(hq, wq, name=tag + "_q")
    kv = _mm(hm, wkv, name=tag + "_kv")
    q = _head_norm(qp, qg, heads=MEM_HEADS, width=MEM_HD, scale=MEM_SCALE, name=tag + "_qn")
    kn = _head_norm(kv, kg, heads=MEM_HEADS, width=MEM_HD, colfn=lambda hh: 2 * hh, name=tag + "_kn")
    kvb = kv.astype(BF16)
    o, lse = _attn_fwd(q, kn, kvb, sb=False, causal=False, heads=MEM_HEADS, dq=MEM_HD, dv=MEM_HD,
                       vcol=lambda hh: 2 * hh + 1, name=tag + "_attn")
    y = _mm(o, wo, residual=x, name=tag + "_out")
    return y, (x, hq, hm, qp, kv, q, kn, kvb, o, lse)


def _xmem_bwd(dy, saved, mem, g, gm, wq, wkv, qg, kg, wo, tag):
    x, hq, hm, qp, kv, q, kn, kvb, o, lse = saved
    m = mem.shape[0]
    do = _mm(dy, wo, tb=True, name=tag + "_do")
    dwo = _mm(o, dy, ta=True, name=tag + "_dwo")
    dq, dk, dv = _attn_bwd(q, kn, kvb, o, do, lse, sb=False, causal=False, heads=MEM_HEADS, dq=MEM_HD, dv=MEM_HD,
                           vcol=lambda hh: 2 * hh + 1, name=tag + "_attn_bwd")
    dqp, dqg = _head_norm_bwd(dq, qp, qg, heads=MEM_HEADS, width=MEM_HD, scale=MEM_SCALE, out_dtype=BF16,
                              name=tag + "_qn_bwd")
    dkp, dkg = _head_norm_bwd(dk, kv, kg, heads=MEM_HEADS, width=MEM_HD, colfn=lambda hh: 2 * hh, out_dtype=F32,
                              name=tag + "_kn_bwd")
    dkv = jnp.concatenate([dkp.reshape(m, MEM_HEADS, MEM_HD), dv.reshape(m, MEM_HEADS, MEM_HD)], axis=2).reshape(m, -1)
    dwkv = _mm(hm, dkv, ta=True, name=tag + "_dwkv")
    dhm = _mm(dkv, wkv, tb=True, name=tag + "_dhm")
    _, dgm = _norm_rows_bwd(dhm, mem, gm, None, name=tag + "_dmnorm")
    dwq = _mm(hq, dqp, ta=True, name=tag + "_dwq")
    dhq = _mm(dqp, wq, tb=True, name=tag + "_dhq")
    dx, dg = _norm_rows_bwd(dhq, x, g, dy, name=tag + "_dnorm")
    return dx, dict(xmem_norm=dg, xmem_mem_norm=dgm, xmem_wq=dwq, xmem_wkv=dwkv, xmem_q_gain=dqg, xmem_k_gain=dkg,
                    xmem_wo=dwo)


def _local_step(x, mem, positions, tgt, w):
    tabs = _rope_tables(positions)
    even = _even_weights(w["sbg_w_in"][0], w["sbg_w_out"][0])
    mla = _mla_weights(w["mla_w_in"][0], w["mla_w_uq"][0], w["mla_w_ukv"][0], w["mla_w_out"][0], w["mla_q_gain"][0],
                       w["mla_k_gain"][0])
    bt = jnp.repeat(w["sgu_b"][0].T, SG_GD, axis=1)
    saved = []
    for l in range(2):
        t = f"l{l}"
        x, s_pre = _ffn_fwd(x, w["ffn_pre_norm"][l], w["ffn_pre_w_gu"][l], w["ffn_pre_w_down"][l], t + "_pre")
        if l == 0:
            x, s_mix = _even_fwd(x, w["mix_norm"][0], even, w["sgu_ln_gain"][0], w["sgu_ln_bias"][0], w["sgu_w"][0], bt,
                                 t + "_even")
        else:
            x, s_mix = _mla_fwd(x, w["mix_norm"][1], mla, w["mla_q_lora_gain"][0], w["mla_kv_lora_gain"][0], tabs,
                                t + "_mla")
        x, s_xm = _xmem_fwd(x, mem, w["xmem_norm"][l], w["xmem_mem_norm"][l], w["xmem_wq"][l], w["xmem_wkv"][l],
                            w["xmem_q_gain"][l], w["xmem_k_gain"][l], w["xmem_wo"][l], t + "_xm")
        x, s_post = _ffn_fwd(x, w["ffn_post_norm"][l], w["ffn_post_w_gu"][l], w["ffn_post_w_down"][l], t + "_post")
        saved.append((s_pre, s_mix, s_xm, s_post))
    loss, dx = _loss_grad(x, tgt, name="loss")
    grads = {}

    def put(name, l, val):
        grads.setdefault(name, {})[l] = val

    for l in (1, 0):
        t = f"l{l}"
        s_pre, s_mix, s_xm, s_post = saved[l]
        dx, dg, dwgu, dwd = _ffn_bwd(dx, s_post, w["ffn_post_norm"][l], w["ffn_post_w_gu"][l], w["ffn_post_w_down"][l],
                                     t + "_post")
        put("ffn_post_norm", l, dg), put("ffn_post_w_gu", l, dwgu), put("ffn_post_w_down", l, dwd)
        dx, gx = _xmem_bwd(dx, s_xm, mem, w["xmem_norm"][l], w["xmem_mem_norm"][l], w["xmem_wq"][l], w["xmem_wkv"][l],
                           w["xmem_q_gain"][l], w["xmem_k_gain"][l], w["xmem_wo"][l], t + "_xm")
        for k_, v_ in gx.items():
            put(k_, l, v_)
        if l == 0:
            dx, gm = _even_bwd(dx, s_mix, w["mix_norm"][0], even, w["sgu_ln_gain"][0], w["sgu_w"][0], bt, t + "_even")
        else:
            dx, gm = _mla_bwd(dx, s_mix, w["mix_norm"][1], mla, w["mla_q_lora_gain"][0], w["mla_kv_lora_gain"][0], tabs,
                              t + "_mla")
        for k_, v_ in gm.items():
            put(k_, l if k_ == "mix_norm" else 0, v_)
        dx, dg, dwgu, dwd = _ffn_bwd(dx, s_pre, w["ffn_pre_norm"][l], w["ffn_pre_w_gu"][l], w["ffn_pre_w_down"][l],
                                     t + "_pre")
        put("ffn_pre_norm", l, dg), put("ffn_pre_w_gu", l, dwgu), put("ffn_pre_w_down", l, dwd)
    return loss, dx, {k_: [v_[l] for l in sorted(v_)] for k_, v_ in grads.items()}


N_CHIPS = 4
PACK_COLS = 1024
PACK_ROW_MULTIPLE = 512


def _place():
    x, y, c = lax.axis_index("x"), lax.axis_index("y"), lax.axis_index("c")
    return x, y, c, [(1 - x, y), (x, 1 - y), (1 - x, 1 - y)]


def _hops(x, y, c):
    return ((x + 1 - c) % 2, (y + c) % 2), ((x + c) % 2, (y + 1 - c) % 2), (1 - x, 1 - y)


def _gather_chips(shard):
    R, C = shard.shape
    Rh = R // 2

    def body(x_ref, out_ref, send_sems, recv_sems):
        x, y, c = lax.axis_index("x"), lax.axis_index("y"), lax.axis_index("c")
        n1, n2, nd = _hops(x, y, c)
        me, q1, q2, qd = 2 * x + y, 2 * n1[0] + n1[1], 2 * n2[0] + n2[1], 2 * nd[0] + nd[1]

        def half(chip, core):
            return out_ref.at[chip, pl.ds(core * Rh, Rh), :]

        def copy(k, chip, core, to, src=None):
            return pltpu.make_async_remote_copy(src_ref=half(chip, core) if src is None else src, dst_ref=half(chip, core),
                                                send_sem=send_sems.at[k], recv_sem=recv_sems.at[k], device_id=to,
                                                device_id_type=MESH)

        own = x_ref.at[pl.ds(c * Rh, Rh), :]
        sibling = (x, y, 1 - c)
        sends = [copy(0, me, c, (*n1, c), src=own), copy(1, me, c, (*n2, c), src=own)]
        sends[0].start()
        sends[1].start()
        copy(0, q1, c, sibling).wait_recv()
        sends += [copy(2, q1, c, (*n2, c)), copy(3, q1, c, sibling)]
        sends[2].start()
        sends[3].start()
        copy(1, q2, c, sibling).wait_recv()
        sends.append(copy(4, q2, c, sibling))
        sends[4].start()
        copy(2, qd, c, sibling).wait_recv()
        sends.append(copy(5, qd, c, sibling))
        sends[5].start()
        copy(3, q2, 1 - c, sibling).wait_recv()
        copy(4, q1, 1 - c, sibling).wait_recv()
        copy(5, qd, 1 - c, sibling).wait_recv()
        for cp in sends:
            cp.wait_send()

    others = pl.pallas_call(
        body, name="gather_weights", out_shape=jax.ShapeDtypeStruct((N_CHIPS, R, C), shard.dtype),
        in_specs=[ANY], out_specs=ANY,
        scratch_shapes=[pltpu.SemaphoreType.DMA((6,)), pltpu.SemaphoreType.DMA((6,))])(shard)
    me = 2 * lax.axis_index("x") + lax.axis_index("y")
    return lax.dynamic_update_slice(others, shard[None], (me, 0, 0))


def _gather_devices(block):
    M, N = block.shape

    def body(x_ref, out_ref, send_sems, recv_sems, local_sem):
        x, y, c, chips = _place()
        me, sibling = (x, y, c), (x, y, 1 - c)

        def rows(px, py, pc):
            return out_ref.at[pl.ds((4 * px + 2 * py + pc) * M, M), :]

        def copy(k, blk, to, src=None):
            return pltpu.make_async_remote_copy(src_ref=rows(*blk) if src is None else src, dst_ref=rows(*blk),
                                                send_sem=send_sems.at[k], recv_sem=recv_sems.at[k], device_id=to,
                                                device_id_type=MESH)

        mine = pltpu.make_async_copy(x_ref, rows(*me), local_sem)
        mine.start()
        first = [copy(0, me, sibling, src=x_ref)]
        first += [copy(1 + j, me, (*chip, c), src=x_ref) for j, chip in enumerate(chips)]
        for cp in first:
            cp.start()
        passed = [copy(4 + j, (*chip, c), sibling) for j, chip in enumerate(chips)]
        for j, chip in enumerate(chips):
            copy(1 + j, (*chip, c), me).wait_recv()
            passed[j].start()
        copy(0, sibling, me).wait_recv()
        for j, chip in enumerate(chips):
            copy(4 + j, (*chip, 1 - c), me).wait_recv()
        for cp in first + passed:
            cp.wait_send()
        mine.wait()

    vmem = pl.BlockSpec(memory_space=pltpu.VMEM)
    return pl.pallas_call(
        body, name=f"gather_devices_{M}", out_shape=jax.ShapeDtypeStruct((8 * M, N), block.dtype),
        in_specs=[vmem], out_specs=vmem,
        scratch_shapes=[pltpu.SemaphoreType.DMA((7,)), pltpu.SemaphoreType.DMA((7,)), pltpu.SemaphoreType.DMA],
        compiler_params=pltpu.CompilerParams(vmem_limit_bytes=VMEM_LIMIT))(block)


def _swap_halves(g):
    n, R, C = g.shape
    Rh = R // 2

    def body(g_ref, a_ref, send_sem, recv_sem):
        x, y, c, _ = _place()
        cp = pltpu.make_async_remote_copy(src_ref=g_ref.at[:, pl.ds((1 - c) * Rh, Rh), :], dst_ref=a_ref,
                                          send_sem=send_sem, recv_sem=recv_sem, device_id=(x, y, 1 - c),
                                          device_id_type=MESH)
        cp.start()
        cp.wait()

    return pl.pallas_call(body, name="grad_swap_halves", out_shape=jax.ShapeDtypeStruct((n, Rh, C), g.dtype),
                          in_specs=[ANY], out_specs=ANY,
                          scratch_shapes=[pltpu.SemaphoreType.DMA, pltpu.SemaphoreType.DMA])(g)


def _add_picked(a, b, picks, *, a_row_half=None, out_dtype, name):
    n_out = picks.shape[0]
    _, rows, C = b.shape
    tr = _row_tile(rows, 512)
    nt = rows // tr
    half = jnp.zeros((1,), jnp.int32) if a_row_half is None else a_row_half

    def body(pick_ref, half_ref, a_ref, b_ref, o_ref):
        o_ref[...] = (a_ref[...].astype(F32) + b_ref[...].astype(F32)).astype(o_ref.dtype)

    spec = pltpu.PrefetchScalarGridSpec(
        num_scalar_prefetch=2, grid=(n_out, nt),
        in_specs=[pl.BlockSpec((1, tr, C), lambda j, i, pick, hf: (pick[j], hf[0] * nt + i, 0)),
                  pl.BlockSpec((1, tr, C), lambda j, i, pick, hf: (pick[j], i, 0))],
        out_specs=pl.BlockSpec((1, tr, C), lambda j, i, pick, hf: (j, i, 0)))
    return pl.pallas_call(body, name=name, grid_spec=spec, out_shape=jax.ShapeDtypeStruct((n_out, rows, C), out_dtype),
                          compiler_params=_params(2))(picks.astype(jnp.int32), half.astype(jnp.int32), a, b)


def _hop_exchange(src, hop, *, name):
    def body(s_ref, d_ref, send_sem, recv_sem):
        x, y, c = lax.axis_index("x"), lax.axis_index("y"), lax.axis_index("c")
        cp = pltpu.make_async_remote_copy(src_ref=s_ref, dst_ref=d_ref, send_sem=send_sem, recv_sem=recv_sem,
                                          device_id=(*_hops(x, y, c)[hop], c), device_id_type=MESH)
        cp.start()
        cp.wait()

    return pl.pallas_call(body, name=name, out_shape=jax.ShapeDtypeStruct(src.shape, src.dtype), in_specs=[ANY],
                          out_specs=ANY, scratch_shapes=[pltpu.SemaphoreType.DMA, pltpu.SemaphoreType.DMA])(src)


def _reduce_over_chips(g):
    x, y, c = lax.axis_index("x"), lax.axis_index("y"), lax.axis_index("c")
    n1, n2, _ = _hops(x, y, c)
    chip = lambda p: 2 * p[0] + p[1]
    near = jnp.stack([chip((x, y)), chip(n2)])
    far = jnp.stack([chip(n1), chip((1 - x, 1 - y))])
    half = c.reshape(1)
    sib = _swap_halves(g)
    kept = _add_picked(g, sib, near, a_row_half=half, out_dtype=F32, name="grad_add_near")
    sent = _add_picked(g, sib, far, a_row_half=half, out_dtype=BF16, name="grad_add_far")
    got = _hop_exchange(sent, 0, name="grad_hop_first")
    mine = _add_picked(kept, got, jnp.zeros((1,), jnp.int32), out_dtype=F32, name="grad_add_mine")
    theirs = _add_picked(kept, got, jnp.ones((1,), jnp.int32), out_dtype=BF16, name="grad_add_theirs")
    got = _hop_exchange(theirs, 1, name="grad_hop_second")
    total = _add_picked(mine, got, jnp.zeros((1,), jnp.int32), out_dtype=F32, name="grad_add_total")
    return _join_halves(total[0])


def _sum_slots(b, *, name):
    n, R, C = b.shape
    tr = _row_tile(R, 512)

    def body(b_ref, o_ref):
        acc = b_ref[0]
        for q in range(1, n):
            acc = acc + b_ref[q]
        o_ref[...] = acc

    return pl.pallas_call(body, name=name, grid=(R // tr,), in_specs=[pl.BlockSpec((n, tr, C), lambda i: (0, i, 0))],
                          out_specs=pl.BlockSpec((tr, C), lambda i: (i, 0)), out_shape=jax.ShapeDtypeStruct((R, C), F32),
                          compiler_params=_params(1))(b)


def _join_halves(r):
    Rh, C = r.shape

    def body(r_ref, o_ref, send_sem, recv_sem):
        x, y, c, _ = _place()
        own, other = o_ref.at[pl.ds(c * Rh, Rh), :], o_ref.at[pl.ds((1 - c) * Rh, Rh), :]
        cp = pltpu.make_async_remote_copy(src_ref=r_ref, dst_ref=own, send_sem=send_sem, recv_sem=recv_sem,
                                          device_id=(x, y, 1 - c), device_id_type=MESH)
        cp.start()
        pltpu.make_async_remote_copy(src_ref=r_ref, dst_ref=other, send_sem=send_sem, recv_sem=recv_sem,
                                     device_id=(x, y, 1 - c), device_id_type=MESH).wait_recv()
        cp.wait_send()

    theirs = pl.pallas_call(
        body, name="grad_join_halves", out_shape=jax.ShapeDtypeStruct((2 * Rh, C), r.dtype), in_specs=[ANY], out_specs=ANY,
        scratch_shapes=[pltpu.SemaphoreType.DMA, pltpu.SemaphoreType.DMA])(r)
    return lax.dynamic_update_slice(theirs, r, (lax.axis_index("c") * Rh, 0))


def _size(shape):
    size = 1
    for d in shape:
        size *= d
    return size


def _pack(pieces, cols, row_multiple, dtype):
    if any(p.size % cols for p in pieces):
        flat = jnp.concatenate([p.reshape(-1).astype(dtype) for p in pieces])
        pieces = [jnp.pad(flat, (0, -flat.shape[0] % cols))]
    rows = jnp.concatenate([p.reshape(-1, cols).astype(dtype) for p in pieces], axis=0)
    return jnp.pad(rows, ((0, -rows.shape[0] % row_multiple), (0, 0)))


def _unpack(buf, shapes):
    cols = buf.shape[1]
    if any(_size(s) % cols for s in shapes):
        flat, out, at = buf.reshape(-1), [], 0
        for shp in shapes:
            out.append(flat[at:at + _size(shp)].reshape(shp))
            at += _size(shp)
        return out
    out, at = [], 0
    for shp in shapes:
        out.append(buf[at:at + _size(shp) // cols].reshape(shp))
        at += _size(shp) // cols
    return out


SHARDED = (("ffn_pre_w_gu", 2), ("ffn_pre_w_down", 1), ("sbg_w_in", 2), ("sbg_w_out", 1), ("mla_w_in", 1),
           ("mla_w_uq", 2), ("mla_w_ukv", 2), ("mla_w_out", 1), ("xmem_wq", 1), ("xmem_wkv", 2), ("xmem_wo", 1),
           ("ffn_post_w_gu", 2), ("ffn_post_w_down", 1))
LORA_GAINS = ("mla_q_lora_gain", "mla_kv_lora_gain")
REPLICATED = ("ffn_pre_norm", "mix_norm", "sgu_ln_gain", "sgu_ln_bias", "sgu_w", "sgu_b", "mla_q_gain", "mla_k_gain",
              "xmem_norm", "xmem_mem_norm", "xmem_q_gain", "xmem_k_gain", "ffn_post_norm")
WEIGHTS = ("ffn_pre_norm", "ffn_pre_w_gu", "ffn_pre_w_down", "mix_norm", "sbg_w_in", "sgu_ln_gain", "sgu_ln_bias", "sgu_w",
           "sgu_b", "sbg_w_out", "mla_w_in", "mla_q_lora_gain", "mla_kv_lora_gain", "mla_w_uq", "mla_w_ukv", "mla_q_gain",
           "mla_k_gain", "mla_w_out", "xmem_norm", "xmem_mem_norm", "xmem_wq", "xmem_wkv", "xmem_q_gain", "xmem_k_gain",
           "xmem_wo", "ffn_post_norm", "ffn_post_w_gu", "ffn_post_w_down")
INPUTS = ("x", "mem", "positions") + WEIGHTS + ("loss_target",) + tuple("m_" + n for n in WEIGHTS) + tuple(
    "v_" + n for n in WEIGHTS)


def _step(a):
    x, y, c, _ = _place()
    chip = 2 * x + y
    shard_shapes = [a[n].shape for n, _ in SHARDED]

    gathered = _gather_chips(_pack([a[n] for n, _ in SHARDED], PACK_COLS, PACK_ROW_MULTIPLE, BF16))
    parts = [_unpack(gathered[q], shard_shapes) for q in range(N_CHIPS)]
    w = {n: jnp.concatenate([parts[q][i] for q in range(N_CHIPS)], axis=ax) for i, (n, ax) in enumerate(SHARDED)}
    gains = jnp.zeros((8, LANE), F32)
    for r, n in enumerate(LORA_GAINS):
        gains = gains.at[r, :a[n].shape[1]].set(a[n][0])
    gains = _gather_devices(gains)
    for r, n in enumerate(LORA_GAINS):
        w[n] = jnp.concatenate([gains[16 * q + r, :a[n].shape[1]] for q in range(N_CHIPS)])[None, :]
    for n in REPLICATED:
        w[n] = a[n]

    loss, dx, grads = _local_step(a["x"][0], a["mem"][0], a["positions"][0], a["loss_target"][0], w)
    loss = lax.psum(loss, ("x", "y", "c"))
    small_names = REPLICATED + LORA_GAINS
    full = {n: jnp.stack(grads[n]).reshape(w[n].shape) for n in small_names}

    def cut(n, ax, q):
        size = w[n].shape[ax] // N_CHIPS
        return [lax.slice_in_dim(gl, q * size, (q + 1) * size, axis=ax - 1) for gl in grads[n]]

    g = jnp.stack([_pack([p for n, ax in SHARDED for p in cut(n, ax, q)], PACK_COLS, PACK_ROW_MULTIPLE, F32)
                   for q in range(N_CHIPS)])
    reduced = _reduce_over_chips(g)
    gw = dict(zip([n for n, _ in SHARDED], _unpack(reduced, shard_shapes)))

    small = _pack([full[n] for n in small_names], LANE, 256, F32)
    rows = small.shape[0]
    summed = _sum_slots(_gather_devices(small).reshape(8, rows, LANE), name="grad_sum_devices")
    for n, val in zip(small_names, _unpack(summed, [full[n].shape for n in small_names])):
        if n in LORA_GAINS:
            size = a[n].shape[1]
            val = lax.dynamic_slice_in_dim(val, chip * size, size, axis=1)
        gw[n] = val

    upd = {n: _adamw(a[n], gw[n], a["m_" + n], a["v_" + n], name="adamw_" + n) for n in WEIGHTS}
    return (loss, dx[None], *[gw[n] for n in WEIGHTS], *[upd[n][0] for n in WEIGHTS], *[upd[n][1] for n in WEIGHTS],
            *[upd[n][2] for n in WEIGHTS])


def kernel(x, mem, positions, ffn_pre_norm, ffn_pre_w_gu, ffn_pre_w_down, mix_norm, sbg_w_in, sgu_ln_gain,
           sgu_ln_bias, sgu_w, sgu_b, sbg_w_out, mla_w_in, mla_q_lora_gain, mla_kv_lora_gain, mla_w_uq, mla_w_ukv,
           mla_q_gain, mla_k_gain, mla_w_out, xmem_norm, xmem_mem_norm, xmem_wq, xmem_wkv, xmem_q_gain, xmem_k_gain,
           xmem_wo, ffn_post_norm, ffn_post_w_gu, ffn_post_w_down, loss_target, m_ffn_pre_norm, m_ffn_pre_w_gu,
           m_ffn_pre_w_down, m_mix_norm, m_sbg_w_in, m_sgu_ln_gain, m_sgu_ln_bias, m_sgu_w, m_sgu_b, m_sbg_w_out,
           m_mla_w_in, m_mla_q_lora_gain, m_mla_kv_lora_gain, m_mla_w_uq, m_mla_w_ukv, m_mla_q_gain, m_mla_k_gain,
           m_mla_w_out, m_xmem_norm, m_xmem_mem_norm, m_xmem_wq, m_xmem_wkv, m_xmem_q_gain, m_xmem_k_gain,
           m_xmem_wo, m_ffn_post_norm, m_ffn_post_w_gu, m_ffn_post_w_down, v_ffn_pre_norm, v_ffn_pre_w_gu,
           v_ffn_pre_w_down, v_mix_norm, v_sbg_w_in, v_sgu_ln_gain, v_sgu_ln_bias, v_sgu_w, v_sgu_b, v_sbg_w_out,
           v_mla_w_in, v_mla_q_lora_gain, v_mla_kv_lora_gain, v_mla_w_uq, v_mla_w_ukv, v_mla_q_gain, v_mla_k_gain,
           v_mla_w_out, v_xmem_norm, v_xmem_mem_norm, v_xmem_wq, v_xmem_wkv, v_xmem_q_gain, v_xmem_k_gain,
           v_xmem_wo, v_ffn_post_norm, v_ffn_post_w_gu, v_ffn_post_w_down):
    given = locals()
    return _step({n: given[n] for n in INPUTS})
```

```python
import functools

import jax
import jax.numpy as jnp
from jax import lax
from jax.experimental import pallas as pl
from jax.experimental.pallas import tpu as pltpu

F32, BF16 = jnp.float32, jnp.bfloat16
LANE = 128
VMEM_LIMIT = 56 * 1024 * 1024
EPS = 1e-6
D_FF = 2816
SB_HEADS, SB_HD = 8, 64
SG_GROUPS, SG_GD, SG_CHUNK = 8, 64, 128
SB_W, SG_W = SB_HEADS * SB_HD, SG_GROUPS * SG_GD
MLA_HEADS, MLA_NOPE, MLA_ROPE, MLA_V = 16, 64, 32, 64
MLA_QK = MLA_NOPE + MLA_ROPE
MLA_QL, MLA_KVL = 512, 256
ROPE_THETA = 10000.0
MEM_HEADS, MEM_HD = 4, 256
SB_SCALE, MLA_SCALE, MEM_SCALE = SB_HD ** -0.5, MLA_QK ** -0.5, MEM_HD ** -0.5
ADAM_LR, ADAM_B1, ADAM_B2, ADAM_EPS, ADAM_WD, ADAM_STEP = 0.001, 0.9, 0.999, 1e-08, 0.01, 10
MESH = pl.DeviceIdType.MESH
ANY = pl.BlockSpec(memory_space=pl.ANY)


def _params(n_axes):
    return pltpu.CompilerParams(dimension_semantics=("arbitrary",) * n_axes, vmem_limit_bytes=VMEM_LIMIT)


MM_TILE_CAP = 1408
MM_VMEM_BUDGET = 40 * 1024 * 1024


def _tile(dim, cap):
    if dim <= cap:
        return dim
    best = max(t for t in range(LANE, cap + 1, LANE) if dim % t == 0)
    return best


def _mm(a, b, *, ta=False, tb=False, out_dtype=F32, scale=1.0, residual=None, bias=None, a_off=(0, 0), b_off=(0, 0),
        m=None, n=None, k=None, name):
    am, ak = (a.shape[1], a.shape[0]) if ta else a.shape
    bk, bn = (b.shape[1], b.shape[0]) if tb else b.shape
    M, N, K = m or am, n or bn, k or ak
    tm, tn = _tile(M, MM_TILE_CAP), _tile(N, MM_TILE_CAP)
    fixed = tm * tn * (4 + 2 * jnp.dtype(out_dtype).itemsize + (2 * residual.dtype.itemsize if residual is not None else 0))
    per_k = (tm * (2 * a.dtype.itemsize + 2) + tn * (2 * b.dtype.itemsize + 2))
    tk = _tile(K, max(LANE, (MM_VMEM_BUDGET - fixed) // per_k))
    nm, nn, nk = M // tm, N // tn, K // tk
    a_off = (a_off[0] // (tk if ta else tm), a_off[1] // (tm if ta else tk))
    b_off = (b_off[0] // (tn if tb else tk), b_off[1] // (tk if tb else tn))
    dims = (((0 if ta else 1,), (1 if tb else 0,)), ((), ()))

    def body(*refs):
        a_ref, b_ref = refs[0], refs[1]
        o_ref, acc_ref = refs[-2], refs[-1]
        kk = pl.program_id(2)

        @pl.when(kk == 0)
        def _():
            acc_ref[...] = jnp.zeros_like(acc_ref)

        acc_ref[...] += lax.dot_general(a_ref[...].astype(BF16), b_ref[...].astype(BF16), dims,
                                        preferred_element_type=F32)

        @pl.when(kk == nk - 1)
        def _():
            out = acc_ref[...] * scale
            for extra in refs[2:-2]:
                out = out + extra[...].astype(F32)
            o_ref[...] = out.astype(o_ref.dtype)

    (ao0, ao1), (bo0, bo1) = a_off, b_off
    a_spec = (pl.BlockSpec((tk, tm), lambda i, j, kk: (kk + ao0, i + ao1)) if ta
              else pl.BlockSpec((tm, tk), lambda i, j, kk: (i + ao0, kk + ao1)))
    b_spec = (pl.BlockSpec((tn, tk), lambda i, j, kk: (j + bo0, kk + bo1)) if tb
              else pl.BlockSpec((tk, tn), lambda i, j, kk: (kk + bo0, j + bo1)))
    o_spec = pl.BlockSpec((tm, tn), lambda i, j, kk: (i, j))
    ins, in_specs = [a, b], [a_spec, b_spec]
    if residual is not None:
        ins.append(residual)
        in_specs.append(o_spec)
    if bias is not None:
        ins.append(bias)
        in_specs.append(pl.BlockSpec((1, tn), lambda i, j, kk: (0, j)))
    return pl.pallas_call(
        body, name=name, grid=(nm, nn, nk), in_specs=in_specs, out_specs=o_spec,
        out_shape=jax.ShapeDtypeStruct((M, N), out_dtype),
        scratch_shapes=[pltpu.VMEM((tm, tn), F32)], compiler_params=_params(3))(*ins)


def _mm_swiglu(h, wgu, *, name):
    M, K = h.shape
    F = wgu.shape[1] // 2
    tm, tn, tk = _tile(M, 512), _tile(F, MM_TILE_CAP), _tile(K, 1024)
    nm, nf, nk = M // tm, F // tn, K // tk

    def body(h_ref, wg_ref, wu_ref, g_ref, u_ref, a_ref, accg, accu):
        kk = pl.program_id(2)

        @pl.when(kk == 0)
        def _():
            accg[...] = jnp.zeros_like(accg)
            accu[...] = jnp.zeros_like(accu)

        hb = h_ref[...]
        accg[...] += jnp.dot(hb, wg_ref[...], preferred_element_type=F32)
        accu[...] += jnp.dot(hb, wu_ref[...], preferred_element_type=F32)

        @pl.when(kk == nk - 1)
        def _():
            g, u = accg[...], accu[...]
            g_ref[...] = g.astype(BF16)
            u_ref[...] = u.astype(BF16)
            a_ref[...] = (g * jax.nn.sigmoid(g) * u).astype(BF16)

    o_spec = pl.BlockSpec((tm, tn), lambda i, j, kk: (i, j))
    shp = jax.ShapeDtypeStruct((M, F), BF16)
    return pl.pallas_call(
        body, name=name, grid=(nm, nf, nk),
        in_specs=[pl.BlockSpec((tm, tk), lambda i, j, kk: (i, kk)),
                  pl.BlockSpec((tk, tn), lambda i, j, kk: (kk, j)),
                  pl.BlockSpec((tk, tn), lambda i, j, kk: (kk, j + nf))],
        out_specs=[o_spec, o_spec, o_spec], out_shape=[shp, shp, shp],
        scratch_shapes=[pltpu.VMEM((tm, tn), F32), pltpu.VMEM((tm, tn), F32)],
        compiler_params=_params(3))(h, wgu, wgu)


def _mm_dswiglu(dy, wd, gate, up, *, scale, name):
    M, K = dy.shape
    F = wd.shape[0]
    tm, tn, tk = _tile(M, 512), _tile(F, MM_TILE_CAP), _tile(K, 1024)
    nm, nf, nk = M // tm, F // tn, K // tk

    def body(dy_ref, wd_ref, g_ref, u_ref, dg_ref, du_ref, acc):
        kk = pl.program_id(2)

        @pl.when(kk == 0)
        def _():
            acc[...] = jnp.zeros_like(acc)

        acc[...] += lax.dot_general(dy_ref[...].astype(BF16), wd_ref[...], (((1,), (1,)), ((), ())),
                                    preferred_element_type=F32)

        @pl.when(kk == nk - 1)
        def _():
            da = acc[...] * scale
            g, u = g_ref[...].astype(F32), u_ref[...].astype(F32)
            sg = jax.nn.sigmoid(g)
            du_ref[...] = (da * g * sg).astype(BF16)
            dg_ref[...] = (da * u * sg * (1.0 + g * (1.0 - sg))).astype(BF16)

    o_spec = pl.BlockSpec((tm, tn), lambda i, j, kk: (i, j))
    shp = jax.ShapeDtypeStruct((M, F), BF16)
    return pl.pallas_call(
        body, name=name, grid=(nm, nf, nk),
        in_specs=[pl.BlockSpec((tm, tk), lambda i, j, kk: (i, kk)),
                  pl.BlockSpec((tn, tk), lambda i, j, kk: (j, kk)), o_spec, o_spec],
        out_specs=[o_spec, o_spec], out_shape=[shp, shp],
        scratch_shapes=[pltpu.VMEM((tm, tn), F32)], compiler_params=_params(3))(dy, wd, gate, up)


HEAD_ROWS = 1024


def _row_tile(rows, cap):
    t = cap
    while t >= 8:
        if rows % t == 0:
            return t
        t //= 2
    return rows


def _rowwise(fn, rows, consts, outs, sums=(), hsums=(), *, heads=None, tm=256, name):
    rows = [r if isinstance(r, tuple) else (r, r.shape[1], None) for r in rows]
    S = rows[0][0].shape[0]
    tm = _row_tile(S, tm)
    nh = heads or 1
    n_r, n_c, n_o, n_h, n_s = len(rows), len(consts), len(outs), len(hsums), len(sums)

    def body(*refs):
        r = [x[...] for x in refs[:n_r]]
        c = [x[...] for x in refs[n_r:n_r + n_c]]
        o_refs = refs[n_r + n_c:n_r + n_c + n_o]
        h_refs = refs[n_r + n_c + n_o:n_r + n_c + n_o + n_h]
        s_refs = refs[n_r + n_c + n_o + n_h:]
        res = fn(*r, *c)
        res = res if isinstance(res, (tuple, list)) else (res,)
        for ref, val in zip(o_refs, res[:n_o]):
            ref[...] = val.astype(ref.dtype)
        if n_h:
            @pl.when(pl.program_id(1) == 0)
            def _():
                for ref in h_refs:
                    ref[...] = jnp.zeros_like(ref)
            for ref, val in zip(h_refs, res[n_o:n_o + n_h]):
                ref[...] += val
        if n_s:
            @pl.when((pl.program_id(0) == 0) & (pl.program_id(1) == 0))
            def _():
                for ref in s_refs:
                    ref[...] = jnp.zeros_like(ref)
            for ref, val in zip(s_refs, res[n_o + n_h:]):
                ref[...] += val

    def col(colfn):
        return (lambda i, h: (i, 0)) if colfn is None else (lambda i, h: (i, colfn(h)))

    in_specs = [pl.BlockSpec((tm, w), col(cf)) for _, w, cf in rows]
    in_specs += [pl.BlockSpec(a.shape, lambda i, h, nd=a.ndim: (0,) * nd) for a in consts]
    out_specs = [pl.BlockSpec((tm, w // nh), (lambda i, h: (i, h)) if heads else (lambda i, h: (i, 0))) for w, _ in outs]
    out_specs += [pl.BlockSpec((tm, w), lambda i, h: (i, 0)) for w in hsums]
    out_specs += [pl.BlockSpec(sh, lambda i, h, nd=len(sh): (0,) * nd) for sh in sums]
    out_shape = [jax.ShapeDtypeStruct((S, w), dt) for w, dt in outs]
    out_shape += [jax.ShapeDtypeStruct((S, w), F32) for w in hsums]
    out_shape += [jax.ShapeDtypeStruct(sh, F32) for sh in sums]
    return pl.pallas_call(body, name=name, grid=(S // tm, nh), in_specs=in_specs, out_specs=out_specs,
                          out_shape=out_shape, compiler_params=_params(2))(*[a for a, _, _ in rows], *consts)


def _rms(x, width=None):
    width = width or x.shape[-1]
    return lax.rsqrt(jnp.sum(x * x, axis=-1, keepdims=True) * (1.0 / width) + EPS)


def _rmsnorm_fwd(x, g, width=None):
    return x * _rms(x, width) * g


def _rmsnorm_bwd(dy, x, g, width=None):
    width = width or x.shape[-1]
    r = _rms(x, width)
    xn = x * r
    dxn = dy * g
    dx = r * (dxn - xn * (jnp.sum(dxn * xn, axis=-1, keepdims=True) * (1.0 / width)))
    return dx, jnp.sum(dy * xn, axis=0, keepdims=True)


def _norm_rows(x, g, *, name, out_dtype=BF16):
    D = x.shape[1]
    return _rowwise(lambda xv, gv: _rmsnorm_fwd(xv.astype(F32), gv), [x], [g.reshape(1, D)], [(D, out_dtype)],
                    name=name)[0]


def _norm_rows_bwd(dh, x, g, dres, *, name):
    D = x.shape[1]

    def fn(dhv, xv, *rest):
        dx, dg = _rmsnorm_bwd(dhv.astype(F32), xv, rest[-1])
        return (dx + rest[0] if dres is not None else dx), dg

    rows = [dh, x] + ([dres] if dres is not None else [])
    return _rowwise(fn, rows, [g.reshape(1, D)], [(D, F32)], [(1, D)], name=name)


def _softplus(z):
    return jnp.where(z > 20.0, z, jnp.log(1.0 + jnp.exp(z)))


def _running_sum(v, u, split=True):
    if not split:
        return jnp.dot(v.astype(BF16), u, preferred_element_type=F32)
    hi = lax.bitcast_convert_type(lax.bitcast_convert_type(v, jnp.uint32) & jnp.uint32(0xFFFF0000), F32)
    return (jnp.dot(hi.astype(BF16), u, preferred_element_type=F32)
            + jnp.dot((v - hi).astype(BF16), u, preferred_element_type=F32))


def _triangle(tk, inclusive_prefix):
    j, s = lax.broadcasted_iota(jnp.int32, (tk, tk), 0), lax.broadcasted_iota(jnp.int32, (tk, tk), 1)
    return ((j <= s) if inclusive_prefix else (j > s)).astype(BF16)


def _nt(a, b):
    return lax.dot_general(a, b, (((1,), (1,)), ((), ())), preferred_element_type=F32)


def _tn(a, b):
    return lax.dot_general(a, b, (((0,), (0,)), ((), ())), preferred_element_type=F32)


ATT_TQ, ATT_TK = 512, 512
SB_SUB = 256
FWD_GROUP = 2


def _attn_fwd(q, k, v, *, sb, causal, heads, dq, dv, group=1, kcol=None, vcol=None, sum_lane=None, name):
    S, Sk = q.shape[0], k.shape[0]
    tq, tk = min(ATT_TQ, S), min(ATT_TK, Sk)
    sub = min(SB_SUB, tk) if sb else tk
    assert tq % sub == 0 or not causal
    kcol = kcol or (lambda h: h)
    vcol = vcol or (lambda h: h)
    members = range(group)

    def body(*refs):
        if sb:
            q_ref, k_ref, v_ref, u_ref, o_ref, lse_ref, acc_ref, r_ref = refs
            r_ref[...] = jnp.zeros_like(r_ref)
        else:
            q_ref, k_ref, v_ref, o_ref, lse_ref, acc_ref, m_ref, l_ref = refs
            m_ref[...] = jnp.full_like(m_ref, -1e30)
            l_ref[...] = jnp.zeros_like(l_ref)
        first_row = pl.program_id(1) * tq
        qb = [q_ref[:, hh * dq:(hh + 1) * dq] for hh in members]
        acc_ref[...] = jnp.zeros_like(acc_ref)
        nblk = (first_row + tq) // sub if causal else Sk // sub
        nfull = (first_row + (0 if sb else 1)) // sub if causal else nblk
        n_cut = tq // sub if causal else 0

        def scores(jj):
            off = pl.multiple_of(jj * sub, sub)
            return tuple(_nt(qb[hh], k_ref[pl.ds(off, sub), hh * dq:(hh + 1) * dq]) for hh in members)

        def weigh(jj, scores_now, masked):
            off = pl.multiple_of(jj * sub, sub)
            if masked:
                kpos = off + lax.broadcasted_iota(jnp.int32, (tq, sub), 1)
                qpos = first_row + lax.broadcasted_iota(jnp.int32, (tq, sub), 0)
                valid = (kpos < qpos) if sb else (kpos <= qpos)
            for hh in members:
                vb = v_ref[pl.ds(off, sub), hh * dv:(hh + 1) * dv]
                s = scores_now[hh]
                if sb:
                    sp = _softplus(s)
                    ls = jnp.where(valid, -sp, 0.0) if masked else -sp
                    w = jnp.exp(s - sp + r_ref[hh] + _running_sum(ls, u_ref[...]))
                    if masked:
                        w = jnp.where(valid, w, 0.0)
                    acc_ref[hh] += jnp.dot(w.astype(BF16), vb, preferred_element_type=F32)
                    r_ref[hh] += jnp.sum(ls, axis=1, keepdims=True)
                else:
                    if masked:
                        s = jnp.where(valid, s, -1e30)
                    m_old = m_ref[hh]
                    m_new = jnp.maximum(m_old, jnp.max(s, axis=1, keepdims=True))
                    p = jnp.exp(s - m_new)
                    alpha = jnp.exp(m_old - m_new)
                    if sum_lane is None:
                        l_ref[hh] = alpha * l_ref[hh] + jnp.sum(p, axis=1, keepdims=True)
                    acc_ref[hh] = alpha * acc_ref[hh] + jnp.dot(p.astype(BF16), vb, preferred_element_type=F32)
                    m_ref[hh] = m_new

        if sb:
            s_cur = scores(nblk - 1)
            for cut in range(n_cut):
                s_next = scores(jnp.maximum(nblk - 2 - cut, 0))
                weigh(nblk - 1 - cut, s_cur, True)
                s_cur = s_next

            def step(t, s_now):
                s_next = scores(jnp.maximum(nfull - 2 - t, 0))
                weigh(nfull - 1 - t, s_now, False)
                return s_next

            lax.fori_loop(0, nfull, step, s_cur)
        else:
            n_loop = nfull if causal else nblk - 1

            def step(t, s_now):
                s_next = scores(jnp.minimum(t + 1, nblk - 1))
                weigh(t, s_now, False)
                return s_next

            s_cur = lax.fori_loop(0, n_loop, step, scores(0))
            tail = n_cut if causal else 1
            for last in range(tail):
                s_next = scores(n_loop + last + 1) if last + 1 < tail else None
                weigh(n_loop + last, s_cur, causal)
                s_cur = s_next
        for hh in members:
            cols = slice(hh * dv, (hh + 1) * dv)
            if sb:
                o_ref[:, cols] = acc_ref[hh]
                lse_ref[hh] = r_ref[hh]
            else:
                acc = acc_ref[hh]
                l = l_ref[hh] if sum_lane is None else acc[:, sum_lane:sum_lane + 1]
                o_ref[:, cols] = acc / l
                lse_ref[hh] = m_ref[hh] + jnp.log(l)

    in_specs = [pl.BlockSpec((tq, group * dq), lambda g, i: (i, g)),
                pl.BlockSpec((Sk, group * dq), lambda g, i: (0, kcol(g))),
                pl.BlockSpec((Sk, group * dv), lambda g, i: (0, vcol(g)))]
    ins = [q, k, v]
    scratch = [pltpu.VMEM((group, tq, dv), F32), pltpu.VMEM((group, tq, 1), F32)]
    if sb:
        ins.append(_triangle(sub, inclusive_prefix=False))
        in_specs.append(pl.BlockSpec((sub, sub), lambda g, i: (0, 0)))
    else:
        scratch.append(pltpu.VMEM((group, tq, 1), F32))
    out_specs = [pl.BlockSpec((tq, group * dv), lambda g, i: (i, g)), pl.BlockSpec((group, tq, 1), lambda g, i: (g, i, 0))]
    out_shape = [jax.ShapeDtypeStruct((S, heads * dv), F32), jax.ShapeDtypeStruct((heads, S, 1), F32)]
    return pl.pallas_call(body, name=name, grid=(heads // group, S // tq), in_specs=in_specs, out_specs=out_specs,
                          out_shape=out_shape, scratch_shapes=scratch, compiler_params=_params(2))(*ins)


def _attn_bwd(q, k, v, o, do, lse, *, sb, causal, heads, dq, dv, kcol=None, vcol=None, name):
    S, Sk = q.shape[0], k.shape[0]
    tq, tk = min(ATT_TQ, S), min(ATT_TK, Sk)
    sub = min(SB_SUB, tk) if sb else tk
    assert tq % sub == 0 or not causal
    nq = S // tq
    kcol = kcol or (lambda h: h)
    vcol = vcol or (lambda h: h)

    def body(*refs):
        if sb:
            q_ref, k_ref, v_ref, o_ref, do_ref, lse_ref, u_ref, dq_ref, dk_ref, dv_ref, acc_ref, r_ref, re_ref = refs
            r_ref[...] = jnp.zeros_like(r_ref)
            re_ref[...] = jnp.zeros_like(re_ref)
        else:
            q_ref, k_ref, v_ref, o_ref, do_ref, lse_ref, dq_ref, dk_ref, dv_ref, acc_ref = refs
        first_row = pl.program_id(1) * tq

        @pl.when(first_row == 0)
        def _():
            dk_ref[...] = jnp.zeros_like(dk_ref)
            dv_ref[...] = jnp.zeros_like(dv_ref)

        qb = q_ref[...]
        dof = do_ref[...].astype(F32)
        dob = dof.astype(BF16)
        if not sb:
            dlt = jnp.sum(dof * o_ref[...], axis=1, keepdims=True)
        acc_ref[...] = jnp.zeros_like(acc_ref)
        nblk = (first_row + tq) // sub if causal else Sk // sub
        nfull = (first_row + (0 if sb else 1)) // sub if causal else nblk
        n_cut = tq // sub if causal else 0

        def products(jj):
            off = pl.multiple_of(jj * sub, sub)
            return _nt(qb, k_ref[pl.ds(off, sub), :]), _nt(dob, v_ref[pl.ds(off, sub), :])

        def piece(jj, now, masked):
            off = pl.multiple_of(jj * sub, sub)
            kb = k_ref[pl.ds(off, sub), :]
            s, dp = now
            if masked:
                qpos = first_row + lax.broadcasted_iota(jnp.int32, (tq, sub), 0)
                kpos = off + lax.broadcasted_iota(jnp.int32, (tq, sub), 1)
                valid = (kpos < qpos) if sb else (kpos <= qpos)
            if sb:
                u = u_ref[...]
                sp = _softplus(s)
                ls = jnp.where(valid, -sp, 0.0) if masked else -sp
                lb = s - sp
                w = jnp.exp(lb + (lse_ref[0] - (r_ref[...] + _running_sum(ls, u))))
                if masked:
                    w = jnp.where(valid, w, 0.0)
                e = dp * w
                ds = e - jnp.exp(lb) * (re_ref[...] + _running_sum(e, u, split=False))
                if masked:
                    ds = jnp.where(valid, ds, 0.0)
                r_ref[...] += jnp.sum(ls, axis=1, keepdims=True)
                re_ref[...] += jnp.sum(e, axis=1, keepdims=True)
            else:
                w = jnp.exp(s - lse_ref[0])
                if masked:
                    w = jnp.where(valid, w, 0.0)
                ds = w * (dp - dlt)
            dsb = ds.astype(BF16)
            dv_ref[pl.ds(off, sub), :] += _tn(w.astype(BF16), dob)
            dk_ref[pl.ds(off, sub), :] += _tn(dsb, qb)
            acc_ref[...] += jnp.dot(dsb, kb, preferred_element_type=F32)

        n_loop = nfull if causal else nblk - 1
        tail = n_cut if causal else 1
        if sb:
            def step(t, now):
                nxt = products(jnp.minimum(t + 1, nblk - 1))
                piece(t, now, False)
                return nxt

            now = lax.fori_loop(0, n_loop, step, products(0))
            for last in range(tail):
                nxt = products(n_loop + last + 1) if last + 1 < tail else None
                piece(n_loop + last, now, causal)
                now = nxt
        else:
            def step(t, carry):
                piece(t, products(t), False)
                return carry

            lax.fori_loop(0, n_loop, step, 0)
            for last in range(tail):
                piece(n_loop + last, products(n_loop + last), causal)
        dq_ref[...] = acc_ref[...]

    ins = [q, k, v, o, do]
    in_specs = [pl.BlockSpec((tq, dq), lambda h, i: (i, h)),
                pl.BlockSpec((Sk, dq), lambda h, i: (0, kcol(h))),
                pl.BlockSpec((Sk, dv), lambda h, i: (0, vcol(h))),
                pl.BlockSpec((tq, dv), lambda h, i: (i, h)),
                pl.BlockSpec((tq, dv), lambda h, i: (i, h))]
    scratch = [pltpu.VMEM((tq, dq), F32)]
    ins.append(lse)
    in_specs.append(pl.BlockSpec((1, tq, 1), lambda h, i: (h, i, 0)))
    if sb:
        ins.append(_triangle(sub, inclusive_prefix=True))
        in_specs.append(pl.BlockSpec((sub, sub), lambda h, i: (0, 0)))
        scratch += [pltpu.VMEM((tq, 1), F32), pltpu.VMEM((tq, 1), F32)]
    out_specs = [pl.BlockSpec((tq, dq), lambda h, i: (i, h)),
                 pl.BlockSpec((Sk, dq), lambda h, i: (0, h)),
                 pl.BlockSpec((Sk, dv), lambda h, i: (0, h))]
    out_shape = [jax.ShapeDtypeStruct((S, heads * dq), F32), jax.ShapeDtypeStruct((Sk, heads * dq), F32),
                 jax.ShapeDtypeStruct((Sk, heads * dv), F32)]
    return pl.pallas_call(body, name=name, grid=(heads, nq), in_specs=in_specs, out_specs=out_specs,
                          out_shape=out_shape, scratch_shapes=scratch, compiler_params=_params(2))(*ins)


GELU_C = 0.7978845608028654
assert 2 * SG_GD == LANE and SG_CHUNK == LANE


def _gelu(z):
    t = jnp.tanh(GELU_C * (z + 0.044715 * z * z * z))
    return 0.5 * z * (1.0 + t), t


def _gelu_grad(z, t):
    return 0.5 * (1.0 + t) + 0.5 * z * (1.0 - t * t) * GELU_C * (1.0 + 3.0 * 0.044715 * z * z)


def _layernorm_parts(g):
    d = g - jnp.mean(g, axis=-1, keepdims=True)
    rstd = lax.rsqrt(jnp.mean(d * d, axis=-1, keepdims=True) + EPS)
    return d * rstd, rstd


def _gelu_ln(z, gain, bias, *, name):
    def fn(zv, gn, bs):
        a, _ = _gelu(zv)
        y, _ = _layernorm_parts(a[:, SG_W:])
        return a[:, :SG_W], y * gn + bs

    return _rowwise(fn, [z], [gain.reshape(1, SG_W), bias.reshape(1, SG_W)], [(SG_W, F32), (SG_W, BF16)], name=name)


def _gelu_ln_bwd(z, du, dgl, gain, *, name):
    def fn(zv, duv, dglv, gn):
        a, t = _gelu(zv)
        y, rstd = _layernorm_parts(a[:, SG_W:])
        dy = dglv * gn
        dgg = rstd * (dy - jnp.mean(dy, axis=-1, keepdims=True) - y * jnp.mean(dy * y, axis=-1, keepdims=True))
        dz = jnp.concatenate([duv, dgg], axis=1) * _gelu_grad(zv, t)
        return dz, jnp.sum(dglv * y, axis=0, keepdims=True), jnp.sum(dglv, axis=0, keepdims=True)

    return _rowwise(fn, [z, du, dgl], [gain.reshape(1, SG_W)], [(2 * SG_W, BF16)], [(1, SG_W), (1, SG_W)], name=name)


def _sg_masks():
    tri = lax.broadcasted_iota(jnp.int32, (SG_CHUNK, SG_CHUNK), 0) >= lax.broadcasted_iota(jnp.int32, (SG_CHUNK, SG_CHUNK), 1)
    first = lax.broadcasted_iota(jnp.int32, (SG_CHUNK, LANE), 1) < SG_GD
    return tri, first


def _spatial(gl, u, w, bt, *, name):
    S = gl.shape[0]
    tm = _row_tile(S, 512)
    nch = tm // SG_CHUNK

    def body(gl_ref, u_ref, w_ref, bt_ref, o_ref):
        tri, first = _sg_masks()
        for p in range(SG_W // LANE):
            cols = slice(p * LANE, (p + 1) * LANE)
            wa = jnp.where(tri, w_ref[2 * p], 0.0).astype(BF16)
            wb = jnp.where(tri, w_ref[2 * p + 1], 0.0).astype(BF16)
            for ci in range(nch):
                rws = slice(ci * SG_CHUNK, (ci + 1) * SG_CHUNK)
                g = gl_ref[rws, cols]
                zero = jnp.zeros_like(g)
                mixed = (jnp.dot(wa, jnp.where(first, g, zero), preferred_element_type=F32)
                         + jnp.dot(wb, jnp.where(first, zero, g), preferred_element_type=F32) + bt_ref[:, cols])
                o_ref[rws, cols] = u_ref[rws, cols] * mixed

    row = pl.BlockSpec((tm, SG_W), lambda i: (i, 0))
    return pl.pallas_call(
        body, name=name, grid=(S // tm,),
        in_specs=[row, row, pl.BlockSpec(w.shape, lambda i: (0, 0, 0)), pl.BlockSpec(bt.shape, lambda i: (0, 0))],
        out_specs=row, out_shape=jax.ShapeDtypeStruct((S, SG_W), F32), compiler_params=_params(1))(gl, u, w, bt)


def _spatial_bwd(d_o, gl, u, w, bt, *, name):
    S = gl.shape[0]
    tm = _row_tile(S, 512)
    nch = tm // SG_CHUNK
    nsteps = S // tm

    def body(do_ref, gl_ref, u_ref, w_ref, bt_ref, du_ref, dgl_ref, dw_ref, db_ref, dbt_ref):
        tri, first = _sg_masks()
        step = pl.program_id(0)

        @pl.when(step == 0)
        def _():
            dw_ref[...] = jnp.zeros_like(dw_ref)
            dbt_ref[...] = jnp.zeros_like(dbt_ref)

        for p in range(SG_W // LANE):
            cols = slice(p * LANE, (p + 1) * LANE)
            wa = jnp.where(tri, w_ref[2 * p], 0.0).astype(BF16)
            wb = jnp.where(tri, w_ref[2 * p + 1], 0.0).astype(BF16)
            for ci in range(nch):
                rws = slice(ci * SG_CHUNK, (ci + 1) * SG_CHUNK)
                g = gl_ref[rws, cols]
                zero = jnp.zeros_like(g)
                mixed = (jnp.dot(wa, jnp.where(first, g, zero), preferred_element_type=F32)
                         + jnp.dot(wb, jnp.where(first, zero, g), preferred_element_type=F32) + bt_ref[:, cols])
                dov = do_ref[rws, cols]
                du_ref[rws, cols] = dov * mixed
                dm = dov * u_ref[rws, cols]
                dbt_ref[:, cols] += dm
                dma = jnp.where(first, dm, 0.0).astype(BF16)
                dmb = jnp.where(first, 0.0, dm).astype(BF16)
                dw_ref[2 * p] += jnp.where(tri, _nt(dma, g), 0.0)
                dw_ref[2 * p + 1] += jnp.where(tri, _nt(dmb, g), 0.0)
                dgl_ref[rws, cols] = _tn(wa, dma) + _tn(wb, dmb)

        @pl.when(step == nsteps - 1)
        def _():
            lane = lax.broadcasted_iota(jnp.int32, (SG_CHUNK, LANE), 1)
            acc = jnp.zeros((SG_CHUNK, LANE), F32)
            for p in range(SG_W // LANE):
                blk = dbt_ref[:, p * LANE:(p + 1) * LANE]
                sa = jnp.sum(jnp.where(first, blk, 0.0), axis=1, keepdims=True)
                sb_ = jnp.sum(jnp.where(first, 0.0, blk), axis=1, keepdims=True)
                acc = acc + jnp.where(lane == 2 * p, sa, 0.0) + jnp.where(lane == 2 * p + 1, sb_, 0.0)
            db_ref[...] = acc

    row = pl.BlockSpec((tm, SG_W), lambda i: (i, 0))
    return pl.pallas_call(
        body, name=name, grid=(nsteps,),
        in_specs=[row, row, row, pl.BlockSpec(w.shape, lambda i: (0, 0, 0)), pl.BlockSpec(bt.shape, lambda i: (0, 0))],
        out_specs=[row, row, pl.BlockSpec(w.shape, lambda i: (0, 0, 0)), pl.BlockSpec((SG_CHUNK, LANE), lambda i: (0, 0))],
        out_shape=[jax.ShapeDtypeStruct((S, SG_W), F32), jax.ShapeDtypeStruct((S, SG_W), F32),
                   jax.ShapeDtypeStruct(w.shape, F32), jax.ShapeDtypeStruct((SG_CHUNK, LANE), F32)],
        scratch_shapes=[pltpu.VMEM((SG_CHUNK, SG_W), F32)], compiler_params=_params(1))(d_o, gl, u, w, bt)


ROPE_HALF = MLA_ROPE // 2
KR_COL = (MLA_QL + MLA_KVL) // LANE
MLA_IN_PAD = MLA_QL + MLA_KVL + LANE


def _rope_tables(positions):
    inv_freq = ROPE_THETA ** (-jnp.arange(ROPE_HALF, dtype=F32) / ROPE_HALF)
    ang = positions.astype(F32)[:, None] * inv_freq
    cos, sin = jnp.cos(ang), jnp.sin(ang)
    S = positions.shape[0]
    z16, tail = jnp.zeros((S, ROPE_HALF), F32), jnp.zeros((S, LANE - MLA_QK), F32)
    ones = jnp.ones((S, MLA_NOPE), F32)
    zeros = jnp.zeros((S, MLA_NOPE), F32)
    return (jnp.concatenate([ones, cos, cos, tail], axis=1), jnp.concatenate([zeros, z16, sin, tail], axis=1),
            jnp.concatenate([zeros, -sin, z16, tail], axis=1))


def _rope(x, cos, sa, sb):
    return x * cos + pltpu.roll(x, ROPE_HALF, 1) * sa + pltpu.roll(x, LANE - ROPE_HALF, 1) * sb


def _rope_t(dy, cos, sa, sb):
    return dy * cos + pltpu.roll(dy * sa, LANE - ROPE_HALF, 1) + pltpu.roll(dy * sb, ROPE_HALF, 1)


def _mla_lora(P, qlg, kvlg, *, name):
    def fn(pv, a, b):
        return _rmsnorm_fwd(pv[:, :MLA_QL], a), _rmsnorm_fwd(pv[:, MLA_QL:MLA_QL + MLA_KVL], b)

    return _rowwise(fn, [P], [qlg.reshape(1, MLA_QL), kvlg.reshape(1, MLA_KVL)], [(MLA_QL, BF16), (MLA_KVL, BF16)], name=name)


def _mla_lora_bwd(dcq, dckv, dkr, P, qlg, kvlg, *, name):
    def fn(d1, d2, d3, pv, a, b):
        x1, g1 = _rmsnorm_bwd(d1, pv[:, :MLA_QL], a)
        x2, g2 = _rmsnorm_bwd(d2, pv[:, MLA_QL:MLA_QL + MLA_KVL], b)
        return jnp.concatenate([x1, x2, d3], axis=1), g1, g2

    return _rowwise(fn, [dcq, dckv, dkr, P], [qlg.reshape(1, MLA_QL), kvlg.reshape(1, MLA_KVL)], [(MLA_IN_PAD, BF16)],
                    [(1, MLA_QL), (1, MLA_KVL)], name=name)


def _mla_qk(q_pre, k_pre, P, tabs, qg, kg, *, name):
    def fn(qp, kp, kr, c, a, b, qgv, kgv):
        return (_rope(_rmsnorm_fwd(qp, qgv, MLA_QK), c, a, b) * MLA_SCALE,
                _rope(_rmsnorm_fwd(kp + kr, kgv, MLA_QK), c, a, b))

    hcol = lambda h: h
    rows = [(q_pre, LANE, hcol), (k_pre, LANE, hcol), (P, LANE, lambda h: KR_COL), *tabs]
    w = MLA_HEADS * LANE
    return _rowwise(fn, rows, [qg, kg], [(w, BF16), (w, BF16)], heads=MLA_HEADS, tm=HEAD_ROWS, name=name)


def _mla_qk_bwd(dq, dk, q_pre, k_pre, P, tabs, qg, kg, *, name):
    def fn(dqv, dkv, qp, kp, kr, c, a, b, qgv, kgv):
        dqp, dqg = _rmsnorm_bwd(_rope_t(dqv * MLA_SCALE, c, a, b), qp, qgv, MLA_QK)
        dkp, dkg = _rmsnorm_bwd(_rope_t(dkv, c, a, b), kp + kr, kgv, MLA_QK)
        lane = lax.broadcasted_iota(jnp.int32, (1, LANE), 1)
        return dqp, dkp, jnp.where((lane >= MLA_NOPE) & (lane < MLA_QK), dkp, 0.0), dqg, dkg

    hcol = lambda h: h
    rows = [(dq, LANE, hcol), (dk, LANE, hcol), (q_pre, LANE, hcol), (k_pre, LANE, hcol), (P, LANE, lambda h: KR_COL), *tabs]
    w = MLA_HEADS * LANE
    return _rowwise(fn, rows, [qg, kg], [(w, BF16), (w, BF16)], [(1, LANE), (1, LANE)], [LANE], heads=MLA_HEADS,
                    tm=HEAD_ROWS, name=name)


def _head_norm(x, g, *, heads, width, colfn=None, scale=1.0, name):
    return _rowwise(lambda xv, gv: _rmsnorm_fwd(xv, gv) * scale, [(x, width, colfn or (lambda h: h))],
                    [g.reshape(1, width)], [(heads * width, BF16)], heads=heads, tm=HEAD_ROWS, name=name)[0]


def _head_norm_bwd(dy, x, g, *, heads, width, colfn=None, scale=1.0, out_dtype, name):
    return _rowwise(lambda dv_, xv, gv: _rmsnorm_bwd(dv_ * scale, xv, gv),
                    [(dy, width, lambda h: h), (x, width, colfn or (lambda h: h))],
                    [g.reshape(1, width)], [(heads * width, out_dtype)], [(1, width)], heads=heads, tm=HEAD_ROWS,
                    name=name)


def _loss_grad(y, tgt, *, name):
    D = y.shape[1]

    def fn(yv, tv):
        d = yv - tv
        return d * (1.0 / D), jnp.sum(d * d, axis=0, keepdims=True) * (0.5 / D)

    dy, part = _rowwise(fn, [y, tgt], [], [(D, F32)], [(1, D)], name=name)
    return jnp.sum(part), dy


def _adamw(w, g, m, v, *, name):
    shape = w.shape
    two_d = (-1, shape[-1])

    def fn(wv, gv, mv, vv):
        m2 = ADAM_B1 * mv + (1.0 - ADAM_B1) * gv
        v2 = ADAM_B2 * vv + (1.0 - ADAM_B2) * (gv * gv)
        m_hat = m2 / (1.0 - ADAM_B1 ** ADAM_STEP)
        v_hat = v2 / (1.0 - ADAM_B2 ** ADAM_STEP)
        return -ADAM_LR * (m_hat / (jnp.sqrt(v_hat) + ADAM_EPS) + ADAM_WD * wv), m2, v2

    outs = _rowwise(fn, [t.reshape(two_d) for t in (w, g, m, v)], [], [(shape[-1], F32)] * 3, name=name)
    return [o.reshape(shape) for o in outs]


def _pad_cols(w, heads, hd):
    k = w.shape[0]
    return jnp.pad(w.reshape(k, heads, hd), ((0, 0), (0, 0), (0, LANE - hd))).reshape(k, heads * LANE)


def _unpad_cols(w, heads, hd):
    k = w.shape[0]
    return w.reshape(k, heads, LANE)[:, :, :hd].reshape(k, heads * hd)


def _pad_rows(w, heads, hd):
    n = w.shape[1]
    return jnp.pad(w.reshape(heads, hd, n), ((0, 0), (0, LANE - hd), (0, 0))).reshape(heads * LANE, n)


def _unpad_rows(w, heads, hd):
    n = w.shape[1]
    return w.reshape(heads, LANE, n)[:, :hd, :].reshape(heads * hd, n)


def _ffn_fwd(x, g, wgu, wd, tag):
    h = _norm_rows(x, g, name=tag + "_norm")
    gate, up, act = _mm_swiglu(h, wgu, name=tag + "_gu")
    y = _mm(act, wd, scale=0.5, residual=x, name=tag + "_down")
    return y, (x, h, gate, up, act)


def _ffn_bwd(dy, saved, g, wgu, wd, tag):
    x, h, gate, up, act = saved
    F = wd.shape[0]
    dwd = _mm(act, dy, ta=True, scale=0.5, name=tag + "_dwd")
    dgate, dup = _mm_dswiglu(dy, wd, gate, up, scale=0.5, name=tag + "_dact")
    dh = _mm(dgate, wgu, tb=True, name=tag + "_dh_g")
    dh = _mm(dup, wgu, tb=True, b_off=(0, F), residual=dh, name=tag + "_dh_u")
    dwgu = jnp.concatenate([_mm(h, dgate, ta=True, name=tag + "_dwg"), _mm(h, dup, ta=True, name=tag + "_dwu")], axis=1)
    dx, dg = _norm_rows_bwd(dh, x, g, dy, name=tag + "_dnorm")
    return dx, dg, dwgu, dwd


def _even_weights(w_in, w_out):
    parts = [w_in[:, :SB_W] * SB_SCALE, w_in[:, SB_W:2 * SB_W], w_in[:, 2 * SB_W:3 * SB_W]]
    wqkv = jnp.concatenate([_pad_cols(p, SB_HEADS, SB_HD) for p in parts], axis=1)
    return wqkv, w_in[:, 3 * SB_W:], _pad_rows(w_out[:SB_W], SB_HEADS, SB_HD), w_out[SB_W:]


def _even_fwd(x, g, wts, ln_g, ln_b, sgu_w, bt, tag):
    wqkv, wz, wo_sb, wo_sg = wts
    h = _norm_rows(x, g, name=tag + "_norm")
    qkv = _mm(h, wqkv, out_dtype=BF16, name=tag + "_qkv")
    z = _mm(h, wz, name=tag + "_z")
    o_sb, tot = _attn_fwd(qkv, qkv, qkv, sb=True, causal=True, heads=SB_HEADS, dq=LANE, dv=LANE, group=FWD_GROUP,
                          kcol=lambda g: SB_HEADS // FWD_GROUP + g, vcol=lambda g: 2 * SB_HEADS // FWD_GROUP + g,
                          name=tag + "_sb")
    u, gl = _gelu_ln(z, ln_g, ln_b, name=tag + "_geluln")
    o_sg = _spatial(gl, u, sgu_w, bt, name=tag + "_sgu")
    y = _mm(o_sb, wo_sb, residual=x, name=tag + "_out_sb")
    y = _mm(o_sg, wo_sg, residual=y, name=tag + "_out_sg")
    return y, (x, h, qkv, z, o_sb, tot, u, gl, o_sg)


def _even_bwd(dy, saved, g, wts, ln_g, sgu_w, bt, tag):
    wqkv, wz, wo_sb, wo_sg = wts
    x, h, qkv, z, o_sb, tot, u, gl, o_sg = saved
    do_sb = _mm(dy, wo_sb, tb=True, name=tag + "_do_sb")
    do_sg = _mm(dy, wo_sg, tb=True, name=tag + "_do_sg")
    dwo = jnp.concatenate([_unpad_rows(_mm(o_sb, dy, ta=True, name=tag + "_dwo_sb"), SB_HEADS, SB_HD),
                           _mm(o_sg, dy, ta=True, name=tag + "_dwo_sg")], axis=0)
    dq, dk, dv = _attn_bwd(qkv, qkv, qkv, o_sb, do_sb, tot, sb=True, causal=True, heads=SB_HEADS, dq=LANE, dv=LANE,
                           kcol=lambda hh: SB_HEADS + hh, vcol=lambda hh: 2 * SB_HEADS + hh, name=tag + "_sb_bwd")
    du, dgl, dsgu_w, db_t = _spatial_bwd(do_sg, gl, u, sgu_w, bt, name=tag + "_sgu_bwd")
    dz, dln_g, dln_b = _gelu_ln_bwd(z, du, dgl, ln_g, name=tag + "_geluln_bwd")
    dh = _mm(dz, wz, tb=True, name=tag + "_dh_z")
    dws = []
    for i, (d, nm) in enumerate(((dq, "q"), (dk, "k"), (dv, "v"))):
        dh = _mm(d, wqkv, tb=True, b_off=(0, i * SB_HEADS * LANE), residual=dh, name=tag + "_dh_" + nm)
        dws.append(_unpad_cols(_mm(h, d, ta=True, scale=SB_SCALE if nm == "q" else 1.0, name=tag + "_dw_" + nm),
                               SB_HEADS, SB_HD))
    dws.append(_mm(h, dz, ta=True, name=tag + "_dw_z"))
    dx, dg = _norm_rows_bwd(dh, x, g, dy, name=tag + "_dnorm")
    return dx, dict(mix_norm=dg, sbg_w_in=jnp.concatenate(dws, axis=1), sgu_ln_gain=dln_g, sgu_ln_bias=dln_b,
                    sgu_w=dsgu_w, sgu_b=db_t[:, :SG_GROUPS].T, sbg_w_out=dwo)


def _mla_weights(w_in, w_uq, w_ukv, w_out, q_gain, k_gain):
    d = w_in.shape[0]
    lat = MLA_QL + MLA_KVL
    w_in_ext = jnp.concatenate([w_in[:, :lat], jnp.zeros((d, MLA_NOPE), w_in.dtype), w_in[:, lat:],
                                jnp.zeros((d, LANE - MLA_QK), w_in.dtype)], axis=1)
    kv = w_ukv.reshape(MLA_KVL, MLA_HEADS, MLA_NOPE + MLA_V)
    wk = _pad_cols(kv[:, :, :MLA_NOPE].reshape(MLA_KVL, -1), MLA_HEADS, MLA_NOPE)
    wv = _pad_cols(kv[:, :, MLA_NOPE:].reshape(MLA_KVL, -1), MLA_HEADS, MLA_V)
    pad_gain = lambda gn: jnp.pad(gn.reshape(1, MLA_QK), ((0, 0), (0, LANE - MLA_QK)))
    return (w_in_ext, _pad_cols(w_uq, MLA_HEADS, MLA_QK), wk, wv, _pad_rows(w_out, MLA_HEADS, MLA_V),
            pad_gain(q_gain), pad_gain(k_gain))


def _mla_fwd(x, g, wts, qlg, kvlg, tabs, tag):
    w_in, w_uq, wk, wv, w_out, qg, kg = wts
    h = _norm_rows(x, g, name=tag + "_norm")
    P = _mm(h, w_in, name=tag + "_in")
    cqn, ckvn = _mla_lora(P, qlg, kvlg, name=tag + "_lora")
    q_pre = _mm(cqn, w_uq, name=tag + "_uq")
    k_pre = _mm(ckvn, wk, name=tag + "_uk")
    ones_lane = jnp.tile((jnp.arange(LANE) == MLA_V).astype(F32), MLA_HEADS)[None, :]
    v = _mm(ckvn, wv, out_dtype=BF16, bias=ones_lane, name=tag + "_uv")
    q, k = _mla_qk(q_pre, k_pre, P, tabs, qg, kg, name=tag + "_qk")
    o, lse = _attn_fwd(q, k, v, sb=False, causal=True, heads=MLA_HEADS, dq=LANE, dv=LANE, group=FWD_GROUP,
                       sum_lane=MLA_V, name=tag + "_attn")
    y = _mm(o, w_out, residual=x, name=tag + "_out")
    return y, (x, h, P, cqn, ckvn, q_pre, k_pre, q, k, v, o, lse)


def _mla_bwd(dy, saved, g, wts, qlg, kvlg, tabs, tag):
    w_in, w_uq, wk, wv, w_out, qg, kg = wts
    x, h, P, cqn, ckvn, q_pre, k_pre, q, k, v, o, lse = saved
    do = _mm(dy, w_out, tb=True, name=tag + "_do")
    dw_out = _unpad_rows(_mm(o, dy, ta=True, name=tag + "_dwo"), MLA_HEADS, MLA_V)
    dq, dk, dv = _attn_bwd(q, k, v, o, do, lse, sb=False, causal=True, heads=MLA_HEADS, dq=LANE, dv=LANE,
                           name=tag + "_attn_bwd")
    dq_pre, dk_pre, dkr, dqg, dkg = _mla_qk_bwd(dq, dk, q_pre, k_pre, P, tabs, qg, kg, name=tag + "_qk_bwd")
    dcqn = _mm(dq_pre, w_uq, tb=True, name=tag + "_dcq")
    dckvn = _mm(dk_pre, wk, tb=True, name=tag + "_dckv_k")
    dckvn = _mm(dv, wv, tb=True, residual=dckvn, name=tag + "_dckv_v")
    dw_uq = _unpad_cols(_mm(cqn, dq_pre, ta=True, name=tag + "_dwuq"), MLA_HEADS, MLA_QK)
    dwk = _unpad_cols(_mm(ckvn, dk_pre, ta=True, name=tag + "_dwk"), MLA_HEADS, MLA_NOPE)
    dwv = _unpad_cols(_mm(ckvn, dv, ta=True, name=tag + "_dwv"), MLA_HEADS, MLA_V)
    dw_ukv = jnp.concatenate([dwk.reshape(MLA_KVL, MLA_HEADS, MLA_NOPE), dwv.reshape(MLA_KVL, MLA_HEADS, MLA_V)],
                             axis=2).reshape(MLA_KVL, -1)
    dP, dqlg, dkvlg = _mla_lora_bwd(dcqn, dckvn, dkr, P, qlg, kvlg, name=tag + "_lora_bwd")
    dh = _mm(dP, w_in, tb=True, name=tag + "_dh")
    dw_in_ext = _mm(h, dP, ta=True, name=tag + "_dwin")
    lat = MLA_QL + MLA_KVL
    dw_in = jnp.concatenate([dw_in_ext[:, :lat], dw_in_ext[:, lat + MLA_NOPE:lat + MLA_QK]], axis=1)
    dx, dg = _norm_rows_bwd(dh, x, g, dy, name=tag + "_dnorm")
    return dx, dict(mix_norm=dg, mla_w_in=dw_in, mla_q_lora_gain=dqlg, mla_kv_lora_gain=dkvlg, mla_w_uq=dw_uq,
                    mla_w_ukv=dw_ukv, mla_q_gain=dqg[:, :MLA_QK], mla_k_gain=dkg[:, :MLA_QK], mla_w_out=dw_out)


def _xmem_fwd(x, mem, g, gm, wq, wkv, qg, kg, wo, tag):
    hq = _norm_rows(x, g, name=tag + "_norm")
    hm = _norm_rows(mem, gm, name=tag + "_mnorm")
    qp = _mm(hq, wq, name=tag + "_q")
    kv = _mm(hm, wkv, name=tag + "_kv")
    q = _head_norm(qp, qg, heads=MEM_HEADS, width=MEM_HD, scale=MEM_SCALE, name=tag + "_qn")
    kn = _head_norm(kv, kg, heads=MEM_HEADS, width=MEM_HD, colfn=lambda hh: 2 * hh, name=tag + "_kn")
    kvb = kv.astype(BF16)
    o, lse = _attn_fwd(q, kn, kvb, sb=False, causal=False, heads=MEM_HEADS, dq=MEM_HD, dv=MEM_HD,
                       vcol=lambda hh: 2 * hh + 1, name=tag + "_attn")
    y = _mm(o, wo, residual=x, name=tag + "_out")
    return y, (x, hq, hm, qp, kv, q, kn, kvb, o, lse)


def _xmem_bwd(dy, saved, mem, g, gm, wq, wkv, qg, kg, wo, tag):
    x, hq, hm, qp, kv, q, kn, kvb, o, lse = saved
    m = mem.shape[0]
    do = _mm(dy, wo, tb=True, name=tag + "_do")
    dwo = _mm(o, dy, ta=True, name=tag + "_dwo")
    dq, dk, dv = _attn_bwd(q, kn, kvb, o, do, lse, sb=False, causal=False, heads=MEM_HEADS, dq=MEM_HD, dv=MEM_HD,
                           vcol=lambda hh: 2 * hh + 1, name=tag + "_attn_bwd")
    dqp, dqg = _head_norm_bwd(dq, qp, qg, heads=MEM_HEADS, width=MEM_HD, scale=MEM_SCALE, out_dtype=BF16,
                              name=tag + "_qn_bwd")
    dkp, dkg = _head_norm_bwd(dk, kv, kg, heads=MEM_HEADS, width=MEM_HD, colfn=lambda hh: 2 * hh, out_dtype=F32,
                              name=tag + "_kn_bwd")
    dkv = jnp.concatenate([dkp.reshape(m, MEM_HEADS, MEM_HD), dv.reshape(m, MEM_HEADS, MEM_HD)], axis=2).reshape(m, -1)
    dwkv = _mm(hm, dkv, ta=True, name=tag + "_dwkv")
    dhm = _mm(dkv, wkv, tb=True, name=tag + "_dhm")
    _, dgm = _norm_rows_bwd(dhm, mem, gm, None, name=tag + "_dmnorm")
    dwq = _mm(hq, dqp, ta=True, name=tag + "_dwq")
    dhq = _mm(dqp, wq, tb=True, name=tag + "_dhq")
    dx, dg = _norm_rows_bwd(dhq, x, g, dy, name=tag + "_dnorm")
    return dx, dict(xmem_norm=dg, xmem_mem_norm=dgm, xmem_wq=dwq, xmem_wkv=dwkv, xmem_q_gain=dqg, xmem_k_gain=dkg,
                    xmem_wo=dwo)


def _local_step(x, mem, positions, tgt, w):
    tabs = _rope_tables(positions)
    even = _even_weights(w["sbg_w_in"][0], w["sbg_w_out"][0])
    mla = _mla_weights(w["mla_w_in"][0], w["mla_w_uq"][0], w["mla_w_ukv"][0], w["mla_w_out"][0], w["mla_q_gain"][0],
                       w["mla_k_gain"][0])
    bt = jnp.repeat(w["sgu_b"][0].T, SG_GD, axis=1)
    saved = []
    for l in range(2):
        t = f"l{l}"
        x, s_pre = _ffn_fwd(x, w["ffn_pre_norm"][l], w["ffn_pre_w_gu"][l], w["ffn_pre_w_down"][l], t + "_pre")
        if l == 0:
            x, s_mix = _even_fwd(x, w["mix_norm"][0], even, w["sgu_ln_gain"][0], w["sgu_ln_bias"][0], w["sgu_w"][0], bt,
                                 t + "_even")
        else:
            x, s_mix = _mla_fwd(x, w["mix_norm"][1], mla, w["mla_q_lora_gain"][0], w["mla_kv_lora_gain"][0], tabs,
                                t + "_mla")
        x, s_xm = _xmem_fwd(x, mem, w["xmem_norm"][l], w["xmem_mem_norm"][l], w["xmem_wq"][l], w["xmem_wkv"][l],
                            w["xmem_q_gain"][l], w["xmem_k_gain"][l], w["xmem_wo"][l], t + "_xm")
        x, s_post = _ffn_fwd(x, w["ffn_post_norm"][l], w["ffn_post_w_gu"][l], w["ffn_post_w_down"][l], t + "_post")
        saved.append((s_pre, s_mix, s_xm, s_post))
    loss, dx = _loss_grad(x, tgt, name="loss")
    grads = {}

    def put(name, l, val):
        grads.setdefault(name, {})[l] = val

    for l in (1, 0):
        t = f"l{l}"
        s_pre, s_mix, s_xm, s_post = saved[l]
        dx, dg, dwgu, dwd = _ffn_bwd(dx, s_post, w["ffn_post_norm"][l], w["ffn_post_w_gu"][l], w["ffn_post_w_down"][l],
                                     t + "_post")
        put("ffn_post_norm", l, dg), put("ffn_post_w_gu", l, dwgu), put("ffn_post_w_down", l, dwd)
        dx, gx = _xmem_bwd(dx, s_xm, mem, w["xmem_norm"][l], w["xmem_mem_norm"][l], w["xmem_wq"][l], w["xmem_wkv"][l],
                           w["xmem_q_gain"][l], w["xmem_k_gain"][l], w["xmem_wo"][l], t + "_xm")
        for k_, v_ in gx.items():
            put(k_, l, v_)
        if l == 0:
            dx, gm = _even_bwd(dx, s_mix, w["mix_norm"][0], even, w["sgu_ln_gain"][0], w["sgu_w"][0], bt, t + "_even")
        else:
            dx, gm = _mla_bwd(dx, s_mix, w["mix_norm"][1], mla, w["mla_q_lora_gain"][0], w["mla_kv_lora_gain"][0], tabs,
                              t + "_mla")
        for k_, v_ in gm.items():
            put(k_, l if k_ == "mix_norm" else 0, v_)
        dx, dg, dwgu, dwd = _ffn_bwd(dx, s_pre, w["ffn_pre_norm"][l], w["ffn_pre_w_gu"][l], w["ffn_pre_w_down"][l],
                                     t + "_pre")
        put("ffn_pre_norm", l, dg), put("ffn_pre_w_gu", l, dwgu), put("ffn_pre_w_down", l, dwd)
    return loss, dx, {k_: [v_[l] for l in sorted(v_)] for k_, v_ in grads.items()}


N_CHIPS = 4
PACK_COLS = 1024
PACK_ROW_MULTIPLE = 512


def _place():
    x, y, c = lax.axis_index("x"), lax.axis_index("y"), lax.axis_index("c")
    return x, y, c, [(1 - x, y), (x, 1 - y), (1 - x, 1 - y)]


def _hops(x, y, c):
    return ((x + 1 - c) % 2, (y + c) % 2), ((x + c) % 2, (y + 1 - c) % 2), (1 - x, 1 - y)


def _gather_chips(shard):
    R, C = shard.shape
    Rh = R // 2

    def body(x_ref, out_ref, send_sems, recv_sems):
        x, y, c = lax.axis_index("x"), lax.axis_index("y"), lax.axis_index("c")
        n1, n2, nd = _hops(x, y, c)
        me, q1, q2, qd = 2 * x + y, 2 * n1[0] + n1[1], 2 * n2[0] + n2[1], 2 * nd[0] + nd[1]

        def half(chip, core):
            return out_ref.at[chip, pl.ds(core * Rh, Rh), :]

        def copy(k, chip, core, to, src=None):
            return pltpu.make_async_remote_copy(src_ref=half(chip, core) if src is None else src, dst_ref=half(chip, core),
                                                send_sem=send_sems.at[k], recv_sem=recv_sems.at[k], device_id=to,
                                                device_id_type=MESH)

        own = x_ref.at[pl.ds(c * Rh, Rh), :]
        sibling = (x, y, 1 - c)
        sends = [copy(0, me, c, (*n1, c), src=own), copy(1, me, c, (*n2, c), src=own)]
        sends[0].start()
        sends[1].start()
        copy(0, q1, c, sibling).wait_recv()
        sends += [copy(2, q1, c, (*n2, c)), copy(3, q1, c, sibling)]
        sends[2].start()
        sends[3].start()
        copy(1, q2, c, sibling).wait_recv()
        sends.append(copy(4, q2, c, sibling))
        sends[4].start()
        copy(2, qd, c, sibling).wait_recv()
        sends.append(copy(5, qd, c, sibling))
        sends[5].start()
        copy(3, q2, 1 - c, sibling).wait_recv()
        copy(4, q1, 1 - c, sibling).wait_recv()
        copy(5, qd, 1 - c, sibling).wait_recv()
        for cp in sends:
            cp.wait_send()

    others = pl.pallas_call(
        body, name="gather_weights", out_shape=jax.ShapeDtypeStruct((N_CHIPS, R, C), shard.dtype),
        in_specs=[ANY], out_specs=ANY,
        scratch_shapes=[pltpu.SemaphoreType.DMA((6,)), pltpu.SemaphoreType.DMA((6,))])(shard)
    me = 2 * lax.axis_index("x") + lax.axis_index("y")
    return lax.dynamic_update_slice(others, shard[None], (me, 0, 0))


def _gather_devices(block):
    M, N = block.shape

    def body(x_ref, out_ref, send_sems, recv_sems, local_sem):
        x, y, c, chips = _place()
        me, sibling = (x, y, c), (x, y, 1 - c)

        def rows(px, py, pc):
            return out_ref.at[pl.ds((4 * px + 2 * py + pc) * M, M), :]

        def copy(k, blk, to, src=None):
            return pltpu.make_async_remote_copy(src_ref=rows(*blk) if src is None else src, dst_ref=rows(*blk),
                                                send_sem=send_sems.at[k], recv_sem=recv_sems.at[k], device_id=to,
                                                device_id_type=MESH)

        mine = pltpu.make_async_copy(x_ref, rows(*me), local_sem)
        mine.start()
        first = [copy(0, me, sibling, src=x_ref)]
        first += [copy(1 + j, me, (*chip, c), src=x_ref) for j, chip in enumerate(chips)]
        for cp in first:
            cp.start()
        passed = [copy(4 + j, (*chip, c), sibling) for j, chip in enumerate(chips)]
        for j, chip in enumerate(chips):
            copy(1 + j, (*chip, c), me).wait_recv()
            passed[j].start()
        copy(0, sibling, me).wait_recv()
        for j, chip in enumerate(chips):
            copy(4 + j, (*chip, 1 - c), me).wait_recv()
        for cp in first + passed:
            cp.wait_send()
        mine.wait()

    vmem = pl.BlockSpec(memory_space=pltpu.VMEM)
    return pl.pallas_call(
        body, name=f"gather_devices_{M}", out_shape=jax.ShapeDtypeStruct((8 * M, N), block.dtype),
        in_specs=[vmem], out_specs=vmem,
        scratch_shapes=[pltpu.SemaphoreType.DMA((7,)), pltpu.SemaphoreType.DMA((7,)), pltpu.SemaphoreType.DMA],
        compiler_params=pltpu.CompilerParams(vmem_limit_bytes=VMEM_LIMIT))(block)


def _swap_halves(g):
    n, R, C = g.shape
    Rh = R // 2

    def body(g_ref, a_ref, send_sem, recv_sem):
        x, y, c, _ = _place()
        cp = pltpu.make_async_remote_copy(src_ref=g_ref.at[:, pl.ds((1 - c) * Rh, Rh), :], dst_ref=a_ref,
                                          send_sem=send_sem, recv_sem=recv_sem, device_id=(x, y, 1 - c),
                                          device_id_type=MESH)
        cp.start()
        cp.wait()

    return pl.pallas_call(body, name="grad_swap_halves", out_shape=jax.ShapeDtypeStruct((n, Rh, C), g.dtype),
                          in_specs=[ANY], out_specs=ANY,
                          scratch_shapes=[pltpu.SemaphoreType.DMA, pltpu.SemaphoreType.DMA])(g)


def _add_picked(a, b, picks, *, a_row_half=None, out_dtype, name):
    n_out = picks.shape[0]
    _, rows, C = b.shape
    tr = _row_tile(rows, 512)
    nt = rows // tr
    half = jnp.zeros((1,), jnp.int32) if a_row_half is None else a_row_half

    def body(pick_ref, half_ref, a_ref, b_ref, o_ref):
        o_ref[...] = (a_ref[...].astype(F32) + b_ref[...].astype(F32)).astype(o_ref.dtype)

    spec = pltpu.PrefetchScalarGridSpec(
        num_scalar_prefetch=2, grid=(n_out, nt),
        in_specs=[pl.BlockSpec((1, tr, C), lambda j, i, pick, hf: (pick[j], hf[0] * nt + i, 0)),
                  pl.BlockSpec((1, tr, C), lambda j, i, pick, hf: (pick[j], i, 0))],
        out_specs=pl.BlockSpec((1, tr, C), lambda j, i, pick, hf: (j, i, 0)))
    return pl.pallas_call(body, name=name, grid_spec=spec, out_shape=jax.ShapeDtypeStruct((n_out, rows, C), out_dtype),
                          compiler_params=_params(2))(picks.astype(jnp.int32), half.astype(jnp.int32), a, b)


def _hop_exchange(src, hop, *, name):
    def body(s_ref, d_ref, send_sem, recv_sem):
        x, y, c = lax.axis_index("x"), lax.axis_index("y"), lax.axis_index("c")
        cp = pltpu.make_async_remote_copy(src_ref=s_ref, dst_ref=d_ref, send_sem=send_sem, recv_sem=recv_sem,
                                          device_id=(*_hops(x, y, c)[hop], c), device_id_type=MESH)
        cp.start()
        cp.wait()

    return pl.pallas_call(body, name=name, out_shape=jax.ShapeDtypeStruct(src.shape, src.dtype), in_specs=[ANY],
                          out_specs=ANY, scratch_shapes=[pltpu.SemaphoreType.DMA, pltpu.SemaphoreType.DMA])(src)


def _reduce_over_chips(g):
    x, y, c = lax.axis_index("x"), lax.axis_index("y"), lax.axis_index("c")
    n1, n2, _ = _hops(x, y, c)
    chip = lambda p: 2 * p[0] + p[1]
    near = jnp.stack([chip((x, y)), chip(n2)])
    far = jnp.stack([chip(n1), chip((1 - x, 1 - y))])
    half = c.reshape(1)
    sib = _swap_halves(g)
    kept = _add_picked(g, sib, near, a_row_half=half, out_dtype=F32, name="grad_add_near")
    sent = _add_picked(g, sib, far, a_row_half=half, out_dtype=BF16, name="grad_add_far")
    got = _hop_exchange(sent, 0, name="grad_hop_first")
    mine = _add_picked(kept, got, jnp.zeros((1,), jnp.int32), out_dtype=F32, name="grad_add_mine")
    theirs = _add_picked(kept, got, jnp.ones((1,), jnp.int32), out_dtype=BF16, name="grad_add_theirs")
    got = _hop_exchange(theirs, 1, name="grad_hop_second")
    total = _add_picked(mine, got, jnp.zeros((1,), jnp.int32), out_dtype=F32, name="grad_add_total")
    return _join_halves(total[0])


def _sum_slots(b, *, name):
    n, R, C = b.shape
    tr = _row_tile(R, 512)

    def body(b_ref, o_ref):
        acc = b_ref[0]
        for q in range(1, n):
            acc = acc + b_ref[q]
        o_ref[...] = acc

    return pl.pallas_call(body, name=name, grid=(R // tr,), in_specs=[pl.BlockSpec((n, tr, C), lambda i: (0, i, 0))],
                          out_specs=pl.BlockSpec((tr, C), lambda i: (i, 0)), out_shape=jax.ShapeDtypeStruct((R, C), F32),
                          compiler_params=_params(1))(b)


def _join_halves(r):
    Rh, C = r.shape

    def body(r_ref, o_ref, send_sem, recv_sem):
        x, y, c, _ = _place()
        own, other = o_ref.at[pl.ds(c * Rh, Rh), :], o_ref.at[pl.ds((1 - c) * Rh, Rh), :]
        cp = pltpu.make_async_remote_copy(src_ref=r_ref, dst_ref=own, send_sem=send_sem, recv_sem=recv_sem,
                                          device_id=(x, y, 1 - c), device_id_type=MESH)
        cp.start()
        pltpu.make_async_remote_copy(src_ref=r_ref, dst_ref=other, send_sem=send_sem, recv_sem=recv_sem,
                                     device_id=(x, y, 1 - c), device_id_type=MESH).wait_recv()
        cp.wait_send()

    theirs = pl.pallas_call(
        body, name="grad_join_halves", out_shape=jax.ShapeDtypeStruct((2 * Rh, C), r.dtype), in_specs=[ANY], out_specs=ANY,
        scratch_shapes=[pltpu.SemaphoreType.DMA, pltpu.SemaphoreType.DMA])(r)
    return lax.dynamic_update_slice(theirs, r, (lax.axis_index("c") * Rh, 0))


def _size(shape):
    size = 1
    for d in shape:
        size *= d
    return size


def _pack(pieces, cols, row_multiple, dtype):
    if any(p.size % cols for p in pieces):
        flat = jnp.concatenate([p.reshape(-1).astype(dtype) for p in pieces])
        pieces = [jnp.pad(flat, (0, -flat.shape[0] % cols))]
    rows = jnp.concatenate([p.reshape(-1, cols).astype(dtype) for p in pieces], axis=0)
    return jnp.pad(rows, ((0, -rows.shape[0] % row_multiple), (0, 0)))


def _unpack(buf, shapes):
    cols = buf.shape[1]
    if any(_size(s) % cols for s in shapes):
        flat, out, at = buf.reshape(-1), [], 0
        for shp in shapes:
            out.append(flat[at:at + _size(shp)].reshape(shp))
            at += _size(shp)
        return out
    out, at = [], 0
    for shp in shapes:
        out.append(buf[at:at + _size(shp) // cols].reshape(shp))
        at += _size(shp) // cols
    return out


SHARDED = (("ffn_pre_w_gu", 2), ("ffn_pre_w_down", 1), ("sbg_w_in", 2), ("sbg_w_out", 1), ("mla_w_in", 1),
           ("mla_w_uq", 2), ("mla_w_ukv", 2), ("mla_w_out", 1), ("xmem_wq", 1), ("xmem_wkv", 2), ("xmem_wo", 1),
           ("ffn_post_w_gu", 2), ("ffn_post_w_down", 1))
LORA_GAINS = ("mla_q_lora_gain", "mla_kv_lora_gain")
REPLICATED = ("ffn_pre_norm", "mix_norm", "sgu_ln_gain", "sgu_ln_bias", "sgu_w", "sgu_b", "mla_q_gain", "mla_k_gain",
              "xmem_norm", "xmem_mem_norm", "xmem_q_gain", "xmem_k_gain", "ffn_post_norm")
WEIGHTS = ("ffn_pre_norm", "ffn_pre_w_gu", "ffn_pre_w_down", "mix_norm", "sbg_w_in", "sgu_ln_gain", "sgu_ln_bias", "sgu_w",
           "sgu_b", "sbg_w_out", "mla_w_in", "mla_q_lora_gain", "mla_kv_lora_gain", "mla_w_uq", "mla_w_ukv", "mla_q_gain",
           "mla_k_gain", "mla_w_out", "xmem_norm", "xmem_mem_norm", "xmem_wq", "xmem_wkv", "xmem_q_gain", "xmem_k_gain",
           "xmem_wo", "ffn_post_norm", "ffn_post_w_gu", "ffn_post_w_down")
INPUTS = ("x", "mem", "positions") + WEIGHTS + ("loss_target",) + tuple("m_" + n for n in WEIGHTS) + tuple(
    "v_" + n for n in WEIGHTS)


def _step(a):
    x, y, c, _ = _place()
    chip = 2 * x + y
    shard_shapes = [a[n].shape for n, _ in SHARDED]

    gathered = _gather_chips(_pack([a[n] for n, _ in SHARDED], PACK_COLS, PACK_ROW_MULTIPLE, BF16))
    parts = [_unpack(gathered[q], shard_shapes) for q in range(N_CHIPS)]
    w = {n: jnp.concatenate([parts[q][i] for q in range(N_CHIPS)], axis=ax) for i, (n, ax) in enumerate(SHARDED)}
    gains = jnp.zeros((8, LANE), F32)
    for r, n in enumerate(LORA_GAINS):
        gains = gains.at[r, :a[n].shape[1]].set(a[n][0])
    gains = _gather_devices(gains)
    for r, n in enumerate(LORA_GAINS):
        w[n] = jnp.concatenate([gains[16 * q + r, :a[n].shape[1]] for q in range(N_CHIPS)])[None, :]
    for n in REPLICATED:
        w[n] = a[n]

    loss, dx, grads = _local_step(a["x"][0], a["mem"][0], a["positions"][0], a["loss_target"][0], w)
    loss = lax.psum(loss, ("x", "y", "c"))
    small_names = REPLICATED + LORA_GAINS
    full = {n: jnp.stack(grads[n]).reshape(w[n].shape) for n in small_names}

    def cut(n, ax, q):
        size = w[n].shape[ax] // N_CHIPS
        return [lax.slice_in_dim(gl, q * size, (q + 1) * size, axis=ax - 1) for gl in grads[n]]

    g = jnp.stack([_pack([p for n, ax in SHARDED for p in cut(n, ax, q)], PACK_COLS, PACK_ROW_MULTIPLE, F32)
                   for q in range(N_CHIPS)])
    reduced = _reduce_over_chips(g)
    gw = dict(zip([n for n, _ in SHARDED], _unpack(reduced, shard_shapes)))

    small = _pack([full[n] for n in small_names], LANE, 256, F32)
    rows = small.shape[0]
    summed = _sum_slots(_gather_devices(small).reshape(8, rows, LANE), name="grad_sum_devices")
    for n, val in zip(small_names, _unpack(summed, [full[n].shape for n in small_names])):
        if n in LORA_GAINS:
            size = a[n].shape[1]
            val = lax.dynamic_slice_in_dim(val, chip * size, size, axis=1)
        gw[n] = val

    upd = {n: _adamw(a[n], gw[n], a["m_" + n], a["v_" + n], name="adamw_" + n) for n in WEIGHTS}
    return (loss, dx[None], *[gw[n] for n in WEIGHTS], *[upd[n][0] for n in WEIGHTS], *[upd[n][1] for n in WEIGHTS],
            *[upd[n][2] for n in WEIGHTS])


def kernel(x, mem, positions, ffn_pre_norm, ffn_pre_w_gu, ffn_pre_w_down, mix_norm, sbg_w_in, sgu_ln_gain,
           sgu_ln_bias, sgu_w, sgu_b, sbg_w_out, mla_w_in, mla_q_lora_gain, mla_kv_lora_gain, mla_w_uq, mla_w_ukv,
           mla_q_gain, mla_k_gain, mla_w_out, xmem_norm, xmem_mem_norm, xmem_wq, xmem_wkv, xmem_q_gain, xmem_k_gain,
           xmem_wo, ffn_post_norm, ffn_post_w_gu, ffn_post_w_down, loss_target, m_ffn_pre_norm, m_ffn_pre_w_gu,
           m_ffn_pre_w_down, m_mix_norm, m_sbg_w_in, m_sgu_ln_gain, m_sgu_ln_bias, m_sgu_w, m_sgu_b, m_sbg_w_out,
           m_mla_w_in, m_mla_q_lora_gain, m_mla_kv_lora_gain, m_mla_w_uq, m_mla_w_ukv, m_mla_q_gain, m_mla_k_gain,
           m_mla_w_out, m_xmem_norm, m_xmem_mem_norm, m_xmem_wq, m_xmem_wkv, m_xmem_q_gain, m_xmem_k_gain,
           m_xmem_wo, m_ffn_post_norm, m_ffn_post_w_gu, m_ffn_post_w_down, v_ffn_pre_norm, v_ffn_pre_w_gu,
           v_ffn_pre_w_down, v_mix_norm, v_sbg_w_in, v_sgu_ln_gain, v_sgu_ln_bias, v_sgu_w, v_sgu_b, v_sbg_w_out,
           v_mla_w_in, v_mla_q_lora_gain, v_mla_kv_lora_gain, v_mla_w_uq, v_mla_w_ukv, v_mla_q_gain, v_mla_k_gain,
           v_mla_w_out, v_xmem_norm, v_xmem_mem_norm, v_xmem_wq, v_xmem_wkv, v_xmem_q_gain, v_xmem_k_gain,
           v_xmem_wo, v_ffn_post_norm, v_ffn_post_w_gu, v_ffn_post_w_down):
    given = locals()
    return _step({n: given[n] for n in INPUTS})
```

```python
import functools

import jax
import jax.numpy as jnp
from jax import lax
from jax.experimental import pallas as pl
from jax.experimental.pallas import tpu as pltpu

F32, BF16 = jnp.float32, jnp.bfloat16
LANE = 128
VMEM_LIMIT = 56 * 1024 * 1024
EPS = 1e-6
D_FF = 2816
SB_HEADS, SB_HD = 8, 64
SG_GROUPS, SG_GD, SG_CHUNK = 8, 64, 128
SB_W, SG_W = SB_HEADS * SB_HD, SG_GROUPS * SG_GD
MLA_HEADS, MLA_NOPE, MLA_ROPE, MLA_V = 16, 64, 32, 64
MLA_QK = MLA_NOPE + MLA_ROPE
MLA_QL, MLA_KVL = 512, 256
ROPE_THETA = 10000.0
MEM_HEADS, MEM_HD = 4, 256
SB_SCALE, MLA_SCALE, MEM_SCALE = SB_HD ** -0.5, MLA_QK ** -0.5, MEM_HD ** -0.5
ADAM_LR, ADAM_B1, ADAM_B2, ADAM_EPS, ADAM_WD, ADAM_STEP = 0.001, 0.9, 0.999, 1e-08, 0.01, 10
MESH = pl.DeviceIdType.MESH
ANY = pl.BlockSpec(memory_space=pl.ANY)


def _params(n_axes):
    return pltpu.CompilerParams(dimension_semantics=("arbitrary",) * n_axes, vmem_limit_bytes=VMEM_LIMIT)


MM_TILE_CAP = 1408
MM_VMEM_BUDGET = 40 * 1024 * 1024


def _tile(dim, cap):
    if dim <= cap:
        return dim
    best = max(t for t in range(LANE, cap + 1, LANE) if dim % t == 0)
    return best


def _mm(a, b, *, ta=False, tb=False, out_dtype=F32, scale=1.0, residual=None, bias=None, a_off=(0, 0), b_off=(0, 0),
        m=None, n=None, k=None, name):
    am, ak = (a.shape[1], a.shape[0]) if ta else a.shape
    bk, bn = (b.shape[1], b.shape[0]) if tb else b.shape
    M, N, K = m or am, n or bn, k or ak
    tm, tn = _tile(M, MM_TILE_CAP), _tile(N, MM_TILE_CAP)
    fixed = tm * tn * (4 + 2 * jnp.dtype(out_dtype).itemsize + (2 * residual.dtype.itemsize if residual is not None else 0))
    per_k = (tm * (2 * a.dtype.itemsize + 2) + tn * (2 * b.dtype.itemsize + 2))
    tk = _tile(K, max(LANE, (MM_VMEM_BUDGET - fixed) // per_k))
    nm, nn, nk = M // tm, N // tn, K // tk
    a_off = (a_off[0] // (tk if ta else tm), a_off[1] // (tm if ta else tk))
    b_off = (b_off[0] // (tn if tb else tk), b_off[1] // (tk if tb else tn))
    dims = (((0 if ta else 1,), (1 if tb else 0,)), ((), ()))

    def body(*refs):
        a_ref, b_ref = refs[0], refs[1]
        o_ref, acc_ref = refs[-2], refs[-1]
        kk = pl.program_id(2)

        @pl.when(kk == 0)
        def _():
            acc_ref[...] = jnp.zeros_like(acc_ref)

        acc_ref[...] += lax.dot_general(a_ref[...].astype(BF16), b_ref[...].astype(BF16), dims,
                                        preferred_element_type=F32)

        @pl.when(kk == nk - 1)
        def _():
            out = acc_ref[...] * scale
            for extra in refs[2:-2]:
                out = out + extra[...].astype(F32)
            o_ref[...] = out.astype(o_ref.dtype)

    (ao0, ao1), (bo0, bo1) = a_off, b_off
    a_spec = (pl.BlockSpec((tk, tm), lambda i, j, kk: (kk + ao0, i + ao1)) if ta
              else pl.BlockSpec((tm, tk), lambda i, j, kk: (i + ao0, kk + ao1)))
    b_spec = (pl.BlockSpec((tn, tk), lambda i, j, kk: (j + bo0, kk + bo1)) if tb
              else pl.BlockSpec((tk, tn), lambda i, j, kk: (kk + bo0, j + bo1)))
    o_spec = pl.BlockSpec((tm, tn), lambda i, j, kk: (i, j))
    ins, in_specs = [a, b], [a_spec, b_spec]
    if residual is not None:
        ins.append(residual)
        in_specs.append(o_spec)
    if bias is not None:
        ins.append(bias)
        in_specs.append(pl.BlockSpec((1, tn), lambda i, j, kk: (0, j)))
    return pl.pallas_call(
        body, name=name, grid=(nm, nn, nk), in_specs=in_specs, out_specs=o_spec,
        out_shape=jax.ShapeDtypeStruct((M, N), out_dtype),
        scratch_shapes=[pltpu.VMEM((tm, tn), F32)], compiler_params=_params(3))(*ins)


def _mm_swiglu(h, wgu, *, name):
    M, K = h.shape
    F = wgu.shape[1] // 2
    tm, tn, tk = _tile(M, 512), _tile(F, MM_TILE_CAP), _tile(K, 1024)
    nm, nf, nk = M // tm, F // tn, K // tk

    def body(h_ref, wg_ref, wu_ref, g_ref, u_ref, a_ref, accg, accu):
        kk = pl.program_id(2)

        @pl.when(kk == 0)
        def _():
            accg[...] = jnp.zeros_like(accg)
            accu[...] = jnp.zeros_like(accu)

        hb = h_ref[...]
        accg[...] += jnp.dot(hb, wg_ref[...], preferred_element_type=F32)
        accu[...] += jnp.dot(hb, wu_ref[...], preferred_element_type=F32)

        @pl.when(kk == nk - 1)
        def _():
            g, u = accg[...], accu[...]
            g_ref[...] = g.astype(BF16)
            u_ref[...] = u.astype(BF16)
            a_ref[...] = (g * jax.nn.sigmoid(g) * u).astype(BF16)

    o_spec = pl.BlockSpec((tm, tn), lambda i, j, kk: (i, j))
    shp = jax.ShapeDtypeStruct((M, F), BF16)
    return pl.pallas_call(
        body, name=name, grid=(nm, nf, nk),
        in_specs=[pl.BlockSpec((tm, tk), lambda i, j, kk: (i, kk)),
                  pl.BlockSpec((tk, tn), lambda i, j, kk: (kk, j)),
                  pl.BlockSpec((tk, tn), lambda i, j, kk: (kk, j + nf))],
        out_specs=[o_spec, o_spec, o_spec], out_shape=[shp, shp, shp],
        scratch_shapes=[pltpu.VMEM((tm, tn), F32), pltpu.VMEM((tm, tn), F32)],
        compiler_params=_params(3))(h, wgu, wgu)


def _mm_dswiglu(dy, wd, gate, up, *, scale, name):
    M, K = dy.shape
    F = wd.shape[0]
    tm, tn, tk = _tile(M, 512), _tile(F, MM_TILE_CAP), _tile(K, 1024)
    nm, nf, nk = M // tm, F // tn, K // tk

    def body(dy_ref, wd_ref, g_ref, u_ref, dg_ref, du_ref, acc):
        kk = pl.program_id(2)

        @pl.when(kk == 0)
        def _():
            acc[...] = jnp.zeros_like(acc)

        acc[...] += lax.dot_general(dy_ref[...].astype(BF16), wd_ref[...], (((1,), (1,)), ((), ())),
                                    preferred_element_type=F32)

        @pl.when(kk == nk - 1)
        def _():
            da = acc[...] * scale
            g, u = g_ref[...].astype(F32), u_ref[...].astype(F32)
            sg = jax.nn.sigmoid(g)
            du_ref[...] = (da * g * sg).astype(BF16)
            dg_ref[...] = (da * u * sg * (1.0 + g * (1.0 - sg))).astype(BF16)

    o_spec = pl.BlockSpec((tm, tn), lambda i, j, kk: (i, j))
    shp = jax.ShapeDtypeStruct((M, F), BF16)
    return pl.pallas_call(
        body, name=name, grid=(nm, nf, nk),
        in_specs=[pl.BlockSpec((tm, tk), lambda i, j, kk: (i, kk)),
                  pl.BlockSpec((tn, tk), lambda i, j, kk: (j, kk)), o_spec, o_spec],
        out_specs=[o_spec, o_spec], out_shape=[shp, shp],
        scratch_shapes=[pltpu.VMEM((tm, tn), F32)], compiler_params=_params(3))(dy, wd, gate, up)


HEAD_ROWS = 1024


def _row_tile(rows, cap):
    t = cap
    while t >= 8:
        if rows % t == 0:
            return t
        t //= 2
    return rows


def _rowwise(fn, rows, consts, outs, sums=(), hsums=(), *, heads=None, tm=256, name):
    rows = [r if isinstance(r, tuple) else (r, r.shape[1], None) for r in rows]
    S = rows[0][0].shape[0]
    tm = _row_tile(S, tm)
    nh = heads or 1
    n_r, n_c, n_o, n_h, n_s = len(rows), len(consts), len(outs), len(hsums), len(sums)

    def body(*refs):
        r = [x[...] for x in refs[:n_r]]
        c = [x[...] for x in refs[n_r:n_r + n_c]]
        o_refs = refs[n_r + n_c:n_r + n_c + n_o]
        h_refs = refs[n_r + n_c + n_o:n_r + n_c + n_o + n_h]
        s_refs = refs[n_r + n_c + n_o + n_h:]
        res = fn(*r, *c)
        res = res if isinstance(res, (tuple, list)) else (res,)
        for ref, val in zip(o_refs, res[:n_o]):
            ref[...] = val.astype(ref.dtype)
        if n_h:
            @pl.when(pl.program_id(1) == 0)
            def _():
                for ref in h_refs:
                    ref[...] = jnp.zeros_like(ref)
            for ref, val in zip(h_refs, res[n_o:n_o + n_h]):
                ref[...] += val
        if n_s:
            @pl.when((pl.program_id(0) == 0) & (pl.program_id(1) == 0))
            def _():
                for ref in s_refs:
                    ref[...] = jnp.zeros_like(ref)
            for ref, val in zip(s_refs, res[n_o + n_h:]):
                ref[...] += val

    def col(colfn):
        return (lambda i, h: (i, 0)) if colfn is None else (lambda i, h: (i, colfn(h)))

    in_specs = [pl.BlockSpec((tm, w), col(cf)) for _, w, cf in rows]
    in_specs += [pl.BlockSpec(a.shape, lambda i, h, nd=a.ndim: (0,) * nd) for a in consts]
    out_specs = [pl.BlockSpec((tm, w // nh), (lambda i, h: (i, h)) if heads else (lambda i, h: (i, 0))) for w, _ in outs]
    out_specs += [pl.BlockSpec((tm, w), lambda i, h: (i, 0)) for w in hsums]
    out_specs += [pl.BlockSpec(sh, lambda i, h, nd=len(sh): (0,) * nd) for sh in sums]
    out_shape = [jax.ShapeDtypeStruct((S, w), dt) for w, dt in outs]
    out_shape += [jax.ShapeDtypeStruct((S, w), F32) for w in hsums]
    out_shape += [jax.ShapeDtypeStruct(sh, F32) for sh in sums]
    return pl.pallas_call(body, name=name, grid=(S // tm, nh), in_specs=in_specs, out_specs=out_specs,
                          out_shape=out_shape, compiler_params=_params(2))(*[a for a, _, _ in rows], *consts)


def _rms(x, width=None):
    width = width or x.shape[-1]
    return lax.rsqrt(jnp.sum(x * x, axis=-1, keepdims=True) * (1.0 / width) + EPS)


def _rmsnorm_fwd(x, g, width=None):
    return x * _rms(x, width) * g


def _rmsnorm_bwd(dy, x, g, width=None):
    width = width or x.shape[-1]
    r = _rms(x, width)
    xn = x * r
    dxn = dy * g
    dx = r * (dxn - xn * (jnp.sum(dxn * xn, axis=-1, keepdims=True) * (1.0 / width)))
    return dx, jnp.sum(dy * xn, axis=0, keepdims=True)


def _norm_rows(x, g, *, name, out_dtype=BF16):
    D = x.shape[1]
    return _rowwise(lambda xv, gv: _rmsnorm_fwd(xv.astype(F32), gv), [x], [g.reshape(1, D)], [(D, out_dtype)],
                    name=name)[0]


def _norm_rows_bwd(dh, x, g, dres, *, name):
    D = x.shape[1]

    def fn(dhv, xv, *rest):
        dx, dg = _rmsnorm_bwd(dhv.astype(F32), xv, rest[-1])
        return (dx + rest[0] if dres is not None else dx), dg

    rows = [dh, x] + ([dres] if dres is not None else [])
    return _rowwise(fn, rows, [g.reshape(1, D)], [(D, F32)], [(1, D)], name=name)


def _softplus(z):
    return jnp.where(z > 20.0, z, jnp.log(1.0 + jnp.exp(z)))


def _running_sum(v, u, split=True):
    if not split:
        return jnp.dot(v.astype(BF16), u, preferred_element_type=F32)
    hi = lax.bitcast_convert_type(lax.bitcast_convert_type(v, jnp.uint32) & jnp.uint32(0xFFFF0000), F32)
    return (jnp.dot(hi.astype(BF16), u, preferred_element_type=F32)
            + jnp.dot((v - hi).astype(BF16), u, preferred_element_type=F32))


def _triangle(tk, inclusive_prefix):
    j, s = lax.broadcasted_iota(jnp.int32, (tk, tk), 0), lax.broadcasted_iota(jnp.int32, (tk, tk), 1)
    return ((j <= s) if inclusive_prefix else (j > s)).astype(BF16)


def _nt(a, b):
    return lax.dot_general(a, b, (((1,), (1,)), ((), ())), preferred_element_type=F32)


def _tn(a, b):
    return lax.dot_general(a, b, (((0,), (0,)), ((), ())), preferred_element_type=F32)


ATT_TQ, ATT_TK = 512, 512
SB_SUB = 256
FWD_GROUP = 2


def _attn_fwd(q, k, v, *, sb, causal, heads, dq, dv, group=1, kcol=None, vcol=None, sum_lane=None, name):
    S, Sk = q.shape[0], k.shape[0]
    tq, tk = min(ATT_TQ, S), min(ATT_TK, Sk)
    sub = min(SB_SUB, tk) if sb else tk
    assert tq % sub == 0 or not causal
    kcol = kcol or (lambda h: h)
    vcol = vcol or (lambda h: h)
    members = range(group)

    def body(*refs):
        if sb:
            q_ref, k_ref, v_ref, u_ref, o_ref, lse_ref, acc_ref, r_ref = refs
            r_ref[...] = jnp.zeros_like(r_ref)
        else:
            q_ref, k_ref, v_ref, o_ref, lse_ref, acc_ref, m_ref, l_ref = refs
            m_ref[...] = jnp.full_like(m_ref, -1e30)
            l_ref[...] = jnp.zeros_like(l_ref)
        first_row = pl.program_id(1) * tq
        qb = [q_ref[:, hh * dq:(hh + 1) * dq] for hh in members]
        acc_ref[...] = jnp.zeros_like(acc_ref)
        nblk = (first_row + tq) // sub if causal else Sk // sub
        nfull = (first_row + (0 if sb else 1)) // sub if causal else nblk
        n_cut = tq // sub if causal else 0

        def scores(jj):
            off = pl.multiple_of(jj * sub, sub)
            return tuple(_nt(qb[hh], k_ref[pl.ds(off, sub), hh * dq:(hh + 1) * dq]) for hh in members)

        def weigh(jj, scores_now, masked):
            off = pl.multiple_of(jj * sub, sub)
            if masked:
                kpos = off + lax.broadcasted_iota(jnp.int32, (tq, sub), 1)
                qpos = first_row + lax.broadcasted_iota(jnp.int32, (tq, sub), 0)
                valid = (kpos < qpos) if sb else (kpos <= qpos)
            for hh in members:
                vb = v_ref[pl.ds(off, sub), hh * dv:(hh + 1) * dv]
                s = scores_now[hh]
                if sb:
                    sp = _softplus(s)
                    ls = jnp.where(valid, -sp, 0.0) if masked else -sp
                    w = jnp.exp(s - sp + r_ref[hh] + _running_sum(ls, u_ref[...]))
                    if masked:
                        w = jnp.where(valid, w, 0.0)
                    acc_ref[hh] += jnp.dot(w.astype(BF16), vb, preferred_element_type=F32)
                    r_ref[hh] += jnp.sum(ls, axis=1, keepdims=True)
                else:
                    if masked:
                        s = jnp.where(valid, s, -1e30)
                    m_old = m_ref[hh]
                    m_new = jnp.maximum(m_old, jnp.max(s, axis=1, keepdims=True))
                    p = jnp.exp(s - m_new)
                    alpha = jnp.exp(m_old - m_new)
                    if sum_lane is None:
                        l_ref[hh] = alpha * l_ref[hh] + jnp.sum(p, axis=1, keepdims=True)
                    acc_ref[hh] = alpha * acc_ref[hh] + jnp.dot(p.astype(BF16), vb, preferred_element_type=F32)
                    m_ref[hh] = m_new

        if sb:
            s_cur = scores(nblk - 1)
            for cut in range(n_cut):
                s_next = scores(jnp.maximum(nblk - 2 - cut, 0))
                weigh(nblk - 1 - cut, s_cur, True)
                s_cur = s_next

            def step(t, s_now):
                s_next = scores(jnp.maximum(nfull - 2 - t, 0))
                weigh(nfull - 1 - t, s_now, False)
                return s_next

            lax.fori_loop(0, nfull, step, s_cur)
        else:
            n_loop = nfull if causal else nblk - 1

            def step(t, s_now):
                s_next = scores(jnp.minimum(t + 1, nblk - 1))
                weigh(t, s_now, False)
                return s_next

            s_cur = lax.fori_loop(0, n_loop, step, scores(0))
            tail = n_cut if causal else 1
            for last in range(tail):
                s_next = scores(n_loop + last + 1) if last + 1 < tail else None
                weigh(n_loop + last, s_cur, causal)
                s_cur = s_next
        for hh in members:
            cols = slice(hh * dv, (hh + 1) * dv)
            if sb:
                o_ref[:, cols] = acc_ref[hh]
                lse_ref[hh] = r_ref[hh]
            else:
                acc = acc_ref[hh]
                l = l_ref[hh] if sum_lane is None else acc[:, sum_lane:sum_lane + 1]
                o_ref[:, cols] = acc / l
                lse_ref[hh] = m_ref[hh] + jnp.log(l)

    in_specs = [pl.BlockSpec((tq, group * dq), lambda g, i: (i, g)),
                pl.BlockSpec((Sk, group * dq), lambda g, i: (0, kcol(g))),
                pl.BlockSpec((Sk, group * dv), lambda g, i: (0, vcol(g)))]
    ins = [q, k, v]
    scratch = [pltpu.VMEM((group, tq, dv), F32), pltpu.VMEM((group, tq, 1), F32)]
    if sb:
        ins.append(_triangle(sub, inclusive_prefix=False))
        in_specs.append(pl.BlockSpec((sub, sub), lambda g, i: (0, 0)))
    else:
        scratch.append(pltpu.VMEM((group, tq, 1), F32))
    out_specs = [pl.BlockSpec((tq, group * dv), lambda g, i: (i, g)), pl.BlockSpec((group, tq, 1), lambda g, i: (g, i, 0))]
    out_shape = [jax.ShapeDtypeStruct((S, heads * dv), F32), jax.ShapeDtypeStruct((heads, S, 1), F32)]
    return pl.pallas_call(body, name=name, grid=(heads // group, S // tq), in_specs=in_specs, out_specs=out_specs,
                          out_shape=out_shape, scratch_shapes=scratch, compiler_params=_params(2))(*ins)


def _attn_bwd(q, k, v, o, do, lse, *, sb, causal, heads, dq, dv, kcol=None, vcol=None, name):
    S, Sk = q.shape[0], k.shape[0]
    tq, tk = min(ATT_TQ, S), min(ATT_TK, Sk)
    sub = min(SB_SUB, tk) if sb else tk
    assert tq % sub == 0 or not causal
    nq = S // tq
    kcol = kcol or (lambda h: h)
    vcol = vcol or (lambda h: h)

    def body(*refs):
        if sb:
            q_ref, k_ref, v_ref, o_ref, do_ref, lse_ref, u_ref, dq_ref, dk_ref, dv_ref, acc_ref, r_ref, re_ref = refs
            r_ref[...] = jnp.zeros_like(r_ref)
            re_ref[...] = jnp.zeros_like(re_ref)
        else:
            q_ref, k_ref, v_ref, o_ref, do_ref, lse_ref, dq_ref, dk_ref, dv_ref, acc_ref = refs
        first_row = pl.program_id(1) * tq

        @pl.when(first_row == 0)
        def _():
            dk_ref[...] = jnp.zeros_like(dk_ref)
            dv_ref[...] = jnp.zeros_like(dv_ref)

        qb = q_ref[...]
        dof = do_ref[...].astype(F32)
        dob = dof.astype(BF16)
        if not sb:
            dlt = jnp.sum(dof * o_ref[...], axis=1, keepdims=True)
        acc_ref[...] = jnp.zeros_like(acc_ref)
        nblk = (first_row + tq) // sub if causal else Sk // sub
        nfull = (first_row + (0 if sb else 1)) // sub if causal else nblk
        n_cut = tq // sub if causal else 0

        def products(jj):
            off = pl.multiple_of(jj * sub, sub)
            return _nt(qb, k_ref[pl.ds(off, sub), :]), _nt(dob, v_ref[pl.ds(off, sub), :])

        def piece(jj, now, masked):
            off = pl.multiple_of(jj * sub, sub)
            kb = k_ref[pl.ds(off, sub), :]
            s, dp = now
            if masked:
                qpos = first_row + lax.broadcasted_iota(jnp.int32, (tq, sub), 0)
                kpos = off + lax.broadcasted_iota(jnp.int32, (tq, sub), 1)
                valid = (kpos < qpos) if sb else (kpos <= qpos)
            if sb:
                u = u_ref[...]
                sp = _softplus(s)
                ls = jnp.where(valid, -sp, 0.0) if masked else -sp
                lb = s - sp
                w = jnp.exp(lb + (lse_ref[0] - (r_ref[...] + _running_sum(ls, u))))
                if masked:
                    w = jnp.where(valid, w, 0.0)
                e = dp * w
                ds = e - jnp.exp(lb) * (re_ref[...] + _running_sum(e, u, split=False))
                if masked:
                    ds = jnp.where(valid, ds, 0.0)
                r_ref[...] += jnp.sum(ls, axis=1, keepdims=True)
                re_ref[...] += jnp.sum(e, axis=1, keepdims=True)
            else:
                w = jnp.exp(s - lse_ref[0])
                if masked:
                    w = jnp.where(valid, w, 0.0)
                ds = w * (dp - dlt)
            dsb = ds.astype(BF16)
            dv_ref[pl.ds(off, sub), :] += _tn(w.astype(BF16), dob)
            dk_ref[pl.ds(off, sub), :] += _tn(dsb, qb)
            acc_ref[...] += jnp.dot(dsb, kb, preferred_element_type=F32)

        n_loop = nfull if causal else nblk - 1
        per_trip = tk // sub

        def trip(t, carry):
            for c in range(per_trip):
                piece(t * per_trip + c, products(t * per_trip + c), False)
            return carry

        lax.fori_loop(0, n_loop // per_trip, trip, 0)
        for last in range(n_cut if causal else 1):
            piece(n_loop + last, products(n_loop + last), causal)
        dq_ref[...] = acc_ref[...]

    ins = [q, k, v, o, do]
    in_specs = [pl.BlockSpec((tq, dq), lambda h, i: (i, h)),
                pl.BlockSpec((Sk, dq), lambda h, i: (0, kcol(h))),
                pl.BlockSpec((Sk, dv), lambda h, i: (0, vcol(h))),
                pl.BlockSpec((tq, dv), lambda h, i: (i, h)),
                pl.BlockSpec((tq, dv), lambda h, i: (i, h))]
    scratch = [pltpu.VMEM((tq, dq), F32)]
    ins.append(lse)
    in_specs.append(pl.BlockSpec((1, tq, 1), lambda h, i: (h, i, 0)))
    if sb:
        ins.append(_triangle(sub, inclusive_prefix=True))
        in_specs.append(pl.BlockSpec((sub, sub), lambda h, i: (0, 0)))
        scratch += [pltpu.VMEM((tq, 1), F32), pltpu.VMEM((tq, 1), F32)]
    out_specs = [pl.BlockSpec((tq, dq), lambda h, i: (i, h)),
                 pl.BlockSpec((Sk, dq), lambda h, i: (0, h)),
                 pl.BlockSpec((Sk, dv), lambda h, i: (0, h))]
    out_shape = [jax.ShapeDtypeStruct((S, heads * dq), F32), jax.ShapeDtypeStruct((Sk, heads * dq), F32),
                 jax.ShapeDtypeStruct((Sk, heads * dv), F32)]
    return pl.pallas_call(body, name=name, grid=(heads, nq), in_specs=in_specs, out_specs=out_specs,
                          out_shape=out_shape, scratch_shapes=scratch, compiler_params=_params(2))(*ins)


GELU_C = 0.7978845608028654
assert 2 * SG_GD == LANE and SG_CHUNK == LANE


def _gelu(z):
    t = jnp.tanh(GELU_C * (z + 0.044715 * z * z * z))
    return 0.5 * z * (1.0 + t), t


def _gelu_grad(z, t):
    return 0.5 * (1.0 + t) + 0.5 * z * (1.0 - t * t) * GELU_C * (1.0 + 3.0 * 0.044715 * z * z)


def _layernorm_parts(g):
    d = g - jnp.mean(g, axis=-1, keepdims=True)
    rstd = lax.rsqrt(jnp.mean(d * d, axis=-1, keepdims=True) + EPS)
    return d * rstd, rstd


def _gelu_ln(z, gain, bias, *, name):
    def fn(zv, gn, bs):
        a, _ = _gelu(zv)
        y, _ = _layernorm_parts(a[:, SG_W:])
        return a[:, :SG_W], y * gn + bs

    return _rowwise(fn, [z], [gain.reshape(1, SG_W), bias.reshape(1, SG_W)], [(SG_W, F32), (SG_W, BF16)], name=name)


def _gelu_ln_bwd(z, du, dgl, gain, *, name):
    def fn(zv, duv, dglv, gn):
        a, t = _gelu(zv)
        y, rstd = _layernorm_parts(a[:, SG_W:])
        dy = dglv * gn
        dgg = rstd * (dy - jnp.mean(dy, axis=-1, keepdims=True) - y * jnp.mean(dy * y, axis=-1, keepdims=True))
        dz = jnp.concatenate([duv, dgg], axis=1) * _gelu_grad(zv, t)
        return dz, jnp.sum(dglv * y, axis=0, keepdims=True), jnp.sum(dglv, axis=0, keepdims=True)

    return _rowwise(fn, [z, du, dgl], [gain.reshape(1, SG_W)], [(2 * SG_W, BF16)], [(1, SG_W), (1, SG_W)], name=name)


def _sg_masks():
    tri = lax.broadcasted_iota(jnp.int32, (SG_CHUNK, SG_CHUNK), 0) >= lax.broadcasted_iota(jnp.int32, (SG_CHUNK, SG_CHUNK), 1)
    first = lax.broadcasted_iota(jnp.int32, (SG_CHUNK, LANE), 1) < SG_GD
    return tri, first


def _spatial(gl, u, w, bt, *, name):
    S = gl.shape[0]
    tm = _row_tile(S, 512)
    nch = tm // SG_CHUNK

    def body(gl_ref, u_ref, w_ref, bt_ref, o_ref):
        tri, first = _sg_masks()
        for p in range(SG_W // LANE):
            cols = slice(p * LANE, (p + 1) * LANE)
            wa = jnp.where(tri, w_ref[2 * p], 0.0).astype(BF16)
            wb = jnp.where(tri, w_ref[2 * p + 1], 0.0).astype(BF16)
            for ci in range(nch):
                rws = slice(ci * SG_CHUNK, (ci + 1) * SG_CHUNK)
                g = gl_ref[rws, cols]
                zero = jnp.zeros_like(g)
                mixed = (jnp.dot(wa, jnp.where(first, g, zero), preferred_element_type=F32)
                         + jnp.dot(wb, jnp.where(first, zero, g), preferred_element_type=F32) + bt_ref[:, cols])
                o_ref[rws, cols] = u_ref[rws, cols] * mixed

    row = pl.BlockSpec((tm, SG_W), lambda i: (i, 0))
    return pl.pallas_call(
        body, name=name, grid=(S // tm,),
        in_specs=[row, row, pl.BlockSpec(w.shape, lambda i: (0, 0, 0)), pl.BlockSpec(bt.shape, lambda i: (0, 0))],
        out_specs=row, out_shape=jax.ShapeDtypeStruct((S, SG_W), F32), compiler_params=_params(1))(gl, u, w, bt)


def _spatial_bwd(d_o, gl, u, w, bt, *, name):
    S = gl.shape[0]
    tm = _row_tile(S, 512)
    nch = tm // SG_CHUNK
    nsteps = S // tm

    def body(do_ref, gl_ref, u_ref, w_ref, bt_ref, du_ref, dgl_ref, dw_ref, db_ref, dbt_ref):
        tri, first = _sg_masks()
        step = pl.program_id(0)

        @pl.when(step == 0)
        def _():
            dw_ref[...] = jnp.zeros_like(dw_ref)
            dbt_ref[...] = jnp.zeros_like(dbt_ref)

        for p in range(SG_W // LANE):
            cols = slice(p * LANE, (p + 1) * LANE)
            wa = jnp.where(tri, w_ref[2 * p], 0.0).astype(BF16)
            wb = jnp.where(tri, w_ref[2 * p + 1], 0.0).astype(BF16)
            for ci in range(nch):
                rws = slice(ci * SG_CHUNK, (ci + 1) * SG_CHUNK)
                g = gl_ref[rws, cols]
                zero = jnp.zeros_like(g)
                mixed = (jnp.dot(wa, jnp.where(first, g, zero), preferred_element_type=F32)
                         + jnp.dot(wb, jnp.where(first, zero, g), preferred_element_type=F32) + bt_ref[:, cols])
                dov = do_ref[rws, cols]
                du_ref[rws, cols] = dov * mixed
                dm = dov * u_ref[rws, cols]
                dbt_ref[:, cols] += dm
                dma = jnp.where(first, dm, 0.0).astype(BF16)
                dmb = jnp.where(first, 0.0, dm).astype(BF16)
                dw_ref[2 * p] += jnp.where(tri, _nt(dma, g), 0.0)
                dw_ref[2 * p + 1] += jnp.where(tri, _nt(dmb, g), 0.0)
                dgl_ref[rws, cols] = _tn(wa, dma) + _tn(wb, dmb)

        @pl.when(step == nsteps - 1)
        def _():
            lane = lax.broadcasted_iota(jnp.int32, (SG_CHUNK, LANE), 1)
            acc = jnp.zeros((SG_CHUNK, LANE), F32)
            for p in range(SG_W // LANE):
                blk = dbt_ref[:, p * LANE:(p + 1) * LANE]
                sa = jnp.sum(jnp.where(first, blk, 0.0), axis=1, keepdims=True)
                sb_ = jnp.sum(jnp.where(first, 0.0, blk), axis=1, keepdims=True)
                acc = acc + jnp.where(lane == 2 * p, sa, 0.0) + jnp.where(lane == 2 * p + 1, sb_, 0.0)
            db_ref[...] = acc

    row = pl.BlockSpec((tm, SG_W), lambda i: (i, 0))
    return pl.pallas_call(
        body, name=name, grid=(nsteps,),
        in_specs=[row, row, row, pl.BlockSpec(w.shape, lambda i: (0, 0, 0)), pl.BlockSpec(bt.shape, lambda i: (0, 0))],
        out_specs=[row, row, pl.BlockSpec(w.shape, lambda i: (0, 0, 0)), pl.BlockSpec((SG_CHUNK, LANE), lambda i: (0, 0))],
        out_shape=[jax.ShapeDtypeStruct((S, SG_W), F32), jax.ShapeDtypeStruct((S, SG_W), F32),
                   jax.ShapeDtypeStruct(w.shape, F32), jax.ShapeDtypeStruct((SG_CHUNK, LANE), F32)],
        scratch_shapes=[pltpu.VMEM((SG_CHUNK, SG_W), F32)], compiler_params=_params(1))(d_o, gl, u, w, bt)


ROPE_HALF = MLA_ROPE // 2
KR_COL = (MLA_QL + MLA_KVL) // LANE
MLA_IN_PAD = MLA_QL + MLA_KVL + LANE


def _rope_tables(positions):
    inv_freq = ROPE_THETA ** (-jnp.arange(ROPE_HALF, dtype=F32) / ROPE_HALF)
    ang = positions.astype(F32)[:, None] * inv_freq
    cos, sin = jnp.cos(ang), jnp.sin(ang)
    S = positions.shape[0]
    z16, tail = jnp.zeros((S, ROPE_HALF), F32), jnp.zeros((S, LANE - MLA_QK), F32)
    ones = jnp.ones((S, MLA_NOPE), F32)
    zeros = jnp.zeros((S, MLA_NOPE), F32)
    return (jnp.concatenate([ones, cos, cos, tail], axis=1), jnp.concatenate([zeros, z16, sin, tail], axis=1),
            jnp.concatenate([zeros, -sin, z16, tail], axis=1))


def _rope(x, cos, sa, sb):
    return x * cos + pltpu.roll(x, ROPE_HALF, 1) * sa + pltpu.roll(x, LANE - ROPE_HALF, 1) * sb


def _rope_t(dy, cos, sa, sb):
    return dy * cos + pltpu.roll(dy * sa, LANE - ROPE_HALF, 1) + pltpu.roll(dy * sb, ROPE_HALF, 1)


def _mla_lora(P, qlg, kvlg, *, name):
    def fn(pv, a, b):
        return _rmsnorm_fwd(pv[:, :MLA_QL], a), _rmsnorm_fwd(pv[:, MLA_QL:MLA_QL + MLA_KVL], b)

    return _rowwise(fn, [P], [qlg.reshape(1, MLA_QL), kvlg.reshape(1, MLA_KVL)], [(MLA_QL, BF16), (MLA_KVL, BF16)], name=name)


def _mla_lora_bwd(dcq, dckv, dkr, P, qlg, kvlg, *, name):
    def fn(d1, d2, d3, pv, a, b):
        x1, g1 = _rmsnorm_bwd(d1, pv[:, :MLA_QL], a)
        x2, g2 = _rmsnorm_bwd(d2, pv[:, MLA_QL:MLA_QL + MLA_KVL], b)
        return jnp.concatenate([x1, x2, d3], axis=1), g1, g2

    return _rowwise(fn, [dcq, dckv, dkr, P], [qlg.reshape(1, MLA_QL), kvlg.reshape(1, MLA_KVL)], [(MLA_IN_PAD, BF16)],
                    [(1, MLA_QL), (1, MLA_KVL)], name=name)


def _mla_qk(q_pre, k_pre, P, tabs, qg, kg, *, name):
    def fn(qp, kp, kr, c, a, b, qgv, kgv):
        return (_rope(_rmsnorm_fwd(qp, qgv, MLA_QK), c, a, b) * MLA_SCALE,
                _rope(_rmsnorm_fwd(kp + kr, kgv, MLA_QK), c, a, b))

    hcol = lambda h: h
    rows = [(q_pre, LANE, hcol), (k_pre, LANE, hcol), (P, LANE, lambda h: KR_COL), *tabs]
    w = MLA_HEADS * LANE
    return _rowwise(fn, rows, [qg, kg], [(w, BF16), (w, BF16)], heads=MLA_HEADS, tm=HEAD_ROWS, name=name)


def _mla_qk_bwd(dq, dk, q_pre, k_pre, P, tabs, qg, kg, *, name):
    def fn(dqv, dkv, qp, kp, kr, c, a, b, qgv, kgv):
        dqp, dqg = _rmsnorm_bwd(_rope_t(dqv * MLA_SCALE, c, a, b), qp, qgv, MLA_QK)
        dkp, dkg = _rmsnorm_bwd(_rope_t(dkv, c, a, b), kp + kr, kgv, MLA_QK)
        lane = lax.broadcasted_iota(jnp.int32, (1, LANE), 1)
        return dqp, dkp, jnp.where((lane >= MLA_NOPE) & (lane < MLA_QK), dkp, 0.0), dqg, dkg

    hcol = lambda h: h
    rows = [(dq, LANE, hcol), (dk, LANE, hcol), (q_pre, LANE, hcol), (k_pre, LANE, hcol), (P, LANE, lambda h: KR_COL), *tabs]
    w = MLA_HEADS * LANE
    return _rowwise(fn, rows, [qg, kg], [(w, BF16), (w, BF16)], [(1, LANE), (1, LANE)], [LANE], heads=MLA_HEADS,
                    tm=HEAD_ROWS, name=name)


def _head_norm(x, g, *, heads, width, colfn=None, scale=1.0, name):
    return _rowwise(lambda xv, gv: _rmsnorm_fwd(xv, gv) * scale, [(x, width, colfn or (lambda h: h))],
                    [g.reshape(1, width)], [(heads * width, BF16)], heads=heads, tm=HEAD_ROWS, name=name)[0]


def _head_norm_bwd(dy, x, g, *, heads, width, colfn=None, scale=1.0, out_dtype, name):
    return _rowwise(lambda dv_, xv, gv: _rmsnorm_bwd(dv_ * scale, xv, gv),
                    [(dy, width, lambda h: h), (x, width, colfn or (lambda h: h))],
                    [g.reshape(1, width)], [(heads * width, out_dtype)], [(1, width)], heads=heads, tm=HEAD_ROWS,
                    name=name)


def _loss_grad(y, tgt, *, name):
    D = y.shape[1]

    def fn(yv, tv):
        d = yv - tv
        return d * (1.0 / D), jnp.sum(d * d, axis=0, keepdims=True) * (0.5 / D)

    dy, part = _rowwise(fn, [y, tgt], [], [(D, F32)], [(1, D)], name=name)
    return jnp.sum(part), dy


def _adamw(w, g, m, v, *, name):
    shape = w.shape
    two_d = (-1, shape[-1])

    def fn(wv, gv, mv, vv):
        m2 = ADAM_B1 * mv + (1.0 - ADAM_B1) * gv
        v2 = ADAM_B2 * vv + (1.0 - ADAM_B2) * (gv * gv)
        m_hat = m2 / (1.0 - ADAM_B1 ** ADAM_STEP)
        v_hat = v2 / (1.0 - ADAM_B2 ** ADAM_STEP)
        return -ADAM_LR * (m_hat / (jnp.sqrt(v_hat) + ADAM_EPS) + ADAM_WD * wv), m2, v2

    outs = _rowwise(fn, [t.reshape(two_d) for t in (w, g, m, v)], [], [(shape[-1], F32)] * 3, name=name)
    return [o.reshape(shape) for o in outs]


def _pad_cols(w, heads, hd):
    k = w.shape[0]
    return jnp.pad(w.reshape(k, heads, hd), ((0, 0), (0, 0), (0, LANE - hd))).reshape(k, heads * LANE)


def _unpad_cols(w, heads, hd):
    k = w.shape[0]
    return w.reshape(k, heads, LANE)[:, :, :hd].reshape(k, heads * hd)


def _pad_rows(w, heads, hd):
    n = w.shape[1]
    return jnp.pad(w.reshape(heads, hd, n), ((0, 0), (0, LANE - hd), (0, 0))).reshape(heads * LANE, n)


def _unpad_rows(w, heads, hd):
    n = w.shape[1]
    return w.reshape(heads, LANE, n)[:, :hd, :].reshape(heads * hd, n)


def _ffn_fwd(x, g, wgu, wd, tag):
    h = _norm_rows(x, g, name=tag + "_norm")
    gate, up, act = _mm_swiglu(h, wgu, name=tag + "_gu")
    y = _mm(act, wd, scale=0.5, residual=x, name=tag + "_down")
    return y, (x, h, gate, up, act)


def _ffn_bwd(dy, saved, g, wgu, wd, tag):
    x, h, gate, up, act = saved
    F = wd.shape[0]
    dwd = _mm(act, dy, ta=True, scale=0.5, name=tag + "_dwd")
    dgate, dup = _mm_dswiglu(dy, wd, gate, up, scale=0.5, name=tag + "_dact")
    dh = _mm(dgate, wgu, tb=True, name=tag + "_dh_g")
    dh = _mm(dup, wgu, tb=True, b_off=(0, F), residual=dh, name=tag + "_dh_u")
    dwgu = jnp.concatenate([_mm(h, dgate, ta=True, name=tag + "_dwg"), _mm(h, dup, ta=True, name=tag + "_dwu")], axis=1)
    dx, dg = _norm_rows_bwd(dh, x, g, dy, name=tag + "_dnorm")
    return dx, dg, dwgu, dwd


def _even_weights(w_in, w_out):
    parts = [w_in[:, :SB_W] * SB_SCALE, w_in[:, SB_W:2 * SB_W], w_in[:, 2 * SB_W:3 * SB_W]]
    wqkv = jnp.concatenate([_pad_cols(p, SB_HEADS, SB_HD) for p in parts], axis=1)
    return wqkv, w_in[:, 3 * SB_W:], _pad_rows(w_out[:SB_W], SB_HEADS, SB_HD), w_out[SB_W:]


def _even_fwd(x, g, wts, ln_g, ln_b, sgu_w, bt, tag):
    wqkv, wz, wo_sb, wo_sg = wts
    h = _norm_rows(x, g, name=tag + "_norm")
    qkv = _mm(h, wqkv, out_dtype=BF16, name=tag + "_qkv")
    z = _mm(h, wz, name=tag + "_z")
    o_sb, tot = _attn_fwd(qkv, qkv, qkv, sb=True, causal=True, heads=SB_HEADS, dq=LANE, dv=LANE, group=FWD_GROUP,
                          kcol=lambda g: SB_HEADS // FWD_GROUP + g, vcol=lambda g: 2 * SB_HEADS // FWD_GROUP + g,
                          name=tag + "_sb")
    u, gl = _gelu_ln(z, ln_g, ln_b, name=tag + "_geluln")
    o_sg = _spatial(gl, u, sgu_w, bt, name=tag + "_sgu")
    y = _mm(o_sb, wo_sb, residual=x, name=tag + "_out_sb")
    y = _mm(o_sg, wo_sg, residual=y, name=tag + "_out_sg")
    return y, (x, h, qkv, z, o_sb, tot, u, gl, o_sg)


def _even_bwd(dy, saved, g, wts, ln_g, sgu_w, bt, tag):
    wqkv, wz, wo_sb, wo_sg = wts
    x, h, qkv, z, o_sb, tot, u, gl, o_sg = saved
    do_sb = _mm(dy, wo_sb, tb=True, name=tag + "_do_sb")
    do_sg = _mm(dy, wo_sg, tb=True, name=tag + "_do_sg")
    dwo = jnp.concatenate([_unpad_rows(_mm(o_sb, dy, ta=True, name=tag + "_dwo_sb"), SB_HEADS, SB_HD),
                           _mm(o_sg, dy, ta=True, name=tag + "_dwo_sg")], axis=0)
    dq, dk, dv = _attn_bwd(qkv, qkv, qkv, o_sb, do_sb, tot, sb=True, causal=True, heads=SB_HEADS, dq=LANE, dv=LANE,
                           kcol=lambda hh: SB_HEADS + hh, vcol=lambda hh: 2 * SB_HEADS + hh, name=tag + "_sb_bwd")
    du, dgl, dsgu_w, db_t = _spatial_bwd(do_sg, gl, u, sgu_w, bt, name=tag + "_sgu_bwd")
    dz, dln_g, dln_b = _gelu_ln_bwd(z, du, dgl, ln_g, name=tag + "_geluln_bwd")
    dh = _mm(dz, wz, tb=True, name=tag + "_dh_z")
    dws = []
    for i, (d, nm) in enumerate(((dq, "q"), (dk, "k"), (dv, "v"))):
        dh = _mm(d, wqkv, tb=True, b_off=(0, i * SB_HEADS * LANE), residual=dh, name=tag + "_dh_" + nm)
        dws.append(_unpad_cols(_mm(h, d, ta=True, scale=SB_SCALE if nm == "q" else 1.0, name=tag + "_dw_" + nm),
                               SB_HEADS, SB_HD))
    dws.append(_mm(h, dz, ta=True, name=tag + "_dw_z"))
    dx, dg = _norm_rows_bwd(dh, x, g, dy, name=tag + "_dnorm")
    return dx, dict(mix_norm=dg, sbg_w_in=jnp.concatenate(dws, axis=1), sgu_ln_gain=dln_g, sgu_ln_bias=dln_b,
                    sgu_w=dsgu_w, sgu_b=db_t[:, :SG_GROUPS].T, sbg_w_out=dwo)


def _mla_weights(w_in, w_uq, w_ukv, w_out, q_gain, k_gain):
    d = w_in.shape[0]
    lat = MLA_QL + MLA_KVL
    w_in_ext = jnp.concatenate([w_in[:, :lat], jnp.zeros((d, MLA_NOPE), w_in.dtype), w_in[:, lat:],
                                jnp.zeros((d, LANE - MLA_QK), w_in.dtype)], axis=1)
    kv = w_ukv.reshape(MLA_KVL, MLA_HEADS, MLA_NOPE + MLA_V)
    wk = _pad_cols(kv[:, :, :MLA_NOPE].reshape(MLA_KVL, -1), MLA_HEADS, MLA_NOPE)
    wv = _pad_cols(kv[:, :, MLA_NOPE:].reshape(MLA_KVL, -1), MLA_HEADS, MLA_V)
    pad_gain = lambda gn: jnp.pad(gn.reshape(1, MLA_QK), ((0, 0), (0, LANE - MLA_QK)))
    return (w_in_ext, _pad_cols(w_uq, MLA_HEADS, MLA_QK), wk, wv, _pad_rows(w_out, MLA_HEADS, MLA_V),
            pad_gain(q_gain), pad_gain(k_gain))


def _mla_fwd(x, g, wts, qlg, kvlg, tabs, tag):
    w_in, w_uq, wk, wv, w_out, qg, kg = wts
    h = _norm_rows(x, g, name=tag + "_norm")
    P = _mm(h, w_in, name=tag + "_in")
    cqn, ckvn = _mla_lora(P, qlg, kvlg, name=tag + "_lora")
    q_pre = _mm(cqn, w_uq, name=tag + "_uq")
    k_pre = _mm(ckvn, wk, name=tag + "_uk")
    ones_lane = jnp.tile((jnp.arange(LANE) == MLA_V).astype(F32), MLA_HEADS)[None, :]
    v = _mm(ckvn, wv, out_dtype=BF16, bias=ones_lane, name=tag + "_uv")
    q, k = _mla_qk(q_pre, k_pre, P, tabs, qg, kg, name=tag + "_qk")
    o, lse = _attn_fwd(q, k, v, sb=False, causal=True, heads=MLA_HEADS, dq=LANE, dv=LANE, group=FWD_GROUP,
                       sum_lane=MLA_V, name=tag + "_attn")
    y = _mm(o, w_out, residual=x, name=tag + "_out")
    return y, (x, h, P, cqn, ckvn, q_pre, k_pre, q, k, v, o, lse)


def _mla_bwd(dy, saved, g, wts, qlg, kvlg, tabs, tag):
    w_in, w_uq, wk, wv, w_out, qg, kg = wts
    x, h, P, cqn, ckvn, q_pre, k_pre, q, k, v, o, lse = saved
    do = _mm(dy, w_out, tb=True, name=tag + "_do")
    dw_out = _unpad_rows(_mm(o, dy, ta=True, name=tag + "_dwo"), MLA_HEADS, MLA_V)
    dq, dk, dv = _attn_bwd(q, k, v, o, do, lse, sb=False, causal=True, heads=MLA_HEADS, dq=LANE, dv=LANE,
                           name=tag + "_attn_bwd")
    dq_pre, dk_pre, dkr, dqg, dkg = _mla_qk_bwd(dq, dk, q_pre, k_pre, P, tabs, qg, kg, name=tag + "_qk_bwd")
    dcqn = _mm(dq_pre, w_uq, tb=True, name=tag + "_dcq")
    dckvn = _mm(dk_pre, wk, tb=True, name=tag + "_dckv_k")
    dckvn = _mm(dv, wv, tb=True, residual=dckvn, name=tag + "_dckv_v")
    dw_uq = _unpad_cols(_mm(cqn, dq_pre, ta=True, name=tag + "_dwuq"), MLA_HEADS, MLA_QK)
    dwk = _unpad_cols(_mm(ckvn, dk_pre, ta=True, name=tag + "_dwk"), MLA_HEADS, MLA_NOPE)
    dwv = _unpad_cols(_mm(ckvn, dv, ta=True, name=tag + "_dwv"), MLA_HEADS, MLA_V)
    dw_ukv = jnp.concatenate([dwk.reshape(MLA_KVL, MLA_HEADS, MLA_NOPE), dwv.reshape(MLA_KVL, MLA_HEADS, MLA_V)],
                             axis=2).reshape(MLA_KVL, -1)
    dP, dqlg, dkvlg = _mla_lora_bwd(dcqn, dckvn, dkr, P, qlg, kvlg, name=tag + "_lora_bwd")
    dh = _mm(dP, w_in, tb=True, name=tag + "_dh")
    dw_in_ext = _mm(h, dP, ta=True, name=tag + "_dwin")
    lat = MLA_QL + MLA_KVL
    dw_in = jnp.concatenate([dw_in_ext[:, :lat], dw_in_ext[:, lat + MLA_NOPE:lat + MLA_QK]], axis=1)
    dx, dg = _norm_rows_bwd(dh, x, g, dy, name=tag + "_dnorm")
    return dx, dict(mix_norm=dg, mla_w_in=dw_in, mla_q_lora_gain=dqlg, mla_kv_lora_gain=dkvlg, mla_w_uq=dw_uq,
                    mla_w_ukv=dw_ukv, mla_q_gain=dqg[:, :MLA_QK], mla_k_gain=dkg[:, :MLA_QK], mla_w_out=dw_out)


def _xmem_fwd(x, mem, g, gm, wq, wkv, qg, kg, wo, tag):
    hq = _norm_rows(x, g, name=tag + "_norm")
    hm = _norm_rows(mem, gm, name=tag + "_mnorm")
    qp = _mm(hq, wq, name=tag + "_q")
    kv = _mm(hm, wkv, name=tag + "_kv")
    q = _head_norm(qp, qg, heads=MEM_HEADS, width=MEM_HD, scale=MEM_SCALE, name=tag + "_qn")
    kn = _head_norm(kv, kg, heads=MEM_HEADS, width=MEM_HD, colfn=lambda hh: 2 * hh, name=tag + "_kn")
    kvb = kv.reshape(-1, MEM_HEADS, 2, MEM_HD)[:, :, 1].reshape(-1, MEM_HEADS * MEM_HD).astype(BF16)
    o, lse = _attn_fwd(q, kn, kvb, sb=False, causal=False, heads=MEM_HEADS, dq=MEM_HD, dv=MEM_HD, group=MEM_HEADS,
                       name=tag + "_attn")
    y = _mm(o, wo, residual=x, name=tag + "_out")
    return y, (x, hq, hm, qp, kv, q, kn, kvb, o, lse)


def _xmem_bwd(dy, saved, mem, g, gm, wq, wkv, qg, kg, wo, tag):
    x, hq, hm, qp, kv, q, kn, kvb, o, lse = saved
    m = mem.shape[0]
    do = _mm(dy, wo, tb=True, name=tag + "_do")
    dwo = _mm(o, dy, ta=True, name=tag + "_dwo")
    dq, dk, dv = _attn_bwd(q, kn, kvb, o, do, lse, sb=False, causal=False, heads=MEM_HEADS, dq=MEM_HD, dv=MEM_HD,
                           name=tag + "_attn_bwd")
    dqp, dqg = _head_norm_bwd(dq, qp, qg, heads=MEM_HEADS, width=MEM_HD, scale=MEM_SCALE, out_dtype=BF16,
                              name=tag + "_qn_bwd")
    dkp, dkg = _head_norm_bwd(dk, kv, kg, heads=MEM_HEADS, width=MEM_HD, colfn=lambda hh: 2 * hh, out_dtype=F32,
                              name=tag + "_kn_bwd")
    dkv = jnp.concatenate([dkp.reshape(m, MEM_HEADS, MEM_HD), dv.reshape(m, MEM_HEADS, MEM_HD)], axis=2).reshape(m, -1)
    dwkv = _mm(hm, dkv, ta=True, name=tag + "_dwkv")
    dhm = _mm(dkv, wkv, tb=True, name=tag + "_dhm")
    _, dgm = _norm_rows_bwd(dhm, mem, gm, None, name=tag + "_dmnorm")
    dwq = _mm(hq, dqp, ta=True, name=tag + "_dwq")
    dhq = _mm(dqp, wq, tb=True, name=tag + "_dhq")
    dx, dg = _norm_rows_bwd(dhq, x, g, dy, name=tag + "_dnorm")
    return dx, dict(xmem_norm=dg, xmem_mem_norm=dgm, xmem_wq=dwq, xmem_wkv=dwkv, xmem_q_gain=dqg, xmem_k_gain=dkg,
                    xmem_wo=dwo)


def _local_step(x, mem, positions, tgt, w):
    tabs = _rope_tables(positions)
    even = _even_weights(w["sbg_w_in"][0], w["sbg_w_out"][0])
    mla = _mla_weights(w["mla_w_in"][0], w["mla_w_uq"][0], w["mla_w_ukv"][0], w["mla_w_out"][0], w["mla_q_gain"][0],
                       w["mla_k_gain"][0])
    bt = jnp.repeat(w["sgu_b"][0].T, SG_GD, axis=1)
    saved = []
    for l in range(2):
        t = f"l{l}"
        x, s_pre = _ffn_fwd(x, w["ffn_pre_norm"][l], w["ffn_pre_w_gu"][l], w["ffn_pre_w_down"][l], t + "_pre")
        if l == 0:
            x, s_mix = _even_fwd(x, w["mix_norm"][0], even, w["sgu_ln_gain"][0], w["sgu_ln_bias"][0], w["sgu_w"][0], bt,
                                 t + "_even")
        else:
            x, s_mix = _mla_fwd(x, w["mix_norm"][1], mla, w["mla_q_lora_gain"][0], w["mla_kv_lora_gain"][0], tabs,
                                t + "_mla")
        x, s_xm = _xmem_fwd(x, mem, w["xmem_norm"][l], w["xmem_mem_norm"][l], w["xmem_wq"][l], w["xmem_wkv"][l],
                            w["xmem_q_gain"][l], w["xmem_k_gain"][l], w["xmem_wo"][l], t + "_xm")
        x, s_post = _ffn_fwd(x, w["ffn_post_norm"][l], w["ffn_post_w_gu"][l], w["ffn_post_w_down"][l], t + "_post")
        saved.append((s_pre, s_mix, s_xm, s_post))
    loss, dx = _loss_grad(x, tgt, name="loss")
    grads = {}

    def put(name, l, val):
        grads.setdefault(name, {})[l] = val

    for l in (1, 0):
        t = f"l{l}"
        s_pre, s_mix, s_xm, s_post = saved[l]
        dx, dg, dwgu, dwd = _ffn_bwd(dx, s_post, w["ffn_post_norm"][l], w["ffn_post_w_gu"][l], w["ffn_post_w_down"][l],
                                     t + "_post")
        put("ffn_post_norm", l, dg), put("ffn_post_w_gu", l, dwgu), put("ffn_post_w_down", l, dwd)
        dx, gx = _xmem_bwd(dx, s_xm, mem, w["xmem_norm"][l], w["xmem_mem_norm"][l], w["xmem_wq"][l], w["xmem_wkv"][l],
                           w["xmem_q_gain"][l], w["xmem_k_gain"][l], w["xmem_wo"][l], t + "_xm")
        for k_, v_ in gx.items():
            put(k_, l, v_)
        if l == 0:
            dx, gm = _even_bwd(dx, s_mix, w["mix_norm"][0], even, w["sgu_ln_gain"][0], w["sgu_w"][0], bt, t + "_even")
        else:
            dx, gm = _mla_bwd(dx, s_mix, w["mix_norm"][1], mla, w["mla_q_lora_gain"][0], w["mla_kv_lora_gain"][0], tabs,
                              t + "_mla")
        for k_, v_ in gm.items():
            put(k_, l if k_ == "mix_norm" else 0, v_)
        dx, dg, dwgu, dwd = _ffn_bwd(dx, s_pre, w["ffn_pre_norm"][l], w["ffn_pre_w_gu"][l], w["ffn_pre_w_down"][l],
                                     t + "_pre")
        put("ffn_pre_norm", l, dg), put("ffn_pre_w_gu", l, dwgu), put("ffn_pre_w_down", l, dwd)
    return loss, dx, {k_: [v_[l] for l in sorted(v_)] for k_, v_ in grads.items()}


N_CHIPS = 4
PACK_COLS = 1024
PACK_ROW_MULTIPLE = 512


def _place():
    x, y, c = lax.axis_index("x"), lax.axis_index("y"), lax.axis_index("c")
    return x, y, c, [(1 - x, y), (x, 1 - y), (1 - x, 1 - y)]


def _hops(x, y, c):
    return ((x + 1 - c) % 2, (y + c) % 2), ((x + c) % 2, (y + 1 - c) % 2), (1 - x, 1 - y)


def _gather_chips(shard):
    R, C = shard.shape
    Rh = R // 2

    def body(x_ref, out_ref, send_sems, recv_sems):
        x, y, c = lax.axis_index("x"), lax.axis_index("y"), lax.axis_index("c")
        n1, n2, nd = _hops(x, y, c)
        me, q1, q2, qd = 2 * x + y, 2 * n1[0] + n1[1], 2 * n2[0] + n2[1], 2 * nd[0] + nd[1]

        def half(chip, core):
            return out_ref.at[chip, pl.ds(core * Rh, Rh), :]

        def copy(k, chip, core, to, src=None):
            return pltpu.make_async_remote_copy(src_ref=half(chip, core) if src is None else src, dst_ref=half(chip, core),
                                                send_sem=send_sems.at[k], recv_sem=recv_sems.at[k], device_id=to,
                                                device_id_type=MESH)

        own = x_ref.at[pl.ds(c * Rh, Rh), :]
        sibling = (x, y, 1 - c)
        sends = [copy(0, me, c, (*n1, c), src=own), copy(1, me, c, (*n2, c), src=own)]
        sends[0].start()
        sends[1].start()
        copy(0, q1, c, sibling).wait_recv()
        sends += [copy(2, q1, c, (*n2, c)), copy(3, q1, c, sibling)]
        sends[2].start()
        sends[3].start()
        copy(1, q2, c, sibling).wait_recv()
        sends.append(copy(4, q2, c, sibling))
        sends[4].start()
        copy(2, qd, c, sibling).wait_recv()
        sends.append(copy(5, qd, c, sibling))
        sends[5].start()
        copy(3, q2, 1 - c, sibling).wait_recv()
        copy(4, q1, 1 - c, sibling).wait_recv()
        copy(5, qd, 1 - c, sibling).wait_recv()
        for cp in sends:
            cp.wait_send()

    others = pl.pallas_call(
        body, name="gather_weights", out_shape=jax.ShapeDtypeStruct((N_CHIPS, R, C), shard.dtype),
        in_specs=[ANY], out_specs=ANY,
        scratch_shapes=[pltpu.SemaphoreType.DMA((6,)), pltpu.SemaphoreType.DMA((6,))])(shard)
    me = 2 * lax.axis_index("x") + lax.axis_index("y")
    return lax.dynamic_update_slice(others, shard[None], (me, 0, 0))


def _gather_devices(block):
    M, N = block.shape

    def body(x_ref, out_ref, send_sems, recv_sems, local_sem):
        x, y, c, chips = _place()
        me, sibling = (x, y, c), (x, y, 1 - c)

        def rows(px, py, pc):
            return out_ref.at[pl.ds((4 * px + 2 * py + pc) * M, M), :]

        def copy(k, blk, to, src=None):
            return pltpu.make_async_remote_copy(src_ref=rows(*blk) if src is None else src, dst_ref=rows(*blk),
                                                send_sem=send_sems.at[k], recv_sem=recv_sems.at[k], device_id=to,
                                                device_id_type=MESH)

        mine = pltpu.make_async_copy(x_ref, rows(*me), local_sem)
        mine.start()
        first = [copy(0, me, sibling, src=x_ref)]
        first += [copy(1 + j, me, (*chip, c), src=x_ref) for j, chip in enumerate(chips)]
        for cp in first:
            cp.start()
        passed = [copy(4 + j, (*chip, c), sibling) for j, chip in enumerate(chips)]
        for j, chip in enumerate(chips):
            copy(1 + j, (*chip, c), me).wait_recv()
            passed[j].start()
        copy(0, sibling, me).wait_recv()
        for j, chip in enumerate(chips):
            copy(4 + j, (*chip, 1 - c), me).wait_recv()
        for cp in first + passed:
            cp.wait_send()
        mine.wait()

    vmem = pl.BlockSpec(memory_space=pltpu.VMEM)
    return pl.pallas_call(
        body, name=f"gather_devices_{M}", out_shape=jax.ShapeDtypeStruct((8 * M, N), block.dtype),
        in_specs=[vmem], out_specs=vmem,
        scratch_shapes=[pltpu.SemaphoreType.DMA((7,)), pltpu.SemaphoreType.DMA((7,)), pltpu.SemaphoreType.DMA],
        compiler_params=pltpu.CompilerParams(vmem_limit_bytes=VMEM_LIMIT))(block)


def _swap_halves(g):
    n, R, C = g.shape
    Rh = R // 2

    def body(g_ref, a_ref, send_sem, recv_sem):
        x, y, c, _ = _place()
        cp = pltpu.make_async_remote_copy(src_ref=g_ref.at[:, pl.ds((1 - c) * Rh, Rh), :], dst_ref=a_ref,
                                          send_sem=send_sem, recv_sem=recv_sem, device_id=(x, y, 1 - c),
                                          device_id_type=MESH)
        cp.start()
        cp.wait()

    return pl.pallas_call(body, name="grad_swap_halves", out_shape=jax.ShapeDtypeStruct((n, Rh, C), g.dtype),
                          in_specs=[ANY], out_specs=ANY,
                          scratch_shapes=[pltpu.SemaphoreType.DMA, pltpu.SemaphoreType.DMA])(g)


def _add_picked(a, b, picks, *, a_row_half=None, out_dtype, name):
    n_out = picks.shape[0]
    _, rows, C = b.shape
    tr = _row_tile(rows, 512)
    nt = rows // tr
    half = jnp.zeros((1,), jnp.int32) if a_row_half is None else a_row_half

    def body(pick_ref, half_ref, a_ref, b_ref, o_ref):
        o_ref[...] = (a_ref[...].astype(F32) + b_ref[...].astype(F32)).astype(o_ref.dtype)

    spec = pltpu.PrefetchScalarGridSpec(
        num_scalar_prefetch=2, grid=(n_out, nt),
        in_specs=[pl.BlockSpec((1, tr, C), lambda j, i, pick, hf: (pick[j], hf[0] * nt + i, 0)),
                  pl.BlockSpec((1, tr, C), lambda j, i, pick, hf: (pick[j], i, 0))],
        out_specs=pl.BlockSpec((1, tr, C), lambda j, i, pick, hf: (j, i, 0)))
    return pl.pallas_call(body, name=name, grid_spec=spec, out_shape=jax.ShapeDtypeStruct((n_out, rows, C), out_dtype),
                          compiler_params=_params(2))(picks.astype(jnp.int32), half.astype(jnp.int32), a, b)


def _hop_exchange(src, hop, *, name):
    def body(s_ref, d_ref, send_sem, recv_sem):
        x, y, c = lax.axis_index("x"), lax.axis_index("y"), lax.axis_index("c")
        cp = pltpu.make_async_remote_copy(src_ref=s_ref, dst_ref=d_ref, send_sem=send_sem, recv_sem=recv_sem,
                                          device_id=(*_hops(x, y, c)[hop], c), device_id_type=MESH)
        cp.start()
        cp.wait()

    return pl.pallas_call(body, name=name, out_shape=jax.ShapeDtypeStruct(src.shape, src.dtype), in_specs=[ANY],
                          out_specs=ANY, scratch_shapes=[pltpu.SemaphoreType.DMA, pltpu.SemaphoreType.DMA])(src)


def _reduce_over_chips(g):
    x, y, c = lax.axis_index("x"), lax.axis_index("y"), lax.axis_index("c")
    n1, n2, _ = _hops(x, y, c)
    chip = lambda p: 2 * p[0] + p[1]
    near = jnp.stack([chip((x, y)), chip(n2)])
    far = jnp.stack([chip(n1), chip((1 - x, 1 - y))])
    half = c.reshape(1)
    sib = _swap_halves(g)
    kept = _add_picked(g, sib, near, a_row_half=half, out_dtype=F32, name="grad_add_near")
    sent = _add_picked(g, sib, far, a_row_half=half, out_dtype=BF16, name="grad_add_far")
    got = _hop_exchange(sent, 0, name="grad_hop_first")
    mine = _add_picked(kept, got, jnp.zeros((1,), jnp.int32), out_dtype=F32, name="grad_add_mine")
    theirs = _add_picked(kept, got, jnp.ones((1,), jnp.int32), out_dtype=BF16, name="grad_add_theirs")
    got = _hop_exchange(theirs, 1, name="grad_hop_second")
    total = _add_picked(mine, got, jnp.zeros((1,), jnp.int32), out_dtype=F32, name="grad_add_total")
    return _join_halves(total[0])


def _sum_slots(b, *, name):
    n, R, C = b.shape
    tr = _row_tile(R, 512)

    def body(b_ref, o_ref):
        acc = b_ref[0]
        for q in range(1, n):
            acc = acc + b_ref[q]
        o_ref[...] = acc

    return pl.pallas_call(body, name=name, grid=(R // tr,), in_specs=[pl.BlockSpec((n, tr, C), lambda i: (0, i, 0))],
                          out_specs=pl.BlockSpec((tr, C), lambda i: (i, 0)), out_shape=jax.ShapeDtypeStruct((R, C), F32),
                          compiler_params=_params(1))(b)


def _join_halves(r):
    Rh, C = r.shape

    def body(r_ref, o_ref, send_sem, recv_sem):
        x, y, c, _ = _place()
        own, other = o_ref.at[pl.ds(c * Rh, Rh), :], o_ref.at[pl.ds((1 - c) * Rh, Rh), :]
        cp = pltpu.make_async_remote_copy(src_ref=r_ref, dst_ref=own, send_sem=send_sem, recv_sem=recv_sem,
                                          device_id=(x, y, 1 - c), device_id_type=MESH)
        cp.start()
        pltpu.make_async_remote_copy(src_ref=r_ref, dst_ref=other, send_sem=send_sem, recv_sem=recv_sem,
                                     device_id=(x, y, 1 - c), device_id_type=MESH).wait_recv()
        cp.wait_send()

    theirs = pl.pallas_call(
        body, name="grad_join_halves", out_shape=jax.ShapeDtypeStruct((2 * Rh, C), r.dtype), in_specs=[ANY], out_specs=ANY,
        scratch_shapes=[pltpu.SemaphoreType.DMA, pltpu.SemaphoreType.DMA])(r)
    return lax.dynamic_update_slice(theirs, r, (lax.axis_index("c") * Rh, 0))


def _size(shape):
    size = 1
    for d in shape:
        size *= d
    return size


def _pack(pieces, cols, row_multiple, dtype):
    if any(p.size % cols for p in pieces):
        flat = jnp.concatenate([p.reshape(-1).astype(dtype) for p in pieces])
        pieces = [jnp.pad(flat, (0, -flat.shape[0] % cols))]
    rows = [p.reshape(-1, cols).astype(dtype) for p in pieces]
    pad = -sum(r.shape[0] for r in rows) % row_multiple
    return jnp.concatenate(rows + ([jnp.zeros((pad, cols), dtype)] if pad else []), axis=0)


def _unpack(buf, shapes):
    cols = buf.shape[1]
    if any(_size(s) % cols for s in shapes):
        flat, out, at = buf.reshape(-1), [], 0
        for shp in shapes:
            out.append(flat[at:at + _size(shp)].reshape(shp))
            at += _size(shp)
        return out
    out, at = [], 0
    for shp in shapes:
        out.append(buf[at:at + _size(shp) // cols].reshape(shp))
        at += _size(shp) // cols
    return out


SHARDED = (("ffn_pre_w_gu", 2), ("ffn_pre_w_down", 1), ("sbg_w_in", 2), ("sbg_w_out", 1), ("mla_w_in", 1),
           ("mla_w_uq", 2), ("mla_w_ukv", 2), ("mla_w_out", 1), ("xmem_wq", 1), ("xmem_wkv", 2), ("xmem_wo", 1),
           ("ffn_post_w_gu", 2), ("ffn_post_w_down", 1))
LORA_GAINS = ("mla_q_lora_gain", "mla_kv_lora_gain")
REPLICATED = ("ffn_pre_norm", "mix_norm", "sgu_ln_gain", "sgu_ln_bias", "sgu_w", "sgu_b", "mla_q_gain", "mla_k_gain",
              "xmem_norm", "xmem_mem_norm", "xmem_q_gain", "xmem_k_gain", "ffn_post_norm")
WEIGHTS = ("ffn_pre_norm", "ffn_pre_w_gu", "ffn_pre_w_down", "mix_norm", "sbg_w_in", "sgu_ln_gain", "sgu_ln_bias", "sgu_w",
           "sgu_b", "sbg_w_out", "mla_w_in", "mla_q_lora_gain", "mla_kv_lora_gain", "mla_w_uq", "mla_w_ukv", "mla_q_gain",
           "mla_k_gain", "mla_w_out", "xmem_norm", "xmem_mem_norm", "xmem_wq", "xmem_wkv", "xmem_q_gain", "xmem_k_gain",
           "xmem_wo", "ffn_post_norm", "ffn_post_w_gu", "ffn_post_w_down")
INPUTS = ("x", "mem", "positions") + WEIGHTS + ("loss_target",) + tuple("m_" + n for n in WEIGHTS) + tuple(
    "v_" + n for n in WEIGHTS)


def _step(a):
    x, y, c, _ = _place()
    chip = 2 * x + y
    shard_shapes = [a[n].shape for n, _ in SHARDED]

    gathered = _gather_chips(_pack([a[n] for n, _ in SHARDED], PACK_COLS, PACK_ROW_MULTIPLE, BF16))
    w, at = {}, 0
    for (n, ax), shp in zip(SHARDED, shard_shapes):
        rows = _size(shp) // PACK_COLS
        per_chip = gathered[:, at:at + rows].reshape((N_CHIPS,) + shp)
        at += rows
        w[n] = jnp.moveaxis(per_chip, 0, ax).reshape(shp[:ax] + (N_CHIPS * shp[ax],) + shp[ax + 1:])
    gains = jnp.zeros((8, LANE), F32)
    for r, n in enumerate(LORA_GAINS):
        gains = gains.at[r, :a[n].shape[1]].set(a[n][0])
    gains = _gather_devices(gains)
    for r, n in enumerate(LORA_GAINS):
        w[n] = jnp.concatenate([gains[16 * q + r, :a[n].shape[1]] for q in range(N_CHIPS)])[None, :]
    for n in REPLICATED:
        w[n] = a[n]

    loss, dx, grads = _local_step(a["x"][0], a["mem"][0], a["positions"][0], a["loss_target"][0], w)
    loss = lax.psum(loss, ("x", "y", "c"))
    small_names = REPLICATED + LORA_GAINS
    full = {n: jnp.stack(grads[n]).reshape(w[n].shape) for n in small_names}

    def cut(n, ax, q):
        size = w[n].shape[ax] // N_CHIPS
        return [lax.slice_in_dim(gl, q * size, (q + 1) * size, axis=ax - 1) for gl in grads[n]]

    g = jnp.stack([_pack([p for n, ax in SHARDED for p in cut(n, ax, q)], PACK_COLS, PACK_ROW_MULTIPLE, F32)
                   for q in range(N_CHIPS)])
    reduced = _reduce_over_chips(g)
    gw = dict(zip([n for n, _ in SHARDED], _unpack(reduced, shard_shapes)))

    small = _pack([full[n] for n in small_names], LANE, 256, F32)
    rows = small.shape[0]
    summed = _sum_slots(_gather_devices(small).reshape(8, rows, LANE), name="grad_sum_devices")
    for n, val in zip(small_names, _unpack(summed, [full[n].shape for n in small_names])):
        if n in LORA_GAINS:
            size = a[n].shape[1]
            val = lax.dynamic_slice_in_dim(val, chip * size, size, axis=1)
        gw[n] = val

    upd = {n: _adamw(a[n], gw[n], a["m_" + n], a["v_" + n], name="adamw_" + n) for n in WEIGHTS}
    return (loss, dx[None], *[gw[n] for n in WEIGHTS], *[upd[n][0] for n in WEIGHTS], *[upd[n][1] for n in WEIGHTS],
            *[upd[n][2] for n in WEIGHTS])


def kernel(x, mem, positions, ffn_pre_norm, ffn_pre_w_gu, ffn_pre_w_down, mix_norm, sbg_w_in, sgu_ln_gain,
           sgu_ln_bias, sgu_w, sgu_b, sbg_w_out, mla_w_in, mla_q_lora_gain, mla_kv_lora_gain, mla_w_uq, mla_w_ukv,
           mla_q_gain, mla_k_gain, mla_w_out, xmem_norm, xmem_mem_norm, xmem_wq, xmem_wkv, xmem_q_gain, xmem_k_gain,
           xmem_wo, ffn_post_norm, ffn_post_w_gu, ffn_post_w_down, loss_target, m_ffn_pre_norm, m_ffn_pre_w_gu,
           m_ffn_pre_w_down, m_mix_norm, m_sbg_w_in, m_sgu_ln_gain, m_sgu_ln_bias, m_sgu_w, m_sgu_b, m_sbg_w_out,
           m_mla_w_in, m_mla_q_lora_gain, m_mla_kv_lora_gain, m_mla_w_uq, m_mla_w_ukv, m_mla_q_gain, m_mla_k_gain,
           m_mla_w_out, m_xmem_norm, m_xmem_mem_norm, m_xmem_wq, m_xmem_wkv, m_xmem_q_gain, m_xmem_k_gain,
           m_xmem_wo, m_ffn_post_norm, m_ffn_post_w_gu, m_ffn_post_w_down, v_ffn_pre_norm, v_ffn_pre_w_gu,
           v_ffn_pre_w_down, v_mix_norm, v_sbg_w_in, v_sgu_ln_gain, v_sgu_ln_bias, v_sgu_w, v_sgu_b, v_sbg_w_out,
           v_mla_w_in, v_mla_q_lora_gain, v_mla_kv_lora_gain, v_mla_w_uq, v_mla_w_ukv, v_mla_q_gain, v_mla_k_gain,
           v_mla_w_out, v_xmem_norm, v_xmem_mem_norm, v_xmem_wq, v_xmem_wkv, v_xmem_q_gain, v_xmem_k_gain,
           v_xmem_wo, v_ffn_post_norm, v_ffn_post_w_gu, v_ffn_post_w_down):
    given = locals()
    return _step({n: given[n] for n in INPUTS})
```

```python
import functools

import jax
import jax.numpy as jnp
from jax import lax
from jax.experimental import pallas as pl
from jax.experimental.pallas import tpu as pltpu

F32, BF16 = jnp.float32, jnp.bfloat16
LANE = 128
VMEM_LIMIT = 56 * 1024 * 1024
EPS = 1e-6
D_FF = 2816
SB_HEADS, SB_HD = 8, 64
SG_GROUPS, SG_GD, SG_CHUNK = 8, 64, 128
SB_W, SG_W = SB_HEADS * SB_HD, SG_GROUPS * SG_GD
MLA_HEADS, MLA_NOPE, MLA_ROPE, MLA_V = 16, 64, 32, 64
MLA_QK = MLA_NOPE + MLA_ROPE
MLA_QL, MLA_KVL = 512, 256
ROPE_THETA = 10000.0
MEM_HEADS, MEM_HD = 4, 256
SB_SCALE, MLA_SCALE, MEM_SCALE = SB_HD ** -0.5, MLA_QK ** -0.5, MEM_HD ** -0.5
ADAM_LR, ADAM_B1, ADAM_B2, ADAM_EPS, ADAM_WD, ADAM_STEP = 0.001, 0.9, 0.999, 1e-08, 0.01, 10
MESH = pl.DeviceIdType.MESH
ANY = pl.BlockSpec(memory_space=pl.ANY)


def _params(n_axes):
    return pltpu.CompilerParams(dimension_semantics=("arbitrary",) * n_axes, vmem_limit_bytes=VMEM_LIMIT)


MM_TILE_CAP = 1408
MM_VMEM_BUDGET = 40 * 1024 * 1024


def _tile(dim, cap):
    if dim <= cap:
        return dim
    best = max(t for t in range(LANE, cap + 1, LANE) if dim % t == 0)
    return best


def _mm(a, b, *, ta=False, tb=False, out_dtype=F32, scale=1.0, residual=None, bias=None, norm_bwd=None, a_off=(0, 0),
        b_off=(0, 0), m=None, n=None, k=None, name):
    am, ak = (a.shape[1], a.shape[0]) if ta else a.shape
    bk, bn = (b.shape[1], b.shape[0]) if tb else b.shape
    M, N, K = m or am, n or bn, k or ak
    tm, tn = _tile(M, MM_TILE_CAP if norm_bwd is None else MM_TILE_CAP // 2), _tile(N, MM_TILE_CAP)
    n_full = (residual is not None) + (2 if norm_bwd is not None else 0)
    fixed = tm * tn * (4 + 2 * jnp.dtype(out_dtype).itemsize + 8 * n_full)
    per_k = (tm * (2 * a.dtype.itemsize + 2) + tn * (2 * b.dtype.itemsize + 2))
    tk = _tile(K, max(LANE, (MM_VMEM_BUDGET - fixed) // per_k))
    nm, nn, nk = M // tm, N // tn, K // tk
    assert norm_bwd is None or nn == 1
    a_off = (a_off[0] // (tk if ta else tm), a_off[1] // (tm if ta else tk))
    b_off = (b_off[0] // (tn if tb else tk), b_off[1] // (tk if tb else tn))
    dims = (((0 if ta else 1,), (1 if tb else 0,)), ((), ()))
    n_out = 1 if norm_bwd is None else 2

    def body(*refs):
        a_ref, b_ref = refs[0], refs[1]
        o_ref, acc_ref = refs[-1 - n_out], refs[-1]
        extras = refs[2:-1 - n_out]
        first_rows, kk = pl.program_id(0) == 0, pl.program_id(2)

        @pl.when(kk == 0)
        def _():
            acc_ref[...] = jnp.zeros_like(acc_ref)

        acc_ref[...] += lax.dot_general(a_ref[...].astype(BF16), b_ref[...].astype(BF16), dims,
                                        preferred_element_type=F32)

        @pl.when(kk == nk - 1)
        def _():
            out = acc_ref[...] * scale
            for extra in (extras if norm_bwd is None else extras[:-3]):
                out = out + extra[...].astype(F32)
            if norm_bwd is not None:
                x_ref, g_ref, dres_ref = extras[-3:]
                dg_ref = refs[-2]
                dx, dg = _rmsnorm_bwd(out, x_ref[...], g_ref[...])
                out = dx + dres_ref[...]

                @pl.when(first_rows)
                def _():
                    dg_ref[...] = jnp.zeros_like(dg_ref)

                dg_ref[...] += dg
            o_ref[...] = out.astype(o_ref.dtype)

    (ao0, ao1), (bo0, bo1) = a_off, b_off
    a_spec = (pl.BlockSpec((tk, tm), lambda i, j, kk: (kk + ao0, i + ao1)) if ta
              else pl.BlockSpec((tm, tk), lambda i, j, kk: (i + ao0, kk + ao1)))
    b_spec = (pl.BlockSpec((tn, tk), lambda i, j, kk: (j + bo0, kk + bo1)) if tb
              else pl.BlockSpec((tk, tn), lambda i, j, kk: (kk + bo0, j + bo1)))
    o_spec = pl.BlockSpec((tm, tn), lambda i, j, kk: (i, j))
    ins, in_specs = [a, b], [a_spec, b_spec]
    if residual is not None:
        ins.append(residual)
        in_specs.append(o_spec)
    if bias is not None:
        ins.append(bias)
        in_specs.append(pl.BlockSpec((1, tn), lambda i, j, kk: (0, j)))
    out_specs, out_shape = o_spec, jax.ShapeDtypeStruct((M, N), out_dtype)
    if norm_bwd is not None:
        x, gain, dres = norm_bwd
        row = pl.BlockSpec((1, tn), lambda i, j, kk: (0, 0))
        ins += [x, gain.reshape(1, N), dres]
        in_specs += [o_spec, row, o_spec]
        out_specs, out_shape = [o_spec, row], [out_shape, jax.ShapeDtypeStruct((1, N), F32)]
    return pl.pallas_call(
        body, name=name, grid=(nm, nn, nk), in_specs=in_specs, out_specs=out_specs, out_shape=out_shape,
        scratch_shapes=[pltpu.VMEM((tm, tn), F32)], compiler_params=_params(3))(*ins)


def _mm_swiglu(h, wgu, *, name):
    M, K = h.shape
    F = wgu.shape[1] // 2
    tm, tn, tk = _tile(M, 512), _tile(F, MM_TILE_CAP), _tile(K, 1024)
    nm, nf, nk = M // tm, F // tn, K // tk

    def body(h_ref, wg_ref, wu_ref, g_ref, u_ref, a_ref, accg, accu):
        kk = pl.program_id(2)

        @pl.when(kk == 0)
        def _():
            accg[...] = jnp.zeros_like(accg)
            accu[...] = jnp.zeros_like(accu)

        hb = h_ref[...]
        accg[...] += jnp.dot(hb, wg_ref[...], preferred_element_type=F32)
        accu[...] += jnp.dot(hb, wu_ref[...], preferred_element_type=F32)

        @pl.when(kk == nk - 1)
        def _():
            g, u = accg[...], accu[...]
            g_ref[...] = g.astype(BF16)
            u_ref[...] = u.astype(BF16)
            a_ref[...] = (g * jax.nn.sigmoid(g) * u).astype(BF16)

    o_spec = pl.BlockSpec((tm, tn), lambda i, j, kk: (i, j))
    shp = jax.ShapeDtypeStruct((M, F), BF16)
    return pl.pallas_call(
        body, name=name, grid=(nm, nf, nk),
        in_specs=[pl.BlockSpec((tm, tk), lambda i, j, kk: (i, kk)),
                  pl.BlockSpec((tk, tn), lambda i, j, kk: (kk, j)),
                  pl.BlockSpec((tk, tn), lambda i, j, kk: (kk, j + nf))],
        out_specs=[o_spec, o_spec, o_spec], out_shape=[shp, shp, shp],
        scratch_shapes=[pltpu.VMEM((tm, tn), F32), pltpu.VMEM((tm, tn), F32)],
        compiler_params=_params(3))(h, wgu, wgu)


def _mm_dswiglu(dy, wd, gate, up, *, scale, name):
    M, K = dy.shape
    F = wd.shape[0]
    tm, tn, tk = _tile(M, 512), _tile(F, MM_TILE_CAP), _tile(K, 1024)
    nm, nf, nk = M // tm, F // tn, K // tk

    def body(dy_ref, wd_ref, g_ref, u_ref, dg_ref, du_ref, acc):
        kk = pl.program_id(2)

        @pl.when(kk == 0)
        def _():
            acc[...] = jnp.zeros_like(acc)

        acc[...] += lax.dot_general(dy_ref[...].astype(BF16), wd_ref[...], (((1,), (1,)), ((), ())),
                                    preferred_element_type=F32)

        @pl.when(kk == nk - 1)
        def _():
            da = acc[...] * scale
            g, u = g_ref[...].astype(F32), u_ref[...].astype(F32)
            sg = jax.nn.sigmoid(g)
            du_ref[...] = (da * g * sg).astype(BF16)
            dg_ref[...] = (da * u * sg * (1.0 + g * (1.0 - sg))).astype(BF16)

    o_spec = pl.BlockSpec((tm, tn), lambda i, j, kk: (i, j))
    shp = jax.ShapeDtypeStruct((M, F), BF16)
    return pl.pallas_call(
        body, name=name, grid=(nm, nf, nk),
        in_specs=[pl.BlockSpec((tm, tk), lambda i, j, kk: (i, kk)),
                  pl.BlockSpec((tn, tk), lambda i, j, kk: (j, kk)), o_spec, o_spec],
        out_specs=[o_spec, o_spec], out_shape=[shp, shp],
        scratch_shapes=[pltpu.VMEM((tm, tn), F32)], compiler_params=_params(3))(dy, wd, gate, up)


HEAD_ROWS = 1024


def _row_tile(rows, cap):
    t = cap
    while t >= 8:
        if rows % t == 0:
            return t
        t //= 2
    return rows


def _rowwise(fn, rows, consts, outs, sums=(), hsums=(), *, heads=None, tm=256, name):
    rows = [r if isinstance(r, tuple) else (r, r.shape[1], None) for r in rows]
    S = rows[0][0].shape[0]
    tm = _row_tile(S, tm)
    nh = heads or 1
    n_r, n_c, n_o, n_h, n_s = len(rows), len(consts), len(outs), len(hsums), len(sums)

    def body(*refs):
        r = [x[...] for x in refs[:n_r]]
        c = [x[...] for x in refs[n_r:n_r + n_c]]
        o_refs = refs[n_r + n_c:n_r + n_c + n_o]
        h_refs = refs[n_r + n_c + n_o:n_r + n_c + n_o + n_h]
        s_refs = refs[n_r + n_c + n_o + n_h:]
        res = fn(*r, *c)
        res = res if isinstance(res, (tuple, list)) else (res,)
        for ref, val in zip(o_refs, res[:n_o]):
            ref[...] = val.astype(ref.dtype)
        if n_h:
            @pl.when(pl.program_id(1) == 0)
            def _():
                for ref in h_refs:
                    ref[...] = jnp.zeros_like(ref)
            for ref, val in zip(h_refs, res[n_o:n_o + n_h]):
                ref[...] += val
        if n_s:
            @pl.when((pl.program_id(0) == 0) & (pl.program_id(1) == 0))
            def _():
                for ref in s_refs:
                    ref[...] = jnp.zeros_like(ref)
            for ref, val in zip(s_refs, res[n_o + n_h:]):
                ref[...] += val

    def col(colfn):
        return (lambda i, h: (i, 0)) if colfn is None else (lambda i, h: (i, colfn(h)))

    in_specs = [pl.BlockSpec((tm, w), col(cf)) for _, w, cf in rows]
    in_specs += [pl.BlockSpec(a.shape, lambda i, h, nd=a.ndim: (0,) * nd) for a in consts]
    out_specs = [pl.BlockSpec((tm, w // nh), (lambda i, h: (i, h)) if heads else (lambda i, h: (i, 0))) for w, _ in outs]
    out_specs += [pl.BlockSpec((tm, w), lambda i, h: (i, 0)) for w in hsums]
    out_specs += [pl.BlockSpec(sh, lambda i, h, nd=len(sh): (0,) * nd) for sh in sums]
    out_shape = [jax.ShapeDtypeStruct((S, w), dt) for w, dt in outs]
    out_shape += [jax.ShapeDtypeStruct((S, w), F32) for w in hsums]
    out_shape += [jax.ShapeDtypeStruct(sh, F32) for sh in sums]
    return pl.pallas_call(body, name=name, grid=(S // tm, nh), in_specs=in_specs, out_specs=out_specs,
                          out_shape=out_shape, compiler_params=_params(2))(*[a for a, _, _ in rows], *consts)


def _rms(x, width=None):
    width = width or x.shape[-1]
    return lax.rsqrt(jnp.sum(x * x, axis=-1, keepdims=True) * (1.0 / width) + EPS)


def _rmsnorm_fwd(x, g, width=None):
    return x * _rms(x, width) * g


def _rmsnorm_bwd(dy, x, g, width=None):
    width = width or x.shape[-1]
    r = _rms(x, width)
    xn = x * r
    dxn = dy * g
    dx = r * (dxn - xn * (jnp.sum(dxn * xn, axis=-1, keepdims=True) * (1.0 / width)))
    return dx, jnp.sum(dy * xn, axis=0, keepdims=True)


def _norm_rows(x, g, *, name, out_dtype=BF16):
    D = x.shape[1]
    return _rowwise(lambda xv, gv: _rmsnorm_fwd(xv.astype(F32), gv), [x], [g.reshape(1, D)], [(D, out_dtype)],
                    name=name)[0]


def _norm_rows_bwd(dh, x, g, dres, *, name):
    D = x.shape[1]

    def fn(dhv, xv, *rest):
        dx, dg = _rmsnorm_bwd(dhv.astype(F32), xv, rest[-1])
        return (dx + rest[0] if dres is not None else dx), dg

    rows = [dh, x] + ([dres] if dres is not None else [])
    return _rowwise(fn, rows, [g.reshape(1, D)], [(D, F32)], [(1, D)], name=name)


def _softplus(z):
    return jnp.where(z > 20.0, z, jnp.log(1.0 + jnp.exp(z)))


def _running_sum(v, u, split=True):
    if not split:
        return jnp.dot(v.astype(BF16), u, preferred_element_type=F32)
    hi = lax.bitcast_convert_type(lax.bitcast_convert_type(v, jnp.uint32) & jnp.uint32(0xFFFF0000), F32)
    return (jnp.dot(hi.astype(BF16), u, preferred_element_type=F32)
            + jnp.dot((v - hi).astype(BF16), u, preferred_element_type=F32))


def _triangle(tk, inclusive_prefix):
    j, s = lax.broadcasted_iota(jnp.int32, (tk, tk), 0), lax.broadcasted_iota(jnp.int32, (tk, tk), 1)
    return ((j <= s) if inclusive_prefix else (j > s)).astype(BF16)


def _nt(a, b):
    return lax.dot_general(a, b, (((1,), (1,)), ((), ())), preferred_element_type=F32)


def _tn(a, b):
    return lax.dot_general(a, b, (((0,), (0,)), ((), ())), preferred_element_type=F32)


ATT_TQ, ATT_TK = 512, 512
SB_SUB = 256
FWD_GROUP = 2


def _attn_fwd(q, k, v, *, sb, causal, heads, dq, dv, group=1, kcol=None, vcol=None, sum_lane=None, name):
    S, Sk = q.shape[0], k.shape[0]
    tq, tk = min(ATT_TQ, S), min(ATT_TK, Sk)
    sub = min(SB_SUB, tk) if sb else tk
    assert tq % sub == 0 or not causal
    kcol = kcol or (lambda h: h)
    vcol = vcol or (lambda h: h)
    members = range(group)

    def body(*refs):
        if sb:
            q_ref, k_ref, v_ref, u_ref, o_ref, lse_ref, acc_ref, r_ref = refs
            r_ref[...] = jnp.zeros_like(r_ref)
        else:
            q_ref, k_ref, v_ref, o_ref, lse_ref, acc_ref, m_ref, l_ref = refs
            m_ref[...] = jnp.full_like(m_ref, -1e30)
            l_ref[...] = jnp.zeros_like(l_ref)
        first_row = pl.program_id(1) * tq
        qb = [q_ref[:, hh * dq:(hh + 1) * dq] for hh in members]
        acc_ref[...] = jnp.zeros_like(acc_ref)
        nblk = (first_row + tq) // sub if causal else Sk // sub
        nfull = (first_row + (0 if sb else 1)) // sub if causal else nblk
        n_cut = tq // sub if causal else 0

        def scores(jj):
            off = pl.multiple_of(jj * sub, sub)
            return tuple(_nt(qb[hh], k_ref[pl.ds(off, sub), hh * dq:(hh + 1) * dq]) for hh in members)

        def weigh(jj, scores_now, masked):
            off = pl.multiple_of(jj * sub, sub)
            if masked:
                kpos = off + lax.broadcasted_iota(jnp.int32, (tq, sub), 1)
                qpos = first_row + lax.broadcasted_iota(jnp.int32, (tq, sub), 0)
                valid = (kpos < qpos) if sb else (kpos <= qpos)
            for hh in members:
                vb = v_ref[pl.ds(off, sub), hh * dv:(hh + 1) * dv]
                s = scores_now[hh]
                if sb:
                    sp = _softplus(s)
                    ls = jnp.where(valid, -sp, 0.0) if masked else -sp
                    w = jnp.exp(s - sp + r_ref[hh] + _running_sum(ls, u_ref[...]))
                    if masked:
                        w = jnp.where(valid, w, 0.0)
                    acc_ref[hh] += jnp.dot(w.astype(BF16), vb, preferred_element_type=F32)
                    r_ref[hh] += jnp.sum(ls, axis=1, keepdims=True)
                else:
                    if masked:
                        s = jnp.where(valid, s, -1e30)
                    m_old = m_ref[hh]
                    m_new = jnp.maximum(m_old, jnp.max(s, axis=1, keepdims=True))
                    p = jnp.exp(s - m_new)
                    alpha = jnp.exp(m_old - m_new)
                    if sum_lane is None:
                        l_ref[hh] = alpha * l_ref[hh] + jnp.sum(p, axis=1, keepdims=True)
                    acc_ref[hh] = alpha * acc_ref[hh] + jnp.dot(p.astype(BF16), vb, preferred_element_type=F32)
                    m_ref[hh] = m_new

        if sb:
            s_cur = scores(nblk - 1)
            for cut in range(n_cut):
                s_next = scores(jnp.maximum(nblk - 2 - cut, 0))
                weigh(nblk - 1 - cut, s_cur, True)
                s_cur = s_next

            def step(t, s_now):
                s_next = scores(jnp.maximum(nfull - 2 - t, 0))
                weigh(nfull - 1 - t, s_now, False)
                return s_next

            lax.fori_loop(0, nfull, step, s_cur)
        else:
            n_loop = nfull if causal else nblk - 1

            def step(t, s_now):
                s_next = scores(jnp.minimum(t + 1, nblk - 1))
                weigh(t, s_now, False)
                return s_next

            s_cur = lax.fori_loop(0, n_loop, step, scores(0))
            tail = n_cut if causal else 1
            for last in range(tail):
                s_next = scores(n_loop + last + 1) if last + 1 < tail else None
                weigh(n_loop + last, s_cur, causal)
                s_cur = s_next
        for hh in members:
            cols = slice(hh * dv, (hh + 1) * dv)
            if sb:
                o_ref[:, cols] = acc_ref[hh]
                lse_ref[hh] = r_ref[hh]
            else:
                acc = acc_ref[hh]
                l = l_ref[hh] if sum_lane is None else acc[:, sum_lane:sum_lane + 1]
                o_ref[:, cols] = acc / l
                lse_ref[hh] = m_ref[hh] + jnp.log(l)

    in_specs = [pl.BlockSpec((tq, group * dq), lambda g, i: (i, g)),
                pl.BlockSpec((Sk, group * dq), lambda g, i: (0, kcol(g))),
                pl.BlockSpec((Sk, group * dv), lambda g, i: (0, vcol(g)))]
    ins = [q, k, v]
    scratch = [pltpu.VMEM((group, tq, dv), F32), pltpu.VMEM((group, tq, 1), F32)]
    if sb:
        ins.append(_triangle(sub, inclusive_prefix=False))
        in_specs.append(pl.BlockSpec((sub, sub), lambda g, i: (0, 0)))
    else:
        scratch.append(pltpu.VMEM((group, tq, 1), F32))
    out_specs = [pl.BlockSpec((tq, group * dv), lambda g, i: (i, g)), pl.BlockSpec((group, tq, 1), lambda g, i: (g, i, 0))]
    out_shape = [jax.ShapeDtypeStruct((S, heads * dv), F32), jax.ShapeDtypeStruct((heads, S, 1), F32)]
    return pl.pallas_call(body, name=name, grid=(heads // group, S // tq), in_specs=in_specs, out_specs=out_specs,
                          out_shape=out_shape, scratch_shapes=scratch, compiler_params=_params(2))(*ins)


def _attn_bwd(q, k, v, o, do, lse, *, sb, causal, heads, dq, dv, kcol=None, vcol=None, name):
    S, Sk = q.shape[0], k.shape[0]
    tq, tk = min(ATT_TQ, S), min(ATT_TK, Sk)
    sub = min(SB_SUB, tk) if sb else tk
    assert tq % sub == 0 or not causal
    nq = S // tq
    kcol = kcol or (lambda h: h)
    vcol = vcol or (lambda h: h)

    def body(*refs):
        if sb:
            q_ref, k_ref, v_ref, o_ref, do_ref, lse_ref, u_ref, dq_ref, dk_ref, dv_ref, acc_ref, r_ref, re_ref = refs
            r_ref[...] = jnp.zeros_like(r_ref)
            re_ref[...] = jnp.zeros_like(re_ref)
        else:
            q_ref, k_ref, v_ref, o_ref, do_ref, lse_ref, dq_ref, dk_ref, dv_ref, acc_ref = refs
        first_row = pl.program_id(1) * tq

        @pl.when(first_row == 0)
        def _():
            dk_ref[...] = jnp.zeros_like(dk_ref)
            dv_ref[...] = jnp.zeros_like(dv_ref)

        qb = q_ref[...]
        dof = do_ref[...].astype(F32)
        dob = dof.astype(BF16)
        if not sb:
            dlt = jnp.sum(dof * o_ref[...], axis=1, keepdims=True)
        acc_ref[...] = jnp.zeros_like(acc_ref)
        nblk = (first_row + tq) // sub if causal else Sk // sub
        nfull = (first_row + (0 if sb else 1)) // sub if causal else nblk
        n_cut = tq // sub if causal else 0

        def products(jj):
            off = pl.multiple_of(jj * sub, sub)
            return _nt(qb, k_ref[pl.ds(off, sub), :]), _nt(dob, v_ref[pl.ds(off, sub), :])

        def piece(jj, now, masked):
            off = pl.multiple_of(jj * sub, sub)
            kb = k_ref[pl.ds(off, sub), :]
            s, dp = now
            if masked:
                qpos = first_row + lax.broadcasted_iota(jnp.int32, (tq, sub), 0)
                kpos = off + lax.broadcasted_iota(jnp.int32, (tq, sub), 1)
                valid = (kpos < qpos) if sb else (kpos <= qpos)
            if sb:
                u = u_ref[...]
                sp = _softplus(s)
                ls = jnp.where(valid, -sp, 0.0) if masked else -sp
                lb = s - sp
                w = jnp.exp(lb + (lse_ref[0] - (r_ref[...] + _running_sum(ls, u))))
                if masked:
                    w = jnp.where(valid, w, 0.0)
                e = dp * w
                ds = e - jnp.exp(lb) * (re_ref[...] + _running_sum(e, u, split=False))
                if masked:
                    ds = jnp.where(valid, ds, 0.0)
                r_ref[...] += jnp.sum(ls, axis=1, keepdims=True)
                re_ref[...] += jnp.sum(e, axis=1, keepdims=True)
            else:
                w = jnp.exp(s - lse_ref[0])
                if masked:
                    w = jnp.where(valid, w, 0.0)
                ds = w * (dp - dlt)
            dsb = ds.astype(BF16)
            dv_ref[pl.ds(off, sub), :] += _tn(w.astype(BF16), dob)
            dk_ref[pl.ds(off, sub), :] += _tn(dsb, qb)
            acc_ref[...] += jnp.dot(dsb, kb, preferred_element_type=F32)

        n_loop = nfull if causal else nblk - 1
        per_trip = tk // sub

        def steps(first, count, masked):
            ready = [products(first + c) for c in range(count)]
            for c in range(count):
                piece(first + c, ready[c], masked)

        def trip(t, carry):
            steps(t * per_trip, per_trip, False)
            return carry

        lax.fori_loop(0, n_loop // per_trip, trip, 0)
        steps(n_loop, n_cut if causal else 1, causal)
        dq_ref[...] = acc_ref[...]

    ins = [q, k, v, o, do]
    in_specs = [pl.BlockSpec((tq, dq), lambda h, i: (i, h)),
                pl.BlockSpec((Sk, dq), lambda h, i: (0, kcol(h))),
                pl.BlockSpec((Sk, dv), lambda h, i: (0, vcol(h))),
                pl.BlockSpec((tq, dv), lambda h, i: (i, h)),
                pl.BlockSpec((tq, dv), lambda h, i: (i, h))]
    scratch = [pltpu.VMEM((tq, dq), F32)]
    ins.append(lse)
    in_specs.append(pl.BlockSpec((1, tq, 1), lambda h, i: (h, i, 0)))
    if sb:
        ins.append(_triangle(sub, inclusive_prefix=True))
        in_specs.append(pl.BlockSpec((sub, sub), lambda h, i: (0, 0)))
        scratch += [pltpu.VMEM((tq, 1), F32), pltpu.VMEM((tq, 1), F32)]
    out_specs = [pl.BlockSpec((tq, dq), lambda h, i: (i, h)),
                 pl.BlockSpec((Sk, dq), lambda h, i: (0, h)),
                 pl.BlockSpec((Sk, dv), lambda h, i: (0, h))]
    out_shape = [jax.ShapeDtypeStruct((S, heads * dq), F32), jax.ShapeDtypeStruct((Sk, heads * dq), F32),
                 jax.ShapeDtypeStruct((Sk, heads * dv), F32)]
    return pl.pallas_call(body, name=name, grid=(heads, nq), in_specs=in_specs, out_specs=out_specs,
                          out_shape=out_shape, scratch_shapes=scratch, compiler_params=_params(2))(*ins)


GELU_C = 0.7978845608028654
assert 2 * SG_GD == LANE and SG_CHUNK == LANE


def _gelu(z):
    t = jnp.tanh(GELU_C * (z + 0.044715 * z * z * z))
    return 0.5 * z * (1.0 + t), t


def _gelu_grad(z, t):
    return 0.5 * (1.0 + t) + 0.5 * z * (1.0 - t * t) * GELU_C * (1.0 + 3.0 * 0.044715 * z * z)


def _layernorm_parts(g):
    d = g - jnp.mean(g, axis=-1, keepdims=True)
    rstd = lax.rsqrt(jnp.mean(d * d, axis=-1, keepdims=True) + EPS)
    return d * rstd, rstd


def _gelu_ln(z, gain, bias, *, name):
    def fn(zv, gn, bs):
        a, _ = _gelu(zv)
        y, _ = _layernorm_parts(a[:, SG_W:])
        return a[:, :SG_W], y * gn + bs

    return _rowwise(fn, [z], [gain.reshape(1, SG_W), bias.reshape(1, SG_W)], [(SG_W, F32), (SG_W, BF16)], name=name)


def _gelu_ln_bwd(z, du, dgl, gain, *, name):
    def fn(zv, duv, dglv, gn):
        a, t = _gelu(zv)
        y, rstd = _layernorm_parts(a[:, SG_W:])
        dy = dglv * gn
        dgg = rstd * (dy - jnp.mean(dy, axis=-1, keepdims=True) - y * jnp.mean(dy * y, axis=-1, keepdims=True))
        dz = jnp.concatenate([duv, dgg], axis=1) * _gelu_grad(zv, t)
        return dz, jnp.sum(dglv * y, axis=0, keepdims=True), jnp.sum(dglv, axis=0, keepdims=True)

    return _rowwise(fn, [z, du, dgl], [gain.reshape(1, SG_W)], [(2 * SG_W, BF16)], [(1, SG_W), (1, SG_W)], name=name)


def _sg_masks():
    tri = lax.broadcasted_iota(jnp.int32, (SG_CHUNK, SG_CHUNK), 0) >= lax.broadcasted_iota(jnp.int32, (SG_CHUNK, SG_CHUNK), 1)
    first = lax.broadcasted_iota(jnp.int32, (SG_CHUNK, LANE), 1) < SG_GD
    return tri, first


def _spatial(gl, u, w, bt, *, name):
    S = gl.shape[0]
    tm = _row_tile(S, 512)
    nch = tm // SG_CHUNK

    def body(gl_ref, u_ref, w_ref, bt_ref, o_ref):
        tri, first = _sg_masks()
        for p in range(SG_W // LANE):
            cols = slice(p * LANE, (p + 1) * LANE)
            wa = jnp.where(tri, w_ref[2 * p], 0.0).astype(BF16)
            wb = jnp.where(tri, w_ref[2 * p + 1], 0.0).astype(BF16)
            for ci in range(nch):
                rws = slice(ci * SG_CHUNK, (ci + 1) * SG_CHUNK)
                g = gl_ref[rws, cols]
                zero = jnp.zeros_like(g)
                mixed = (jnp.dot(wa, jnp.where(first, g, zero), preferred_element_type=F32)
                         + jnp.dot(wb, jnp.where(first, zero, g), preferred_element_type=F32) + bt_ref[:, cols])
                o_ref[rws, cols] = u_ref[rws, cols] * mixed

    row = pl.BlockSpec((tm, SG_W), lambda i: (i, 0))
    return pl.pallas_call(
        body, name=name, grid=(S // tm,),
        in_specs=[row, row, pl.BlockSpec(w.shape, lambda i: (0, 0, 0)), pl.BlockSpec(bt.shape, lambda i: (0, 0))],
        out_specs=row, out_shape=jax.ShapeDtypeStruct((S, SG_W), F32), compiler_params=_params(1))(gl, u, w, bt)


def _spatial_bwd(d_o, gl, u, w, bt, *, name):
    S = gl.shape[0]
    tm = _row_tile(S, 512)
    nch = tm // SG_CHUNK
    nsteps = S // tm

    def body(do_ref, gl_ref, u_ref, w_ref, bt_ref, du_ref, dgl_ref, dw_ref, db_ref, dbt_ref):
        tri, first = _sg_masks()
        step = pl.program_id(0)

        @pl.when(step == 0)
        def _():
            dw_ref[...] = jnp.zeros_like(dw_ref)
            dbt_ref[...] = jnp.zeros_like(dbt_ref)

        for p in range(SG_W // LANE):
            cols = slice(p * LANE, (p + 1) * LANE)
            wa = jnp.where(tri, w_ref[2 * p], 0.0).astype(BF16)
            wb = jnp.where(tri, w_ref[2 * p + 1], 0.0).astype(BF16)
            for ci in range(nch):
                rws = slice(ci * SG_CHUNK, (ci + 1) * SG_CHUNK)
                g = gl_ref[rws, cols]
                zero = jnp.zeros_like(g)
                mixed = (jnp.dot(wa, jnp.where(first, g, zero), preferred_element_type=F32)
                         + jnp.dot(wb, jnp.where(first, zero, g), preferred_element_type=F32) + bt_ref[:, cols])
                dov = do_ref[rws, cols]
                du_ref[rws, cols] = dov * mixed
                dm = dov * u_ref[rws, cols]
                dbt_ref[:, cols] += dm
                dma = jnp.where(first, dm, 0.0).astype(BF16)
                dmb = jnp.where(first, 0.0, dm).astype(BF16)
                dw_ref[2 * p] += jnp.where(tri, _nt(dma, g), 0.0)
                dw_ref[2 * p + 1] += jnp.where(tri, _nt(dmb, g), 0.0)
                dgl_ref[rws, cols] = _tn(wa, dma) + _tn(wb, dmb)

        @pl.when(step == nsteps - 1)
        def _():
            lane = lax.broadcasted_iota(jnp.int32, (SG_CHUNK, LANE), 1)
            acc = jnp.zeros((SG_CHUNK, LANE), F32)
            for p in range(SG_W // LANE):
                blk = dbt_ref[:, p * LANE:(p + 1) * LANE]
                sa = jnp.sum(jnp.where(first, blk, 0.0), axis=1, keepdims=True)
                sb_ = jnp.sum(jnp.where(first, 0.0, blk), axis=1, keepdims=True)
                acc = acc + jnp.where(lane == 2 * p, sa, 0.0) + jnp.where(lane == 2 * p + 1, sb_, 0.0)
            db_ref[...] = acc

    row = pl.BlockSpec((tm, SG_W), lambda i: (i, 0))
    return pl.pallas_call(
        body, name=name, grid=(nsteps,),
        in_specs=[row, row, row, pl.BlockSpec(w.shape, lambda i: (0, 0, 0)), pl.BlockSpec(bt.shape, lambda i: (0, 0))],
        out_specs=[row, row, pl.BlockSpec(w.shape, lambda i: (0, 0, 0)), pl.BlockSpec((SG_CHUNK, LANE), lambda i: (0, 0))],
        out_shape=[jax.ShapeDtypeStruct((S, SG_W), F32), jax.ShapeDtypeStruct((S, SG_W), F32),
                   jax.ShapeDtypeStruct(w.shape, F32), jax.ShapeDtypeStruct((SG_CHUNK, LANE), F32)],
        scratch_shapes=[pltpu.VMEM((SG_CHUNK, SG_W), F32)], compiler_params=_params(1))(d_o, gl, u, w, bt)


ROPE_HALF = MLA_ROPE // 2
KR_COL = (MLA_QL + MLA_KVL) // LANE
MLA_IN_PAD = MLA_QL + MLA_KVL + LANE


def _rope_tables(positions):
    inv_freq = ROPE_THETA ** (-jnp.arange(ROPE_HALF, dtype=F32) / ROPE_HALF)
    ang = positions.astype(F32)[:, None] * inv_freq
    cos, sin = jnp.cos(ang), jnp.sin(ang)
    S = positions.shape[0]
    z16, tail = jnp.zeros((S, ROPE_HALF), F32), jnp.zeros((S, LANE - MLA_QK), F32)
    ones = jnp.ones((S, MLA_NOPE), F32)
    zeros = jnp.zeros((S, MLA_NOPE), F32)
    return (jnp.concatenate([ones, cos, cos, tail], axis=1), jnp.concatenate([zeros, z16, sin, tail], axis=1),
            jnp.concatenate([zeros, -sin, z16, tail], axis=1))


def _rope(x, cos, sa, sb):
    return x * cos + pltpu.roll(x, ROPE_HALF, 1) * sa + pltpu.roll(x, LANE - ROPE_HALF, 1) * sb


def _rope_t(dy, cos, sa, sb):
    return dy * cos + pltpu.roll(dy * sa, LANE - ROPE_HALF, 1) + pltpu.roll(dy * sb, ROPE_HALF, 1)


def _mla_lora(P, qlg, kvlg, *, name):
    def fn(pv, a, b):
        return _rmsnorm_fwd(pv[:, :MLA_QL], a), _rmsnorm_fwd(pv[:, MLA_QL:MLA_QL + MLA_KVL], b)

    return _rowwise(fn, [P], [qlg.reshape(1, MLA_QL), kvlg.reshape(1, MLA_KVL)], [(MLA_QL, BF16), (MLA_KVL, BF16)], name=name)


def _mla_lora_bwd(dcq, dckv, dkr, P, qlg, kvlg, *, name):
    def fn(d1, d2, d3, pv, a, b):
        x1, g1 = _rmsnorm_bwd(d1, pv[:, :MLA_QL], a)
        x2, g2 = _rmsnorm_bwd(d2, pv[:, MLA_QL:MLA_QL + MLA_KVL], b)
        return jnp.concatenate([x1, x2, d3], axis=1), g1, g2

    return _rowwise(fn, [dcq, dckv, dkr, P], [qlg.reshape(1, MLA_QL), kvlg.reshape(1, MLA_KVL)], [(MLA_IN_PAD, BF16)],
                    [(1, MLA_QL), (1, MLA_KVL)], name=name)


def _mla_qk(q_pre, k_pre, P, tabs, qg, kg, *, name):
    def fn(qp, kp, kr, c, a, b, qgv, kgv):
        return (_rope(_rmsnorm_fwd(qp, qgv, MLA_QK), c, a, b) * MLA_SCALE,
                _rope(_rmsnorm_fwd(kp + kr, kgv, MLA_QK), c, a, b))

    hcol = lambda h: h
    rows = [(q_pre, LANE, hcol), (k_pre, LANE, hcol), (P, LANE, lambda h: KR_COL), *tabs]
    w = MLA_HEADS * LANE
    return _rowwise(fn, rows, [qg, kg], [(w, BF16), (w, BF16)], heads=MLA_HEADS, tm=HEAD_ROWS, name=name)


def _mla_qk_bwd(dq, dk, q_pre, k_pre, P, tabs, qg, kg, *, name):
    def fn(dqv, dkv, qp, kp, kr, c, a, b, qgv, kgv):
        dqp, dqg = _rmsnorm_bwd(_rope_t(dqv * MLA_SCALE, c, a, b), qp, qgv, MLA_QK)
        dkp, dkg = _rmsnorm_bwd(_rope_t(dkv, c, a, b), kp + kr, kgv, MLA_QK)
        lane = lax.broadcasted_iota(jnp.int32, (1, LANE), 1)
        return dqp, dkp, jnp.where((lane >= MLA_NOPE) & (lane < MLA_QK), dkp, 0.0), dqg, dkg

    hcol = lambda h: h
    rows = [(dq, LANE, hcol), (dk, LANE, hcol), (q_pre, LANE, hcol), (k_pre, LANE, hcol), (P, LANE, lambda h: KR_COL), *tabs]
    w = MLA_HEADS * LANE
    return _rowwise(fn, rows, [qg, kg], [(w, BF16), (w, BF16)], [(1, LANE), (1, LANE)], [LANE], heads=MLA_HEADS,
                    tm=HEAD_ROWS, name=name)


def _head_norm(x, g, *, heads, width, colfn=None, scale=1.0, name):
    return _rowwise(lambda xv, gv: _rmsnorm_fwd(xv, gv) * scale, [(x, width, colfn or (lambda h: h))],
                    [g.reshape(1, width)], [(heads * width, BF16)], heads=heads, tm=HEAD_ROWS, name=name)[0]


def _head_norm_bwd(dy, x, g, *, heads, width, colfn=None, scale=1.0, out_dtype, name):
    return _rowwise(lambda dv_, xv, gv: _rmsnorm_bwd(dv_ * scale, xv, gv),
                    [(dy, width, lambda h: h), (x, width, colfn or (lambda h: h))],
                    [g.reshape(1, width)], [(heads * width, out_dtype)], [(1, width)], heads=heads, tm=HEAD_ROWS,
                    name=name)


def _loss_grad(y, tgt, *, name):
    D = y.shape[1]

    def fn(yv, tv):
        d = yv - tv
        return d * (1.0 / D), jnp.sum(d * d, axis=0, keepdims=True) * (0.5 / D)

    dy, part = _rowwise(fn, [y, tgt], [], [(D, F32)], [(1, D)], name=name)
    return jnp.sum(part), dy


def _adamw(w, g, m, v, *, name):
    shape = w.shape
    two_d = (-1, shape[-1])

    def fn(wv, gv, mv, vv):
        m2 = ADAM_B1 * mv + (1.0 - ADAM_B1) * gv
        v2 = ADAM_B2 * vv + (1.0 - ADAM_B2) * (gv * gv)
        m_hat = m2 / (1.0 - ADAM_B1 ** ADAM_STEP)
        v_hat = v2 / (1.0 - ADAM_B2 ** ADAM_STEP)
        return -ADAM_LR * (m_hat / (jnp.sqrt(v_hat) + ADAM_EPS) + ADAM_WD * wv), m2, v2

    outs = _rowwise(fn, [t.reshape(two_d) for t in (w, g, m, v)], [], [(shape[-1], F32)] * 3, name=name)
    return [o.reshape(shape) for o in outs]


def _pad_cols(w, heads, hd):
    k = w.shape[0]
    return jnp.pad(w.reshape(k, heads, hd), ((0, 0), (0, 0), (0, LANE - hd))).reshape(k, heads * LANE)


def _unpad_cols(w, heads, hd):
    k = w.shape[0]
    return w.reshape(k, heads, LANE)[:, :, :hd].reshape(k, heads * hd)


def _pad_rows(w, heads, hd):
    n = w.shape[1]
    return jnp.pad(w.reshape(heads, hd, n), ((0, 0), (0, LANE - hd), (0, 0))).reshape(heads * LANE, n)


def _unpad_rows(w, heads, hd):
    n = w.shape[1]
    return w.reshape(heads, LANE, n)[:, :hd, :].reshape(heads * hd, n)


def _ffn_fwd(x, g, wgu, wd, tag):
    h = _norm_rows(x, g, name=tag + "_norm")
    gate, up, act = _mm_swiglu(h, wgu, name=tag + "_gu")
    y = _mm(act, wd, scale=0.5, residual=x, name=tag + "_down")
    return y, (x, h, gate, up, act)


def _ffn_bwd(dy, saved, g, wgu, wd, tag):
    x, h, gate, up, act = saved
    F = wd.shape[0]
    dwd = _mm(act, dy, ta=True, scale=0.5, name=tag + "_dwd")
    dgate, dup = _mm_dswiglu(dy, wd, gate, up, scale=0.5, name=tag + "_dact")
    dh = _mm(dgate, wgu, tb=True, name=tag + "_dh_g")
    dx, dg = _mm(dup, wgu, tb=True, b_off=(0, F), residual=dh, norm_bwd=(x, g, dy), name=tag + "_dh_u")
    dwgu = jnp.concatenate([_mm(h, dgate, ta=True, name=tag + "_dwg"), _mm(h, dup, ta=True, name=tag + "_dwu")], axis=1)
    return dx, dg, dwgu, dwd


def _even_weights(w_in, w_out):
    parts = [w_in[:, :SB_W] * SB_SCALE, w_in[:, SB_W:2 * SB_W], w_in[:, 2 * SB_W:3 * SB_W]]
    wqkv = jnp.concatenate([_pad_cols(p, SB_HEADS, SB_HD) for p in parts], axis=1)
    return wqkv, w_in[:, 3 * SB_W:], _pad_rows(w_out[:SB_W], SB_HEADS, SB_HD), w_out[SB_W:]


def _even_fwd(x, g, wts, ln_g, ln_b, sgu_w, bt, tag):
    wqkv, wz, wo_sb, wo_sg = wts
    h = _norm_rows(x, g, name=tag + "_norm")
    qkv = _mm(h, wqkv, out_dtype=BF16, name=tag + "_qkv")
    z = _mm(h, wz, name=tag + "_z")
    o_sb, tot = _attn_fwd(qkv, qkv, qkv, sb=True, causal=True, heads=SB_HEADS, dq=LANE, dv=LANE, group=FWD_GROUP,
                          kcol=lambda g: SB_HEADS // FWD_GROUP + g, vcol=lambda g: 2 * SB_HEADS // FWD_GROUP + g,
                          name=tag + "_sb")
    u, gl = _gelu_ln(z, ln_g, ln_b, name=tag + "_geluln")
    o_sg = _spatial(gl, u, sgu_w, bt, name=tag + "_sgu")
    y = _mm(o_sb, wo_sb, residual=x, name=tag + "_out_sb")
    y = _mm(o_sg, wo_sg, residual=y, name=tag + "_out_sg")
    return y, (x, h, qkv, z, o_sb, tot, u, gl, o_sg)


def _even_bwd(dy, saved, g, wts, ln_g, sgu_w, bt, tag):
    wqkv, wz, wo_sb, wo_sg = wts
    x, h, qkv, z, o_sb, tot, u, gl, o_sg = saved
    do_sb = _mm(dy, wo_sb, tb=True, name=tag + "_do_sb")
    do_sg = _mm(dy, wo_sg, tb=True, name=tag + "_do_sg")
    dwo = jnp.concatenate([_unpad_rows(_mm(o_sb, dy, ta=True, name=tag + "_dwo_sb"), SB_HEADS, SB_HD),
                           _mm(o_sg, dy, ta=True, name=tag + "_dwo_sg")], axis=0)
    dq, dk, dv = _attn_bwd(qkv, qkv, qkv, o_sb, do_sb, tot, sb=True, causal=True, heads=SB_HEADS, dq=LANE, dv=LANE,
                           kcol=lambda hh: SB_HEADS + hh, vcol=lambda hh: 2 * SB_HEADS + hh, name=tag + "_sb_bwd")
    du, dgl, dsgu_w, db_t = _spatial_bwd(do_sg, gl, u, sgu_w, bt, name=tag + "_sgu_bwd")
    dz, dln_g, dln_b = _gelu_ln_bwd(z, du, dgl, ln_g, name=tag + "_geluln_bwd")
    dh = _mm(dz, wz, tb=True, name=tag + "_dh_z")
    dws = []
    for i, (d, nm) in enumerate(((dq, "q"), (dk, "k"), (dv, "v"))):
        dh = _mm(d, wqkv, tb=True, b_off=(0, i * SB_HEADS * LANE), residual=dh,
                 norm_bwd=(x, g, dy) if nm == "v" else None, name=tag + "_dh_" + nm)
        dws.append(_unpad_cols(_mm(h, d, ta=True, scale=SB_SCALE if nm == "q" else 1.0, name=tag + "_dw_" + nm),
                               SB_HEADS, SB_HD))
    dws.append(_mm(h, dz, ta=True, name=tag + "_dw_z"))
    dx, dg = dh
    return dx, dict(mix_norm=dg, sbg_w_in=jnp.concatenate(dws, axis=1), sgu_ln_gain=dln_g, sgu_ln_bias=dln_b,
                    sgu_w=dsgu_w, sgu_b=db_t[:, :SG_GROUPS].T, sbg_w_out=dwo)


def _mla_weights(w_in, w_uq, w_ukv, w_out, q_gain, k_gain):
    d = w_in.shape[0]
    lat = MLA_QL + MLA_KVL
    w_in_ext = jnp.concatenate([w_in[:, :lat], jnp.zeros((d, MLA_NOPE), w_in.dtype), w_in[:, lat:],
                                jnp.zeros((d, LANE - MLA_QK), w_in.dtype)], axis=1)
    kv = w_ukv.reshape(MLA_KVL, MLA_HEADS, MLA_NOPE + MLA_V)
    wk = _pad_cols(kv[:, :, :MLA_NOPE].reshape(MLA_KVL, -1), MLA_HEADS, MLA_NOPE)
    wv = _pad_cols(kv[:, :, MLA_NOPE:].reshape(MLA_KVL, -1), MLA_HEADS, MLA_V)
    pad_gain = lambda gn: jnp.pad(gn.reshape(1, MLA_QK), ((0, 0), (0, LANE - MLA_QK)))
    return (w_in_ext, _pad_cols(w_uq, MLA_HEADS, MLA_QK), wk, wv, _pad_rows(w_out, MLA_HEADS, MLA_V),
            pad_gain(q_gain), pad_gain(k_gain))


def _mla_fwd(x, g, wts, qlg, kvlg, tabs, tag):
    w_in, w_uq, wk, wv, w_out, qg, kg = wts
    h = _norm_rows(x, g, name=tag + "_norm")
    P = _mm(h, w_in, name=tag + "_in")
    cqn, ckvn = _mla_lora(P, qlg, kvlg, name=tag + "_lora")
    q_pre = _mm(cqn, w_uq, name=tag + "_uq")
    k_pre = _mm(ckvn, wk, name=tag + "_uk")
    ones_lane = jnp.tile((jnp.arange(LANE) == MLA_V).astype(F32), MLA_HEADS)[None, :]
    v = _mm(ckvn, wv, out_dtype=BF16, bias=ones_lane, name=tag + "_uv")
    q, k = _mla_qk(q_pre, k_pre, P, tabs, qg, kg, name=tag + "_qk")
    o, lse = _attn_fwd(q, k, v, sb=False, causal=True, heads=MLA_HEADS, dq=LANE, dv=LANE, group=FWD_GROUP,
                       sum_lane=MLA_V, name=tag + "_attn")
    y = _mm(o, w_out, residual=x, name=tag + "_out")
    return y, (x, h, P, cqn, ckvn, q_pre, k_pre, q, k, v, o, lse)


def _mla_bwd(dy, saved, g, wts, qlg, kvlg, tabs, tag):
    w_in, w_uq, wk, wv, w_out, qg, kg = wts
    x, h, P, cqn, ckvn, q_pre, k_pre, q, k, v, o, lse = saved
    do = _mm(dy, w_out, tb=True, name=tag + "_do")
    dw_out = _unpad_rows(_mm(o, dy, ta=True, name=tag + "_dwo"), MLA_HEADS, MLA_V)
    dq, dk, dv = _attn_bwd(q, k, v, o, do, lse, sb=False, causal=True, heads=MLA_HEADS, dq=LANE, dv=LANE,
                           name=tag + "_attn_bwd")
    dq_pre, dk_pre, dkr, dqg, dkg = _mla_qk_bwd(dq, dk, q_pre, k_pre, P, tabs, qg, kg, name=tag + "_qk_bwd")
    dcqn = _mm(dq_pre, w_uq, tb=True, name=tag + "_dcq")
    dckvn = _mm(dk_pre, wk, tb=True, name=tag + "_dckv_k")
    dckvn = _mm(dv, wv, tb=True, residual=dckvn, name=tag + "_dckv_v")
    dw_uq = _unpad_cols(_mm(cqn, dq_pre, ta=True, name=tag + "_dwuq"), MLA_HEADS, MLA_QK)
    dwk = _unpad_cols(_mm(ckvn, dk_pre, ta=True, name=tag + "_dwk"), MLA_HEADS, MLA_NOPE)
    dwv = _unpad_cols(_mm(ckvn, dv, ta=True, name=tag + "_dwv"), MLA_HEADS, MLA_V)
    dw_ukv = jnp.concatenate([dwk.reshape(MLA_KVL, MLA_HEADS, MLA_NOPE), dwv.reshape(MLA_KVL, MLA_HEADS, MLA_V)],
                             axis=2).reshape(MLA_KVL, -1)
    dP, dqlg, dkvlg = _mla_lora_bwd(dcqn, dckvn, dkr, P, qlg, kvlg, name=tag + "_lora_bwd")
    dx, dg = _mm(dP, w_in, tb=True, norm_bwd=(x, g, dy), name=tag + "_dh")
    dw_in_ext = _mm(h, dP, ta=True, name=tag + "_dwin")
    lat = MLA_QL + MLA_KVL
    dw_in = jnp.concatenate([dw_in_ext[:, :lat], dw_in_ext[:, lat + MLA_NOPE:lat + MLA_QK]], axis=1)
    return dx, dict(mix_norm=dg, mla_w_in=dw_in, mla_q_lora_gain=dqlg, mla_kv_lora_gain=dkvlg, mla_w_uq=dw_uq,
                    mla_w_ukv=dw_ukv, mla_q_gain=dqg[:, :MLA_QK], mla_k_gain=dkg[:, :MLA_QK], mla_w_out=dw_out)


def _xmem_fwd(x, mem, g, gm, wq, wkv, qg, kg, wo, tag):
    hq = _norm_rows(x, g, name=tag + "_norm")
    hm = _norm_rows(mem, gm, name=tag + "_mnorm")
    qp = _mm(hq, wq, name=tag + "_q")
    kv = _mm(hm, wkv, name=tag + "_kv")
    q = _head_norm(qp, qg, heads=MEM_HEADS, width=MEM_HD, scale=MEM_SCALE, name=tag + "_qn")
    kn = _head_norm(kv, kg, heads=MEM_HEADS, width=MEM_HD, colfn=lambda hh: 2 * hh, name=tag + "_kn")
    kvb = kv.reshape(-1, MEM_HEADS, 2, MEM_HD)[:, :, 1].reshape(-1, MEM_HEADS * MEM_HD).astype(BF16)
    o, lse = _attn_fwd(q, kn, kvb, sb=False, causal=False, heads=MEM_HEADS, dq=MEM_HD, dv=MEM_HD, group=MEM_HEADS,
                       name=tag + "_attn")
    y = _mm(o, wo, residual=x, name=tag + "_out")
    return y, (x, hq, hm, qp, kv, q, kn, kvb, o, lse)


def _xmem_bwd(dy, saved, mem, g, gm, wq, wkv, qg, kg, wo, tag):
    x, hq, hm, qp, kv, q, kn, kvb, o, lse = saved
    m = mem.shape[0]
    do = _mm(dy, wo, tb=True, name=tag + "_do")
    dwo = _mm(o, dy, ta=True, name=tag + "_dwo")
    dq, dk, dv = _attn_bwd(q, kn, kvb, o, do, lse, sb=False, causal=False, heads=MEM_HEADS, dq=MEM_HD, dv=MEM_HD,
                           name=tag + "_attn_bwd")
    dqp, dqg = _head_norm_bwd(dq, qp, qg, heads=MEM_HEADS, width=MEM_HD, scale=MEM_SCALE, out_dtype=BF16,
                              name=tag + "_qn_bwd")
    dkp, dkg = _head_norm_bwd(dk, kv, kg, heads=MEM_HEADS, width=MEM_HD, colfn=lambda hh: 2 * hh, out_dtype=F32,
                              name=tag + "_kn_bwd")
    dkv = jnp.concatenate([dkp.reshape(m, MEM_HEADS, MEM_HD), dv.reshape(m, MEM_HEADS, MEM_HD)], axis=2).reshape(m, -1)
    dwkv = _mm(hm, dkv, ta=True, name=tag + "_dwkv")
    dhm = _mm(dkv, wkv, tb=True, name=tag + "_dhm")
    _, dgm = _norm_rows_bwd(dhm, mem, gm, None, name=tag + "_dmnorm")
    dwq = _mm(hq, dqp, ta=True, name=tag + "_dwq")
    dx, dg = _mm(dqp, wq, tb=True, norm_bwd=(x, g, dy), name=tag + "_dhq")
    return dx, dict(xmem_norm=dg, xmem_mem_norm=dgm, xmem_wq=dwq, xmem_wkv=dwkv, xmem_q_gain=dqg, xmem_k_gain=dkg,
                    xmem_wo=dwo)


def _local_step(x, mem, positions, tgt, w):
    tabs = _rope_tables(positions)
    even = _even_weights(w["sbg_w_in"][0], w["sbg_w_out"][0])
    mla = _mla_weights(w["mla_w_in"][0], w["mla_w_uq"][0], w["mla_w_ukv"][0], w["mla_w_out"][0], w["mla_q_gain"][0],
                       w["mla_k_gain"][0])
    bt = jnp.repeat(w["sgu_b"][0].T, SG_GD, axis=1)
    saved = []
    for l in range(2):
        t = f"l{l}"
        x, s_pre = _ffn_fwd(x, w["ffn_pre_norm"][l], w["ffn_pre_w_gu"][l], w["ffn_pre_w_down"][l], t + "_pre")
        if l == 0:
            x, s_mix = _even_fwd(x, w["mix_norm"][0], even, w["sgu_ln_gain"][0], w["sgu_ln_bias"][0], w["sgu_w"][0], bt,
                                 t + "_even")
        else:
            x, s_mix = _mla_fwd(x, w["mix_norm"][1], mla, w["mla_q_lora_gain"][0], w["mla_kv_lora_gain"][0], tabs,
                                t + "_mla")
        x, s_xm = _xmem_fwd(x, mem, w["xmem_norm"][l], w["xmem_mem_norm"][l], w["xmem_wq"][l], w["xmem_wkv"][l],
                            w["xmem_q_gain"][l], w["xmem_k_gain"][l], w["xmem_wo"][l], t + "_xm")
        x, s_post = _ffn_fwd(x, w["ffn_post_norm"][l], w["ffn_post_w_gu"][l], w["ffn_post_w_down"][l], t + "_post")
        saved.append((s_pre, s_mix, s_xm, s_post))
    loss, dx = _loss_grad(x, tgt, name="loss")
    grads = {}

    def put(name, l, val):
        grads.setdefault(name, {})[l] = val

    for l in (1, 0):
        t = f"l{l}"
        s_pre, s_mix, s_xm, s_post = saved[l]
        dx, dg, dwgu, dwd = _ffn_bwd(dx, s_post, w["ffn_post_norm"][l], w["ffn_post_w_gu"][l], w["ffn_post_w_down"][l],
                                     t + "_post")
        put("ffn_post_norm", l, dg), put("ffn_post_w_gu", l, dwgu), put("ffn_post_w_down", l, dwd)
        dx, gx = _xmem_bwd(dx, s_xm, mem, w["xmem_norm"][l], w["xmem_mem_norm"][l], w["xmem_wq"][l], w["xmem_wkv"][l],
                           w["xmem_q_gain"][l], w["xmem_k_gain"][l], w["xmem_wo"][l], t + "_xm")
        for k_, v_ in gx.items():
            put(k_, l, v_)
        if l == 0:
            dx, gm = _even_bwd(dx, s_mix, w["mix_norm"][0], even, w["sgu_ln_gain"][0], w["sgu_w"][0], bt, t + "_even")
        else:
            dx, gm = _mla_bwd(dx, s_mix, w["mix_norm"][1], mla, w["mla_q_lora_gain"][0], w["mla_kv_lora_gain"][0], tabs,
                              t + "_mla")
        for k_, v_ in gm.items():
            put(k_, l if k_ == "mix_norm" else 0, v_)
        dx, dg, dwgu, dwd = _ffn_bwd(dx, s_pre, w["ffn_pre_norm"][l], w["ffn_pre_w_gu"][l], w["ffn_pre_w_down"][l],
                                     t + "_pre")
        put("ffn_pre_norm", l, dg), put("ffn_pre_w_gu", l, dwgu), put("ffn_pre_w_down", l, dwd)
    return loss, dx, {k_: [v_[l] for l in sorted(v_)] for k_, v_ in grads.items()}


N_CHIPS = 4
PACK_COLS = 1024
PACK_ROW_MULTIPLE = 512


def _place():
    x, y, c = lax.axis_index("x"), lax.axis_index("y"), lax.axis_index("c")
    return x, y, c, [(1 - x, y), (x, 1 - y), (1 - x, 1 - y)]


def _hops(x, y, c):
    return ((x + 1 - c) % 2, (y + c) % 2), ((x + c) % 2, (y + 1 - c) % 2), (1 - x, 1 - y)


def _gather_chips(shard):
    R, C = shard.shape
    Rh = R // 2

    def body(x_ref, out_ref, send_sems, recv_sems):
        x, y, c = lax.axis_index("x"), lax.axis_index("y"), lax.axis_index("c")
        n1, n2, nd = _hops(x, y, c)
        me, q1, q2, qd = 2 * x + y, 2 * n1[0] + n1[1], 2 * n2[0] + n2[1], 2 * nd[0] + nd[1]

        def half(chip, core):
            return out_ref.at[chip, pl.ds(core * Rh, Rh), :]

        def copy(k, chip, core, to, src=None):
            return pltpu.make_async_remote_copy(src_ref=half(chip, core) if src is None else src, dst_ref=half(chip, core),
                                                send_sem=send_sems.at[k], recv_sem=recv_sems.at[k], device_id=to,
                                                device_id_type=MESH)

        own = x_ref.at[pl.ds(c * Rh, Rh), :]
        sibling = (x, y, 1 - c)
        sends = [copy(0, me, c, (*n1, c), src=own), copy(1, me, c, (*n2, c), src=own)]
        sends[0].start()
        sends[1].start()
        copy(0, q1, c, sibling).wait_recv()
        sends += [copy(2, q1, c, (*n2, c)), copy(3, q1, c, sibling)]
        sends[2].start()
        sends[3].start()
        copy(1, q2, c, sibling).wait_recv()
        sends.append(copy(4, q2, c, sibling))
        sends[4].start()
        copy(2, qd, c, sibling).wait_recv()
        sends.append(copy(5, qd, c, sibling))
        sends[5].start()
        copy(3, q2, 1 - c, sibling).wait_recv()
        copy(4, q1, 1 - c, sibling).wait_recv()
        copy(5, qd, 1 - c, sibling).wait_recv()
        for cp in sends:
            cp.wait_send()

    others = pl.pallas_call(
        body, name="gather_weights", out_shape=jax.ShapeDtypeStruct((N_CHIPS, R, C), shard.dtype),
        in_specs=[ANY], out_specs=ANY,
        scratch_shapes=[pltpu.SemaphoreType.DMA((6,)), pltpu.SemaphoreType.DMA((6,))])(shard)
    me = 2 * lax.axis_index("x") + lax.axis_index("y")
    return lax.dynamic_update_slice(others, shard[None], (me, 0, 0))


def _gather_devices(block):
    M, N = block.shape

    def body(x_ref, out_ref, send_sems, recv_sems, local_sem):
        x, y, c, chips = _place()
        me, sibling = (x, y, c), (x, y, 1 - c)

        def rows(px, py, pc):
            return out_ref.at[pl.ds((4 * px + 2 * py + pc) * M, M), :]

        def copy(k, blk, to, src=None):
            return pltpu.make_async_remote_copy(src_ref=rows(*blk) if src is None else src, dst_ref=rows(*blk),
                                                send_sem=send_sems.at[k], recv_sem=recv_sems.at[k], device_id=to,
                                                device_id_type=MESH)

        mine = pltpu.make_async_copy(x_ref, rows(*me), local_sem)
        mine.start()
        first = [copy(0, me, sibling, src=x_ref)]
        first += [copy(1 + j, me, (*chip, c), src=x_ref) for j, chip in enumerate(chips)]
        for cp in first:
            cp.start()
        passed = [copy(4 + j, (*chip, c), sibling) for j, chip in enumerate(chips)]
        for j, chip in enumerate(chips):
            copy(1 + j, (*chip, c), me).wait_recv()
            passed[j].start()
        copy(0, sibling, me).wait_recv()
        for j, chip in enumerate(chips):
            copy(4 + j, (*chip, 1 - c), me).wait_recv()
        for cp in first + passed:
            cp.wait_send()
        mine.wait()

    vmem = pl.BlockSpec(memory_space=pltpu.VMEM)
    return pl.pallas_call(
        body, name=f"gather_devices_{M}", out_shape=jax.ShapeDtypeStruct((8 * M, N), block.dtype),
        in_specs=[vmem], out_specs=vmem,
        scratch_shapes=[pltpu.SemaphoreType.DMA((7,)), pltpu.SemaphoreType.DMA((7,)), pltpu.SemaphoreType.DMA],
        compiler_params=pltpu.CompilerParams(vmem_limit_bytes=VMEM_LIMIT))(block)


def _swap_halves(g):
    n, R, C = g.shape
    Rh = R // 2

    def body(g_ref, a_ref, send_sem, recv_sem):
        x, y, c, _ = _place()
        cp = pltpu.make_async_remote_copy(src_ref=g_ref.at[:, pl.ds((1 - c) * Rh, Rh), :], dst_ref=a_ref,
                                          send_sem=send_sem, recv_sem=recv_sem, device_id=(x, y, 1 - c),
                                          device_id_type=MESH)
        cp.start()
        cp.wait()

    return pl.pallas_call(body, name="grad_swap_halves", out_shape=jax.ShapeDtypeStruct((n, Rh, C), g.dtype),
                          in_specs=[ANY], out_specs=ANY,
                          scratch_shapes=[pltpu.SemaphoreType.DMA, pltpu.SemaphoreType.DMA])(g)


def _add_picked(a, b, picks, *, a_row_half=None, out_dtype, name):
    n_out = picks.shape[0]
    _, rows, C = b.shape
    tr = _row_tile(rows, 512)
    nt = rows // tr
    half = jnp.zeros((1,), jnp.int32) if a_row_half is None else a_row_half

    def body(pick_ref, half_ref, a_ref, b_ref, o_ref):
        o_ref[...] = (a_ref[...].astype(F32) + b_ref[...].astype(F32)).astype(o_ref.dtype)

    spec = pltpu.PrefetchScalarGridSpec(
        num_scalar_prefetch=2, grid=(n_out, nt),
        in_specs=[pl.BlockSpec((1, tr, C), lambda j, i, pick, hf: (pick[j], hf[0] * nt + i, 0)),
                  pl.BlockSpec((1, tr, C), lambda j, i, pick, hf: (pick[j], i, 0))],
        out_specs=pl.BlockSpec((1, tr, C), lambda j, i, pick, hf: (j, i, 0)))
    return pl.pallas_call(body, name=name, grid_spec=spec, out_shape=jax.ShapeDtypeStruct((n_out, rows, C), out_dtype),
                          compiler_params=_params(2))(picks.astype(jnp.int32), half.astype(jnp.int32), a, b)


def _hop_exchange(src, hop, *, name):
    def body(s_ref, d_ref, send_sem, recv_sem):
        x, y, c = lax.axis_index("x"), lax.axis_index("y"), lax.axis_index("c")
        cp = pltpu.make_async_remote_copy(src_ref=s_ref, dst_ref=d_ref, send_sem=send_sem, recv_sem=recv_sem,
                                          device_id=(*_hops(x, y, c)[hop], c), device_id_type=MESH)
        cp.start()
        cp.wait()

    return pl.pallas_call(body, name=name, out_shape=jax.ShapeDtypeStruct(src.shape, src.dtype), in_specs=[ANY],
                          out_specs=ANY, scratch_shapes=[pltpu.SemaphoreType.DMA, pltpu.SemaphoreType.DMA])(src)


def _reduce_over_chips(g):
    x, y, c = lax.axis_index("x"), lax.axis_index("y"), lax.axis_index("c")
    n1, n2, _ = _hops(x, y, c)
    chip = lambda p: 2 * p[0] + p[1]
    near = jnp.stack([chip((x, y)), chip(n2)])
    far = jnp.stack([chip(n1), chip((1 - x, 1 - y))])
    half = c.reshape(1)
    sib = _swap_halves(g)
    kept = _add_picked(g, sib, near, a_row_half=half, out_dtype=F32, name="grad_add_near")
    sent = _add_picked(g, sib, far, a_row_half=half, out_dtype=BF16, name="grad_add_far")
    got = _hop_exchange(sent, 0, name="grad_hop_first")
    mine = _add_picked(kept, got, jnp.zeros((1,), jnp.int32), out_dtype=F32, name="grad_add_mine")
    theirs = _add_picked(kept, got, jnp.ones((1,), jnp.int32), out_dtype=BF16, name="grad_add_theirs")
    got = _hop_exchange(theirs, 1, name="grad_hop_second")
    total = _add_picked(mine, got, jnp.zeros((1,), jnp.int32), out_dtype=F32, name="grad_add_total")
    return _join_halves(total[0])


def _sum_slots(b, *, name):
    n, R, C = b.shape
    tr = _row_tile(R, 512)

    def body(b_ref, o_ref):
        acc = b_ref[0]
        for q in range(1, n):
            acc = acc + b_ref[q]
        o_ref[...] = acc

    return pl.pallas_call(body, name=name, grid=(R // tr,), in_specs=[pl.BlockSpec((n, tr, C), lambda i: (0, i, 0))],
                          out_specs=pl.BlockSpec((tr, C), lambda i: (i, 0)), out_shape=jax.ShapeDtypeStruct((R, C), F32),
                          compiler_params=_params(1))(b)


def _join_halves(r):
    Rh, C = r.shape

    def body(r_ref, o_ref, send_sem, recv_sem):
        x, y, c, _ = _place()
        own, other = o_ref.at[pl.ds(c * Rh, Rh), :], o_ref.at[pl.ds((1 - c) * Rh, Rh), :]
        cp = pltpu.make_async_remote_copy(src_ref=r_ref, dst_ref=own, send_sem=send_sem, recv_sem=recv_sem,
                                          device_id=(x, y, 1 - c), device_id_type=MESH)
        cp.start()
        pltpu.make_async_remote_copy(src_ref=r_ref, dst_ref=other, send_sem=send_sem, recv_sem=recv_sem,
                                     device_id=(x, y, 1 - c), device_id_type=MESH).wait_recv()
        cp.wait_send()

    theirs = pl.pallas_call(
        body, name="grad_join_halves", out_shape=jax.ShapeDtypeStruct((2 * Rh, C), r.dtype), in_specs=[ANY], out_specs=ANY,
        scratch_shapes=[pltpu.SemaphoreType.DMA, pltpu.SemaphoreType.DMA])(r)
    return lax.dynamic_update_slice(theirs, r, (lax.axis_index("c") * Rh, 0))


def _size(shape):
    size = 1
    for d in shape:
        size *= d
    return size


def _pack(pieces, cols, row_multiple, dtype):
    if any(p.size % cols for p in pieces):
        flat = jnp.concatenate([p.reshape(-1).astype(dtype) for p in pieces])
        pieces = [jnp.pad(flat, (0, -flat.shape[0] % cols))]
    rows = [p.reshape(-1, cols).astype(dtype) for p in pieces]
    pad = -sum(r.shape[0] for r in rows) % row_multiple
    return jnp.concatenate(rows + ([jnp.zeros((pad, cols), dtype)] if pad else []), axis=0)


def _unpack(buf, shapes):
    cols = buf.shape[1]
    if any(_size(s) % cols for s in shapes):
        flat, out, at = buf.reshape(-1), [], 0
        for shp in shapes:
            out.append(flat[at:at + _size(shp)].reshape(shp))
            at += _size(shp)
        return out
    out, at = [], 0
    for shp in shapes:
        out.append(buf[at:at + _size(shp) // cols].reshape(shp))
        at += _size(shp) // cols
    return out


SHARDED = (("ffn_pre_w_gu", 2), ("ffn_pre_w_down", 1), ("sbg_w_in", 2), ("sbg_w_out", 1), ("mla_w_in", 1),
           ("mla_w_uq", 2), ("mla_w_ukv", 2), ("mla_w_out", 1), ("xmem_wq", 1), ("xmem_wkv", 2), ("xmem_wo", 1),
           ("ffn_post_w_gu", 2), ("ffn_post_w_down", 1))
LORA_GAINS = ("mla_q_lora_gain", "mla_kv_lora_gain")
REPLICATED = ("ffn_pre_norm", "mix_norm", "sgu_ln_gain", "sgu_ln_bias", "sgu_w", "sgu_b", "mla_q_gain", "mla_k_gain",
              "xmem_norm", "xmem_mem_norm", "xmem_q_gain", "xmem_k_gain", "ffn_post_norm")
WEIGHTS = ("ffn_pre_norm", "ffn_pre_w_gu", "ffn_pre_w_down", "mix_norm", "sbg_w_in", "sgu_ln_gain", "sgu_ln_bias", "sgu_w",
           "sgu_b", "sbg_w_out", "mla_w_in", "mla_q_lora_gain", "mla_kv_lora_gain", "mla_w_uq", "mla_w_ukv", "mla_q_gain",
           "mla_k_gain", "mla_w_out", "xmem_norm", "xmem_mem_norm", "xmem_wq", "xmem_wkv", "xmem_q_gain", "xmem_k_gain",
           "xmem_wo", "ffn_post_norm", "ffn_post_w_gu", "ffn_post_w_down")
INPUTS = ("x", "mem", "positions") + WEIGHTS + ("loss_target",) + tuple("m_" + n for n in WEIGHTS) + tuple(
    "v_" + n for n in WEIGHTS)


def _step(a):
    x, y, c, _ = _place()
    chip = 2 * x + y
    shard_shapes = [a[n].shape for n, _ in SHARDED]

    gathered = _gather_chips(_pack([a[n] for n, _ in SHARDED], PACK_COLS, PACK_ROW_MULTIPLE, BF16))
    w, at = {}, 0
    for (n, ax), shp in zip(SHARDED, shard_shapes):
        rows = _size(shp) // PACK_COLS
        per_chip = gathered[:, at:at + rows].reshape((N_CHIPS,) + shp)
        at += rows
        w[n] = jnp.moveaxis(per_chip, 0, ax).reshape(shp[:ax] + (N_CHIPS * shp[ax],) + shp[ax + 1:])
    gains = jnp.zeros((8, LANE), F32)
    for r, n in enumerate(LORA_GAINS):
        gains = gains.at[r, :a[n].shape[1]].set(a[n][0])
    gains = _gather_devices(gains)
    for r, n in enumerate(LORA_GAINS):
        w[n] = jnp.concatenate([gains[16 * q + r, :a[n].shape[1]] for q in range(N_CHIPS)])[None, :]
    for n in REPLICATED:
        w[n] = a[n]

    loss, dx, grads = _local_step(a["x"][0], a["mem"][0], a["positions"][0], a["loss_target"][0], w)
    loss = lax.psum(loss, ("x", "y", "c"))
    small_names = REPLICATED + LORA_GAINS
    full = {n: jnp.stack(grads[n]).reshape(w[n].shape) for n in small_names}

    def cut(n, ax, q):
        size = w[n].shape[ax] // N_CHIPS
        return [lax.slice_in_dim(gl, q * size, (q + 1) * size, axis=ax - 1) for gl in grads[n]]

    g = jnp.stack([_pack([p for n, ax in SHARDED for p in cut(n, ax, q)], PACK_COLS, PACK_ROW_MULTIPLE, F32)
                   for q in range(N_CHIPS)])
    reduced = _reduce_over_chips(g)
    gw = dict(zip([n for n, _ in SHARDED], _unpack(reduced, shard_shapes)))

    small = _pack([full[n] for n in small_names], LANE, 256, F32)
    rows = small.shape[0]
    summed = _sum_slots(_gather_devices(small).reshape(8, rows, LANE), name="grad_sum_devices")
    for n, val in zip(small_names, _unpack(summed, [full[n].shape for n in small_names])):
        if n in LORA_GAINS:
            size = a[n].shape[1]
            val = lax.dynamic_slice_in_dim(val, chip * size, size, axis=1)
        gw[n] = val

    upd = {n: _adamw(a[n], gw[n], a["m_" + n], a["v_" + n], name="adamw_" + n) for n in WEIGHTS}
    return (loss, dx[None], *[gw[n] for n in WEIGHTS], *[upd[n][0] for n in WEIGHTS], *[upd[n][1] for n in WEIGHTS],
            *[upd[n][2] for n in WEIGHTS])


def kernel(x, mem, positions, ffn_pre_norm, ffn_pre_w_gu, ffn_pre_w_down, mix_norm, sbg_w_in, sgu_ln_gain,
           sgu_ln_bias, sgu_w, sgu_b, sbg_w_out, mla_w_in, mla_q_lora_gain, mla_kv_lora_gain, mla_w_uq, mla_w_ukv,
           mla_q_gain, mla_k_gain, mla_w_out, xmem_norm, xmem_mem_norm, xmem_wq, xmem_wkv, xmem_q_gain, xmem_k_gain,
           xmem_wo, ffn_post_norm, ffn_post_w_gu, ffn_post_w_down, loss_target, m_ffn_pre_norm, m_ffn_pre_w_gu,
           m_ffn_pre_w_down, m_mix_norm, m_sbg_w_in, m_sgu_ln_gain, m_sgu_ln_bias, m_sgu_w, m_sgu_b, m_sbg_w_out,
           m_mla_w_in, m_mla_q_lora_gain, m_mla_kv_lora_gain, m_mla_w_uq, m_mla_w_ukv, m_mla_q_gain, m_mla_k_gain,
           m_mla_w_out, m_xmem_norm, m_xmem_mem_norm, m_xmem_wq, m_xmem_wkv, m_xmem_q_gain, m_xmem_k_gain,
           m_xmem_wo, m_ffn_post_norm, m_ffn_post_w_gu, m_ffn_post_w_down, v_ffn_pre_norm, v_ffn_pre_w_gu,
           v_ffn_pre_w_down, v_mix_norm, v_sbg_w_in, v_sgu_ln_gain, v_sgu_ln_bias, v_sgu_w, v_sgu_b, v_sbg_w_out,
           v_mla_w_in, v_mla_q_lora_gain, v_mla_kv_lora_gain, v_mla_w_uq, v_mla_w_ukv, v_mla_q_gain, v_mla_k_gain,
           v_mla_w_out, v_xmem_norm, v_xmem_mem_norm, v_xmem_wq, v_xmem_wkv, v_xmem_q_gain, v_xmem_k_gain,
           v_xmem_wo, v_ffn_post_norm, v_ffn_post_w_gu, v_ffn_post_w_down):
    given = locals()
    return _step({n: given[n] for n in INPUTS})
```

```python
import functools

import jax
import jax.numpy as jnp
from jax import lax
from jax.experimental import pallas as pl
from jax.experimental.pallas import tpu as pltpu

F32, BF16 = jnp.float32, jnp.bfloat16
LANE = 128
VMEM_LIMIT = 56 * 1024 * 1024
EPS = 1e-6
D_FF = 2816
SB_HEADS, SB_HD = 8, 64
SG_GROUPS, SG_GD, SG_CHUNK = 8, 64, 128
SB_W, SG_W = SB_HEADS * SB_HD, SG_GROUPS * SG_GD
MLA_HEADS, MLA_NOPE, MLA_ROPE, MLA_V = 16, 64, 32, 64
MLA_QK = MLA_NOPE + MLA_ROPE
MLA_QL, MLA_KVL = 512, 256
ROPE_THETA = 10000.0
MEM_HEADS, MEM_HD = 4, 256
SB_SCALE, MLA_SCALE, MEM_SCALE = SB_HD ** -0.5, MLA_QK ** -0.5, MEM_HD ** -0.5
ADAM_LR, ADAM_B1, ADAM_B2, ADAM_EPS, ADAM_WD, ADAM_STEP = 0.001, 0.9, 0.999, 1e-08, 0.01, 10
MESH = pl.DeviceIdType.MESH
ANY = pl.BlockSpec(memory_space=pl.ANY)


def _params(n_axes):
    return pltpu.CompilerParams(dimension_semantics=("arbitrary",) * n_axes, vmem_limit_bytes=VMEM_LIMIT)


MM_TILE_CAP = 1408
MM_VMEM_BUDGET = 40 * 1024 * 1024


def _tile(dim, cap):
    if dim <= cap:
        return dim
    best = max(t for t in range(LANE, cap + 1, LANE) if dim % t == 0)
    return best


def _k_scratch(count, tile, nk):
    return [pltpu.VMEM(tile, F32)] * count if nk > 1 else []


def _over_k_steps(prods, acc_refs, nk, finish):
    if nk == 1:
        finish(prods)
        return
    kk = pl.program_id(2)

    @pl.when(kk == 0)
    def _():
        for ref, p in zip(acc_refs, prods):
            ref[...] = p

    @pl.when(kk > 0)
    def _():
        for ref, p in zip(acc_refs, prods):
            ref[...] += p

    @pl.when(kk == nk - 1)
    def _():
        finish([ref[...] for ref in acc_refs])


def _mm(a, b, *, ta=False, tb=False, out_dtype=F32, scale=1.0, residual=None, bias=None, norm_bwd=None, a_off=(0, 0),
        b_off=(0, 0), m=None, n=None, k=None, name):
    am, ak = (a.shape[1], a.shape[0]) if ta else a.shape
    bk, bn = (b.shape[1], b.shape[0]) if tb else b.shape
    M, N, K = m or am, n or bn, k or ak
    tm, tn = _tile(M, MM_TILE_CAP if norm_bwd is None else MM_TILE_CAP // 2), _tile(N, MM_TILE_CAP)
    n_full = (residual is not None) + (2 if norm_bwd is not None else 0)
    fixed = tm * tn * (4 + 2 * jnp.dtype(out_dtype).itemsize + 8 * n_full)
    per_k = (tm * (2 * a.dtype.itemsize + 2) + tn * (2 * b.dtype.itemsize + 2))
    tk = _tile(K, max(LANE, (MM_VMEM_BUDGET - fixed) // per_k))
    nm, nn, nk = M // tm, N // tn, K // tk
    assert norm_bwd is None or nn == 1
    a_off = (a_off[0] // (tk if ta else tm), a_off[1] // (tm if ta else tk))
    b_off = (b_off[0] // (tn if tb else tk), b_off[1] // (tk if tb else tn))
    dims = (((0 if ta else 1,), (1 if tb else 0,)), ((), ()))
    n_out = 1 if norm_bwd is None else 2

    def body(*refs):
        a_ref, b_ref = refs[0], refs[1]
        n_in = len(ins)
        o_ref, extras, acc_refs = refs[n_in], refs[2:n_in], refs[n_in + n_out:]
        first_rows = pl.program_id(0) == 0

        def finish(total):
            out = total * scale
            for extra in (extras if norm_bwd is None else extras[:-3]):
                out = out + extra[...].astype(F32)
            if norm_bwd is not None:
                x_ref, g_ref, dres_ref = extras[-3:]
                dg_ref = refs[n_in + 1]
                dx, dg = _rmsnorm_bwd(out, x_ref[...], g_ref[...])
                out = dx + dres_ref[...]

                @pl.when(first_rows)
                def _():
                    dg_ref[...] = jnp.zeros_like(dg_ref)

                dg_ref[...] += dg
            o_ref[...] = out.astype(o_ref.dtype)

        prod = lax.dot_general(a_ref[...].astype(BF16), b_ref[...].astype(BF16), dims, preferred_element_type=F32)
        _over_k_steps([prod], acc_refs, nk, lambda totals: finish(totals[0]))

    (ao0, ao1), (bo0, bo1) = a_off, b_off
    a_spec = (pl.BlockSpec((tk, tm), lambda i, j, kk: (kk + ao0, i + ao1)) if ta
              else pl.BlockSpec((tm, tk), lambda i, j, kk: (i + ao0, kk + ao1)))
    b_spec = (pl.BlockSpec((tn, tk), lambda i, j, kk: (j + bo0, kk + bo1)) if tb
              else pl.BlockSpec((tk, tn), lambda i, j, kk: (kk + bo0, j + bo1)))
    o_spec = pl.BlockSpec((tm, tn), lambda i, j, kk: (i, j))
    ins, in_specs = [a, b], [a_spec, b_spec]
    if residual is not None:
        ins.append(residual)
        in_specs.append(o_spec)
    if bias is not None:
        ins.append(bias)
        in_specs.append(pl.BlockSpec((1, tn), lambda i, j, kk: (0, j)))
    out_specs, out_shape = o_spec, jax.ShapeDtypeStruct((M, N), out_dtype)
    if norm_bwd is not None:
        x, gain, dres = norm_bwd
        row = pl.BlockSpec((1, tn), lambda i, j, kk: (0, 0))
        ins += [x, gain.reshape(1, N), dres]
        in_specs += [o_spec, row, o_spec]
        out_specs, out_shape = [o_spec, row], [out_shape, jax.ShapeDtypeStruct((1, N), F32)]
    return pl.pallas_call(
        body, name=name, grid=(nm, nn, nk), in_specs=in_specs, out_specs=out_specs, out_shape=out_shape,
        scratch_shapes=_k_scratch(1, (tm, tn), nk), compiler_params=_params(3))(*ins)


def _mm_swiglu(h, wgu, *, name):
    M, K = h.shape
    F = wgu.shape[1] // 2
    tm, tn, tk = _tile(M, 512), _tile(F, MM_TILE_CAP), _tile(K, 1024)
    nm, nf, nk = M // tm, F // tn, K // tk

    def body(h_ref, wg_ref, wu_ref, g_ref, u_ref, a_ref, *acc_refs):
        def finish(totals):
            g, u = totals
            g_ref[...] = g.astype(BF16)
            u_ref[...] = u.astype(BF16)
            a_ref[...] = (g * jax.nn.sigmoid(g) * u).astype(BF16)

        hb = h_ref[...]
        _over_k_steps([jnp.dot(hb, wg_ref[...], preferred_element_type=F32),
                       jnp.dot(hb, wu_ref[...], preferred_element_type=F32)], acc_refs, nk, finish)

    o_spec = pl.BlockSpec((tm, tn), lambda i, j, kk: (i, j))
    shp = jax.ShapeDtypeStruct((M, F), BF16)
    return pl.pallas_call(
        body, name=name, grid=(nm, nf, nk),
        in_specs=[pl.BlockSpec((tm, tk), lambda i, j, kk: (i, kk)),
                  pl.BlockSpec((tk, tn), lambda i, j, kk: (kk, j)),
                  pl.BlockSpec((tk, tn), lambda i, j, kk: (kk, j + nf))],
        out_specs=[o_spec, o_spec, o_spec], out_shape=[shp, shp, shp],
        scratch_shapes=_k_scratch(2, (tm, tn), nk), compiler_params=_params(3))(h, wgu, wgu)


def _mm_dswiglu(dy, wd, gate, up, *, scale, name):
    M, K = dy.shape
    F = wd.shape[0]
    tm, tn, tk = _tile(M, 512), _tile(F, MM_TILE_CAP), _tile(K, 1024)
    nm, nf, nk = M // tm, F // tn, K // tk

    def body(dy_ref, wd_ref, g_ref, u_ref, dg_ref, du_ref, *acc_refs):
        def finish(totals):
            da = totals[0] * scale
            g, u = g_ref[...].astype(F32), u_ref[...].astype(F32)
            sg = jax.nn.sigmoid(g)
            du_ref[...] = (da * g * sg).astype(BF16)
            dg_ref[...] = (da * u * sg * (1.0 + g * (1.0 - sg))).astype(BF16)

        _over_k_steps([_nt(dy_ref[...].astype(BF16), wd_ref[...])], acc_refs, nk, finish)

    o_spec = pl.BlockSpec((tm, tn), lambda i, j, kk: (i, j))
    shp = jax.ShapeDtypeStruct((M, F), BF16)
    return pl.pallas_call(
        body, name=name, grid=(nm, nf, nk),
        in_specs=[pl.BlockSpec((tm, tk), lambda i, j, kk: (i, kk)),
                  pl.BlockSpec((tn, tk), lambda i, j, kk: (j, kk)), o_spec, o_spec],
        out_specs=[o_spec, o_spec], out_shape=[shp, shp],
        scratch_shapes=_k_scratch(1, (tm, tn), nk), compiler_params=_params(3))(dy, wd, gate, up)


HEAD_ROWS = 256


def _row_tile(rows, cap):
    t = cap
    while t >= 8:
        if rows % t == 0:
            return t
        t //= 2
    return rows


def _rowwise(fn, rows, consts, outs, sums=(), hsums=(), *, heads=None, tm=256, name):
    rows = [r if isinstance(r, tuple) else (r, r.shape[1], None) for r in rows]
    S = rows[0][0].shape[0]
    tm = _row_tile(S, tm)
    nh = heads or 1
    n_r, n_c, n_o, n_h, n_s = len(rows), len(consts), len(outs), len(hsums), len(sums)

    def body(*refs):
        c = [x[...] for x in refs[n_r:n_r + n_c]]
        o_refs = refs[n_r + n_c:n_r + n_c + n_o]
        h_refs = refs[n_r + n_c + n_o:n_r + n_c + n_o + n_h]
        s_refs = refs[n_r + n_c + n_o + n_h:]
        first = pl.program_id(0) == 0
        over_heads = [None] * (n_h + n_s)
        for h in range(nh):
            r = [ref[...] if cf is None else ref[:, cf(h) * w:(cf(h) + 1) * w] for ref, (_, w, cf) in zip(refs, rows)]
            res = fn(*r, *c)
            res = res if isinstance(res, (tuple, list)) else (res,)
            for ref, val, (w, _) in zip(o_refs, res[:n_o], outs):
                ref[:, h * (w // nh):(h + 1) * (w // nh)] = val.astype(ref.dtype)
            over_heads = [val if acc is None else acc + val for acc, val in zip(over_heads, res[n_o:])]
        for ref, val in zip(h_refs, over_heads[:n_h]):
            ref[...] = val
        if n_s:
            @pl.when(first)
            def _():
                for ref in s_refs:
                    ref[...] = jnp.zeros_like(ref)
            for ref, val in zip(s_refs, over_heads[n_h:]):
                ref[...] += val

    in_specs = [pl.BlockSpec((tm, a.shape[1]), lambda i: (i, 0)) for a, _, _ in rows]
    in_specs += [pl.BlockSpec(a.shape, lambda i, nd=a.ndim: (0,) * nd) for a in consts]
    out_specs = [pl.BlockSpec((tm, w), lambda i: (i, 0)) for w, _ in outs]
    out_specs += [pl.BlockSpec((tm, w), lambda i: (i, 0)) for w in hsums]
    out_specs += [pl.BlockSpec(sh, lambda i, nd=len(sh): (0,) * nd) for sh in sums]
    out_shape = [jax.ShapeDtypeStruct((S, w), dt) for w, dt in outs]
    out_shape += [jax.ShapeDtypeStruct((S, w), F32) for w in hsums]
    out_shape += [jax.ShapeDtypeStruct(sh, F32) for sh in sums]
    return pl.pallas_call(body, name=name, grid=(S // tm,), in_specs=in_specs, out_specs=out_specs,
                          out_shape=out_shape, compiler_params=_params(1))(*[a for a, _, _ in rows], *consts)


def _rms(x, width=None):
    width = width or x.shape[-1]
    return lax.rsqrt(jnp.sum(x * x, axis=-1, keepdims=True) * (1.0 / width) + EPS)


def _rmsnorm_fwd(x, g, width=None):
    return x * _rms(x, width) * g


def _rmsnorm_bwd(dy, x, g, width=None):
    width = width or x.shape[-1]
    r = _rms(x, width)
    xn = x * r
    dxn = dy * g
    dx = r * (dxn - xn * (jnp.sum(dxn * xn, axis=-1, keepdims=True) * (1.0 / width)))
    return dx, jnp.sum(dy * xn, axis=0, keepdims=True)


def _norm_rows(x, g, *, name, out_dtype=BF16):
    D = x.shape[1]
    return _rowwise(lambda xv, gv: _rmsnorm_fwd(xv.astype(F32), gv), [x], [g.reshape(1, D)], [(D, out_dtype)],
                    name=name)[0]


def _norm_rows_bwd(dh, x, g, dres, *, name):
    D = x.shape[1]

    def fn(dhv, xv, *rest):
        dx, dg = _rmsnorm_bwd(dhv.astype(F32), xv, rest[-1])
        return (dx + rest[0] if dres is not None else dx), dg

    rows = [dh, x] + ([dres] if dres is not None else [])
    return _rowwise(fn, rows, [g.reshape(1, D)], [(D, F32)], [(1, D)], name=name)


def _softplus(z):
    return jnp.where(z > 20.0, z, jnp.log(1.0 + jnp.exp(z)))


def _running_sum(v, u, split=True):
    if not split:
        return jnp.dot(v.astype(BF16), u, preferred_element_type=F32)
    hi = lax.bitcast_convert_type(lax.bitcast_convert_type(v, jnp.uint32) & jnp.uint32(0xFFFF0000), F32)
    return (jnp.dot(hi.astype(BF16), u, preferred_element_type=F32)
            + jnp.dot((v - hi).astype(BF16), u, preferred_element_type=F32))


def _triangle(tk, inclusive_prefix):
    j, s = lax.broadcasted_iota(jnp.int32, (tk, tk), 0), lax.broadcasted_iota(jnp.int32, (tk, tk), 1)
    return ((j <= s) if inclusive_prefix else (j > s)).astype(BF16)


def _nt(a, b):
    return lax.dot_general(a, b, (((1,), (1,)), ((), ())), preferred_element_type=F32)


def _tn(a, b):
    return lax.dot_general(a, b, (((0,), (0,)), ((), ())), preferred_element_type=F32)


ATT_TQ, ATT_TK = 512, 512
SB_SUB = 256
FWD_GROUP = 2


def _attn_fwd(q, k, v, *, sb, causal, heads, dq, dv, group=1, kcol=None, vcol=None, sum_lane=None, name):
    S, Sk = q.shape[0], k.shape[0]
    tq, tk = min(ATT_TQ, S), min(ATT_TK, Sk)
    sub = min(SB_SUB, tk) if sb else tk
    assert tq % sub == 0 or not causal
    kcol = kcol or (lambda h: h)
    vcol = vcol or (lambda h: h)
    members = range(group)

    def body(*refs):
        if sb:
            q_ref, k_ref, v_ref, u_ref, o_ref, lse_ref, acc_ref, r_ref = refs
            r_ref[...] = jnp.zeros_like(r_ref)
        else:
            q_ref, k_ref, v_ref, o_ref, lse_ref, acc_ref, m_ref, l_ref = refs
            m_ref[...] = jnp.full_like(m_ref, -1e30)
            l_ref[...] = jnp.zeros_like(l_ref)
        first_row = pl.program_id(1) * tq
        qb = [q_ref[:, hh * dq:(hh + 1) * dq] for hh in members]
        acc_ref[...] = jnp.zeros_like(acc_ref)
        nblk = (first_row + tq) // sub if causal else Sk // sub
        nfull = (first_row + (0 if sb else 1)) // sub if causal else nblk
        n_cut = tq // sub if causal else 0

        def scores(jj):
            off = pl.multiple_of(jj * sub, sub)
            return tuple(_nt(qb[hh], k_ref[pl.ds(off, sub), hh * dq:(hh + 1) * dq]) for hh in members)

        def weigh(jj, scores_now, masked):
            off = pl.multiple_of(jj * sub, sub)
            if masked:
                kpos = off + lax.broadcasted_iota(jnp.int32, (tq, sub), 1)
                qpos = first_row + lax.broadcasted_iota(jnp.int32, (tq, sub), 0)
                valid = (kpos < qpos) if sb else (kpos <= qpos)
            for hh in members:
                vb = v_ref[pl.ds(off, sub), hh * dv:(hh + 1) * dv]
                s = scores_now[hh]
                if sb:
                    sp = _softplus(s)
                    ls = jnp.where(valid, -sp, 0.0) if masked else -sp
                    w = jnp.exp(s - sp + r_ref[hh] + _running_sum(ls, u_ref[...]))
                    if masked:
                        w = jnp.where(valid, w, 0.0)
                    acc_ref[hh] += jnp.dot(w.astype(BF16), vb, preferred_element_type=F32)
                    r_ref[hh] += jnp.sum(ls, axis=1, keepdims=True)
                else:
                    if masked:
                        s = jnp.where(valid, s, -1e30)
                    m_old = m_ref[hh]
                    m_new = jnp.maximum(m_old, jnp.max(s, axis=1, keepdims=True))
                    p = jnp.exp(s - m_new)
                    alpha = jnp.exp(m_old - m_new)
                    if sum_lane is None:
                        l_ref[hh] = alpha * l_ref[hh] + jnp.sum(p, axis=1, keepdims=True)
                    acc_ref[hh] = alpha * acc_ref[hh] + jnp.dot(p.astype(BF16), vb, preferred_element_type=F32)
                    m_ref[hh] = m_new

        if sb:
            s_cur = scores(nblk - 1)
            for cut in range(n_cut):
                s_next = scores(jnp.maximum(nblk - 2 - cut, 0))
                weigh(nblk - 1 - cut, s_cur, True)
                s_cur = s_next

            def step(t, s_now):
                s_next = scores(jnp.maximum(nfull - 2 - t, 0))
                weigh(nfull - 1 - t, s_now, False)
                return s_next

            lax.fori_loop(0, nfull, step, s_cur)
        else:
            n_loop = nfull if causal else nblk - 1

            def step(t, s_now):
                s_next = scores(jnp.minimum(t + 1, nblk - 1))
                weigh(t, s_now, False)
                return s_next

            s_cur = lax.fori_loop(0, n_loop, step, scores(0))
            tail = n_cut if causal else 1
            for last in range(tail):
                s_next = scores(n_loop + last + 1) if last + 1 < tail else None
                weigh(n_loop + last, s_cur, causal)
                s_cur = s_next
        for hh in members:
            cols = slice(hh * dv, (hh + 1) * dv)
            if sb:
                o_ref[:, cols] = acc_ref[hh]
                lse_ref[hh] = r_ref[hh]
            else:
                acc = acc_ref[hh]
                l = l_ref[hh] if sum_lane is None else acc[:, sum_lane:sum_lane + 1]
                o_ref[:, cols] = acc / l
                lse_ref[hh] = m_ref[hh] + jnp.log(l)

    in_specs = [pl.BlockSpec((tq, group * dq), lambda g, i: (i, g)),
                pl.BlockSpec((Sk, group * dq), lambda g, i: (0, kcol(g))),
                pl.BlockSpec((Sk, group * dv), lambda g, i: (0, vcol(g)))]
    ins = [q, k, v]
    scratch = [pltpu.VMEM((group, tq, dv), F32), pltpu.VMEM((group, tq, 1), F32)]
    if sb:
        ins.append(_triangle(sub, inclusive_prefix=False))
        in_specs.append(pl.BlockSpec((sub, sub), lambda g, i: (0, 0)))
    else:
        scratch.append(pltpu.VMEM((group, tq, 1), F32))
    out_specs = [pl.BlockSpec((tq, group * dv), lambda g, i: (i, g)), pl.BlockSpec((group, tq, 1), lambda g, i: (g, i, 0))]
    out_shape = [jax.ShapeDtypeStruct((S, heads * dv), F32), jax.ShapeDtypeStruct((heads, S, 1), F32)]
    return pl.pallas_call(body, name=name, grid=(heads // group, S // tq), in_specs=in_specs, out_specs=out_specs,
                          out_shape=out_shape, scratch_shapes=scratch, compiler_params=_params(2))(*ins)


def _attn_bwd(q, k, v, o, do, lse, *, sb, causal, heads, dq, dv, kcol=None, vcol=None, name):
    S, Sk = q.shape[0], k.shape[0]
    tq, tk = min(ATT_TQ, S), min(ATT_TK, Sk)
    sub = min(SB_SUB, tk) if sb else tk
    assert tq % sub == 0 or not causal
    nq = S // tq
    kcol = kcol or (lambda h: h)
    vcol = vcol or (lambda h: h)

    def body(*refs):
        if sb:
            q_ref, k_ref, v_ref, o_ref, do_ref, lse_ref, u_ref, dq_ref, dk_ref, dv_ref, acc_ref, r_ref, re_ref = refs
            r_ref[...] = jnp.zeros_like(r_ref)
            re_ref[...] = jnp.zeros_like(re_ref)
        else:
            q_ref, k_ref, v_ref, o_ref, do_ref, lse_ref, dq_ref, dk_ref, dv_ref, acc_ref = refs
        first_row = pl.program_id(1) * tq

        @pl.when(first_row == 0)
        def _():
            dk_ref[...] = jnp.zeros_like(dk_ref)
            dv_ref[...] = jnp.zeros_like(dv_ref)

        qb = q_ref[...]
        dof = do_ref[...].astype(F32)
        dob = dof.astype(BF16)
        if not sb:
            dlt = jnp.sum(dof * o_ref[...], axis=1, keepdims=True)
        acc_ref[...] = jnp.zeros_like(acc_ref)
        nblk = (first_row + tq) // sub if causal else Sk // sub
        nfull = (first_row + (0 if sb else 1)) // sub if causal else nblk
        n_cut = tq // sub if causal else 0

        def products(jj):
            off = pl.multiple_of(jj * sub, sub)
            return _nt(qb, k_ref[pl.ds(off, sub), :]), _nt(dob, v_ref[pl.ds(off, sub), :])

        def piece(jj, now, masked):
            off = pl.multiple_of(jj * sub, sub)
            kb = k_ref[pl.ds(off, sub), :]
            s, dp = now
            if masked:
                qpos = first_row + lax.broadcasted_iota(jnp.int32, (tq, sub), 0)
                kpos = off + lax.broadcasted_iota(jnp.int32, (tq, sub), 1)
                valid = (kpos < qpos) if sb else (kpos <= qpos)
            if sb:
                u = u_ref[...]
                sp = _softplus(s)
                ls = jnp.where(valid, -sp, 0.0) if masked else -sp
                lb = s - sp
                w = jnp.exp(lb + (lse_ref[0] - (r_ref[...] + _running_sum(ls, u))))
                if masked:
                    w = jnp.where(valid, w, 0.0)
                e = dp * w
                ds = e - jnp.exp(lb) * (re_ref[...] + _running_sum(e, u, split=False))
                if masked:
                    ds = jnp.where(valid, ds, 0.0)
                r_ref[...] += jnp.sum(ls, axis=1, keepdims=True)
                re_ref[...] += jnp.sum(e, axis=1, keepdims=True)
            else:
                w = jnp.exp(s - lse_ref[0])
                if masked:
                    w = jnp.where(valid, w, 0.0)
                ds = w * (dp - dlt)
            dsb = ds.astype(BF16)
            dv_ref[pl.ds(off, sub), :] += _tn(w.astype(BF16), dob)
            dk_ref[pl.ds(off, sub), :] += _tn(dsb, qb)
            acc_ref[...] += jnp.dot(dsb, kb, preferred_element_type=F32)

        n_loop = nfull if causal else nblk - 1
        per_trip = tk // sub

        def steps(first, count, masked):
            ready = [products(first + c) for c in range(count)]
            for c in range(count):
                piece(first + c, ready[c], masked)

        def trip(t, carry):
            steps(t * per_trip, per_trip, False)
            return carry

        lax.fori_loop(0, n_loop // per_trip, trip, 0)
        steps(n_loop, n_cut if causal else 1, causal)
        dq_ref[...] = acc_ref[...]

    ins = [q, k, v, o, do]
    in_specs = [pl.BlockSpec((tq, dq), lambda h, i: (i, h)),
                pl.BlockSpec((Sk, dq), lambda h, i: (0, kcol(h))),
                pl.BlockSpec((Sk, dv), lambda h, i: (0, vcol(h))),
                pl.BlockSpec((tq, dv), lambda h, i: (i, h)),
                pl.BlockSpec((tq, dv), lambda h, i: (i, h))]
    scratch = [pltpu.VMEM((tq, dq), F32)]
    ins.append(lse)
    in_specs.append(pl.BlockSpec((1, tq, 1), lambda h, i: (h, i, 0)))
    if sb:
        ins.append(_triangle(sub, inclusive_prefix=True))
        in_specs.append(pl.BlockSpec((sub, sub), lambda h, i: (0, 0)))
        scratch += [pltpu.VMEM((tq, 1), F32), pltpu.VMEM((tq, 1), F32)]
    out_specs = [pl.BlockSpec((tq, dq), lambda h, i: (i, h)),
                 pl.BlockSpec((Sk, dq), lambda h, i: (0, h)),
                 pl.BlockSpec((Sk, dv), lambda h, i: (0, h))]
    out_shape = [jax.ShapeDtypeStruct((S, heads * dq), F32), jax.ShapeDtypeStruct((Sk, heads * dq), F32),
                 jax.ShapeDtypeStruct((Sk, heads * dv), F32)]
    return pl.pallas_call(body, name=name, grid=(heads, nq), in_specs=in_specs, out_specs=out_specs,
                          out_shape=out_shape, scratch_shapes=scratch, compiler_params=_params(2))(*ins)


GELU_C = 0.7978845608028654
assert 2 * SG_GD == LANE and SG_CHUNK == LANE


def _gelu(z):
    t = jnp.tanh(GELU_C * (z + 0.044715 * z * z * z))
    return 0.5 * z * (1.0 + t), t


def _gelu_grad(z, t):
    return 0.5 * (1.0 + t) + 0.5 * z * (1.0 - t * t) * GELU_C * (1.0 + 3.0 * 0.044715 * z * z)


def _layernorm_parts(g):
    d = g - jnp.mean(g, axis=-1, keepdims=True)
    rstd = lax.rsqrt(jnp.mean(d * d, axis=-1, keepdims=True) + EPS)
    return d * rstd, rstd


def _gelu_ln(z, gain, bias, *, name):
    def fn(zv, gn, bs):
        a, _ = _gelu(zv)
        y, _ = _layernorm_parts(a[:, SG_W:])
        return a[:, :SG_W], y * gn + bs

    return _rowwise(fn, [z], [gain.reshape(1, SG_W), bias.reshape(1, SG_W)], [(SG_W, F32), (SG_W, BF16)], name=name)


def _gelu_ln_bwd(z, du, dgl, gain, *, name):
    def fn(zv, duv, dglv, gn):
        a, t = _gelu(zv)
        y, rstd = _layernorm_parts(a[:, SG_W:])
        dy = dglv * gn
        dgg = rstd * (dy - jnp.mean(dy, axis=-1, keepdims=True) - y * jnp.mean(dy * y, axis=-1, keepdims=True))
        dz = jnp.concatenate([duv, dgg], axis=1) * _gelu_grad(zv, t)
        return dz, jnp.sum(dglv * y, axis=0, keepdims=True), jnp.sum(dglv, axis=0, keepdims=True)

    return _rowwise(fn, [z, du, dgl], [gain.reshape(1, SG_W)], [(2 * SG_W, BF16)], [(1, SG_W), (1, SG_W)], name=name)


def _sg_masks():
    tri = lax.broadcasted_iota(jnp.int32, (SG_CHUNK, SG_CHUNK), 0) >= lax.broadcasted_iota(jnp.int32, (SG_CHUNK, SG_CHUNK), 1)
    first = lax.broadcasted_iota(jnp.int32, (SG_CHUNK, LANE), 1) < SG_GD
    return tri, first


def _spatial(gl, u, w, bt, *, name):
    S = gl.shape[0]
    tm = _row_tile(S, 512)
    nch = tm // SG_CHUNK

    def body(gl_ref, u_ref, w_ref, bt_ref, o_ref):
        tri, first = _sg_masks()
        for p in range(SG_W // LANE):
            cols = slice(p * LANE, (p + 1) * LANE)
            wa = jnp.where(tri, w_ref[2 * p], 0.0).astype(BF16)
            wb = jnp.where(tri, w_ref[2 * p + 1], 0.0).astype(BF16)
            for ci in range(nch):
                rws = slice(ci * SG_CHUNK, (ci + 1) * SG_CHUNK)
                g = gl_ref[rws, cols]
                zero = jnp.zeros_like(g)
                mixed = (jnp.dot(wa, jnp.where(first, g, zero), preferred_element_type=F32)
                         + jnp.dot(wb, jnp.where(first, zero, g), preferred_element_type=F32) + bt_ref[:, cols])
                o_ref[rws, cols] = u_ref[rws, cols] * mixed

    row = pl.BlockSpec((tm, SG_W), lambda i: (i, 0))
    return pl.pallas_call(
        body, name=name, grid=(S // tm,),
        in_specs=[row, row, pl.BlockSpec(w.shape, lambda i: (0, 0, 0)), pl.BlockSpec(bt.shape, lambda i: (0, 0))],
        out_specs=row, out_shape=jax.ShapeDtypeStruct((S, SG_W), F32), compiler_params=_params(1))(gl, u, w, bt)


def _spatial_bwd(d_o, gl, u, w, bt, *, name):
    S = gl.shape[0]
    tm = _row_tile(S, 512)
    nch = tm // SG_CHUNK
    nsteps = S // tm

    def body(do_ref, gl_ref, u_ref, w_ref, bt_ref, du_ref, dgl_ref, dw_ref, db_ref, dbt_ref):
        tri, first = _sg_masks()
        step = pl.program_id(0)

        @pl.when(step == 0)
        def _():
            dw_ref[...] = jnp.zeros_like(dw_ref)
            dbt_ref[...] = jnp.zeros_like(dbt_ref)

        for p in range(SG_W // LANE):
            cols = slice(p * LANE, (p + 1) * LANE)
            wa = jnp.where(tri, w_ref[2 * p], 0.0).astype(BF16)
            wb = jnp.where(tri, w_ref[2 * p + 1], 0.0).astype(BF16)
            for ci in range(nch):
                rws = slice(ci * SG_CHUNK, (ci + 1) * SG_CHUNK)
                g = gl_ref[rws, cols]
                zero = jnp.zeros_like(g)
                mixed = (jnp.dot(wa, jnp.where(first, g, zero), preferred_element_type=F32)
                         + jnp.dot(wb, jnp.where(first, zero, g), preferred_element_type=F32) + bt_ref[:, cols])
                dov = do_ref[rws, cols]
                du_ref[rws, cols] = dov * mixed
                dm = dov * u_ref[rws, cols]
                dbt_ref[:, cols] += dm
                dma = jnp.where(first, dm, 0.0).astype(BF16)
                dmb = jnp.where(first, 0.0, dm).astype(BF16)
                dw_ref[2 * p] += jnp.where(tri, _nt(dma, g), 0.0)
                dw_ref[2 * p + 1] += jnp.where(tri, _nt(dmb, g), 0.0)
                dgl_ref[rws, cols] = _tn(wa, dma) + _tn(wb, dmb)

        @pl.when(step == nsteps - 1)
        def _():
            lane = lax.broadcasted_iota(jnp.int32, (SG_CHUNK, LANE), 1)
            acc = jnp.zeros((SG_CHUNK, LANE), F32)
            for p in range(SG_W // LANE):
                blk = dbt_ref[:, p * LANE:(p + 1) * LANE]
                sa = jnp.sum(jnp.where(first, blk, 0.0), axis=1, keepdims=True)
                sb_ = jnp.sum(jnp.where(first, 0.0, blk), axis=1, keepdims=True)
                acc = acc + jnp.where(lane == 2 * p, sa, 0.0) + jnp.where(lane == 2 * p + 1, sb_, 0.0)
            db_ref[...] = acc

    row = pl.BlockSpec((tm, SG_W), lambda i: (i, 0))
    return pl.pallas_call(
        body, name=name, grid=(nsteps,),
        in_specs=[row, row, row, pl.BlockSpec(w.shape, lambda i: (0, 0, 0)), pl.BlockSpec(bt.shape, lambda i: (0, 0))],
        out_specs=[row, row, pl.BlockSpec(w.shape, lambda i: (0, 0, 0)), pl.BlockSpec((SG_CHUNK, LANE), lambda i: (0, 0))],
        out_shape=[jax.ShapeDtypeStruct((S, SG_W), F32), jax.ShapeDtypeStruct((S, SG_W), F32),
                   jax.ShapeDtypeStruct(w.shape, F32), jax.ShapeDtypeStruct((SG_CHUNK, LANE), F32)],
        scratch_shapes=[pltpu.VMEM((SG_CHUNK, SG_W), F32)], compiler_params=_params(1))(d_o, gl, u, w, bt)


ROPE_HALF = MLA_ROPE // 2
KR_COL = (MLA_QL + MLA_KVL) // LANE
MLA_IN_PAD = MLA_QL + MLA_KVL + LANE


def _rope_tables(positions):
    inv_freq = ROPE_THETA ** (-jnp.arange(ROPE_HALF, dtype=F32) / ROPE_HALF)
    ang = positions.astype(F32)[:, None] * inv_freq
    cos, sin = jnp.cos(ang), jnp.sin(ang)
    S = positions.shape[0]
    z16, tail = jnp.zeros((S, ROPE_HALF), F32), jnp.zeros((S, LANE - MLA_QK), F32)
    ones = jnp.ones((S, MLA_NOPE), F32)
    zeros = jnp.zeros((S, MLA_NOPE), F32)
    return (jnp.concatenate([ones, cos, cos, tail], axis=1), jnp.concatenate([zeros, z16, sin, tail], axis=1),
            jnp.concatenate([zeros, -sin, z16, tail], axis=1))


def _rope(x, cos, sa, sb):
    return x * cos + pltpu.roll(x, ROPE_HALF, 1) * sa + pltpu.roll(x, LANE - ROPE_HALF, 1) * sb


def _rope_t(dy, cos, sa, sb):
    return dy * cos + pltpu.roll(dy * sa, LANE - ROPE_HALF, 1) + pltpu.roll(dy * sb, ROPE_HALF, 1)


def _mla_lora(P, qlg, kvlg, *, name):
    def fn(pv, a, b):
        return _rmsnorm_fwd(pv[:, :MLA_QL], a), _rmsnorm_fwd(pv[:, MLA_QL:MLA_QL + MLA_KVL], b)

    return _rowwise(fn, [P], [qlg.reshape(1, MLA_QL), kvlg.reshape(1, MLA_KVL)], [(MLA_QL, BF16), (MLA_KVL, BF16)], name=name)


def _mla_lora_bwd(dcq, dckv, dkr, P, qlg, kvlg, *, name):
    def fn(d1, d2, d3, pv, a, b):
        x1, g1 = _rmsnorm_bwd(d1, pv[:, :MLA_QL], a)
        x2, g2 = _rmsnorm_bwd(d2, pv[:, MLA_QL:MLA_QL + MLA_KVL], b)
        return jnp.concatenate([x1, x2, d3], axis=1), g1, g2

    return _rowwise(fn, [dcq, dckv, dkr, P], [qlg.reshape(1, MLA_QL), kvlg.reshape(1, MLA_KVL)], [(MLA_IN_PAD, BF16)],
                    [(1, MLA_QL), (1, MLA_KVL)], name=name)


def _mla_qk(q_pre, k_pre, P, tabs, qg, kg, *, name):
    def fn(qp, kp, kr, c, a, b, qgv, kgv):
        return (_rope(_rmsnorm_fwd(qp, qgv, MLA_QK), c, a, b) * MLA_SCALE,
                _rope(_rmsnorm_fwd(kp + kr, kgv, MLA_QK), c, a, b))

    hcol = lambda h: h
    rows = [(q_pre, LANE, hcol), (k_pre, LANE, hcol), (P, LANE, lambda h: KR_COL), *tabs]
    w = MLA_HEADS * LANE
    return _rowwise(fn, rows, [qg, kg], [(w, BF16), (w, BF16)], heads=MLA_HEADS, tm=HEAD_ROWS, name=name)


def _mla_qk_bwd(dq, dk, q_pre, k_pre, P, tabs, qg, kg, *, name):
    def fn(dqv, dkv, qp, kp, kr, c, a, b, qgv, kgv):
        dqp, dqg = _rmsnorm_bwd(_rope_t(dqv * MLA_SCALE, c, a, b), qp, qgv, MLA_QK)
        dkp, dkg = _rmsnorm_bwd(_rope_t(dkv, c, a, b), kp + kr, kgv, MLA_QK)
        lane = lax.broadcasted_iota(jnp.int32, (1, LANE), 1)
        return dqp, dkp, jnp.where((lane >= MLA_NOPE) & (lane < MLA_QK), dkp, 0.0), dqg, dkg

    hcol = lambda h: h
    rows = [(dq, LANE, hcol), (dk, LANE, hcol), (q_pre, LANE, hcol), (k_pre, LANE, hcol), (P, LANE, lambda h: KR_COL), *tabs]
    w = MLA_HEADS * LANE
    return _rowwise(fn, rows, [qg, kg], [(w, BF16), (w, BF16)], [(1, LANE), (1, LANE)], [LANE], heads=MLA_HEADS,
                    tm=HEAD_ROWS, name=name)


def _head_norm(x, g, *, heads, width, colfn=None, scale=1.0, name):
    return _rowwise(lambda xv, gv: _rmsnorm_fwd(xv, gv) * scale, [(x, width, colfn or (lambda h: h))],
                    [g.reshape(1, width)], [(heads * width, BF16)], heads=heads, tm=HEAD_ROWS, name=name)[0]


def _head_norm_bwd(dy, x, g, *, heads, width, colfn=None, scale=1.0, out_dtype, name):
    return _rowwise(lambda dv_, xv, gv: _rmsnorm_bwd(dv_ * scale, xv, gv),
                    [(dy, width, lambda h: h), (x, width, colfn or (lambda h: h))],
                    [g.reshape(1, width)], [(heads * width, out_dtype)], [(1, width)], heads=heads, tm=HEAD_ROWS,
                    name=name)


def _loss_grad(y, tgt, *, name):
    D = y.shape[1]

    def fn(yv, tv):
        d = yv - tv
        return d * (1.0 / D), jnp.sum(d * d, axis=0, keepdims=True) * (0.5 / D)

    dy, part = _rowwise(fn, [y, tgt], [], [(D, F32)], [(1, D)], name=name)
    return jnp.sum(part), dy


def _adamw(w, g, m, v, *, name):
    shape = w.shape
    two_d = (-1, shape[-1])

    def fn(wv, gv, mv, vv):
        m2 = ADAM_B1 * mv + (1.0 - ADAM_B1) * gv
        v2 = ADAM_B2 * vv + (1.0 - ADAM_B2) * (gv * gv)
        m_hat = m2 / (1.0 - ADAM_B1 ** ADAM_STEP)
        v_hat = v2 / (1.0 - ADAM_B2 ** ADAM_STEP)
        return -ADAM_LR * (m_hat / (jnp.sqrt(v_hat) + ADAM_EPS) + ADAM_WD * wv), m2, v2

    outs = _rowwise(fn, [t.reshape(two_d) for t in (w, g, m, v)], [], [(shape[-1], F32)] * 3, name=name)
    return [o.reshape(shape) for o in outs]


def _pad_cols(w, heads, hd):
    k = w.shape[0]
    return jnp.pad(w.reshape(k, heads, hd), ((0, 0), (0, 0), (0, LANE - hd))).reshape(k, heads * LANE)


def _unpad_cols(w, heads, hd):
    k = w.shape[0]
    return w.reshape(k, heads, LANE)[:, :, :hd].reshape(k, heads * hd)


def _pad_rows(w, heads, hd):
    n = w.shape[1]
    return jnp.pad(w.reshape(heads, hd, n), ((0, 0), (0, LANE - hd), (0, 0))).reshape(heads * LANE, n)


def _unpad_rows(w, heads, hd):
    n = w.shape[1]
    return w.reshape(heads, LANE, n)[:, :hd, :].reshape(heads * hd, n)


def _ffn_fwd(x, g, wgu, wd, tag):
    h = _norm_rows(x, g, name=tag + "_norm")
    gate, up, act = _mm_swiglu(h, wgu, name=tag + "_gu")
    y = _mm(act, wd, scale=0.5, residual=x, name=tag + "_down")
    return y, (x, h, gate, up, act)


def _ffn_bwd(dy, saved, g, wgu, wd, tag):
    x, h, gate, up, act = saved
    F = wd.shape[0]
    dwd = _mm(act, dy, ta=True, scale=0.5, name=tag + "_dwd")
    dgate, dup = _mm_dswiglu(dy, wd, gate, up, scale=0.5, name=tag + "_dact")
    dh = _mm(dgate, wgu, tb=True, name=tag + "_dh_g")
    dx, dg = _mm(dup, wgu, tb=True, b_off=(0, F), residual=dh, norm_bwd=(x, g, dy), name=tag + "_dh_u")
    dwgu = jnp.concatenate([_mm(h, dgate, ta=True, name=tag + "_dwg"), _mm(h, dup, ta=True, name=tag + "_dwu")], axis=1)
    return dx, dg, dwgu, dwd


def _even_weights(w_in, w_out):
    parts = [w_in[:, :SB_W] * SB_SCALE, w_in[:, SB_W:2 * SB_W], w_in[:, 2 * SB_W:3 * SB_W]]
    wqkv = jnp.concatenate([_pad_cols(p, SB_HEADS, SB_HD) for p in parts], axis=1)
    return wqkv, w_in[:, 3 * SB_W:], _pad_rows(w_out[:SB_W], SB_HEADS, SB_HD), w_out[SB_W:]


def _even_fwd(x, g, wts, ln_g, ln_b, sgu_w, bt, tag):
    wqkv, wz, wo_sb, wo_sg = wts
    h = _norm_rows(x, g, name=tag + "_norm")
    qkv = _mm(h, wqkv, out_dtype=BF16, name=tag + "_qkv")
    z = _mm(h, wz, name=tag + "_z")
    o_sb, tot = _attn_fwd(qkv, qkv, qkv, sb=True, causal=True, heads=SB_HEADS, dq=LANE, dv=LANE, group=FWD_GROUP,
                          kcol=lambda g: SB_HEADS // FWD_GROUP + g, vcol=lambda g: 2 * SB_HEADS // FWD_GROUP + g,
                          name=tag + "_sb")
    u, gl = _gelu_ln(z, ln_g, ln_b, name=tag + "_geluln")
    o_sg = _spatial(gl, u, sgu_w, bt, name=tag + "_sgu")
    y = _mm(o_sb, wo_sb, residual=x, name=tag + "_out_sb")
    y = _mm(o_sg, wo_sg, residual=y, name=tag + "_out_sg")
    return y, (x, h, qkv, z, o_sb, tot, u, gl, o_sg)


def _even_bwd(dy, saved, g, wts, ln_g, sgu_w, bt, tag):
    wqkv, wz, wo_sb, wo_sg = wts
    x, h, qkv, z, o_sb, tot, u, gl, o_sg = saved
    do_sb = _mm(dy, wo_sb, tb=True, name=tag + "_do_sb")
    do_sg = _mm(dy, wo_sg, tb=True, name=tag + "_do_sg")
    dwo = jnp.concatenate([_unpad_rows(_mm(o_sb, dy, ta=True, name=tag + "_dwo_sb"), SB_HEADS, SB_HD),
                           _mm(o_sg, dy, ta=True, name=tag + "_dwo_sg")], axis=0)
    dq, dk, dv = _attn_bwd(qkv, qkv, qkv, o_sb, do_sb, tot, sb=True, causal=True, heads=SB_HEADS, dq=LANE, dv=LANE,
                           kcol=lambda hh: SB_HEADS + hh, vcol=lambda hh: 2 * SB_HEADS + hh, name=tag + "_sb_bwd")
    du, dgl, dsgu_w, db_t = _spatial_bwd(do_sg, gl, u, sgu_w, bt, name=tag + "_sgu_bwd")
    dz, dln_g, dln_b = _gelu_ln_bwd(z, du, dgl, ln_g, name=tag + "_geluln_bwd")
    dh = _mm(dz, wz, tb=True, name=tag + "_dh_z")
    dws = []
    for i, (d, nm) in enumerate(((dq, "q"), (dk, "k"), (dv, "v"))):
        dh = _mm(d, wqkv, tb=True, b_off=(0, i * SB_HEADS * LANE), residual=dh,
                 norm_bwd=(x, g, dy) if nm == "v" else None, name=tag + "_dh_" + nm)
        dws.append(_unpad_cols(_mm(h, d, ta=True, scale=SB_SCALE if nm == "q" else 1.0, name=tag + "_dw_" + nm),
                               SB_HEADS, SB_HD))
    dws.append(_mm(h, dz, ta=True, name=tag + "_dw_z"))
    dx, dg = dh
    return dx, dict(mix_norm=dg, sbg_w_in=jnp.concatenate(dws, axis=1), sgu_ln_gain=dln_g, sgu_ln_bias=dln_b,
                    sgu_w=dsgu_w, sgu_b=db_t[:, :SG_GROUPS].T, sbg_w_out=dwo)


def _mla_weights(w_in, w_uq, w_ukv, w_out, q_gain, k_gain):
    d = w_in.shape[0]
    lat = MLA_QL + MLA_KVL
    w_in_ext = jnp.concatenate([w_in[:, :lat], jnp.zeros((d, MLA_NOPE), w_in.dtype), w_in[:, lat:],
                                jnp.zeros((d, LANE - MLA_QK), w_in.dtype)], axis=1)
    kv = w_ukv.reshape(MLA_KVL, MLA_HEADS, MLA_NOPE + MLA_V)
    wk = _pad_cols(kv[:, :, :MLA_NOPE].reshape(MLA_KVL, -1), MLA_HEADS, MLA_NOPE)
    wv = _pad_cols(kv[:, :, MLA_NOPE:].reshape(MLA_KVL, -1), MLA_HEADS, MLA_V)
    pad_gain = lambda gn: jnp.pad(gn.reshape(1, MLA_QK), ((0, 0), (0, LANE - MLA_QK)))
    return (w_in_ext, _pad_cols(w_uq, MLA_HEADS, MLA_QK), wk, wv, _pad_rows(w_out, MLA_HEADS, MLA_V),
            pad_gain(q_gain), pad_gain(k_gain))


def _mla_fwd(x, g, wts, qlg, kvlg, tabs, tag):
    w_in, w_uq, wk, wv, w_out, qg, kg = wts
    h = _norm_rows(x, g, name=tag + "_norm")
    P = _mm(h, w_in, name=tag + "_in")
    cqn, ckvn = _mla_lora(P, qlg, kvlg, name=tag + "_lora")
    q_pre = _mm(cqn, w_uq, name=tag + "_uq")
    k_pre = _mm(ckvn, wk, name=tag + "_uk")
    ones_lane = jnp.tile((jnp.arange(LANE) == MLA_V).astype(F32), MLA_HEADS)[None, :]
    v = _mm(ckvn, wv, out_dtype=BF16, bias=ones_lane, name=tag + "_uv")
    q, k = _mla_qk(q_pre, k_pre, P, tabs, qg, kg, name=tag + "_qk")
    o, lse = _attn_fwd(q, k, v, sb=False, causal=True, heads=MLA_HEADS, dq=LANE, dv=LANE, group=FWD_GROUP,
                       sum_lane=MLA_V, name=tag + "_attn")
    y = _mm(o, w_out, residual=x, name=tag + "_out")
    return y, (x, h, P, cqn, ckvn, q_pre, k_pre, q, k, v, o, lse)


def _mla_bwd(dy, saved, g, wts, qlg, kvlg, tabs, tag):
    w_in, w_uq, wk, wv, w_out, qg, kg = wts
    x, h, P, cqn, ckvn, q_pre, k_pre, q, k, v, o, lse = saved
    do = _mm(dy, w_out, tb=True, name=tag + "_do")
    dw_out = _unpad_rows(_mm(o, dy, ta=True, name=tag + "_dwo"), MLA_HEADS, MLA_V)
    dq, dk, dv = _attn_bwd(q, k, v, o, do, lse, sb=False, causal=True, heads=MLA_HEADS, dq=LANE, dv=LANE,
                           name=tag + "_attn_bwd")
    dq_pre, dk_pre, dkr, dqg, dkg = _mla_qk_bwd(dq, dk, q_pre, k_pre, P, tabs, qg, kg, name=tag + "_qk_bwd")
    dcqn = _mm(dq_pre, w_uq, tb=True, name=tag + "_dcq")
    dckvn = _mm(dk_pre, wk, tb=True, name=tag + "_dckv_k")
    dckvn = _mm(dv, wv, tb=True, residual=dckvn, name=tag + "_dckv_v")
    dw_uq = _unpad_cols(_mm(cqn, dq_pre, ta=True, name=tag + "_dwuq"), MLA_HEADS, MLA_QK)
    dwk = _unpad_cols(_mm(ckvn, dk_pre, ta=True, name=tag + "_dwk"), MLA_HEADS, MLA_NOPE)
    dwv = _unpad_cols(_mm(ckvn, dv, ta=True, name=tag + "_dwv"), MLA_HEADS, MLA_V)
    dw_ukv = jnp.concatenate([dwk.reshape(MLA_KVL, MLA_HEADS, MLA_NOPE), dwv.reshape(MLA_KVL, MLA_HEADS, MLA_V)],
                             axis=2).reshape(MLA_KVL, -1)
    dP, dqlg, dkvlg = _mla_lora_bwd(dcqn, dckvn, dkr, P, qlg, kvlg, name=tag + "_lora_bwd")
    dx, dg = _mm(dP, w_in, tb=True, norm_bwd=(x, g, dy), name=tag + "_dh")
    dw_in_ext = _mm(h, dP, ta=True, name=tag + "_dwin")
    lat = MLA_QL + MLA_KVL
    dw_in = jnp.concatenate([dw_in_ext[:, :lat], dw_in_ext[:, lat + MLA_NOPE:lat + MLA_QK]], axis=1)
    return dx, dict(mix_norm=dg, mla_w_in=dw_in, mla_q_lora_gain=dqlg, mla_kv_lora_gain=dkvlg, mla_w_uq=dw_uq,
                    mla_w_ukv=dw_ukv, mla_q_gain=dqg[:, :MLA_QK], mla_k_gain=dkg[:, :MLA_QK], mla_w_out=dw_out)


def _xmem_fwd(x, mem, g, gm, wq, wkv, qg, kg, wo, tag):
    hq = _norm_rows(x, g, name=tag + "_norm")
    hm = _norm_rows(mem, gm, name=tag + "_mnorm")
    qp = _mm(hq, wq, name=tag + "_q")
    kv = _mm(hm, wkv, name=tag + "_kv")
    q = _head_norm(qp, qg, heads=MEM_HEADS, width=MEM_HD, scale=MEM_SCALE, name=tag + "_qn")
    kn = _head_norm(kv, kg, heads=MEM_HEADS, width=MEM_HD, colfn=lambda hh: 2 * hh, name=tag + "_kn")
    kvb = kv.reshape(-1, MEM_HEADS, 2, MEM_HD)[:, :, 1].reshape(-1, MEM_HEADS * MEM_HD).astype(BF16)
    o, lse = _attn_fwd(q, kn, kvb, sb=False, causal=False, heads=MEM_HEADS, dq=MEM_HD, dv=MEM_HD, group=MEM_HEADS,
                       name=tag + "_attn")
    y = _mm(o, wo, residual=x, name=tag + "_out")
    return y, (x, hq, hm, qp, kv, q, kn, kvb, o, lse)


def _xmem_bwd(dy, saved, mem, g, gm, wq, wkv, qg, kg, wo, tag):
    x, hq, hm, qp, kv, q, kn, kvb, o, lse = saved
    m = mem.shape[0]
    do = _mm(dy, wo, tb=True, name=tag + "_do")
    dwo = _mm(o, dy, ta=True, name=tag + "_dwo")
    dq, dk, dv = _attn_bwd(q, kn, kvb, o, do, lse, sb=False, causal=False, heads=MEM_HEADS, dq=MEM_HD, dv=MEM_HD,
                           name=tag + "_attn_bwd")
    dqp, dqg = _head_norm_bwd(dq, qp, qg, heads=MEM_HEADS, width=MEM_HD, scale=MEM_SCALE, out_dtype=BF16,
                              name=tag + "_qn_bwd")
    dkp, dkg = _head_norm_bwd(dk, kv, kg, heads=MEM_HEADS, width=MEM_HD, colfn=lambda hh: 2 * hh, out_dtype=F32,
                              name=tag + "_kn_bwd")
    dkv = jnp.concatenate([dkp.reshape(m, MEM_HEADS, MEM_HD), dv.reshape(m, MEM_HEADS, MEM_HD)], axis=2).reshape(m, -1)
    dwkv = _mm(hm, dkv, ta=True, name=tag + "_dwkv")
    dhm = _mm(dkv, wkv, tb=True, name=tag + "_dhm")
    _, dgm = _norm_rows_bwd(dhm, mem, gm, None, name=tag + "_dmnorm")
    dwq = _mm(hq, dqp, ta=True, name=tag + "_dwq")
    dx, dg = _mm(dqp, wq, tb=True, norm_bwd=(x, g, dy), name=tag + "_dhq")
    return dx, dict(xmem_norm=dg, xmem_mem_norm=dgm, xmem_wq=dwq, xmem_wkv=dwkv, xmem_q_gain=dqg, xmem_k_gain=dkg,
                    xmem_wo=dwo)


def _local_step(x, mem, positions, tgt, w):
    tabs = _rope_tables(positions)
    even = _even_weights(w["sbg_w_in"][0], w["sbg_w_out"][0])
    mla = _mla_weights(w["mla_w_in"][0], w["mla_w_uq"][0], w["mla_w_ukv"][0], w["mla_w_out"][0], w["mla_q_gain"][0],
                       w["mla_k_gain"][0])
    bt = jnp.repeat(w["sgu_b"][0].T, SG_GD, axis=1)
    saved = []
    for l in range(2):
        t = f"l{l}"
        x, s_pre = _ffn_fwd(x, w["ffn_pre_norm"][l], w["ffn_pre_w_gu"][l], w["ffn_pre_w_down"][l], t + "_pre")
        if l == 0:
            x, s_mix = _even_fwd(x, w["mix_norm"][0], even, w["sgu_ln_gain"][0], w["sgu_ln_bias"][0], w["sgu_w"][0], bt,
                                 t + "_even")
        else:
            x, s_mix = _mla_fwd(x, w["mix_norm"][1], mla, w["mla_q_lora_gain"][0], w["mla_kv_lora_gain"][0], tabs,
                                t + "_mla")
        x, s_xm = _xmem_fwd(x, mem, w["xmem_norm"][l], w["xmem_mem_norm"][l], w["xmem_wq"][l], w["xmem_wkv"][l],
                            w["xmem_q_gain"][l], w["xmem_k_gain"][l], w["xmem_wo"][l], t + "_xm")
        x, s_post = _ffn_fwd(x, w["ffn_post_norm"][l], w["ffn_post_w_gu"][l], w["ffn_post_w_down"][l], t + "_post")
        saved.append((s_pre, s_mix, s_xm, s_post))
    loss, dx = _loss_grad(x, tgt, name="loss")
    grads = {}

    def put(name, l, val):
        grads.setdefault(name, {})[l] = val

    for l in (1, 0):
        t = f"l{l}"
        s_pre, s_mix, s_xm, s_post = saved[l]
        dx, dg, dwgu, dwd = _ffn_bwd(dx, s_post, w["ffn_post_norm"][l], w["ffn_post_w_gu"][l], w["ffn_post_w_down"][l],
                                     t + "_post")
        put("ffn_post_norm", l, dg), put("ffn_post_w_gu", l, dwgu), put("ffn_post_w_down", l, dwd)
        dx, gx = _xmem_bwd(dx, s_xm, mem, w["xmem_norm"][l], w["xmem_mem_norm"][l], w["xmem_wq"][l], w["xmem_wkv"][l],
                           w["xmem_q_gain"][l], w["xmem_k_gain"][l], w["xmem_wo"][l], t + "_xm")
        for k_, v_ in gx.items():
            put(k_, l, v_)
        if l == 0:
            dx, gm = _even_bwd(dx, s_mix, w["mix_norm"][0], even, w["sgu_ln_gain"][0], w["sgu_w"][0], bt, t + "_even")
        else:
            dx, gm = _mla_bwd(dx, s_mix, w["mix_norm"][1], mla, w["mla_q_lora_gain"][0], w["mla_kv_lora_gain"][0], tabs,
                              t + "_mla")
        for k_, v_ in gm.items():
            put(k_, l if k_ == "mix_norm" else 0, v_)
        dx, dg, dwgu, dwd = _ffn_bwd(dx, s_pre, w["ffn_pre_norm"][l], w["ffn_pre_w_gu"][l], w["ffn_pre_w_down"][l],
                                     t + "_pre")
        put("ffn_pre_norm", l, dg), put("ffn_pre_w_gu", l, dwgu), put("ffn_pre_w_down", l, dwd)
    return loss, dx, {k_: [v_[l] for l in sorted(v_)] for k_, v_ in grads.items()}


N_CHIPS = 4
PACK_COLS = 1024
PACK_ROW_MULTIPLE = 512


def _place():
    x, y, c = lax.axis_index("x"), lax.axis_index("y"), lax.axis_index("c")
    return x, y, c, [(1 - x, y), (x, 1 - y), (1 - x, 1 - y)]


def _hops(x, y, c):
    return ((x + 1 - c) % 2, (y + c) % 2), ((x + c) % 2, (y + 1 - c) % 2), (1 - x, 1 - y)


def _gather_chips(shard):
    R, C = shard.shape
    Rh = R // 2

    def body(x_ref, out_ref, send_sems, recv_sems):
        x, y, c = lax.axis_index("x"), lax.axis_index("y"), lax.axis_index("c")
        n1, n2, nd = _hops(x, y, c)
        me, q1, q2, qd = 2 * x + y, 2 * n1[0] + n1[1], 2 * n2[0] + n2[1], 2 * nd[0] + nd[1]

        def half(chip, core):
            return out_ref.at[chip, pl.ds(core * Rh, Rh), :]

        def copy(k, chip, core, to, src=None):
            return pltpu.make_async_remote_copy(src_ref=half(chip, core) if src is None else src, dst_ref=half(chip, core),
                                                send_sem=send_sems.at[k], recv_sem=recv_sems.at[k], device_id=to,
                                                device_id_type=MESH)

        own = x_ref.at[pl.ds(c * Rh, Rh), :]
        sibling = (x, y, 1 - c)
        sends = [copy(0, me, c, (*n1, c), src=own), copy(1, me, c, (*n2, c), src=own)]
        sends[0].start()
        sends[1].start()
        copy(0, q1, c, sibling).wait_recv()
        sends += [copy(2, q1, c, (*n2, c)), copy(3, q1, c, sibling)]
        sends[2].start()
        sends[3].start()
        copy(1, q2, c, sibling).wait_recv()
        sends.append(copy(4, q2, c, sibling))
        sends[4].start()
        copy(2, qd, c, sibling).wait_recv()
        sends.append(copy(5, qd, c, sibling))
        sends[5].start()
        copy(3, q2, 1 - c, sibling).wait_recv()
        copy(4, q1, 1 - c, sibling).wait_recv()
        copy(5, qd, 1 - c, sibling).wait_recv()
        for cp in sends:
            cp.wait_send()

    others = pl.pallas_call(
        body, name="gather_weights", out_shape=jax.ShapeDtypeStruct((N_CHIPS, R, C), shard.dtype),
        in_specs=[ANY], out_specs=ANY,
        scratch_shapes=[pltpu.SemaphoreType.DMA((6,)), pltpu.SemaphoreType.DMA((6,))])(shard)
    me = 2 * lax.axis_index("x") + lax.axis_index("y")
    return lax.dynamic_update_slice(others, shard[None], (me, 0, 0))


def _gather_devices(block):
    M, N = block.shape

    def body(x_ref, out_ref, send_sems, recv_sems, local_sem):
        x, y, c, chips = _place()
        me, sibling = (x, y, c), (x, y, 1 - c)

        def rows(px, py, pc):
            return out_ref.at[pl.ds((4 * px + 2 * py + pc) * M, M), :]

        def copy(k, blk, to, src=None):
            return pltpu.make_async_remote_copy(src_ref=rows(*blk) if src is None else src, dst_ref=rows(*blk),
                                                send_sem=send_sems.at[k], recv_sem=recv_sems.at[k], device_id=to,
                                                device_id_type=MESH)

        mine = pltpu.make_async_copy(x_ref, rows(*me), local_sem)
        mine.start()
        first = [copy(0, me, sibling, src=x_ref)]
        first += [copy(1 + j, me, (*chip, c), src=x_ref) for j, chip in enumerate(chips)]
        for cp in first:
            cp.start()
        passed = [copy(4 + j, (*chip, c), sibling) for j, chip in enumerate(chips)]
        for j, chip in enumerate(chips):
            copy(1 + j, (*chip, c), me).wait_recv()
            passed[j].start()
        copy(0, sibling, me).wait_recv()
        for j, chip in enumerate(chips):
            copy(4 + j, (*chip, 1 - c), me).wait_recv()
        for cp in first + passed:
            cp.wait_send()
        mine.wait()

    vmem = pl.BlockSpec(memory_space=pltpu.VMEM)
    return pl.pallas_call(
        body, name=f"gather_devices_{M}", out_shape=jax.ShapeDtypeStruct((8 * M, N), block.dtype),
        in_specs=[vmem], out_specs=vmem,
        scratch_shapes=[pltpu.SemaphoreType.DMA((7,)), pltpu.SemaphoreType.DMA((7,)), pltpu.SemaphoreType.DMA],
        compiler_params=pltpu.CompilerParams(vmem_limit_bytes=VMEM_LIMIT))(block)


def _swap_halves(g):
    n, R, C = g.shape
    Rh = R // 2

    def body(g_ref, a_ref, send_sem, recv_sem):
        x, y, c, _ = _place()
        cp = pltpu.make_async_remote_copy(src_ref=g_ref.at[:, pl.ds((1 - c) * Rh, Rh), :], dst_ref=a_ref,
                                          send_sem=send_sem, recv_sem=recv_sem, device_id=(x, y, 1 - c),
                                          device_id_type=MESH)
        cp.start()
        cp.wait()

    return pl.pallas_call(body, name="grad_swap_halves", out_shape=jax.ShapeDtypeStruct((n, Rh, C), g.dtype),
                          in_specs=[ANY], out_specs=ANY,
                          scratch_shapes=[pltpu.SemaphoreType.DMA, pltpu.SemaphoreType.DMA])(g)


def _add_picked(a, b, picks, *, a_row_half=None, out_dtype, name):
    n_out = picks.shape[0]
    _, rows, C = b.shape
    tr = _row_tile(rows, 512)
    nt = rows // tr
    half = jnp.zeros((1,), jnp.int32) if a_row_half is None else a_row_half

    def body(pick_ref, half_ref, a_ref, b_ref, o_ref):
        o_ref[...] = (a_ref[...].astype(F32) + b_ref[...].astype(F32)).astype(o_ref.dtype)

    spec = pltpu.PrefetchScalarGridSpec(
        num_scalar_prefetch=2, grid=(n_out, nt),
        in_specs=[pl.BlockSpec((1, tr, C), lambda j, i, pick, hf: (pick[j], hf[0] * nt + i, 0)),
                  pl.BlockSpec((1, tr, C), lambda j, i, pick, hf: (pick[j], i, 0))],
        out_specs=pl.BlockSpec((1, tr, C), lambda j, i, pick, hf: (j, i, 0)))
    return pl.pallas_call(body, name=name, grid_spec=spec, out_shape=jax.ShapeDtypeStruct((n_out, rows, C), out_dtype),
                          compiler_params=_params(2))(picks.astype(jnp.int32), half.astype(jnp.int32), a, b)


def _hop_exchange(src, hop, *, name):
    def body(s_ref, d_ref, send_sem, recv_sem):
        x, y, c = lax.axis_index("x"), lax.axis_index("y"), lax.axis_index("c")
        cp = pltpu.make_async_remote_copy(src_ref=s_ref, dst_ref=d_ref, send_sem=send_sem, recv_sem=recv_sem,
                                          device_id=(*_hops(x, y, c)[hop], c), device_id_type=MESH)
        cp.start()
        cp.wait()

    return pl.pallas_call(body, name=name, out_shape=jax.ShapeDtypeStruct(src.shape, src.dtype), in_specs=[ANY],
                          out_specs=ANY, scratch_shapes=[pltpu.SemaphoreType.DMA, pltpu.SemaphoreType.DMA])(src)


def _reduce_over_chips(g):
    x, y, c = lax.axis_index("x"), lax.axis_index("y"), lax.axis_index("c")
    n1, n2, _ = _hops(x, y, c)
    chip = lambda p: 2 * p[0] + p[1]
    near = jnp.stack([chip((x, y)), chip(n2)])
    far = jnp.stack([chip(n1), chip((1 - x, 1 - y))])
    half = c.reshape(1)
    sib = _swap_halves(g)
    kept = _add_picked(g, sib, near, a_row_half=half, out_dtype=F32, name="grad_add_near")
    sent = _add_picked(g, sib, far, a_row_half=half, out_dtype=BF16, name="grad_add_far")
    got = _hop_exchange(sent, 0, name="grad_hop_first")
    mine = _add_picked(kept, got, jnp.zeros((1,), jnp.int32), out_dtype=F32, name="grad_add_mine")
    theirs = _add_picked(kept, got, jnp.ones((1,), jnp.int32), out_dtype=BF16, name="grad_add_theirs")
    got = _hop_exchange(theirs, 1, name="grad_hop_second")
    total = _add_picked(mine, got, jnp.zeros((1,), jnp.int32), out_dtype=F32, name="grad_add_total")
    return _join_halves(total[0])


def _sum_slots(b, *, name):
    n, R, C = b.shape
    tr = _row_tile(R, 512)

    def body(b_ref, o_ref):
        acc = b_ref[0]
        for q in range(1, n):
            acc = acc + b_ref[q]
        o_ref[...] = acc

    return pl.pallas_call(body, name=name, grid=(R // tr,), in_specs=[pl.BlockSpec((n, tr, C), lambda i: (0, i, 0))],
                          out_specs=pl.BlockSpec((tr, C), lambda i: (i, 0)), out_shape=jax.ShapeDtypeStruct((R, C), F32),
                          compiler_params=_params(1))(b)


def _join_halves(r):
    Rh, C = r.shape

    def body(r_ref, o_ref, send_sem, recv_sem):
        x, y, c, _ = _place()
        own, other = o_ref.at[pl.ds(c * Rh, Rh), :], o_ref.at[pl.ds((1 - c) * Rh, Rh), :]
        cp = pltpu.make_async_remote_copy(src_ref=r_ref, dst_ref=own, send_sem=send_sem, recv_sem=recv_sem,
                                          device_id=(x, y, 1 - c), device_id_type=MESH)
        cp.start()
        pltpu.make_async_remote_copy(src_ref=r_ref, dst_ref=other, send_sem=send_sem, recv_sem=recv_sem,
                                     device_id=(x, y, 1 - c), device_id_type=MESH).wait_recv()
        cp.wait_send()

    theirs = pl.pallas_call(
        body, name="grad_join_halves", out_shape=jax.ShapeDtypeStruct((2 * Rh, C), r.dtype), in_specs=[ANY], out_specs=ANY,
        scratch_shapes=[pltpu.SemaphoreType.DMA, pltpu.SemaphoreType.DMA])(r)
    return lax.dynamic_update_slice(theirs, r, (lax.axis_index("c") * Rh, 0))


def _size(shape):
    size = 1
    for d in shape:
        size *= d
    return size


def _pack(pieces, cols, row_multiple, dtype):
    if any(p.size % cols for p in pieces):
        flat = jnp.concatenate([p.reshape(-1).astype(dtype) for p in pieces])
        pieces = [jnp.pad(flat, (0, -flat.shape[0] % cols))]
    rows = [p.reshape(-1, cols).astype(dtype) for p in pieces]
    pad = -sum(r.shape[0] for r in rows) % row_multiple
    return jnp.concatenate(rows + ([jnp.zeros((pad, cols), dtype)] if pad else []), axis=0)


def _unpack(buf, shapes):
    cols = buf.shape[1]
    if any(_size(s) % cols for s in shapes):
        flat, out, at = buf.reshape(-1), [], 0
        for shp in shapes:
            out.append(flat[at:at + _size(shp)].reshape(shp))
            at += _size(shp)
        return out
    out, at = [], 0
    for shp in shapes:
        out.append(buf[at:at + _size(shp) // cols].reshape(shp))
        at += _size(shp) // cols
    return out


SHARDED = (("ffn_pre_w_gu", 2), ("ffn_pre_w_down", 1), ("sbg_w_in", 2), ("sbg_w_out", 1), ("mla_w_in", 1),
           ("mla_w_uq", 2), ("mla_w_ukv", 2), ("mla_w_out", 1), ("xmem_wq", 1), ("xmem_wkv", 2), ("xmem_wo", 1),
           ("ffn_post_w_gu", 2), ("ffn_post_w_down", 1))
LORA_GAINS = ("mla_q_lora_gain", "mla_kv_lora_gain")
REPLICATED = ("ffn_pre_norm", "mix_norm", "sgu_ln_gain", "sgu_ln_bias", "sgu_w", "sgu_b", "mla_q_gain", "mla_k_gain",
              "xmem_norm", "xmem_mem_norm", "xmem_q_gain", "xmem_k_gain", "ffn_post_norm")
WEIGHTS = ("ffn_pre_norm", "ffn_pre_w_gu", "ffn_pre_w_down", "mix_norm", "sbg_w_in", "sgu_ln_gain", "sgu_ln_bias", "sgu_w",
           "sgu_b", "sbg_w_out", "mla_w_in", "mla_q_lora_gain", "mla_kv_lora_gain", "mla_w_uq", "mla_w_ukv", "mla_q_gain",
           "mla_k_gain", "mla_w_out", "xmem_norm", "xmem_mem_norm", "xmem_wq", "xmem_wkv", "xmem_q_gain", "xmem_k_gain",
           "xmem_wo", "ffn_post_norm", "ffn_post_w_gu", "ffn_post_w_down")
INPUTS = ("x", "mem", "positions") + WEIGHTS + ("loss_target",) + tuple("m_" + n for n in WEIGHTS) + tuple(
    "v_" + n for n in WEIGHTS)


def _step(a):
    x, y, c, _ = _place()
    chip = 2 * x + y
    shard_shapes = [a[n].shape for n, _ in SHARDED]

    gathered = _gather_chips(_pack([a[n] for n, _ in SHARDED], PACK_COLS, PACK_ROW_MULTIPLE, BF16))
    w, at = {}, 0
    for (n, ax), shp in zip(SHARDED, shard_shapes):
        rows = _size(shp) // PACK_COLS
        per_chip = gathered[:, at:at + rows].reshape((N_CHIPS,) + shp)
        at += rows
        w[n] = jnp.moveaxis(per_chip, 0, ax).reshape(shp[:ax] + (N_CHIPS * shp[ax],) + shp[ax + 1:])
    gains = jnp.zeros((8, LANE), F32)
    for r, n in enumerate(LORA_GAINS):
        gains = gains.at[r, :a[n].shape[1]].set(a[n][0])
    gains = _gather_devices(gains)
    for r, n in enumerate(LORA_GAINS):
        w[n] = jnp.concatenate([gains[16 * q + r, :a[n].shape[1]] for q in range(N_CHIPS)])[None, :]
    for n in REPLICATED:
        w[n] = a[n]

    loss, dx, grads = _local_step(a["x"][0], a["mem"][0], a["positions"][0], a["loss_target"][0], w)
    loss = lax.psum(loss, ("x", "y", "c"))
    small_names = REPLICATED + LORA_GAINS
    full = {n: jnp.stack(grads[n]).reshape(w[n].shape) for n in small_names}

    def cut(n, ax, q):
        size = w[n].shape[ax] // N_CHIPS
        return [lax.slice_in_dim(gl, q * size, (q + 1) * size, axis=ax - 1) for gl in grads[n]]

    g = jnp.stack([_pack([p for n, ax in SHARDED for p in cut(n, ax, q)], PACK_COLS, PACK_ROW_MULTIPLE, F32)
                   for q in range(N_CHIPS)])
    reduced = _reduce_over_chips(g)
    gw = dict(zip([n for n, _ in SHARDED], _unpack(reduced, shard_shapes)))

    small = _pack([full[n] for n in small_names], LANE, 256, F32)
    rows = small.shape[0]
    summed = _sum_slots(_gather_devices(small).reshape(8, rows, LANE), name="grad_sum_devices")
    for n, val in zip(small_names, _unpack(summed, [full[n].shape for n in small_names])):
        if n in LORA_GAINS:
            size = a[n].shape[1]
            val = lax.dynamic_slice_in_dim(val, chip * size, size, axis=1)
        gw[n] = val

    upd = {n: _adamw(a[n], gw[n], a["m_" + n], a["v_" + n], name="adamw_" + n) for n in WEIGHTS}
    return (loss, dx[None], *[gw[n] for n in WEIGHTS], *[upd[n][0] for n in WEIGHTS], *[upd[n][1] for n in WEIGHTS],
            *[upd[n][2] for n in WEIGHTS])


def kernel(x, mem, positions, ffn_pre_norm, ffn_pre_w_gu, ffn_pre_w_down, mix_norm, sbg_w_in, sgu_ln_gain,
           sgu_ln_bias, sgu_w, sgu_b, sbg_w_out, mla_w_in, mla_q_lora_gain, mla_kv_lora_gain, mla_w_uq, mla_w_ukv,
           mla_q_gain, mla_k_gain, mla_w_out, xmem_norm, xmem_mem_norm, xmem_wq, xmem_wkv, xmem_q_gain, xmem_k_gain,
           xmem_wo, ffn_post_norm, ffn_post_w_gu, ffn_post_w_down, loss_target, m_ffn_pre_norm, m_ffn_pre_w_gu,
           m_ffn_pre_w_down, m_mix_norm, m_sbg_w_in, m_sgu_ln_gain, m_sgu_ln_bias, m_sgu_w, m_sgu_b, m_sbg_w_out,
           m_mla_w_in, m_mla_q_lora_gain, m_mla_kv_lora_gain, m_mla_w_uq, m_mla_w_ukv, m_mla_q_gain, m_mla_k_gain,
           m_mla_w_out, m_xmem_norm, m_xmem_mem_norm, m_xmem_wq, m_xmem_wkv, m_xmem_q_gain, m_xmem_k_gain,
           m_xmem_wo, m_ffn_post_norm, m_ffn_post_w_gu, m_ffn_post_w_down, v_ffn_pre_norm, v_ffn_pre_w_gu,
           v_ffn_pre_w_down, v_mix_norm, v_sbg_w_in, v_sgu_ln_gain, v_sgu_ln_bias, v_sgu_w, v_sgu_b, v_sbg_w_out,
           v_mla_w_in, v_mla_q_lora_gain, v_mla_kv_lora_gain, v_mla_w_uq, v_mla_w_ukv, v_mla_q_gain, v_mla_k_gain,
           v_mla_w_out, v_xmem_norm, v_xmem_mem_norm, v_xmem_wq, v_xmem_wkv, v_xmem_q_gain, v_xmem_k_gain,
           v_xmem_wo, v_ffn_post_norm, v_ffn_post_w_gu, v_ffn_post_w_down):
    given = locals()
    return _step({n: given[n] for n in INPUTS})
```

```python
import functools

import jax
import jax.numpy as jnp
from jax import lax
from jax.experimental import pallas as pl
from jax.experimental.pallas import tpu as pltpu

F32, BF16 = jnp.float32, jnp.bfloat16
LANE = 128
VMEM_LIMIT = 56 * 1024 * 1024
EPS = 1e-6
D_FF = 2816
SB_HEADS, SB_HD = 8, 64
SG_GROUPS, SG_GD, SG_CHUNK = 8, 64, 128
SB_W, SG_W = SB_HEADS * SB_HD, SG_GROUPS * SG_GD
MLA_HEADS, MLA_NOPE, MLA_ROPE, MLA_V = 16, 64, 32, 64
MLA_QK = MLA_NOPE + MLA_ROPE
MLA_QL, MLA_KVL = 512, 256
ROPE_THETA = 10000.0
MEM_HEADS, MEM_HD = 4, 256
SB_SCALE, MLA_SCALE, MEM_SCALE = SB_HD ** -0.5, MLA_QK ** -0.5, MEM_HD ** -0.5
ADAM_LR, ADAM_B1, ADAM_B2, ADAM_EPS, ADAM_WD, ADAM_STEP = 0.001, 0.9, 0.999, 1e-08, 0.01, 10
MESH = pl.DeviceIdType.MESH
ANY = pl.BlockSpec(memory_space=pl.ANY)


def _params(n_axes):
    return pltpu.CompilerParams(dimension_semantics=("arbitrary",) * n_axes, vmem_limit_bytes=VMEM_LIMIT)


MM_TILE_CAP = 1408
MM_VMEM_BUDGET = 40 * 1024 * 1024


def _tile(dim, cap):
    if dim <= cap:
        return dim
    best = max(t for t in range(LANE, cap + 1, LANE) if dim % t == 0)
    return best


def _k_scratch(count, tile, nk):
    return [pltpu.VMEM(tile, F32)] * count if nk > 1 else []


def _over_k_steps(prods, acc_refs, nk, finish):
    if nk == 1:
        finish(prods)
        return
    kk = pl.program_id(2)

    @pl.when(kk == 0)
    def _():
        for ref, p in zip(acc_refs, prods):
            ref[...] = p

    @pl.when(kk > 0)
    def _():
        for ref, p in zip(acc_refs, prods):
            ref[...] += p

    @pl.when(kk == nk - 1)
    def _():
        finish([ref[...] for ref in acc_refs])


def _mm(a, b, *, ta=False, tb=False, out_dtype=F32, scale=1.0, residual=None, bias=None, norm_bwd=None, a_off=(0, 0),
        b_off=(0, 0), m=None, n=None, k=None, name):
    am, ak = (a.shape[1], a.shape[0]) if ta else a.shape
    bk, bn = (b.shape[1], b.shape[0]) if tb else b.shape
    M, N, K = m or am, n or bn, k or ak
    tm, tn = _tile(M, MM_TILE_CAP if norm_bwd is None else MM_TILE_CAP // 2), _tile(N, MM_TILE_CAP)
    n_full = (residual is not None) + (2 if norm_bwd is not None else 0)
    fixed = tm * tn * (4 + 2 * jnp.dtype(out_dtype).itemsize + 8 * n_full)
    per_k = (tm * (2 * a.dtype.itemsize + 2) + tn * (2 * b.dtype.itemsize + 2))
    tk = _tile(K, max(LANE, (MM_VMEM_BUDGET - fixed) // per_k))
    nm, nn, nk = M // tm, N // tn, K // tk
    assert norm_bwd is None or nn == 1
    a_off = (a_off[0] // (tk if ta else tm), a_off[1] // (tm if ta else tk))
    b_off = (b_off[0] // (tn if tb else tk), b_off[1] // (tk if tb else tn))
    dims = (((0 if ta else 1,), (1 if tb else 0,)), ((), ()))
    n_out = 1 if norm_bwd is None else 2

    def body(*refs):
        a_ref, b_ref = refs[0], refs[1]
        n_in = len(ins)
        o_ref, extras, acc_refs = refs[n_in], refs[2:n_in], refs[n_in + n_out:]
        first_rows = pl.program_id(0) == 0

        def finish(total):
            out = total * scale
            for extra in (extras if norm_bwd is None else extras[:-3]):
                out = out + extra[...].astype(F32)
            if norm_bwd is not None:
                x_ref, g_ref, dres_ref = extras[-3:]
                dg_ref = refs[n_in + 1]
                dx, dg = _rmsnorm_bwd(out, x_ref[...], g_ref[...])
                out = dx + dres_ref[...]

                @pl.when(first_rows)
                def _():
                    dg_ref[...] = jnp.zeros_like(dg_ref)

                dg_ref[...] += dg
            o_ref[...] = out.astype(o_ref.dtype)

        prod = lax.dot_general(a_ref[...].astype(BF16), b_ref[...].astype(BF16), dims, preferred_element_type=F32)
        _over_k_steps([prod], acc_refs, nk, lambda totals: finish(totals[0]))

    (ao0, ao1), (bo0, bo1) = a_off, b_off
    a_spec = (pl.BlockSpec((tk, tm), lambda i, j, kk: (kk + ao0, i + ao1)) if ta
              else pl.BlockSpec((tm, tk), lambda i, j, kk: (i + ao0, kk + ao1)))
    b_spec = (pl.BlockSpec((tn, tk), lambda i, j, kk: (j + bo0, kk + bo1)) if tb
              else pl.BlockSpec((tk, tn), lambda i, j, kk: (kk + bo0, j + bo1)))
    o_spec = pl.BlockSpec((tm, tn), lambda i, j, kk: (i, j))
    ins, in_specs = [a, b], [a_spec, b_spec]
    if residual is not None:
        ins.append(residual)
        in_specs.append(o_spec)
    if bias is not None:
        ins.append(bias)
        in_specs.append(pl.BlockSpec((1, tn), lambda i, j, kk: (0, j)))
    out_specs, out_shape = o_spec, jax.ShapeDtypeStruct((M, N), out_dtype)
    if norm_bwd is not None:
        x, gain, dres = norm_bwd
        row = pl.BlockSpec((1, tn), lambda i, j, kk: (0, 0))
        ins += [x, gain.reshape(1, N), dres]
        in_specs += [o_spec, row, o_spec]
        out_specs, out_shape = [o_spec, row], [out_shape, jax.ShapeDtypeStruct((1, N), F32)]
    return pl.pallas_call(
        body, name=name, grid=(nm, nn, nk), in_specs=in_specs, out_specs=out_specs, out_shape=out_shape,
        scratch_shapes=_k_scratch(1, (tm, tn), nk), compiler_params=_params(3))(*ins)


def _mm_swiglu(h, wgu, *, name):
    M, K = h.shape
    F = wgu.shape[1] // 2
    tm, tn, tk = _tile(M, 512), _tile(F, MM_TILE_CAP), _tile(K, 1024)
    nm, nf, nk = M // tm, F // tn, K // tk

    def body(h_ref, wg_ref, wu_ref, g_ref, u_ref, a_ref, *acc_refs):
        def finish(totals):
            g, u = totals
            g_ref[...] = g.astype(BF16)
            u_ref[...] = u.astype(BF16)
            a_ref[...] = (g * jax.nn.sigmoid(g) * u).astype(BF16)

        hb = h_ref[...]
        _over_k_steps([jnp.dot(hb, wg_ref[...], preferred_element_type=F32),
                       jnp.dot(hb, wu_ref[...], preferred_element_type=F32)], acc_refs, nk, finish)

    o_spec = pl.BlockSpec((tm, tn), lambda i, j, kk: (i, j))
    shp = jax.ShapeDtypeStruct((M, F), BF16)
    return pl.pallas_call(
        body, name=name, grid=(nm, nf, nk),
        in_specs=[pl.BlockSpec((tm, tk), lambda i, j, kk: (i, kk)),
                  pl.BlockSpec((tk, tn), lambda i, j, kk: (kk, j)),
                  pl.BlockSpec((tk, tn), lambda i, j, kk: (kk, j + nf))],
        out_specs=[o_spec, o_spec, o_spec], out_shape=[shp, shp, shp],
        scratch_shapes=_k_scratch(2, (tm, tn), nk), compiler_params=_params(3))(h, wgu, wgu)


def _mm_dswiglu(dy, wd, gate, up, *, scale, name):
    M, K = dy.shape
    F = wd.shape[0]
    tm, tn, tk = _tile(M, 512), _tile(F, MM_TILE_CAP), _tile(K, 1024)
    nm, nf, nk = M // tm, F // tn, K // tk

    def body(dy_ref, wd_ref, g_ref, u_ref, dg_ref, du_ref, *acc_refs):
        def finish(totals):
            da = totals[0] * scale
            g, u = g_ref[...].astype(F32), u_ref[...].astype(F32)
            sg = jax.nn.sigmoid(g)
            du_ref[...] = (da * g * sg).astype(BF16)
            dg_ref[...] = (da * u * sg * (1.0 + g * (1.0 - sg))).astype(BF16)

        _over_k_steps([_nt(dy_ref[...].astype(BF16), wd_ref[...])], acc_refs, nk, finish)

    o_spec = pl.BlockSpec((tm, tn), lambda i, j, kk: (i, j))
    shp = jax.ShapeDtypeStruct((M, F), BF16)
    return pl.pallas_call(
        body, name=name, grid=(nm, nf, nk),
        in_specs=[pl.BlockSpec((tm, tk), lambda i, j, kk: (i, kk)),
                  pl.BlockSpec((tn, tk), lambda i, j, kk: (j, kk)), o_spec, o_spec],
        out_specs=[o_spec, o_spec], out_shape=[shp, shp],
        scratch_shapes=_k_scratch(1, (tm, tn), nk), compiler_params=_params(3))(dy, wd, gate, up)


HEAD_ROWS = 1024


def _row_tile(rows, cap):
    t = cap
    while t >= 8:
        if rows % t == 0:
            return t
        t //= 2
    return rows


def _rowwise(fn, rows, consts, outs, sums=(), hsums=(), *, heads=None, tm=256, name):
    rows = [r if isinstance(r, tuple) else (r, r.shape[1], None) for r in rows]
    S = rows[0][0].shape[0]
    tm = _row_tile(S, tm)
    nh = heads or 1
    n_r, n_c, n_o, n_h, n_s = len(rows), len(consts), len(outs), len(hsums), len(sums)

    def body(*refs):
        r = [x[...] for x in refs[:n_r]]
        c = [x[...] for x in refs[n_r:n_r + n_c]]
        o_refs = refs[n_r + n_c:n_r + n_c + n_o]
        h_refs = refs[n_r + n_c + n_o:n_r + n_c + n_o + n_h]
        s_refs = refs[n_r + n_c + n_o + n_h:]
        res = fn(*r, *c)
        res = res if isinstance(res, (tuple, list)) else (res,)
        for ref, val in zip(o_refs, res[:n_o]):
            ref[...] = val.astype(ref.dtype)
        if n_h:
            @pl.when(pl.program_id(1) == 0)
            def _():
                for ref in h_refs:
                    ref[...] = jnp.zeros_like(ref)
            for ref, val in zip(h_refs, res[n_o:n_o + n_h]):
                ref[...] += val
        if n_s:
            @pl.when((pl.program_id(0) == 0) & (pl.program_id(1) == 0))
            def _():
                for ref in s_refs:
                    ref[...] = jnp.zeros_like(ref)
            for ref, val in zip(s_refs, res[n_o + n_h:]):
                ref[...] += val

    def col(colfn):
        return (lambda i, h: (i, 0)) if colfn is None else (lambda i, h: (i, colfn(h)))

    in_specs = [pl.BlockSpec((tm, w), col(cf)) for _, w, cf in rows]
    in_specs += [pl.BlockSpec(a.shape, lambda i, h, nd=a.ndim: (0,) * nd) for a in consts]
    out_specs = [pl.BlockSpec((tm, w // nh), (lambda i, h: (i, h)) if heads else (lambda i, h: (i, 0))) for w, _ in outs]
    out_specs += [pl.BlockSpec((tm, w), lambda i, h: (i, 0)) for w in hsums]
    out_specs += [pl.BlockSpec(sh, lambda i, h, nd=len(sh): (0,) * nd) for sh in sums]
    out_shape = [jax.ShapeDtypeStruct((S, w), dt) for w, dt in outs]
    out_shape += [jax.ShapeDtypeStruct((S, w), F32) for w in hsums]
    out_shape += [jax.ShapeDtypeStruct(sh, F32) for sh in sums]
    return pl.pallas_call(body, name=name, grid=(S // tm, nh), in_specs=in_specs, out_specs=out_specs,
                          out_shape=out_shape, compiler_params=_params(2))(*[a for a, _, _ in rows], *consts)


def _rms(x, width=None):
    width = width or x.shape[-1]
    return lax.rsqrt(jnp.sum(x * x, axis=-1, keepdims=True) * (1.0 / width) + EPS)


def _rmsnorm_fwd(x, g, width=None):
    return x * _rms(x, width) * g


def _rmsnorm_bwd(dy, x, g, width=None):
    width = width or x.shape[-1]
    r = _rms(x, width)
    xn = x * r
    dxn = dy * g
    dx = r * (dxn - xn * (jnp.sum(dxn * xn, axis=-1, keepdims=True) * (1.0 / width)))
    return dx, jnp.sum(dy * xn, axis=0, keepdims=True)


def _norm_rows(x, g, *, name, out_dtype=BF16):
    D = x.shape[1]
    return _rowwise(lambda xv, gv: _rmsnorm_fwd(xv.astype(F32), gv), [x], [g.reshape(1, D)], [(D, out_dtype)],
                    name=name)[0]


def _norm_rows_bwd(dh, x, g, dres, *, name):
    D = x.shape[1]

    def fn(dhv, xv, *rest):
        dx, dg = _rmsnorm_bwd(dhv.astype(F32), xv, rest[-1])
        return (dx + rest[0] if dres is not None else dx), dg

    rows = [dh, x] + ([dres] if dres is not None else [])
    return _rowwise(fn, rows, [g.reshape(1, D)], [(D, F32)], [(1, D)], name=name)


def _softplus(z):
    return jnp.where(z > 20.0, z, jnp.log(1.0 + jnp.exp(z)))


def _running_sum(v, u, split=True):
    if not split:
        return jnp.dot(v.astype(BF16), u, preferred_element_type=F32)
    hi = lax.bitcast_convert_type(lax.bitcast_convert_type(v, jnp.uint32) & jnp.uint32(0xFFFF0000), F32)
    return (jnp.dot(hi.astype(BF16), u, preferred_element_type=F32)
            + jnp.dot((v - hi).astype(BF16), u, preferred_element_type=F32))


def _triangle(tk, inclusive_prefix):
    j, s = lax.broadcasted_iota(jnp.int32, (tk, tk), 0), lax.broadcasted_iota(jnp.int32, (tk, tk), 1)
    return ((j <= s) if inclusive_prefix else (j > s)).astype(BF16)


def _nt(a, b):
    return lax.dot_general(a, b, (((1,), (1,)), ((), ())), preferred_element_type=F32)


def _tn(a, b):
    return lax.dot_general(a, b, (((0,), (0,)), ((), ())), preferred_element_type=F32)


ATT_TQ, ATT_TK = 512, 512
SB_SUB = 256
FWD_GROUP = 2


def _attn_fwd(q, k, v, *, sb, causal, heads, dq, dv, group=1, kcol=None, vcol=None, sum_lane=None, name):
    S, Sk = q.shape[0], k.shape[0]
    tq, tk = min(ATT_TQ, S), min(ATT_TK, Sk)
    sub = min(SB_SUB, tk) if sb else tk
    assert tq % sub == 0 or not causal
    kcol = kcol or (lambda h: h)
    vcol = vcol or (lambda h: h)
    members = range(group)

    def body(*refs):
        if sb:
            q_ref, k_ref, v_ref, u_ref, o_ref, lse_ref, acc_ref, r_ref = refs
            r_ref[...] = jnp.zeros_like(r_ref)
        else:
            q_ref, k_ref, v_ref, o_ref, lse_ref, acc_ref, m_ref, l_ref = refs
            m_ref[...] = jnp.full_like(m_ref, -1e30)
            l_ref[...] = jnp.zeros_like(l_ref)
        first_row = pl.program_id(1) * tq
        qb = [q_ref[:, hh * dq:(hh + 1) * dq] for hh in members]
        acc_ref[...] = jnp.zeros_like(acc_ref)
        nblk = (first_row + tq) // sub if causal else Sk // sub
        nfull = (first_row + (0 if sb else 1)) // sub if causal else nblk
        n_cut = tq // sub if causal else 0

        def scores(jj):
            off = pl.multiple_of(jj * sub, sub)
            return tuple(_nt(qb[hh], k_ref[pl.ds(off, sub), hh * dq:(hh + 1) * dq]) for hh in members)

        def weigh(jj, scores_now, masked):
            off = pl.multiple_of(jj * sub, sub)
            if masked:
                kpos = off + lax.broadcasted_iota(jnp.int32, (tq, sub), 1)
                qpos = first_row + lax.broadcasted_iota(jnp.int32, (tq, sub), 0)
                valid = (kpos < qpos) if sb else (kpos <= qpos)
            for hh in members:
                vb = v_ref[pl.ds(off, sub), hh * dv:(hh + 1) * dv]
                s = scores_now[hh]
                if sb:
                    sp = _softplus(s)
                    ls = jnp.where(valid, -sp, 0.0) if masked else -sp
                    w = jnp.exp(s - sp + r_ref[hh] + _running_sum(ls, u_ref[...]))
                    if masked:
                        w = jnp.where(valid, w, 0.0)
                    acc_ref[hh] += jnp.dot(w.astype(BF16), vb, preferred_element_type=F32)
                    r_ref[hh] += jnp.sum(ls, axis=1, keepdims=True)
                else:
                    if masked:
                        s = jnp.where(valid, s, -1e30)
                    m_old = m_ref[hh]
                    m_new = jnp.maximum(m_old, jnp.max(s, axis=1, keepdims=True))
                    p = jnp.exp(s - m_new)
                    alpha = jnp.exp(m_old - m_new)
                    if sum_lane is None:
                        l_ref[hh] = alpha * l_ref[hh] + jnp.sum(p, axis=1, keepdims=True)
                    acc_ref[hh] = alpha * acc_ref[hh] + jnp.dot(p.astype(BF16), vb, preferred_element_type=F32)
                    m_ref[hh] = m_new

        if sb:
            s_cur = scores(nblk - 1)
            for cut in range(n_cut):
                s_next = scores(jnp.maximum(nblk - 2 - cut, 0))
                weigh(nblk - 1 - cut, s_cur, True)
                s_cur = s_next

            def step(t, s_now):
                s_next = scores(jnp.maximum(nfull - 2 - t, 0))
                weigh(nfull - 1 - t, s_now, False)
                return s_next

            lax.fori_loop(0, nfull, step, s_cur)
        else:
            n_loop = nfull if causal else nblk - 1

            def step(t, s_now):
                s_next = scores(jnp.minimum(t + 1, nblk - 1))
                weigh(t, s_now, False)
                return s_next

            s_cur = lax.fori_loop(0, n_loop, step, scores(0))
            tail = n_cut if causal else 1
            for last in range(tail):
                s_next = scores(n_loop + last + 1) if last + 1 < tail else None
                weigh(n_loop + last, s_cur, causal)
                s_cur = s_next
        for hh in members:
            cols = slice(hh * dv, (hh + 1) * dv)
            if sb:
                o_ref[:, cols] = acc_ref[hh]
                lse_ref[hh] = r_ref[hh]
            else:
                acc = acc_ref[hh]
                l = l_ref[hh] if sum_lane is None else acc[:, sum_lane:sum_lane + 1]
                o_ref[:, cols] = acc / l
                lse_ref[hh] = m_ref[hh] + jnp.log(l)

    in_specs = [pl.BlockSpec((tq, group * dq), lambda g, i: (i, g)),
                pl.BlockSpec((Sk, group * dq), lambda g, i: (0, kcol(g))),
                pl.BlockSpec((Sk, group * dv), lambda g, i: (0, vcol(g)))]
    ins = [q, k, v]
    scratch = [pltpu.VMEM((group, tq, dv), F32), pltpu.VMEM((group, tq, 1), F32)]
    if sb:
        ins.append(_triangle(sub, inclusive_prefix=False))
        in_specs.append(pl.BlockSpec((sub, sub), lambda g, i: (0, 0)))
    else:
        scratch.append(pltpu.VMEM((group, tq, 1), F32))
    out_specs = [pl.BlockSpec((tq, group * dv), lambda g, i: (i, g)), pl.BlockSpec((group, tq, 1), lambda g, i: (g, i, 0))]
    out_shape = [jax.ShapeDtypeStruct((S, heads * dv), F32), jax.ShapeDtypeStruct((heads, S, 1), F32)]
    return pl.pallas_call(body, name=name, grid=(heads // group, S // tq), in_specs=in_specs, out_specs=out_specs,
                          out_shape=out_shape, scratch_shapes=scratch, compiler_params=_params(2))(*ins)


def _attn_bwd(q, k, v, o, do, lse, *, sb, causal, heads, dq, dv, kcol=None, vcol=None, name):
    S, Sk = q.shape[0], k.shape[0]
    tq, tk = min(ATT_TQ, S), min(ATT_TK, Sk)
    sub = min(SB_SUB, tk) if sb else tk
    assert tq % sub == 0 or not causal
    nq = S // tq
    kcol = kcol or (lambda h: h)
    vcol = vcol or (lambda h: h)

    def body(*refs):
        if sb:
            q_ref, k_ref, v_ref, o_ref, do_ref, lse_ref, u_ref, dq_ref, dk_ref, dv_ref, acc_ref, r_ref, re_ref = refs
            r_ref[...] = jnp.zeros_like(r_ref)
            re_ref[...] = jnp.zeros_like(re_ref)
        else:
            q_ref, k_ref, v_ref, o_ref, do_ref, lse_ref, dq_ref, dk_ref, dv_ref, acc_ref = refs
        first_row = pl.program_id(1) * tq

        @pl.when(first_row == 0)
        def _():
            dk_ref[...] = jnp.zeros_like(dk_ref)
            dv_ref[...] = jnp.zeros_like(dv_ref)

        qb = q_ref[...]
        dof = do_ref[...].astype(F32)
        dob = dof.astype(BF16)
        q_t, do_t = qb.T, dob.T
        if not sb:
            dlt = jnp.sum(dof * o_ref[...], axis=1, keepdims=True)
        acc_ref[...] = jnp.zeros_like(acc_ref)
        nblk = (first_row + tq) // sub if causal else Sk // sub
        nfull = (first_row + (0 if sb else 1)) // sub if causal else nblk
        n_cut = tq // sub if causal else 0

        def products(jj):
            off = pl.multiple_of(jj * sub, sub)
            return _nt(qb, k_ref[pl.ds(off, sub), :]), _nt(dob, v_ref[pl.ds(off, sub), :])

        def piece(jj, now, masked):
            off = pl.multiple_of(jj * sub, sub)
            kb = k_ref[pl.ds(off, sub), :]
            s, dp = now
            if masked:
                qpos = first_row + lax.broadcasted_iota(jnp.int32, (tq, sub), 0)
                kpos = off + lax.broadcasted_iota(jnp.int32, (tq, sub), 1)
                valid = (kpos < qpos) if sb else (kpos <= qpos)
            if sb:
                u = u_ref[...]
                sp = _softplus(s)
                ls = jnp.where(valid, -sp, 0.0) if masked else -sp
                lb = s - sp
                w = jnp.exp(lb + (lse_ref[0] - (r_ref[...] + _running_sum(ls, u))))
                if masked:
                    w = jnp.where(valid, w, 0.0)
                e = dp * w
                ds = e - jnp.exp(lb) * (re_ref[...] + _running_sum(e, u, split=False))
                if masked:
                    ds = jnp.where(valid, ds, 0.0)
                r_ref[...] += jnp.sum(ls, axis=1, keepdims=True)
                re_ref[...] += jnp.sum(e, axis=1, keepdims=True)
            else:
                w = jnp.exp(s - lse_ref[0])
                if masked:
                    w = jnp.where(valid, w, 0.0)
                ds = w * (dp - dlt)
            dsb = ds.astype(BF16)
            dv_ref[:, pl.ds(off, sub)] += jnp.dot(do_t, w.astype(BF16), preferred_element_type=F32)
            dk_ref[:, pl.ds(off, sub)] += jnp.dot(q_t, dsb, preferred_element_type=F32)
            acc_ref[...] += jnp.dot(dsb, kb, preferred_element_type=F32)

        n_loop = nfull if causal else nblk - 1
        per_trip = tk // sub

        def steps(first, count, masked):
            ready = [products(first + c) for c in range(count)]
            for c in range(count):
                piece(first + c, ready[c], masked)

        def trip(t, carry):
            steps(t * per_trip, per_trip, False)
            return carry

        lax.fori_loop(0, n_loop // per_trip, trip, 0)
        steps(n_loop, n_cut if causal else 1, causal)
        dq_ref[...] = acc_ref[...]

    ins = [q, k, v, o, do]
    in_specs = [pl.BlockSpec((tq, dq), lambda h, i: (i, h)),
                pl.BlockSpec((Sk, dq), lambda h, i: (0, kcol(h))),
                pl.BlockSpec((Sk, dv), lambda h, i: (0, vcol(h))),
                pl.BlockSpec((tq, dv), lambda h, i: (i, h)),
                pl.BlockSpec((tq, dv), lambda h, i: (i, h))]
    scratch = [pltpu.VMEM((tq, dq), F32)]
    ins.append(lse)
    in_specs.append(pl.BlockSpec((1, tq, 1), lambda h, i: (h, i, 0)))
    if sb:
        ins.append(_triangle(sub, inclusive_prefix=True))
        in_specs.append(pl.BlockSpec((sub, sub), lambda h, i: (0, 0)))
        scratch += [pltpu.VMEM((tq, 1), F32), pltpu.VMEM((tq, 1), F32)]
    out_specs = [pl.BlockSpec((tq, dq), lambda h, i: (i, h)),
                 pl.BlockSpec((dq, Sk), lambda h, i: (h, 0)),
                 pl.BlockSpec((dv, Sk), lambda h, i: (h, 0))]
    out_shape = [jax.ShapeDtypeStruct((S, heads * dq), F32), jax.ShapeDtypeStruct((heads * dq, Sk), F32),
                 jax.ShapeDtypeStruct((heads * dv, Sk), F32)]
    return pl.pallas_call(body, name=name, grid=(heads, nq), in_specs=in_specs, out_specs=out_specs,
                          out_shape=out_shape, scratch_shapes=scratch, compiler_params=_params(2))(*ins)


GELU_C = 0.7978845608028654
assert 2 * SG_GD == LANE and SG_CHUNK == LANE


def _gelu(z):
    t = jnp.tanh(GELU_C * (z + 0.044715 * z * z * z))
    return 0.5 * z * (1.0 + t), t


def _gelu_grad(z, t):
    return 0.5 * (1.0 + t) + 0.5 * z * (1.0 - t * t) * GELU_C * (1.0 + 3.0 * 0.044715 * z * z)


def _layernorm_parts(g):
    d = g - jnp.mean(g, axis=-1, keepdims=True)
    rstd = lax.rsqrt(jnp.mean(d * d, axis=-1, keepdims=True) + EPS)
    return d * rstd, rstd


def _gelu_ln(z, gain, bias, *, name):
    def fn(zv, gn, bs):
        a, _ = _gelu(zv)
        y, _ = _layernorm_parts(a[:, SG_W:])
        return a[:, :SG_W], y * gn + bs

    return _rowwise(fn, [z], [gain.reshape(1, SG_W), bias.reshape(1, SG_W)], [(SG_W, F32), (SG_W, BF16)], name=name)


def _gelu_ln_bwd(z, du, dgl, gain, *, name):
    def fn(zv, duv, dglv, gn):
        a, t = _gelu(zv)
        y, rstd = _layernorm_parts(a[:, SG_W:])
        dy = dglv * gn
        dgg = rstd * (dy - jnp.mean(dy, axis=-1, keepdims=True) - y * jnp.mean(dy * y, axis=-1, keepdims=True))
        dz = jnp.concatenate([duv, dgg], axis=1) * _gelu_grad(zv, t)
        return dz, jnp.sum(dglv * y, axis=0, keepdims=True), jnp.sum(dglv, axis=0, keepdims=True)

    return _rowwise(fn, [z, du, dgl], [gain.reshape(1, SG_W)], [(2 * SG_W, BF16)], [(1, SG_W), (1, SG_W)], name=name)


def _sg_masks():
    tri = lax.broadcasted_iota(jnp.int32, (SG_CHUNK, SG_CHUNK), 0) >= lax.broadcasted_iota(jnp.int32, (SG_CHUNK, SG_CHUNK), 1)
    first = lax.broadcasted_iota(jnp.int32, (SG_CHUNK, LANE), 1) < SG_GD
    return tri, first


def _spatial(gl, u, w, bt, *, name):
    S = gl.shape[0]
    tm = _row_tile(S, 512)
    nch = tm // SG_CHUNK

    def body(gl_ref, u_ref, w_ref, bt_ref, o_ref):
        tri, first = _sg_masks()
        for p in range(SG_W // LANE):
            cols = slice(p * LANE, (p + 1) * LANE)
            wa = jnp.where(tri, w_ref[2 * p], 0.0).astype(BF16)
            wb = jnp.where(tri, w_ref[2 * p + 1], 0.0).astype(BF16)
            for ci in range(nch):
                rws = slice(ci * SG_CHUNK, (ci + 1) * SG_CHUNK)
                g = gl_ref[rws, cols]
                zero = jnp.zeros_like(g)
                mixed = (jnp.dot(wa, jnp.where(first, g, zero), preferred_element_type=F32)
                         + jnp.dot(wb, jnp.where(first, zero, g), preferred_element_type=F32) + bt_ref[:, cols])
                o_ref[rws, cols] = u_ref[rws, cols] * mixed

    row = pl.BlockSpec((tm, SG_W), lambda i: (i, 0))
    return pl.pallas_call(
        body, name=name, grid=(S // tm,),
        in_specs=[row, row, pl.BlockSpec(w.shape, lambda i: (0, 0, 0)), pl.BlockSpec(bt.shape, lambda i: (0, 0))],
        out_specs=row, out_shape=jax.ShapeDtypeStruct((S, SG_W), F32), compiler_params=_params(1))(gl, u, w, bt)


def _spatial_bwd(d_o, gl, u, w, bt, *, name):
    S = gl.shape[0]
    tm = _row_tile(S, 512)
    nch = tm // SG_CHUNK
    nsteps = S // tm

    def body(do_ref, gl_ref, u_ref, w_ref, bt_ref, du_ref, dgl_ref, dw_ref, db_ref, dbt_ref):
        tri, first = _sg_masks()
        step = pl.program_id(0)

        @pl.when(step == 0)
        def _():
            dw_ref[...] = jnp.zeros_like(dw_ref)
            dbt_ref[...] = jnp.zeros_like(dbt_ref)

        for p in range(SG_W // LANE):
            cols = slice(p * LANE, (p + 1) * LANE)
            wa = jnp.where(tri, w_ref[2 * p], 0.0).astype(BF16)
            wb = jnp.where(tri, w_ref[2 * p + 1], 0.0).astype(BF16)
            for ci in range(nch):
                rws = slice(ci * SG_CHUNK, (ci + 1) * SG_CHUNK)
                g = gl_ref[rws, cols]
                zero = jnp.zeros_like(g)
                mixed = (jnp.dot(wa, jnp.where(first, g, zero), preferred_element_type=F32)
                         + jnp.dot(wb, jnp.where(first, zero, g), preferred_element_type=F32) + bt_ref[:, cols])
                dov = do_ref[rws, cols]
                du_ref[rws, cols] = dov * mixed
                dm = dov * u_ref[rws, cols]
                dbt_ref[:, cols] += dm
                dma = jnp.where(first, dm, 0.0).astype(BF16)
                dmb = jnp.where(first, 0.0, dm).astype(BF16)
                dw_ref[2 * p] += jnp.where(tri, _nt(dma, g), 0.0)
                dw_ref[2 * p + 1] += jnp.where(tri, _nt(dmb, g), 0.0)
                dgl_ref[rws, cols] = _tn(wa, dma) + _tn(wb, dmb)

        @pl.when(step == nsteps - 1)
        def _():
            lane = lax.broadcasted_iota(jnp.int32, (SG_CHUNK, LANE), 1)
            acc = jnp.zeros((SG_CHUNK, LANE), F32)
            for p in range(SG_W // LANE):
                blk = dbt_ref[:, p * LANE:(p + 1) * LANE]
                sa = jnp.sum(jnp.where(first, blk, 0.0), axis=1, keepdims=True)
                sb_ = jnp.sum(jnp.where(first, 0.0, blk), axis=1, keepdims=True)
                acc = acc + jnp.where(lane == 2 * p, sa, 0.0) + jnp.where(lane == 2 * p + 1, sb_, 0.0)
            db_ref[...] = acc

    row = pl.BlockSpec((tm, SG_W), lambda i: (i, 0))
    return pl.pallas_call(
        body, name=name, grid=(nsteps,),
        in_specs=[row, row, row, pl.BlockSpec(w.shape, lambda i: (0, 0, 0)), pl.BlockSpec(bt.shape, lambda i: (0, 0))],
        out_specs=[row, row, pl.BlockSpec(w.shape, lambda i: (0, 0, 0)), pl.BlockSpec((SG_CHUNK, LANE), lambda i: (0, 0))],
        out_shape=[jax.ShapeDtypeStruct((S, SG_W), F32), jax.ShapeDtypeStruct((S, SG_W), F32),
                   jax.ShapeDtypeStruct(w.shape, F32), jax.ShapeDtypeStruct((SG_CHUNK, LANE), F32)],
        scratch_shapes=[pltpu.VMEM((SG_CHUNK, SG_W), F32)], compiler_params=_params(1))(d_o, gl, u, w, bt)


ROPE_HALF = MLA_ROPE // 2
KR_COL = (MLA_QL + MLA_KVL) // LANE
MLA_IN_PAD = MLA_QL + MLA_KVL + LANE


def _rope_tables(positions):
    inv_freq = ROPE_THETA ** (-jnp.arange(ROPE_HALF, dtype=F32) / ROPE_HALF)
    ang = positions.astype(F32)[:, None] * inv_freq
    cos, sin = jnp.cos(ang), jnp.sin(ang)
    S = positions.shape[0]
    z16, tail = jnp.zeros((S, ROPE_HALF), F32), jnp.zeros((S, LANE - MLA_QK), F32)
    ones = jnp.ones((S, MLA_NOPE), F32)
    zeros = jnp.zeros((S, MLA_NOPE), F32)
    return (jnp.concatenate([ones, cos, cos, tail], axis=1), jnp.concatenate([zeros, z16, sin, tail], axis=1),
            jnp.concatenate([zeros, -sin, z16, tail], axis=1))


def _rope(x, cos, sa, sb):
    return x * cos + pltpu.roll(x, ROPE_HALF, 1) * sa + pltpu.roll(x, LANE - ROPE_HALF, 1) * sb


def _rope_t(dy, cos, sa, sb):
    return dy * cos + pltpu.roll(dy * sa, LANE - ROPE_HALF, 1) + pltpu.roll(dy * sb, ROPE_HALF, 1)


def _mla_lora(P, qlg, kvlg, *, name):
    def fn(pv, a, b):
        return _rmsnorm_fwd(pv[:, :MLA_QL], a), _rmsnorm_fwd(pv[:, MLA_QL:MLA_QL + MLA_KVL], b)

    return _rowwise(fn, [P], [qlg.reshape(1, MLA_QL), kvlg.reshape(1, MLA_KVL)], [(MLA_QL, BF16), (MLA_KVL, BF16)], name=name)


def _mla_lora_bwd(dcq, dckv, dkr, P, qlg, kvlg, *, name):
    def fn(d1, d2, d3, pv, a, b):
        x1, g1 = _rmsnorm_bwd(d1, pv[:, :MLA_QL], a)
        x2, g2 = _rmsnorm_bwd(d2, pv[:, MLA_QL:MLA_QL + MLA_KVL], b)
        return jnp.concatenate([x1, x2, d3], axis=1), g1, g2

    return _rowwise(fn, [dcq, dckv, dkr, P], [qlg.reshape(1, MLA_QL), kvlg.reshape(1, MLA_KVL)], [(MLA_IN_PAD, BF16)],
                    [(1, MLA_QL), (1, MLA_KVL)], name=name)


def _mla_qk(q_pre, k_pre, P, tabs, qg, kg, *, name):
    def fn(qp, kp, kr, c, a, b, qgv, kgv):
        return (_rope(_rmsnorm_fwd(qp, qgv, MLA_QK), c, a, b) * MLA_SCALE,
                _rope(_rmsnorm_fwd(kp + kr, kgv, MLA_QK), c, a, b))

    hcol = lambda h: h
    rows = [(q_pre, LANE, hcol), (k_pre, LANE, hcol), (P, LANE, lambda h: KR_COL), *tabs]
    w = MLA_HEADS * LANE
    return _rowwise(fn, rows, [qg, kg], [(w, BF16), (w, BF16)], heads=MLA_HEADS, tm=HEAD_ROWS, name=name)


def _mla_qk_bwd(dq, dk, q_pre, k_pre, P, tabs, qg, kg, *, name):
    def fn(dqv, dkv, qp, kp, kr, c, a, b, qgv, kgv):
        dqp, dqg = _rmsnorm_bwd(_rope_t(dqv * MLA_SCALE, c, a, b), qp, qgv, MLA_QK)
        dkp, dkg = _rmsnorm_bwd(_rope_t(dkv, c, a, b), kp + kr, kgv, MLA_QK)
        lane = lax.broadcasted_iota(jnp.int32, (1, LANE), 1)
        return dqp, dkp, jnp.where((lane >= MLA_NOPE) & (lane < MLA_QK), dkp, 0.0), dqg, dkg

    hcol = lambda h: h
    rows = [(dq, LANE, hcol), (dk, LANE, hcol), (q_pre, LANE, hcol), (k_pre, LANE, hcol), (P, LANE, lambda h: KR_COL), *tabs]
    w = MLA_HEADS * LANE
    return _rowwise(fn, rows, [qg, kg], [(w, BF16), (w, BF16)], [(1, LANE), (1, LANE)], [LANE], heads=MLA_HEADS,
                    tm=HEAD_ROWS, name=name)


def _head_norm(x, g, *, heads, width, colfn=None, scale=1.0, name):
    return _rowwise(lambda xv, gv: _rmsnorm_fwd(xv, gv) * scale, [(x, width, colfn or (lambda h: h))],
                    [g.reshape(1, width)], [(heads * width, BF16)], heads=heads, tm=HEAD_ROWS, name=name)[0]


def _head_norm_bwd(dy, x, g, *, heads, width, colfn=None, scale=1.0, out_dtype, name):
    return _rowwise(lambda dv_, xv, gv: _rmsnorm_bwd(dv_ * scale, xv, gv),
                    [(dy, width, lambda h: h), (x, width, colfn or (lambda h: h))],
                    [g.reshape(1, width)], [(heads * width, out_dtype)], [(1, width)], heads=heads, tm=HEAD_ROWS,
                    name=name)


def _loss_grad(y, tgt, *, name):
    D = y.shape[1]

    def fn(yv, tv):
        d = yv - tv
        return d * (1.0 / D), jnp.sum(d * d, axis=0, keepdims=True) * (0.5 / D)

    dy, part = _rowwise(fn, [y, tgt], [], [(D, F32)], [(1, D)], name=name)
    return jnp.sum(part), dy


def _adamw(w, g, m, v, *, name):
    shape = w.shape
    two_d = (-1, shape[-1])

    def fn(wv, gv, mv, vv):
        m2 = ADAM_B1 * mv + (1.0 - ADAM_B1) * gv
        v2 = ADAM_B2 * vv + (1.0 - ADAM_B2) * (gv * gv)
        m_hat = m2 / (1.0 - ADAM_B1 ** ADAM_STEP)
        v_hat = v2 / (1.0 - ADAM_B2 ** ADAM_STEP)
        return -ADAM_LR * (m_hat / (jnp.sqrt(v_hat) + ADAM_EPS) + ADAM_WD * wv), m2, v2

    outs = _rowwise(fn, [t.reshape(two_d) for t in (w, g, m, v)], [], [(shape[-1], F32)] * 3, name=name)
    return [o.reshape(shape) for o in outs]


def _pad_cols(w, heads, hd):
    k = w.shape[0]
    return jnp.pad(w.reshape(k, heads, hd), ((0, 0), (0, 0), (0, LANE - hd))).reshape(k, heads * LANE)


def _unpad_cols(w, heads, hd):
    k = w.shape[0]
    return w.reshape(k, heads, LANE)[:, :, :hd].reshape(k, heads * hd)


def _pad_rows(w, heads, hd):
    n = w.shape[1]
    return jnp.pad(w.reshape(heads, hd, n), ((0, 0), (0, LANE - hd), (0, 0))).reshape(heads * LANE, n)


def _unpad_rows(w, heads, hd):
    n = w.shape[1]
    return w.reshape(heads, LANE, n)[:, :hd, :].reshape(heads * hd, n)


def _ffn_fwd(x, g, wgu, wd, tag):
    h = _norm_rows(x, g, name=tag + "_norm")
    gate, up, act = _mm_swiglu(h, wgu, name=tag + "_gu")
    y = _mm(act, wd, scale=0.5, residual=x, name=tag + "_down")
    return y, (x, h, gate, up, act)


def _ffn_bwd(dy, saved, g, wgu, wd, tag):
    x, h, gate, up, act = saved
    F = wd.shape[0]
    dwd = _mm(act, dy, ta=True, scale=0.5, name=tag + "_dwd")
    dgate, dup = _mm_dswiglu(dy, wd, gate, up, scale=0.5, name=tag + "_dact")
    dh = _mm(dgate, wgu, tb=True, name=tag + "_dh_g")
    dx, dg = _mm(dup, wgu, tb=True, b_off=(0, F), residual=dh, norm_bwd=(x, g, dy), name=tag + "_dh_u")
    dwgu = jnp.concatenate([_mm(h, dgate, ta=True, name=tag + "_dwg"), _mm(h, dup, ta=True, name=tag + "_dwu")], axis=1)
    return dx, dg, dwgu, dwd


def _even_weights(w_in, w_out):
    parts = [w_in[:, :SB_W] * SB_SCALE, w_in[:, SB_W:2 * SB_W], w_in[:, 2 * SB_W:3 * SB_W]]
    wqkv = jnp.concatenate([_pad_cols(p, SB_HEADS, SB_HD) for p in parts], axis=1)
    return wqkv, w_in[:, 3 * SB_W:], _pad_rows(w_out[:SB_W], SB_HEADS, SB_HD), w_out[SB_W:]


def _even_fwd(x, g, wts, ln_g, ln_b, sgu_w, bt, tag):
    wqkv, wz, wo_sb, wo_sg = wts
    h = _norm_rows(x, g, name=tag + "_norm")
    qkv = _mm(h, wqkv, out_dtype=BF16, name=tag + "_qkv")
    z = _mm(h, wz, name=tag + "_z")
    o_sb, tot = _attn_fwd(qkv, qkv, qkv, sb=True, causal=True, heads=SB_HEADS, dq=LANE, dv=LANE, group=FWD_GROUP,
                          kcol=lambda g: SB_HEADS // FWD_GROUP + g, vcol=lambda g: 2 * SB_HEADS // FWD_GROUP + g,
                          name=tag + "_sb")
    u, gl = _gelu_ln(z, ln_g, ln_b, name=tag + "_geluln")
    o_sg = _spatial(gl, u, sgu_w, bt, name=tag + "_sgu")
    y = _mm(o_sb, wo_sb, residual=x, name=tag + "_out_sb")
    y = _mm(o_sg, wo_sg, residual=y, name=tag + "_out_sg")
    return y, (x, h, qkv, z, o_sb, tot, u, gl, o_sg)


def _even_bwd(dy, saved, g, wts, ln_g, sgu_w, bt, tag):
    wqkv, wz, wo_sb, wo_sg = wts
    x, h, qkv, z, o_sb, tot, u, gl, o_sg = saved
    do_sb = _mm(dy, wo_sb, tb=True, name=tag + "_do_sb")
    do_sg = _mm(dy, wo_sg, tb=True, name=tag + "_do_sg")
    dwo = jnp.concatenate([_unpad_rows(_mm(o_sb, dy, ta=True, name=tag + "_dwo_sb"), SB_HEADS, SB_HD),
                           _mm(o_sg, dy, ta=True, name=tag + "_dwo_sg")], axis=0)
    dq, dk_t, dv_t = _attn_bwd(qkv, qkv, qkv, o_sb, do_sb, tot, sb=True, causal=True, heads=SB_HEADS, dq=LANE, dv=LANE,
                               kcol=lambda hh: SB_HEADS + hh, vcol=lambda hh: 2 * SB_HEADS + hh, name=tag + "_sb_bwd")
    du, dgl, dsgu_w, db_t = _spatial_bwd(do_sg, gl, u, sgu_w, bt, name=tag + "_sgu_bwd")
    dz, dln_g, dln_b = _gelu_ln_bwd(z, du, dgl, ln_g, name=tag + "_geluln_bwd")
    dh = _mm(dz, wz, tb=True, name=tag + "_dh_z")
    dh = _mm(dq, wqkv, tb=True, residual=dh, name=tag + "_dh_q")
    dws = [_unpad_cols(_mm(h, dq, ta=True, scale=SB_SCALE, name=tag + "_dw_q"), SB_HEADS, SB_HD)]
    for i, (d_t, nm) in enumerate(((dk_t, "k"), (dv_t, "v")), start=1):
        dh = _mm(d_t, wqkv, ta=True, tb=True, b_off=(0, i * SB_HEADS * LANE), residual=dh,
                 norm_bwd=(x, g, dy) if nm == "v" else None, name=tag + "_dh_" + nm)
        dws.append(_unpad_rows(_mm(d_t, h, name=tag + "_dw_" + nm), SB_HEADS, SB_HD).T)
    dws.append(_mm(h, dz, ta=True, name=tag + "_dw_z"))
    dx, dg = dh
    return dx, dict(mix_norm=dg, sbg_w_in=jnp.concatenate(dws, axis=1), sgu_ln_gain=dln_g, sgu_ln_bias=dln_b,
                    sgu_w=dsgu_w, sgu_b=db_t[:, :SG_GROUPS].T, sbg_w_out=dwo)


def _mla_weights(w_in, w_uq, w_ukv, w_out, q_gain, k_gain):
    d = w_in.shape[0]
    lat = MLA_QL + MLA_KVL
    w_in_ext = jnp.concatenate([w_in[:, :lat], jnp.zeros((d, MLA_NOPE), w_in.dtype), w_in[:, lat:],
                                jnp.zeros((d, LANE - MLA_QK), w_in.dtype)], axis=1)
    kv = w_ukv.reshape(MLA_KVL, MLA_HEADS, MLA_NOPE + MLA_V)
    wk = _pad_cols(kv[:, :, :MLA_NOPE].reshape(MLA_KVL, -1), MLA_HEADS, MLA_NOPE)
    wv = _pad_cols(kv[:, :, MLA_NOPE:].reshape(MLA_KVL, -1), MLA_HEADS, MLA_V)
    pad_gain = lambda gn: jnp.pad(gn.reshape(1, MLA_QK), ((0, 0), (0, LANE - MLA_QK)))
    return (w_in_ext, _pad_cols(w_uq, MLA_HEADS, MLA_QK), wk, wv, _pad_rows(w_out, MLA_HEADS, MLA_V),
            pad_gain(q_gain), pad_gain(k_gain))


def _mla_fwd(x, g, wts, qlg, kvlg, tabs, tag):
    w_in, w_uq, wk, wv, w_out, qg, kg = wts
    h = _norm_rows(x, g, name=tag + "_norm")
    P = _mm(h, w_in, name=tag + "_in")
    cqn, ckvn = _mla_lora(P, qlg, kvlg, name=tag + "_lora")
    q_pre = _mm(cqn, w_uq, name=tag + "_uq")
    k_pre = _mm(ckvn, wk, name=tag + "_uk")
    ones_lane = jnp.tile((jnp.arange(LANE) == MLA_V).astype(F32), MLA_HEADS)[None, :]
    v = _mm(ckvn, wv, out_dtype=BF16, bias=ones_lane, name=tag + "_uv")
    q, k = _mla_qk(q_pre, k_pre, P, tabs, qg, kg, name=tag + "_qk")
    o, lse = _attn_fwd(q, k, v, sb=False, causal=True, heads=MLA_HEADS, dq=LANE, dv=LANE, group=FWD_GROUP,
                       sum_lane=MLA_V, name=tag + "_attn")
    y = _mm(o, w_out, residual=x, name=tag + "_out")
    return y, (x, h, P, cqn, ckvn, q_pre, k_pre, q, k, v, o, lse)


def _mla_bwd(dy, saved, g, wts, qlg, kvlg, tabs, tag):
    w_in, w_uq, wk, wv, w_out, qg, kg = wts
    x, h, P, cqn, ckvn, q_pre, k_pre, q, k, v, o, lse = saved
    do = _mm(dy, w_out, tb=True, name=tag + "_do")
    dw_out = _unpad_rows(_mm(o, dy, ta=True, name=tag + "_dwo"), MLA_HEADS, MLA_V)
    dq, dk_t, dv_t = _attn_bwd(q, k, v, o, do, lse, sb=False, causal=True, heads=MLA_HEADS, dq=LANE, dv=LANE,
                               name=tag + "_attn_bwd")
    dq_pre, dk_pre, dkr, dqg, dkg = _mla_qk_bwd(dq, dk_t.T, q_pre, k_pre, P, tabs, qg, kg, name=tag + "_qk_bwd")
    dcqn = _mm(dq_pre, w_uq, tb=True, name=tag + "_dcq")
    dckvn = _mm(dk_pre, wk, tb=True, name=tag + "_dckv_k")
    dckvn = _mm(dv_t, wv, ta=True, tb=True, residual=dckvn, name=tag + "_dckv_v")
    dw_uq = _unpad_cols(_mm(cqn, dq_pre, ta=True, name=tag + "_dwuq"), MLA_HEADS, MLA_QK)
    dwk = _unpad_cols(_mm(ckvn, dk_pre, ta=True, name=tag + "_dwk"), MLA_HEADS, MLA_NOPE)
    dwv = _unpad_rows(_mm(dv_t, ckvn, name=tag + "_dwv"), MLA_HEADS, MLA_V).T
    dw_ukv = jnp.concatenate([dwk.reshape(MLA_KVL, MLA_HEADS, MLA_NOPE), dwv.reshape(MLA_KVL, MLA_HEADS, MLA_V)],
                             axis=2).reshape(MLA_KVL, -1)
    dP, dqlg, dkvlg = _mla_lora_bwd(dcqn, dckvn, dkr, P, qlg, kvlg, name=tag + "_lora_bwd")
    dx, dg = _mm(dP, w_in, tb=True, norm_bwd=(x, g, dy), name=tag + "_dh")
    dw_in_ext = _mm(h, dP, ta=True, name=tag + "_dwin")
    lat = MLA_QL + MLA_KVL
    dw_in = jnp.concatenate([dw_in_ext[:, :lat], dw_in_ext[:, lat + MLA_NOPE:lat + MLA_QK]], axis=1)
    return dx, dict(mix_norm=dg, mla_w_in=dw_in, mla_q_lora_gain=dqlg, mla_kv_lora_gain=dkvlg, mla_w_uq=dw_uq,
                    mla_w_ukv=dw_ukv, mla_q_gain=dqg[:, :MLA_QK], mla_k_gain=dkg[:, :MLA_QK], mla_w_out=dw_out)


def _xmem_fwd(x, mem, g, gm, wq, wkv, qg, kg, wo, tag):
    hq = _norm_rows(x, g, name=tag + "_norm")
    hm = _norm_rows(mem, gm, name=tag + "_mnorm")
    qp = _mm(hq, wq, name=tag + "_q")
    kv = _mm(hm, wkv, name=tag + "_kv")
    q = _head_norm(qp, qg, heads=MEM_HEADS, width=MEM_HD, scale=MEM_SCALE, name=tag + "_qn")
    kn = _head_norm(kv, kg, heads=MEM_HEADS, width=MEM_HD, colfn=lambda hh: 2 * hh, name=tag + "_kn")
    kvb = kv.reshape(-1, MEM_HEADS, 2, MEM_HD)[:, :, 1].reshape(-1, MEM_HEADS * MEM_HD).astype(BF16)
    o, lse = _attn_fwd(q, kn, kvb, sb=False, causal=False, heads=MEM_HEADS, dq=MEM_HD, dv=MEM_HD, group=MEM_HEADS,
                       name=tag + "_attn")
    y = _mm(o, wo, residual=x, name=tag + "_out")
    return y, (x, hq, hm, qp, kv, q, kn, kvb, o, lse)


def _xmem_bwd(dy, saved, mem, g, gm, wq, wkv, qg, kg, wo, tag):
    x, hq, hm, qp, kv, q, kn, kvb, o, lse = saved
    m = mem.shape[0]
    do = _mm(dy, wo, tb=True, name=tag + "_do")
    dwo = _mm(o, dy, ta=True, name=tag + "_dwo")
    dq, dk_t, dv_t = _attn_bwd(q, kn, kvb, o, do, lse, sb=False, causal=False, heads=MEM_HEADS, dq=MEM_HD, dv=MEM_HD,
                               name=tag + "_attn_bwd")
    dk, dv = dk_t.T, dv_t.T
    dqp, dqg = _head_norm_bwd(dq, qp, qg, heads=MEM_HEADS, width=MEM_HD, scale=MEM_SCALE, out_dtype=BF16,
                              name=tag + "_qn_bwd")
    dkp, dkg = _head_norm_bwd(dk, kv, kg, heads=MEM_HEADS, width=MEM_HD, colfn=lambda hh: 2 * hh, out_dtype=F32,
                              name=tag + "_kn_bwd")
    dkv = jnp.concatenate([dkp.reshape(m, MEM_HEADS, MEM_HD), dv.reshape(m, MEM_HEADS, MEM_HD)], axis=2).reshape(m, -1)
    dwkv = _mm(hm, dkv, ta=True, name=tag + "_dwkv")
    dhm = _mm(dkv, wkv, tb=True, name=tag + "_dhm")
    _, dgm = _norm_rows_bwd(dhm, mem, gm, None, name=tag + "_dmnorm")
    dwq = _mm(hq, dqp, ta=True, name=tag + "_dwq")
    dx, dg = _mm(dqp, wq, tb=True, norm_bwd=(x, g, dy), name=tag + "_dhq")
    return dx, dict(xmem_norm=dg, xmem_mem_norm=dgm, xmem_wq=dwq, xmem_wkv=dwkv, xmem_q_gain=dqg, xmem_k_gain=dkg,
                    xmem_wo=dwo)


def _local_step(x, mem, positions, tgt, w):
    tabs = _rope_tables(positions)
    even = _even_weights(w["sbg_w_in"][0], w["sbg_w_out"][0])
    mla = _mla_weights(w["mla_w_in"][0], w["mla_w_uq"][0], w["mla_w_ukv"][0], w["mla_w_out"][0], w["mla_q_gain"][0],
                       w["mla_k_gain"][0])
    bt = jnp.repeat(w["sgu_b"][0].T, SG_GD, axis=1)
    saved = []
    for l in range(2):
        t = f"l{l}"
        x, s_pre = _ffn_fwd(x, w["ffn_pre_norm"][l], w["ffn_pre_w_gu"][l], w["ffn_pre_w_down"][l], t + "_pre")
        if l == 0:
            x, s_mix = _even_fwd(x, w["mix_norm"][0], even, w["sgu_ln_gain"][0], w["sgu_ln_bias"][0], w["sgu_w"][0], bt,
                                 t + "_even")
        else:
            x, s_mix = _mla_fwd(x, w["mix_norm"][1], mla, w["mla_q_lora_gain"][0], w["mla_kv_lora_gain"][0], tabs,
                                t + "_mla")
        x, s_xm = _xmem_fwd(x, mem, w["xmem_norm"][l], w["xmem_mem_norm"][l], w["xmem_wq"][l], w["xmem_wkv"][l],
                            w["xmem_q_gain"][l], w["xmem_k_gain"][l], w["xmem_wo"][l], t + "_xm")
        x, s_post = _ffn_fwd(x, w["ffn_post_norm"][l], w["ffn_post_w_gu"][l], w["ffn_post_w_down"][l], t + "_post")
        saved.append((s_pre, s_mix, s_xm, s_post))
    loss, dx = _loss_grad(x, tgt, name="loss")
    grads = {}

    def put(name, l, val):
        grads.setdefault(name, {})[l] = val

    for l in (1, 0):
        t = f"l{l}"
        s_pre, s_mix, s_xm, s_post = saved[l]
        dx, dg, dwgu, dwd = _ffn_bwd(dx, s_post, w["ffn_post_norm"][l], w["ffn_post_w_gu"][l], w["ffn_post_w_down"][l],
                                     t + "_post")
        put("ffn_post_norm", l, dg), put("ffn_post_w_gu", l, dwgu), put("ffn_post_w_down", l, dwd)
        dx, gx = _xmem_bwd(dx, s_xm, mem, w["xmem_norm"][l], w["xmem_mem_norm"][l], w["xmem_wq"][l], w["xmem_wkv"][l],
                           w["xmem_q_gain"][l], w["xmem_k_gain"][l], w["xmem_wo"][l], t + "_xm")
        for k_, v_ in gx.items():
            put(k_, l, v_)
        if l == 0:
            dx, gm = _even_bwd(dx, s_mix, w["mix_norm"][0], even, w["sgu_ln_gain"][0], w["sgu_w"][0], bt, t + "_even")
        else:
            dx, gm = _mla_bwd(dx, s_mix, w["mix_norm"][1], mla, w["mla_q_lora_gain"][0], w["mla_kv_lora_gain"][0], tabs,
                              t + "_mla")
        for k_, v_ in gm.items():
            put(k_, l if k_ == "mix_norm" else 0, v_)
        dx, dg, dwgu, dwd = _ffn_bwd(dx, s_pre, w["ffn_pre_norm"][l], w["ffn_pre_w_gu"][l], w["ffn_pre_w_down"][l],
                                     t + "_pre")
        put("ffn_pre_norm", l, dg), put("ffn_pre_w_gu", l, dwgu), put("ffn_pre_w_down", l, dwd)
    return loss, dx, {k_: [v_[l] for l in sorted(v_)] for k_, v_ in grads.items()}


N_CHIPS = 4
PACK_COLS = 1024
PACK_ROW_MULTIPLE = 512


def _place():
    x, y, c = lax.axis_index("x"), lax.axis_index("y"), lax.axis_index("c")
    return x, y, c, [(1 - x, y), (x, 1 - y), (1 - x, 1 - y)]


def _hops(x, y, c):
    return ((x + 1 - c) % 2, (y + c) % 2), ((x + c) % 2, (y + 1 - c) % 2), (1 - x, 1 - y)


def _gather_chips(shard):
    R, C = shard.shape
    Rh = R // 2

    def body(x_ref, out_ref, send_sems, recv_sems):
        x, y, c = lax.axis_index("x"), lax.axis_index("y"), lax.axis_index("c")
        n1, n2, nd = _hops(x, y, c)
        me, q1, q2, qd = 2 * x + y, 2 * n1[0] + n1[1], 2 * n2[0] + n2[1], 2 * nd[0] + nd[1]

        def half(chip, core):
            return out_ref.at[chip, pl.ds(core * Rh, Rh), :]

        def copy(k, chip, core, to, src=None):
            return pltpu.make_async_remote_copy(src_ref=half(chip, core) if src is None else src, dst_ref=half(chip, core),
                                                send_sem=send_sems.at[k], recv_sem=recv_sems.at[k], device_id=to,
                                                device_id_type=MESH)

        own = x_ref.at[pl.ds(c * Rh, Rh), :]
        sibling = (x, y, 1 - c)
        sends = [copy(0, me, c, (*n1, c), src=own), copy(1, me, c, (*n2, c), src=own)]
        sends[0].start()
        sends[1].start()
        copy(0, q1, c, sibling).wait_recv()
        sends += [copy(2, q1, c, (*n2, c)), copy(3, q1, c, sibling)]
        sends[2].start()
        sends[3].start()
        copy(1, q2, c, sibling).wait_recv()
        sends.append(copy(4, q2, c, sibling))
        sends[4].start()
        copy(2, qd, c, sibling).wait_recv()
        sends.append(copy(5, qd, c, sibling))
        sends[5].start()
        copy(3, q2, 1 - c, sibling).wait_recv()
        copy(4, q1, 1 - c, sibling).wait_recv()
        copy(5, qd, 1 - c, sibling).wait_recv()
        for cp in sends:
            cp.wait_send()

    others = pl.pallas_call(
        body, name="gather_weights", out_shape=jax.ShapeDtypeStruct((N_CHIPS, R, C), shard.dtype),
        in_specs=[ANY], out_specs=ANY,
        scratch_shapes=[pltpu.SemaphoreType.DMA((6,)), pltpu.SemaphoreType.DMA((6,))])(shard)
    me = 2 * lax.axis_index("x") + lax.axis_index("y")
    return lax.dynamic_update_slice(others, shard[None], (me, 0, 0))


def _gather_devices(block):
    M, N = block.shape

    def body(x_ref, out_ref, send_sems, recv_sems, local_sem):
        x, y, c, chips = _place()
        me, sibling = (x, y, c), (x, y, 1 - c)

        def rows(px, py, pc):
            return out_ref.at[pl.ds((4 * px + 2 * py + pc) * M, M), :]

        def copy(k, blk, to, src=None):
            return pltpu.make_async_remote_copy(src_ref=rows(*blk) if src is None else src, dst_ref=rows(*blk),
                                                send_sem=send_sems.at[k], recv_sem=recv_sems.at[k], device_id=to,
                                                device_id_type=MESH)

        mine = pltpu.make_async_copy(x_ref, rows(*me), local_sem)
        mine.start()
        first = [copy(0, me, sibling, src=x_ref)]
        first += [copy(1 + j, me, (*chip, c), src=x_ref) for j, chip in enumerate(chips)]
        for cp in first:
            cp.start()
        passed = [copy(4 + j, (*chip, c), sibling) for j, chip in enumerate(chips)]
        for j, chip in enumerate(chips):
            copy(1 + j, (*chip, c), me).wait_recv()
            passed[j].start()
        copy(0, sibling, me).wait_recv()
        for j, chip in enumerate(chips):
            copy(4 + j, (*chip, 1 - c), me).wait_recv()
        for cp in first + passed:
            cp.wait_send()
        mine.wait()

    vmem = pl.BlockSpec(memory_space=pltpu.VMEM)
    return pl.pallas_call(
        body, name=f"gather_devices_{M}", out_shape=jax.ShapeDtypeStruct((8 * M, N), block.dtype),
        in_specs=[vmem], out_specs=vmem,
        scratch_shapes=[pltpu.SemaphoreType.DMA((7,)), pltpu.SemaphoreType.DMA((7,)), pltpu.SemaphoreType.DMA],
        compiler_params=pltpu.CompilerParams(vmem_limit_bytes=VMEM_LIMIT))(block)


def _swap_halves(g):
    n, R, C = g.shape
    Rh = R // 2

    def body(g_ref, a_ref, send_sem, recv_sem):
        x, y, c, _ = _place()
        cp = pltpu.make_async_remote_copy(src_ref=g_ref.at[:, pl.ds((1 - c) * Rh, Rh), :], dst_ref=a_ref,
                                          send_sem=send_sem, recv_sem=recv_sem, device_id=(x, y, 1 - c),
                                          device_id_type=MESH)
        cp.start()
        cp.wait()

    return pl.pallas_call(body, name="grad_swap_halves", out_shape=jax.ShapeDtypeStruct((n, Rh, C), g.dtype),
                          in_specs=[ANY], out_specs=ANY,
                          scratch_shapes=[pltpu.SemaphoreType.DMA, pltpu.SemaphoreType.DMA])(g)


def _add_picked(a, b, picks, *, a_row_half=None, out_dtype, name):
    n_out = picks.shape[0]
    _, rows, C = b.shape
    tr = _row_tile(rows, 512)
    nt = rows // tr
    half = jnp.zeros((1,), jnp.int32) if a_row_half is None else a_row_half

    def body(pick_ref, half_ref, a_ref, b_ref, o_ref):
        o_ref[...] = (a_ref[...].astype(F32) + b_ref[...].astype(F32)).astype(o_ref.dtype)

    spec = pltpu.PrefetchScalarGridSpec(
        num_scalar_prefetch=2, grid=(n_out, nt),
        in_specs=[pl.BlockSpec((1, tr, C), lambda j, i, pick, hf: (pick[j], hf[0] * nt + i, 0)),
                  pl.BlockSpec((1, tr, C), lambda j, i, pick, hf: (pick[j], i, 0))],
        out_specs=pl.BlockSpec((1, tr, C), lambda j, i, pick, hf: (j, i, 0)))
    return pl.pallas_call(body, name=name, grid_spec=spec, out_shape=jax.ShapeDtypeStruct((n_out, rows, C), out_dtype),
                          compiler_params=_params(2))(picks.astype(jnp.int32), half.astype(jnp.int32), a, b)


def _hop_exchange(src, hop, *, name):
    def body(s_ref, d_ref, send_sem, recv_sem):
        x, y, c = lax.axis_index("x"), lax.axis_index("y"), lax.axis_index("c")
        cp = pltpu.make_async_remote_copy(src_ref=s_ref, dst_ref=d_ref, send_sem=send_sem, recv_sem=recv_sem,
                                          device_id=(*_hops(x, y, c)[hop], c), device_id_type=MESH)
        cp.start()
        cp.wait()

    return pl.pallas_call(body, name=name, out_shape=jax.ShapeDtypeStruct(src.shape, src.dtype), in_specs=[ANY],
                          out_specs=ANY, scratch_shapes=[pltpu.SemaphoreType.DMA, pltpu.SemaphoreType.DMA])(src)


def _reduce_over_chips(g):
    x, y, c = lax.axis_index("x"), lax.axis_index("y"), lax.axis_index("c")
    n1, n2, _ = _hops(x, y, c)
    chip = lambda p: 2 * p[0] + p[1]
    near = jnp.stack([chip((x, y)), chip(n2)])
    far = jnp.stack([chip(n1), chip((1 - x, 1 - y))])
    half = c.reshape(1)
    sib = _swap_halves(g)
    kept = _add_picked(g, sib, near, a_row_half=half, out_dtype=F32, name="grad_add_near")
    sent = _add_picked(g, sib, far, a_row_half=half, out_dtype=BF16, name="grad_add_far")
    got = _hop_exchange(sent, 0, name="grad_hop_first")
    mine = _add_picked(kept, got, jnp.zeros((1,), jnp.int32), out_dtype=F32, name="grad_add_mine")
    theirs = _add_picked(kept, got, jnp.ones((1,), jnp.int32), out_dtype=BF16, name="grad_add_theirs")
    got = _hop_exchange(theirs, 1, name="grad_hop_second")
    total = _add_picked(mine, got, jnp.zeros((1,), jnp.int32), out_dtype=F32, name="grad_add_total")
    return _join_halves(total[0])


def _sum_slots(b, *, name):
    n, R, C = b.shape
    tr = _row_tile(R, 512)

    def body(b_ref, o_ref):
        acc = b_ref[0]
        for q in range(1, n):
            acc = acc + b_ref[q]
        o_ref[...] = acc

    return pl.pallas_call(body, name=name, grid=(R // tr,), in_specs=[pl.BlockSpec((n, tr, C), lambda i: (0, i, 0))],
                          out_specs=pl.BlockSpec((tr, C), lambda i: (i, 0)), out_shape=jax.ShapeDtypeStruct((R, C), F32),
                          compiler_params=_params(1))(b)


def _join_halves(r):
    Rh, C = r.shape

    def body(r_ref, o_ref, send_sem, recv_sem):
        x, y, c, _ = _place()
        own, other = o_ref.at[pl.ds(c * Rh, Rh), :], o_ref.at[pl.ds((1 - c) * Rh, Rh), :]
        cp = pltpu.make_async_remote_copy(src_ref=r_ref, dst_ref=own, send_sem=send_sem, recv_sem=recv_sem,
                                          device_id=(x, y, 1 - c), device_id_type=MESH)
        cp.start()
        pltpu.make_async_remote_copy(src_ref=r_ref, dst_ref=other, send_sem=send_sem, recv_sem=recv_sem,
                                     device_id=(x, y, 1 - c), device_id_type=MESH).wait_recv()
        cp.wait_send()

    theirs = pl.pallas_call(
        body, name="grad_join_halves", out_shape=jax.ShapeDtypeStruct((2 * Rh, C), r.dtype), in_specs=[ANY], out_specs=ANY,
        scratch_shapes=[pltpu.SemaphoreType.DMA, pltpu.SemaphoreType.DMA])(r)
    return lax.dynamic_update_slice(theirs, r, (lax.axis_index("c") * Rh, 0))


def _size(shape):
    size = 1
    for d in shape:
        size *= d
    return size


def _pack(pieces, cols, row_multiple, dtype):
    if any(p.size % cols for p in pieces):
        flat = jnp.concatenate([p.reshape(-1).astype(dtype) for p in pieces])
        pieces = [jnp.pad(flat, (0, -flat.shape[0] % cols))]
    rows = [p.reshape(-1, cols).astype(dtype) for p in pieces]
    pad = -sum(r.shape[0] for r in rows) % row_multiple
    return jnp.concatenate(rows + ([jnp.zeros((pad, cols), dtype)] if pad else []), axis=0)


def _unpack(buf, shapes):
    cols = buf.shape[1]
    if any(_size(s) % cols for s in shapes):
        flat, out, at = buf.reshape(-1), [], 0
        for shp in shapes:
            out.append(flat[at:at + _size(shp)].reshape(shp))
            at += _size(shp)
        return out
    out, at = [], 0
    for shp in shapes:
        out.append(buf[at:at + _size(shp) // cols].reshape(shp))
        at += _size(shp) // cols
    return out


SHARDED = (("ffn_pre_w_gu", 2), ("ffn_pre_w_down", 1), ("sbg_w_in", 2), ("sbg_w_out", 1), ("mla_w_in", 1),
           ("mla_w_uq", 2), ("mla_w_ukv", 2), ("mla_w_out", 1), ("xmem_wq", 1), ("xmem_wkv", 2), ("xmem_wo", 1),
           ("ffn_post_w_gu", 2), ("ffn_post_w_down", 1))
LORA_GAINS = ("mla_q_lora_gain", "mla_kv_lora_gain")
REPLICATED = ("ffn_pre_norm", "mix_norm", "sgu_ln_gain", "sgu_ln_bias", "sgu_w", "sgu_b", "mla_q_gain", "mla_k_gain",
              "xmem_norm", "xmem_mem_norm", "xmem_q_gain", "xmem_k_gain", "ffn_post_norm")
WEIGHTS = ("ffn_pre_norm", "ffn_pre_w_gu", "ffn_pre_w_down", "mix_norm", "sbg_w_in", "sgu_ln_gain", "sgu_ln_bias", "sgu_w",
           "sgu_b", "sbg_w_out", "mla_w_in", "mla_q_lora_gain", "mla_kv_lora_gain", "mla_w_uq", "mla_w_ukv", "mla_q_gain",
           "mla_k_gain", "mla_w_out", "xmem_norm", "xmem_mem_norm", "xmem_wq", "xmem_wkv", "xmem_q_gain", "xmem_k_gain",
           "xmem_wo", "ffn_post_norm", "ffn_post_w_gu", "ffn_post_w_down")
INPUTS = ("x", "mem", "positions") + WEIGHTS + ("loss_target",) + tuple("m_" + n for n in WEIGHTS) + tuple(
    "v_" + n for n in WEIGHTS)


def _step(a):
    x, y, c, _ = _place()
    chip = 2 * x + y
    shard_shapes = [a[n].shape for n, _ in SHARDED]

    gathered = _gather_chips(_pack([a[n] for n, _ in SHARDED], PACK_COLS, PACK_ROW_MULTIPLE, BF16))
    w, at = {}, 0
    for (n, ax), shp in zip(SHARDED, shard_shapes):
        rows = _size(shp) // PACK_COLS
        per_chip = gathered[:, at:at + rows].reshape((N_CHIPS,) + shp)
        at += rows
        w[n] = jnp.moveaxis(per_chip, 0, ax).reshape(shp[:ax] + (N_CHIPS * shp[ax],) + shp[ax + 1:])
    gains = jnp.zeros((8, LANE), F32)
    for r, n in enumerate(LORA_GAINS):
        gains = gains.at[r, :a[n].shape[1]].set(a[n][0])
    gains = _gather_devices(gains)
    for r, n in enumerate(LORA_GAINS):
        w[n] = jnp.concatenate([gains[16 * q + r, :a[n].shape[1]] for q in range(N_CHIPS)])[None, :]
    for n in REPLICATED:
        w[n] = a[n]

    loss, dx, grads = _local_step(a["x"][0], a["mem"][0], a["positions"][0], a["loss_target"][0], w)
    loss = lax.psum(loss, ("x", "y", "c"))
    small_names = REPLICATED + LORA_GAINS
    full = {n: jnp.stack(grads[n]).reshape(w[n].shape) for n in small_names}

    def cut(n, ax, q):
        size = w[n].shape[ax] // N_CHIPS
        return [lax.slice_in_dim(gl, q * size, (q + 1) * size, axis=ax - 1) for gl in grads[n]]

    g = jnp.stack([_pack([p for n, ax in SHARDED for p in cut(n, ax, q)], PACK_COLS, PACK_ROW_MULTIPLE, F32)
                   for q in range(N_CHIPS)])
    reduced = _reduce_over_chips(g)
    gw = dict(zip([n for n, _ in SHARDED], _unpack(reduced, shard_shapes)))

    small = _pack([full[n] for n in small_names], LANE, 256, F32)
    rows = small.shape[0]
    summed = _sum_slots(_gather_devices(small).reshape(8, rows, LANE), name="grad_sum_devices")
    for n, val in zip(small_names, _unpack(summed, [full[n].shape for n in small_names])):
        if n in LORA_GAINS:
            size = a[n].shape[1]
            val = lax.dynamic_slice_in_dim(val, chip * size, size, axis=1)
        gw[n] = val

    upd = {n: _adamw(a[n], gw[n], a["m_" + n], a["v_" + n], name="adamw_" + n) for n in WEIGHTS}
    return (loss, dx[None], *[gw[n] for n in WEIGHTS], *[upd[n][0] for n in WEIGHTS], *[upd[n][1] for n in WEIGHTS],
            *[upd[n][2] for n in WEIGHTS])


def kernel(x, mem, positions, ffn_pre_norm, ffn_pre_w_gu, ffn_pre_w_down, mix_norm, sbg_w_in, sgu_ln_gain,
           sgu_ln_bias, sgu_w, sgu_b, sbg_w_out, mla_w_in, mla_q_lora_gain, mla_kv_lora_gain, mla_w_uq, mla_w_ukv,
           mla_q_gain, mla_k_gain, mla_w_out, xmem_norm, xmem_mem_norm, xmem_wq, xmem_wkv, xmem_q_gain, xmem_k_gain,
           xmem_wo, ffn_post_norm, ffn_post_w_gu, ffn_post_w_down, loss_target, m_ffn_pre_norm, m_ffn_pre_w_gu,
           m_ffn_pre_w_down, m_mix_norm, m_sbg_w_in, m_sgu_ln_gain, m_sgu_ln_bias, m_sgu_w, m_sgu_b, m_sbg_w_out,
           m_mla_w_in, m_mla_q_lora_gain, m_mla_kv_lora_gain, m_mla_w_uq, m_mla_w_ukv, m_mla_q_gain, m_mla_k_gain,
           m_mla_w_out, m_xmem_norm, m_xmem_mem_norm, m_xmem_wq, m_xmem_wkv, m_xmem_q_gain, m_xmem_k_gain,
           m_xmem_wo, m_ffn_post_norm, m_ffn_post_w_gu, m_ffn_post_w_down, v_ffn_pre_norm, v_ffn_pre_w_gu,
           v_ffn_pre_w_down, v_mix_norm, v_sbg_w_in, v_sgu_ln_gain, v_sgu_ln_bias, v_sgu_w, v_sgu_b, v_sbg_w_out,
           v_mla_w_in, v_mla_q_lora_gain, v_mla_kv_lora_gain, v_mla_w_uq, v_mla_w_ukv, v_mla_q_gain, v_mla_k_gain,
           v_mla_w_out, v_xmem_norm, v_xmem_mem_norm, v_xmem_wq, v_xmem_wkv, v_xmem_q_gain, v_xmem_k_gain,
           v_xmem_wo, v_ffn_post_norm, v_ffn_post_w_gu, v_ffn_post_w_down):
    given = locals()
    return _step({n: given[n] for n in INPUTS})
```

```python
import functools

import jax
import jax.numpy as jnp
from jax import lax
from jax.experimental import pallas as pl
from jax.experimental.pallas import tpu as pltpu

F32, BF16 = jnp.float32, jnp.bfloat16
LANE = 128
VMEM_LIMIT = 56 * 1024 * 1024
EPS = 1e-6
D_FF = 2816
SB_HEADS, SB_HD = 8, 64
SG_GROUPS, SG_GD, SG_CHUNK = 8, 64, 128
SB_W, SG_W = SB_HEADS * SB_HD, SG_GROUPS * SG_GD
MLA_HEADS, MLA_NOPE, MLA_ROPE, MLA_V = 16, 64, 32, 64
MLA_QK = MLA_NOPE + MLA_ROPE
MLA_QL, MLA_KVL = 512, 256
ROPE_THETA = 10000.0
MEM_HEADS, MEM_HD = 4, 256
SB_SCALE, MLA_SCALE, MEM_SCALE = SB_HD ** -0.5, MLA_QK ** -0.5, MEM_HD ** -0.5
ADAM_LR, ADAM_B1, ADAM_B2, ADAM_EPS, ADAM_WD, ADAM_STEP = 0.001, 0.9, 0.999, 1e-08, 0.01, 10
MESH = pl.DeviceIdType.MESH
ANY = pl.BlockSpec(memory_space=pl.ANY)


def _params(n_axes):
    return pltpu.CompilerParams(dimension_semantics=("arbitrary",) * n_axes, vmem_limit_bytes=VMEM_LIMIT)


MM_TILE_CAP = 1408
MM_VMEM_BUDGET = 40 * 1024 * 1024


def _tile(dim, cap):
    if dim <= cap:
        return dim
    best = max(t for t in range(LANE, cap + 1, LANE) if dim % t == 0)
    return best


def _k_scratch(count, tile, nk):
    return [pltpu.VMEM(tile, F32)] * count if nk > 1 else []


def _over_k_steps(prods, acc_refs, nk, finish):
    if nk == 1:
        finish(prods)
        return
    kk = pl.program_id(2)

    @pl.when(kk == 0)
    def _():
        for ref, p in zip(acc_refs, prods):
            ref[...] = p

    @pl.when(kk > 0)
    def _():
        for ref, p in zip(acc_refs, prods):
            ref[...] += p

    @pl.when(kk == nk - 1)
    def _():
        finish([ref[...] for ref in acc_refs])


def _mm(a, b, *, ta=False, tb=False, out_dtype=F32, scale=1.0, residual=None, bias=None, norm_bwd=None, a_off=(0, 0),
        b_off=(0, 0), m=None, n=None, k=None, name):
    am, ak = (a.shape[1], a.shape[0]) if ta else a.shape
    bk, bn = (b.shape[1], b.shape[0]) if tb else b.shape
    M, N, K = m or am, n or bn, k or ak
    tm, tn = _tile(M, MM_TILE_CAP if norm_bwd is None else MM_TILE_CAP // 2), _tile(N, MM_TILE_CAP)
    n_full = (residual is not None) + (2 if norm_bwd is not None else 0)
    fixed = tm * tn * (4 + 2 * jnp.dtype(out_dtype).itemsize + 8 * n_full)
    per_k = (tm * (2 * a.dtype.itemsize + 2) + tn * (2 * b.dtype.itemsize + 2))
    tk = _tile(K, max(LANE, (MM_VMEM_BUDGET - fixed) // per_k))
    nm, nn, nk = M // tm, N // tn, K // tk
    assert norm_bwd is None or nn == 1
    a_off = (a_off[0] // (tk if ta else tm), a_off[1] // (tm if ta else tk))
    b_off = (b_off[0] // (tn if tb else tk), b_off[1] // (tk if tb else tn))
    dims = (((0 if ta else 1,), (1 if tb else 0,)), ((), ()))
    n_out = 1 if norm_bwd is None else 2

    def body(*refs):
        a_ref, b_ref = refs[0], refs[1]
        n_in = len(ins)
        o_ref, extras, acc_refs = refs[n_in], refs[2:n_in], refs[n_in + n_out:]
        first_rows = pl.program_id(0) == 0

        def finish(total):
            out = total * scale
            for extra in (extras if norm_bwd is None else extras[:-3]):
                out = out + extra[...].astype(F32)
            if norm_bwd is not None:
                x_ref, g_ref, dres_ref = extras[-3:]
                dg_ref = refs[n_in + 1]
                dx, dg = _rmsnorm_bwd(out, x_ref[...], g_ref[...])
                out = dx + dres_ref[...]

                @pl.when(first_rows)
                def _():
                    dg_ref[...] = jnp.zeros_like(dg_ref)

                dg_ref[...] += dg
            o_ref[...] = out.astype(o_ref.dtype)

        prod = lax.dot_general(a_ref[...].astype(BF16), b_ref[...].astype(BF16), dims, preferred_element_type=F32)
        _over_k_steps([prod], acc_refs, nk, lambda totals: finish(totals[0]))

    (ao0, ao1), (bo0, bo1) = a_off, b_off
    a_spec = (pl.BlockSpec((tk, tm), lambda i, j, kk: (kk + ao0, i + ao1)) if ta
              else pl.BlockSpec((tm, tk), lambda i, j, kk: (i + ao0, kk + ao1)))
    b_spec = (pl.BlockSpec((tn, tk), lambda i, j, kk: (j + bo0, kk + bo1)) if tb
              else pl.BlockSpec((tk, tn), lambda i, j, kk: (kk + bo0, j + bo1)))
    o_spec = pl.BlockSpec((tm, tn), lambda i, j, kk: (i, j))
    ins, in_specs = [a, b], [a_spec, b_spec]
    if residual is not None:
        ins.append(residual)
        in_specs.append(o_spec)
    if bias is not None:
        ins.append(bias)
        in_specs.append(pl.BlockSpec((1, tn), lambda i, j, kk: (0, j)))
    out_specs, out_shape = o_spec, jax.ShapeDtypeStruct((M, N), out_dtype)
    if norm_bwd is not None:
        x, gain, dres = norm_bwd
        row = pl.BlockSpec((1, tn), lambda i, j, kk: (0, 0))
        ins += [x, gain.reshape(1, N), dres]
        in_specs += [o_spec, row, o_spec]
        out_specs, out_shape = [o_spec, row], [out_shape, jax.ShapeDtypeStruct((1, N), F32)]
    return pl.pallas_call(
        body, name=name, grid=(nm, nn, nk), in_specs=in_specs, out_specs=out_specs, out_shape=out_shape,
        scratch_shapes=_k_scratch(1, (tm, tn), nk), compiler_params=_params(3))(*ins)


def _mm_swiglu(h, wgu, *, name):
    M, K = h.shape
    F = wgu.shape[1] // 2
    tm, tn, tk = _tile(M, 512), _tile(F, MM_TILE_CAP), _tile(K, 1024)
    nm, nf, nk = M // tm, F // tn, K // tk

    def body(h_ref, wg_ref, wu_ref, g_ref, u_ref, a_ref, *acc_refs):
        def finish(totals):
            g, u = totals
            g_ref[...] = g.astype(BF16)
            u_ref[...] = u.astype(BF16)
            a_ref[...] = (g * jax.nn.sigmoid(g) * u).astype(BF16)

        hb = h_ref[...]
        _over_k_steps([jnp.dot(hb, wg_ref[...], preferred_element_type=F32),
                       jnp.dot(hb, wu_ref[...], preferred_element_type=F32)], acc_refs, nk, finish)

    o_spec = pl.BlockSpec((tm, tn), lambda j, i, kk: (i, j))
    shp = jax.ShapeDtypeStruct((M, F), BF16)
    return pl.pallas_call(
        body, name=name, grid=(nf, nm, nk),
        in_specs=[pl.BlockSpec((tm, tk), lambda j, i, kk: (i, kk)),
                  pl.BlockSpec((tk, tn), lambda j, i, kk: (kk, j)),
                  pl.BlockSpec((tk, tn), lambda j, i, kk: (kk, j + nf))],
        out_specs=[o_spec, o_spec, o_spec], out_shape=[shp, shp, shp],
        scratch_shapes=_k_scratch(2, (tm, tn), nk), compiler_params=_params(3))(h, wgu, wgu)


def _mm_dswiglu(dy, wd, gate, up, *, scale, name):
    M, K = dy.shape
    F = wd.shape[0]
    tm, tn, tk = _tile(M, 512), _tile(F, MM_TILE_CAP), _tile(K, 1024)
    nm, nf, nk = M // tm, F // tn, K // tk

    def body(dy_ref, wd_ref, g_ref, u_ref, dg_ref, du_ref, *acc_refs):
        def finish(totals):
            da = totals[0] * scale
            g, u = g_ref[...].astype(F32), u_ref[...].astype(F32)
            sg = jax.nn.sigmoid(g)
            du_ref[...] = (da * g * sg).astype(BF16)
            dg_ref[...] = (da * u * sg * (1.0 + g * (1.0 - sg))).astype(BF16)

        _over_k_steps([_nt(dy_ref[...].astype(BF16), wd_ref[...])], acc_refs, nk, finish)

    o_spec = pl.BlockSpec((tm, tn), lambda j, i, kk: (i, j))
    shp = jax.ShapeDtypeStruct((M, F), BF16)
    return pl.pallas_call(
        body, name=name, grid=(nf, nm, nk),
        in_specs=[pl.BlockSpec((tm, tk), lambda j, i, kk: (i, kk)),
                  pl.BlockSpec((tn, tk), lambda j, i, kk: (j, kk)), o_spec, o_spec],
        out_specs=[o_spec, o_spec], out_shape=[shp, shp],
        scratch_shapes=_k_scratch(1, (tm, tn), nk), compiler_params=_params(3))(dy, wd, gate, up)


HEAD_ROWS = 1024


def _row_tile(rows, cap):
    t = cap
    while t >= 8:
        if rows % t == 0:
            return t
        t //= 2
    return rows


def _rowwise(fn, rows, consts, outs, sums=(), hsums=(), *, heads=None, tm=256, name):
    rows = [r if isinstance(r, tuple) else (r, r.shape[1], None) for r in rows]
    S = rows[0][0].shape[0]
    tm = _row_tile(S, tm)
    nh = heads or 1
    n_r, n_c, n_o, n_h, n_s = len(rows), len(consts), len(outs), len(hsums), len(sums)

    def body(*refs):
        r = [x[...] for x in refs[:n_r]]
        c = [x[...] for x in refs[n_r:n_r + n_c]]
        o_refs = refs[n_r + n_c:n_r + n_c + n_o]
        h_refs = refs[n_r + n_c + n_o:n_r + n_c + n_o + n_h]
        s_refs = refs[n_r + n_c + n_o + n_h:]
        res = fn(*r, *c)
        res = res if isinstance(res, (tuple, list)) else (res,)
        for ref, val in zip(o_refs, res[:n_o]):
            ref[...] = val.astype(ref.dtype)
        if n_h:
            @pl.when(pl.program_id(1) == 0)
            def _():
                for ref in h_refs:
                    ref[...] = jnp.zeros_like(ref)
            for ref, val in zip(h_refs, res[n_o:n_o + n_h]):
                ref[...] += val
        if n_s:
            @pl.when((pl.program_id(0) == 0) & (pl.program_id(1) == 0))
            def _():
                for ref in s_refs:
                    ref[...] = jnp.zeros_like(ref)
            for ref, val in zip(s_refs, res[n_o + n_h:]):
                ref[...] += val

    def col(colfn):
        return (lambda i, h: (i, 0)) if colfn is None else (lambda i, h: (i, colfn(h)))

    in_specs = [pl.BlockSpec((tm, w), col(cf)) for _, w, cf in rows]
    in_specs += [pl.BlockSpec(a.shape, lambda i, h, nd=a.ndim: (0,) * nd) for a in consts]
    out_specs = [pl.BlockSpec((tm, w // nh), (lambda i, h: (i, h)) if heads else (lambda i, h: (i, 0))) for w, _ in outs]
    out_specs += [pl.BlockSpec((tm, w), lambda i, h: (i, 0)) for w in hsums]
    out_specs += [pl.BlockSpec(sh, lambda i, h, nd=len(sh): (0,) * nd) for sh in sums]
    out_shape = [jax.ShapeDtypeStruct((S, w), dt) for w, dt in outs]
    out_shape += [jax.ShapeDtypeStruct((S, w), F32) for w in hsums]
    out_shape += [jax.ShapeDtypeStruct(sh, F32) for sh in sums]
    return pl.pallas_call(body, name=name, grid=(S // tm, nh), in_specs=in_specs, out_specs=out_specs,
                          out_shape=out_shape, compiler_params=_params(2))(*[a for a, _, _ in rows], *consts)


def _rms(x, width=None):
    width = width or x.shape[-1]
    return lax.rsqrt(jnp.sum(x * x, axis=-1, keepdims=True) * (1.0 / width) + EPS)


def _rmsnorm_fwd(x, g, width=None):
    return x * _rms(x, width) * g


def _rmsnorm_bwd(dy, x, g, width=None):
    width = width or x.shape[-1]
    r = _rms(x, width)
    xn = x * r
    dxn = dy * g
    dx = r * (dxn - xn * (jnp.sum(dxn * xn, axis=-1, keepdims=True) * (1.0 / width)))
    return dx, jnp.sum(dy * xn, axis=0, keepdims=True)


def _norm_rows(x, g, *, name, out_dtype=BF16):
    D = x.shape[1]
    return _rowwise(lambda xv, gv: _rmsnorm_fwd(xv.astype(F32), gv), [x], [g.reshape(1, D)], [(D, out_dtype)],
                    name=name)[0]


def _norm_rows_bwd(dh, x, g, dres, *, name):
    D = x.shape[1]

    def fn(dhv, xv, *rest):
        dx, dg = _rmsnorm_bwd(dhv.astype(F32), xv, rest[-1])
        return (dx + rest[0] if dres is not None else dx), dg

    rows = [dh, x] + ([dres] if dres is not None else [])
    return _rowwise(fn, rows, [g.reshape(1, D)], [(D, F32)], [(1, D)], name=name)


def _softplus(z):
    return jnp.where(z > 20.0, z, jnp.log(1.0 + jnp.exp(z)))


def _running_sum(v, u, split=True):
    if not split:
        return jnp.dot(v.astype(BF16), u, preferred_element_type=F32)
    hi = lax.bitcast_convert_type(lax.bitcast_convert_type(v, jnp.uint32) & jnp.uint32(0xFFFF0000), F32)
    return (jnp.dot(hi.astype(BF16), u, preferred_element_type=F32)
            + jnp.dot((v - hi).astype(BF16), u, preferred_element_type=F32))


def _triangle(tk, inclusive_prefix):
    j, s = lax.broadcasted_iota(jnp.int32, (tk, tk), 0), lax.broadcasted_iota(jnp.int32, (tk, tk), 1)
    return ((j <= s) if inclusive_prefix else (j > s)).astype(BF16)


def _nt(a, b):
    return lax.dot_general(a, b, (((1,), (1,)), ((), ())), preferred_element_type=F32)


def _tn(a, b):
    return lax.dot_general(a, b, (((0,), (0,)), ((), ())), preferred_element_type=F32)


ATT_TQ, ATT_TK = 512, 512
SB_SUB = 256
FWD_GROUP = 2


def _attn_fwd(q, k, v, *, sb, causal, heads, dq, dv, group=1, kcol=None, vcol=None, sum_lane=None, name):
    S, Sk = q.shape[0], k.shape[0]
    tq, tk = min(ATT_TQ, S), min(ATT_TK, Sk)
    sub = min(SB_SUB, tk) if sb else tk
    assert tq % sub == 0 or not causal
    kcol = kcol or (lambda h: h)
    vcol = vcol or (lambda h: h)
    members = range(group)

    def body(*refs):
        if sb:
            q_ref, k_ref, v_ref, u_ref, o_ref, lse_ref, acc_ref, r_ref = refs
            r_ref[...] = jnp.zeros_like(r_ref)
        else:
            q_ref, k_ref, v_ref, o_ref, lse_ref, acc_ref, m_ref, l_ref = refs
            m_ref[...] = jnp.full_like(m_ref, -1e30)
            l_ref[...] = jnp.zeros_like(l_ref)
        first_row = pl.program_id(1) * tq
        qb = [q_ref[:, hh * dq:(hh + 1) * dq] for hh in members]
        acc_ref[...] = jnp.zeros_like(acc_ref)
        nblk = (first_row + tq) // sub if causal else Sk // sub
        nfull = (first_row + (0 if sb else 1)) // sub if causal else nblk
        n_cut = tq // sub if causal else 0

        def scores(jj):
            off = pl.multiple_of(jj * sub, sub)
            return tuple(_nt(qb[hh], k_ref[pl.ds(off, sub), hh * dq:(hh + 1) * dq]) for hh in members)

        def weigh(jj, scores_now, masked):
            off = pl.multiple_of(jj * sub, sub)
            if masked:
                kpos = off + lax.broadcasted_iota(jnp.int32, (tq, sub), 1)
                qpos = first_row + lax.broadcasted_iota(jnp.int32, (tq, sub), 0)
                valid = (kpos < qpos) if sb else (kpos <= qpos)
            for hh in members:
                vb = v_ref[pl.ds(off, sub), hh * dv:(hh + 1) * dv]
                s = scores_now[hh]
                if sb:
                    sp = _softplus(s)
                    ls = jnp.where(valid, -sp, 0.0) if masked else -sp
                    w = jnp.exp(s - sp + r_ref[hh] + _running_sum(ls, u_ref[...]))
                    if masked:
                        w = jnp.where(valid, w, 0.0)
                    acc_ref[hh] += jnp.dot(w.astype(BF16), vb, preferred_element_type=F32)
                    r_ref[hh] += jnp.sum(ls, axis=1, keepdims=True)
                else:
                    if masked:
                        s = jnp.where(valid, s, -1e30)
                    m_old = m_ref[hh]
                    m_new = jnp.maximum(m_old, jnp.max(s, axis=1, keepdims=True))
                    p = jnp.exp(s - m_new)
                    alpha = jnp.exp(m_old - m_new)
                    if sum_lane is None:
                        l_ref[hh] = alpha * l_ref[hh] + jnp.sum(p, axis=1, keepdims=True)
                    acc_ref[hh] = alpha * acc_ref[hh] + jnp.dot(p.astype(BF16), vb, preferred_element_type=F32)
                    m_ref[hh] = m_new

        if sb:
            s_cur = scores(nblk - 1)
            for cut in range(n_cut):
                s_next = scores(jnp.maximum(nblk - 2 - cut, 0))
                weigh(nblk - 1 - cut, s_cur, True)
                s_cur = s_next

            def step(t, s_now):
                s_next = scores(jnp.maximum(nfull - 2 - t, 0))
                weigh(nfull - 1 - t, s_now, False)
                return s_next

            lax.fori_loop(0, nfull, step, s_cur)
        else:
            n_loop = nfull if causal else nblk - 1

            def step(t, s_now):
                s_next = scores(jnp.minimum(t + 1, nblk - 1))
                weigh(t, s_now, False)
                return s_next

            s_cur = lax.fori_loop(0, n_loop, step, scores(0))
            tail = n_cut if causal else 1
            for last in range(tail):
                s_next = scores(n_loop + last + 1) if last + 1 < tail else None
                weigh(n_loop + last, s_cur, causal)
                s_cur = s_next
        for hh in members:
            cols = slice(hh * dv, (hh + 1) * dv)
            if sb:
                o_ref[:, cols] = acc_ref[hh]
                lse_ref[hh] = r_ref[hh]
            else:
                acc = acc_ref[hh]
                l = l_ref[hh] if sum_lane is None else acc[:, sum_lane:sum_lane + 1]
                o_ref[:, cols] = acc / l
                lse_ref[hh] = m_ref[hh] + jnp.log(l)

    in_specs = [pl.BlockSpec((tq, group * dq), lambda g, i: (i, g)),
                pl.BlockSpec((Sk, group * dq), lambda g, i: (0, kcol(g))),
                pl.BlockSpec((Sk, group * dv), lambda g, i: (0, vcol(g)))]
    ins = [q, k, v]
    scratch = [pltpu.VMEM((group, tq, dv), F32), pltpu.VMEM((group, tq, 1), F32)]
    if sb:
        ins.append(_triangle(sub, inclusive_prefix=False))
        in_specs.append(pl.BlockSpec((sub, sub), lambda g, i: (0, 0)))
    else:
        scratch.append(pltpu.VMEM((group, tq, 1), F32))
    out_specs = [pl.BlockSpec((tq, group * dv), lambda g, i: (i, g)), pl.BlockSpec((group, tq, 1), lambda g, i: (g, i, 0))]
    out_shape = [jax.ShapeDtypeStruct((S, heads * dv), F32), jax.ShapeDtypeStruct((heads, S, 1), F32)]
    return pl.pallas_call(body, name=name, grid=(heads // group, S // tq), in_specs=in_specs, out_specs=out_specs,
                          out_shape=out_shape, scratch_shapes=scratch, compiler_params=_params(2))(*ins)


def _attn_bwd(q, k, v, o, do, lse, *, sb, causal, heads, dq, dv, kcol=None, vcol=None, name):
    S, Sk = q.shape[0], k.shape[0]
    tq, tk = min(ATT_TQ, S), min(ATT_TK, Sk)
    sub = min(SB_SUB, tk) if sb else tk
    assert tq % sub == 0 or not causal
    nq = S // tq
    kcol = kcol or (lambda h: h)
    vcol = vcol or (lambda h: h)

    def body(*refs):
        if sb:
            q_ref, k_ref, v_ref, o_ref, do_ref, lse_ref, u_ref, dq_ref, dk_ref, dv_ref, acc_ref, r_ref, re_ref = refs
            r_ref[...] = jnp.zeros_like(r_ref)
            re_ref[...] = jnp.zeros_like(re_ref)
        else:
            q_ref, k_ref, v_ref, o_ref, do_ref, lse_ref, dq_ref, dk_ref, dv_ref, acc_ref = refs
        first_row = pl.program_id(1) * tq

        @pl.when(first_row == 0)
        def _():
            dk_ref[...] = jnp.zeros_like(dk_ref)
            dv_ref[...] = jnp.zeros_like(dv_ref)

        qb = q_ref[...]
        dof = do_ref[...].astype(F32)
        dob = dof.astype(BF16)
        q_t, do_t = qb.T, dob.T
        if not sb:
            dlt = jnp.sum(dof * o_ref[...], axis=1, keepdims=True)
        acc_ref[...] = jnp.zeros_like(acc_ref)
        nblk = (first_row + tq) // sub if causal else Sk // sub
        nfull = (first_row + (0 if sb else 1)) // sub if causal else nblk
        n_cut = tq // sub if causal else 0

        def products(jj):
            off = pl.multiple_of(jj * sub, sub)
            return _nt(qb, k_ref[pl.ds(off, sub), :]), _nt(dob, v_ref[pl.ds(off, sub), :])

        def piece(jj, now, masked):
            off = pl.multiple_of(jj * sub, sub)
            kb = k_ref[pl.ds(off, sub), :]
            s, dp = now
            if masked:
                qpos = first_row + lax.broadcasted_iota(jnp.int32, (tq, sub), 0)
                kpos = off + lax.broadcasted_iota(jnp.int32, (tq, sub), 1)
                valid = (kpos < qpos) if sb else (kpos <= qpos)
            if sb:
                u = u_ref[...]
                sp = _softplus(s)
                ls = jnp.where(valid, -sp, 0.0) if masked else -sp
                lb = s - sp
                w = jnp.exp(lb + (lse_ref[0] - (r_ref[...] + _running_sum(ls, u))))
                if masked:
                    w = jnp.where(valid, w, 0.0)
                e = dp * w
                ds = e - jnp.exp(lb) * (re_ref[...] + _running_sum(e, u, split=False))
                if masked:
                    ds = jnp.where(valid, ds, 0.0)
                r_ref[...] += jnp.sum(ls, axis=1, keepdims=True)
                re_ref[...] += jnp.sum(e, axis=1, keepdims=True)
            else:
                w = jnp.exp(s - lse_ref[0])
                if masked:
                    w = jnp.where(valid, w, 0.0)
                ds = w * (dp - dlt)
            dsb = ds.astype(BF16)
            dv_ref[:, pl.ds(off, sub)] += jnp.dot(do_t, w.astype(BF16), preferred_element_type=F32)
            dk_ref[:, pl.ds(off, sub)] += jnp.dot(q_t, dsb, preferred_element_type=F32)
            acc_ref[...] += jnp.dot(dsb, kb, preferred_element_type=F32)

        n_loop = nfull if causal else nblk - 1
        per_trip = tk // sub

        def steps(first, count, masked):
            ready = [products(first + c) for c in range(count)]
            for c in range(count):
                piece(first + c, ready[c], masked)

        def trip(t, carry):
            steps(t * per_trip, per_trip, False)
            return carry

        lax.fori_loop(0, n_loop // per_trip, trip, 0)
        steps(n_loop, n_cut if causal else 1, causal)
        dq_ref[...] = acc_ref[...]

    ins = [q, k, v, o, do]
    in_specs = [pl.BlockSpec((tq, dq), lambda h, i: (i, h)),
                pl.BlockSpec((Sk, dq), lambda h, i: (0, kcol(h))),
                pl.BlockSpec((Sk, dv), lambda h, i: (0, vcol(h))),
                pl.BlockSpec((tq, dv), lambda h, i: (i, h)),
                pl.BlockSpec((tq, dv), lambda h, i: (i, h))]
    scratch = [pltpu.VMEM((tq, dq), F32)]
    ins.append(lse)
    in_specs.append(pl.BlockSpec((1, tq, 1), lambda h, i: (h, i, 0)))
    if sb:
        ins.append(_triangle(sub, inclusive_prefix=True))
        in_specs.append(pl.BlockSpec((sub, sub), lambda h, i: (0, 0)))
        scratch += [pltpu.VMEM((tq, 1), F32), pltpu.VMEM((tq, 1), F32)]
    out_specs = [pl.BlockSpec((tq, dq), lambda h, i: (i, h)),
                 pl.BlockSpec((dq, Sk), lambda h, i: (h, 0)),
                 pl.BlockSpec((dv, Sk), lambda h, i: (h, 0))]
    out_shape = [jax.ShapeDtypeStruct((S, heads * dq), F32), jax.ShapeDtypeStruct((heads * dq, Sk), F32),
                 jax.ShapeDtypeStruct((heads * dv, Sk), F32)]
    return pl.pallas_call(body, name=name, grid=(heads, nq), in_specs=in_specs, out_specs=out_specs,
                          out_shape=out_shape, scratch_shapes=scratch, compiler_params=_params(2))(*ins)


GELU_C = 0.7978845608028654
assert 2 * SG_GD == LANE and SG_CHUNK == LANE


def _gelu(z):
    t = jnp.tanh(GELU_C * (z + 0.044715 * z * z * z))
    return 0.5 * z * (1.0 + t), t


def _gelu_grad(z, t):
    return 0.5 * (1.0 + t) + 0.5 * z * (1.0 - t * t) * GELU_C * (1.0 + 3.0 * 0.044715 * z * z)


def _layernorm_parts(g):
    d = g - jnp.mean(g, axis=-1, keepdims=True)
    rstd = lax.rsqrt(jnp.mean(d * d, axis=-1, keepdims=True) + EPS)
    return d * rstd, rstd


def _gelu_ln(z, gain, bias, *, name):
    def fn(zv, gn, bs):
        a, _ = _gelu(zv)
        y, _ = _layernorm_parts(a[:, SG_W:])
        return a[:, :SG_W], y * gn + bs

    return _rowwise(fn, [z], [gain.reshape(1, SG_W), bias.reshape(1, SG_W)], [(SG_W, F32), (SG_W, BF16)], name=name)


def _gelu_ln_bwd(z, du, dgl, gain, *, name):
    def fn(zv, duv, dglv, gn):
        a, t = _gelu(zv)
        y, rstd = _layernorm_parts(a[:, SG_W:])
        dy = dglv * gn
        dgg = rstd * (dy - jnp.mean(dy, axis=-1, keepdims=True) - y * jnp.mean(dy * y, axis=-1, keepdims=True))
        dz = jnp.concatenate([duv, dgg], axis=1) * _gelu_grad(zv, t)
        return dz, jnp.sum(dglv * y, axis=0, keepdims=True), jnp.sum(dglv, axis=0, keepdims=True)

    return _rowwise(fn, [z, du, dgl], [gain.reshape(1, SG_W)], [(2 * SG_W, BF16)], [(1, SG_W), (1, SG_W)], name=name)


def _sg_masks():
    tri = lax.broadcasted_iota(jnp.int32, (SG_CHUNK, SG_CHUNK), 0) >= lax.broadcasted_iota(jnp.int32, (SG_CHUNK, SG_CHUNK), 1)
    first = lax.broadcasted_iota(jnp.int32, (SG_CHUNK, LANE), 1) < SG_GD
    return tri, first


def _spatial(gl, u, w, bt, *, name):
    S = gl.shape[0]
    tm = _row_tile(S, 512)
    nch = tm // SG_CHUNK

    def body(gl_ref, u_ref, w_ref, bt_ref, o_ref):
        tri, first = _sg_masks()
        for p in range(SG_W // LANE):
            cols = slice(p * LANE, (p + 1) * LANE)
            wa = jnp.where(tri, w_ref[2 * p], 0.0).astype(BF16)
            wb = jnp.where(tri, w_ref[2 * p + 1], 0.0).astype(BF16)
            for ci in range(nch):
                rws = slice(ci * SG_CHUNK, (ci + 1) * SG_CHUNK)
                g = gl_ref[rws, cols]
                zero = jnp.zeros_like(g)
                mixed = (jnp.dot(wa, jnp.where(first, g, zero), preferred_element_type=F32)
                         + jnp.dot(wb, jnp.where(first, zero, g), preferred_element_type=F32) + bt_ref[:, cols])
                o_ref[rws, cols] = u_ref[rws, cols] * mixed

    row = pl.BlockSpec((tm, SG_W), lambda i: (i, 0))
    return pl.pallas_call(
        body, name=name, grid=(S // tm,),
        in_specs=[row, row, pl.BlockSpec(w.shape, lambda i: (0, 0, 0)), pl.BlockSpec(bt.shape, lambda i: (0, 0))],
        out_specs=row, out_shape=jax.ShapeDtypeStruct((S, SG_W), F32), compiler_params=_params(1))(gl, u, w, bt)


def _spatial_bwd(d_o, gl, u, w, bt, *, name):
    S = gl.shape[0]
    tm = _row_tile(S, 512)
    nch = tm // SG_CHUNK
    nsteps = S // tm

    def body(do_ref, gl_ref, u_ref, w_ref, bt_ref, du_ref, dgl_ref, dw_ref, db_ref, dbt_ref):
        tri, first = _sg_masks()
        step = pl.program_id(0)

        @pl.when(step == 0)
        def _():
            dw_ref[...] = jnp.zeros_like(dw_ref)
            dbt_ref[...] = jnp.zeros_like(dbt_ref)

        for p in range(SG_W // LANE):
            cols = slice(p * LANE, (p + 1) * LANE)
            wa = jnp.where(tri, w_ref[2 * p], 0.0).astype(BF16)
            wb = jnp.where(tri, w_ref[2 * p + 1], 0.0).astype(BF16)
            for ci in range(nch):
                rws = slice(ci * SG_CHUNK, (ci + 1) * SG_CHUNK)
                g = gl_ref[rws, cols]
                zero = jnp.zeros_like(g)
                mixed = (jnp.dot(wa, jnp.where(first, g, zero), preferred_element_type=F32)
                         + jnp.dot(wb, jnp.where(first, zero, g), preferred_element_type=F32) + bt_ref[:, cols])
                dov = do_ref[rws, cols]
                du_ref[rws, cols] = dov * mixed
                dm = dov * u_ref[rws, cols]
                dbt_ref[:, cols] += dm
                dma = jnp.where(first, dm, 0.0).astype(BF16)
                dmb = jnp.where(first, 0.0, dm).astype(BF16)
                dw_ref[2 * p] += jnp.where(tri, _nt(dma, g), 0.0)
                dw_ref[2 * p + 1] += jnp.where(tri, _nt(dmb, g), 0.0)
                dgl_ref[rws, cols] = _tn(wa, dma) + _tn(wb, dmb)

        @pl.when(step == nsteps - 1)
        def _():
            lane = lax.broadcasted_iota(jnp.int32, (SG_CHUNK, LANE), 1)
            acc = jnp.zeros((SG_CHUNK, LANE), F32)
            for p in range(SG_W // LANE):
                blk = dbt_ref[:, p * LANE:(p + 1) * LANE]
                sa = jnp.sum(jnp.where(first, blk, 0.0), axis=1, keepdims=True)
                sb_ = jnp.sum(jnp.where(first, 0.0, blk), axis=1, keepdims=True)
                acc = acc + jnp.where(lane == 2 * p, sa, 0.0) + jnp.where(lane == 2 * p + 1, sb_, 0.0)
            db_ref[...] = acc

    row = pl.BlockSpec((tm, SG_W), lambda i: (i, 0))
    return pl.pallas_call(
        body, name=name, grid=(nsteps,),
        in_specs=[row, row, row, pl.BlockSpec(w.shape, lambda i: (0, 0, 0)), pl.BlockSpec(bt.shape, lambda i: (0, 0))],
        out_specs=[row, row, pl.BlockSpec(w.shape, lambda i: (0, 0, 0)), pl.BlockSpec((SG_CHUNK, LANE), lambda i: (0, 0))],
        out_shape=[jax.ShapeDtypeStruct((S, SG_W), F32), jax.ShapeDtypeStruct((S, SG_W), F32),
                   jax.ShapeDtypeStruct(w.shape, F32), jax.ShapeDtypeStruct((SG_CHUNK, LANE), F32)],
        scratch_shapes=[pltpu.VMEM((SG_CHUNK, SG_W), F32)], compiler_params=_params(1))(d_o, gl, u, w, bt)


ROPE_HALF = MLA_ROPE // 2
KR_COL = (MLA_QL + MLA_KVL) // LANE
MLA_IN_PAD = MLA_QL + MLA_KVL + LANE


def _rope_tables(positions):
    inv_freq = ROPE_THETA ** (-jnp.arange(ROPE_HALF, dtype=F32) / ROPE_HALF)
    ang = positions.astype(F32)[:, None] * inv_freq
    cos, sin = jnp.cos(ang), jnp.sin(ang)
    S = positions.shape[0]
    z16, tail = jnp.zeros((S, ROPE_HALF), F32), jnp.zeros((S, LANE - MLA_QK), F32)
    ones = jnp.ones((S, MLA_NOPE), F32)
    zeros = jnp.zeros((S, MLA_NOPE), F32)
    return (jnp.concatenate([ones, cos, cos, tail], axis=1), jnp.concatenate([zeros, z16, sin, tail], axis=1),
            jnp.concatenate([zeros, -sin, z16, tail], axis=1))


def _rope(x, cos, sa, sb):
    return x * cos + pltpu.roll(x, ROPE_HALF, 1) * sa + pltpu.roll(x, LANE - ROPE_HALF, 1) * sb


def _rope_t(dy, cos, sa, sb):
    return dy * cos + pltpu.roll(dy * sa, LANE - ROPE_HALF, 1) + pltpu.roll(dy * sb, ROPE_HALF, 1)


def _mla_lora(P, qlg, kvlg, *, name):
    def fn(pv, a, b):
        return _rmsnorm_fwd(pv[:, :MLA_QL], a), _rmsnorm_fwd(pv[:, MLA_QL:MLA_QL + MLA_KVL], b)

    return _rowwise(fn, [P], [qlg.reshape(1, MLA_QL), kvlg.reshape(1, MLA_KVL)], [(MLA_QL, BF16), (MLA_KVL, BF16)], name=name)


def _mla_lora_bwd(dcq, dckv, dkr, P, qlg, kvlg, *, name):
    def fn(d1, d2, d3, pv, a, b):
        x1, g1 = _rmsnorm_bwd(d1, pv[:, :MLA_QL], a)
        x2, g2 = _rmsnorm_bwd(d2, pv[:, MLA_QL:MLA_QL + MLA_KVL], b)
        return jnp.concatenate([x1, x2, d3], axis=1), g1, g2

    return _rowwise(fn, [dcq, dckv, dkr, P], [qlg.reshape(1, MLA_QL), kvlg.reshape(1, MLA_KVL)], [(MLA_IN_PAD, BF16)],
                    [(1, MLA_QL), (1, MLA_KVL)], name=name)


def _mla_qk(q_pre, k_pre, P, tabs, qg, kg, *, name):
    def fn(qp, kp, kr, c, a, b, qgv, kgv):
        return (_rope(_rmsnorm_fwd(qp, qgv, MLA_QK), c, a, b) * MLA_SCALE,
                _rope(_rmsnorm_fwd(kp + kr, kgv, MLA_QK), c, a, b))

    hcol = lambda h: h
    rows = [(q_pre, LANE, hcol), (k_pre, LANE, hcol), (P, LANE, lambda h: KR_COL), *tabs]
    w = MLA_HEADS * LANE
    return _rowwise(fn, rows, [qg, kg], [(w, BF16), (w, BF16)], heads=MLA_HEADS, tm=HEAD_ROWS, name=name)


def _mla_qk_bwd(dq, dk, q_pre, k_pre, P, tabs, qg, kg, *, name):
    def fn(dqv, dkv, qp, kp, kr, c, a, b, qgv, kgv):
        dqp, dqg = _rmsnorm_bwd(_rope_t(dqv * MLA_SCALE, c, a, b), qp, qgv, MLA_QK)
        dkp, dkg = _rmsnorm_bwd(_rope_t(dkv, c, a, b), kp + kr, kgv, MLA_QK)
        lane = lax.broadcasted_iota(jnp.int32, (1, LANE), 1)
        return dqp, dkp, jnp.where((lane >= MLA_NOPE) & (lane < MLA_QK), dkp, 0.0), dqg, dkg

    hcol = lambda h: h
    rows = [(dq, LANE, hcol), (dk, LANE, hcol), (q_pre, LANE, hcol), (k_pre, LANE, hcol), (P, LANE, lambda h: KR_COL), *tabs]
    w = MLA_HEADS * LANE
    return _rowwise(fn, rows, [qg, kg], [(w, BF16), (w, BF16)], [(1, LANE), (1, LANE)], [LANE], heads=MLA_HEADS,
                    tm=HEAD_ROWS, name=name)


def _head_norm(x, g, *, heads, width, colfn=None, scale=1.0, name):
    return _rowwise(lambda xv, gv: _rmsnorm_fwd(xv, gv) * scale, [(x, width, colfn or (lambda h: h))],
                    [g.reshape(1, width)], [(heads * width, BF16)], heads=heads, tm=HEAD_ROWS, name=name)[0]


def _head_norm_bwd(dy, x, g, *, heads, width, colfn=None, scale=1.0, out_dtype, name):
    return _rowwise(lambda dv_, xv, gv: _rmsnorm_bwd(dv_ * scale, xv, gv),
                    [(dy, width, lambda h: h), (x, width, colfn or (lambda h: h))],
                    [g.reshape(1, width)], [(heads * width, out_dtype)], [(1, width)], heads=heads, tm=HEAD_ROWS,
                    name=name)


def _loss_grad(y, tgt, *, name):
    D = y.shape[1]

    def fn(yv, tv):
        d = yv - tv
        return d * (1.0 / D), jnp.sum(d * d, axis=0, keepdims=True) * (0.5 / D)

    dy, part = _rowwise(fn, [y, tgt], [], [(D, F32)], [(1, D)], name=name)
    return jnp.sum(part), dy


def _adamw(w, g, m, v, *, name):
    shape = w.shape
    two_d = (-1, shape[-1])

    def fn(wv, gv, mv, vv):
        m2 = ADAM_B1 * mv + (1.0 - ADAM_B1) * gv
        v2 = ADAM_B2 * vv + (1.0 - ADAM_B2) * (gv * gv)
        m_hat = m2 / (1.0 - ADAM_B1 ** ADAM_STEP)
        v_hat = v2 / (1.0 - ADAM_B2 ** ADAM_STEP)
        return -ADAM_LR * (m_hat / (jnp.sqrt(v_hat) + ADAM_EPS) + ADAM_WD * wv), m2, v2

    outs = _rowwise(fn, [t.reshape(two_d) for t in (w, g, m, v)], [], [(shape[-1], F32)] * 3, name=name)
    return [o.reshape(shape) for o in outs]


def _pad_cols(w, heads, hd):
    k = w.shape[0]
    return jnp.pad(w.reshape(k, heads, hd), ((0, 0), (0, 0), (0, LANE - hd))).reshape(k, heads * LANE)


def _unpad_cols(w, heads, hd):
    k = w.shape[0]
    return w.reshape(k, heads, LANE)[:, :, :hd].reshape(k, heads * hd)


def _pad_rows(w, heads, hd):
    n = w.shape[1]
    return jnp.pad(w.reshape(heads, hd, n), ((0, 0), (0, LANE - hd), (0, 0))).reshape(heads * LANE, n)


def _unpad_rows(w, heads, hd):
    n = w.shape[1]
    return w.reshape(heads, LANE, n)[:, :hd, :].reshape(heads * hd, n)


def _ffn_fwd(x, g, wgu, wd, tag):
    h = _norm_rows(x, g, name=tag + "_norm")
    gate, up, act = _mm_swiglu(h, wgu, name=tag + "_gu")
    y = _mm(act, wd, scale=0.5, residual=x, name=tag + "_down")
    return y, (x, h, gate, up, act)


def _ffn_bwd(dy, saved, g, wgu, wd, tag):
    x, h, gate, up, act = saved
    F = wd.shape[0]
    dwd = _mm(act, dy, ta=True, scale=0.5, name=tag + "_dwd")
    dgate, dup = _mm_dswiglu(dy, wd, gate, up, scale=0.5, name=tag + "_dact")
    dh = _mm(dgate, wgu, tb=True, name=tag + "_dh_g")
    dx, dg = _mm(dup, wgu, tb=True, b_off=(0, F), residual=dh, norm_bwd=(x, g, dy), name=tag + "_dh_u")
    dwgu = jnp.concatenate([_mm(h, dgate, ta=True, name=tag + "_dwg"), _mm(h, dup, ta=True, name=tag + "_dwu")], axis=1)
    return dx, dg, dwgu, dwd


def _even_weights(w_in, w_out):
    parts = [w_in[:, :SB_W] * SB_SCALE, w_in[:, SB_W:2 * SB_W], w_in[:, 2 * SB_W:3 * SB_W]]
    wqkv = jnp.concatenate([_pad_cols(p, SB_HEADS, SB_HD) for p in parts], axis=1)
    return wqkv, w_in[:, 3 * SB_W:], _pad_rows(w_out[:SB_W], SB_HEADS, SB_HD), w_out[SB_W:]


def _even_fwd(x, g, wts, ln_g, ln_b, sgu_w, bt, tag):
    wqkv, wz, wo_sb, wo_sg = wts
    h = _norm_rows(x, g, name=tag + "_norm")
    qkv = _mm(h, wqkv, out_dtype=BF16, name=tag + "_qkv")
    z = _mm(h, wz, name=tag + "_z")
    o_sb, tot = _attn_fwd(qkv, qkv, qkv, sb=True, causal=True, heads=SB_HEADS, dq=LANE, dv=LANE, group=FWD_GROUP,
                          kcol=lambda g: SB_HEADS // FWD_GROUP + g, vcol=lambda g: 2 * SB_HEADS // FWD_GROUP + g,
                          name=tag + "_sb")
    u, gl = _gelu_ln(z, ln_g, ln_b, name=tag + "_geluln")
    o_sg = _spatial(gl, u, sgu_w, bt, name=tag + "_sgu")
    y = _mm(o_sb, wo_sb, residual=x, name=tag + "_out_sb")
    y = _mm(o_sg, wo_sg, residual=y, name=tag + "_out_sg")
    return y, (x, h, qkv, z, o_sb, tot, u, gl, o_sg)


def _even_bwd(dy, saved, g, wts, ln_g, sgu_w, bt, tag):
    wqkv, wz, wo_sb, wo_sg = wts
    x, h, qkv, z, o_sb, tot, u, gl, o_sg = saved
    do_sb = _mm(dy, wo_sb, tb=True, name=tag + "_do_sb")
    do_sg = _mm(dy, wo_sg, tb=True, name=tag + "_do_sg")
    dwo = jnp.concatenate([_unpad_rows(_mm(o_sb, dy, ta=True, name=tag + "_dwo_sb"), SB_HEADS, SB_HD),
                           _mm(o_sg, dy, ta=True, name=tag + "_dwo_sg")], axis=0)
    dq, dk_t, dv_t = _attn_bwd(qkv, qkv, qkv, o_sb, do_sb, tot, sb=True, causal=True, heads=SB_HEADS, dq=LANE, dv=LANE,
                               kcol=lambda hh: SB_HEADS + hh, vcol=lambda hh: 2 * SB_HEADS + hh, name=tag + "_sb_bwd")
    du, dgl, dsgu_w, db_t = _spatial_bwd(do_sg, gl, u, sgu_w, bt, name=tag + "_sgu_bwd")
    dz, dln_g, dln_b = _gelu_ln_bwd(z, du, dgl, ln_g, name=tag + "_geluln_bwd")
    dh = _mm(dz, wz, tb=True, name=tag + "_dh_z")
    dh = _mm(dq, wqkv, tb=True, residual=dh, name=tag + "_dh_q")
    dws = [_unpad_cols(_mm(h, dq, ta=True, scale=SB_SCALE, name=tag + "_dw_q"), SB_HEADS, SB_HD)]
    for i, (d_t, nm) in enumerate(((dk_t, "k"), (dv_t, "v")), start=1):
        dh = _mm(d_t, wqkv, ta=True, tb=True, b_off=(0, i * SB_HEADS * LANE), residual=dh,
                 norm_bwd=(x, g, dy) if nm == "v" else None, name=tag + "_dh_" + nm)
        dws.append(_unpad_rows(_mm(d_t, h, name=tag + "_dw_" + nm), SB_HEADS, SB_HD).T)
    dws.append(_mm(h, dz, ta=True, name=tag + "_dw_z"))
    dx, dg = dh
    return dx, dict(mix_norm=dg, sbg_w_in=jnp.concatenate(dws, axis=1), sgu_ln_gain=dln_g, sgu_ln_bias=dln_b,
                    sgu_w=dsgu_w, sgu_b=db_t[:, :SG_GROUPS].T, sbg_w_out=dwo)


def _mla_weights(w_in, w_uq, w_ukv, w_out, q_gain, k_gain):
    d = w_in.shape[0]
    lat = MLA_QL + MLA_KVL
    w_in_ext = jnp.concatenate([w_in[:, :lat], jnp.zeros((d, MLA_NOPE), w_in.dtype), w_in[:, lat:],
                                jnp.zeros((d, LANE - MLA_QK), w_in.dtype)], axis=1)
    kv = w_ukv.reshape(MLA_KVL, MLA_HEADS, MLA_NOPE + MLA_V)
    wk = _pad_cols(kv[:, :, :MLA_NOPE].reshape(MLA_KVL, -1), MLA_HEADS, MLA_NOPE)
    wv = _pad_cols(kv[:, :, MLA_NOPE:].reshape(MLA_KVL, -1), MLA_HEADS, MLA_V)
    pad_gain = lambda gn: jnp.pad(gn.reshape(1, MLA_QK), ((0, 0), (0, LANE - MLA_QK)))
    return (w_in_ext, _pad_cols(w_uq, MLA_HEADS, MLA_QK), wk, wv, _pad_rows(w_out, MLA_HEADS, MLA_V),
            pad_gain(q_gain), pad_gain(k_gain))


def _mla_fwd(x, g, wts, qlg, kvlg, tabs, tag):
    w_in, w_uq, wk, wv, w_out, qg, kg = wts
    h = _norm_rows(x, g, name=tag + "_norm")
    P = _mm(h, w_in, name=tag + "_in")
    cqn, ckvn = _mla_lora(P, qlg, kvlg, name=tag + "_lora")
    q_pre = _mm(cqn, w_uq, name=tag + "_uq")
    k_pre = _mm(ckvn, wk, name=tag + "_uk")
    ones_lane = jnp.tile((jnp.arange(LANE) == MLA_V).astype(F32), MLA_HEADS)[None, :]
    v = _mm(ckvn, wv, out_dtype=BF16, bias=ones_lane, name=tag + "_uv")
    q, k = _mla_qk(q_pre, k_pre, P, tabs, qg, kg, name=tag + "_qk")
    o, lse = _attn_fwd(q, k, v, sb=False, causal=True, heads=MLA_HEADS, dq=LANE, dv=LANE, group=FWD_GROUP,
                       sum_lane=MLA_V, name=tag + "_attn")
    y = _mm(o, w_out, residual=x, name=tag + "_out")
    return y, (x, h, P, cqn, ckvn, q_pre, k_pre, q, k, v, o, lse)


def _mla_bwd(dy, saved, g, wts, qlg, kvlg, tabs, tag):
    w_in, w_uq, wk, wv, w_out, qg, kg = wts
    x, h, P, cqn, ckvn, q_pre, k_pre, q, k, v, o, lse = saved
    do = _mm(dy, w_out, tb=True, name=tag + "_do")
    dw_out = _unpad_rows(_mm(o, dy, ta=True, name=tag + "_dwo"), MLA_HEADS, MLA_V)
    dq, dk_t, dv_t = _attn_bwd(q, k, v, o, do, lse, sb=False, causal=True, heads=MLA_HEADS, dq=LANE, dv=LANE,
                               name=tag + "_attn_bwd")
    dq_pre, dk_pre, dkr, dqg, dkg = _mla_qk_bwd(dq, dk_t.T, q_pre, k_pre, P, tabs, qg, kg, name=tag + "_qk_bwd")
    dcqn = _mm(dq_pre, w_uq, tb=True, name=tag + "_dcq")
    dckvn = _mm(dk_pre, wk, tb=True, name=tag + "_dckv_k")
    dckvn = _mm(dv_t, wv, ta=True, tb=True, residual=dckvn, name=tag + "_dckv_v")
    dw_uq = _unpad_cols(_mm(cqn, dq_pre, ta=True, name=tag + "_dwuq"), MLA_HEADS, MLA_QK)
    dwk = _unpad_cols(_mm(ckvn, dk_pre, ta=True, name=tag + "_dwk"), MLA_HEADS, MLA_NOPE)
    dwv = _unpad_rows(_mm(dv_t, ckvn, name=tag + "_dwv"), MLA_HEADS, MLA_V).T
    dw_ukv = jnp.concatenate([dwk.reshape(MLA_KVL, MLA_HEADS, MLA_NOPE), dwv.reshape(MLA_KVL, MLA_HEADS, MLA_V)],
                             axis=2).reshape(MLA_KVL, -1)
    dP, dqlg, dkvlg = _mla_lora_bwd(dcqn, dckvn, dkr, P, qlg, kvlg, name=tag + "_lora_bwd")
    dx, dg = _mm(dP, w_in, tb=True, norm_bwd=(x, g, dy), name=tag + "_dh")
    dw_in_ext = _mm(h, dP, ta=True, name=tag + "_dwin")
    lat = MLA_QL + MLA_KVL
    dw_in = jnp.concatenate([dw_in_ext[:, :lat], dw_in_ext[:, lat + MLA_NOPE:lat + MLA_QK]], axis=1)
    return dx, dict(mix_norm=dg, mla_w_in=dw_in, mla_q_lora_gain=dqlg, mla_kv_lora_gain=dkvlg, mla_w_uq=dw_uq,
                    mla_w_ukv=dw_ukv, mla_q_gain=dqg[:, :MLA_QK], mla_k_gain=dkg[:, :MLA_QK], mla_w_out=dw_out)


def _xmem_fwd(x, mem, g, gm, wq, wkv, qg, kg, wo, tag):
    hq = _norm_rows(x, g, name=tag + "_norm")
    hm = _norm_rows(mem, gm, name=tag + "_mnorm")
    qp = _mm(hq, wq, name=tag + "_q")
    kv = _mm(hm, wkv, name=tag + "_kv")
    q = _head_norm(qp, qg, heads=MEM_HEADS, width=MEM_HD, scale=MEM_SCALE, name=tag + "_qn")
    kn = _head_norm(kv, kg, heads=MEM_HEADS, width=MEM_HD, colfn=lambda hh: 2 * hh, name=tag + "_kn")
    kvb = kv.reshape(-1, MEM_HEADS, 2, MEM_HD)[:, :, 1].reshape(-1, MEM_HEADS * MEM_HD).astype(BF16)
    o, lse = _attn_fwd(q, kn, kvb, sb=False, causal=False, heads=MEM_HEADS, dq=MEM_HD, dv=MEM_HD, group=MEM_HEADS,
                       name=tag + "_attn")
    y = _mm(o, wo, residual=x, name=tag + "_out")
    return y, (x, hq, hm, qp, kv, q, kn, kvb, o, lse)


def _xmem_bwd(dy, saved, mem, g, gm, wq, wkv, qg, kg, wo, tag):
    x, hq, hm, qp, kv, q, kn, kvb, o, lse = saved
    m = mem.shape[0]
    do = _mm(dy, wo, tb=True, name=tag + "_do")
    dwo = _mm(o, dy, ta=True, name=tag + "_dwo")
    dq, dk_t, dv_t = _attn_bwd(q, kn, kvb, o, do, lse, sb=False, causal=False, heads=MEM_HEADS, dq=MEM_HD, dv=MEM_HD,
                               name=tag + "_attn_bwd")
    dk, dv = dk_t.T, dv_t.T
    dqp, dqg = _head_norm_bwd(dq, qp, qg, heads=MEM_HEADS, width=MEM_HD, scale=MEM_SCALE, out_dtype=BF16,
                              name=tag + "_qn_bwd")
    dkp, dkg = _head_norm_bwd(dk, kv, kg, heads=MEM_HEADS, width=MEM_HD, colfn=lambda hh: 2 * hh, out_dtype=F32,
                              name=tag + "_kn_bwd")
    dkv = jnp.concatenate([dkp.reshape(m, MEM_HEADS, MEM_HD), dv.reshape(m, MEM_HEADS, MEM_HD)], axis=2).reshape(m, -1)
    dwkv = _mm(hm, dkv, ta=True, name=tag + "_dwkv")
    dhm = _mm(dkv, wkv, tb=True, name=tag + "_dhm")
    _, dgm = _norm_rows_bwd(dhm, mem, gm, None, name=tag + "_dmnorm")
    dwq = _mm(hq, dqp, ta=True, name=tag + "_dwq")
    dx, dg = _mm(dqp, wq, tb=True, norm_bwd=(x, g, dy), name=tag + "_dhq")
    return dx, dict(xmem_norm=dg, xmem_mem_norm=dgm, xmem_wq=dwq, xmem_wkv=dwkv, xmem_q_gain=dqg, xmem_k_gain=dkg,
                    xmem_wo=dwo)


def _local_step(x, mem, positions, tgt, w):
    tabs = _rope_tables(positions)
    even = _even_weights(w["sbg_w_in"][0], w["sbg_w_out"][0])
    mla = _mla_weights(w["mla_w_in"][0], w["mla_w_uq"][0], w["mla_w_ukv"][0], w["mla_w_out"][0], w["mla_q_gain"][0],
                       w["mla_k_gain"][0])
    bt = jnp.repeat(w["sgu_b"][0].T, SG_GD, axis=1)
    saved = []
    for l in range(2):
        t = f"l{l}"
        x, s_pre = _ffn_fwd(x, w["ffn_pre_norm"][l], w["ffn_pre_w_gu"][l], w["ffn_pre_w_down"][l], t + "_pre")
        if l == 0:
            x, s_mix = _even_fwd(x, w["mix_norm"][0], even, w["sgu_ln_gain"][0], w["sgu_ln_bias"][0], w["sgu_w"][0], bt,
                                 t + "_even")
        else:
            x, s_mix = _mla_fwd(x, w["mix_norm"][1], mla, w["mla_q_lora_gain"][0], w["mla_kv_lora_gain"][0], tabs,
                                t + "_mla")
        x, s_xm = _xmem_fwd(x, mem, w["xmem_norm"][l], w["xmem_mem_norm"][l], w["xmem_wq"][l], w["xmem_wkv"][l],
                            w["xmem_q_gain"][l], w["xmem_k_gain"][l], w["xmem_wo"][l], t + "_xm")
        x, s_post = _ffn_fwd(x, w["ffn_post_norm"][l], w["ffn_post_w_gu"][l], w["ffn_post_w_down"][l], t + "_post")
        saved.append((s_pre, s_mix, s_xm, s_post))
    loss, dx = _loss_grad(x, tgt, name="loss")
    grads = {}

    def put(name, l, val):
        grads.setdefault(name, {})[l] = val

    for l in (1, 0):
        t = f"l{l}"
        s_pre, s_mix, s_xm, s_post = saved[l]
        dx, dg, dwgu, dwd = _ffn_bwd(dx, s_post, w["ffn_post_norm"][l], w["ffn_post_w_gu"][l], w["ffn_post_w_down"][l],
                                     t + "_post")
        put("ffn_post_norm", l, dg), put("ffn_post_w_gu", l, dwgu), put("ffn_post_w_down", l, dwd)
        dx, gx = _xmem_bwd(dx, s_xm, mem, w["xmem_norm"][l], w["xmem_mem_norm"][l], w["xmem_wq"][l], w["xmem_wkv"][l],
                           w["xmem_q_gain"][l], w["xmem_k_gain"][l], w["xmem_wo"][l], t + "_xm")
        for k_, v_ in gx.items():
            put(k_, l, v_)
        if l == 0:
            dx, gm = _even_bwd(dx, s_mix, w["mix_norm"][0], even, w["sgu_ln_gain"][0], w["sgu_w"][0], bt, t + "_even")
        else:
            dx, gm = _mla_bwd(dx, s_mix, w["mix_norm"][1], mla, w["mla_q_lora_gain"][0], w["mla_kv_lora_gain"][0], tabs,
                              t + "_mla")
        for k_, v_ in gm.items():
            put(k_, l if k_ == "mix_norm" else 0, v_)
        dx, dg, dwgu, dwd = _ffn_bwd(dx, s_pre, w["ffn_pre_norm"][l], w["ffn_pre_w_gu"][l], w["ffn_pre_w_down"][l],
                                     t + "_pre")
        put("ffn_pre_norm", l, dg), put("ffn_pre_w_gu", l, dwgu), put("ffn_pre_w_down", l, dwd)
    return loss, dx, {k_: [v_[l] for l in sorted(v_)] for k_, v_ in grads.items()}


N_CHIPS = 4
PACK_COLS = 1024
PACK_ROW_MULTIPLE = 512


def _place():
    x, y, c = lax.axis_index("x"), lax.axis_index("y"), lax.axis_index("c")
    return x, y, c, [(1 - x, y), (x, 1 - y), (1 - x, 1 - y)]


def _hops(x, y, c):
    return ((x + 1 - c) % 2, (y + c) % 2), ((x + c) % 2, (y + 1 - c) % 2), (1 - x, 1 - y)


def _gather_chips(shard):
    R, C = shard.shape
    Rh = R // 2

    def body(x_ref, out_ref, send_sems, recv_sems):
        x, y, c = lax.axis_index("x"), lax.axis_index("y"), lax.axis_index("c")
        n1, n2, nd = _hops(x, y, c)
        me, q1, q2, qd = 2 * x + y, 2 * n1[0] + n1[1], 2 * n2[0] + n2[1], 2 * nd[0] + nd[1]

        def half(chip, core):
            return out_ref.at[chip, pl.ds(core * Rh, Rh), :]

        def copy(k, chip, core, to, src=None):
            return pltpu.make_async_remote_copy(src_ref=half(chip, core) if src is None else src, dst_ref=half(chip, core),
                                                send_sem=send_sems.at[k], recv_sem=recv_sems.at[k], device_id=to,
                                                device_id_type=MESH)

        own = x_ref.at[pl.ds(c * Rh, Rh), :]
        sibling = (x, y, 1 - c)
        sends = [copy(0, me, c, (*n1, c), src=own), copy(1, me, c, (*n2, c), src=own)]
        sends[0].start()
        sends[1].start()
        copy(0, q1, c, sibling).wait_recv()
        sends += [copy(2, q1, c, (*n2, c)), copy(3, q1, c, sibling)]
        sends[2].start()
        sends[3].start()
        copy(1, q2, c, sibling).wait_recv()
        sends.append(copy(4, q2, c, sibling))
        sends[4].start()
        copy(2, qd, c, sibling).wait_recv()
        sends.append(copy(5, qd, c, sibling))
        sends[5].start()
        copy(3, q2, 1 - c, sibling).wait_recv()
        copy(4, q1, 1 - c, sibling).wait_recv()
        copy(5, qd, 1 - c, sibling).wait_recv()
        for cp in sends:
            cp.wait_send()

    others = pl.pallas_call(
        body, name="gather_weights", out_shape=jax.ShapeDtypeStruct((N_CHIPS, R, C), shard.dtype),
        in_specs=[ANY], out_specs=ANY,
        scratch_shapes=[pltpu.SemaphoreType.DMA((6,)), pltpu.SemaphoreType.DMA((6,))])(shard)
    me = 2 * lax.axis_index("x") + lax.axis_index("y")
    return lax.dynamic_update_slice(others, shard[None], (me, 0, 0))


def _gather_devices(block):
    M, N = block.shape

    def body(x_ref, out_ref, send_sems, recv_sems, local_sem):
        x, y, c, chips = _place()
        me, sibling = (x, y, c), (x, y, 1 - c)

        def rows(px, py, pc):
            return out_ref.at[pl.ds((4 * px + 2 * py + pc) * M, M), :]

        def copy(k, blk, to, src=None):
            return pltpu.make_async_remote_copy(src_ref=rows(*blk) if src is None else src, dst_ref=rows(*blk),
                                                send_sem=send_sems.at[k], recv_sem=recv_sems.at[k], device_id=to,
                                                device_id_type=MESH)

        mine = pltpu.make_async_copy(x_ref, rows(*me), local_sem)
        mine.start()
        first = [copy(0, me, sibling, src=x_ref)]
        first += [copy(1 + j, me, (*chip, c), src=x_ref) for j, chip in enumerate(chips)]
        for cp in first:
            cp.start()
        passed = [copy(4 + j, (*chip, c), sibling) for j, chip in enumerate(chips)]
        for j, chip in enumerate(chips):
            copy(1 + j, (*chip, c), me).wait_recv()
            passed[j].start()
        copy(0, sibling, me).wait_recv()
        for j, chip in enumerate(chips):
            copy(4 + j, (*chip, 1 - c), me).wait_recv()
        for cp in first + passed:
            cp.wait_send()
        mine.wait()

    vmem = pl.BlockSpec(memory_space=pltpu.VMEM)
    return pl.pallas_call(
        body, name=f"gather_devices_{M}", out_shape=jax.ShapeDtypeStruct((8 * M, N), block.dtype),
        in_specs=[vmem], out_specs=vmem,
        scratch_shapes=[pltpu.SemaphoreType.DMA((7,)), pltpu.SemaphoreType.DMA((7,)), pltpu.SemaphoreType.DMA],
        compiler_params=pltpu.CompilerParams(vmem_limit_bytes=VMEM_LIMIT))(block)


def _swap_halves(g):
    n, R, C = g.shape
    Rh = R // 2

    def body(g_ref, a_ref, send_sem, recv_sem):
        x, y, c, _ = _place()
        cp = pltpu.make_async_remote_copy(src_ref=g_ref.at[:, pl.ds((1 - c) * Rh, Rh), :], dst_ref=a_ref,
                                          send_sem=send_sem, recv_sem=recv_sem, device_id=(x, y, 1 - c),
                                          device_id_type=MESH)
        cp.start()
        cp.wait()

    return pl.pallas_call(body, name="grad_swap_halves", out_shape=jax.ShapeDtypeStruct((n, Rh, C), g.dtype),
                          in_specs=[ANY], out_specs=ANY,
                          scratch_shapes=[pltpu.SemaphoreType.DMA, pltpu.SemaphoreType.DMA])(g)


def _add_picked(a, b, picks, *, a_row_half=None, out_dtype, name):
    n_out = picks.shape[0]
    _, rows, C = b.shape
    tr = _row_tile(rows, 512)
    nt = rows // tr
    half = jnp.zeros((1,), jnp.int32) if a_row_half is None else a_row_half

    def body(pick_ref, half_ref, a_ref, b_ref, o_ref):
        o_ref[...] = (a_ref[...].astype(F32) + b_ref[...].astype(F32)).astype(o_ref.dtype)

    spec = pltpu.PrefetchScalarGridSpec(
        num_scalar_prefetch=2, grid=(n_out, nt),
        in_specs=[pl.BlockSpec((1, tr, C), lambda j, i, pick, hf: (pick[j], hf[0] * nt + i, 0)),
                  pl.BlockSpec((1, tr, C), lambda j, i, pick, hf: (pick[j], i, 0))],
        out_specs=pl.BlockSpec((1, tr, C), lambda j, i, pick, hf: (j, i, 0)))
    return pl.pallas_call(body, name=name, grid_spec=spec, out_shape=jax.ShapeDtypeStruct((n_out, rows, C), out_dtype),
                          compiler_params=_params(2))(picks.astype(jnp.int32), half.astype(jnp.int32), a, b)


def _hop_exchange(src, hop, *, name):
    def body(s_ref, d_ref, send_sem, recv_sem):
        x, y, c = lax.axis_index("x"), lax.axis_index("y"), lax.axis_index("c")
        cp = pltpu.make_async_remote_copy(src_ref=s_ref, dst_ref=d_ref, send_sem=send_sem, recv_sem=recv_sem,
                                          device_id=(*_hops(x, y, c)[hop], c), device_id_type=MESH)
        cp.start()
        cp.wait()

    return pl.pallas_call(body, name=name, out_shape=jax.ShapeDtypeStruct(src.shape, src.dtype), in_specs=[ANY],
                          out_specs=ANY, scratch_shapes=[pltpu.SemaphoreType.DMA, pltpu.SemaphoreType.DMA])(src)


def _reduce_over_chips(g):
    x, y, c = lax.axis_index("x"), lax.axis_index("y"), lax.axis_index("c")
    n1, n2, _ = _hops(x, y, c)
    chip = lambda p: 2 * p[0] + p[1]
    near = jnp.stack([chip((x, y)), chip(n2)])
    far = jnp.stack([chip(n1), chip((1 - x, 1 - y))])
    half = c.reshape(1)
    sib = _swap_halves(g)
    kept = _add_picked(g, sib, near, a_row_half=half, out_dtype=F32, name="grad_add_near")
    sent = _add_picked(g, sib, far, a_row_half=half, out_dtype=BF16, name="grad_add_far")
    got = _hop_exchange(sent, 0, name="grad_hop_first")
    mine = _add_picked(kept, got, jnp.zeros((1,), jnp.int32), out_dtype=F32, name="grad_add_mine")
    theirs = _add_picked(kept, got, jnp.ones((1,), jnp.int32), out_dtype=BF16, name="grad_add_theirs")
    got = _hop_exchange(theirs, 1, name="grad_hop_second")
    total = _add_picked(mine, got, jnp.zeros((1,), jnp.int32), out_dtype=F32, name="grad_add_total")
    return _join_halves(total[0])


def _sum_slots(b, *, name):
    n, R, C = b.shape
    tr = _row_tile(R, 512)

    def body(b_ref, o_ref):
        acc = b_ref[0]
        for q in range(1, n):
            acc = acc + b_ref[q]
        o_ref[...] = acc

    return pl.pallas_call(body, name=name, grid=(R // tr,), in_specs=[pl.BlockSpec((n, tr, C), lambda i: (0, i, 0))],
                          out_specs=pl.BlockSpec((tr, C), lambda i: (i, 0)), out_shape=jax.ShapeDtypeStruct((R, C), F32),
                          compiler_params=_params(1))(b)


def _join_halves(r):
    Rh, C = r.shape

    def body(r_ref, o_ref, send_sem, recv_sem):
        x, y, c, _ = _place()
        own, other = o_ref.at[pl.ds(c * Rh, Rh), :], o_ref.at[pl.ds((1 - c) * Rh, Rh), :]
        cp = pltpu.make_async_remote_copy(src_ref=r_ref, dst_ref=own, send_sem=send_sem, recv_sem=recv_sem,
                                          device_id=(x, y, 1 - c), device_id_type=MESH)
        cp.start()
        pltpu.make_async_remote_copy(src_ref=r_ref, dst_ref=other, send_sem=send_sem, recv_sem=recv_sem,
                                     device_id=(x, y, 1 - c), device_id_type=MESH).wait_recv()
        cp.wait_send()

    theirs = pl.pallas_call(
        body, name="grad_join_halves", out_shape=jax.ShapeDtypeStruct((2 * Rh, C), r.dtype), in_specs=[ANY], out_specs=ANY,
        scratch_shapes=[pltpu.SemaphoreType.DMA, pltpu.SemaphoreType.DMA])(r)
    return lax.dynamic_update_slice(theirs, r, (lax.axis_index("c") * Rh, 0))


def _size(shape):
    size = 1
    for d in shape:
        size *= d
    return size


def _pack(pieces, cols, row_multiple, dtype):
    if any(p.size % cols for p in pieces):
        flat = jnp.concatenate([p.reshape(-1).astype(dtype) for p in pieces])
        pieces = [jnp.pad(flat, (0, -flat.shape[0] % cols))]
    rows = [p.reshape(-1, cols).astype(dtype) for p in pieces]
    pad = -sum(r.shape[0] for r in rows) % row_multiple
    return jnp.concatenate(rows + ([jnp.zeros((pad, cols), dtype)] if pad else []), axis=0)


def _unpack(buf, shapes):
    cols = buf.shape[1]
    if any(_size(s) % cols for s in shapes):
        flat, out, at = buf.reshape(-1), [], 0
        for shp in shapes:
            out.append(flat[at:at + _size(shp)].reshape(shp))
            at += _size(shp)
        return out
    out, at = [], 0
    for shp in shapes:
        out.append(buf[at:at + _size(shp) // cols].reshape(shp))
        at += _size(shp) // cols
    return out


SHARDED = (("ffn_pre_w_gu", 2), ("ffn_pre_w_down", 1), ("sbg_w_in", 2), ("sbg_w_out", 1), ("mla_w_in", 1),
           ("mla_w_uq", 2), ("mla_w_ukv", 2), ("mla_w_out", 1), ("xmem_wq", 1), ("xmem_wkv", 2), ("xmem_wo", 1),
           ("ffn_post_w_gu", 2), ("ffn_post_w_down", 1))
LORA_GAINS = ("mla_q_lora_gain", "mla_kv_lora_gain")
REPLICATED = ("ffn_pre_norm", "mix_norm", "sgu_ln_gain", "sgu_ln_bias", "sgu_w", "sgu_b", "mla_q_gain", "mla_k_gain",
              "xmem_norm", "xmem_mem_norm", "xmem_q_gain", "xmem_k_gain", "ffn_post_norm")
WEIGHTS = ("ffn_pre_norm", "ffn_pre_w_gu", "ffn_pre_w_down", "mix_norm", "sbg_w_in", "sgu_ln_gain", "sgu_ln_bias", "sgu_w",
           "sgu_b", "sbg_w_out", "mla_w_in", "mla_q_lora_gain", "mla_kv_lora_gain", "mla_w_uq", "mla_w_ukv", "mla_q_gain",
           "mla_k_gain", "mla_w_out", "xmem_norm", "xmem_mem_norm", "xmem_wq", "xmem_wkv", "xmem_q_gain", "xmem_k_gain",
           "xmem_wo", "ffn_post_norm", "ffn_post_w_gu", "ffn_post_w_down")
INPUTS = ("x", "mem", "positions") + WEIGHTS + ("loss_target",) + tuple("m_" + n for n in WEIGHTS) + tuple(
    "v_" + n for n in WEIGHTS)


def _step(a):
    x, y, c, _ = _place()
    chip = 2 * x + y
    shard_shapes = [a[n].shape for n, _ in SHARDED]

    gathered = _gather_chips(_pack([a[n] for n, _ in SHARDED], PACK_COLS, PACK_ROW_MULTIPLE, BF16))
    w, at = {}, 0
    for (n, ax), shp in zip(SHARDED, shard_shapes):
        rows = _size(shp) // PACK_COLS
        per_chip = gathered[:, at:at + rows].reshape((N_CHIPS,) + shp)
        at += rows
        w[n] = jnp.moveaxis(per_chip, 0, ax).reshape(shp[:ax] + (N_CHIPS * shp[ax],) + shp[ax + 1:])
    gains = jnp.zeros((8, LANE), F32)
    for r, n in enumerate(LORA_GAINS):
        gains = gains.at[r, :a[n].shape[1]].set(a[n][0])
    gains = _gather_devices(gains)
    for r, n in enumerate(LORA_GAINS):
        w[n] = jnp.concatenate([gains[16 * q + r, :a[n].shape[1]] for q in range(N_CHIPS)])[None, :]
    for n in REPLICATED:
        w[n] = a[n]

    loss, dx, grads = _local_step(a["x"][0], a["mem"][0], a["positions"][0], a["loss_target"][0], w)
    loss = lax.psum(loss, ("x", "y", "c"))
    small_names = REPLICATED + LORA_GAINS
    full = {n: jnp.stack(grads[n]).reshape(w[n].shape) for n in small_names}

    def cut(n, ax, q):
        size = w[n].shape[ax] // N_CHIPS
        return [lax.slice_in_dim(gl, q * size, (q + 1) * size, axis=ax - 1) for gl in grads[n]]

    g = jnp.stack([_pack([p for n, ax in SHARDED for p in cut(n, ax, q)], PACK_COLS, PACK_ROW_MULTIPLE, F32)
                   for q in range(N_CHIPS)])
    reduced = _reduce_over_chips(g)
    gw = dict(zip([n for n, _ in SHARDED], _unpack(reduced, shard_shapes)))

    small = _pack([full[n] for n in small_names], LANE, 256, F32)
    rows = small.shape[0]
    summed = _sum_slots(_gather_devices(small).reshape(8, rows, LANE), name="grad_sum_devices")
    for n, val in zip(small_names, _unpack(summed, [full[n].shape for n in small_names])):
        if n in LORA_GAINS:
            size = a[n].shape[1]
            val = lax.dynamic_slice_in_dim(val, chip * size, size, axis=1)
        gw[n] = val

    upd = {n: _adamw(a[n], gw[n], a["m_" + n], a["v_" + n], name="adamw_" + n) for n in WEIGHTS}
    return (loss, dx[None], *[gw[n] for n in WEIGHTS], *[upd[n][0] for n in WEIGHTS], *[upd[n][1] for n in WEIGHTS],
            *[upd[n][2] for n in WEIGHTS])


def kernel(x, mem, positions, ffn_pre_norm, ffn_pre_w_gu, ffn_pre_w_down, mix_norm, sbg_w_in, sgu_ln_gain,
           sgu_ln_bias, sgu_w, sgu_b, sbg_w_out, mla_w_in, mla_q_lora_gain, mla_kv_lora_gain, mla_w_uq, mla_w_ukv,
           mla_q_gain, mla_k_gain, mla_w_out, xmem_norm, xmem_mem_norm, xmem_wq, xmem_wkv, xmem_q_gain, xmem_k_gain,
           xmem_wo, ffn_post_norm, ffn_post_w_gu, ffn_post_w_down, loss_target, m_ffn_pre_norm, m_ffn_pre_w_gu,
           m_ffn_pre_w_down, m_mix_norm, m_sbg_w_in, m_sgu_ln_gain, m_sgu_ln_bias, m_sgu_w, m_sgu_b, m_sbg_w_out,
           m_mla_w_in, m_mla_q_lora_gain, m_mla_kv_lora_gain, m_mla_w_uq, m_mla_w_ukv, m_mla_q_gain, m_mla_k_gain,
           m_mla_w_out, m_xmem_norm, m_xmem_mem_norm, m_xmem_wq, m_xmem_wkv, m_xmem_q_gain, m_xmem_k_gain,
           m_xmem_wo, m_ffn_post_norm, m_ffn_post_w_gu, m_ffn_post_w_down, v_ffn_pre_norm, v_ffn_pre_w_gu,
           v_ffn_pre_w_down, v_mix_norm, v_sbg_w_in, v_sgu_ln_gain, v_sgu_ln_bias, v_sgu_w, v_sgu_b, v_sbg_w_out,
           v_mla_w_in, v_mla_q_lora_gain, v_mla_kv_lora_gain, v_mla_w_uq, v_mla_w_ukv, v_mla_q_gain, v_mla_k_gain,
           v_mla_w_out, v_xmem_norm, v_xmem_mem_norm, v_xmem_wq, v_xmem_wkv, v_xmem_q_gain, v_xmem_k_gain,
           v_xmem_wo, v_ffn_post_norm, v_ffn_post_w_gu, v_ffn_post_w_down):
    given = locals()
    return _step({n: given[n] for n in INPUTS})
```

```python
import jax
import jax.numpy as jnp
from jax import lax
from jax.experimental import pallas as pl
from jax.experimental.pallas import tpu as pltpu

F32, BF16 = jnp.float32, jnp.bfloat16
LANE = 128
VMEM_LIMIT = 56 * 1024 * 1024
EPS = 1e-6
SB_HEADS, SB_HD = 8, 64
SG_GROUPS, SG_GD, SG_CHUNK = 8, 64, 128
SB_W, SG_W = SB_HEADS * SB_HD, SG_GROUPS * SG_GD
MLA_HEADS, MLA_NOPE, MLA_ROPE, MLA_V = 16, 64, 32, 64
MLA_QK = MLA_NOPE + MLA_ROPE
MLA_QL, MLA_KVL = 512, 256
ROPE_THETA = 10000.0
MEM_HEADS, MEM_HD = 4, 256
SB_SCALE, MLA_SCALE, MEM_SCALE = SB_HD ** -0.5, MLA_QK ** -0.5, MEM_HD ** -0.5
ADAM_LR, ADAM_B1, ADAM_B2, ADAM_EPS, ADAM_WD, ADAM_STEP = 0.001, 0.9, 0.999, 1e-08, 0.01, 10
MESH = pl.DeviceIdType.MESH
ANY = pl.BlockSpec(memory_space=pl.ANY)


def _params(n_axes):
    return pltpu.CompilerParams(dimension_semantics=("arbitrary",) * n_axes, vmem_limit_bytes=VMEM_LIMIT)


MM_TILE_CAP = 1408
MM_VMEM_BUDGET = 40 * 1024 * 1024


def _tile(dim, cap):
    if dim <= cap:
        return dim
    best = max(t for t in range(LANE, cap + 1, LANE) if dim % t == 0)
    return best


def _k_scratch(count, tile, nk):
    return [pltpu.VMEM(tile, F32)] * count if nk > 1 else []


def _over_k_steps(prods, acc_refs, nk, finish):
    if nk == 1:
        finish(prods)
        return
    kk = pl.program_id(2)

    @pl.when(kk == 0)
    def _():
        for ref, p in zip(acc_refs, prods):
            ref[...] = p

    @pl.when(kk > 0)
    def _():
        for ref, p in zip(acc_refs, prods):
            ref[...] += p

    @pl.when(kk == nk - 1)
    def _():
        finish([ref[...] for ref in acc_refs])


def _mm(a, b, *, ta=False, tb=False, out_dtype=F32, scale=1.0, residual=None, bias=None, norm_bwd=None, a_off=(0, 0),
        b_off=(0, 0), m=None, n=None, k=None, name):
    am, ak = (a.shape[1], a.shape[0]) if ta else a.shape
    bk, bn = (b.shape[1], b.shape[0]) if tb else b.shape
    M, N, K = m or am, n or bn, k or ak
    tm, tn = _tile(M, MM_TILE_CAP if norm_bwd is None else MM_TILE_CAP // 2), _tile(N, MM_TILE_CAP)
    n_full = (residual is not None) + (2 if norm_bwd is not None else 0)
    fixed = tm * tn * (4 + 2 * jnp.dtype(out_dtype).itemsize + 8 * n_full)
    per_k = (tm * (2 * a.dtype.itemsize + 2) + tn * (2 * b.dtype.itemsize + 2))
    tk = _tile(K, max(LANE, (MM_VMEM_BUDGET - fixed) // per_k))
    nm, nn, nk = M // tm, N // tn, K // tk
    assert norm_bwd is None or nn == 1
    a_off = (a_off[0] // (tk if ta else tm), a_off[1] // (tm if ta else tk))
    b_off = (b_off[0] // (tn if tb else tk), b_off[1] // (tk if tb else tn))
    dims = (((0 if ta else 1,), (1 if tb else 0,)), ((), ()))
    n_out = 1 if norm_bwd is None else 2

    def body(*refs):
        a_ref, b_ref = refs[0], refs[1]
        n_in = len(ins)
        o_ref, extras, acc_refs = refs[n_in], refs[2:n_in], refs[n_in + n_out:]
        first_rows = pl.program_id(0) == 0

        def finish(total):
            out = total * scale
            for extra in (extras if norm_bwd is None else extras[:-3]):
                out = out + extra[...].astype(F32)
            if norm_bwd is not None:
                x_ref, g_ref, dres_ref = extras[-3:]
                dg_ref = refs[n_in + 1]
                dx, dg = _rmsnorm_bwd(out, x_ref[...], g_ref[...])
                out = dx + dres_ref[...]

                @pl.when(first_rows)
                def _():
                    dg_ref[...] = jnp.zeros_like(dg_ref)

                dg_ref[...] += dg
            o_ref[...] = out.astype(o_ref.dtype)

        prod = lax.dot_general(a_ref[...].astype(BF16), b_ref[...].astype(BF16), dims, preferred_element_type=F32)
        _over_k_steps([prod], acc_refs, nk, lambda totals: finish(totals[0]))

    (ao0, ao1), (bo0, bo1) = a_off, b_off
    a_spec = (pl.BlockSpec((tk, tm), lambda i, j, kk: (kk + ao0, i + ao1)) if ta
              else pl.BlockSpec((tm, tk), lambda i, j, kk: (i + ao0, kk + ao1)))
    b_spec = (pl.BlockSpec((tn, tk), lambda i, j, kk: (j + bo0, kk + bo1)) if tb
              else pl.BlockSpec((tk, tn), lambda i, j, kk: (kk + bo0, j + bo1)))
    o_spec = pl.BlockSpec((tm, tn), lambda i, j, kk: (i, j))
    ins, in_specs = [a, b], [a_spec, b_spec]
    if residual is not None:
        ins.append(residual)
        in_specs.append(o_spec)
    if bias is not None:
        ins.append(bias)
        in_specs.append(pl.BlockSpec((1, tn), lambda i, j, kk: (0, j)))
    out_specs, out_shape = o_spec, jax.ShapeDtypeStruct((M, N), out_dtype)
    if norm_bwd is not None:
        x, gain, dres = norm_bwd
        row = pl.BlockSpec((1, tn), lambda i, j, kk: (0, 0))
        ins += [x, gain.reshape(1, N), dres]
        in_specs += [o_spec, row, o_spec]
        out_specs, out_shape = [o_spec, row], [out_shape, jax.ShapeDtypeStruct((1, N), F32)]
    return pl.pallas_call(
        body, name=name, grid=(nm, nn, nk), in_specs=in_specs, out_specs=out_specs, out_shape=out_shape,
        scratch_shapes=_k_scratch(1, (tm, tn), nk), compiler_params=_params(3))(*ins)


def _mm_swiglu(h, wgu, *, name):
    M, K = h.shape
    F = wgu.shape[1] // 2
    tm, tn, tk = _tile(M, 512), _tile(F, MM_TILE_CAP), _tile(K, 1024)
    nm, nf, nk = M // tm, F // tn, K // tk

    def body(h_ref, wg_ref, wu_ref, g_ref, u_ref, a_ref, *acc_refs):
        def finish(totals):
            g, u = totals
            g_ref[...] = g.astype(BF16)
            u_ref[...] = u.astype(BF16)
            a_ref[...] = (g * jax.nn.sigmoid(g) * u).astype(BF16)

        hb = h_ref[...]
        _over_k_steps([jnp.dot(hb, wg_ref[...], preferred_element_type=F32),
                       jnp.dot(hb, wu_ref[...], preferred_element_type=F32)], acc_refs, nk, finish)

    o_spec = pl.BlockSpec((tm, tn), lambda j, i, kk: (i, j))
    shp = jax.ShapeDtypeStruct((M, F), BF16)
    return pl.pallas_call(
        body, name=name, grid=(nf, nm, nk),
        in_specs=[pl.BlockSpec((tm, tk), lambda j, i, kk: (i, kk)),
                  pl.BlockSpec((tk, tn), lambda j, i, kk: (kk, j)),
                  pl.BlockSpec((tk, tn), lambda j, i, kk: (kk, j + nf))],
        out_specs=[o_spec, o_spec, o_spec], out_shape=[shp, shp, shp],
        scratch_shapes=_k_scratch(2, (tm, tn), nk), compiler_params=_params(3))(h, wgu, wgu)


def _mm_dswiglu(dy, wd, gate, up, *, scale, name):
    M, K = dy.shape
    F = wd.shape[0]
    tm, tn, tk = _tile(M, 512), _tile(F, MM_TILE_CAP), _tile(K, 1024)
    nm, nf, nk = M // tm, F // tn, K // tk

    def body(dy_ref, wd_ref, g_ref, u_ref, dg_ref, du_ref, *acc_refs):
        def finish(totals):
            da = totals[0] * scale
            g, u = g_ref[...].astype(F32), u_ref[...].astype(F32)
            sg = jax.nn.sigmoid(g)
            du_ref[...] = (da * g * sg).astype(BF16)
            dg_ref[...] = (da * u * sg * (1.0 + g * (1.0 - sg))).astype(BF16)

        _over_k_steps([_nt(dy_ref[...].astype(BF16), wd_ref[...])], acc_refs, nk, finish)

    o_spec = pl.BlockSpec((tm, tn), lambda j, i, kk: (i, j))
    shp = jax.ShapeDtypeStruct((M, F), BF16)
    return pl.pallas_call(
        body, name=name, grid=(nf, nm, nk),
        in_specs=[pl.BlockSpec((tm, tk), lambda j, i, kk: (i, kk)),
                  pl.BlockSpec((tn, tk), lambda j, i, kk: (j, kk)), o_spec, o_spec],
        out_specs=[o_spec, o_spec], out_shape=[shp, shp],
        scratch_shapes=_k_scratch(1, (tm, tn), nk), compiler_params=_params(3))(dy, wd, gate, up)


HEAD_ROWS = 1024


def _row_tile(rows, cap):
    t = cap
    while t >= 8:
        if rows % t == 0:
            return t
        t //= 2
    return rows


def _rowwise(fn, rows, consts, outs, sums=(), hsums=(), *, heads=None, tm=256, name):
    rows = [r if isinstance(r, tuple) else (r, r.shape[1], None) for r in rows]
    rows = [r if len(r) == 4 else (*r, False) for r in rows]
    S = rows[0][0].shape[0]
    tm = _row_tile(S, tm)
    nh = heads or 1
    n_r, n_c, n_o, n_h, n_s = len(rows), len(consts), len(outs), len(hsums), len(sums)

    def body(*refs):
        r = [x[...].T if row[3] else x[...] for x, row in zip(refs, rows)]
        c = [x[...] for x in refs[n_r:n_r + n_c]]
        o_refs = refs[n_r + n_c:n_r + n_c + n_o]
        h_refs = refs[n_r + n_c + n_o:n_r + n_c + n_o + n_h]
        s_refs = refs[n_r + n_c + n_o + n_h:]
        res = fn(*r, *c)
        res = res if isinstance(res, (tuple, list)) else (res,)
        for ref, val in zip(o_refs, res[:n_o]):
            ref[...] = val.astype(ref.dtype)
        if n_h:
            @pl.when(pl.program_id(1) == 0)
            def _():
                for ref in h_refs:
                    ref[...] = jnp.zeros_like(ref)
            for ref, val in zip(h_refs, res[n_o:n_o + n_h]):
                ref[...] += val
        if n_s:
            @pl.when((pl.program_id(0) == 0) & (pl.program_id(1) == 0))
            def _():
                for ref in s_refs:
                    ref[...] = jnp.zeros_like(ref)
            for ref, val in zip(s_refs, res[n_o + n_h:]):
                ref[...] += val

    def col(colfn):
        return (lambda i, h: (i, 0)) if colfn is None else (lambda i, h: (i, colfn(h)))

    in_specs = [pl.BlockSpec((w, tm), lambda i, h, cf=cf: (cf(h), i)) if flipped else pl.BlockSpec((tm, w), col(cf))
                for _, w, cf, flipped in rows]
    in_specs += [pl.BlockSpec(a.shape, lambda i, h, nd=a.ndim: (0,) * nd) for a in consts]
    out_specs = [pl.BlockSpec((tm, w // nh), (lambda i, h: (i, h)) if heads else (lambda i, h: (i, 0))) for w, _ in outs]
    out_specs += [pl.BlockSpec((tm, w), lambda i, h: (i, 0)) for w in hsums]
    out_specs += [pl.BlockSpec(sh, lambda i, h, nd=len(sh): (0,) * nd) for sh in sums]
    out_shape = [jax.ShapeDtypeStruct((S, w), dt) for w, dt in outs]
    out_shape += [jax.ShapeDtypeStruct((S, w), F32) for w in hsums]
    out_shape += [jax.ShapeDtypeStruct(sh, F32) for sh in sums]
    return pl.pallas_call(body, name=name, grid=(S // tm, nh), in_specs=in_specs, out_specs=out_specs,
                          out_shape=out_shape, compiler_params=_params(2))(*[row[0] for row in rows], *consts)


def _rms(x, width=None):
    width = width or x.shape[-1]
    return lax.rsqrt(jnp.sum(x * x, axis=-1, keepdims=True) * (1.0 / width) + EPS)


def _rmsnorm_fwd(x, g, width=None):
    return x * _rms(x, width) * g


def _rmsnorm_bwd(dy, x, g, width=None):
    width = width or x.shape[-1]
    r = _rms(x, width)
    xn = x * r
    dxn = dy * g
    dx = r * (dxn - xn * (jnp.sum(dxn * xn, axis=-1, keepdims=True) * (1.0 / width)))
    return dx, jnp.sum(dy * xn, axis=0, keepdims=True)


def _norm_rows(x, g, *, name, out_dtype=BF16):
    D = x.shape[1]
    return _rowwise(lambda xv, gv: _rmsnorm_fwd(xv.astype(F32), gv), [x], [g.reshape(1, D)], [(D, out_dtype)],
                    name=name)[0]


def _norm_rows_bwd(dh, x, g, dres, *, name):
    D = x.shape[1]

    def fn(dhv, xv, *rest):
        dx, dg = _rmsnorm_bwd(dhv.astype(F32), xv, rest[-1])
        return (dx + rest[0] if dres is not None else dx), dg

    rows = [dh, x] + ([dres] if dres is not None else [])
    return _rowwise(fn, rows, [g.reshape(1, D)], [(D, F32)], [(1, D)], name=name)


def _softplus(z):
    return jnp.where(z > 20.0, z, jnp.log(1.0 + jnp.exp(z)))


def _running_sum(v, u, split=True):
    if not split:
        return jnp.dot(v.astype(BF16), u, preferred_element_type=F32)
    hi = lax.bitcast_convert_type(lax.bitcast_convert_type(v, jnp.uint32) & jnp.uint32(0xFFFF0000), F32)
    return (jnp.dot(hi.astype(BF16), u, preferred_element_type=F32)
            + jnp.dot((v - hi).astype(BF16), u, preferred_element_type=F32))


def _triangle(tk, inclusive_prefix):
    j, s = lax.broadcasted_iota(jnp.int32, (tk, tk), 0), lax.broadcasted_iota(jnp.int32, (tk, tk), 1)
    return ((j <= s) if inclusive_prefix else (j > s)).astype(BF16)


def _nt(a, b):
    return lax.dot_general(a, b, (((1,), (1,)), ((), ())), preferred_element_type=F32)


def _tn(a, b):
    return lax.dot_general(a, b, (((0,), (0,)), ((), ())), preferred_element_type=F32)


ATT_TQ, ATT_TK = 512, 512
SB_SUB = 256
FWD_GROUP = 2


def _attn_fwd(q, k, v, *, sb, causal, heads, dq, dv, group=1, kcol=None, vcol=None, sum_lane=None, name):
    S, Sk = q.shape[0], k.shape[0]
    tq, tk = min(ATT_TQ, S), min(ATT_TK, Sk)
    sub = min(SB_SUB, tk) if sb else tk
    assert tq % sub == 0 or not causal
    kcol = kcol or (lambda h: h)
    vcol = vcol or (lambda h: h)
    members = range(group)

    def body(*refs):
        if sb:
            q_ref, k_ref, v_ref, u_ref, o_ref, lse_ref, acc_ref, r_ref = refs
            r_ref[...] = jnp.zeros_like(r_ref)
        else:
            q_ref, k_ref, v_ref, o_ref, lse_ref, acc_ref, m_ref, l_ref = refs
            m_ref[...] = jnp.full_like(m_ref, -1e30)
            l_ref[...] = jnp.zeros_like(l_ref)
        first_row = pl.program_id(1) * tq
        qb = [q_ref[:, hh * dq:(hh + 1) * dq] for hh in members]
        acc_ref[...] = jnp.zeros_like(acc_ref)
        nblk = (first_row + tq) // sub if causal else Sk // sub
        nfull = (first_row + (0 if sb else 1)) // sub if causal else nblk
        n_cut = tq // sub if causal else 0

        def scores(jj):
            off = pl.multiple_of(jj * sub, sub)
            return tuple(_nt(qb[hh], k_ref[pl.ds(off, sub), hh * dq:(hh + 1) * dq]) for hh in members)

        def weigh(jj, scores_now, masked):
            off = pl.multiple_of(jj * sub, sub)
            if masked:
                kpos = off + lax.broadcasted_iota(jnp.int32, (tq, sub), 1)
                qpos = first_row + lax.broadcasted_iota(jnp.int32, (tq, sub), 0)
                valid = (kpos < qpos) if sb else (kpos <= qpos)
            for hh in members:
                vb = v_ref[pl.ds(off, sub), hh * dv:(hh + 1) * dv]
                s = scores_now[hh]
                if sb:
                    sp = _softplus(s)
                    ls = jnp.where(valid, -sp, 0.0) if masked else -sp
                    w = jnp.exp(s - sp + r_ref[hh] + _running_sum(ls, u_ref[...]))
                    if masked:
                        w = jnp.where(valid, w, 0.0)
                    acc_ref[hh] += jnp.dot(w.astype(BF16), vb, preferred_element_type=F32)
                    r_ref[hh] += jnp.sum(ls, axis=1, keepdims=True)
                else:
                    if masked:
                        s = jnp.where(valid, s, -1e30)
                    m_old = m_ref[hh]
                    m_new = jnp.maximum(m_old, jnp.max(s, axis=1, keepdims=True))
                    p = jnp.exp(s - m_new)
                    alpha = jnp.exp(m_old - m_new)
                    if sum_lane is None:
                        l_ref[hh] = alpha * l_ref[hh] + jnp.sum(p, axis=1, keepdims=True)
                    acc_ref[hh] = alpha * acc_ref[hh] + jnp.dot(p.astype(BF16), vb, preferred_element_type=F32)
                    m_ref[hh] = m_new

        if sb:
            s_cur = scores(nblk - 1)
            for cut in range(n_cut):
                s_next = scores(jnp.maximum(nblk - 2 - cut, 0))
                weigh(nblk - 1 - cut, s_cur, True)
                s_cur = s_next

            def step(t, s_now):
                s_next = scores(jnp.maximum(nfull - 2 - t, 0))
                weigh(nfull - 1 - t, s_now, False)
                return s_next

            lax.fori_loop(0, nfull, step, s_cur)
        else:
            n_loop = nfull if causal else nblk - 1

            def step(t, s_now):
                s_next = scores(jnp.minimum(t + 1, nblk - 1))
                weigh(t, s_now, False)
                return s_next

            s_cur = lax.fori_loop(0, n_loop, step, scores(0))
            tail = n_cut if causal else 1
            for last in range(tail):
                s_next = scores(n_loop + last + 1) if last + 1 < tail else None
                weigh(n_loop + last, s_cur, causal)
                s_cur = s_next
        for hh in members:
            cols = slice(hh * dv, (hh + 1) * dv)
            if sb:
                o_ref[:, cols] = acc_ref[hh]
                lse_ref[hh] = r_ref[hh]
            else:
                acc = acc_ref[hh]
                l = l_ref[hh] if sum_lane is None else acc[:, sum_lane:sum_lane + 1]
                o_ref[:, cols] = acc / l
                lse_ref[hh] = m_ref[hh] + jnp.log(l)

    in_specs = [pl.BlockSpec((tq, group * dq), lambda g, i: (i, g)),
                pl.BlockSpec((Sk, group * dq), lambda g, i: (0, kcol(g))),
                pl.BlockSpec((Sk, group * dv), lambda g, i: (0, vcol(g)))]
    ins = [q, k, v]
    scratch = [pltpu.VMEM((group, tq, dv), F32), pltpu.VMEM((group, tq, 1), F32)]
    if sb:
        ins.append(_triangle(sub, inclusive_prefix=False))
        in_specs.append(pl.BlockSpec((sub, sub), lambda g, i: (0, 0)))
    else:
        scratch.append(pltpu.VMEM((group, tq, 1), F32))
    out_specs = [pl.BlockSpec((tq, group * dv), lambda g, i: (i, g)), pl.BlockSpec((group, tq, 1), lambda g, i: (g, i, 0))]
    out_shape = [jax.ShapeDtypeStruct((S, heads * dv), F32), jax.ShapeDtypeStruct((heads, S, 1), F32)]
    return pl.pallas_call(body, name=name, grid=(heads // group, S // tq), in_specs=in_specs, out_specs=out_specs,
                          out_shape=out_shape, scratch_shapes=scratch, compiler_params=_params(2))(*ins)


def _attn_bwd(q, k, v, o, do, lse, *, sb, causal, heads, dq, dv, kcol=None, vcol=None, name):
    S, Sk = q.shape[0], k.shape[0]
    tq, tk = min(ATT_TQ, S), min(ATT_TK, Sk)
    sub = min(SB_SUB, tk) if sb else tk
    assert tq % sub == 0 or not causal
    nq = S // tq
    kcol = kcol or (lambda h: h)
    vcol = vcol or (lambda h: h)

    def body(*refs):
        if sb:
            q_ref, k_ref, v_ref, o_ref, do_ref, lse_ref, u_ref, dq_ref, dk_ref, dv_ref, acc_ref, r_ref, re_ref = refs
            r_ref[...] = jnp.zeros_like(r_ref)
            re_ref[...] = jnp.zeros_like(re_ref)
        else:
            q_ref, k_ref, v_ref, o_ref, do_ref, lse_ref, dq_ref, dk_ref, dv_ref, acc_ref = refs
        first_row = pl.program_id(1) * tq

        @pl.when(first_row == 0)
        def _():
            dk_ref[...] = jnp.zeros_like(dk_ref)
            dv_ref[...] = jnp.zeros_like(dv_ref)

        qb = q_ref[...]
        dof = do_ref[...].astype(F32)
        dob = dof.astype(BF16)
        q_t, do_t = qb.T, dob.T
        if not sb:
            dlt = jnp.sum(dof * o_ref[...], axis=1, keepdims=True)
        acc_ref[...] = jnp.zeros_like(acc_ref)
        nblk = (first_row + tq) // sub if causal else Sk // sub
        nfull = (first_row + (0 if sb else 1)) // sub if causal else nblk
        n_cut = tq // sub if causal else 0

        def products(jj):
            off = pl.multiple_of(jj * sub, sub)
            return _nt(qb, k_ref[pl.ds(off, sub), :]), _nt(dob, v_ref[pl.ds(off, sub), :])

        def piece(jj, now, masked):
            off = pl.multiple_of(jj * sub, sub)
            kb = k_ref[pl.ds(off, sub), :]
            s, dp = now
            if masked:
                qpos = first_row + lax.broadcasted_iota(jnp.int32, (tq, sub), 0)
                kpos = off + lax.broadcasted_iota(jnp.int32, (tq, sub), 1)
                valid = (kpos < qpos) if sb else (kpos <= qpos)
            if sb:
                u = u_ref[...]
                sp = _softplus(s)
                ls = jnp.where(valid, -sp, 0.0) if masked else -sp
                lb = s - sp
                w = jnp.exp(lb + (lse_ref[0] - (r_ref[...] + _running_sum(ls, u))))
                if masked:
                    w = jnp.where(valid, w, 0.0)
                e = dp * w
                ds = e - jnp.exp(lb) * (re_ref[...] + _running_sum(e, u, split=False))
                if masked:
                    ds = jnp.where(valid, ds, 0.0)
                r_ref[...] += jnp.sum(ls, axis=1, keepdims=True)
                re_ref[...] += jnp.sum(e, axis=1, keepdims=True)
            else:
                w = jnp.exp(s - lse_ref[0])
                if masked:
                    w = jnp.where(valid, w, 0.0)
                ds = w * (dp - dlt)
            dsb = ds.astype(BF16)
            dv_ref[:, pl.ds(off, sub)] += jnp.dot(do_t, w.astype(BF16), preferred_element_type=F32)
            dk_ref[:, pl.ds(off, sub)] += jnp.dot(q_t, dsb, preferred_element_type=F32)
            acc_ref[...] += jnp.dot(dsb, kb, preferred_element_type=F32)

        n_loop = nfull if causal else nblk - 1
        per_trip = tk // sub

        def steps(first, count, masked):
            ready = [products(first + c) for c in range(count)]
            for c in range(count):
                piece(first + c, ready[c], masked)

        def trip(t, carry):
            steps(t * per_trip, per_trip, False)
            return carry

        lax.fori_loop(0, n_loop // per_trip, trip, 0)
        steps(n_loop, n_cut if causal else 1, causal)
        dq_ref[...] = acc_ref[...]

    ins = [q, k, v, o, do]
    in_specs = [pl.BlockSpec((tq, dq), lambda h, i: (i, h)),
                pl.BlockSpec((Sk, dq), lambda h, i: (0, kcol(h))),
                pl.BlockSpec((Sk, dv), lambda h, i: (0, vcol(h))),
                pl.BlockSpec((tq, dv), lambda h, i: (i, h)),
                pl.BlockSpec((tq, dv), lambda h, i: (i, h))]
    scratch = [pltpu.VMEM((tq, dq), F32)]
    ins.append(lse)
    in_specs.append(pl.BlockSpec((1, tq, 1), lambda h, i: (h, i, 0)))
    if sb:
        ins.append(_triangle(sub, inclusive_prefix=True))
        in_specs.append(pl.BlockSpec((sub, sub), lambda h, i: (0, 0)))
        scratch += [pltpu.VMEM((tq, 1), F32), pltpu.VMEM((tq, 1), F32)]
    out_specs = [pl.BlockSpec((tq, dq), lambda h, i: (i, h)),
                 pl.BlockSpec((dq, Sk), lambda h, i: (h, 0)),
                 pl.BlockSpec((dv, Sk), lambda h, i: (h, 0))]
    out_shape = [jax.ShapeDtypeStruct((S, heads * dq), F32), jax.ShapeDtypeStruct((heads * dq, Sk), F32),
                 jax.ShapeDtypeStruct((heads * dv, Sk), F32)]
    return pl.pallas_call(body, name=name, grid=(heads, nq), in_specs=in_specs, out_specs=out_specs,
                          out_shape=out_shape, scratch_shapes=scratch, compiler_params=_params(2))(*ins)


GELU_C = 0.7978845608028654
assert 2 * SG_GD == LANE and SG_CHUNK == LANE


def _gelu(z):
    t = jnp.tanh(GELU_C * (z + 0.044715 * z * z * z))
    return 0.5 * z * (1.0 + t), t


def _gelu_grad(z, t):
    return 0.5 * (1.0 + t) + 0.5 * z * (1.0 - t * t) * GELU_C * (1.0 + 3.0 * 0.044715 * z * z)


def _layernorm_parts(g):
    d = g - jnp.mean(g, axis=-1, keepdims=True)
    rstd = lax.rsqrt(jnp.mean(d * d, axis=-1, keepdims=True) + EPS)
    return d * rstd, rstd


def _gelu_ln(z, gain, bias, *, name):
    def fn(zv, gn, bs):
        a, _ = _gelu(zv)
        y, _ = _layernorm_parts(a[:, SG_W:])
        return a[:, :SG_W], y * gn + bs

    return _rowwise(fn, [z], [gain.reshape(1, SG_W), bias.reshape(1, SG_W)], [(SG_W, F32), (SG_W, BF16)], name=name)


def _gelu_ln_bwd(z, du, dgl, gain, *, name):
    def fn(zv, duv, dglv, gn):
        a, t = _gelu(zv)
        y, rstd = _layernorm_parts(a[:, SG_W:])
        dy = dglv * gn
        dgg = rstd * (dy - jnp.mean(dy, axis=-1, keepdims=True) - y * jnp.mean(dy * y, axis=-1, keepdims=True))
        dz = jnp.concatenate([duv, dgg], axis=1) * _gelu_grad(zv, t)
        return dz, jnp.sum(dglv * y, axis=0, keepdims=True), jnp.sum(dglv, axis=0, keepdims=True)

    return _rowwise(fn, [z, du, dgl], [gain.reshape(1, SG_W)], [(2 * SG_W, BF16)], [(1, SG_W), (1, SG_W)], name=name)


def _sg_masks():
    tri = lax.broadcasted_iota(jnp.int32, (SG_CHUNK, SG_CHUNK), 0) >= lax.broadcasted_iota(jnp.int32, (SG_CHUNK, SG_CHUNK), 1)
    first = lax.broadcasted_iota(jnp.int32, (SG_CHUNK, LANE), 1) < SG_GD
    return tri, first


def _spatial(gl, u, w, bt, *, name):
    S = gl.shape[0]
    tm = _row_tile(S, 512)
    nch = tm // SG_CHUNK

    def body(gl_ref, u_ref, w_ref, bt_ref, o_ref):
        tri, first = _sg_masks()
        for p in range(SG_W // LANE):
            cols = slice(p * LANE, (p + 1) * LANE)
            wa = jnp.where(tri, w_ref[2 * p], 0.0).astype(BF16)
            wb = jnp.where(tri, w_ref[2 * p + 1], 0.0).astype(BF16)
            for ci in range(nch):
                rws = slice(ci * SG_CHUNK, (ci + 1) * SG_CHUNK)
                g = gl_ref[rws, cols]
                zero = jnp.zeros_like(g)
                mixed = (jnp.dot(wa, jnp.where(first, g, zero), preferred_element_type=F32)
                         + jnp.dot(wb, jnp.where(first, zero, g), preferred_element_type=F32) + bt_ref[:, cols])
                o_ref[rws, cols] = u_ref[rws, cols] * mixed

    row = pl.BlockSpec((tm, SG_W), lambda i: (i, 0))
    return pl.pallas_call(
        body, name=name, grid=(S // tm,),
        in_specs=[row, row, pl.BlockSpec(w.shape, lambda i: (0, 0, 0)), pl.BlockSpec(bt.shape, lambda i: (0, 0))],
        out_specs=row, out_shape=jax.ShapeDtypeStruct((S, SG_W), F32), compiler_params=_params(1))(gl, u, w, bt)


def _spatial_bwd(d_o, gl, u, w, bt, *, name):
    S = gl.shape[0]
    tm = _row_tile(S, 512)
    nch = tm // SG_CHUNK
    nsteps = S // tm

    def body(do_ref, gl_ref, u_ref, w_ref, bt_ref, du_ref, dgl_ref, dw_ref, db_ref, dbt_ref):
        tri, first = _sg_masks()
        step = pl.program_id(0)

        @pl.when(step == 0)
        def _():
            dw_ref[...] = jnp.zeros_like(dw_ref)
            dbt_ref[...] = jnp.zeros_like(dbt_ref)

        for p in range(SG_W // LANE):
            cols = slice(p * LANE, (p + 1) * LANE)
            wa = jnp.where(tri, w_ref[2 * p], 0.0).astype(BF16)
            wb = jnp.where(tri, w_ref[2 * p + 1], 0.0).astype(BF16)
            for ci in range(nch):
                rws = slice(ci * SG_CHUNK, (ci + 1) * SG_CHUNK)
                g = gl_ref[rws, cols]
                zero = jnp.zeros_like(g)
                mixed = (jnp.dot(wa, jnp.where(first, g, zero), preferred_element_type=F32)
                         + jnp.dot(wb, jnp.where(first, zero, g), preferred_element_type=F32) + bt_ref[:, cols])
                dov = do_ref[rws, cols]
                du_ref[rws, cols] = dov * mixed
                dm = dov * u_ref[rws, cols]
                dbt_ref[:, cols] += dm
                dma = jnp.where(first, dm, 0.0).astype(BF16)
                dmb = jnp.where(first, 0.0, dm).astype(BF16)
                dw_ref[2 * p] += jnp.where(tri, _nt(dma, g), 0.0)
                dw_ref[2 * p + 1] += jnp.where(tri, _nt(dmb, g), 0.0)
                dgl_ref[rws, cols] = _tn(wa, dma) + _tn(wb, dmb)

        @pl.when(step == nsteps - 1)
        def _():
            lane = lax.broadcasted_iota(jnp.int32, (SG_CHUNK, LANE), 1)
            acc = jnp.zeros((SG_CHUNK, LANE), F32)
            for p in range(SG_W // LANE):
                blk = dbt_ref[:, p * LANE:(p + 1) * LANE]
                sa = jnp.sum(jnp.where(first, blk, 0.0), axis=1, keepdims=True)
                sb_ = jnp.sum(jnp.where(first, 0.0, blk), axis=1, keepdims=True)
                acc = acc + jnp.where(lane == 2 * p, sa, 0.0) + jnp.where(lane == 2 * p + 1, sb_, 0.0)
            db_ref[...] = acc

    row = pl.BlockSpec((tm, SG_W), lambda i: (i, 0))
    return pl.pallas_call(
        body, name=name, grid=(nsteps,),
        in_specs=[row, row, row, pl.BlockSpec(w.shape, lambda i: (0, 0, 0)), pl.BlockSpec(bt.shape, lambda i: (0, 0))],
        out_specs=[row, row, pl.BlockSpec(w.shape, lambda i: (0, 0, 0)), pl.BlockSpec((SG_CHUNK, LANE), lambda i: (0, 0))],
        out_shape=[jax.ShapeDtypeStruct((S, SG_W), F32), jax.ShapeDtypeStruct((S, SG_W), F32),
                   jax.ShapeDtypeStruct(w.shape, F32), jax.ShapeDtypeStruct((SG_CHUNK, LANE), F32)],
        scratch_shapes=[pltpu.VMEM((SG_CHUNK, SG_W), F32)], compiler_params=_params(1))(d_o, gl, u, w, bt)


ROPE_HALF = MLA_ROPE // 2
KR_COL = (MLA_QL + MLA_KVL) // LANE
MLA_IN_PAD = MLA_QL + MLA_KVL + LANE


def _rope_tables(positions):
    inv_freq = ROPE_THETA ** (-jnp.arange(ROPE_HALF, dtype=F32) / ROPE_HALF)
    ang = positions.astype(F32)[:, None] * inv_freq
    cos, sin = jnp.cos(ang), jnp.sin(ang)
    S = positions.shape[0]
    z16, tail = jnp.zeros((S, ROPE_HALF), F32), jnp.zeros((S, LANE - MLA_QK), F32)
    ones = jnp.ones((S, MLA_NOPE), F32)
    zeros = jnp.zeros((S, MLA_NOPE), F32)
    return (jnp.concatenate([ones, cos, cos, tail], axis=1), jnp.concatenate([zeros, z16, sin, tail], axis=1),
            jnp.concatenate([zeros, -sin, z16, tail], axis=1))


def _rope(x, cos, sa, sb):
    return x * cos + pltpu.roll(x, ROPE_HALF, 1) * sa + pltpu.roll(x, LANE - ROPE_HALF, 1) * sb


def _rope_t(dy, cos, sa, sb):
    return dy * cos + pltpu.roll(dy * sa, LANE - ROPE_HALF, 1) + pltpu.roll(dy * sb, ROPE_HALF, 1)


def _mla_lora(P, qlg, kvlg, *, name):
    def fn(pv, a, b):
        return _rmsnorm_fwd(pv[:, :MLA_QL], a), _rmsnorm_fwd(pv[:, MLA_QL:MLA_QL + MLA_KVL], b)

    return _rowwise(fn, [P], [qlg.reshape(1, MLA_QL), kvlg.reshape(1, MLA_KVL)], [(MLA_QL, BF16), (MLA_KVL, BF16)], name=name)


def _mla_lora_bwd(dcq, dckv, dkr, P, qlg, kvlg, *, name):
    def fn(d1, d2, d3, pv, a, b):
        x1, g1 = _rmsnorm_bwd(d1, pv[:, :MLA_QL], a)
        x2, g2 = _rmsnorm_bwd(d2, pv[:, MLA_QL:MLA_QL + MLA_KVL], b)
        return jnp.concatenate([x1, x2, d3], axis=1), g1, g2

    return _rowwise(fn, [dcq, dckv, dkr, P], [qlg.reshape(1, MLA_QL), kvlg.reshape(1, MLA_KVL)], [(MLA_IN_PAD, BF16)],
                    [(1, MLA_QL), (1, MLA_KVL)], name=name)


def _mla_qk(q_pre, k_pre, P, tabs, qg, kg, *, name):
    def fn(qp, kp, kr, c, a, b, qgv, kgv):
        return (_rope(_rmsnorm_fwd(qp, qgv, MLA_QK), c, a, b) * MLA_SCALE,
                _rope(_rmsnorm_fwd(kp + kr, kgv, MLA_QK), c, a, b))

    hcol = lambda h: h
    rows = [(q_pre, LANE, hcol), (k_pre, LANE, hcol), (P, LANE, lambda h: KR_COL), *tabs]
    w = MLA_HEADS * LANE
    return _rowwise(fn, rows, [qg, kg], [(w, BF16), (w, BF16)], heads=MLA_HEADS, tm=HEAD_ROWS, name=name)


def _mla_qk_bwd(dq, dk_t, q_pre, k_pre, P, tabs, qg, kg, *, name):
    def fn(dqv, dkv, qp, kp, kr, c, a, b, qgv, kgv):
        dqp, dqg = _rmsnorm_bwd(_rope_t(dqv * MLA_SCALE, c, a, b), qp, qgv, MLA_QK)
        dkp, dkg = _rmsnorm_bwd(_rope_t(dkv, c, a, b), kp + kr, kgv, MLA_QK)
        lane = lax.broadcasted_iota(jnp.int32, (1, LANE), 1)
        return dqp, dkp, jnp.where((lane >= MLA_NOPE) & (lane < MLA_QK), dkp, 0.0), dqg, dkg

    hcol = lambda h: h
    rows = [(dq, LANE, hcol), (dk_t, LANE, hcol, True), (q_pre, LANE, hcol), (k_pre, LANE, hcol),
            (P, LANE, lambda h: KR_COL), *tabs]
    w = MLA_HEADS * LANE
    return _rowwise(fn, rows, [qg, kg], [(w, BF16), (w, BF16)], [(1, LANE), (1, LANE)], [LANE], heads=MLA_HEADS,
                    tm=HEAD_ROWS, name=name)


def _head_norm(x, g, *, heads, width, colfn=None, scale=1.0, name):
    return _rowwise(lambda xv, gv: _rmsnorm_fwd(xv, gv) * scale, [(x, width, colfn or (lambda h: h))],
                    [g.reshape(1, width)], [(heads * width, BF16)], heads=heads, tm=HEAD_ROWS, name=name)[0]


def _head_norm_bwd(dy, x, g, *, heads, width, colfn=None, scale=1.0, out_dtype, name):
    return _rowwise(lambda dv_, xv, gv: _rmsnorm_bwd(dv_ * scale, xv, gv),
                    [(dy, width, lambda h: h), (x, width, colfn or (lambda h: h))],
                    [g.reshape(1, width)], [(heads * width, out_dtype)], [(1, width)], heads=heads, tm=HEAD_ROWS,
                    name=name)


def _loss_grad(y, tgt, *, name):
    D = y.shape[1]

    def fn(yv, tv):
        d = yv - tv
        return d * (1.0 / D), jnp.sum(d * d, axis=0, keepdims=True) * (0.5 / D)

    dy, part = _rowwise(fn, [y, tgt], [], [(D, F32)], [(1, D)], name=name)
    return jnp.sum(part), dy


def _adamw(w, g, m, v, *, name):
    shape = w.shape
    two_d = (-1, shape[-1])

    def fn(wv, gv, mv, vv):
        m2 = ADAM_B1 * mv + (1.0 - ADAM_B1) * gv
        v2 = ADAM_B2 * vv + (1.0 - ADAM_B2) * (gv * gv)
        m_hat = m2 / (1.0 - ADAM_B1 ** ADAM_STEP)
        v_hat = v2 / (1.0 - ADAM_B2 ** ADAM_STEP)
        return -ADAM_LR * (m_hat / (jnp.sqrt(v_hat) + ADAM_EPS) + ADAM_WD * wv), m2, v2

    outs = _rowwise(fn, [t.reshape(two_d) for t in (w, g, m, v)], [], [(shape[-1], F32)] * 3, name=name)
    return [o.reshape(shape) for o in outs]


def _pad_cols(w, heads, hd):
    k = w.shape[0]
    return jnp.pad(w.reshape(k, heads, hd), ((0, 0), (0, 0), (0, LANE - hd))).reshape(k, heads * LANE)


def _unpad_cols(w, heads, hd):
    k = w.shape[0]
    return w.reshape(k, heads, LANE)[:, :, :hd].reshape(k, heads * hd)


def _pad_rows(w, heads, hd):
    n = w.shape[1]
    return jnp.pad(w.reshape(heads, hd, n), ((0, 0), (0, LANE - hd), (0, 0))).reshape(heads * LANE, n)


def _unpad_rows(w, heads, hd):
    n = w.shape[1]
    return w.reshape(heads, LANE, n)[:, :hd, :].reshape(heads * hd, n)


def _ffn_fwd(x, g, wgu, wd, tag):
    h = _norm_rows(x, g, name=tag + "_norm")
    gate, up, act = _mm_swiglu(h, wgu, name=tag + "_gu")
    y = _mm(act, wd, scale=0.5, residual=x, name=tag + "_down")
    return y, (x, h, gate, up, act)


def _ffn_bwd(dy, saved, g, wgu, wd, tag):
    x, h, gate, up, act = saved
    F = wd.shape[0]
    dwd = _mm(act, dy, ta=True, scale=0.5, name=tag + "_dwd")
    dgate, dup = _mm_dswiglu(dy, wd, gate, up, scale=0.5, name=tag + "_dact")
    dh = _mm(dgate, wgu, tb=True, name=tag + "_dh_g")
    dx, dg = _mm(dup, wgu, tb=True, b_off=(0, F), residual=dh, norm_bwd=(x, g, dy), name=tag + "_dh_u")
    dwgu = jnp.concatenate([_mm(h, dgate, ta=True, name=tag + "_dwg"), _mm(h, dup, ta=True, name=tag + "_dwu")], axis=1)
    return dx, dg, dwgu, dwd


def _even_weights(w_in, w_out):
    parts = [w_in[:, :SB_W] * SB_SCALE, w_in[:, SB_W:2 * SB_W], w_in[:, 2 * SB_W:3 * SB_W]]
    wqkv = jnp.concatenate([_pad_cols(p, SB_HEADS, SB_HD) for p in parts], axis=1)
    return wqkv, w_in[:, 3 * SB_W:], _pad_rows(w_out[:SB_W], SB_HEADS, SB_HD), w_out[SB_W:]


def _even_fwd(x, g, wts, ln_g, ln_b, sgu_w, bt, tag):
    wqkv, wz, wo_sb, wo_sg = wts
    h = _norm_rows(x, g, name=tag + "_norm")
    qkv = _mm(h, wqkv, out_dtype=BF16, name=tag + "_qkv")
    z = _mm(h, wz, name=tag + "_z")
    o_sb, tot = _attn_fwd(qkv, qkv, qkv, sb=True, causal=True, heads=SB_HEADS, dq=LANE, dv=LANE, group=FWD_GROUP,
                          kcol=lambda g: SB_HEADS // FWD_GROUP + g, vcol=lambda g: 2 * SB_HEADS // FWD_GROUP + g,
                          name=tag + "_sb")
    u, gl = _gelu_ln(z, ln_g, ln_b, name=tag + "_geluln")
    o_sg = _spatial(gl, u, sgu_w, bt, name=tag + "_sgu")
    y = _mm(o_sb, wo_sb, residual=x, name=tag + "_out_sb")
    y = _mm(o_sg, wo_sg, residual=y, name=tag + "_out_sg")
    return y, (x, h, qkv, z, o_sb, tot, u, gl, o_sg)


def _even_bwd(dy, saved, g, wts, ln_g, sgu_w, bt, tag):
    wqkv, wz, wo_sb, wo_sg = wts
    x, h, qkv, z, o_sb, tot, u, gl, o_sg = saved
    do_sb = _mm(dy, wo_sb, tb=True, name=tag + "_do_sb")
    do_sg = _mm(dy, wo_sg, tb=True, name=tag + "_do_sg")
    dwo = jnp.concatenate([_unpad_rows(_mm(o_sb, dy, ta=True, name=tag + "_dwo_sb"), SB_HEADS, SB_HD),
                           _mm(o_sg, dy, ta=True, name=tag + "_dwo_sg")], axis=0)
    dq, dk_t, dv_t = _attn_bwd(qkv, qkv, qkv, o_sb, do_sb, tot, sb=True, causal=True, heads=SB_HEADS, dq=LANE, dv=LANE,
                               kcol=lambda hh: SB_HEADS + hh, vcol=lambda hh: 2 * SB_HEADS + hh, name=tag + "_sb_bwd")
    du, dgl, dsgu_w, db_t = _spatial_bwd(do_sg, gl, u, sgu_w, bt, name=tag + "_sgu_bwd")
    dz, dln_g, dln_b = _gelu_ln_bwd(z, du, dgl, ln_g, name=tag + "_geluln_bwd")
    dh = _mm(dz, wz, tb=True, name=tag + "_dh_z")
    dh = _mm(dq, wqkv, tb=True, residual=dh, name=tag + "_dh_q")
    dws = [_unpad_cols(_mm(h, dq, ta=True, scale=SB_SCALE, name=tag + "_dw_q"), SB_HEADS, SB_HD)]
    for i, (d_t, nm) in enumerate(((dk_t, "k"), (dv_t, "v")), start=1):
        dh = _mm(d_t, wqkv, ta=True, tb=True, b_off=(0, i * SB_HEADS * LANE), residual=dh,
                 norm_bwd=(x, g, dy) if nm == "v" else None, name=tag + "_dh_" + nm)
        dws.append(_unpad_rows(_mm(d_t, h, name=tag + "_dw_" + nm), SB_HEADS, SB_HD).T)
    dws.append(_mm(h, dz, ta=True, name=tag + "_dw_z"))
    dx, dg = dh
    return dx, dict(mix_norm=dg, sbg_w_in=jnp.concatenate(dws, axis=1), sgu_ln_gain=dln_g, sgu_ln_bias=dln_b,
                    sgu_w=dsgu_w, sgu_b=db_t[:, :SG_GROUPS].T, sbg_w_out=dwo)


def _mla_weights(w_in, w_uq, w_ukv, w_out, q_gain, k_gain):
    d = w_in.shape[0]
    lat = MLA_QL + MLA_KVL
    w_in_ext = jnp.concatenate([w_in[:, :lat], jnp.zeros((d, MLA_NOPE), w_in.dtype), w_in[:, lat:],
                                jnp.zeros((d, LANE - MLA_QK), w_in.dtype)], axis=1)
    kv = w_ukv.reshape(MLA_KVL, MLA_HEADS, MLA_NOPE + MLA_V)
    wk = _pad_cols(kv[:, :, :MLA_NOPE].reshape(MLA_KVL, -1), MLA_HEADS, MLA_NOPE)
    wv = _pad_cols(kv[:, :, MLA_NOPE:].reshape(MLA_KVL, -1), MLA_HEADS, MLA_V)
    pad_gain = lambda gn: jnp.pad(gn.reshape(1, MLA_QK), ((0, 0), (0, LANE - MLA_QK)))
    return (w_in_ext, _pad_cols(w_uq, MLA_HEADS, MLA_QK), wk, wv, _pad_rows(w_out, MLA_HEADS, MLA_V),
            pad_gain(q_gain), pad_gain(k_gain))


def _mla_fwd(x, g, wts, qlg, kvlg, tabs, tag):
    w_in, w_uq, wk, wv, w_out, qg, kg = wts
    h = _norm_rows(x, g, name=tag + "_norm")
    P = _mm(h, w_in, name=tag + "_in")
    cqn, ckvn = _mla_lora(P, qlg, kvlg, name=tag + "_lora")
    q_pre = _mm(cqn, w_uq, name=tag + "_uq")
    k_pre = _mm(ckvn, wk, name=tag + "_uk")
    ones_lane = jnp.tile((jnp.arange(LANE) == MLA_V).astype(F32), MLA_HEADS)[None, :]
    v = _mm(ckvn, wv, out_dtype=BF16, bias=ones_lane, name=tag + "_uv")
    q, k = _mla_qk(q_pre, k_pre, P, tabs, qg, kg, name=tag + "_qk")
    o, lse = _attn_fwd(q, k, v, sb=False, causal=True, heads=MLA_HEADS, dq=LANE, dv=LANE, group=FWD_GROUP,
                       sum_lane=MLA_V, name=tag + "_attn")
    y = _mm(o, w_out, residual=x, name=tag + "_out")
    return y, (x, h, P, cqn, ckvn, q_pre, k_pre, q, k, v, o, lse)


def _mla_bwd(dy, saved, g, wts, qlg, kvlg, tabs, tag):
    w_in, w_uq, wk, wv, w_out, qg, kg = wts
    x, h, P, cqn, ckvn, q_pre, k_pre, q, k, v, o, lse = saved
    do = _mm(dy, w_out, tb=True, name=tag + "_do")
    dw_out = _unpad_rows(_mm(o, dy, ta=True, name=tag + "_dwo"), MLA_HEADS, MLA_V)
    dq, dk_t, dv_t = _attn_bwd(q, k, v, o, do, lse, sb=False, causal=True, heads=MLA_HEADS, dq=LANE, dv=LANE,
                               name=tag + "_attn_bwd")
    dq_pre, dk_pre, dkr, dqg, dkg = _mla_qk_bwd(dq, dk_t, q_pre, k_pre, P, tabs, qg, kg, name=tag + "_qk_bwd")
    dcqn = _mm(dq_pre, w_uq, tb=True, name=tag + "_dcq")
    dckvn = _mm(dk_pre, wk, tb=True, name=tag + "_dckv_k")
    dckvn = _mm(dv_t, wv, ta=True, tb=True, residual=dckvn, name=tag + "_dckv_v")
    dw_uq = _unpad_cols(_mm(cqn, dq_pre, ta=True, name=tag + "_dwuq"), MLA_HEADS, MLA_QK)
    dwk = _unpad_cols(_mm(ckvn, dk_pre, ta=True, name=tag + "_dwk"), MLA_HEADS, MLA_NOPE)
    dwv = _unpad_rows(_mm(dv_t, ckvn, name=tag + "_dwv"), MLA_HEADS, MLA_V).T
    dw_ukv = jnp.concatenate([dwk.reshape(MLA_KVL, MLA_HEADS, MLA_NOPE), dwv.reshape(MLA_KVL, MLA_HEADS, MLA_V)],
                             axis=2).reshape(MLA_KVL, -1)
    dP, dqlg, dkvlg = _mla_lora_bwd(dcqn, dckvn, dkr, P, qlg, kvlg, name=tag + "_lora_bwd")
    dx, dg = _mm(dP, w_in, tb=True, norm_bwd=(x, g, dy), name=tag + "_dh")
    dw_in_ext = _mm(h, dP, ta=True, name=tag + "_dwin")
    lat = MLA_QL + MLA_KVL
    dw_in = jnp.concatenate([dw_in_ext[:, :lat], dw_in_ext[:, lat + MLA_NOPE:lat + MLA_QK]], axis=1)
    return dx, dict(mix_norm=dg, mla_w_in=dw_in, mla_q_lora_gain=dqlg, mla_kv_lora_gain=dkvlg, mla_w_uq=dw_uq,
                    mla_w_ukv=dw_ukv, mla_q_gain=dqg[:, :MLA_QK], mla_k_gain=dkg[:, :MLA_QK], mla_w_out=dw_out)


def _xmem_fwd(x, mem, g, gm, wq, wkv, qg, kg, wo, tag):
    hq = _norm_rows(x, g, name=tag + "_norm")
    hm = _norm_rows(mem, gm, name=tag + "_mnorm")
    qp = _mm(hq, wq, name=tag + "_q")
    kv = _mm(hm, wkv, name=tag + "_kv")
    q = _head_norm(qp, qg, heads=MEM_HEADS, width=MEM_HD, scale=MEM_SCALE, name=tag + "_qn")
    kn = _head_norm(kv, kg, heads=MEM_HEADS, width=MEM_HD, colfn=lambda hh: 2 * hh, name=tag + "_kn")
    kvb = kv.reshape(-1, MEM_HEADS, 2, MEM_HD)[:, :, 1].reshape(-1, MEM_HEADS * MEM_HD).astype(BF16)
    o, lse = _attn_fwd(q, kn, kvb, sb=False, causal=False, heads=MEM_HEADS, dq=MEM_HD, dv=MEM_HD, group=MEM_HEADS,
                       name=tag + "_attn")
    y = _mm(o, wo, residual=x, name=tag + "_out")
    return y, (x, hq, hm, qp, kv, q, kn, kvb, o, lse)


def _xmem_bwd(dy, saved, mem, g, gm, wq, wkv, qg, kg, wo, tag):
    x, hq, hm, qp, kv, q, kn, kvb, o, lse = saved
    m = mem.shape[0]
    do = _mm(dy, wo, tb=True, name=tag + "_do")
    dwo = _mm(o, dy, ta=True, name=tag + "_dwo")
    dq, dk_t, dv_t = _attn_bwd(q, kn, kvb, o, do, lse, sb=False, causal=False, heads=MEM_HEADS, dq=MEM_HD, dv=MEM_HD,
                               name=tag + "_attn_bwd")
    dk, dv = dk_t.T, dv_t.T
    dqp, dqg = _head_norm_bwd(dq, qp, qg, heads=MEM_HEADS, width=MEM_HD, scale=MEM_SCALE, out_dtype=BF16,
                              name=tag + "_qn_bwd")
    dkp, dkg = _head_norm_bwd(dk, kv, kg, heads=MEM_HEADS, width=MEM_HD, colfn=lambda hh: 2 * hh, out_dtype=F32,
                              name=tag + "_kn_bwd")
    dkv = jnp.concatenate([dkp.reshape(m, MEM_HEADS, MEM_HD), dv.reshape(m, MEM_HEADS, MEM_HD)], axis=2).reshape(m, -1)
    dwkv = _mm(hm, dkv, ta=True, name=tag + "_dwkv")
    dhm = _mm(dkv, wkv, tb=True, name=tag + "_dhm")
    _, dgm = _norm_rows_bwd(dhm, mem, gm, None, name=tag + "_dmnorm")
    dwq = _mm(hq, dqp, ta=True, name=tag + "_dwq")
    dx, dg = _mm(dqp, wq, tb=True, norm_bwd=(x, g, dy), name=tag + "_dhq")
    return dx, dict(xmem_norm=dg, xmem_mem_norm=dgm, xmem_wq=dwq, xmem_wkv=dwkv, xmem_q_gain=dqg, xmem_k_gain=dkg,
                    xmem_wo=dwo)


def _local_step(x, mem, positions, tgt, w):
    tabs = _rope_tables(positions)
    even = _even_weights(w["sbg_w_in"][0], w["sbg_w_out"][0])
    mla = _mla_weights(w["mla_w_in"][0], w["mla_w_uq"][0], w["mla_w_ukv"][0], w["mla_w_out"][0], w["mla_q_gain"][0],
                       w["mla_k_gain"][0])
    bt = jnp.repeat(w["sgu_b"][0].T, SG_GD, axis=1)
    saved = []
    for l in range(2):
        t = f"l{l}"
        x, s_pre = _ffn_fwd(x, w["ffn_pre_norm"][l], w["ffn_pre_w_gu"][l], w["ffn_pre_w_down"][l], t + "_pre")
        if l == 0:
            x, s_mix = _even_fwd(x, w["mix_norm"][0], even, w["sgu_ln_gain"][0], w["sgu_ln_bias"][0], w["sgu_w"][0], bt,
                                 t + "_even")
        else:
            x, s_mix = _mla_fwd(x, w["mix_norm"][1], mla, w["mla_q_lora_gain"][0], w["mla_kv_lora_gain"][0], tabs,
                                t + "_mla")
        x, s_xm = _xmem_fwd(x, mem, w["xmem_norm"][l], w["xmem_mem_norm"][l], w["xmem_wq"][l], w["xmem_wkv"][l],
                            w["xmem_q_gain"][l], w["xmem_k_gain"][l], w["xmem_wo"][l], t + "_xm")
        x, s_post = _ffn_fwd(x, w["ffn_post_norm"][l], w["ffn_post_w_gu"][l], w["ffn_post_w_down"][l], t + "_post")
        saved.append((s_pre, s_mix, s_xm, s_post))
    loss, dx = _loss_grad(x, tgt, name="loss")
    grads = {}

    def put(name, l, val):
        grads.setdefault(name, {})[l] = val

    for l in (1, 0):
        t = f"l{l}"
        s_pre, s_mix, s_xm, s_post = saved[l]
        dx, dg, dwgu, dwd = _ffn_bwd(dx, s_post, w["ffn_post_norm"][l], w["ffn_post_w_gu"][l], w["ffn_post_w_down"][l],
                                     t + "_post")
        put("ffn_post_norm", l, dg), put("ffn_post_w_gu", l, dwgu), put("ffn_post_w_down", l, dwd)
        dx, gx = _xmem_bwd(dx, s_xm, mem, w["xmem_norm"][l], w["xmem_mem_norm"][l], w["xmem_wq"][l], w["xmem_wkv"][l],
                           w["xmem_q_gain"][l], w["xmem_k_gain"][l], w["xmem_wo"][l], t + "_xm")
        for k_, v_ in gx.items():
            put(k_, l, v_)
        if l == 0:
            dx, gm = _even_bwd(dx, s_mix, w["mix_norm"][0], even, w["sgu_ln_gain"][0], w["sgu_w"][0], bt, t + "_even")
        else:
            dx, gm = _mla_bwd(dx, s_mix, w["mix_norm"][1], mla, w["mla_q_lora_gain"][0], w["mla_kv_lora_gain"][0], tabs,
                              t + "_mla")
        for k_, v_ in gm.items():
            put(k_, l if k_ == "mix_norm" else 0, v_)
        dx, dg, dwgu, dwd = _ffn_bwd(dx, s_pre, w["ffn_pre_norm"][l], w["ffn_pre_w_gu"][l], w["ffn_pre_w_down"][l],
                                     t + "_pre")
        put("ffn_pre_norm", l, dg), put("ffn_pre_w_gu", l, dwgu), put("ffn_pre_w_down", l, dwd)
    return loss, dx, {k_: [v_[l] for l in sorted(v_)] for k_, v_ in grads.items()}


N_CHIPS = 4
PACK_COLS = 1024
PACK_ROW_MULTIPLE = 512


def _place():
    x, y, c = lax.axis_index("x"), lax.axis_index("y"), lax.axis_index("c")
    return x, y, c, [(1 - x, y), (x, 1 - y), (1 - x, 1 - y)]


def _hops(x, y, c):
    return ((x + 1 - c) % 2, (y + c) % 2), ((x + c) % 2, (y + 1 - c) % 2), (1 - x, 1 - y)


def _gather_chips(shard):
    R, C = shard.shape
    Rh = R // 2

    def body(x_ref, out_ref, send_sems, recv_sems):
        x, y, c = lax.axis_index("x"), lax.axis_index("y"), lax.axis_index("c")
        n1, n2, nd = _hops(x, y, c)
        me, q1, q2, qd = 2 * x + y, 2 * n1[0] + n1[1], 2 * n2[0] + n2[1], 2 * nd[0] + nd[1]

        def half(chip, core):
            return out_ref.at[chip, pl.ds(core * Rh, Rh), :]

        def copy(k, chip, core, to, src=None):
            return pltpu.make_async_remote_copy(src_ref=half(chip, core) if src is None else src, dst_ref=half(chip, core),
                                                send_sem=send_sems.at[k], recv_sem=recv_sems.at[k], device_id=to,
                                                device_id_type=MESH)

        own = x_ref.at[pl.ds(c * Rh, Rh), :]
        sibling = (x, y, 1 - c)
        sends = [copy(0, me, c, (*n1, c), src=own), copy(1, me, c, (*n2, c), src=own)]
        sends[0].start()
        sends[1].start()
        copy(0, q1, c, sibling).wait_recv()
        sends += [copy(2, q1, c, (*n2, c)), copy(3, q1, c, sibling)]
        sends[2].start()
        sends[3].start()
        copy(1, q2, c, sibling).wait_recv()
        sends.append(copy(4, q2, c, sibling))
        sends[4].start()
        copy(2, qd, c, sibling).wait_recv()
        sends.append(copy(5, qd, c, sibling))
        sends[5].start()
        copy(3, q2, 1 - c, sibling).wait_recv()
        copy(4, q1, 1 - c, sibling).wait_recv()
        copy(5, qd, 1 - c, sibling).wait_recv()
        for cp in sends:
            cp.wait_send()

    others = pl.pallas_call(
        body, name="gather_weights", out_shape=jax.ShapeDtypeStruct((N_CHIPS, R, C), shard.dtype),
        in_specs=[ANY], out_specs=ANY,
        scratch_shapes=[pltpu.SemaphoreType.DMA((6,)), pltpu.SemaphoreType.DMA((6,))])(shard)
    me = 2 * lax.axis_index("x") + lax.axis_index("y")
    return lax.dynamic_update_slice(others, shard[None], (me, 0, 0))


def _gather_devices(block):
    M, N = block.shape

    def body(x_ref, out_ref, send_sems, recv_sems, local_sem):
        x, y, c, chips = _place()
        me, sibling = (x, y, c), (x, y, 1 - c)

        def rows(px, py, pc):
            return out_ref.at[pl.ds((4 * px + 2 * py + pc) * M, M), :]

        def copy(k, blk, to, src=None):
            return pltpu.make_async_remote_copy(src_ref=rows(*blk) if src is None else src, dst_ref=rows(*blk),
                                                send_sem=send_sems.at[k], recv_sem=recv_sems.at[k], device_id=to,
                                                device_id_type=MESH)

        mine = pltpu.make_async_copy(x_ref, rows(*me), local_sem)
        mine.start()
        first = [copy(0, me, sibling, src=x_ref)]
        first += [copy(1 + j, me, (*chip, c), src=x_ref) for j, chip in enumerate(chips)]
        for cp in first:
            cp.start()
        passed = [copy(4 + j, (*chip, c), sibling) for j, chip in enumerate(chips)]
        for j, chip in enumerate(chips):
            copy(1 + j, (*chip, c), me).wait_recv()
            passed[j].start()
        copy(0, sibling, me).wait_recv()
        for j, chip in enumerate(chips):
            copy(4 + j, (*chip, 1 - c), me).wait_recv()
        for cp in first + passed:
            cp.wait_send()
        mine.wait()

    vmem = pl.BlockSpec(memory_space=pltpu.VMEM)
    return pl.pallas_call(
        body, name=f"gather_devices_{M}", out_shape=jax.ShapeDtypeStruct((8 * M, N), block.dtype),
        in_specs=[vmem], out_specs=vmem,
        scratch_shapes=[pltpu.SemaphoreType.DMA((7,)), pltpu.SemaphoreType.DMA((7,)), pltpu.SemaphoreType.DMA],
        compiler_params=pltpu.CompilerParams(vmem_limit_bytes=VMEM_LIMIT))(block)


def _swap_halves(g):
    n, R, C = g.shape
    Rh = R // 2

    def body(g_ref, a_ref, send_sem, recv_sem):
        x, y, c, _ = _place()
        cp = pltpu.make_async_remote_copy(src_ref=g_ref.at[:, pl.ds((1 - c) * Rh, Rh), :], dst_ref=a_ref,
                                          send_sem=send_sem, recv_sem=recv_sem, device_id=(x, y, 1 - c),
                                          device_id_type=MESH)
        cp.start()
        cp.wait()

    return pl.pallas_call(body, name="grad_swap_halves", out_shape=jax.ShapeDtypeStruct((n, Rh, C), g.dtype),
                          in_specs=[ANY], out_specs=ANY,
                          scratch_shapes=[pltpu.SemaphoreType.DMA, pltpu.SemaphoreType.DMA])(g)


def _add_picked(a, b, picks, *, a_row_half=None, out_dtype, name):
    n_out = picks.shape[0]
    _, rows, C = b.shape
    tr = _row_tile(rows, 512)
    nt = rows // tr
    half = jnp.zeros((1,), jnp.int32) if a_row_half is None else a_row_half

    def body(pick_ref, half_ref, a_ref, b_ref, o_ref):
        o_ref[...] = (a_ref[...].astype(F32) + b_ref[...].astype(F32)).astype(o_ref.dtype)

    spec = pltpu.PrefetchScalarGridSpec(
        num_scalar_prefetch=2, grid=(n_out, nt),
        in_specs=[pl.BlockSpec((1, tr, C), lambda j, i, pick, hf: (pick[j], hf[0] * nt + i, 0)),
                  pl.BlockSpec((1, tr, C), lambda j, i, pick, hf: (pick[j], i, 0))],
        out_specs=pl.BlockSpec((1, tr, C), lambda j, i, pick, hf: (j, i, 0)))
    return pl.pallas_call(body, name=name, grid_spec=spec, out_shape=jax.ShapeDtypeStruct((n_out, rows, C), out_dtype),
                          compiler_params=_params(2))(picks.astype(jnp.int32), half.astype(jnp.int32), a, b)


def _hop_exchange(src, hop, *, name):
    def body(s_ref, d_ref, send_sem, recv_sem):
        x, y, c = lax.axis_index("x"), lax.axis_index("y"), lax.axis_index("c")
        cp = pltpu.make_async_remote_copy(src_ref=s_ref, dst_ref=d_ref, send_sem=send_sem, recv_sem=recv_sem,
                                          device_id=(*_hops(x, y, c)[hop], c), device_id_type=MESH)
        cp.start()
        cp.wait()

    return pl.pallas_call(body, name=name, out_shape=jax.ShapeDtypeStruct(src.shape, src.dtype), in_specs=[ANY],
                          out_specs=ANY, scratch_shapes=[pltpu.SemaphoreType.DMA, pltpu.SemaphoreType.DMA])(src)


def _reduce_over_chips(g):
    x, y, c = lax.axis_index("x"), lax.axis_index("y"), lax.axis_index("c")
    n1, n2, _ = _hops(x, y, c)
    chip = lambda p: 2 * p[0] + p[1]
    near = jnp.stack([chip((x, y)), chip(n2)])
    far = jnp.stack([chip(n1), chip((1 - x, 1 - y))])
    half = c.reshape(1)
    sib = _swap_halves(g)
    kept = _add_picked(g, sib, near, a_row_half=half, out_dtype=F32, name="grad_add_near")
    sent = _add_picked(g, sib, far, a_row_half=half, out_dtype=BF16, name="grad_add_far")
    got = _hop_exchange(sent, 0, name="grad_hop_first")
    mine = _add_picked(kept, got, jnp.zeros((1,), jnp.int32), out_dtype=F32, name="grad_add_mine")
    theirs = _add_picked(kept, got, jnp.ones((1,), jnp.int32), out_dtype=BF16, name="grad_add_theirs")
    got = _hop_exchange(theirs, 1, name="grad_hop_second")
    total = _add_picked(mine, got, jnp.zeros((1,), jnp.int32), out_dtype=F32, name="grad_add_total")
    return _join_halves(total[0])


def _sum_slots(b, *, name):
    n, R, C = b.shape
    tr = _row_tile(R, 512)

    def body(b_ref, o_ref):
        acc = b_ref[0]
        for q in range(1, n):
            acc = acc + b_ref[q]
        o_ref[...] = acc

    return pl.pallas_call(body, name=name, grid=(R // tr,), in_specs=[pl.BlockSpec((n, tr, C), lambda i: (0, i, 0))],
                          out_specs=pl.BlockSpec((tr, C), lambda i: (i, 0)), out_shape=jax.ShapeDtypeStruct((R, C), F32),
                          compiler_params=_params(1))(b)


def _join_halves(r):
    Rh, C = r.shape

    def body(r_ref, o_ref, send_sem, recv_sem):
        x, y, c, _ = _place()
        own, other = o_ref.at[pl.ds(c * Rh, Rh), :], o_ref.at[pl.ds((1 - c) * Rh, Rh), :]
        cp = pltpu.make_async_remote_copy(src_ref=r_ref, dst_ref=own, send_sem=send_sem, recv_sem=recv_sem,
                                          device_id=(x, y, 1 - c), device_id_type=MESH)
        cp.start()
        pltpu.make_async_remote_copy(src_ref=r_ref, dst_ref=other, send_sem=send_sem, recv_sem=recv_sem,
                                     device_id=(x, y, 1 - c), device_id_type=MESH).wait_recv()
        cp.wait_send()

    theirs = pl.pallas_call(
        body, name="grad_join_halves", out_shape=jax.ShapeDtypeStruct((2 * Rh, C), r.dtype), in_specs=[ANY], out_specs=ANY,
        scratch_shapes=[pltpu.SemaphoreType.DMA, pltpu.SemaphoreType.DMA])(r)
    return lax.dynamic_update_slice(theirs, r, (lax.axis_index("c") * Rh, 0))


def _size(shape):
    size = 1
    for d in shape:
        size *= d
    return size


def _pack(pieces, cols, row_multiple, dtype):
    if any(p.size % cols for p in pieces):
        flat = jnp.concatenate([p.reshape(-1).astype(dtype) for p in pieces])
        pieces = [jnp.pad(flat, (0, -flat.shape[0] % cols))]
    rows = [p.reshape(-1, cols).astype(dtype) for p in pieces]
    pad = -sum(r.shape[0] for r in rows) % row_multiple
    return jnp.concatenate(rows + ([jnp.zeros((pad, cols), dtype)] if pad else []), axis=0)


def _unpack(buf, shapes):
    cols = buf.shape[1]
    if any(_size(s) % cols for s in shapes):
        flat, out, at = buf.reshape(-1), [], 0
        for shp in shapes:
            out.append(flat[at:at + _size(shp)].reshape(shp))
            at += _size(shp)
        return out
    out, at = [], 0
    for shp in shapes:
        out.append(buf[at:at + _size(shp) // cols].reshape(shp))
        at += _size(shp) // cols
    return out


SHARDED = (("ffn_pre_w_gu", 2), ("ffn_pre_w_down", 1), ("sbg_w_in", 2), ("sbg_w_out", 1), ("mla_w_in", 1),
           ("mla_w_uq", 2), ("mla_w_ukv", 2), ("mla_w_out", 1), ("xmem_wq", 1), ("xmem_wkv", 2), ("xmem_wo", 1),
           ("ffn_post_w_gu", 2), ("ffn_post_w_down", 1))
LORA_GAINS = ("mla_q_lora_gain", "mla_kv_lora_gain")
REPLICATED = ("ffn_pre_norm", "mix_norm", "sgu_ln_gain", "sgu_ln_bias", "sgu_w", "sgu_b", "mla_q_gain", "mla_k_gain",
              "xmem_norm", "xmem_mem_norm", "xmem_q_gain", "xmem_k_gain", "ffn_post_norm")
WEIGHTS = ("ffn_pre_norm", "ffn_pre_w_gu", "ffn_pre_w_down", "mix_norm", "sbg_w_in", "sgu_ln_gain", "sgu_ln_bias", "sgu_w",
           "sgu_b", "sbg_w_out", "mla_w_in", "mla_q_lora_gain", "mla_kv_lora_gain", "mla_w_uq", "mla_w_ukv", "mla_q_gain",
           "mla_k_gain", "mla_w_out", "xmem_norm", "xmem_mem_norm", "xmem_wq", "xmem_wkv", "xmem_q_gain", "xmem_k_gain",
           "xmem_wo", "ffn_post_norm", "ffn_post_w_gu", "ffn_post_w_down")
INPUTS = ("x", "mem", "positions") + WEIGHTS + ("loss_target",) + tuple("m_" + n for n in WEIGHTS) + tuple(
    "v_" + n for n in WEIGHTS)


def _step(a):
    x, y, c, _ = _place()
    chip = 2 * x + y
    shard_shapes = [a[n].shape for n, _ in SHARDED]

    gathered = _gather_chips(_pack([a[n] for n, _ in SHARDED], PACK_COLS, PACK_ROW_MULTIPLE, BF16))
    w, at = {}, 0
    for (n, ax), shp in zip(SHARDED, shard_shapes):
        rows = _size(shp) // PACK_COLS
        per_chip = gathered[:, at:at + rows].reshape((N_CHIPS,) + shp)
        at += rows
        w[n] = jnp.moveaxis(per_chip, 0, ax).reshape(shp[:ax] + (N_CHIPS * shp[ax],) + shp[ax + 1:])
    gains = jnp.zeros((8, LANE), F32)
    for r, n in enumerate(LORA_GAINS):
        gains = gains.at[r, :a[n].shape[1]].set(a[n][0])
    gains = _gather_devices(gains)
    for r, n in enumerate(LORA_GAINS):
        w[n] = jnp.concatenate([gains[16 * q + r, :a[n].shape[1]] for q in range(N_CHIPS)])[None, :]
    for n in REPLICATED:
        w[n] = a[n]

    loss, dx, grads = _local_step(a["x"][0], a["mem"][0], a["positions"][0], a["loss_target"][0], w)
    loss = lax.psum(loss, ("x", "y", "c"))
    small_names = REPLICATED + LORA_GAINS
    full = {n: jnp.stack(grads[n]).reshape(w[n].shape) for n in small_names}

    def cut(n, ax, q):
        size = w[n].shape[ax] // N_CHIPS
        return [lax.slice_in_dim(gl, q * size, (q + 1) * size, axis=ax - 1) for gl in grads[n]]

    g = jnp.stack([_pack([p for n, ax in SHARDED for p in cut(n, ax, q)], PACK_COLS, PACK_ROW_MULTIPLE, F32)
                   for q in range(N_CHIPS)])
    reduced = _reduce_over_chips(g)
    gw = dict(zip([n for n, _ in SHARDED], _unpack(reduced, shard_shapes)))

    small = _pack([full[n] for n in small_names], LANE, 256, F32)
    rows = small.shape[0]
    summed = _sum_slots(_gather_devices(small).reshape(8, rows, LANE), name="grad_sum_devices")
    for n, val in zip(small_names, _unpack(summed, [full[n].shape for n in small_names])):
        if n in LORA_GAINS:
            size = a[n].shape[1]
            val = lax.dynamic_slice_in_dim(val, chip * size, size, axis=1)
        gw[n] = val

    upd = {n: _adamw(a[n], gw[n], a["m_" + n], a["v_" + n], name="adamw_" + n) for n in WEIGHTS}
    return (loss, dx[None], *[gw[n] for n in WEIGHTS], *[upd[n][0] for n in WEIGHTS], *[upd[n][1] for n in WEIGHTS],
            *[upd[n][2] for n in WEIGHTS])


def kernel(x, mem, positions, ffn_pre_norm, ffn_pre_w_gu, ffn_pre_w_down, mix_norm, sbg_w_in, sgu_ln_gain,
           sgu_ln_bias, sgu_w, sgu_b, sbg_w_out, mla_w_in, mla_q_lora_gain, mla_kv_lora_gain, mla_w_uq, mla_w_ukv,
           mla_q_gain, mla_k_gain, mla_w_out, xmem_norm, xmem_mem_norm, xmem_wq, xmem_wkv, xmem_q_gain, xmem_k_gain,
           xmem_wo, ffn_post_norm, ffn_post_w_gu, ffn_post_w_down, loss_target, m_ffn_pre_norm, m_ffn_pre_w_gu,
           m_ffn_pre_w_down, m_mix_norm, m_sbg_w_in, m_sgu_ln_gain, m_sgu_ln_bias, m_sgu_w, m_sgu_b, m_sbg_w_out,
           m_mla_w_in, m_mla_q_lora_gain, m_mla_kv_lora_gain, m_mla_w_uq, m_mla_w_ukv, m_mla_q_gain, m_mla_k_gain,
           m_mla_w_out, m_xmem_norm, m_xmem_mem_norm, m_xmem_wq, m_xmem_wkv, m_xmem_q_gain, m_xmem_k_gain,
           m_xmem_wo, m_ffn_post_norm, m_ffn_post_w_gu, m_ffn_post_w_down, v_ffn_pre_norm, v_ffn_pre_w_gu,
           v_ffn_pre_w_down, v_mix_norm, v_sbg_w_in, v_sgu_ln_gain, v_sgu_ln_bias, v_sgu_w, v_sgu_b, v_sbg_w_out,
           v_mla_w_in, v_mla_q_lora_gain, v_mla_kv_lora_gain, v_mla_w_uq, v_mla_w_ukv, v_mla_q_gain, v_mla_k_gain,
           v_mla_w_out, v_xmem_norm, v_xmem_mem_norm, v_xmem_wq, v_xmem_wkv, v_xmem_q_gain, v_xmem_k_gain,
           v_xmem_wo, v_ffn_post_norm, v_ffn_post_w_gu, v_ffn_post_w_down):
    given = locals()
    return _step({n: given[n] for n in INPUTS})
```

```python
import jax
import jax.numpy as jnp
from jax import lax
from jax.experimental import pallas as pl
from jax.experimental.pallas import tpu as pltpu

F32, BF16 = jnp.float32, jnp.bfloat16
LANE = 128
VMEM_LIMIT = 56 * 1024 * 1024
EPS = 1e-6
SB_HEADS, SB_HD = 8, 64
SG_GROUPS, SG_GD, SG_CHUNK = 8, 64, 128
SB_W, SG_W = SB_HEADS * SB_HD, SG_GROUPS * SG_GD
MLA_HEADS, MLA_NOPE, MLA_ROPE, MLA_V = 16, 64, 32, 64
MLA_QK = MLA_NOPE + MLA_ROPE
MLA_QL, MLA_KVL = 512, 256
ROPE_THETA = 10000.0
MEM_HEADS, MEM_HD = 4, 256
SB_SCALE, MLA_SCALE, MEM_SCALE = SB_HD ** -0.5, MLA_QK ** -0.5, MEM_HD ** -0.5
ADAM_LR, ADAM_B1, ADAM_B2, ADAM_EPS, ADAM_WD, ADAM_STEP = 0.001, 0.9, 0.999, 1e-08, 0.01, 10
MESH = pl.DeviceIdType.MESH
ANY = pl.BlockSpec(memory_space=pl.ANY)


def _params(n_axes):
    return pltpu.CompilerParams(dimension_semantics=("arbitrary",) * n_axes, vmem_limit_bytes=VMEM_LIMIT)


MM_TILE_CAP = 1408
MM_VMEM_BUDGET = 40 * 1024 * 1024


def _tile(dim, cap):
    if dim <= cap:
        return dim
    best = max(t for t in range(LANE, cap + 1, LANE) if dim % t == 0)
    return best


def _k_scratch(count, tile, nk):
    return [pltpu.VMEM(tile, F32)] * count if nk > 1 else []


def _over_k_steps(prods, acc_refs, nk, finish):
    if nk == 1:
        finish(prods)
        return
    kk = pl.program_id(2)

    @pl.when(kk == 0)
    def _():
        for ref, p in zip(acc_refs, prods):
            ref[...] = p

    @pl.when(kk > 0)
    def _():
        for ref, p in zip(acc_refs, prods):
            ref[...] += p

    @pl.when(kk == nk - 1)
    def _():
        finish([ref[...] for ref in acc_refs])


def _mm(a, b, *, ta=False, tb=False, out_dtype=F32, scale=1.0, residual=None, bias=None, norm_bwd=None, a_off=(0, 0),
        b_off=(0, 0), m=None, n=None, k=None, name):
    am, ak = (a.shape[1], a.shape[0]) if ta else a.shape
    bk, bn = (b.shape[1], b.shape[0]) if tb else b.shape
    M, N, K = m or am, n or bn, k or ak
    tm, tn = _tile(M, MM_TILE_CAP if norm_bwd is None else MM_TILE_CAP // 2), _tile(N, MM_TILE_CAP)
    n_full = (residual is not None) + (2 if norm_bwd is not None else 0)
    fixed = tm * tn * (4 + 2 * jnp.dtype(out_dtype).itemsize + 8 * n_full)
    per_k = (tm * (2 * a.dtype.itemsize + 2) + tn * (2 * b.dtype.itemsize + 2))
    tk = _tile(K, max(LANE, (MM_VMEM_BUDGET - fixed) // per_k))
    nm, nn, nk = M // tm, N // tn, K // tk
    assert norm_bwd is None or nn == 1
    a_off = (a_off[0] // (tk if ta else tm), a_off[1] // (tm if ta else tk))
    b_off = (b_off[0] // (tn if tb else tk), b_off[1] // (tk if tb else tn))
    dims = (((0 if ta else 1,), (1 if tb else 0,)), ((), ()))
    n_out = 1 if norm_bwd is None else 2

    def body(*refs):
        a_ref, b_ref = refs[0], refs[1]
        n_in = len(ins)
        o_ref, extras, acc_refs = refs[n_in], refs[2:n_in], refs[n_in + n_out:]
        first_rows = pl.program_id(0) == 0

        def finish(total):
            out = total * scale
            for extra in (extras if norm_bwd is None else extras[:-3]):
                out = out + extra[...].astype(F32)
            if norm_bwd is not None:
                x_ref, g_ref, dres_ref = extras[-3:]
                dg_ref = refs[n_in + 1]
                dx, dg = _rmsnorm_bwd(out, x_ref[...], g_ref[...])
                out = dx + dres_ref[...]

                @pl.when(first_rows)
                def _():
                    dg_ref[...] = jnp.zeros_like(dg_ref)

                dg_ref[...] += dg
            o_ref[...] = out.astype(o_ref.dtype)

        prod = lax.dot_general(a_ref[...].astype(BF16), b_ref[...].astype(BF16), dims, preferred_element_type=F32)
        _over_k_steps([prod], acc_refs, nk, lambda totals: finish(totals[0]))

    (ao0, ao1), (bo0, bo1) = a_off, b_off
    a_spec = (pl.BlockSpec((tk, tm), lambda i, j, kk: (kk + ao0, i + ao1)) if ta
              else pl.BlockSpec((tm, tk), lambda i, j, kk: (i + ao0, kk + ao1)))
    b_spec = (pl.BlockSpec((tn, tk), lambda i, j, kk: (j + bo0, kk + bo1)) if tb
              else pl.BlockSpec((tk, tn), lambda i, j, kk: (kk + bo0, j + bo1)))
    o_spec = pl.BlockSpec((tm, tn), lambda i, j, kk: (i, j))
    ins, in_specs = [a, b], [a_spec, b_spec]
    if residual is not None:
        ins.append(residual)
        in_specs.append(o_spec)
    if bias is not None:
        ins.append(bias)
        in_specs.append(pl.BlockSpec((1, tn), lambda i, j, kk: (0, j)))
    out_specs, out_shape = o_spec, jax.ShapeDtypeStruct((M, N), out_dtype)
    if norm_bwd is not None:
        x, gain, dres = norm_bwd
        row = pl.BlockSpec((1, tn), lambda i, j, kk: (0, 0))
        ins += [x, gain.reshape(1, N), dres]
        in_specs += [o_spec, row, o_spec]
        out_specs, out_shape = [o_spec, row], [out_shape, jax.ShapeDtypeStruct((1, N), F32)]
    return pl.pallas_call(
        body, name=name, grid=(nm, nn, nk), in_specs=in_specs, out_specs=out_specs, out_shape=out_shape,
        scratch_shapes=_k_scratch(1, (tm, tn), nk), compiler_params=_params(3))(*ins)


def _mm_swiglu(h, wgu, *, name):
    M, K = h.shape
    F = wgu.shape[1] // 2
    tm, tn, tk = _tile(M, 512), _tile(F, MM_TILE_CAP), _tile(K, 1024)
    nm, nf, nk = M // tm, F // tn, K // tk

    def body(h_ref, wg_ref, wu_ref, g_ref, u_ref, a_ref, *acc_refs):
        def finish(totals):
            g, u = totals
            g_ref[...] = g.astype(BF16)
            u_ref[...] = u.astype(BF16)
            a_ref[...] = (g * jax.nn.sigmoid(g) * u).astype(BF16)

        hb = h_ref[...]
        _over_k_steps([jnp.dot(hb, wg_ref[...], preferred_element_type=F32),
                       jnp.dot(hb, wu_ref[...], preferred_element_type=F32)], acc_refs, nk, finish)

    o_spec = pl.BlockSpec((tm, tn), lambda j, i, kk: (i, j))
    shp = jax.ShapeDtypeStruct((M, F), BF16)
    return pl.pallas_call(
        body, name=name, grid=(nf, nm, nk),
        in_specs=[pl.BlockSpec((tm, tk), lambda j, i, kk: (i, kk)),
                  pl.BlockSpec((tk, tn), lambda j, i, kk: (kk, j)),
                  pl.BlockSpec((tk, tn), lambda j, i, kk: (kk, j + nf))],
        out_specs=[o_spec, o_spec, o_spec], out_shape=[shp, shp, shp],
        scratch_shapes=_k_scratch(2, (tm, tn), nk), compiler_params=_params(3))(h, wgu, wgu)


def _mm_dswiglu(dy, wd, gate, up, *, scale, name):
    M, K = dy.shape
    F = wd.shape[0]
    tm, tn, tk = _tile(M, 512), _tile(F, MM_TILE_CAP), _tile(K, 1024)
    nm, nf, nk = M // tm, F // tn, K // tk

    def body(dy_ref, wd_ref, g_ref, u_ref, dg_ref, du_ref, *acc_refs):
        def finish(totals):
            da = totals[0] * scale
            g, u = g_ref[...].astype(F32), u_ref[...].astype(F32)
            sg = jax.nn.sigmoid(g)
            du_ref[...] = (da * g * sg).astype(BF16)
            dg_ref[...] = (da * u * sg * (1.0 + g * (1.0 - sg))).astype(BF16)

        _over_k_steps([_nt(dy_ref[...].astype(BF16), wd_ref[...])], acc_refs, nk, finish)

    o_spec = pl.BlockSpec((tm, tn), lambda j, i, kk: (i, j))
    shp = jax.ShapeDtypeStruct((M, F), BF16)
    return pl.pallas_call(
        body, name=name, grid=(nf, nm, nk),
        in_specs=[pl.BlockSpec((tm, tk), lambda j, i, kk: (i, kk)),
                  pl.BlockSpec((tn, tk), lambda j, i, kk: (j, kk)), o_spec, o_spec],
        out_specs=[o_spec, o_spec], out_shape=[shp, shp],
        scratch_shapes=_k_scratch(1, (tm, tn), nk), compiler_params=_params(3))(dy, wd, gate, up)


HEAD_ROWS = 1024


def _row_tile(rows, cap):
    t = cap
    while t >= 8:
        if rows % t == 0:
            return t
        t //= 2
    return rows


def _rowwise(fn, rows, consts, outs, sums=(), hsums=(), *, heads=None, tm=256, name):
    rows = [r if isinstance(r, tuple) else (r, r.shape[1], None) for r in rows]
    rows = [r if len(r) == 4 else (*r, False) for r in rows]
    S = rows[0][0].shape[0]
    tm = _row_tile(S, tm)
    nh = heads or 1
    n_r, n_c, n_o, n_h, n_s = len(rows), len(consts), len(outs), len(hsums), len(sums)

    def body(*refs):
        r = [x[...].T if row[3] else x[...] for x, row in zip(refs, rows)]
        c = [x[...] for x in refs[n_r:n_r + n_c]]
        o_refs = refs[n_r + n_c:n_r + n_c + n_o]
        h_refs = refs[n_r + n_c + n_o:n_r + n_c + n_o + n_h]
        s_refs = refs[n_r + n_c + n_o + n_h:]
        res = fn(*r, *c)
        res = res if isinstance(res, (tuple, list)) else (res,)
        for ref, val in zip(o_refs, res[:n_o]):
            ref[...] = val.astype(ref.dtype)
        if n_h:
            @pl.when(pl.program_id(1) == 0)
            def _():
                for ref in h_refs:
                    ref[...] = jnp.zeros_like(ref)
            for ref, val in zip(h_refs, res[n_o:n_o + n_h]):
                ref[...] += val
        if n_s:
            @pl.when((pl.program_id(0) == 0) & (pl.program_id(1) == 0))
            def _():
                for ref in s_refs:
                    ref[...] = jnp.zeros_like(ref)
            for ref, val in zip(s_refs, res[n_o + n_h:]):
                ref[...] += val

    def col(colfn):
        return (lambda i, h: (i, 0)) if colfn is None else (lambda i, h: (i, colfn(h)))

    in_specs = [pl.BlockSpec((w, tm), lambda i, h, cf=cf: (cf(h), i)) if flipped else pl.BlockSpec((tm, w), col(cf))
                for _, w, cf, flipped in rows]
    in_specs += [pl.BlockSpec(a.shape, lambda i, h, nd=a.ndim: (0,) * nd) for a in consts]
    out_specs = [pl.BlockSpec((tm, w // nh), (lambda i, h: (i, h)) if heads else (lambda i, h: (i, 0))) for w, _ in outs]
    out_specs += [pl.BlockSpec((tm, w), lambda i, h: (i, 0)) for w in hsums]
    out_specs += [pl.BlockSpec(sh, lambda i, h, nd=len(sh): (0,) * nd) for sh in sums]
    out_shape = [jax.ShapeDtypeStruct((S, w), dt) for w, dt in outs]
    out_shape += [jax.ShapeDtypeStruct((S, w), F32) for w in hsums]
    out_shape += [jax.ShapeDtypeStruct(sh, F32) for sh in sums]
    return pl.pallas_call(body, name=name, grid=(S // tm, nh), in_specs=in_specs, out_specs=out_specs,
                          out_shape=out_shape, compiler_params=_params(2))(*[row[0] for row in rows], *consts)


def _rms(x, width=None):
    width = width or x.shape[-1]
    return lax.rsqrt(jnp.sum(x * x, axis=-1, keepdims=True) * (1.0 / width) + EPS)


def _rmsnorm_fwd(x, g, width=None):
    return x * _rms(x, width) * g


def _rmsnorm_bwd(dy, x, g, width=None):
    width = width or x.shape[-1]
    r = _rms(x, width)
    xn = x * r
    dxn = dy * g
    dx = r * (dxn - xn * (jnp.sum(dxn * xn, axis=-1, keepdims=True) * (1.0 / width)))
    return dx, jnp.sum(dy * xn, axis=0, keepdims=True)


def _norm_rows(x, g, *, name, out_dtype=BF16):
    D = x.shape[1]
    return _rowwise(lambda xv, gv: _rmsnorm_fwd(xv.astype(F32), gv), [x], [g.reshape(1, D)], [(D, out_dtype)],
                    name=name)[0]


def _norm_rows_bwd(dh, x, g, dres, *, name):
    D = x.shape[1]

    def fn(dhv, xv, *rest):
        dx, dg = _rmsnorm_bwd(dhv.astype(F32), xv, rest[-1])
        return (dx + rest[0] if dres is not None else dx), dg

    rows = [dh, x] + ([dres] if dres is not None else [])
    return _rowwise(fn, rows, [g.reshape(1, D)], [(D, F32)], [(1, D)], name=name)


def _softplus(z):
    return jnp.where(z > 20.0, z, jnp.log(1.0 + jnp.exp(z)))


def _running_sum(v, u, split=True):
    if not split:
        return jnp.dot(v.astype(BF16), u, preferred_element_type=F32)
    hi = lax.bitcast_convert_type(lax.bitcast_convert_type(v, jnp.uint32) & jnp.uint32(0xFFFF0000), F32)
    return (jnp.dot(hi.astype(BF16), u, preferred_element_type=F32)
            + jnp.dot((v - hi).astype(BF16), u, preferred_element_type=F32))


def _triangle(tk, inclusive_prefix):
    j, s = lax.broadcasted_iota(jnp.int32, (tk, tk), 0), lax.broadcasted_iota(jnp.int32, (tk, tk), 1)
    return ((j <= s) if inclusive_prefix else (j > s)).astype(BF16)


def _nt(a, b):
    return lax.dot_general(a, b, (((1,), (1,)), ((), ())), preferred_element_type=F32)


def _tn(a, b):
    return lax.dot_general(a, b, (((0,), (0,)), ((), ())), preferred_element_type=F32)


ATT_TQ, ATT_TK = 512, 512
SB_SUB = 256
FWD_GROUP = 2


def _attn_fwd(q, k, v, *, sb, causal, heads, dq, dv, group=1, kcol=None, vcol=None, sum_lane=None, side_gather=None,
              name):
    S, Sk = q.shape[0], k.shape[0]
    tq, tk = min(ATT_TQ, S), min(ATT_TK, Sk)
    sub = min(SB_SUB, tk) if sb else tk
    assert tq % sub == 0 or not causal
    kcol = kcol or (lambda h: h)
    vcol = vcol or (lambda h: h)
    members = range(group)
    assert side_gather is None or heads // group >= GATHER_STAGES

    def body(*refs):
        if side_gather is not None:
            n_in = 4 if sb else 3
            stages = _gather_stages(refs[n_in], refs[n_in + 3], *refs[-2:])
            for n, stage in enumerate(stages):
                pl.when((pl.program_id(0) == n) & (pl.program_id(1) == 0))(stage)
            refs = refs[:n_in] + refs[n_in + 1:n_in + 3] + refs[n_in + 4:-2]
        if sb:
            q_ref, k_ref, v_ref, u_ref, o_ref, lse_ref, acc_ref, r_ref = refs
            r_ref[...] = jnp.zeros_like(r_ref)
        else:
            q_ref, k_ref, v_ref, o_ref, lse_ref, acc_ref, m_ref, l_ref = refs
            m_ref[...] = jnp.full_like(m_ref, -1e30)
            l_ref[...] = jnp.zeros_like(l_ref)
        first_row = pl.program_id(1) * tq
        qb = [q_ref[:, hh * dq:(hh + 1) * dq] for hh in members]
        acc_ref[...] = jnp.zeros_like(acc_ref)
        nblk = (first_row + tq) // sub if causal else Sk // sub
        nfull = (first_row + (0 if sb else 1)) // sub if causal else nblk
        n_cut = tq // sub if causal else 0

        def scores(jj):
            off = pl.multiple_of(jj * sub, sub)
            return tuple(_nt(qb[hh], k_ref[pl.ds(off, sub), hh * dq:(hh + 1) * dq]) for hh in members)

        def weigh(jj, scores_now, masked):
            off = pl.multiple_of(jj * sub, sub)
            if masked:
                kpos = off + lax.broadcasted_iota(jnp.int32, (tq, sub), 1)
                qpos = first_row + lax.broadcasted_iota(jnp.int32, (tq, sub), 0)
                valid = (kpos < qpos) if sb else (kpos <= qpos)
            for hh in members:
                vb = v_ref[pl.ds(off, sub), hh * dv:(hh + 1) * dv]
                s = scores_now[hh]
                if sb:
                    sp = _softplus(s)
                    ls = jnp.where(valid, -sp, 0.0) if masked else -sp
                    w = jnp.exp(s - sp + r_ref[hh] + _running_sum(ls, u_ref[...]))
                    if masked:
                        w = jnp.where(valid, w, 0.0)
                    acc_ref[hh] += jnp.dot(w.astype(BF16), vb, preferred_element_type=F32)
                    r_ref[hh] += jnp.sum(ls, axis=1, keepdims=True)
                else:
                    if masked:
                        s = jnp.where(valid, s, -1e30)
                    m_old = m_ref[hh]
                    m_new = jnp.maximum(m_old, jnp.max(s, axis=1, keepdims=True))
                    p = jnp.exp(s - m_new)
                    alpha = jnp.exp(m_old - m_new)
                    if sum_lane is None:
                        l_ref[hh] = alpha * l_ref[hh] + jnp.sum(p, axis=1, keepdims=True)
                    acc_ref[hh] = alpha * acc_ref[hh] + jnp.dot(p.astype(BF16), vb, preferred_element_type=F32)
                    m_ref[hh] = m_new

        if sb:
            s_cur = scores(nblk - 1)
            for cut in range(n_cut):
                s_next = scores(jnp.maximum(nblk - 2 - cut, 0))
                weigh(nblk - 1 - cut, s_cur, True)
                s_cur = s_next

            def step(t, s_now):
                s_next = scores(jnp.maximum(nfull - 2 - t, 0))
                weigh(nfull - 1 - t, s_now, False)
                return s_next

            lax.fori_loop(0, nfull, step, s_cur)
        else:
            n_loop = nfull if causal else nblk - 1

            def step(t, s_now):
                s_next = scores(jnp.minimum(t + 1, nblk - 1))
                weigh(t, s_now, False)
                return s_next

            s_cur = lax.fori_loop(0, n_loop, step, scores(0))
            tail = n_cut if causal else 1
            for last in range(tail):
                s_next = scores(n_loop + last + 1) if last + 1 < tail else None
                weigh(n_loop + last, s_cur, causal)
                s_cur = s_next
        for hh in members:
            cols = slice(hh * dv, (hh + 1) * dv)
            if sb:
                o_ref[:, cols] = acc_ref[hh]
                lse_ref[hh] = r_ref[hh]
            else:
                acc = acc_ref[hh]
                l = l_ref[hh] if sum_lane is None else acc[:, sum_lane:sum_lane + 1]
                o_ref[:, cols] = acc / l
                lse_ref[hh] = m_ref[hh] + jnp.log(l)

    in_specs = [pl.BlockSpec((tq, group * dq), lambda g, i: (i, g)),
                pl.BlockSpec((Sk, group * dq), lambda g, i: (0, kcol(g))),
                pl.BlockSpec((Sk, group * dv), lambda g, i: (0, vcol(g)))]
    ins = [q, k, v]
    scratch = [pltpu.VMEM((group, tq, dv), F32), pltpu.VMEM((group, tq, 1), F32)]
    if sb:
        ins.append(_triangle(sub, inclusive_prefix=False))
        in_specs.append(pl.BlockSpec((sub, sub), lambda g, i: (0, 0)))
    else:
        scratch.append(pltpu.VMEM((group, tq, 1), F32))
    out_specs = [pl.BlockSpec((tq, group * dv), lambda g, i: (i, g)), pl.BlockSpec((group, tq, 1), lambda g, i: (g, i, 0))]
    out_shape = [jax.ShapeDtypeStruct((S, heads * dv), F32), jax.ShapeDtypeStruct((heads, S, 1), F32)]
    if side_gather is not None:
        ins.append(side_gather)
        in_specs.append(ANY)
        out_specs.append(ANY)
        out_shape.append(jax.ShapeDtypeStruct((N_CHIPS,) + side_gather.shape, side_gather.dtype))
        scratch += [pltpu.SemaphoreType.DMA((GATHER_COPIES,)), pltpu.SemaphoreType.DMA((GATHER_COPIES,))]
    outs = pl.pallas_call(body, name=name, grid=(heads // group, S // tq), in_specs=in_specs, out_specs=out_specs,
                          out_shape=out_shape, scratch_shapes=scratch, compiler_params=_params(2))(*ins)
    return outs if side_gather is None else (outs[0], outs[1], _place_own_slot(outs[2], side_gather))


def _attn_bwd(q, k, v, o, do, lse, *, sb, causal, heads, dq, dv, kcol=None, vcol=None, name):
    S, Sk = q.shape[0], k.shape[0]
    tq, tk = min(ATT_TQ, S), min(ATT_TK, Sk)
    sub = min(SB_SUB, tk) if sb else tk
    assert tq % sub == 0 or not causal
    nq = S // tq
    kcol = kcol or (lambda h: h)
    vcol = vcol or (lambda h: h)

    def body(*refs):
        if sb:
            q_ref, k_ref, v_ref, o_ref, do_ref, lse_ref, u_ref, dq_ref, dk_ref, dv_ref, acc_ref, r_ref, re_ref = refs
            r_ref[...] = jnp.zeros_like(r_ref)
            re_ref[...] = jnp.zeros_like(re_ref)
        else:
            q_ref, k_ref, v_ref, o_ref, do_ref, lse_ref, dq_ref, dk_ref, dv_ref, acc_ref = refs
        first_row = pl.program_id(1) * tq

        @pl.when(first_row == 0)
        def _():
            dk_ref[...] = jnp.zeros_like(dk_ref)
            dv_ref[...] = jnp.zeros_like(dv_ref)

        qb = q_ref[...]
        dof = do_ref[...].astype(F32)
        dob = dof.astype(BF16)
        q_t, do_t = qb.T, dob.T
        if not sb:
            dlt = jnp.sum(dof * o_ref[...], axis=1, keepdims=True)
        acc_ref[...] = jnp.zeros_like(acc_ref)
        nblk = (first_row + tq) // sub if causal else Sk // sub
        nfull = (first_row + (0 if sb else 1)) // sub if causal else nblk
        n_cut = tq // sub if causal else 0

        def products(jj):
            off = pl.multiple_of(jj * sub, sub)
            return _nt(qb, k_ref[pl.ds(off, sub), :]), _nt(dob, v_ref[pl.ds(off, sub), :])

        def piece(jj, now, masked):
            off = pl.multiple_of(jj * sub, sub)
            kb = k_ref[pl.ds(off, sub), :]
            s, dp = now
            if masked:
                qpos = first_row + lax.broadcasted_iota(jnp.int32, (tq, sub), 0)
                kpos = off + lax.broadcasted_iota(jnp.int32, (tq, sub), 1)
                valid = (kpos < qpos) if sb else (kpos <= qpos)
            if sb:
                u = u_ref[...]
                sp = _softplus(s)
                ls = jnp.where(valid, -sp, 0.0) if masked else -sp
                lb = s - sp
                w = jnp.exp(lb + (lse_ref[0] - (r_ref[...] + _running_sum(ls, u))))
                if masked:
                    w = jnp.where(valid, w, 0.0)
                e = dp * w
                ds = e - jnp.exp(lb) * (re_ref[...] + _running_sum(e, u, split=False))
                if masked:
                    ds = jnp.where(valid, ds, 0.0)
                r_ref[...] += jnp.sum(ls, axis=1, keepdims=True)
                re_ref[...] += jnp.sum(e, axis=1, keepdims=True)
            else:
                w = jnp.exp(s - lse_ref[0])
                if masked:
                    w = jnp.where(valid, w, 0.0)
                ds = w * (dp - dlt)
            dsb = ds.astype(BF16)
            dv_ref[:, pl.ds(off, sub)] += jnp.dot(do_t, w.astype(BF16), preferred_element_type=F32)
            dk_ref[:, pl.ds(off, sub)] += jnp.dot(q_t, dsb, preferred_element_type=F32)
            acc_ref[...] += jnp.dot(dsb, kb, preferred_element_type=F32)

        n_loop = nfull if causal else nblk - 1
        per_trip = tk // sub

        def steps(first, count, masked):
            ready = [products(first + c) for c in range(count)]
            for c in range(count):
                piece(first + c, ready[c], masked)

        def trip(t, carry):
            steps(t * per_trip, per_trip, False)
            return carry

        lax.fori_loop(0, n_loop // per_trip, trip, 0)
        steps(n_loop, n_cut if causal else 1, causal)
        dq_ref[...] = acc_ref[...]

    ins = [q, k, v, o, do]
    in_specs = [pl.BlockSpec((tq, dq), lambda h, i: (i, h)),
                pl.BlockSpec((Sk, dq), lambda h, i: (0, kcol(h))),
                pl.BlockSpec((Sk, dv), lambda h, i: (0, vcol(h))),
                pl.BlockSpec((tq, dv), lambda h, i: (i, h)),
                pl.BlockSpec((tq, dv), lambda h, i: (i, h))]
    scratch = [pltpu.VMEM((tq, dq), F32)]
    ins.append(lse)
    in_specs.append(pl.BlockSpec((1, tq, 1), lambda h, i: (h, i, 0)))
    if sb:
        ins.append(_triangle(sub, inclusive_prefix=True))
        in_specs.append(pl.BlockSpec((sub, sub), lambda h, i: (0, 0)))
        scratch += [pltpu.VMEM((tq, 1), F32), pltpu.VMEM((tq, 1), F32)]
    out_specs = [pl.BlockSpec((tq, dq), lambda h, i: (i, h)),
                 pl.BlockSpec((dq, Sk), lambda h, i: (h, 0)),
                 pl.BlockSpec((dv, Sk), lambda h, i: (h, 0))]
    out_shape = [jax.ShapeDtypeStruct((S, heads * dq), F32), jax.ShapeDtypeStruct((heads * dq, Sk), F32),
                 jax.ShapeDtypeStruct((heads * dv, Sk), F32)]
    return pl.pallas_call(body, name=name, grid=(heads, nq), in_specs=in_specs, out_specs=out_specs,
                          out_shape=out_shape, scratch_shapes=scratch, compiler_params=_params(2))(*ins)


GELU_C = 0.7978845608028654
assert 2 * SG_GD == LANE and SG_CHUNK == LANE


def _gelu(z):
    t = jnp.tanh(GELU_C * (z + 0.044715 * z * z * z))
    return 0.5 * z * (1.0 + t), t


def _gelu_grad(z, t):
    return 0.5 * (1.0 + t) + 0.5 * z * (1.0 - t * t) * GELU_C * (1.0 + 3.0 * 0.044715 * z * z)


def _layernorm_parts(g):
    d = g - jnp.mean(g, axis=-1, keepdims=True)
    rstd = lax.rsqrt(jnp.mean(d * d, axis=-1, keepdims=True) + EPS)
    return d * rstd, rstd


def _gelu_ln(z, gain, bias, *, name):
    def fn(zv, gn, bs):
        a, _ = _gelu(zv)
        y, _ = _layernorm_parts(a[:, SG_W:])
        return a[:, :SG_W], y * gn + bs

    return _rowwise(fn, [z], [gain.reshape(1, SG_W), bias.reshape(1, SG_W)], [(SG_W, F32), (SG_W, BF16)], name=name)


def _gelu_ln_bwd(z, du, dgl, gain, *, name):
    def fn(zv, duv, dglv, gn):
        a, t = _gelu(zv)
        y, rstd = _layernorm_parts(a[:, SG_W:])
        dy = dglv * gn
        dgg = rstd * (dy - jnp.mean(dy, axis=-1, keepdims=True) - y * jnp.mean(dy * y, axis=-1, keepdims=True))
        dz = jnp.concatenate([duv, dgg], axis=1) * _gelu_grad(zv, t)
        return dz, jnp.sum(dglv * y, axis=0, keepdims=True), jnp.sum(dglv, axis=0, keepdims=True)

    return _rowwise(fn, [z, du, dgl], [gain.reshape(1, SG_W)], [(2 * SG_W, BF16)], [(1, SG_W), (1, SG_W)], name=name)


def _sg_masks():
    tri = lax.broadcasted_iota(jnp.int32, (SG_CHUNK, SG_CHUNK), 0) >= lax.broadcasted_iota(jnp.int32, (SG_CHUNK, SG_CHUNK), 1)
    first = lax.broadcasted_iota(jnp.int32, (SG_CHUNK, LANE), 1) < SG_GD
    return tri, first


def _spatial(gl, u, w, bt, *, name):
    S = gl.shape[0]
    tm = _row_tile(S, 512)
    nch = tm // SG_CHUNK

    def body(gl_ref, u_ref, w_ref, bt_ref, o_ref):
        tri, first = _sg_masks()
        for p in range(SG_W // LANE):
            cols = slice(p * LANE, (p + 1) * LANE)
            wa = jnp.where(tri, w_ref[2 * p], 0.0).astype(BF16)
            wb = jnp.where(tri, w_ref[2 * p + 1], 0.0).astype(BF16)
            for ci in range(nch):
                rws = slice(ci * SG_CHUNK, (ci + 1) * SG_CHUNK)
                g = gl_ref[rws, cols]
                zero = jnp.zeros_like(g)
                mixed = (jnp.dot(wa, jnp.where(first, g, zero), preferred_element_type=F32)
                         + jnp.dot(wb, jnp.where(first, zero, g), preferred_element_type=F32) + bt_ref[:, cols])
                o_ref[rws, cols] = u_ref[rws, cols] * mixed

    row = pl.BlockSpec((tm, SG_W), lambda i: (i, 0))
    return pl.pallas_call(
        body, name=name, grid=(S // tm,),
        in_specs=[row, row, pl.BlockSpec(w.shape, lambda i: (0, 0, 0)), pl.BlockSpec(bt.shape, lambda i: (0, 0))],
        out_specs=row, out_shape=jax.ShapeDtypeStruct((S, SG_W), F32), compiler_params=_params(1))(gl, u, w, bt)


def _spatial_bwd(d_o, gl, u, w, bt, *, name):
    S = gl.shape[0]
    tm = _row_tile(S, 512)
    nch = tm // SG_CHUNK
    nsteps = S // tm

    def body(do_ref, gl_ref, u_ref, w_ref, bt_ref, du_ref, dgl_ref, dw_ref, db_ref, dbt_ref):
        tri, first = _sg_masks()
        step = pl.program_id(0)

        @pl.when(step == 0)
        def _():
            dw_ref[...] = jnp.zeros_like(dw_ref)
            dbt_ref[...] = jnp.zeros_like(dbt_ref)

        for p in range(SG_W // LANE):
            cols = slice(p * LANE, (p + 1) * LANE)
            wa = jnp.where(tri, w_ref[2 * p], 0.0).astype(BF16)
            wb = jnp.where(tri, w_ref[2 * p + 1], 0.0).astype(BF16)
            for ci in range(nch):
                rws = slice(ci * SG_CHUNK, (ci + 1) * SG_CHUNK)
                g = gl_ref[rws, cols]
                zero = jnp.zeros_like(g)
                mixed = (jnp.dot(wa, jnp.where(first, g, zero), preferred_element_type=F32)
                         + jnp.dot(wb, jnp.where(first, zero, g), preferred_element_type=F32) + bt_ref[:, cols])
                dov = do_ref[rws, cols]
                du_ref[rws, cols] = dov * mixed
                dm = dov * u_ref[rws, cols]
                dbt_ref[:, cols] += dm
                dma = jnp.where(first, dm, 0.0).astype(BF16)
                dmb = jnp.where(first, 0.0, dm).astype(BF16)
                dw_ref[2 * p] += jnp.where(tri, _nt(dma, g), 0.0)
                dw_ref[2 * p + 1] += jnp.where(tri, _nt(dmb, g), 0.0)
                dgl_ref[rws, cols] = _tn(wa, dma) + _tn(wb, dmb)

        @pl.when(step == nsteps - 1)
        def _():
            lane = lax.broadcasted_iota(jnp.int32, (SG_CHUNK, LANE), 1)
            acc = jnp.zeros((SG_CHUNK, LANE), F32)
            for p in range(SG_W // LANE):
                blk = dbt_ref[:, p * LANE:(p + 1) * LANE]
                sa = jnp.sum(jnp.where(first, blk, 0.0), axis=1, keepdims=True)
                sb_ = jnp.sum(jnp.where(first, 0.0, blk), axis=1, keepdims=True)
                acc = acc + jnp.where(lane == 2 * p, sa, 0.0) + jnp.where(lane == 2 * p + 1, sb_, 0.0)
            db_ref[...] = acc

    row = pl.BlockSpec((tm, SG_W), lambda i: (i, 0))
    return pl.pallas_call(
        body, name=name, grid=(nsteps,),
        in_specs=[row, row, row, pl.BlockSpec(w.shape, lambda i: (0, 0, 0)), pl.BlockSpec(bt.shape, lambda i: (0, 0))],
        out_specs=[row, row, pl.BlockSpec(w.shape, lambda i: (0, 0, 0)), pl.BlockSpec((SG_CHUNK, LANE), lambda i: (0, 0))],
        out_shape=[jax.ShapeDtypeStruct((S, SG_W), F32), jax.ShapeDtypeStruct((S, SG_W), F32),
                   jax.ShapeDtypeStruct(w.shape, F32), jax.ShapeDtypeStruct((SG_CHUNK, LANE), F32)],
        scratch_shapes=[pltpu.VMEM((SG_CHUNK, SG_W), F32)], compiler_params=_params(1))(d_o, gl, u, w, bt)


ROPE_HALF = MLA_ROPE // 2
KR_COL = (MLA_QL + MLA_KVL) // LANE
MLA_IN_PAD = MLA_QL + MLA_KVL + LANE


def _rope_tables(positions):
    inv_freq = ROPE_THETA ** (-jnp.arange(ROPE_HALF, dtype=F32) / ROPE_HALF)
    ang = positions.astype(F32)[:, None] * inv_freq
    cos, sin = jnp.cos(ang), jnp.sin(ang)
    S = positions.shape[0]
    z16, tail = jnp.zeros((S, ROPE_HALF), F32), jnp.zeros((S, LANE - MLA_QK), F32)
    ones = jnp.ones((S, MLA_NOPE), F32)
    zeros = jnp.zeros((S, MLA_NOPE), F32)
    return (jnp.concatenate([ones, cos, cos, tail], axis=1), jnp.concatenate([zeros, z16, sin, tail], axis=1),
            jnp.concatenate([zeros, -sin, z16, tail], axis=1))


def _rope(x, cos, sa, sb):
    return x * cos + pltpu.roll(x, ROPE_HALF, 1) * sa + pltpu.roll(x, LANE - ROPE_HALF, 1) * sb


def _rope_t(dy, cos, sa, sb):
    return dy * cos + pltpu.roll(dy * sa, LANE - ROPE_HALF, 1) + pltpu.roll(dy * sb, ROPE_HALF, 1)


def _mla_lora(P, qlg, kvlg, *, name):
    def fn(pv, a, b):
        return _rmsnorm_fwd(pv[:, :MLA_QL], a), _rmsnorm_fwd(pv[:, MLA_QL:MLA_QL + MLA_KVL], b)

    return _rowwise(fn, [P], [qlg.reshape(1, MLA_QL), kvlg.reshape(1, MLA_KVL)], [(MLA_QL, BF16), (MLA_KVL, BF16)], name=name)


def _mla_lora_bwd(dcq, dckv, dkr, P, qlg, kvlg, *, name):
    def fn(d1, d2, d3, pv, a, b):
        x1, g1 = _rmsnorm_bwd(d1, pv[:, :MLA_QL], a)
        x2, g2 = _rmsnorm_bwd(d2, pv[:, MLA_QL:MLA_QL + MLA_KVL], b)
        return jnp.concatenate([x1, x2, d3], axis=1), g1, g2

    return _rowwise(fn, [dcq, dckv, dkr, P], [qlg.reshape(1, MLA_QL), kvlg.reshape(1, MLA_KVL)], [(MLA_IN_PAD, BF16)],
                    [(1, MLA_QL), (1, MLA_KVL)], name=name)


def _mla_qk(q_pre, k_pre, P, tabs, qg, kg, *, name):
    def fn(qp, kp, kr, c, a, b, qgv, kgv):
        return (_rope(_rmsnorm_fwd(qp, qgv, MLA_QK), c, a, b) * MLA_SCALE,
                _rope(_rmsnorm_fwd(kp + kr, kgv, MLA_QK), c, a, b))

    hcol = lambda h: h
    rows = [(q_pre, LANE, hcol), (k_pre, LANE, hcol), (P, LANE, lambda h: KR_COL), *tabs]
    w = MLA_HEADS * LANE
    return _rowwise(fn, rows, [qg, kg], [(w, BF16), (w, BF16)], heads=MLA_HEADS, tm=HEAD_ROWS, name=name)


def _mla_qk_bwd(dq, dk_t, q_pre, k_pre, P, tabs, qg, kg, *, name):
    def fn(dqv, dkv, qp, kp, kr, c, a, b, qgv, kgv):
        dqp, dqg = _rmsnorm_bwd(_rope_t(dqv * MLA_SCALE, c, a, b), qp, qgv, MLA_QK)
        dkp, dkg = _rmsnorm_bwd(_rope_t(dkv, c, a, b), kp + kr, kgv, MLA_QK)
        lane = lax.broadcasted_iota(jnp.int32, (1, LANE), 1)
        return dqp, dkp, jnp.where((lane >= MLA_NOPE) & (lane < MLA_QK), dkp, 0.0), dqg, dkg

    hcol = lambda h: h
    rows = [(dq, LANE, hcol), (dk_t, LANE, hcol, True), (q_pre, LANE, hcol), (k_pre, LANE, hcol),
            (P, LANE, lambda h: KR_COL), *tabs]
    w = MLA_HEADS * LANE
    return _rowwise(fn, rows, [qg, kg], [(w, BF16), (w, BF16)], [(1, LANE), (1, LANE)], [LANE], heads=MLA_HEADS,
                    tm=HEAD_ROWS, name=name)


def _head_norm(x, g, *, heads, width, colfn=None, scale=1.0, name):
    return _rowwise(lambda xv, gv: _rmsnorm_fwd(xv, gv) * scale, [(x, width, colfn or (lambda h: h))],
                    [g.reshape(1, width)], [(heads * width, BF16)], heads=heads, tm=HEAD_ROWS, name=name)[0]


def _head_norm_bwd(dy, x, g, *, heads, width, colfn=None, scale=1.0, out_dtype, name):
    return _rowwise(lambda dv_, xv, gv: _rmsnorm_bwd(dv_ * scale, xv, gv),
                    [(dy, width, lambda h: h), (x, width, colfn or (lambda h: h))],
                    [g.reshape(1, width)], [(heads * width, out_dtype)], [(1, width)], heads=heads, tm=HEAD_ROWS,
                    name=name)


def _loss_grad(y, tgt, *, name):
    D = y.shape[1]

    def fn(yv, tv):
        d = yv - tv
        return d * (1.0 / D), jnp.sum(d * d, axis=0, keepdims=True) * (0.5 / D)

    dy, part = _rowwise(fn, [y, tgt], [], [(D, F32)], [(1, D)], name=name)
    return jnp.sum(part), dy


def _adamw(w, g, m, v, *, name):
    shape = w.shape
    two_d = (-1, shape[-1])

    def fn(wv, gv, mv, vv):
        m2 = ADAM_B1 * mv + (1.0 - ADAM_B1) * gv
        v2 = ADAM_B2 * vv + (1.0 - ADAM_B2) * (gv * gv)
        m_hat = m2 / (1.0 - ADAM_B1 ** ADAM_STEP)
        v_hat = v2 / (1.0 - ADAM_B2 ** ADAM_STEP)
        return -ADAM_LR * (m_hat / (jnp.sqrt(v_hat) + ADAM_EPS) + ADAM_WD * wv), m2, v2

    outs = _rowwise(fn, [t.reshape(two_d) for t in (w, g, m, v)], [], [(shape[-1], F32)] * 3, name=name)
    return [o.reshape(shape) for o in outs]


def _pad_cols(w, heads, hd):
    k = w.shape[0]
    return jnp.pad(w.reshape(k, heads, hd), ((0, 0), (0, 0), (0, LANE - hd))).reshape(k, heads * LANE)


def _unpad_cols(w, heads, hd):
    k = w.shape[0]
    return w.reshape(k, heads, LANE)[:, :, :hd].reshape(k, heads * hd)


def _pad_rows(w, heads, hd):
    n = w.shape[1]
    return jnp.pad(w.reshape(heads, hd, n), ((0, 0), (0, LANE - hd), (0, 0))).reshape(heads * LANE, n)


def _unpad_rows(w, heads, hd):
    n = w.shape[1]
    return w.reshape(heads, LANE, n)[:, :hd, :].reshape(heads * hd, n)


def _ffn_fwd(x, g, wgu, wd, tag):
    h = _norm_rows(x, g, name=tag + "_norm")
    gate, up, act = _mm_swiglu(h, wgu, name=tag + "_gu")
    y = _mm(act, wd, scale=0.5, residual=x, name=tag + "_down")
    return y, (x, h, gate, up, act)


def _ffn_bwd(dy, saved, g, wgu, wd, tag):
    x, h, gate, up, act = saved
    F = wd.shape[0]
    dwd = _mm(act, dy, ta=True, scale=0.5, name=tag + "_dwd")
    dgate, dup = _mm_dswiglu(dy, wd, gate, up, scale=0.5, name=tag + "_dact")
    dh = _mm(dgate, wgu, tb=True, name=tag + "_dh_g")
    dx, dg = _mm(dup, wgu, tb=True, b_off=(0, F), residual=dh, norm_bwd=(x, g, dy), name=tag + "_dh_u")
    dwgu = jnp.concatenate([_mm(h, dgate, ta=True, name=tag + "_dwg"), _mm(h, dup, ta=True, name=tag + "_dwu")], axis=1)
    return dx, dg, dwgu, dwd


def _even_weights(w_in, w_out):
    parts = [w_in[:, :SB_W] * SB_SCALE, w_in[:, SB_W:2 * SB_W], w_in[:, 2 * SB_W:3 * SB_W]]
    wqkv = jnp.concatenate([_pad_cols(p, SB_HEADS, SB_HD) for p in parts], axis=1)
    return wqkv, w_in[:, 3 * SB_W:], _pad_rows(w_out[:SB_W], SB_HEADS, SB_HD), w_out[SB_W:]


def _even_fwd(x, g, wts, ln_g, ln_b, sgu_w, bt, late_shard, tag):
    wqkv, wz, wo_sb, wo_sg = wts
    h = _norm_rows(x, g, name=tag + "_norm")
    qkv = _mm(h, wqkv, out_dtype=BF16, name=tag + "_qkv")
    z = _mm(h, wz, name=tag + "_z")
    o_sb, tot, late = _attn_fwd(qkv, qkv, qkv, sb=True, causal=True, heads=SB_HEADS, dq=LANE, dv=LANE, group=FWD_GROUP,
                                kcol=lambda g: SB_HEADS // FWD_GROUP + g, vcol=lambda g: 2 * SB_HEADS // FWD_GROUP + g,
                                side_gather=late_shard, name=tag + "_sb")
    u, gl = _gelu_ln(z, ln_g, ln_b, name=tag + "_geluln")
    o_sg = _spatial(gl, u, sgu_w, bt, name=tag + "_sgu")
    y = _mm(o_sb, wo_sb, residual=x, name=tag + "_out_sb")
    y = _mm(o_sg, wo_sg, residual=y, name=tag + "_out_sg")
    return y, (x, h, qkv, z, o_sb, tot, u, gl, o_sg), late


def _even_bwd(dy, saved, g, wts, ln_g, sgu_w, bt, tag):
    wqkv, wz, wo_sb, wo_sg = wts
    x, h, qkv, z, o_sb, tot, u, gl, o_sg = saved
    do_sb = _mm(dy, wo_sb, tb=True, name=tag + "_do_sb")
    do_sg = _mm(dy, wo_sg, tb=True, name=tag + "_do_sg")
    dwo = jnp.concatenate([_unpad_rows(_mm(o_sb, dy, ta=True, name=tag + "_dwo_sb"), SB_HEADS, SB_HD),
                           _mm(o_sg, dy, ta=True, name=tag + "_dwo_sg")], axis=0)
    dq, dk_t, dv_t = _attn_bwd(qkv, qkv, qkv, o_sb, do_sb, tot, sb=True, causal=True, heads=SB_HEADS, dq=LANE, dv=LANE,
                               kcol=lambda hh: SB_HEADS + hh, vcol=lambda hh: 2 * SB_HEADS + hh, name=tag + "_sb_bwd")
    du, dgl, dsgu_w, db_t = _spatial_bwd(do_sg, gl, u, sgu_w, bt, name=tag + "_sgu_bwd")
    dz, dln_g, dln_b = _gelu_ln_bwd(z, du, dgl, ln_g, name=tag + "_geluln_bwd")
    dh = _mm(dz, wz, tb=True, name=tag + "_dh_z")
    dh = _mm(dq, wqkv, tb=True, residual=dh, name=tag + "_dh_q")
    dws = [_unpad_cols(_mm(h, dq, ta=True, scale=SB_SCALE, name=tag + "_dw_q"), SB_HEADS, SB_HD)]
    for i, (d_t, nm) in enumerate(((dk_t, "k"), (dv_t, "v")), start=1):
        dh = _mm(d_t, wqkv, ta=True, tb=True, b_off=(0, i * SB_HEADS * LANE), residual=dh,
                 norm_bwd=(x, g, dy) if nm == "v" else None, name=tag + "_dh_" + nm)
        dws.append(_unpad_rows(_mm(d_t, h, name=tag + "_dw_" + nm), SB_HEADS, SB_HD).T)
    dws.append(_mm(h, dz, ta=True, name=tag + "_dw_z"))
    dx, dg = dh
    return dx, dict(mix_norm=dg, sbg_w_in=jnp.concatenate(dws, axis=1), sgu_ln_gain=dln_g, sgu_ln_bias=dln_b,
                    sgu_w=dsgu_w, sgu_b=db_t[:, :SG_GROUPS].T, sbg_w_out=dwo)


def _mla_weights(w_in, w_uq, w_ukv, w_out, q_gain, k_gain):
    d = w_in.shape[0]
    lat = MLA_QL + MLA_KVL
    w_in_ext = jnp.concatenate([w_in[:, :lat], jnp.zeros((d, MLA_NOPE), w_in.dtype), w_in[:, lat:],
                                jnp.zeros((d, LANE - MLA_QK), w_in.dtype)], axis=1)
    kv = w_ukv.reshape(MLA_KVL, MLA_HEADS, MLA_NOPE + MLA_V)
    wk = _pad_cols(kv[:, :, :MLA_NOPE].reshape(MLA_KVL, -1), MLA_HEADS, MLA_NOPE)
    wv = _pad_cols(kv[:, :, MLA_NOPE:].reshape(MLA_KVL, -1), MLA_HEADS, MLA_V)
    pad_gain = lambda gn: jnp.pad(gn.reshape(1, MLA_QK), ((0, 0), (0, LANE - MLA_QK)))
    return (w_in_ext, _pad_cols(w_uq, MLA_HEADS, MLA_QK), wk, wv, _pad_rows(w_out, MLA_HEADS, MLA_V),
            pad_gain(q_gain), pad_gain(k_gain))


def _mla_fwd(x, g, wts, qlg, kvlg, tabs, tag):
    w_in, w_uq, wk, wv, w_out, qg, kg = wts
    h = _norm_rows(x, g, name=tag + "_norm")
    P = _mm(h, w_in, name=tag + "_in")
    cqn, ckvn = _mla_lora(P, qlg, kvlg, name=tag + "_lora")
    q_pre = _mm(cqn, w_uq, name=tag + "_uq")
    k_pre = _mm(ckvn, wk, name=tag + "_uk")
    ones_lane = jnp.tile((jnp.arange(LANE) == MLA_V).astype(F32), MLA_HEADS)[None, :]
    v = _mm(ckvn, wv, out_dtype=BF16, bias=ones_lane, name=tag + "_uv")
    q, k = _mla_qk(q_pre, k_pre, P, tabs, qg, kg, name=tag + "_qk")
    o, lse = _attn_fwd(q, k, v, sb=False, causal=True, heads=MLA_HEADS, dq=LANE, dv=LANE, group=FWD_GROUP,
                       sum_lane=MLA_V, name=tag + "_attn")
    y = _mm(o, w_out, residual=x, name=tag + "_out")
    return y, (x, h, P, cqn, ckvn, q_pre, k_pre, q, k, v, o, lse)


def _mla_bwd(dy, saved, g, wts, qlg, kvlg, tabs, tag):
    w_in, w_uq, wk, wv, w_out, qg, kg = wts
    x, h, P, cqn, ckvn, q_pre, k_pre, q, k, v, o, lse = saved
    do = _mm(dy, w_out, tb=True, name=tag + "_do")
    dw_out = _unpad_rows(_mm(o, dy, ta=True, name=tag + "_dwo"), MLA_HEADS, MLA_V)
    dq, dk_t, dv_t = _attn_bwd(q, k, v, o, do, lse, sb=False, causal=True, heads=MLA_HEADS, dq=LANE, dv=LANE,
                               name=tag + "_attn_bwd")
    dq_pre, dk_pre, dkr, dqg, dkg = _mla_qk_bwd(dq, dk_t, q_pre, k_pre, P, tabs, qg, kg, name=tag + "_qk_bwd")
    dcqn = _mm(dq_pre, w_uq, tb=True, name=tag + "_dcq")
    dckvn = _mm(dk_pre, wk, tb=True, name=tag + "_dckv_k")
    dckvn = _mm(dv_t, wv, ta=True, tb=True, residual=dckvn, name=tag + "_dckv_v")
    dw_uq = _unpad_cols(_mm(cqn, dq_pre, ta=True, name=tag + "_dwuq"), MLA_HEADS, MLA_QK)
    dwk = _unpad_cols(_mm(ckvn, dk_pre, ta=True, name=tag + "_dwk"), MLA_HEADS, MLA_NOPE)
    dwv = _unpad_rows(_mm(dv_t, ckvn, name=tag + "_dwv"), MLA_HEADS, MLA_V).T
    dw_ukv = jnp.concatenate([dwk.reshape(MLA_KVL, MLA_HEADS, MLA_NOPE), dwv.reshape(MLA_KVL, MLA_HEADS, MLA_V)],
                             axis=2).reshape(MLA_KVL, -1)
    dP, dqlg, dkvlg = _mla_lora_bwd(dcqn, dckvn, dkr, P, qlg, kvlg, name=tag + "_lora_bwd")
    dx, dg = _mm(dP, w_in, tb=True, norm_bwd=(x, g, dy), name=tag + "_dh")
    dw_in_ext = _mm(h, dP, ta=True, name=tag + "_dwin")
    lat = MLA_QL + MLA_KVL
    dw_in = jnp.concatenate([dw_in_ext[:, :lat], dw_in_ext[:, lat + MLA_NOPE:lat + MLA_QK]], axis=1)
    return dx, dict(mix_norm=dg, mla_w_in=dw_in, mla_q_lora_gain=dqlg, mla_kv_lora_gain=dkvlg, mla_w_uq=dw_uq,
                    mla_w_ukv=dw_ukv, mla_q_gain=dqg[:, :MLA_QK], mla_k_gain=dkg[:, :MLA_QK], mla_w_out=dw_out)


def _xmem_fwd(x, mem, g, gm, wq, wkv, qg, kg, wo, tag):
    hq = _norm_rows(x, g, name=tag + "_norm")
    hm = _norm_rows(mem, gm, name=tag + "_mnorm")
    qp = _mm(hq, wq, name=tag + "_q")
    kv = _mm(hm, wkv, name=tag + "_kv")
    q = _head_norm(qp, qg, heads=MEM_HEADS, width=MEM_HD, scale=MEM_SCALE, name=tag + "_qn")
    kn = _head_norm(kv, kg, heads=MEM_HEADS, width=MEM_HD, colfn=lambda hh: 2 * hh, name=tag + "_kn")
    kvb = kv.reshape(-1, MEM_HEADS, 2, MEM_HD)[:, :, 1].reshape(-1, MEM_HEADS * MEM_HD).astype(BF16)
    o, lse = _attn_fwd(q, kn, kvb, sb=False, causal=False, heads=MEM_HEADS, dq=MEM_HD, dv=MEM_HD, group=MEM_HEADS,
                       name=tag + "_attn")
    y = _mm(o, wo, residual=x, name=tag + "_out")
    return y, (x, hq, hm, qp, kv, q, kn, kvb, o, lse)


def _xmem_bwd(dy, saved, mem, g, gm, wq, wkv, qg, kg, wo, tag):
    x, hq, hm, qp, kv, q, kn, kvb, o, lse = saved
    m = mem.shape[0]
    do = _mm(dy, wo, tb=True, name=tag + "_do")
    dwo = _mm(o, dy, ta=True, name=tag + "_dwo")
    dq, dk_t, dv_t = _attn_bwd(q, kn, kvb, o, do, lse, sb=False, causal=False, heads=MEM_HEADS, dq=MEM_HD, dv=MEM_HD,
                               name=tag + "_attn_bwd")
    dk, dv = dk_t.T, dv_t.T
    dqp, dqg = _head_norm_bwd(dq, qp, qg, heads=MEM_HEADS, width=MEM_HD, scale=MEM_SCALE, out_dtype=BF16,
                              name=tag + "_qn_bwd")
    dkp, dkg = _head_norm_bwd(dk, kv, kg, heads=MEM_HEADS, width=MEM_HD, colfn=lambda hh: 2 * hh, out_dtype=F32,
                              name=tag + "_kn_bwd")
    dkv = jnp.concatenate([dkp.reshape(m, MEM_HEADS, MEM_HD), dv.reshape(m, MEM_HEADS, MEM_HD)], axis=2).reshape(m, -1)
    dwkv = _mm(hm, dkv, ta=True, name=tag + "_dwkv")
    dhm = _mm(dkv, wkv, tb=True, name=tag + "_dhm")
    _, dgm = _norm_rows_bwd(dhm, mem, gm, None, name=tag + "_dmnorm")
    dwq = _mm(hq, dqp, ta=True, name=tag + "_dwq")
    dx, dg = _mm(dqp, wq, tb=True, norm_bwd=(x, g, dy), name=tag + "_dhq")
    return dx, dict(xmem_norm=dg, xmem_mem_norm=dgm, xmem_wq=dwq, xmem_wkv=dwkv, xmem_q_gain=dqg, xmem_k_gain=dkg,
                    xmem_wo=dwo)


def _local_step(x, mem, positions, tgt, w, late_shard, finish_late):
    tabs = _rope_tables(positions)
    even = _even_weights(w["sbg_w_in"][0], w["sbg_w_out"][0])
    bt = jnp.repeat(w["sgu_b"][0].T, SG_GD, axis=1)
    saved = []
    for l in range(2):
        t = f"l{l}"
        x, s_pre = _ffn_fwd(x, w["ffn_pre_norm"][l], w["ffn_pre_w_gu"][l], w["ffn_pre_w_down"][l], t + "_pre")
        if l == 0:
            x, s_mix, late = _even_fwd(x, w["mix_norm"][0], even, w["sgu_ln_gain"][0], w["sgu_ln_bias"][0], w["sgu_w"][0],
                                       bt, late_shard, t + "_even")
            finish_late(late)
            mla = _mla_weights(w["mla_w_in"][0], w["mla_w_uq"][0], w["mla_w_ukv"][0], w["mla_w_out"][0],
                               w["mla_q_gain"][0], w["mla_k_gain"][0])
        else:
            x, s_mix = _mla_fwd(x, w["mix_norm"][1], mla, w["mla_q_lora_gain"][0], w["mla_kv_lora_gain"][0], tabs,
                                t + "_mla")
        x, s_xm = _xmem_fwd(x, mem, w["xmem_norm"][l], w["xmem_mem_norm"][l], w["xmem_wq"][l], w["xmem_wkv"][l],
                            w["xmem_q_gain"][l], w["xmem_k_gain"][l], w["xmem_wo"][l], t + "_xm")
        x, s_post = _ffn_fwd(x, w["ffn_post_norm"][l], w["ffn_post_w_gu"][l], w["ffn_post_w_down"][l], t + "_post")
        saved.append((s_pre, s_mix, s_xm, s_post))
    loss, dx = _loss_grad(x, tgt, name="loss")
    grads = {}

    def put(name, l, val):
        grads.setdefault(name, {})[l] = val

    for l in (1, 0):
        t = f"l{l}"
        s_pre, s_mix, s_xm, s_post = saved[l]
        dx, dg, dwgu, dwd = _ffn_bwd(dx, s_post, w["ffn_post_norm"][l], w["ffn_post_w_gu"][l], w["ffn_post_w_down"][l],
                                     t + "_post")
        put("ffn_post_norm", l, dg), put("ffn_post_w_gu", l, dwgu), put("ffn_post_w_down", l, dwd)
        dx, gx = _xmem_bwd(dx, s_xm, mem, w["xmem_norm"][l], w["xmem_mem_norm"][l], w["xmem_wq"][l], w["xmem_wkv"][l],
                           w["xmem_q_gain"][l], w["xmem_k_gain"][l], w["xmem_wo"][l], t + "_xm")
        for k_, v_ in gx.items():
            put(k_, l, v_)
        if l == 0:
            dx, gm = _even_bwd(dx, s_mix, w["mix_norm"][0], even, w["sgu_ln_gain"][0], w["sgu_w"][0], bt, t + "_even")
        else:
            dx, gm = _mla_bwd(dx, s_mix, w["mix_norm"][1], mla, w["mla_q_lora_gain"][0], w["mla_kv_lora_gain"][0], tabs,
                              t + "_mla")
        for k_, v_ in gm.items():
            put(k_, l if k_ == "mix_norm" else 0, v_)
        dx, dg, dwgu, dwd = _ffn_bwd(dx, s_pre, w["ffn_pre_norm"][l], w["ffn_pre_w_gu"][l], w["ffn_pre_w_down"][l],
                                     t + "_pre")
        put("ffn_pre_norm", l, dg), put("ffn_pre_w_gu", l, dwgu), put("ffn_pre_w_down", l, dwd)
    return loss, dx, {k_: [v_[l] for l in sorted(v_)] for k_, v_ in grads.items()}


N_CHIPS = 4
PACK_COLS = 1024
PACK_ROW_MULTIPLE = 512


def _place():
    x, y, c = lax.axis_index("x"), lax.axis_index("y"), lax.axis_index("c")
    return x, y, c, [(1 - x, y), (x, 1 - y), (1 - x, 1 - y)]


def _hops(x, y, c):
    return ((x + 1 - c) % 2, (y + c) % 2), ((x + c) % 2, (y + 1 - c) % 2), (1 - x, 1 - y)


GATHER_COPIES = 6
GATHER_STAGES = 4


def _gather_stages(x_ref, out_ref, send_sems, recv_sems):
    Rh = x_ref.shape[0] // 2
    x, y, c = lax.axis_index("x"), lax.axis_index("y"), lax.axis_index("c")
    n1, n2, nd = _hops(x, y, c)
    me, q1, q2, qd = 2 * x + y, 2 * n1[0] + n1[1], 2 * n2[0] + n2[1], 2 * nd[0] + nd[1]
    sibling = (x, y, 1 - c)

    def half(chip, core):
        return out_ref.at[chip, pl.ds(core * Rh, Rh), :]

    def copy(k, chip, core, to, own=False):
        return pltpu.make_async_remote_copy(src_ref=x_ref.at[pl.ds(c * Rh, Rh), :] if own else half(chip, core),
                                            dst_ref=half(chip, core), send_sem=send_sems.at[k], recv_sem=recv_sems.at[k],
                                            device_id=to, device_id_type=MESH)

    sends = [lambda: copy(0, me, c, (*n1, c), own=True), lambda: copy(1, me, c, (*n2, c), own=True),
             lambda: copy(2, q1, c, (*n2, c)), lambda: copy(3, q1, c, sibling), lambda: copy(4, q2, c, sibling),
             lambda: copy(5, qd, c, sibling)]

    def own_halves_out():
        sends[0]().start()
        sends[1]().start()

    def first_neighbours_on():
        copy(0, q1, c, sibling).wait_recv()
        sends[2]().start()
        sends[3]().start()

    def others_to_sibling():
        copy(1, q2, c, sibling).wait_recv()
        sends[4]().start()
        copy(2, qd, c, sibling).wait_recv()
        sends[5]().start()

    def all_landed():
        copy(3, q2, 1 - c, sibling).wait_recv()
        copy(4, q1, 1 - c, sibling).wait_recv()
        copy(5, qd, 1 - c, sibling).wait_recv()
        for send in sends:
            send().wait_send()

    return own_halves_out, first_neighbours_on, others_to_sibling, all_landed


def _place_own_slot(others, shard):
    return lax.dynamic_update_slice(others, shard[None], (2 * lax.axis_index("x") + lax.axis_index("y"), 0, 0))


def _gather_chips(shard):
    def body(x_ref, out_ref, send_sems, recv_sems):
        for stage in _gather_stages(x_ref, out_ref, send_sems, recv_sems):
            stage()

    others = pl.pallas_call(
        body, name="gather_weights", out_shape=jax.ShapeDtypeStruct((N_CHIPS,) + shard.shape, shard.dtype),
        in_specs=[ANY], out_specs=ANY,
        scratch_shapes=[pltpu.SemaphoreType.DMA((GATHER_COPIES,)), pltpu.SemaphoreType.DMA((GATHER_COPIES,))])(shard)
    return _place_own_slot(others, shard)


def _gather_devices(block):
    M, N = block.shape

    def body(x_ref, out_ref, send_sems, recv_sems, local_sem):
        x, y, c, chips = _place()
        me, sibling = (x, y, c), (x, y, 1 - c)

        def rows(px, py, pc):
            return out_ref.at[pl.ds((4 * px + 2 * py + pc) * M, M), :]

        def copy(k, blk, to, src=None):
            return pltpu.make_async_remote_copy(src_ref=rows(*blk) if src is None else src, dst_ref=rows(*blk),
                                                send_sem=send_sems.at[k], recv_sem=recv_sems.at[k], device_id=to,
                                                device_id_type=MESH)

        mine = pltpu.make_async_copy(x_ref, rows(*me), local_sem)
        mine.start()
        first = [copy(0, me, sibling, src=x_ref)]
        first += [copy(1 + j, me, (*chip, c), src=x_ref) for j, chip in enumerate(chips)]
        for cp in first:
            cp.start()
        passed = [copy(4 + j, (*chip, c), sibling) for j, chip in enumerate(chips)]
        for j, chip in enumerate(chips):
            copy(1 + j, (*chip, c), me).wait_recv()
            passed[j].start()
        copy(0, sibling, me).wait_recv()
        for j, chip in enumerate(chips):
            copy(4 + j, (*chip, 1 - c), me).wait_recv()
        for cp in first + passed:
            cp.wait_send()
        mine.wait()

    vmem = pl.BlockSpec(memory_space=pltpu.VMEM)
    return pl.pallas_call(
        body, name=f"gather_devices_{M}", out_shape=jax.ShapeDtypeStruct((8 * M, N), block.dtype),
        in_specs=[vmem], out_specs=vmem,
        scratch_shapes=[pltpu.SemaphoreType.DMA((7,)), pltpu.SemaphoreType.DMA((7,)), pltpu.SemaphoreType.DMA],
        compiler_params=pltpu.CompilerParams(vmem_limit_bytes=VMEM_LIMIT))(block)


def _swap_halves(g):
    n, R, C = g.shape
    Rh = R // 2

    def body(g_ref, a_ref, send_sem, recv_sem):
        x, y, c, _ = _place()
        cp = pltpu.make_async_remote_copy(src_ref=g_ref.at[:, pl.ds((1 - c) * Rh, Rh), :], dst_ref=a_ref,
                                          send_sem=send_sem, recv_sem=recv_sem, device_id=(x, y, 1 - c),
                                          device_id_type=MESH)
        cp.start()
        cp.wait()

    return pl.pallas_call(body, name="grad_swap_halves", out_shape=jax.ShapeDtypeStruct((n, Rh, C), g.dtype),
                          in_specs=[ANY], out_specs=ANY,
                          scratch_shapes=[pltpu.SemaphoreType.DMA, pltpu.SemaphoreType.DMA])(g)


def _add_picked(a, b, picks, *, a_row_half=None, out_dtype, name):
    n_out = picks.shape[0]
    _, rows, C = b.shape
    tr = _row_tile(rows, 512)
    nt = rows // tr
    half = jnp.zeros((1,), jnp.int32) if a_row_half is None else a_row_half

    def body(pick_ref, half_ref, a_ref, b_ref, o_ref):
        o_ref[...] = (a_ref[...].astype(F32) + b_ref[...].astype(F32)).astype(o_ref.dtype)

    spec = pltpu.PrefetchScalarGridSpec(
        num_scalar_prefetch=2, grid=(n_out, nt),
        in_specs=[pl.BlockSpec((1, tr, C), lambda j, i, pick, hf: (pick[j], hf[0] * nt + i, 0)),
                  pl.BlockSpec((1, tr, C), lambda j, i, pick, hf: (pick[j], i, 0))],
        out_specs=pl.BlockSpec((1, tr, C), lambda j, i, pick, hf: (j, i, 0)))
    return pl.pallas_call(body, name=name, grid_spec=spec, out_shape=jax.ShapeDtypeStruct((n_out, rows, C), out_dtype),
                          compiler_params=_params(2))(picks.astype(jnp.int32), half.astype(jnp.int32), a, b)


def _hop_exchange(src, hop, *, name):
    def body(s_ref, d_ref, send_sem, recv_sem):
        x, y, c = lax.axis_index("x"), lax.axis_index("y"), lax.axis_index("c")
        cp = pltpu.make_async_remote_copy(src_ref=s_ref, dst_ref=d_ref, send_sem=send_sem, recv_sem=recv_sem,
                                          device_id=(*_hops(x, y, c)[hop], c), device_id_type=MESH)
        cp.start()
        cp.wait()

    return pl.pallas_call(body, name=name, out_shape=jax.ShapeDtypeStruct(src.shape, src.dtype), in_specs=[ANY],
                          out_specs=ANY, scratch_shapes=[pltpu.SemaphoreType.DMA, pltpu.SemaphoreType.DMA])(src)


def _reduce_over_chips(g):
    x, y, c = lax.axis_index("x"), lax.axis_index("y"), lax.axis_index("c")
    n1, n2, _ = _hops(x, y, c)
    chip = lambda p: 2 * p[0] + p[1]
    near = jnp.stack([chip((x, y)), chip(n2)])
    far = jnp.stack([chip(n1), chip((1 - x, 1 - y))])
    half = c.reshape(1)
    sib = _swap_halves(g)
    kept = _add_picked(g, sib, near, a_row_half=half, out_dtype=F32, name="grad_add_near")
    sent = _add_picked(g, sib, far, a_row_half=half, out_dtype=BF16, name="grad_add_far")
    got = _hop_exchange(sent, 0, name="grad_hop_first")
    mine = _add_picked(kept, got, jnp.zeros((1,), jnp.int32), out_dtype=F32, name="grad_add_mine")
    theirs = _add_picked(kept, got, jnp.ones((1,), jnp.int32), out_dtype=BF16, name="grad_add_theirs")
    got = _hop_exchange(theirs, 1, name="grad_hop_second")
    total = _add_picked(mine, got, jnp.zeros((1,), jnp.int32), out_dtype=F32, name="grad_add_total")
    return _join_halves(total[0])


def _sum_slots(b, *, name):
    n, R, C = b.shape
    tr = _row_tile(R, 512)

    def body(b_ref, o_ref):
        acc = b_ref[0]
        for q in range(1, n):
            acc = acc + b_ref[q]
        o_ref[...] = acc

    return pl.pallas_call(body, name=name, grid=(R // tr,), in_specs=[pl.BlockSpec((n, tr, C), lambda i: (0, i, 0))],
                          out_specs=pl.BlockSpec((tr, C), lambda i: (i, 0)), out_shape=jax.ShapeDtypeStruct((R, C), F32),
                          compiler_params=_params(1))(b)


def _join_halves(r):
    Rh, C = r.shape

    def body(r_ref, o_ref, send_sem, recv_sem):
        x, y, c, _ = _place()
        own, other = o_ref.at[pl.ds(c * Rh, Rh), :], o_ref.at[pl.ds((1 - c) * Rh, Rh), :]
        cp = pltpu.make_async_remote_copy(src_ref=r_ref, dst_ref=own, send_sem=send_sem, recv_sem=recv_sem,
                                          device_id=(x, y, 1 - c), device_id_type=MESH)
        cp.start()
        pltpu.make_async_remote_copy(src_ref=r_ref, dst_ref=other, send_sem=send_sem, recv_sem=recv_sem,
                                     device_id=(x, y, 1 - c), device_id_type=MESH).wait_recv()
        cp.wait_send()

    theirs = pl.pallas_call(
        body, name="grad_join_halves", out_shape=jax.ShapeDtypeStruct((2 * Rh, C), r.dtype), in_specs=[ANY], out_specs=ANY,
        scratch_shapes=[pltpu.SemaphoreType.DMA, pltpu.SemaphoreType.DMA])(r)
    return lax.dynamic_update_slice(theirs, r, (lax.axis_index("c") * Rh, 0))


def _size(shape):
    size = 1
    for d in shape:
        size *= d
    return size


def _pack(pieces, cols, row_multiple, dtype):
    if any(p.size % cols for p in pieces):
        flat = jnp.concatenate([p.reshape(-1).astype(dtype) for p in pieces])
        pieces = [jnp.pad(flat, (0, -flat.shape[0] % cols))]
    rows = [p.reshape(-1, cols).astype(dtype) for p in pieces]
    pad = -sum(r.shape[0] for r in rows) % row_multiple
    return jnp.concatenate(rows + ([jnp.zeros((pad, cols), dtype)] if pad else []), axis=0)


def _unpack(buf, shapes):
    cols = buf.shape[1]
    if any(_size(s) % cols for s in shapes):
        flat, out, at = buf.reshape(-1), [], 0
        for shp in shapes:
            out.append(flat[at:at + _size(shp)].reshape(shp))
            at += _size(shp)
        return out
    out, at = [], 0
    for shp in shapes:
        out.append(buf[at:at + _size(shp) // cols].reshape(shp))
        at += _size(shp) // cols
    return out


SHARDED = (("ffn_pre_w_gu", 2), ("ffn_pre_w_down", 1), ("sbg_w_in", 2), ("sbg_w_out", 1), ("mla_w_in", 1),
           ("mla_w_uq", 2), ("mla_w_ukv", 2), ("mla_w_out", 1), ("xmem_wq", 1), ("xmem_wkv", 2), ("xmem_wo", 1),
           ("ffn_post_w_gu", 2), ("ffn_post_w_down", 1))
EARLY = (("ffn_pre_w_gu", 0), ("ffn_pre_w_down", 0), ("sbg_w_in", 0), ("sbg_w_out", 0))
LORA_GAINS = ("mla_q_lora_gain", "mla_kv_lora_gain")
REPLICATED = ("ffn_pre_norm", "mix_norm", "sgu_ln_gain", "sgu_ln_bias", "sgu_w", "sgu_b", "mla_q_gain", "mla_k_gain",
              "xmem_norm", "xmem_mem_norm", "xmem_q_gain", "xmem_k_gain", "ffn_post_norm")
WEIGHTS = ("ffn_pre_norm", "ffn_pre_w_gu", "ffn_pre_w_down", "mix_norm", "sbg_w_in", "sgu_ln_gain", "sgu_ln_bias", "sgu_w",
           "sgu_b", "sbg_w_out", "mla_w_in", "mla_q_lora_gain", "mla_kv_lora_gain", "mla_w_uq", "mla_w_ukv", "mla_q_gain",
           "mla_k_gain", "mla_w_out", "xmem_norm", "xmem_mem_norm", "xmem_wq", "xmem_wkv", "xmem_q_gain", "xmem_k_gain",
           "xmem_wo", "ffn_post_norm", "ffn_post_w_gu", "ffn_post_w_down")
INPUTS = ("x", "mem", "positions") + WEIGHTS + ("loss_target",) + tuple("m_" + n for n in WEIGHTS) + tuple(
    "v_" + n for n in WEIGHTS)


def _step(a):
    x, y, c, _ = _place()
    chip = 2 * x + y
    shard_shapes = [a[n].shape for n, _ in SHARDED]

    w = {n: [None] * a[n].shape[0] for n, _ in SHARDED}
    lots = {early: [(n, l, ax) for n, ax in SHARDED for l in range(a[n].shape[0]) if ((n, l) in EARLY) == early]
            for early in (True, False)}

    def packed(lot):
        return _pack([a[n][l] for n, l, _ in lot], PACK_COLS, PACK_ROW_MULTIPLE, BF16)

    def unpack(gathered, lot):
        at = 0
        for n, l, ax in lot:
            shp = a[n].shape[1:]
            rows = _size(shp) // PACK_COLS
            per_chip = gathered[:, at:at + rows].reshape((N_CHIPS,) + shp)
            at += rows
            w[n][l] = jnp.moveaxis(per_chip, 0, ax - 1).reshape(shp[:ax - 1] + (N_CHIPS * shp[ax - 1],) + shp[ax:])

    unpack(_gather_chips(packed(lots[True])), lots[True])
    gains = jnp.zeros((8, LANE), F32)
    for r, n in enumerate(LORA_GAINS):
        gains = gains.at[r, :a[n].shape[1]].set(a[n][0])
    gains = _gather_devices(gains)
    for r, n in enumerate(LORA_GAINS):
        w[n] = jnp.concatenate([gains[16 * q + r, :a[n].shape[1]] for q in range(N_CHIPS)])[None, :]
    for n in REPLICATED:
        w[n] = a[n]

    loss, dx, grads = _local_step(a["x"][0], a["mem"][0], a["positions"][0], a["loss_target"][0], w,
                                  packed(lots[False]), lambda gathered: unpack(gathered, lots[False]))
    loss = lax.psum(loss, ("x", "y", "c"))
    small_names = REPLICATED + LORA_GAINS
    full = {n: jnp.stack(grads[n]).reshape(w[n].shape) for n in small_names}

    def cut(n, ax, q):
        size = a[n].shape[ax]
        return [lax.slice_in_dim(gl, q * size, (q + 1) * size, axis=ax - 1) for gl in grads[n]]

    g = jnp.stack([_pack([p for n, ax in SHARDED for p in cut(n, ax, q)], PACK_COLS, PACK_ROW_MULTIPLE, F32)
                   for q in range(N_CHIPS)])
    reduced = _reduce_over_chips(g)
    gw = dict(zip([n for n, _ in SHARDED], _unpack(reduced, shard_shapes)))

    small = _pack([full[n] for n in small_names], LANE, 256, F32)
    rows = small.shape[0]
    summed = _sum_slots(_gather_devices(small).reshape(8, rows, LANE), name="grad_sum_devices")
    for n, val in zip(small_names, _unpack(summed, [full[n].shape for n in small_names])):
        if n in LORA_GAINS:
            size = a[n].shape[1]
            val = lax.dynamic_slice_in_dim(val, chip * size, size, axis=1)
        gw[n] = val

    upd = {n: _adamw(a[n], gw[n], a["m_" + n], a["v_" + n], name="adamw_" + n) for n in WEIGHTS}
    return (loss, dx[None], *[gw[n] for n in WEIGHTS], *[upd[n][0] for n in WEIGHTS], *[upd[n][1] for n in WEIGHTS],
            *[upd[n][2] for n in WEIGHTS])


def kernel(x, mem, positions, ffn_pre_norm, ffn_pre_w_gu, ffn_pre_w_down, mix_norm, sbg_w_in, sgu_ln_gain,
           sgu_ln_bias, sgu_w, sgu_b, sbg_w_out, mla_w_in, mla_q_lora_gain, mla_kv_lora_gain, mla_w_uq, mla_w_ukv,
           mla_q_gain, mla_k_gain, mla_w_out, xmem_norm, xmem_mem_norm, xmem_wq, xmem_wkv, xmem_q_gain, xmem_k_gain,
           xmem_wo, ffn_post_norm, ffn_post_w_gu, ffn_post_w_down, loss_target, m_ffn_pre_norm, m_ffn_pre_w_gu,
           m_ffn_pre_w_down, m_mix_norm, m_sbg_w_in, m_sgu_ln_gain, m_sgu_ln_bias, m_sgu_w, m_sgu_b, m_sbg_w_out,
           m_mla_w_in, m_mla_q_lora_gain, m_mla_kv_lora_gain, m_mla_w_uq, m_mla_w_ukv, m_mla_q_gain, m_mla_k_gain,
           m_mla_w_out, m_xmem_norm, m_xmem_mem_norm, m_xmem_wq, m_xmem_wkv, m_xmem_q_gain, m_xmem_k_gain,
           m_xmem_wo, m_ffn_post_norm, m_ffn_post_w_gu, m_ffn_post_w_down, v_ffn_pre_norm, v_ffn_pre_w_gu,
           v_ffn_pre_w_down, v_mix_norm, v_sbg_w_in, v_sgu_ln_gain, v_sgu_ln_bias, v_sgu_w, v_sgu_b, v_sbg_w_out,
           v_mla_w_in, v_mla_q_lora_gain, v_mla_kv_lora_gain, v_mla_w_uq, v_mla_w_ukv, v_mla_q_gain, v_mla_k_gain,
           v_mla_w_out, v_xmem_norm, v_xmem_mem_norm, v_xmem_wq, v_xmem_wkv, v_xmem_q_gain, v_xmem_k_gain,
           v_xmem_wo, v_ffn_post_norm, v_ffn_post_w_gu, v_ffn_post_w_down):
    given = locals()
    return _step({n: given[n] for n in INPUTS})
```

```python
import jax
import jax.numpy as jnp
from jax import lax
from jax.experimental import pallas as pl
from jax.experimental.pallas import tpu as pltpu

F32, BF16 = jnp.float32, jnp.bfloat16
LANE = 128
VMEM_LIMIT = 56 * 1024 * 1024
EPS = 1e-6
SB_HEADS, SB_HD = 8, 64
SG_GROUPS, SG_GD, SG_CHUNK = 8, 64, 128
SB_W, SG_W = SB_HEADS * SB_HD, SG_GROUPS * SG_GD
MLA_HEADS, MLA_NOPE, MLA_ROPE, MLA_V = 16, 64, 32, 64
MLA_QK = MLA_NOPE + MLA_ROPE
MLA_QL, MLA_KVL = 512, 256
ROPE_THETA = 10000.0
MEM_HEADS, MEM_HD = 4, 256
SB_SCALE, MLA_SCALE, MEM_SCALE = SB_HD ** -0.5, MLA_QK ** -0.5, MEM_HD ** -0.5
ADAM_LR, ADAM_B1, ADAM_B2, ADAM_EPS, ADAM_WD, ADAM_STEP = 0.001, 0.9, 0.999, 1e-08, 0.01, 10
MESH = pl.DeviceIdType.MESH
ANY = pl.BlockSpec(memory_space=pl.ANY)


def _params(n_axes):
    return pltpu.CompilerParams(dimension_semantics=("arbitrary",) * n_axes, vmem_limit_bytes=VMEM_LIMIT)


MM_TILE_CAP = 1408
MM_VMEM_BUDGET = 40 * 1024 * 1024


def _tile(dim, cap):
    if dim <= cap:
        return dim
    best = max(t for t in range(LANE, cap + 1, LANE) if dim % t == 0)
    return best


def _k_scratch(count, tile, nk):
    return [pltpu.VMEM(tile, F32)] * count if nk > 1 else []


def _over_k_steps(prods, acc_refs, nk, finish):
    if nk == 1:
        finish(prods)
        return
    kk = pl.program_id(2)

    @pl.when(kk == 0)
    def _():
        for ref, p in zip(acc_refs, prods):
            ref[...] = p

    @pl.when(kk > 0)
    def _():
        for ref, p in zip(acc_refs, prods):
            ref[...] += p

    @pl.when(kk == nk - 1)
    def _():
        finish([ref[...] for ref in acc_refs])


def _mm(a, b, *, ta=False, tb=False, out_dtype=F32, scale=1.0, residual=None, bias=None, norm_bwd=None, a_off=(0, 0),
        b_off=(0, 0), m=None, n=None, k=None, name):
    am, ak = (a.shape[1], a.shape[0]) if ta else a.shape
    bk, bn = (b.shape[1], b.shape[0]) if tb else b.shape
    M, N, K = m or am, n or bn, k or ak
    tm, tn = _tile(M, MM_TILE_CAP if norm_bwd is None else MM_TILE_CAP // 2), _tile(N, MM_TILE_CAP)
    n_full = (residual is not None) + (2 if norm_bwd is not None else 0)
    fixed = tm * tn * (4 + 2 * jnp.dtype(out_dtype).itemsize + 8 * n_full)
    per_k = (tm * (2 * a.dtype.itemsize + 2) + tn * (2 * b.dtype.itemsize + 2))
    tk = _tile(K, max(LANE, (MM_VMEM_BUDGET - fixed) // per_k))
    nm, nn, nk = M // tm, N // tn, K // tk
    assert norm_bwd is None or nn == 1
    a_off = (a_off[0] // (tk if ta else tm), a_off[1] // (tm if ta else tk))
    b_off = (b_off[0] // (tn if tb else tk), b_off[1] // (tk if tb else tn))
    dims = (((0 if ta else 1,), (1 if tb else 0,)), ((), ()))
    n_out = 1 if norm_bwd is None else 2

    def body(*refs):
        a_ref, b_ref = refs[0], refs[1]
        n_in = len(ins)
        o_ref, extras, acc_refs = refs[n_in], refs[2:n_in], refs[n_in + n_out:]
        first_rows = pl.program_id(0) == 0

        def finish(total):
            out = total * scale
            for extra in (extras if norm_bwd is None else extras[:-3]):
                out = out + extra[...].astype(F32)
            if norm_bwd is not None:
                x_ref, g_ref, dres_ref = extras[-3:]
                dg_ref = refs[n_in + 1]
                dx, dg = _rmsnorm_bwd(out, x_ref[...], g_ref[...])
                out = dx + dres_ref[...]

                @pl.when(first_rows)
                def _():
                    dg_ref[...] = jnp.zeros_like(dg_ref)

                dg_ref[...] += dg
            o_ref[...] = out.astype(o_ref.dtype)

        prod = lax.dot_general(a_ref[...].astype(BF16), b_ref[...].astype(BF16), dims, preferred_element_type=F32)
        _over_k_steps([prod], acc_refs, nk, lambda totals: finish(totals[0]))

    (ao0, ao1), (bo0, bo1) = a_off, b_off
    a_spec = (pl.BlockSpec((tk, tm), lambda i, j, kk: (kk + ao0, i + ao1)) if ta
              else pl.BlockSpec((tm, tk), lambda i, j, kk: (i + ao0, kk + ao1)))
    b_spec = (pl.BlockSpec((tn, tk), lambda i, j, kk: (j + bo0, kk + bo1)) if tb
              else pl.BlockSpec((tk, tn), lambda i, j, kk: (kk + bo0, j + bo1)))
    o_spec = pl.BlockSpec((tm, tn), lambda i, j, kk: (i, j))
    ins, in_specs = [a, b], [a_spec, b_spec]
    if residual is not None:
        ins.append(residual)
        in_specs.append(o_spec)
    if bias is not None:
        ins.append(bias)
        in_specs.append(pl.BlockSpec((1, tn), lambda i, j, kk: (0, j)))
    out_specs, out_shape = o_spec, jax.ShapeDtypeStruct((M, N), out_dtype)
    if norm_bwd is not None:
        x, gain, dres = norm_bwd
        row = pl.BlockSpec((1, tn), lambda i, j, kk: (0, 0))
        ins += [x, gain.reshape(1, N), dres]
        in_specs += [o_spec, row, o_spec]
        out_specs, out_shape = [o_spec, row], [out_shape, jax.ShapeDtypeStruct((1, N), F32)]
    return pl.pallas_call(
        body, name=name, grid=(nm, nn, nk), in_specs=in_specs, out_specs=out_specs, out_shape=out_shape,
        scratch_shapes=_k_scratch(1, (tm, tn), nk), compiler_params=_params(3))(*ins)


def _mm_swiglu(h, wgu, *, name):
    M, K = h.shape
    F = wgu.shape[1] // 2
    tm, tn, tk = _tile(M, 512), _tile(F, MM_TILE_CAP), _tile(K, 1024)
    nm, nf, nk = M // tm, F // tn, K // tk

    def body(h_ref, wg_ref, wu_ref, g_ref, u_ref, a_ref, *acc_refs):
        def finish(totals):
            g, u = totals
            g_ref[...] = g.astype(BF16)
            u_ref[...] = u.astype(BF16)
            a_ref[...] = (g * jax.nn.sigmoid(g) * u).astype(BF16)

        hb = h_ref[...]
        _over_k_steps([jnp.dot(hb, wg_ref[...], preferred_element_type=F32),
                       jnp.dot(hb, wu_ref[...], preferred_element_type=F32)], acc_refs, nk, finish)

    o_spec = pl.BlockSpec((tm, tn), lambda j, i, kk: (i, j))
    shp = jax.ShapeDtypeStruct((M, F), BF16)
    return pl.pallas_call(
        body, name=name, grid=(nf, nm, nk),
        in_specs=[pl.BlockSpec((tm, tk), lambda j, i, kk: (i, kk)),
                  pl.BlockSpec((tk, tn), lambda j, i, kk: (kk, j)),
                  pl.BlockSpec((tk, tn), lambda j, i, kk: (kk, j + nf))],
        out_specs=[o_spec, o_spec, o_spec], out_shape=[shp, shp, shp],
        scratch_shapes=_k_scratch(2, (tm, tn), nk), compiler_params=_params(3))(h, wgu, wgu)


def _mm_dswiglu(dy, wd, gate, up, *, scale, name):
    M, K = dy.shape
    F = wd.shape[0]
    tm, tn, tk = _tile(M, 512), _tile(F, MM_TILE_CAP), _tile(K, 1024)
    nm, nf, nk = M // tm, F // tn, K // tk

    def body(dy_ref, wd_ref, g_ref, u_ref, dg_ref, du_ref, *acc_refs):
        def finish(totals):
            da = totals[0] * scale
            g, u = g_ref[...].astype(F32), u_ref[...].astype(F32)
            sg = jax.nn.sigmoid(g)
            du_ref[...] = (da * g * sg).astype(BF16)
            dg_ref[...] = (da * u * sg * (1.0 + g * (1.0 - sg))).astype(BF16)

        _over_k_steps([_nt(dy_ref[...].astype(BF16), wd_ref[...])], acc_refs, nk, finish)

    o_spec = pl.BlockSpec((tm, tn), lambda j, i, kk: (i, j))
    shp = jax.ShapeDtypeStruct((M, F), BF16)
    return pl.pallas_call(
        body, name=name, grid=(nf, nm, nk),
        in_specs=[pl.BlockSpec((tm, tk), lambda j, i, kk: (i, kk)),
                  pl.BlockSpec((tn, tk), lambda j, i, kk: (j, kk)), o_spec, o_spec],
        out_specs=[o_spec, o_spec], out_shape=[shp, shp],
        scratch_shapes=_k_scratch(1, (tm, tn), nk), compiler_params=_params(3))(dy, wd, gate, up)


HEAD_ROWS = 1024


def _row_tile(rows, cap):
    t = cap
    while t >= 8:
        if rows % t == 0:
            return t
        t //= 2
    return rows


def _rowwise(fn, rows, consts, outs, sums=(), hsums=(), *, heads=None, tm=256, name):
    rows = [r if isinstance(r, tuple) else (r, r.shape[1], None) for r in rows]
    rows = [r if len(r) == 4 else (*r, False) for r in rows]
    S = rows[0][0].shape[0]
    tm = _row_tile(S, tm)
    nh = heads or 1
    n_r, n_c, n_o, n_h, n_s = len(rows), len(consts), len(outs), len(hsums), len(sums)

    def body(*refs):
        r = [x[...].T if row[3] else x[...] for x, row in zip(refs, rows)]
        c = [x[...] for x in refs[n_r:n_r + n_c]]
        o_refs = refs[n_r + n_c:n_r + n_c + n_o]
        h_refs = refs[n_r + n_c + n_o:n_r + n_c + n_o + n_h]
        s_refs = refs[n_r + n_c + n_o + n_h:]
        res = fn(*r, *c)
        res = res if isinstance(res, (tuple, list)) else (res,)
        for ref, val in zip(o_refs, res[:n_o]):
            ref[...] = val.astype(ref.dtype)
        if n_h:
            @pl.when(pl.program_id(1) == 0)
            def _():
                for ref in h_refs:
                    ref[...] = jnp.zeros_like(ref)
            for ref, val in zip(h_refs, res[n_o:n_o + n_h]):
                ref[...] += val
        if n_s:
            @pl.when((pl.program_id(0) == 0) & (pl.program_id(1) == 0))
            def _():
                for ref in s_refs:
                    ref[...] = jnp.zeros_like(ref)
            for ref, val in zip(s_refs, res[n_o + n_h:]):
                ref[...] += val

    def col(colfn):
        return (lambda i, h: (i, 0)) if colfn is None else (lambda i, h: (i, colfn(h)))

    in_specs = [pl.BlockSpec((w, tm), lambda i, h, cf=cf: (cf(h), i)) if flipped else pl.BlockSpec((tm, w), col(cf))
                for _, w, cf, flipped in rows]
    in_specs += [pl.BlockSpec(a.shape, lambda i, h, nd=a.ndim: (0,) * nd) for a in consts]
    out_specs = [pl.BlockSpec((tm, w // nh), (lambda i, h: (i, h)) if heads else (lambda i, h: (i, 0))) for w, _ in outs]
    out_specs += [pl.BlockSpec((tm, w), lambda i, h: (i, 0)) for w in hsums]
    out_specs += [pl.BlockSpec(sh, lambda i, h, nd=len(sh): (0,) * nd) for sh in sums]
    out_shape = [jax.ShapeDtypeStruct((S, w), dt) for w, dt in outs]
    out_shape += [jax.ShapeDtypeStruct((S, w), F32) for w in hsums]
    out_shape += [jax.ShapeDtypeStruct(sh, F32) for sh in sums]
    return pl.pallas_call(body, name=name, grid=(S // tm, nh), in_specs=in_specs, out_specs=out_specs,
                          out_shape=out_shape, compiler_params=_params(2))(*[row[0] for row in rows], *consts)


def _rms(x, width=None):
    width = width or x.shape[-1]
    return lax.rsqrt(jnp.sum(x * x, axis=-1, keepdims=True) * (1.0 / width) + EPS)


def _rmsnorm_fwd(x, g, width=None):
    return x * _rms(x, width) * g


def _rmsnorm_bwd(dy, x, g, width=None):
    width = width or x.shape[-1]
    r = _rms(x, width)
    xn = x * r
    dxn = dy * g
    dx = r * (dxn - xn * (jnp.sum(dxn * xn, axis=-1, keepdims=True) * (1.0 / width)))
    return dx, jnp.sum(dy * xn, axis=0, keepdims=True)


def _norm_rows(x, g, *, name, out_dtype=BF16):
    D = x.shape[1]
    return _rowwise(lambda xv, gv: _rmsnorm_fwd(xv.astype(F32), gv), [x], [g.reshape(1, D)], [(D, out_dtype)],
                    name=name)[0]


def _norm_rows_bwd(dh, x, g, dres, *, name):
    D = x.shape[1]

    def fn(dhv, xv, *rest):
        dx, dg = _rmsnorm_bwd(dhv.astype(F32), xv, rest[-1])
        return (dx + rest[0] if dres is not None else dx), dg

    rows = [dh, x] + ([dres] if dres is not None else [])
    return _rowwise(fn, rows, [g.reshape(1, D)], [(D, F32)], [(1, D)], name=name)


def _softplus(z):
    return jnp.where(z > 20.0, z, jnp.log(1.0 + jnp.exp(z)))


def _running_sum(v, u, split=True):
    if not split:
        return jnp.dot(v.astype(BF16), u, preferred_element_type=F32)
    hi = lax.bitcast_convert_type(lax.bitcast_convert_type(v, jnp.uint32) & jnp.uint32(0xFFFF0000), F32)
    return (jnp.dot(hi.astype(BF16), u, preferred_element_type=F32)
            + jnp.dot((v - hi).astype(BF16), u, preferred_element_type=F32))


def _triangle(tk, inclusive_prefix):
    j, s = lax.broadcasted_iota(jnp.int32, (tk, tk), 0), lax.broadcasted_iota(jnp.int32, (tk, tk), 1)
    return ((j <= s) if inclusive_prefix else (j > s)).astype(BF16)


def _nt(a, b):
    return lax.dot_general(a, b, (((1,), (1,)), ((), ())), preferred_element_type=F32)


def _tn(a, b):
    return lax.dot_general(a, b, (((0,), (0,)), ((), ())), preferred_element_type=F32)


ATT_TQ, ATT_TK = 512, 512
SB_SUB = 256
FWD_GROUP = 2


def _attn_fwd(q, k, v, *, sb, causal, heads, dq, dv, group=1, kcol=None, vcol=None, sum_lane=None, side_gather=None,
              name):
    S, Sk = q.shape[0], k.shape[0]
    tq, tk = min(ATT_TQ, S), min(ATT_TK, Sk)
    sub = min(SB_SUB, tk) if sb else tk
    assert tq % sub == 0 or not causal
    kcol = kcol or (lambda h: h)
    vcol = vcol or (lambda h: h)
    members = range(group)
    assert side_gather is None or heads // group >= GATHER_STAGES

    def body(*refs):
        if side_gather is not None:
            n_in = 4 if sb else 3
            stages = _gather_stages(refs[n_in], refs[n_in + 3], *refs[-2:])
            for n, stage in enumerate(stages):
                pl.when((pl.program_id(0) == n) & (pl.program_id(1) == 0))(stage)
            refs = refs[:n_in] + refs[n_in + 1:n_in + 3] + refs[n_in + 4:-2]
        if sb:
            q_ref, k_ref, v_ref, u_ref, o_ref, lse_ref, acc_ref, r_ref = refs
            r_ref[...] = jnp.zeros_like(r_ref)
        else:
            q_ref, k_ref, v_ref, o_ref, lse_ref, acc_ref, m_ref, l_ref = refs
            m_ref[...] = jnp.full_like(m_ref, -1e30)
            l_ref[...] = jnp.zeros_like(l_ref)
        first_row = pl.program_id(1) * tq
        qb = [q_ref[:, hh * dq:(hh + 1) * dq] for hh in members]
        acc_ref[...] = jnp.zeros_like(acc_ref)
        nblk = (first_row + tq) // sub if causal else Sk // sub
        nfull = (first_row + (0 if sb else 1)) // sub if causal else nblk
        n_cut = tq // sub if causal else 0

        def scores(jj):
            off = pl.multiple_of(jj * sub, sub)
            return tuple(_nt(qb[hh], k_ref[pl.ds(off, sub), hh * dq:(hh + 1) * dq]) for hh in members)

        def weigh(jj, scores_now, masked):
            off = pl.multiple_of(jj * sub, sub)
            if masked:
                kpos = off + lax.broadcasted_iota(jnp.int32, (tq, sub), 1)
                qpos = first_row + lax.broadcasted_iota(jnp.int32, (tq, sub), 0)
                valid = (kpos < qpos) if sb else (kpos <= qpos)
            for hh in members:
                vb = v_ref[pl.ds(off, sub), hh * dv:(hh + 1) * dv]
                s = scores_now[hh]
                if sb:
                    sp = _softplus(s)
                    ls = jnp.where(valid, -sp, 0.0) if masked else -sp
                    w = jnp.exp(s - sp + r_ref[hh] + _running_sum(ls, u_ref[...]))
                    if masked:
                        w = jnp.where(valid, w, 0.0)
                    acc_ref[hh] += jnp.dot(w.astype(BF16), vb, preferred_element_type=F32)
                    r_ref[hh] += jnp.sum(ls, axis=1, keepdims=True)
                else:
                    if masked:
                        s = jnp.where(valid, s, -1e30)
                    m_old = m_ref[hh]
                    m_new = jnp.maximum(m_old, jnp.max(s, axis=1, keepdims=True))
                    p = jnp.exp(s - m_new)
                    alpha = jnp.exp(m_old - m_new)
                    if sum_lane is None:
                        l_ref[hh] = alpha * l_ref[hh] + jnp.sum(p, axis=1, keepdims=True)
                    acc_ref[hh] = alpha * acc_ref[hh] + jnp.dot(p.astype(BF16), vb, preferred_element_type=F32)
                    m_ref[hh] = m_new

        if sb:
            s_cur = scores(nblk - 1)
            for cut in range(n_cut):
                s_next = scores(jnp.maximum(nblk - 2 - cut, 0))
                weigh(nblk - 1 - cut, s_cur, True)
                s_cur = s_next

            def step(t, s_now):
                s_next = scores(jnp.maximum(nfull - 2 - t, 0))
                weigh(nfull - 1 - t, s_now, False)
                return s_next

            lax.fori_loop(0, nfull, step, s_cur)
        else:
            n_loop = nfull if causal else nblk - 1

            def step(t, s_now):
                s_next = scores(jnp.minimum(t + 1, nblk - 1))
                weigh(t, s_now, False)
                return s_next

            s_cur = lax.fori_loop(0, n_loop, step, scores(0))
            tail = n_cut if causal else 1
            for last in range(tail):
                s_next = scores(n_loop + last + 1) if last + 1 < tail else None
                weigh(n_loop + last, s_cur, causal)
                s_cur = s_next
        for hh in members:
            cols = slice(hh * dv, (hh + 1) * dv)
            if sb:
                o_ref[:, cols] = acc_ref[hh]
                lse_ref[hh] = r_ref[hh]
            else:
                acc = acc_ref[hh]
                l = l_ref[hh] if sum_lane is None else acc[:, sum_lane:sum_lane + 1]
                o_ref[:, cols] = acc / l
                lse_ref[hh] = m_ref[hh] + jnp.log(l)

    in_specs = [pl.BlockSpec((tq, group * dq), lambda g, i: (i, g)),
                pl.BlockSpec((Sk, group * dq), lambda g, i: (0, kcol(g))),
                pl.BlockSpec((Sk, group * dv), lambda g, i: (0, vcol(g)))]
    ins = [q, k, v]
    scratch = [pltpu.VMEM((group, tq, dv), F32), pltpu.VMEM((group, tq, 1), F32)]
    if sb:
        ins.append(_triangle(sub, inclusive_prefix=False))
        in_specs.append(pl.BlockSpec((sub, sub), lambda g, i: (0, 0)))
    else:
        scratch.append(pltpu.VMEM((group, tq, 1), F32))
    out_specs = [pl.BlockSpec((tq, group * dv), lambda g, i: (i, g)), pl.BlockSpec((group, tq, 1), lambda g, i: (g, i, 0))]
    out_shape = [jax.ShapeDtypeStruct((S, heads * dv), F32), jax.ShapeDtypeStruct((heads, S, 1), F32)]
    if side_gather is not None:
        ins.append(side_gather)
        in_specs.append(ANY)
        out_specs.append(ANY)
        out_shape.append(jax.ShapeDtypeStruct((N_CHIPS,) + side_gather.shape, side_gather.dtype))
        scratch += [pltpu.SemaphoreType.DMA((GATHER_COPIES,)), pltpu.SemaphoreType.DMA((GATHER_COPIES,))]
    outs = pl.pallas_call(body, name=name, grid=(heads // group, S // tq), in_specs=in_specs, out_specs=out_specs,
                          out_shape=out_shape, scratch_shapes=scratch, compiler_params=_params(2))(*ins)
    return outs if side_gather is None else (outs[0], outs[1], _place_own_slot(outs[2], side_gather))


def _attn_bwd(q, k, v, o, do, lse, *, sb, causal, heads, dq, dv, kcol=None, vcol=None, side_exchange=None, name):
    S, Sk = q.shape[0], k.shape[0]
    tq, tk = min(ATT_TQ, S), min(ATT_TK, Sk)
    sub = min(SB_SUB, tk) if sb else tk
    assert tq % sub == 0 or not causal
    nq = S // tq
    kcol = kcol or (lambda h: h)
    vcol = vcol or (lambda h: h)
    n_in = 7 if sb else 6

    def body(*refs):
        if side_exchange is not None:
            start, finish = _partials_exchange(refs[n_in], refs[n_in + 4], *refs[-2:])
            pl.when((pl.program_id(0) == 0) & (pl.program_id(1) == 0))(start)
            pl.when((pl.program_id(0) == heads - 1) & (pl.program_id(1) == 0))(finish)
            refs = refs[:n_in] + refs[n_in + 1:n_in + 4] + refs[n_in + 5:-2]
        if sb:
            q_ref, k_ref, v_ref, o_ref, do_ref, lse_ref, u_ref, dq_ref, dk_ref, dv_ref, acc_ref, r_ref, re_ref = refs
            r_ref[...] = jnp.zeros_like(r_ref)
            re_ref[...] = jnp.zeros_like(re_ref)
        else:
            q_ref, k_ref, v_ref, o_ref, do_ref, lse_ref, dq_ref, dk_ref, dv_ref, acc_ref = refs
        first_row = pl.program_id(1) * tq

        @pl.when(first_row == 0)
        def _():
            dk_ref[...] = jnp.zeros_like(dk_ref)
            dv_ref[...] = jnp.zeros_like(dv_ref)

        qb = q_ref[...]
        dof = do_ref[...].astype(F32)
        dob = dof.astype(BF16)
        q_t, do_t = qb.T, dob.T
        if not sb:
            dlt = jnp.sum(dof * o_ref[...], axis=1, keepdims=True)
        acc_ref[...] = jnp.zeros_like(acc_ref)
        nblk = (first_row + tq) // sub if causal else Sk // sub
        nfull = (first_row + (0 if sb else 1)) // sub if causal else nblk
        n_cut = tq // sub if causal else 0

        def products(jj):
            off = pl.multiple_of(jj * sub, sub)
            return _nt(qb, k_ref[pl.ds(off, sub), :]), _nt(dob, v_ref[pl.ds(off, sub), :])

        def piece(jj, now, masked):
            off = pl.multiple_of(jj * sub, sub)
            kb = k_ref[pl.ds(off, sub), :]
            s, dp = now
            if masked:
                qpos = first_row + lax.broadcasted_iota(jnp.int32, (tq, sub), 0)
                kpos = off + lax.broadcasted_iota(jnp.int32, (tq, sub), 1)
                valid = (kpos < qpos) if sb else (kpos <= qpos)
            if sb:
                u = u_ref[...]
                sp = _softplus(s)
                ls = jnp.where(valid, -sp, 0.0) if masked else -sp
                lb = s - sp
                w = jnp.exp(lb + (lse_ref[0] - (r_ref[...] + _running_sum(ls, u))))
                if masked:
                    w = jnp.where(valid, w, 0.0)
                e = dp * w
                ds = e - jnp.exp(lb) * (re_ref[...] + _running_sum(e, u, split=False))
                if masked:
                    ds = jnp.where(valid, ds, 0.0)
                r_ref[...] += jnp.sum(ls, axis=1, keepdims=True)
                re_ref[...] += jnp.sum(e, axis=1, keepdims=True)
            else:
                w = jnp.exp(s - lse_ref[0])
                if masked:
                    w = jnp.where(valid, w, 0.0)
                ds = w * (dp - dlt)
            dsb = ds.astype(BF16)
            dv_ref[:, pl.ds(off, sub)] += jnp.dot(do_t, w.astype(BF16), preferred_element_type=F32)
            dk_ref[:, pl.ds(off, sub)] += jnp.dot(q_t, dsb, preferred_element_type=F32)
            acc_ref[...] += jnp.dot(dsb, kb, preferred_element_type=F32)

        n_loop = nfull if causal else nblk - 1
        per_trip = tk // sub

        def steps(first, count, masked):
            ready = [products(first + c) for c in range(count)]
            for c in range(count):
                piece(first + c, ready[c], masked)

        def trip(t, carry):
            steps(t * per_trip, per_trip, False)
            return carry

        lax.fori_loop(0, n_loop // per_trip, trip, 0)
        steps(n_loop, n_cut if causal else 1, causal)
        dq_ref[...] = acc_ref[...]

    ins = [q, k, v, o, do]
    in_specs = [pl.BlockSpec((tq, dq), lambda h, i: (i, h)),
                pl.BlockSpec((Sk, dq), lambda h, i: (0, kcol(h))),
                pl.BlockSpec((Sk, dv), lambda h, i: (0, vcol(h))),
                pl.BlockSpec((tq, dv), lambda h, i: (i, h)),
                pl.BlockSpec((tq, dv), lambda h, i: (i, h))]
    scratch = [pltpu.VMEM((tq, dq), F32)]
    ins.append(lse)
    in_specs.append(pl.BlockSpec((1, tq, 1), lambda h, i: (h, i, 0)))
    if sb:
        ins.append(_triangle(sub, inclusive_prefix=True))
        in_specs.append(pl.BlockSpec((sub, sub), lambda h, i: (0, 0)))
        scratch += [pltpu.VMEM((tq, 1), F32), pltpu.VMEM((tq, 1), F32)]
    out_specs = [pl.BlockSpec((tq, dq), lambda h, i: (i, h)),
                 pl.BlockSpec((dq, Sk), lambda h, i: (h, 0)),
                 pl.BlockSpec((dv, Sk), lambda h, i: (h, 0))]
    out_shape = [jax.ShapeDtypeStruct((S, heads * dq), F32), jax.ShapeDtypeStruct((heads * dq, Sk), F32),
                 jax.ShapeDtypeStruct((heads * dv, Sk), F32)]
    if side_exchange is not None:
        _, R, C = side_exchange.shape
        ins.append(side_exchange)
        in_specs.append(ANY)
        out_specs.append(ANY)
        out_shape.append(jax.ShapeDtypeStruct((N_DEVICES, R // 2, C), side_exchange.dtype))
        scratch += [pltpu.SemaphoreType.DMA((N_DEVICES,)), pltpu.SemaphoreType.DMA((N_DEVICES,))]
    return pl.pallas_call(body, name=name, grid=(heads, nq), in_specs=in_specs, out_specs=out_specs,
                          out_shape=out_shape, scratch_shapes=scratch, compiler_params=_params(2))(*ins)


GELU_C = 0.7978845608028654
assert 2 * SG_GD == LANE and SG_CHUNK == LANE


def _gelu(z):
    t = jnp.tanh(GELU_C * (z + 0.044715 * z * z * z))
    return 0.5 * z * (1.0 + t), t


def _gelu_grad(z, t):
    return 0.5 * (1.0 + t) + 0.5 * z * (1.0 - t * t) * GELU_C * (1.0 + 3.0 * 0.044715 * z * z)


def _layernorm_parts(g):
    d = g - jnp.mean(g, axis=-1, keepdims=True)
    rstd = lax.rsqrt(jnp.mean(d * d, axis=-1, keepdims=True) + EPS)
    return d * rstd, rstd


def _gelu_ln(z, gain, bias, *, name):
    def fn(zv, gn, bs):
        a, _ = _gelu(zv)
        y, _ = _layernorm_parts(a[:, SG_W:])
        return a[:, :SG_W], y * gn + bs

    return _rowwise(fn, [z], [gain.reshape(1, SG_W), bias.reshape(1, SG_W)], [(SG_W, F32), (SG_W, BF16)], name=name)


def _gelu_ln_bwd(z, du, dgl, gain, *, name):
    def fn(zv, duv, dglv, gn):
        a, t = _gelu(zv)
        y, rstd = _layernorm_parts(a[:, SG_W:])
        dy = dglv * gn
        dgg = rstd * (dy - jnp.mean(dy, axis=-1, keepdims=True) - y * jnp.mean(dy * y, axis=-1, keepdims=True))
        dz = jnp.concatenate([duv, dgg], axis=1) * _gelu_grad(zv, t)
        return dz, jnp.sum(dglv * y, axis=0, keepdims=True), jnp.sum(dglv, axis=0, keepdims=True)

    return _rowwise(fn, [z, du, dgl], [gain.reshape(1, SG_W)], [(2 * SG_W, BF16)], [(1, SG_W), (1, SG_W)], name=name)


def _sg_masks():
    tri = lax.broadcasted_iota(jnp.int32, (SG_CHUNK, SG_CHUNK), 0) >= lax.broadcasted_iota(jnp.int32, (SG_CHUNK, SG_CHUNK), 1)
    first = lax.broadcasted_iota(jnp.int32, (SG_CHUNK, LANE), 1) < SG_GD
    return tri, first


def _spatial(gl, u, w, bt, *, name):
    S = gl.shape[0]
    tm = _row_tile(S, 512)
    nch = tm // SG_CHUNK

    def body(gl_ref, u_ref, w_ref, bt_ref, o_ref):
        tri, first = _sg_masks()
        for p in range(SG_W // LANE):
            cols = slice(p * LANE, (p + 1) * LANE)
            wa = jnp.where(tri, w_ref[2 * p], 0.0).astype(BF16)
            wb = jnp.where(tri, w_ref[2 * p + 1], 0.0).astype(BF16)
            for ci in range(nch):
                rws = slice(ci * SG_CHUNK, (ci + 1) * SG_CHUNK)
                g = gl_ref[rws, cols]
                zero = jnp.zeros_like(g)
                mixed = (jnp.dot(wa, jnp.where(first, g, zero), preferred_element_type=F32)
                         + jnp.dot(wb, jnp.where(first, zero, g), preferred_element_type=F32) + bt_ref[:, cols])
                o_ref[rws, cols] = u_ref[rws, cols] * mixed

    row = pl.BlockSpec((tm, SG_W), lambda i: (i, 0))
    return pl.pallas_call(
        body, name=name, grid=(S // tm,),
        in_specs=[row, row, pl.BlockSpec(w.shape, lambda i: (0, 0, 0)), pl.BlockSpec(bt.shape, lambda i: (0, 0))],
        out_specs=row, out_shape=jax.ShapeDtypeStruct((S, SG_W), F32), compiler_params=_params(1))(gl, u, w, bt)


def _spatial_bwd(d_o, gl, u, w, bt, *, name):
    S = gl.shape[0]
    tm = _row_tile(S, 512)
    nch = tm // SG_CHUNK
    nsteps = S // tm

    def body(do_ref, gl_ref, u_ref, w_ref, bt_ref, du_ref, dgl_ref, dw_ref, db_ref, dbt_ref):
        tri, first = _sg_masks()
        step = pl.program_id(0)

        @pl.when(step == 0)
        def _():
            dw_ref[...] = jnp.zeros_like(dw_ref)
            dbt_ref[...] = jnp.zeros_like(dbt_ref)

        for p in range(SG_W // LANE):
            cols = slice(p * LANE, (p + 1) * LANE)
            wa = jnp.where(tri, w_ref[2 * p], 0.0).astype(BF16)
            wb = jnp.where(tri, w_ref[2 * p + 1], 0.0).astype(BF16)
            for ci in range(nch):
                rws = slice(ci * SG_CHUNK, (ci + 1) * SG_CHUNK)
                g = gl_ref[rws, cols]
                zero = jnp.zeros_like(g)
                mixed = (jnp.dot(wa, jnp.where(first, g, zero), preferred_element_type=F32)
                         + jnp.dot(wb, jnp.where(first, zero, g), preferred_element_type=F32) + bt_ref[:, cols])
                dov = do_ref[rws, cols]
                du_ref[rws, cols] = dov * mixed
                dm = dov * u_ref[rws, cols]
                dbt_ref[:, cols] += dm
                dma = jnp.where(first, dm, 0.0).astype(BF16)
                dmb = jnp.where(first, 0.0, dm).astype(BF16)
                dw_ref[2 * p] += jnp.where(tri, _nt(dma, g), 0.0)
                dw_ref[2 * p + 1] += jnp.where(tri, _nt(dmb, g), 0.0)
                dgl_ref[rws, cols] = _tn(wa, dma) + _tn(wb, dmb)

        @pl.when(step == nsteps - 1)
        def _():
            lane = lax.broadcasted_iota(jnp.int32, (SG_CHUNK, LANE), 1)
            acc = jnp.zeros((SG_CHUNK, LANE), F32)
            for p in range(SG_W // LANE):
                blk = dbt_ref[:, p * LANE:(p + 1) * LANE]
                sa = jnp.sum(jnp.where(first, blk, 0.0), axis=1, keepdims=True)
                sb_ = jnp.sum(jnp.where(first, 0.0, blk), axis=1, keepdims=True)
                acc = acc + jnp.where(lane == 2 * p, sa, 0.0) + jnp.where(lane == 2 * p + 1, sb_, 0.0)
            db_ref[...] = acc

    row = pl.BlockSpec((tm, SG_W), lambda i: (i, 0))
    return pl.pallas_call(
        body, name=name, grid=(nsteps,),
        in_specs=[row, row, row, pl.BlockSpec(w.shape, lambda i: (0, 0, 0)), pl.BlockSpec(bt.shape, lambda i: (0, 0))],
        out_specs=[row, row, pl.BlockSpec(w.shape, lambda i: (0, 0, 0)), pl.BlockSpec((SG_CHUNK, LANE), lambda i: (0, 0))],
        out_shape=[jax.ShapeDtypeStruct((S, SG_W), F32), jax.ShapeDtypeStruct((S, SG_W), F32),
                   jax.ShapeDtypeStruct(w.shape, F32), jax.ShapeDtypeStruct((SG_CHUNK, LANE), F32)],
        scratch_shapes=[pltpu.VMEM((SG_CHUNK, SG_W), F32)], compiler_params=_params(1))(d_o, gl, u, w, bt)


ROPE_HALF = MLA_ROPE // 2
KR_COL = (MLA_QL + MLA_KVL) // LANE
MLA_IN_PAD = MLA_QL + MLA_KVL + LANE


def _rope_tables(positions):
    inv_freq = ROPE_THETA ** (-jnp.arange(ROPE_HALF, dtype=F32) / ROPE_HALF)
    ang = positions.astype(F32)[:, None] * inv_freq
    cos, sin = jnp.cos(ang), jnp.sin(ang)
    S = positions.shape[0]
    z16, tail = jnp.zeros((S, ROPE_HALF), F32), jnp.zeros((S, LANE - MLA_QK), F32)
    ones = jnp.ones((S, MLA_NOPE), F32)
    zeros = jnp.zeros((S, MLA_NOPE), F32)
    return (jnp.concatenate([ones, cos, cos, tail], axis=1), jnp.concatenate([zeros, z16, sin, tail], axis=1),
            jnp.concatenate([zeros, -sin, z16, tail], axis=1))


def _rope(x, cos, sa, sb):
    return x * cos + pltpu.roll(x, ROPE_HALF, 1) * sa + pltpu.roll(x, LANE - ROPE_HALF, 1) * sb


def _rope_t(dy, cos, sa, sb):
    return dy * cos + pltpu.roll(dy * sa, LANE - ROPE_HALF, 1) + pltpu.roll(dy * sb, ROPE_HALF, 1)


def _mla_lora(P, qlg, kvlg, *, name):
    def fn(pv, a, b):
        return _rmsnorm_fwd(pv[:, :MLA_QL], a), _rmsnorm_fwd(pv[:, MLA_QL:MLA_QL + MLA_KVL], b)

    return _rowwise(fn, [P], [qlg.reshape(1, MLA_QL), kvlg.reshape(1, MLA_KVL)], [(MLA_QL, BF16), (MLA_KVL, BF16)], name=name)


def _mla_lora_bwd(dcq, dckv, dkr, P, qlg, kvlg, *, name):
    def fn(d1, d2, d3, pv, a, b):
        x1, g1 = _rmsnorm_bwd(d1, pv[:, :MLA_QL], a)
        x2, g2 = _rmsnorm_bwd(d2, pv[:, MLA_QL:MLA_QL + MLA_KVL], b)
        return jnp.concatenate([x1, x2, d3], axis=1), g1, g2

    return _rowwise(fn, [dcq, dckv, dkr, P], [qlg.reshape(1, MLA_QL), kvlg.reshape(1, MLA_KVL)], [(MLA_IN_PAD, BF16)],
                    [(1, MLA_QL), (1, MLA_KVL)], name=name)


def _mla_qk(q_pre, k_pre, P, tabs, qg, kg, *, name):
    def fn(qp, kp, kr, c, a, b, qgv, kgv):
        return (_rope(_rmsnorm_fwd(qp, qgv, MLA_QK), c, a, b) * MLA_SCALE,
                _rope(_rmsnorm_fwd(kp + kr, kgv, MLA_QK), c, a, b))

    hcol = lambda h: h
    rows = [(q_pre, LANE, hcol), (k_pre, LANE, hcol), (P, LANE, lambda h: KR_COL), *tabs]
    w = MLA_HEADS * LANE
    return _rowwise(fn, rows, [qg, kg], [(w, BF16), (w, BF16)], heads=MLA_HEADS, tm=HEAD_ROWS, name=name)


def _mla_qk_bwd(dq, dk_t, q_pre, k_pre, P, tabs, qg, kg, *, name):
    def fn(dqv, dkv, qp, kp, kr, c, a, b, qgv, kgv):
        dqp, dqg = _rmsnorm_bwd(_rope_t(dqv * MLA_SCALE, c, a, b), qp, qgv, MLA_QK)
        dkp, dkg = _rmsnorm_bwd(_rope_t(dkv, c, a, b), kp + kr, kgv, MLA_QK)
        lane = lax.broadcasted_iota(jnp.int32, (1, LANE), 1)
        return dqp, dkp, jnp.where((lane >= MLA_NOPE) & (lane < MLA_QK), dkp, 0.0), dqg, dkg

    hcol = lambda h: h
    rows = [(dq, LANE, hcol), (dk_t, LANE, hcol, True), (q_pre, LANE, hcol), (k_pre, LANE, hcol),
            (P, LANE, lambda h: KR_COL), *tabs]
    w = MLA_HEADS * LANE
    return _rowwise(fn, rows, [qg, kg], [(w, BF16), (w, BF16)], [(1, LANE), (1, LANE)], [LANE], heads=MLA_HEADS,
                    tm=HEAD_ROWS, name=name)


def _head_norm(x, g, *, heads, width, colfn=None, scale=1.0, name):
    return _rowwise(lambda xv, gv: _rmsnorm_fwd(xv, gv) * scale, [(x, width, colfn or (lambda h: h))],
                    [g.reshape(1, width)], [(heads * width, BF16)], heads=heads, tm=HEAD_ROWS, name=name)[0]


def _head_norm_bwd(dy, x, g, *, heads, width, colfn=None, scale=1.0, out_dtype, name):
    return _rowwise(lambda dv_, xv, gv: _rmsnorm_bwd(dv_ * scale, xv, gv),
                    [(dy, width, lambda h: h), (x, width, colfn or (lambda h: h))],
                    [g.reshape(1, width)], [(heads * width, out_dtype)], [(1, width)], heads=heads, tm=HEAD_ROWS,
                    name=name)


def _loss_grad(y, tgt, *, name):
    D = y.shape[1]

    def fn(yv, tv):
        d = yv - tv
        return d * (1.0 / D), jnp.sum(d * d, axis=0, keepdims=True) * (0.5 / D)

    dy, part = _rowwise(fn, [y, tgt], [], [(D, F32)], [(1, D)], name=name)
    return jnp.sum(part), dy


def _adamw(w, g, m, v, *, name):
    shape = w.shape
    two_d = (-1, shape[-1])

    def fn(wv, gv, mv, vv):
        m2 = ADAM_B1 * mv + (1.0 - ADAM_B1) * gv
        v2 = ADAM_B2 * vv + (1.0 - ADAM_B2) * (gv * gv)
        m_hat = m2 / (1.0 - ADAM_B1 ** ADAM_STEP)
        v_hat = v2 / (1.0 - ADAM_B2 ** ADAM_STEP)
        return -ADAM_LR * (m_hat / (jnp.sqrt(v_hat) + ADAM_EPS) + ADAM_WD * wv), m2, v2

    outs = _rowwise(fn, [t.reshape(two_d) for t in (w, g, m, v)], [], [(shape[-1], F32)] * 3, name=name)
    return [o.reshape(shape) for o in outs]


def _pad_cols(w, heads, hd):
    k = w.shape[0]
    return jnp.pad(w.reshape(k, heads, hd), ((0, 0), (0, 0), (0, LANE - hd))).reshape(k, heads * LANE)


def _unpad_cols(w, heads, hd):
    k = w.shape[0]
    return w.reshape(k, heads, LANE)[:, :, :hd].reshape(k, heads * hd)


def _pad_rows(w, heads, hd):
    n = w.shape[1]
    return jnp.pad(w.reshape(heads, hd, n), ((0, 0), (0, LANE - hd), (0, 0))).reshape(heads * LANE, n)


def _unpad_rows(w, heads, hd):
    n = w.shape[1]
    return w.reshape(heads, LANE, n)[:, :hd, :].reshape(heads * hd, n)


def _ffn_fwd(x, g, wgu, wd, tag):
    h = _norm_rows(x, g, name=tag + "_norm")
    gate, up, act = _mm_swiglu(h, wgu, name=tag + "_gu")
    y = _mm(act, wd, scale=0.5, residual=x, name=tag + "_down")
    return y, (x, h, gate, up, act)


def _ffn_bwd(dy, saved, g, wgu, wd, tag):
    x, h, gate, up, act = saved
    F = wd.shape[0]
    dwd = _mm(act, dy, ta=True, scale=0.5, name=tag + "_dwd")
    dgate, dup = _mm_dswiglu(dy, wd, gate, up, scale=0.5, name=tag + "_dact")
    dh = _mm(dgate, wgu, tb=True, name=tag + "_dh_g")
    dx, dg = _mm(dup, wgu, tb=True, b_off=(0, F), residual=dh, norm_bwd=(x, g, dy), name=tag + "_dh_u")
    dwgu = jnp.concatenate([_mm(h, dgate, ta=True, name=tag + "_dwg"), _mm(h, dup, ta=True, name=tag + "_dwu")], axis=1)
    return dx, dg, dwgu, dwd


def _even_weights(w_in, w_out):
    parts = [w_in[:, :SB_W] * SB_SCALE, w_in[:, SB_W:2 * SB_W], w_in[:, 2 * SB_W:3 * SB_W]]
    wqkv = jnp.concatenate([_pad_cols(p, SB_HEADS, SB_HD) for p in parts], axis=1)
    return wqkv, w_in[:, 3 * SB_W:], _pad_rows(w_out[:SB_W], SB_HEADS, SB_HD), w_out[SB_W:]


def _even_fwd(x, g, wts, ln_g, ln_b, sgu_w, bt, late_shard, tag):
    wqkv, wz, wo_sb, wo_sg = wts
    h = _norm_rows(x, g, name=tag + "_norm")
    qkv = _mm(h, wqkv, out_dtype=BF16, name=tag + "_qkv")
    z = _mm(h, wz, name=tag + "_z")
    o_sb, tot, late = _attn_fwd(qkv, qkv, qkv, sb=True, causal=True, heads=SB_HEADS, dq=LANE, dv=LANE, group=FWD_GROUP,
                                kcol=lambda g: SB_HEADS // FWD_GROUP + g, vcol=lambda g: 2 * SB_HEADS // FWD_GROUP + g,
                                side_gather=late_shard, name=tag + "_sb")
    u, gl = _gelu_ln(z, ln_g, ln_b, name=tag + "_geluln")
    o_sg = _spatial(gl, u, sgu_w, bt, name=tag + "_sgu")
    y = _mm(o_sb, wo_sb, residual=x, name=tag + "_out_sb")
    y = _mm(o_sg, wo_sg, residual=y, name=tag + "_out_sg")
    return y, (x, h, qkv, z, o_sb, tot, u, gl, o_sg), late


def _even_bwd(dy, saved, g, wts, ln_g, sgu_w, bt, partials, tag):
    wqkv, wz, wo_sb, wo_sg = wts
    x, h, qkv, z, o_sb, tot, u, gl, o_sg = saved
    do_sb = _mm(dy, wo_sb, tb=True, name=tag + "_do_sb")
    do_sg = _mm(dy, wo_sg, tb=True, name=tag + "_do_sg")
    dwo = jnp.concatenate([_unpad_rows(_mm(o_sb, dy, ta=True, name=tag + "_dwo_sb"), SB_HEADS, SB_HD),
                           _mm(o_sg, dy, ta=True, name=tag + "_dwo_sg")], axis=0)
    dq, dk_t, dv_t, received = _attn_bwd(qkv, qkv, qkv, o_sb, do_sb, tot, sb=True, causal=True, heads=SB_HEADS, dq=LANE,
                                         dv=LANE, kcol=lambda hh: SB_HEADS + hh, vcol=lambda hh: 2 * SB_HEADS + hh,
                                         side_exchange=partials, name=tag + "_sb_bwd")
    du, dgl, dsgu_w, db_t = _spatial_bwd(do_sg, gl, u, sgu_w, bt, name=tag + "_sgu_bwd")
    dz, dln_g, dln_b = _gelu_ln_bwd(z, du, dgl, ln_g, name=tag + "_geluln_bwd")
    dh = _mm(dz, wz, tb=True, name=tag + "_dh_z")
    dh = _mm(dq, wqkv, tb=True, residual=dh, name=tag + "_dh_q")
    dws = [_unpad_cols(_mm(h, dq, ta=True, scale=SB_SCALE, name=tag + "_dw_q"), SB_HEADS, SB_HD)]
    for i, (d_t, nm) in enumerate(((dk_t, "k"), (dv_t, "v")), start=1):
        dh = _mm(d_t, wqkv, ta=True, tb=True, b_off=(0, i * SB_HEADS * LANE), residual=dh,
                 norm_bwd=(x, g, dy) if nm == "v" else None, name=tag + "_dh_" + nm)
        dws.append(_unpad_rows(_mm(d_t, h, name=tag + "_dw_" + nm), SB_HEADS, SB_HD).T)
    dws.append(_mm(h, dz, ta=True, name=tag + "_dw_z"))
    dx, dg = dh
    return dx, dict(mix_norm=dg, sbg_w_in=jnp.concatenate(dws, axis=1), sgu_ln_gain=dln_g, sgu_ln_bias=dln_b,
                    sgu_w=dsgu_w, sgu_b=db_t[:, :SG_GROUPS].T, sbg_w_out=dwo), received


def _mla_weights(w_in, w_uq, w_ukv, w_out, q_gain, k_gain):
    d = w_in.shape[0]
    lat = MLA_QL + MLA_KVL
    w_in_ext = jnp.concatenate([w_in[:, :lat], jnp.zeros((d, MLA_NOPE), w_in.dtype), w_in[:, lat:],
                                jnp.zeros((d, LANE - MLA_QK), w_in.dtype)], axis=1)
    kv = w_ukv.reshape(MLA_KVL, MLA_HEADS, MLA_NOPE + MLA_V)
    wk = _pad_cols(kv[:, :, :MLA_NOPE].reshape(MLA_KVL, -1), MLA_HEADS, MLA_NOPE)
    wv = _pad_cols(kv[:, :, MLA_NOPE:].reshape(MLA_KVL, -1), MLA_HEADS, MLA_V)
    pad_gain = lambda gn: jnp.pad(gn.reshape(1, MLA_QK), ((0, 0), (0, LANE - MLA_QK)))
    return (w_in_ext, _pad_cols(w_uq, MLA_HEADS, MLA_QK), wk, wv, _pad_rows(w_out, MLA_HEADS, MLA_V),
            pad_gain(q_gain), pad_gain(k_gain))


def _mla_fwd(x, g, wts, qlg, kvlg, tabs, tag):
    w_in, w_uq, wk, wv, w_out, qg, kg = wts
    h = _norm_rows(x, g, name=tag + "_norm")
    P = _mm(h, w_in, name=tag + "_in")
    cqn, ckvn = _mla_lora(P, qlg, kvlg, name=tag + "_lora")
    q_pre = _mm(cqn, w_uq, name=tag + "_uq")
    k_pre = _mm(ckvn, wk, name=tag + "_uk")
    ones_lane = jnp.tile((jnp.arange(LANE) == MLA_V).astype(F32), MLA_HEADS)[None, :]
    v = _mm(ckvn, wv, out_dtype=BF16, bias=ones_lane, name=tag + "_uv")
    q, k = _mla_qk(q_pre, k_pre, P, tabs, qg, kg, name=tag + "_qk")
    o, lse = _attn_fwd(q, k, v, sb=False, causal=True, heads=MLA_HEADS, dq=LANE, dv=LANE, group=FWD_GROUP,
                       sum_lane=MLA_V, name=tag + "_attn")
    y = _mm(o, w_out, residual=x, name=tag + "_out")
    return y, (x, h, P, cqn, ckvn, q_pre, k_pre, q, k, v, o, lse)


def _mla_bwd(dy, saved, g, wts, qlg, kvlg, tabs, tag):
    w_in, w_uq, wk, wv, w_out, qg, kg = wts
    x, h, P, cqn, ckvn, q_pre, k_pre, q, k, v, o, lse = saved
    do = _mm(dy, w_out, tb=True, name=tag + "_do")
    dw_out = _unpad_rows(_mm(o, dy, ta=True, name=tag + "_dwo"), MLA_HEADS, MLA_V)
    dq, dk_t, dv_t = _attn_bwd(q, k, v, o, do, lse, sb=False, causal=True, heads=MLA_HEADS, dq=LANE, dv=LANE,
                               name=tag + "_attn_bwd")
    dq_pre, dk_pre, dkr, dqg, dkg = _mla_qk_bwd(dq, dk_t, q_pre, k_pre, P, tabs, qg, kg, name=tag + "_qk_bwd")
    dcqn = _mm(dq_pre, w_uq, tb=True, name=tag + "_dcq")
    dckvn = _mm(dk_pre, wk, tb=True, name=tag + "_dckv_k")
    dckvn = _mm(dv_t, wv, ta=True, tb=True, residual=dckvn, name=tag + "_dckv_v")
    dw_uq = _unpad_cols(_mm(cqn, dq_pre, ta=True, name=tag + "_dwuq"), MLA_HEADS, MLA_QK)
    dwk = _unpad_cols(_mm(ckvn, dk_pre, ta=True, name=tag + "_dwk"), MLA_HEADS, MLA_NOPE)
    dwv = _unpad_rows(_mm(dv_t, ckvn, name=tag + "_dwv"), MLA_HEADS, MLA_V).T
    dw_ukv = jnp.concatenate([dwk.reshape(MLA_KVL, MLA_HEADS, MLA_NOPE), dwv.reshape(MLA_KVL, MLA_HEADS, MLA_V)],
                             axis=2).reshape(MLA_KVL, -1)
    dP, dqlg, dkvlg = _mla_lora_bwd(dcqn, dckvn, dkr, P, qlg, kvlg, name=tag + "_lora_bwd")
    dx, dg = _mm(dP, w_in, tb=True, norm_bwd=(x, g, dy), name=tag + "_dh")
    dw_in_ext = _mm(h, dP, ta=True, name=tag + "_dwin")
    lat = MLA_QL + MLA_KVL
    dw_in = jnp.concatenate([dw_in_ext[:, :lat], dw_in_ext[:, lat + MLA_NOPE:lat + MLA_QK]], axis=1)
    return dx, dict(mix_norm=dg, mla_w_in=dw_in, mla_q_lora_gain=dqlg, mla_kv_lora_gain=dkvlg, mla_w_uq=dw_uq,
                    mla_w_ukv=dw_ukv, mla_q_gain=dqg[:, :MLA_QK], mla_k_gain=dkg[:, :MLA_QK], mla_w_out=dw_out)


def _xmem_fwd(x, mem, g, gm, wq, wkv, qg, kg, wo, tag):
    hq = _norm_rows(x, g, name=tag + "_norm")
    hm = _norm_rows(mem, gm, name=tag + "_mnorm")
    qp = _mm(hq, wq, name=tag + "_q")
    kv = _mm(hm, wkv, name=tag + "_kv")
    q = _head_norm(qp, qg, heads=MEM_HEADS, width=MEM_HD, scale=MEM_SCALE, name=tag + "_qn")
    kn = _head_norm(kv, kg, heads=MEM_HEADS, width=MEM_HD, colfn=lambda hh: 2 * hh, name=tag + "_kn")
    kvb = kv.reshape(-1, MEM_HEADS, 2, MEM_HD)[:, :, 1].reshape(-1, MEM_HEADS * MEM_HD).astype(BF16)
    o, lse = _attn_fwd(q, kn, kvb, sb=False, causal=False, heads=MEM_HEADS, dq=MEM_HD, dv=MEM_HD, group=MEM_HEADS,
                       name=tag + "_attn")
    y = _mm(o, wo, residual=x, name=tag + "_out")
    return y, (x, hq, hm, qp, kv, q, kn, kvb, o, lse)


def _xmem_bwd(dy, saved, mem, g, gm, wq, wkv, qg, kg, wo, tag):
    x, hq, hm, qp, kv, q, kn, kvb, o, lse = saved
    m = mem.shape[0]
    do = _mm(dy, wo, tb=True, name=tag + "_do")
    dwo = _mm(o, dy, ta=True, name=tag + "_dwo")
    dq, dk_t, dv_t = _attn_bwd(q, kn, kvb, o, do, lse, sb=False, causal=False, heads=MEM_HEADS, dq=MEM_HD, dv=MEM_HD,
                               name=tag + "_attn_bwd")
    dk, dv = dk_t.T, dv_t.T
    dqp, dqg = _head_norm_bwd(dq, qp, qg, heads=MEM_HEADS, width=MEM_HD, scale=MEM_SCALE, out_dtype=BF16,
                              name=tag + "_qn_bwd")
    dkp, dkg = _head_norm_bwd(dk, kv, kg, heads=MEM_HEADS, width=MEM_HD, colfn=lambda hh: 2 * hh, out_dtype=F32,
                              name=tag + "_kn_bwd")
    dkv = jnp.concatenate([dkp.reshape(m, MEM_HEADS, MEM_HD), dv.reshape(m, MEM_HEADS, MEM_HD)], axis=2).reshape(m, -1)
    dwkv = _mm(hm, dkv, ta=True, name=tag + "_dwkv")
    dhm = _mm(dkv, wkv, tb=True, name=tag + "_dhm")
    _, dgm = _norm_rows_bwd(dhm, mem, gm, None, name=tag + "_dmnorm")
    dwq = _mm(hq, dqp, ta=True, name=tag + "_dwq")
    dx, dg = _mm(dqp, wq, tb=True, norm_bwd=(x, g, dy), name=tag + "_dhq")
    return dx, dict(xmem_norm=dg, xmem_mem_norm=dgm, xmem_wq=dwq, xmem_wkv=dwkv, xmem_q_gain=dqg, xmem_k_gain=dkg,
                    xmem_wo=dwo)


def _local_step(x, mem, positions, tgt, w, late_shard, finish_late, partials_so_far):
    tabs = _rope_tables(positions)
    even = _even_weights(w["sbg_w_in"][0], w["sbg_w_out"][0])
    bt = jnp.repeat(w["sgu_b"][0].T, SG_GD, axis=1)
    saved = []
    for l in range(2):
        t = f"l{l}"
        x, s_pre = _ffn_fwd(x, w["ffn_pre_norm"][l], w["ffn_pre_w_gu"][l], w["ffn_pre_w_down"][l], t + "_pre")
        if l == 0:
            x, s_mix, late = _even_fwd(x, w["mix_norm"][0], even, w["sgu_ln_gain"][0], w["sgu_ln_bias"][0], w["sgu_w"][0],
                                       bt, late_shard, t + "_even")
            finish_late(late)
            mla = _mla_weights(w["mla_w_in"][0], w["mla_w_uq"][0], w["mla_w_ukv"][0], w["mla_w_out"][0],
                               w["mla_q_gain"][0], w["mla_k_gain"][0])
        else:
            x, s_mix = _mla_fwd(x, w["mix_norm"][1], mla, w["mla_q_lora_gain"][0], w["mla_kv_lora_gain"][0], tabs,
                                t + "_mla")
        x, s_xm = _xmem_fwd(x, mem, w["xmem_norm"][l], w["xmem_mem_norm"][l], w["xmem_wq"][l], w["xmem_wkv"][l],
                            w["xmem_q_gain"][l], w["xmem_k_gain"][l], w["xmem_wo"][l], t + "_xm")
        x, s_post = _ffn_fwd(x, w["ffn_post_norm"][l], w["ffn_post_w_gu"][l], w["ffn_post_w_down"][l], t + "_post")
        saved.append((s_pre, s_mix, s_xm, s_post))
    loss, dx = _loss_grad(x, tgt, name="loss")
    grads = {}

    def put(name, l, val):
        grads.setdefault(name, {})[l] = val

    for l in (1, 0):
        t = f"l{l}"
        s_pre, s_mix, s_xm, s_post = saved[l]
        dx, dg, dwgu, dwd = _ffn_bwd(dx, s_post, w["ffn_post_norm"][l], w["ffn_post_w_gu"][l], w["ffn_post_w_down"][l],
                                     t + "_post")
        put("ffn_post_norm", l, dg), put("ffn_post_w_gu", l, dwgu), put("ffn_post_w_down", l, dwd)
        dx, gx = _xmem_bwd(dx, s_xm, mem, w["xmem_norm"][l], w["xmem_mem_norm"][l], w["xmem_wq"][l], w["xmem_wkv"][l],
                           w["xmem_q_gain"][l], w["xmem_k_gain"][l], w["xmem_wo"][l], t + "_xm")
        for k_, v_ in gx.items():
            put(k_, l, v_)
        if l == 0:
            dx, gm, received = _even_bwd(dx, s_mix, w["mix_norm"][0], even, w["sgu_ln_gain"][0], w["sgu_w"][0], bt,
                                         partials_so_far(grads), t + "_even")
        else:
            dx, gm = _mla_bwd(dx, s_mix, w["mix_norm"][1], mla, w["mla_q_lora_gain"][0], w["mla_kv_lora_gain"][0], tabs,
                              t + "_mla")
        for k_, v_ in gm.items():
            put(k_, l if k_ == "mix_norm" else 0, v_)
        dx, dg, dwgu, dwd = _ffn_bwd(dx, s_pre, w["ffn_pre_norm"][l], w["ffn_pre_w_gu"][l], w["ffn_pre_w_down"][l],
                                     t + "_pre")
        put("ffn_pre_norm", l, dg), put("ffn_pre_w_gu", l, dwgu), put("ffn_pre_w_down", l, dwd)
    return loss, dx, {k_: [v_[l] for l in sorted(v_)] for k_, v_ in grads.items()}, received


N_CHIPS = 4
PACK_COLS = 1024
PACK_ROW_MULTIPLE = 512


def _place():
    x, y, c = lax.axis_index("x"), lax.axis_index("y"), lax.axis_index("c")
    return x, y, c, [(1 - x, y), (x, 1 - y), (1 - x, 1 - y)]


def _hops(x, y, c):
    return ((x + 1 - c) % 2, (y + c) % 2), ((x + c) % 2, (y + 1 - c) % 2), (1 - x, 1 - y)


GATHER_COPIES = 6
GATHER_STAGES = 4


def _gather_stages(x_ref, out_ref, send_sems, recv_sems):
    Rh = x_ref.shape[0] // 2
    x, y, c = lax.axis_index("x"), lax.axis_index("y"), lax.axis_index("c")
    n1, n2, nd = _hops(x, y, c)
    me, q1, q2, qd = 2 * x + y, 2 * n1[0] + n1[1], 2 * n2[0] + n2[1], 2 * nd[0] + nd[1]
    sibling = (x, y, 1 - c)

    def half(chip, core):
        return out_ref.at[chip, pl.ds(core * Rh, Rh), :]

    def copy(k, chip, core, to, own=False):
        return pltpu.make_async_remote_copy(src_ref=x_ref.at[pl.ds(c * Rh, Rh), :] if own else half(chip, core),
                                            dst_ref=half(chip, core), send_sem=send_sems.at[k], recv_sem=recv_sems.at[k],
                                            device_id=to, device_id_type=MESH)

    sends = [lambda: copy(0, me, c, (*n1, c), own=True), lambda: copy(1, me, c, (*n2, c), own=True),
             lambda: copy(2, q1, c, (*n2, c)), lambda: copy(3, q1, c, sibling), lambda: copy(4, q2, c, sibling),
             lambda: copy(5, qd, c, sibling)]

    def own_halves_out():
        sends[0]().start()
        sends[1]().start()

    def first_neighbours_on():
        copy(0, q1, c, sibling).wait_recv()
        sends[2]().start()
        sends[3]().start()

    def others_to_sibling():
        copy(1, q2, c, sibling).wait_recv()
        sends[4]().start()
        copy(2, qd, c, sibling).wait_recv()
        sends[5]().start()

    def all_landed():
        copy(3, q2, 1 - c, sibling).wait_recv()
        copy(4, q1, 1 - c, sibling).wait_recv()
        copy(5, qd, 1 - c, sibling).wait_recv()
        for send in sends:
            send().wait_send()

    return own_halves_out, first_neighbours_on, others_to_sibling, all_landed


N_DEVICES = 8


def _partials_exchange(g_ref, recv_ref, send_sems, recv_sems):
    Rh = g_ref.shape[1] // 2
    x, y, c = lax.axis_index("x"), lax.axis_index("y"), lax.axis_index("c")

    def copy(k):
        tx, ty, tc = (x + (k >> 2)) % 2, (y + ((k >> 1) & 1)) % 2, (c + (k & 1)) % 2
        return pltpu.make_async_remote_copy(src_ref=g_ref.at[2 * tx + ty, pl.ds(tc * Rh, Rh), :],
                                            dst_ref=recv_ref.at[4 * x + 2 * y + c], send_sem=send_sems.at[k],
                                            recv_sem=recv_sems.at[k], device_id=(tx, ty, tc), device_id_type=MESH)

    def start():
        for k in range(1, N_DEVICES):
            copy(k).start()

    def finish():
        for k in range(1, N_DEVICES):
            copy(k).wait_recv()
            copy(k).wait_send()

    return start, finish


def _sum_partials(g, recv, *, name):
    _, R, C = g.shape
    Rh = R // 2
    tr = _row_tile(Rh, 512)
    nt = Rh // tr
    x, y, c = lax.axis_index("x"), lax.axis_index("y"), lax.axis_index("c")
    where = jnp.stack([2 * x + y, c, 4 * x + 2 * y + c]).astype(jnp.int32)

    def body(where_ref, g_ref, r_ref, o_ref):
        own, mine = g_ref[0], where_ref[2]
        total = None
        for d in range(N_DEVICES):
            term = jnp.where(mine == d, own, r_ref[d].astype(F32))
            total = term if total is None else total + term
        o_ref[...] = total

    spec = pltpu.PrefetchScalarGridSpec(
        num_scalar_prefetch=1, grid=(nt,),
        in_specs=[pl.BlockSpec((1, tr, C), lambda i, wh: (wh[0], wh[1] * nt + i, 0)),
                  pl.BlockSpec((N_DEVICES, tr, C), lambda i, wh: (0, i, 0))],
        out_specs=pl.BlockSpec((tr, C), lambda i, wh: (i, 0)))
    return pl.pallas_call(body, name=name, grid_spec=spec, out_shape=jax.ShapeDtypeStruct((Rh, C), F32),
                          compiler_params=_params(1))(where, g, recv)


def _place_own_slot(others, shard):
    return lax.dynamic_update_slice(others, shard[None], (2 * lax.axis_index("x") + lax.axis_index("y"), 0, 0))


def _gather_chips(shard):
    def body(x_ref, out_ref, send_sems, recv_sems):
        for stage in _gather_stages(x_ref, out_ref, send_sems, recv_sems):
            stage()

    others = pl.pallas_call(
        body, name="gather_weights", out_shape=jax.ShapeDtypeStruct((N_CHIPS,) + shard.shape, shard.dtype),
        in_specs=[ANY], out_specs=ANY,
        scratch_shapes=[pltpu.SemaphoreType.DMA((GATHER_COPIES,)), pltpu.SemaphoreType.DMA((GATHER_COPIES,))])(shard)
    return _place_own_slot(others, shard)


def _gather_devices(block):
    M, N = block.shape

    def body(x_ref, out_ref, send_sems, recv_sems, local_sem):
        x, y, c, chips = _place()
        me, sibling = (x, y, c), (x, y, 1 - c)

        def rows(px, py, pc):
            return out_ref.at[pl.ds((4 * px + 2 * py + pc) * M, M), :]

        def copy(k, blk, to, src=None):
            return pltpu.make_async_remote_copy(src_ref=rows(*blk) if src is None else src, dst_ref=rows(*blk),
                                                send_sem=send_sems.at[k], recv_sem=recv_sems.at[k], device_id=to,
                                                device_id_type=MESH)

        mine = pltpu.make_async_copy(x_ref, rows(*me), local_sem)
        mine.start()
        first = [copy(0, me, sibling, src=x_ref)]
        first += [copy(1 + j, me, (*chip, c), src=x_ref) for j, chip in enumerate(chips)]
        for cp in first:
            cp.start()
        passed = [copy(4 + j, (*chip, c), sibling) for j, chip in enumerate(chips)]
        for j, chip in enumerate(chips):
            copy(1 + j, (*chip, c), me).wait_recv()
            passed[j].start()
        copy(0, sibling, me).wait_recv()
        for j, chip in enumerate(chips):
            copy(4 + j, (*chip, 1 - c), me).wait_recv()
        for cp in first + passed:
            cp.wait_send()
        mine.wait()

    vmem = pl.BlockSpec(memory_space=pltpu.VMEM)
    return pl.pallas_call(
        body, name=f"gather_devices_{M}", out_shape=jax.ShapeDtypeStruct((8 * M, N), block.dtype),
        in_specs=[vmem], out_specs=vmem,
        scratch_shapes=[pltpu.SemaphoreType.DMA((7,)), pltpu.SemaphoreType.DMA((7,)), pltpu.SemaphoreType.DMA],
        compiler_params=pltpu.CompilerParams(vmem_limit_bytes=VMEM_LIMIT))(block)


def _swap_halves(g):
    n, R, C = g.shape
    Rh = R // 2

    def body(g_ref, a_ref, send_sem, recv_sem):
        x, y, c, _ = _place()
        cp = pltpu.make_async_remote_copy(src_ref=g_ref.at[:, pl.ds((1 - c) * Rh, Rh), :], dst_ref=a_ref,
                                          send_sem=send_sem, recv_sem=recv_sem, device_id=(x, y, 1 - c),
                                          device_id_type=MESH)
        cp.start()
        cp.wait()

    return pl.pallas_call(body, name="grad_swap_halves", out_shape=jax.ShapeDtypeStruct((n, Rh, C), g.dtype),
                          in_specs=[ANY], out_specs=ANY,
                          scratch_shapes=[pltpu.SemaphoreType.DMA, pltpu.SemaphoreType.DMA])(g)


def _add_picked(a, b, picks, *, a_row_half=None, out_dtype, name):
    n_out = picks.shape[0]
    _, rows, C = b.shape
    tr = _row_tile(rows, 512)
    nt = rows // tr
    half = jnp.zeros((1,), jnp.int32) if a_row_half is None else a_row_half

    def body(pick_ref, half_ref, a_ref, b_ref, o_ref):
        o_ref[...] = (a_ref[...].astype(F32) + b_ref[...].astype(F32)).astype(o_ref.dtype)

    spec = pltpu.PrefetchScalarGridSpec(
        num_scalar_prefetch=2, grid=(n_out, nt),
        in_specs=[pl.BlockSpec((1, tr, C), lambda j, i, pick, hf: (pick[j], hf[0] * nt + i, 0)),
                  pl.BlockSpec((1, tr, C), lambda j, i, pick, hf: (pick[j], i, 0))],
        out_specs=pl.BlockSpec((1, tr, C), lambda j, i, pick, hf: (j, i, 0)))
    return pl.pallas_call(body, name=name, grid_spec=spec, out_shape=jax.ShapeDtypeStruct((n_out, rows, C), out_dtype),
                          compiler_params=_params(2))(picks.astype(jnp.int32), half.astype(jnp.int32), a, b)


def _hop_exchange(src, hop, *, name):
    def body(s_ref, d_ref, send_sem, recv_sem):
        x, y, c = lax.axis_index("x"), lax.axis_index("y"), lax.axis_index("c")
        cp = pltpu.make_async_remote_copy(src_ref=s_ref, dst_ref=d_ref, send_sem=send_sem, recv_sem=recv_sem,
                                          device_id=(*_hops(x, y, c)[hop], c), device_id_type=MESH)
        cp.start()
        cp.wait()

    return pl.pallas_call(body, name=name, out_shape=jax.ShapeDtypeStruct(src.shape, src.dtype), in_specs=[ANY],
                          out_specs=ANY, scratch_shapes=[pltpu.SemaphoreType.DMA, pltpu.SemaphoreType.DMA])(src)


def _reduce_over_chips(g):
    x, y, c = lax.axis_index("x"), lax.axis_index("y"), lax.axis_index("c")
    n1, n2, _ = _hops(x, y, c)
    chip = lambda p: 2 * p[0] + p[1]
    near = jnp.stack([chip((x, y)), chip(n2)])
    far = jnp.stack([chip(n1), chip((1 - x, 1 - y))])
    half = c.reshape(1)
    sib = _swap_halves(g)
    kept = _add_picked(g, sib, near, a_row_half=half, out_dtype=F32, name="grad_add_near")
    sent = _add_picked(g, sib, far, a_row_half=half, out_dtype=BF16, name="grad_add_far")
    got = _hop_exchange(sent, 0, name="grad_hop_first")
    mine = _add_picked(kept, got, jnp.zeros((1,), jnp.int32), out_dtype=F32, name="grad_add_mine")
    theirs = _add_picked(kept, got, jnp.ones((1,), jnp.int32), out_dtype=BF16, name="grad_add_theirs")
    got = _hop_exchange(theirs, 1, name="grad_hop_second")
    total = _add_picked(mine, got, jnp.zeros((1,), jnp.int32), out_dtype=F32, name="grad_add_total")
    return _join_halves(total[0])


def _sum_slots(b, *, name):
    n, R, C = b.shape
    tr = _row_tile(R, 512)

    def body(b_ref, o_ref):
        acc = b_ref[0]
        for q in range(1, n):
            acc = acc + b_ref[q]
        o_ref[...] = acc

    return pl.pallas_call(body, name=name, grid=(R // tr,), in_specs=[pl.BlockSpec((n, tr, C), lambda i: (0, i, 0))],
                          out_specs=pl.BlockSpec((tr, C), lambda i: (i, 0)), out_shape=jax.ShapeDtypeStruct((R, C), F32),
                          compiler_params=_params(1))(b)


def _join_halves(r):
    Rh, C = r.shape

    def body(r_ref, o_ref, send_sem, recv_sem):
        x, y, c, _ = _place()
        own, other = o_ref.at[pl.ds(c * Rh, Rh), :], o_ref.at[pl.ds((1 - c) * Rh, Rh), :]
        cp = pltpu.make_async_remote_copy(src_ref=r_ref, dst_ref=own, send_sem=send_sem, recv_sem=recv_sem,
                                          device_id=(x, y, 1 - c), device_id_type=MESH)
        cp.start()
        pltpu.make_async_remote_copy(src_ref=r_ref, dst_ref=other, send_sem=send_sem, recv_sem=recv_sem,
                                     device_id=(x, y, 1 - c), device_id_type=MESH).wait_recv()
        cp.wait_send()

    theirs = pl.pallas_call(
        body, name="grad_join_halves", out_shape=jax.ShapeDtypeStruct((2 * Rh, C), r.dtype), in_specs=[ANY], out_specs=ANY,
        scratch_shapes=[pltpu.SemaphoreType.DMA, pltpu.SemaphoreType.DMA])(r)
    return lax.dynamic_update_slice(theirs, r, (lax.axis_index("c") * Rh, 0))


def _size(shape):
    size = 1
    for d in shape:
        size *= d
    return size


def _pack(pieces, cols, row_multiple, dtype):
    if any(p.size % cols for p in pieces):
        flat = jnp.concatenate([p.reshape(-1).astype(dtype) for p in pieces])
        pieces = [jnp.pad(flat, (0, -flat.shape[0] % cols))]
    rows = [p.reshape(-1, cols).astype(dtype) for p in pieces]
    pad = -sum(r.shape[0] for r in rows) % row_multiple
    return jnp.concatenate(rows + ([jnp.zeros((pad, cols), dtype)] if pad else []), axis=0)


def _unpack(buf, shapes):
    cols = buf.shape[1]
    if any(_size(s) % cols for s in shapes):
        flat, out, at = buf.reshape(-1), [], 0
        for shp in shapes:
            out.append(flat[at:at + _size(shp)].reshape(shp))
            at += _size(shp)
        return out
    out, at = [], 0
    for shp in shapes:
        out.append(buf[at:at + _size(shp) // cols].reshape(shp))
        at += _size(shp) // cols
    return out


SHARDED = (("ffn_pre_w_gu", 2), ("ffn_pre_w_down", 1), ("sbg_w_in", 2), ("sbg_w_out", 1), ("mla_w_in", 1),
           ("mla_w_uq", 2), ("mla_w_ukv", 2), ("mla_w_out", 1), ("xmem_wq", 1), ("xmem_wkv", 2), ("xmem_wo", 1),
           ("ffn_post_w_gu", 2), ("ffn_post_w_down", 1))
EARLY = (("ffn_pre_w_gu", 0), ("ffn_pre_w_down", 0), ("sbg_w_in", 0), ("sbg_w_out", 0))
LORA_GAINS = ("mla_q_lora_gain", "mla_kv_lora_gain")
REPLICATED = ("ffn_pre_norm", "mix_norm", "sgu_ln_gain", "sgu_ln_bias", "sgu_w", "sgu_b", "mla_q_gain", "mla_k_gain",
              "xmem_norm", "xmem_mem_norm", "xmem_q_gain", "xmem_k_gain", "ffn_post_norm")
WEIGHTS = ("ffn_pre_norm", "ffn_pre_w_gu", "ffn_pre_w_down", "mix_norm", "sbg_w_in", "sgu_ln_gain", "sgu_ln_bias", "sgu_w",
           "sgu_b", "sbg_w_out", "mla_w_in", "mla_q_lora_gain", "mla_kv_lora_gain", "mla_w_uq", "mla_w_ukv", "mla_q_gain",
           "mla_k_gain", "mla_w_out", "xmem_norm", "xmem_mem_norm", "xmem_wq", "xmem_wkv", "xmem_q_gain", "xmem_k_gain",
           "xmem_wo", "ffn_post_norm", "ffn_post_w_gu", "ffn_post_w_down")
INPUTS = ("x", "mem", "positions") + WEIGHTS + ("loss_target",) + tuple("m_" + n for n in WEIGHTS) + tuple(
    "v_" + n for n in WEIGHTS)


def _step(a):
    x, y, c, _ = _place()
    chip = 2 * x + y
    w = {n: [None] * a[n].shape[0] for n, _ in SHARDED}
    lots = {early: [(n, l, ax) for n, ax in SHARDED for l in range(a[n].shape[0]) if ((n, l) in EARLY) == early]
            for early in (True, False)}

    def packed(lot):
        return _pack([a[n][l] for n, l, _ in lot], PACK_COLS, PACK_ROW_MULTIPLE, BF16)

    def unpack(gathered, lot):
        at = 0
        for n, l, ax in lot:
            shp = a[n].shape[1:]
            rows = _size(shp) // PACK_COLS
            per_chip = gathered[:, at:at + rows].reshape((N_CHIPS,) + shp)
            at += rows
            w[n][l] = jnp.moveaxis(per_chip, 0, ax - 1).reshape(shp[:ax - 1] + (N_CHIPS * shp[ax - 1],) + shp[ax:])

    unpack(_gather_chips(packed(lots[True])), lots[True])
    gains = jnp.zeros((8, LANE), F32)
    for r, n in enumerate(LORA_GAINS):
        gains = gains.at[r, :a[n].shape[1]].set(a[n][0])
    gains = _gather_devices(gains)
    for r, n in enumerate(LORA_GAINS):
        w[n] = jnp.concatenate([gains[16 * q + r, :a[n].shape[1]] for q in range(N_CHIPS)])[None, :]
    for n in REPLICATED:
        w[n] = a[n]

    def packed_grads(lot, grads):
        def part(n, l, ax, q):
            size = a[n].shape[ax]
            return lax.slice_in_dim(grads[n][l], q * size, (q + 1) * size, axis=ax - 1)

        return jnp.stack([_pack([part(n, l, ax, q) for n, l, ax in lot], PACK_COLS, PACK_ROW_MULTIPLE, F32)
                          for q in range(N_CHIPS)])

    sent = {}

    def partials_so_far(grads):
        sent["f32"] = packed_grads(lots[False], grads)
        return sent["f32"].astype(BF16)

    loss, dx, grads, received = _local_step(a["x"][0], a["mem"][0], a["positions"][0], a["loss_target"][0], w,
                                            packed(lots[False]), lambda gathered: unpack(gathered, lots[False]),
                                            partials_so_far)
    loss = lax.psum(loss, ("x", "y", "c"))
    small_names = REPLICATED + LORA_GAINS
    full = {n: jnp.stack(grads[n]).reshape(w[n].shape) for n in small_names}

    reduced = {False: _join_halves(_sum_partials(sent["f32"], received, name="grad_sum_partials")),
               True: _reduce_over_chips(packed_grads(lots[True], grads))}
    layers = {n: [None] * a[n].shape[0] for n, _ in SHARDED}
    for early, lot in lots.items():
        for (n, l, _), piece in zip(lot, _unpack(reduced[early], [a[n].shape[1:] for n, _, _ in lot])):
            layers[n][l] = piece
    gw = {n: jnp.stack(layers[n]) for n, _ in SHARDED}

    small = _pack([full[n] for n in small_names], LANE, 256, F32)
    rows = small.shape[0]
    summed = _sum_slots(_gather_devices(small).reshape(8, rows, LANE), name="grad_sum_devices")
    for n, val in zip(small_names, _unpack(summed, [full[n].shape for n in small_names])):
        if n in LORA_GAINS:
            size = a[n].shape[1]
            val = lax.dynamic_slice_in_dim(val, chip * size, size, axis=1)
        gw[n] = val

    upd = {n: _adamw(a[n], gw[n], a["m_" + n], a["v_" + n], name="adamw_" + n) for n in WEIGHTS}
    return (loss, dx[None], *[gw[n] for n in WEIGHTS], *[upd[n][0] for n in WEIGHTS], *[upd[n][1] for n in WEIGHTS],
            *[upd[n][2] for n in WEIGHTS])


def kernel(x, mem, positions, ffn_pre_norm, ffn_pre_w_gu, ffn_pre_w_down, mix_norm, sbg_w_in, sgu_ln_gain,
           sgu_ln_bias, sgu_w, sgu_b, sbg_w_out, mla_w_in, mla_q_lora_gain, mla_kv_lora_gain, mla_w_uq, mla_w_ukv,
           mla_q_gain, mla_k_gain, mla_w_out, xmem_norm, xmem_mem_norm, xmem_wq, xmem_wkv, xmem_q_gain, xmem_k_gain,
           xmem_wo, ffn_post_norm, ffn_post_w_gu, ffn_post_w_down, loss_target, m_ffn_pre_norm, m_ffn_pre_w_gu,
           m_ffn_pre_w_down, m_mix_norm, m_sbg_w_in, m_sgu_ln_gain, m_sgu_ln_bias, m_sgu_w, m_sgu_b, m_sbg_w_out,
           m_mla_w_in, m_mla_q_lora_gain, m_mla_kv_lora_gain, m_mla_w_uq, m_mla_w_ukv, m_mla_q_gain, m_mla_k_gain,
           m_mla_w_out, m_xmem_norm, m_xmem_mem_norm, m_xmem_wq, m_xmem_wkv, m_xmem_q_gain, m_xmem_k_gain,
           m_xmem_wo, m_ffn_post_norm, m_ffn_post_w_gu, m_ffn_post_w_down, v_ffn_pre_norm, v_ffn_pre_w_gu,
           v_ffn_pre_w_down, v_mix_norm, v_sbg_w_in, v_sgu_ln_gain, v_sgu_ln_bias, v_sgu_w, v_sgu_b, v_sbg_w_out,
           v_mla_w_in, v_mla_q_lora_gain, v_mla_kv_lora_gain, v_mla_w_uq, v_mla_w_ukv, v_mla_q_gain, v_mla_k_gain,
           v_mla_w_out, v_xmem_norm, v_xmem_mem_norm, v_xmem_wq, v_xmem_wkv, v_xmem_q_gain, v_xmem_k_gain,
           v_xmem_wo, v_ffn_post_norm, v_ffn_post_w_gu, v_ffn_post_w_down):
    given = locals()
    return _step({n: given[n] for n in INPUTS})
```

```python
import jax
import jax.numpy as jnp
from jax import lax
from jax.experimental import pallas as pl
from jax.experimental.pallas import tpu as pltpu

F32, BF16 = jnp.float32, jnp.bfloat16
LANE = 128
VMEM_LIMIT = 56 * 1024 * 1024
EPS = 1e-6
SB_HEADS, SB_HD = 8, 64
SG_GROUPS, SG_GD, SG_CHUNK = 8, 64, 128
SB_W, SG_W = SB_HEADS * SB_HD, SG_GROUPS * SG_GD
MLA_HEADS, MLA_NOPE, MLA_ROPE, MLA_V = 16, 64, 32, 64
MLA_QK = MLA_NOPE + MLA_ROPE
MLA_QL, MLA_KVL = 512, 256
ROPE_THETA = 10000.0
MEM_HEADS, MEM_HD = 4, 256
SB_SCALE, MLA_SCALE, MEM_SCALE = SB_HD ** -0.5, MLA_QK ** -0.5, MEM_HD ** -0.5
ADAM_LR, ADAM_B1, ADAM_B2, ADAM_EPS, ADAM_WD, ADAM_STEP = 0.001, 0.9, 0.999, 1e-08, 0.01, 10
MESH = pl.DeviceIdType.MESH
ANY = pl.BlockSpec(memory_space=pl.ANY)


def _params(n_axes):
    return pltpu.CompilerParams(dimension_semantics=("arbitrary",) * n_axes, vmem_limit_bytes=VMEM_LIMIT)


MM_TILE_CAP = 1408
MM_VMEM_BUDGET = 40 * 1024 * 1024


def _tile(dim, cap):
    if dim <= cap:
        return dim
    best = max(t for t in range(LANE, cap + 1, LANE) if dim % t == 0)
    return best


def _k_scratch(count, tile, nk):
    return [pltpu.VMEM(tile, F32)] * count if nk > 1 else []


def _over_k_steps(prods, acc_refs, nk, finish):
    if nk == 1:
        finish(prods)
        return
    kk = pl.program_id(2)

    @pl.when(kk == 0)
    def _():
        for ref, p in zip(acc_refs, prods):
            ref[...] = p

    @pl.when(kk > 0)
    def _():
        for ref, p in zip(acc_refs, prods):
            ref[...] += p

    @pl.when(kk == nk - 1)
    def _():
        finish([ref[...] for ref in acc_refs])


def _mm(a, b, *, ta=False, tb=False, out_dtype=F32, scale=1.0, residual=None, bias=None, norm_bwd=None,
        side_exchange=None, a_off=(0, 0), b_off=(0, 0), m=None, n=None, k=None, name):
    am, ak = (a.shape[1], a.shape[0]) if ta else a.shape
    bk, bn = (b.shape[1], b.shape[0]) if tb else b.shape
    M, N, K = m or am, n or bn, k or ak
    tm, tn = _tile(M, MM_TILE_CAP if norm_bwd is None else MM_TILE_CAP // 2), _tile(N, MM_TILE_CAP)
    n_full = (residual is not None) + (2 if norm_bwd is not None else 0)
    fixed = tm * tn * (4 + 2 * jnp.dtype(out_dtype).itemsize + 8 * n_full)
    per_k = (tm * (2 * a.dtype.itemsize + 2) + tn * (2 * b.dtype.itemsize + 2))
    tk = _tile(K, max(LANE, (MM_VMEM_BUDGET - fixed) // per_k))
    nm, nn, nk = M // tm, N // tn, K // tk
    assert norm_bwd is None or nn == 1
    a_off = (a_off[0] // (tk if ta else tm), a_off[1] // (tm if ta else tk))
    b_off = (b_off[0] // (tn if tb else tk), b_off[1] // (tk if tb else tn))
    dims = (((0 if ta else 1,), (1 if tb else 0,)), ((), ()))
    n_out = 1 if norm_bwd is None else 2
    n_side = 0 if side_exchange is None else 1

    def body(*refs):
        a_ref, b_ref = refs[0], refs[1]
        n_in = len(ins)
        o_ref, extras = refs[n_in], refs[2:n_in - n_side]
        acc_refs = refs[n_in + n_out + n_side:len(refs) - 2 * n_side]
        first_rows = pl.program_id(0) == 0
        if n_side:
            start, landed = _partials_exchange(refs[n_in - 1], refs[n_in + n_out], *refs[-2:])
            steps = [pl.program_id(d) for d in range(3)]
            pl.when((steps[0] == 0) & (steps[1] == 0) & (steps[2] == 0))(start)

        def finish(total):
            out = total * scale
            for extra in (extras if norm_bwd is None else extras[:-3]):
                out = out + extra[...].astype(F32)
            if norm_bwd is not None:
                x_ref, g_ref, dres_ref = extras[-3:]
                dg_ref = refs[n_in + 1]
                dx, dg = _rmsnorm_bwd(out, x_ref[...], g_ref[...])
                out = dx + dres_ref[...]

                @pl.when(first_rows)
                def _():
                    dg_ref[...] = jnp.zeros_like(dg_ref)

                dg_ref[...] += dg
            o_ref[...] = out.astype(o_ref.dtype)

        prod = lax.dot_general(a_ref[...].astype(BF16), b_ref[...].astype(BF16), dims, preferred_element_type=F32)
        _over_k_steps([prod], acc_refs, nk, lambda totals: finish(totals[0]))
        if n_side:
            pl.when((steps[0] == nm - 1) & (steps[1] == nn - 1) & (steps[2] == nk - 1))(landed)

    (ao0, ao1), (bo0, bo1) = a_off, b_off
    a_spec = (pl.BlockSpec((tk, tm), lambda i, j, kk: (kk + ao0, i + ao1)) if ta
              else pl.BlockSpec((tm, tk), lambda i, j, kk: (i + ao0, kk + ao1)))
    b_spec = (pl.BlockSpec((tn, tk), lambda i, j, kk: (j + bo0, kk + bo1)) if tb
              else pl.BlockSpec((tk, tn), lambda i, j, kk: (kk + bo0, j + bo1)))
    o_spec = pl.BlockSpec((tm, tn), lambda i, j, kk: (i, j))
    ins, in_specs = [a, b], [a_spec, b_spec]
    if residual is not None:
        ins.append(residual)
        in_specs.append(o_spec)
    if bias is not None:
        ins.append(bias)
        in_specs.append(pl.BlockSpec((1, tn), lambda i, j, kk: (0, j)))
    out_specs, out_shape = o_spec, jax.ShapeDtypeStruct((M, N), out_dtype)
    if norm_bwd is not None:
        x, gain, dres = norm_bwd
        row = pl.BlockSpec((1, tn), lambda i, j, kk: (0, 0))
        ins += [x, gain.reshape(1, N), dres]
        in_specs += [o_spec, row, o_spec]
        out_specs, out_shape = [o_spec, row], [out_shape, jax.ShapeDtypeStruct((1, N), F32)]
    scratch = _k_scratch(1, (tm, tn), nk)
    if side_exchange is not None:
        _, rows, cols = side_exchange.shape
        ins.append(side_exchange)
        in_specs.append(ANY)
        out_specs = (out_specs if isinstance(out_specs, list) else [out_specs]) + [ANY]
        out_shape = (out_shape if isinstance(out_shape, list) else [out_shape]) + [
            jax.ShapeDtypeStruct((N_DEVICES, rows // 2, cols), side_exchange.dtype)]
        scratch = scratch + [pltpu.SemaphoreType.DMA((N_DEVICES,)), pltpu.SemaphoreType.DMA((N_DEVICES,))]
    return pl.pallas_call(
        body, name=name, grid=(nm, nn, nk), in_specs=in_specs, out_specs=out_specs, out_shape=out_shape,
        scratch_shapes=scratch, compiler_params=_params(3))(*ins)


def _mm_swiglu(h, wgu, *, name):
    M, K = h.shape
    F = wgu.shape[1] // 2
    tm, tn, tk = _tile(M, 512), _tile(F, MM_TILE_CAP), _tile(K, 1024)
    nm, nf, nk = M // tm, F // tn, K // tk

    def body(h_ref, wg_ref, wu_ref, g_ref, u_ref, a_ref, *acc_refs):
        def finish(totals):
            g, u = totals
            g_ref[...] = g.astype(BF16)
            u_ref[...] = u.astype(BF16)
            a_ref[...] = (g * jax.nn.sigmoid(g) * u).astype(BF16)

        hb = h_ref[...]
        _over_k_steps([jnp.dot(hb, wg_ref[...], preferred_element_type=F32),
                       jnp.dot(hb, wu_ref[...], preferred_element_type=F32)], acc_refs, nk, finish)

    o_spec = pl.BlockSpec((tm, tn), lambda j, i, kk: (i, j))
    shp = jax.ShapeDtypeStruct((M, F), BF16)
    return pl.pallas_call(
        body, name=name, grid=(nf, nm, nk),
        in_specs=[pl.BlockSpec((tm, tk), lambda j, i, kk: (i, kk)),
                  pl.BlockSpec((tk, tn), lambda j, i, kk: (kk, j)),
                  pl.BlockSpec((tk, tn), lambda j, i, kk: (kk, j + nf))],
        out_specs=[o_spec, o_spec, o_spec], out_shape=[shp, shp, shp],
        scratch_shapes=_k_scratch(2, (tm, tn), nk), compiler_params=_params(3))(h, wgu, wgu)


def _mm_dswiglu(dy, wd, gate, up, *, scale, name):
    M, K = dy.shape
    F = wd.shape[0]
    tm, tn, tk = _tile(M, 512), _tile(F, MM_TILE_CAP), _tile(K, 1024)
    nm, nf, nk = M // tm, F // tn, K // tk

    def body(dy_ref, wd_ref, g_ref, u_ref, dg_ref, du_ref, *acc_refs):
        def finish(totals):
            da = totals[0] * scale
            g, u = g_ref[...].astype(F32), u_ref[...].astype(F32)
            sg = jax.nn.sigmoid(g)
            du_ref[...] = (da * g * sg).astype(BF16)
            dg_ref[...] = (da * u * sg * (1.0 + g * (1.0 - sg))).astype(BF16)

        _over_k_steps([_nt(dy_ref[...].astype(BF16), wd_ref[...])], acc_refs, nk, finish)

    o_spec = pl.BlockSpec((tm, tn), lambda j, i, kk: (i, j))
    shp = jax.ShapeDtypeStruct((M, F), BF16)
    return pl.pallas_call(
        body, name=name, grid=(nf, nm, nk),
        in_specs=[pl.BlockSpec((tm, tk), lambda j, i, kk: (i, kk)),
                  pl.BlockSpec((tn, tk), lambda j, i, kk: (j, kk)), o_spec, o_spec],
        out_specs=[o_spec, o_spec], out_shape=[shp, shp],
        scratch_shapes=_k_scratch(1, (tm, tn), nk), compiler_params=_params(3))(dy, wd, gate, up)


HEAD_ROWS = 1024


def _row_tile(rows, cap):
    t = cap
    while t >= 8:
        if rows % t == 0:
            return t
        t //= 2
    return rows


def _rowwise(fn, rows, consts, outs, sums=(), hsums=(), *, heads=None, tm=256, name):
    rows = [r if isinstance(r, tuple) else (r, r.shape[1], None) for r in rows]
    rows = [r if len(r) == 4 else (*r, False) for r in rows]
    S = rows[0][0].shape[0]
    tm = _row_tile(S, tm)
    nh = heads or 1
    n_r, n_c, n_o, n_h, n_s = len(rows), len(consts), len(outs), len(hsums), len(sums)

    def body(*refs):
        r = [x[...].T if row[3] else x[...] for x, row in zip(refs, rows)]
        c = [x[...] for x in refs[n_r:n_r + n_c]]
        o_refs = refs[n_r + n_c:n_r + n_c + n_o]
        h_refs = refs[n_r + n_c + n_o:n_r + n_c + n_o + n_h]
        s_refs = refs[n_r + n_c + n_o + n_h:]
        res = fn(*r, *c)
        res = res if isinstance(res, (tuple, list)) else (res,)
        for ref, val in zip(o_refs, res[:n_o]):
            ref[...] = val.astype(ref.dtype)
        if n_h:
            @pl.when(pl.program_id(1) == 0)
            def _():
                for ref in h_refs:
                    ref[...] = jnp.zeros_like(ref)
            for ref, val in zip(h_refs, res[n_o:n_o + n_h]):
                ref[...] += val
        if n_s:
            @pl.when((pl.program_id(0) == 0) & (pl.program_id(1) == 0))
            def _():
                for ref in s_refs:
                    ref[...] = jnp.zeros_like(ref)
            for ref, val in zip(s_refs, res[n_o + n_h:]):
                ref[...] += val

    def col(colfn):
        return (lambda i, h: (i, 0)) if colfn is None else (lambda i, h: (i, colfn(h)))

    in_specs = [pl.BlockSpec((w, tm), lambda i, h, cf=cf: (cf(h), i)) if flipped else pl.BlockSpec((tm, w), col(cf))
                for _, w, cf, flipped in rows]
    in_specs += [pl.BlockSpec(a.shape, lambda i, h, nd=a.ndim: (0,) * nd) for a in consts]
    out_specs = [pl.BlockSpec((tm, w // nh), (lambda i, h: (i, h)) if heads else (lambda i, h: (i, 0))) for w, _ in outs]
    out_specs += [pl.BlockSpec((tm, w), lambda i, h: (i, 0)) for w in hsums]
    out_specs += [pl.BlockSpec(sh, lambda i, h, nd=len(sh): (0,) * nd) for sh in sums]
    out_shape = [jax.ShapeDtypeStruct((S, w), dt) for w, dt in outs]
    out_shape += [jax.ShapeDtypeStruct((S, w), F32) for w in hsums]
    out_shape += [jax.ShapeDtypeStruct(sh, F32) for sh in sums]
    return pl.pallas_call(body, name=name, grid=(S // tm, nh), in_specs=in_specs, out_specs=out_specs,
                          out_shape=out_shape, compiler_params=_params(2))(*[row[0] for row in rows], *consts)


def _rms(x, width=None):
    width = width or x.shape[-1]
    return lax.rsqrt(jnp.sum(x * x, axis=-1, keepdims=True) * (1.0 / width) + EPS)


def _rmsnorm_fwd(x, g, width=None):
    return x * _rms(x, width) * g


def _rmsnorm_bwd(dy, x, g, width=None):
    width = width or x.shape[-1]
    r = _rms(x, width)
    xn = x * r
    dxn = dy * g
    dx = r * (dxn - xn * (jnp.sum(dxn * xn, axis=-1, keepdims=True) * (1.0 / width)))
    return dx, jnp.sum(dy * xn, axis=0, keepdims=True)


def _norm_rows(x, g, *, name, out_dtype=BF16):
    D = x.shape[1]
    return _rowwise(lambda xv, gv: _rmsnorm_fwd(xv.astype(F32), gv), [x], [g.reshape(1, D)], [(D, out_dtype)],
                    name=name)[0]


def _norm_rows_bwd(dh, x, g, dres, *, name):
    D = x.shape[1]

    def fn(dhv, xv, *rest):
        dx, dg = _rmsnorm_bwd(dhv.astype(F32), xv, rest[-1])
        return (dx + rest[0] if dres is not None else dx), dg

    rows = [dh, x] + ([dres] if dres is not None else [])
    return _rowwise(fn, rows, [g.reshape(1, D)], [(D, F32)], [(1, D)], name=name)


def _softplus(z):
    return jnp.where(z > 20.0, z, jnp.log(1.0 + jnp.exp(z)))


def _running_sum(v, u, split=True):
    if not split:
        return jnp.dot(v.astype(BF16), u, preferred_element_type=F32)
    hi = lax.bitcast_convert_type(lax.bitcast_convert_type(v, jnp.uint32) & jnp.uint32(0xFFFF0000), F32)
    return (jnp.dot(hi.astype(BF16), u, preferred_element_type=F32)
            + jnp.dot((v - hi).astype(BF16), u, preferred_element_type=F32))


def _triangle(tk, inclusive_prefix):
    j, s = lax.broadcasted_iota(jnp.int32, (tk, tk), 0), lax.broadcasted_iota(jnp.int32, (tk, tk), 1)
    return ((j <= s) if inclusive_prefix else (j > s)).astype(BF16)


def _nt(a, b):
    return lax.dot_general(a, b, (((1,), (1,)), ((), ())), preferred_element_type=F32)


def _tn(a, b):
    return lax.dot_general(a, b, (((0,), (0,)), ((), ())), preferred_element_type=F32)


ATT_TQ, ATT_TK = 512, 512
SB_SUB = 256
FWD_GROUP = 2


def _attn_fwd(q, k, v, *, sb, causal, heads, dq, dv, group=1, kcol=None, vcol=None, sum_lane=None, side_gather=None,
              name):
    S, Sk = q.shape[0], k.shape[0]
    tq, tk = min(ATT_TQ, S), min(ATT_TK, Sk)
    sub = min(SB_SUB, tk) if sb else tk
    assert tq % sub == 0 or not causal
    kcol = kcol or (lambda h: h)
    vcol = vcol or (lambda h: h)
    members = range(group)
    assert side_gather is None or heads // group >= GATHER_STAGES

    def body(*refs):
        if side_gather is not None:
            n_in = 4 if sb else 3
            stages = _gather_stages(refs[n_in], refs[n_in + 3], *refs[-2:])
            for n, stage in enumerate(stages):
                pl.when((pl.program_id(0) == n) & (pl.program_id(1) == 0))(stage)
            refs = refs[:n_in] + refs[n_in + 1:n_in + 3] + refs[n_in + 4:-2]
        if sb:
            q_ref, k_ref, v_ref, u_ref, o_ref, lse_ref, acc_ref, r_ref = refs
            r_ref[...] = jnp.zeros_like(r_ref)
        else:
            q_ref, k_ref, v_ref, o_ref, lse_ref, acc_ref, m_ref, l_ref = refs
            m_ref[...] = jnp.full_like(m_ref, -1e30)
            l_ref[...] = jnp.zeros_like(l_ref)
        first_row = pl.program_id(1) * tq
        qb = [q_ref[:, hh * dq:(hh + 1) * dq] for hh in members]
        acc_ref[...] = jnp.zeros_like(acc_ref)
        nblk = (first_row + tq) // sub if causal else Sk // sub
        nfull = (first_row + (0 if sb else 1)) // sub if causal else nblk
        n_cut = tq // sub if causal else 0

        def scores(jj):
            off = pl.multiple_of(jj * sub, sub)
            return tuple(_nt(qb[hh], k_ref[pl.ds(off, sub), hh * dq:(hh + 1) * dq]) for hh in members)

        def weigh(jj, scores_now, masked):
            off = pl.multiple_of(jj * sub, sub)
            if masked:
                kpos = off + lax.broadcasted_iota(jnp.int32, (tq, sub), 1)
                qpos = first_row + lax.broadcasted_iota(jnp.int32, (tq, sub), 0)
                valid = (kpos < qpos) if sb else (kpos <= qpos)
            for hh in members:
                vb = v_ref[pl.ds(off, sub), hh * dv:(hh + 1) * dv]
                s = scores_now[hh]
                if sb:
                    sp = _softplus(s)
                    ls = jnp.where(valid, -sp, 0.0) if masked else -sp
                    w = jnp.exp(s - sp + r_ref[hh] + _running_sum(ls, u_ref[...]))
                    if masked:
                        w = jnp.where(valid, w, 0.0)
                    acc_ref[hh] += jnp.dot(w.astype(BF16), vb, preferred_element_type=F32)
                    r_ref[hh] += jnp.sum(ls, axis=1, keepdims=True)
                else:
                    if masked:
                        s = jnp.where(valid, s, -1e30)
                    m_old = m_ref[hh]
                    m_new = jnp.maximum(m_old, jnp.max(s, axis=1, keepdims=True))
                    p = jnp.exp(s - m_new)
                    alpha = jnp.exp(m_old - m_new)
                    if sum_lane is None:
                        l_ref[hh] = alpha * l_ref[hh] + jnp.sum(p, axis=1, keepdims=True)
                    acc_ref[hh] = alpha * acc_ref[hh] + jnp.dot(p.astype(BF16), vb, preferred_element_type=F32)
                    m_ref[hh] = m_new

        if sb:
            s_cur = scores(nblk - 1)
            for cut in range(n_cut):
                s_next = scores(jnp.maximum(nblk - 2 - cut, 0))
                weigh(nblk - 1 - cut, s_cur, True)
                s_cur = s_next

            def step(t, s_now):
                s_next = scores(jnp.maximum(nfull - 2 - t, 0))
                weigh(nfull - 1 - t, s_now, False)
                return s_next

            lax.fori_loop(0, nfull, step, s_cur)
        else:
            n_loop = nfull if causal else nblk - 1

            def step(t, s_now):
                s_next = scores(jnp.minimum(t + 1, nblk - 1))
                weigh(t, s_now, False)
                return s_next

            s_cur = lax.fori_loop(0, n_loop, step, scores(0))
            tail = n_cut if causal else 1
            for last in range(tail):
                s_next = scores(n_loop + last + 1) if last + 1 < tail else None
                weigh(n_loop + last, s_cur, causal)
                s_cur = s_next
        for hh in members:
            cols = slice(hh * dv, (hh + 1) * dv)
            if sb:
                o_ref[:, cols] = acc_ref[hh]
                lse_ref[hh] = r_ref[hh]
            else:
                acc = acc_ref[hh]
                l = l_ref[hh] if sum_lane is None else acc[:, sum_lane:sum_lane + 1]
                o_ref[:, cols] = acc / l
                lse_ref[hh] = m_ref[hh] + jnp.log(l)

    in_specs = [pl.BlockSpec((tq, group * dq), lambda g, i: (i, g)),
                pl.BlockSpec((Sk, group * dq), lambda g, i: (0, kcol(g))),
                pl.BlockSpec((Sk, group * dv), lambda g, i: (0, vcol(g)))]
    ins = [q, k, v]
    scratch = [pltpu.VMEM((group, tq, dv), F32), pltpu.VMEM((group, tq, 1), F32)]
    if sb:
        ins.append(_triangle(sub, inclusive_prefix=False))
        in_specs.append(pl.BlockSpec((sub, sub), lambda g, i: (0, 0)))
    else:
        scratch.append(pltpu.VMEM((group, tq, 1), F32))
    out_specs = [pl.BlockSpec((tq, group * dv), lambda g, i: (i, g)), pl.BlockSpec((group, tq, 1), lambda g, i: (g, i, 0))]
    out_shape = [jax.ShapeDtypeStruct((S, heads * dv), F32), jax.ShapeDtypeStruct((heads, S, 1), F32)]
    if side_gather is not None:
        ins.append(side_gather)
        in_specs.append(ANY)
        out_specs.append(ANY)
        out_shape.append(jax.ShapeDtypeStruct((N_CHIPS,) + side_gather.shape, side_gather.dtype))
        scratch += [pltpu.SemaphoreType.DMA((GATHER_COPIES,)), pltpu.SemaphoreType.DMA((GATHER_COPIES,))]
    outs = pl.pallas_call(body, name=name, grid=(heads // group, S // tq), in_specs=in_specs, out_specs=out_specs,
                          out_shape=out_shape, scratch_shapes=scratch, compiler_params=_params(2))(*ins)
    return outs if side_gather is None else (outs[0], outs[1], _place_own_slot(outs[2], side_gather))


def _attn_bwd(q, k, v, o, do, lse, *, sb, causal, heads, dq, dv, kcol=None, vcol=None, side_exchange=None, name):
    S, Sk = q.shape[0], k.shape[0]
    tq, tk = min(ATT_TQ, S), min(ATT_TK, Sk)
    sub = min(SB_SUB, tk) if sb else tk
    assert tq % sub == 0 or not causal
    nq = S // tq
    kcol = kcol or (lambda h: h)
    vcol = vcol or (lambda h: h)
    n_in = 7 if sb else 6

    def body(*refs):
        if side_exchange is not None:
            start, finish = _partials_exchange(refs[n_in], refs[n_in + 4], *refs[-2:])
            pl.when((pl.program_id(0) == 0) & (pl.program_id(1) == 0))(start)
            pl.when((pl.program_id(0) == heads - 1) & (pl.program_id(1) == 0))(finish)
            refs = refs[:n_in] + refs[n_in + 1:n_in + 4] + refs[n_in + 5:-2]
        if sb:
            q_ref, k_ref, v_ref, o_ref, do_ref, lse_ref, u_ref, dq_ref, dk_ref, dv_ref, acc_ref, r_ref, re_ref = refs
            r_ref[...] = jnp.zeros_like(r_ref)
            re_ref[...] = jnp.zeros_like(re_ref)
        else:
            q_ref, k_ref, v_ref, o_ref, do_ref, lse_ref, dq_ref, dk_ref, dv_ref, acc_ref = refs
        first_row = pl.program_id(1) * tq

        @pl.when(first_row == 0)
        def _():
            dk_ref[...] = jnp.zeros_like(dk_ref)
            dv_ref[...] = jnp.zeros_like(dv_ref)

        qb = q_ref[...]
        dof = do_ref[...].astype(F32)
        dob = dof.astype(BF16)
        q_t, do_t = qb.T, dob.T
        if not sb:
            dlt = jnp.sum(dof * o_ref[...], axis=1, keepdims=True)
        acc_ref[...] = jnp.zeros_like(acc_ref)
        nblk = (first_row + tq) // sub if causal else Sk // sub
        nfull = (first_row + (0 if sb else 1)) // sub if causal else nblk
        n_cut = tq // sub if causal else 0

        def products(jj):
            off = pl.multiple_of(jj * sub, sub)
            return _nt(qb, k_ref[pl.ds(off, sub), :]), _nt(dob, v_ref[pl.ds(off, sub), :])

        def piece(jj, now, masked):
            off = pl.multiple_of(jj * sub, sub)
            kb = k_ref[pl.ds(off, sub), :]
            s, dp = now
            if masked:
                qpos = first_row + lax.broadcasted_iota(jnp.int32, (tq, sub), 0)
                kpos = off + lax.broadcasted_iota(jnp.int32, (tq, sub), 1)
                valid = (kpos < qpos) if sb else (kpos <= qpos)
            if sb:
                u = u_ref[...]
                sp = _softplus(s)
                ls = jnp.where(valid, -sp, 0.0) if masked else -sp
                lb = s - sp
                w = jnp.exp(lb + (lse_ref[0] - (r_ref[...] + _running_sum(ls, u))))
                if masked:
                    w = jnp.where(valid, w, 0.0)
                e = dp * w
                ds = e - jnp.exp(lb) * (re_ref[...] + _running_sum(e, u, split=False))
                if masked:
                    ds = jnp.where(valid, ds, 0.0)
                r_ref[...] += jnp.sum(ls, axis=1, keepdims=True)
                re_ref[...] += jnp.sum(e, axis=1, keepdims=True)
            else:
                w = jnp.exp(s - lse_ref[0])
                if masked:
                    w = jnp.where(valid, w, 0.0)
                ds = w * (dp - dlt)
            dsb = ds.astype(BF16)
            dv_ref[:, pl.ds(off, sub)] += jnp.dot(do_t, w.astype(BF16), preferred_element_type=F32)
            dk_ref[:, pl.ds(off, sub)] += jnp.dot(q_t, dsb, preferred_element_type=F32)
            acc_ref[...] += jnp.dot(dsb, kb, preferred_element_type=F32)

        n_loop = nfull if causal else nblk - 1
        per_trip = tk // sub

        def steps(first, count, masked):
            ready = [products(first + c) for c in range(count)]
            for c in range(count):
                piece(first + c, ready[c], masked)

        def trip(t, carry):
            steps(t * per_trip, per_trip, False)
            return carry

        lax.fori_loop(0, n_loop // per_trip, trip, 0)
        steps(n_loop, n_cut if causal else 1, causal)
        dq_ref[...] = acc_ref[...]

    ins = [q, k, v, o, do]
    in_specs = [pl.BlockSpec((tq, dq), lambda h, i: (i, h)),
                pl.BlockSpec((Sk, dq), lambda h, i: (0, kcol(h))),
                pl.BlockSpec((Sk, dv), lambda h, i: (0, vcol(h))),
                pl.BlockSpec((tq, dv), lambda h, i: (i, h)),
                pl.BlockSpec((tq, dv), lambda h, i: (i, h))]
    scratch = [pltpu.VMEM((tq, dq), F32)]
    ins.append(lse)
    in_specs.append(pl.BlockSpec((1, tq, 1), lambda h, i: (h, i, 0)))
    if sb:
        ins.append(_triangle(sub, inclusive_prefix=True))
        in_specs.append(pl.BlockSpec((sub, sub), lambda h, i: (0, 0)))
        scratch += [pltpu.VMEM((tq, 1), F32), pltpu.VMEM((tq, 1), F32)]
    out_specs = [pl.BlockSpec((tq, dq), lambda h, i: (i, h)),
                 pl.BlockSpec((dq, Sk), lambda h, i: (h, 0)),
                 pl.BlockSpec((dv, Sk), lambda h, i: (h, 0))]
    out_shape = [jax.ShapeDtypeStruct((S, heads * dq), F32), jax.ShapeDtypeStruct((heads * dq, Sk), F32),
                 jax.ShapeDtypeStruct((heads * dv, Sk), F32)]
    if side_exchange is not None:
        _, R, C = side_exchange.shape
        ins.append(side_exchange)
        in_specs.append(ANY)
        out_specs.append(ANY)
        out_shape.append(jax.ShapeDtypeStruct((N_DEVICES, R // 2, C), side_exchange.dtype))
        scratch += [pltpu.SemaphoreType.DMA((N_DEVICES,)), pltpu.SemaphoreType.DMA((N_DEVICES,))]
    return pl.pallas_call(body, name=name, grid=(heads, nq), in_specs=in_specs, out_specs=out_specs,
                          out_shape=out_shape, scratch_shapes=scratch, compiler_params=_params(2))(*ins)


GELU_C = 0.7978845608028654
assert 2 * SG_GD == LANE and SG_CHUNK == LANE


def _gelu(z):
    t = jnp.tanh(GELU_C * (z + 0.044715 * z * z * z))
    return 0.5 * z * (1.0 + t), t


def _gelu_grad(z, t):
    return 0.5 * (1.0 + t) + 0.5 * z * (1.0 - t * t) * GELU_C * (1.0 + 3.0 * 0.044715 * z * z)


def _layernorm_parts(g):
    d = g - jnp.mean(g, axis=-1, keepdims=True)
    rstd = lax.rsqrt(jnp.mean(d * d, axis=-1, keepdims=True) + EPS)
    return d * rstd, rstd


def _gelu_ln(z, gain, bias, *, name):
    def fn(zv, gn, bs):
        a, _ = _gelu(zv)
        y, _ = _layernorm_parts(a[:, SG_W:])
        return a[:, :SG_W], y * gn + bs

    return _rowwise(fn, [z], [gain.reshape(1, SG_W), bias.reshape(1, SG_W)], [(SG_W, F32), (SG_W, BF16)], name=name)


def _gelu_ln_bwd(z, du, dgl, gain, *, name):
    def fn(zv, duv, dglv, gn):
        a, t = _gelu(zv)
        y, rstd = _layernorm_parts(a[:, SG_W:])
        dy = dglv * gn
        dgg = rstd * (dy - jnp.mean(dy, axis=-1, keepdims=True) - y * jnp.mean(dy * y, axis=-1, keepdims=True))
        dz = jnp.concatenate([duv, dgg], axis=1) * _gelu_grad(zv, t)
        return dz, jnp.sum(dglv * y, axis=0, keepdims=True), jnp.sum(dglv, axis=0, keepdims=True)

    return _rowwise(fn, [z, du, dgl], [gain.reshape(1, SG_W)], [(2 * SG_W, BF16)], [(1, SG_W), (1, SG_W)], name=name)


def _sg_masks():
    tri = lax.broadcasted_iota(jnp.int32, (SG_CHUNK, SG_CHUNK), 0) >= lax.broadcasted_iota(jnp.int32, (SG_CHUNK, SG_CHUNK), 1)
    first = lax.broadcasted_iota(jnp.int32, (SG_CHUNK, LANE), 1) < SG_GD
    return tri, first


def _spatial(gl, u, w, bt, *, name):
    S = gl.shape[0]
    tm = _row_tile(S, 512)
    nch = tm // SG_CHUNK

    def body(gl_ref, u_ref, w_ref, bt_ref, o_ref):
        tri, first = _sg_masks()
        for p in range(SG_W // LANE):
            cols = slice(p * LANE, (p + 1) * LANE)
            wa = jnp.where(tri, w_ref[2 * p], 0.0).astype(BF16)
            wb = jnp.where(tri, w_ref[2 * p + 1], 0.0).astype(BF16)
            for ci in range(nch):
                rws = slice(ci * SG_CHUNK, (ci + 1) * SG_CHUNK)
                g = gl_ref[rws, cols]
                zero = jnp.zeros_like(g)
                mixed = (jnp.dot(wa, jnp.where(first, g, zero), preferred_element_type=F32)
                         + jnp.dot(wb, jnp.where(first, zero, g), preferred_element_type=F32) + bt_ref[:, cols])
                o_ref[rws, cols] = u_ref[rws, cols] * mixed

    row = pl.BlockSpec((tm, SG_W), lambda i: (i, 0))
    return pl.pallas_call(
        body, name=name, grid=(S // tm,),
        in_specs=[row, row, pl.BlockSpec(w.shape, lambda i: (0, 0, 0)), pl.BlockSpec(bt.shape, lambda i: (0, 0))],
        out_specs=row, out_shape=jax.ShapeDtypeStruct((S, SG_W), F32), compiler_params=_params(1))(gl, u, w, bt)


def _spatial_bwd(d_o, gl, u, w, bt, *, name):
    S = gl.shape[0]
    tm = _row_tile(S, 512)
    nch = tm // SG_CHUNK
    nsteps = S // tm

    def body(do_ref, gl_ref, u_ref, w_ref, bt_ref, du_ref, dgl_ref, dw_ref, db_ref, dbt_ref):
        tri, first = _sg_masks()
        step = pl.program_id(0)

        @pl.when(step == 0)
        def _():
            dw_ref[...] = jnp.zeros_like(dw_ref)
            dbt_ref[...] = jnp.zeros_like(dbt_ref)

        for p in range(SG_W // LANE):
            cols = slice(p * LANE, (p + 1) * LANE)
            wa = jnp.where(tri, w_ref[2 * p], 0.0).astype(BF16)
            wb = jnp.where(tri, w_ref[2 * p + 1], 0.0).astype(BF16)
            for ci in range(nch):
                rws = slice(ci * SG_CHUNK, (ci + 1) * SG_CHUNK)
                g = gl_ref[rws, cols]
                zero = jnp.zeros_like(g)
                mixed = (jnp.dot(wa, jnp.where(first, g, zero), preferred_element_type=F32)
                         + jnp.dot(wb, jnp.where(first, zero, g), preferred_element_type=F32) + bt_ref[:, cols])
                dov = do_ref[rws, cols]
                du_ref[rws, cols] = dov * mixed
                dm = dov * u_ref[rws, cols]
                dbt_ref[:, cols] += dm
                dma = jnp.where(first, dm, 0.0).astype(BF16)
                dmb = jnp.where(first, 0.0, dm).astype(BF16)
                dw_ref[2 * p] += jnp.where(tri, _nt(dma, g), 0.0)
                dw_ref[2 * p + 1] += jnp.where(tri, _nt(dmb, g), 0.0)
                dgl_ref[rws, cols] = _tn(wa, dma) + _tn(wb, dmb)

        @pl.when(step == nsteps - 1)
        def _():
            lane = lax.broadcasted_iota(jnp.int32, (SG_CHUNK, LANE), 1)
            acc = jnp.zeros((SG_CHUNK, LANE), F32)
            for p in range(SG_W // LANE):
                blk = dbt_ref[:, p * LANE:(p + 1) * LANE]
                sa = jnp.sum(jnp.where(first, blk, 0.0), axis=1, keepdims=True)
                sb_ = jnp.sum(jnp.where(first, 0.0, blk), axis=1, keepdims=True)
                acc = acc + jnp.where(lane == 2 * p, sa, 0.0) + jnp.where(lane == 2 * p + 1, sb_, 0.0)
            db_ref[...] = acc

    row = pl.BlockSpec((tm, SG_W), lambda i: (i, 0))
    return pl.pallas_call(
        body, name=name, grid=(nsteps,),
        in_specs=[row, row, row, pl.BlockSpec(w.shape, lambda i: (0, 0, 0)), pl.BlockSpec(bt.shape, lambda i: (0, 0))],
        out_specs=[row, row, pl.BlockSpec(w.shape, lambda i: (0, 0, 0)), pl.BlockSpec((SG_CHUNK, LANE), lambda i: (0, 0))],
        out_shape=[jax.ShapeDtypeStruct((S, SG_W), F32), jax.ShapeDtypeStruct((S, SG_W), F32),
                   jax.ShapeDtypeStruct(w.shape, F32), jax.ShapeDtypeStruct((SG_CHUNK, LANE), F32)],
        scratch_shapes=[pltpu.VMEM((SG_CHUNK, SG_W), F32)], compiler_params=_params(1))(d_o, gl, u, w, bt)


ROPE_HALF = MLA_ROPE // 2
KR_COL = (MLA_QL + MLA_KVL) // LANE
MLA_IN_PAD = MLA_QL + MLA_KVL + LANE


def _rope_tables(positions):
    inv_freq = ROPE_THETA ** (-jnp.arange(ROPE_HALF, dtype=F32) / ROPE_HALF)
    ang = positions.astype(F32)[:, None] * inv_freq
    cos, sin = jnp.cos(ang), jnp.sin(ang)
    S = positions.shape[0]
    z16, tail = jnp.zeros((S, ROPE_HALF), F32), jnp.zeros((S, LANE - MLA_QK), F32)
    ones = jnp.ones((S, MLA_NOPE), F32)
    zeros = jnp.zeros((S, MLA_NOPE), F32)
    return (jnp.concatenate([ones, cos, cos, tail], axis=1), jnp.concatenate([zeros, z16, sin, tail], axis=1),
            jnp.concatenate([zeros, -sin, z16, tail], axis=1))


def _rope(x, cos, sa, sb):
    return x * cos + pltpu.roll(x, ROPE_HALF, 1) * sa + pltpu.roll(x, LANE - ROPE_HALF, 1) * sb


def _rope_t(dy, cos, sa, sb):
    return dy * cos + pltpu.roll(dy * sa, LANE - ROPE_HALF, 1) + pltpu.roll(dy * sb, ROPE_HALF, 1)


def _mla_lora(P, qlg, kvlg, *, name):
    def fn(pv, a, b):
        return _rmsnorm_fwd(pv[:, :MLA_QL], a), _rmsnorm_fwd(pv[:, MLA_QL:MLA_QL + MLA_KVL], b)

    return _rowwise(fn, [P], [qlg.reshape(1, MLA_QL), kvlg.reshape(1, MLA_KVL)], [(MLA_QL, BF16), (MLA_KVL, BF16)], name=name)


def _mla_lora_bwd(dcq, dckv, dkr, P, qlg, kvlg, *, name):
    def fn(d1, d2, d3, pv, a, b):
        x1, g1 = _rmsnorm_bwd(d1, pv[:, :MLA_QL], a)
        x2, g2 = _rmsnorm_bwd(d2, pv[:, MLA_QL:MLA_QL + MLA_KVL], b)
        return jnp.concatenate([x1, x2, d3], axis=1), g1, g2

    return _rowwise(fn, [dcq, dckv, dkr, P], [qlg.reshape(1, MLA_QL), kvlg.reshape(1, MLA_KVL)], [(MLA_IN_PAD, BF16)],
                    [(1, MLA_QL), (1, MLA_KVL)], name=name)


def _mla_qk(q_pre, k_pre, P, tabs, qg, kg, *, name):
    def fn(qp, kp, kr, c, a, b, qgv, kgv):
        return (_rope(_rmsnorm_fwd(qp, qgv, MLA_QK), c, a, b) * MLA_SCALE,
                _rope(_rmsnorm_fwd(kp + kr, kgv, MLA_QK), c, a, b))

    hcol = lambda h: h
    rows = [(q_pre, LANE, hcol), (k_pre, LANE, hcol), (P, LANE, lambda h: KR_COL), *tabs]
    w = MLA_HEADS * LANE
    return _rowwise(fn, rows, [qg, kg], [(w, BF16), (w, BF16)], heads=MLA_HEADS, tm=HEAD_ROWS, name=name)


def _mla_qk_bwd(dq, dk_t, q_pre, k_pre, P, tabs, qg, kg, *, name):
    def fn(dqv, dkv, qp, kp, kr, c, a, b, qgv, kgv):
        dqp, dqg = _rmsnorm_bwd(_rope_t(dqv * MLA_SCALE, c, a, b), qp, qgv, MLA_QK)
        dkp, dkg = _rmsnorm_bwd(_rope_t(dkv, c, a, b), kp + kr, kgv, MLA_QK)
        lane = lax.broadcasted_iota(jnp.int32, (1, LANE), 1)
        return dqp, dkp, jnp.where((lane >= MLA_NOPE) & (lane < MLA_QK), dkp, 0.0), dqg, dkg

    hcol = lambda h: h
    rows = [(dq, LANE, hcol), (dk_t, LANE, hcol, True), (q_pre, LANE, hcol), (k_pre, LANE, hcol),
            (P, LANE, lambda h: KR_COL), *tabs]
    w = MLA_HEADS * LANE
    return _rowwise(fn, rows, [qg, kg], [(w, BF16), (w, BF16)], [(1, LANE), (1, LANE)], [LANE], heads=MLA_HEADS,
                    tm=HEAD_ROWS, name=name)


def _head_norm(x, g, *, heads, width, colfn=None, scale=1.0, name):
    return _rowwise(lambda xv, gv: _rmsnorm_fwd(xv, gv) * scale, [(x, width, colfn or (lambda h: h))],
                    [g.reshape(1, width)], [(heads * width, BF16)], heads=heads, tm=HEAD_ROWS, name=name)[0]


def _head_norm_bwd(dy, x, g, *, heads, width, colfn=None, scale=1.0, out_dtype, name):
    return _rowwise(lambda dv_, xv, gv: _rmsnorm_bwd(dv_ * scale, xv, gv),
                    [(dy, width, lambda h: h), (x, width, colfn or (lambda h: h))],
                    [g.reshape(1, width)], [(heads * width, out_dtype)], [(1, width)], heads=heads, tm=HEAD_ROWS,
                    name=name)


def _loss_grad(y, tgt, *, name):
    D = y.shape[1]

    def fn(yv, tv):
        d = yv - tv
        return d * (1.0 / D), jnp.sum(d * d, axis=0, keepdims=True) * (0.5 / D)

    dy, part = _rowwise(fn, [y, tgt], [], [(D, F32)], [(1, D)], name=name)
    return jnp.sum(part), dy


def _adamw(w, g, m, v, *, name):
    shape = w.shape
    two_d = (-1, shape[-1])

    def fn(wv, gv, mv, vv):
        m2 = ADAM_B1 * mv + (1.0 - ADAM_B1) * gv
        v2 = ADAM_B2 * vv + (1.0 - ADAM_B2) * (gv * gv)
        m_hat = m2 / (1.0 - ADAM_B1 ** ADAM_STEP)
        v_hat = v2 / (1.0 - ADAM_B2 ** ADAM_STEP)
        return -ADAM_LR * (m_hat / (jnp.sqrt(v_hat) + ADAM_EPS) + ADAM_WD * wv), m2, v2

    outs = _rowwise(fn, [t.reshape(two_d) for t in (w, g, m, v)], [], [(shape[-1], F32)] * 3, name=name)
    return [o.reshape(shape) for o in outs]


def _pad_cols(w, heads, hd):
    k = w.shape[0]
    return jnp.pad(w.reshape(k, heads, hd), ((0, 0), (0, 0), (0, LANE - hd))).reshape(k, heads * LANE)


def _unpad_cols(w, heads, hd):
    k = w.shape[0]
    return w.reshape(k, heads, LANE)[:, :, :hd].reshape(k, heads * hd)


def _pad_rows(w, heads, hd):
    n = w.shape[1]
    return jnp.pad(w.reshape(heads, hd, n), ((0, 0), (0, LANE - hd), (0, 0))).reshape(heads * LANE, n)


def _unpad_rows(w, heads, hd):
    n = w.shape[1]
    return w.reshape(heads, LANE, n)[:, :hd, :].reshape(heads * hd, n)


def _ffn_fwd(x, g, wgu, wd, tag):
    h = _norm_rows(x, g, name=tag + "_norm")
    gate, up, act = _mm_swiglu(h, wgu, name=tag + "_gu")
    y = _mm(act, wd, scale=0.5, residual=x, name=tag + "_down")
    return y, (x, h, gate, up, act)


def _ffn_bwd(dy, saved, g, wgu, wd, tag, last_partials=None):
    x, h, gate, up, act = saved
    F = wd.shape[0]
    dwd = _mm(act, dy, ta=True, scale=0.5, name=tag + "_dwd")
    dgate, dup = _mm_dswiglu(dy, wd, gate, up, scale=0.5, name=tag + "_dact")
    dwgu = jnp.concatenate([_mm(h, dgate, ta=True, name=tag + "_dwg"), _mm(h, dup, ta=True, name=tag + "_dwu")], axis=1)
    dh = _mm(dgate, wgu, tb=True, name=tag + "_dh_g")
    outs = _mm(dup, wgu, tb=True, b_off=(0, F), residual=dh, norm_bwd=(x, g, dy),
               side_exchange=None if last_partials is None else last_partials(dwgu, dwd), name=tag + "_dh_u")
    return (outs[0], outs[1], dwgu, dwd) + tuple(outs[2:])


def _even_weights(w_in, w_out):
    parts = [w_in[:, :SB_W] * SB_SCALE, w_in[:, SB_W:2 * SB_W], w_in[:, 2 * SB_W:3 * SB_W]]
    wqkv = jnp.concatenate([_pad_cols(p, SB_HEADS, SB_HD) for p in parts], axis=1)
    return wqkv, w_in[:, 3 * SB_W:], _pad_rows(w_out[:SB_W], SB_HEADS, SB_HD), w_out[SB_W:]


def _even_fwd(x, g, wts, ln_g, ln_b, sgu_w, bt, late_shard, tag):
    wqkv, wz, wo_sb, wo_sg = wts
    h = _norm_rows(x, g, name=tag + "_norm")
    qkv = _mm(h, wqkv, out_dtype=BF16, name=tag + "_qkv")
    z = _mm(h, wz, name=tag + "_z")
    o_sb, tot, late = _attn_fwd(qkv, qkv, qkv, sb=True, causal=True, heads=SB_HEADS, dq=LANE, dv=LANE, group=FWD_GROUP,
                                kcol=lambda g: SB_HEADS // FWD_GROUP + g, vcol=lambda g: 2 * SB_HEADS // FWD_GROUP + g,
                                side_gather=late_shard, name=tag + "_sb")
    u, gl = _gelu_ln(z, ln_g, ln_b, name=tag + "_geluln")
    o_sg = _spatial(gl, u, sgu_w, bt, name=tag + "_sgu")
    y = _mm(o_sb, wo_sb, residual=x, name=tag + "_out_sb")
    y = _mm(o_sg, wo_sg, residual=y, name=tag + "_out_sg")
    return y, (x, h, qkv, z, o_sb, tot, u, gl, o_sg), late


def _even_bwd(dy, saved, g, wts, ln_g, sgu_w, bt, partials, tag):
    wqkv, wz, wo_sb, wo_sg = wts
    x, h, qkv, z, o_sb, tot, u, gl, o_sg = saved
    do_sb = _mm(dy, wo_sb, tb=True, name=tag + "_do_sb")
    do_sg = _mm(dy, wo_sg, tb=True, name=tag + "_do_sg")
    dwo = jnp.concatenate([_unpad_rows(_mm(o_sb, dy, ta=True, name=tag + "_dwo_sb"), SB_HEADS, SB_HD),
                           _mm(o_sg, dy, ta=True, name=tag + "_dwo_sg")], axis=0)
    dq, dk_t, dv_t, received = _attn_bwd(qkv, qkv, qkv, o_sb, do_sb, tot, sb=True, causal=True, heads=SB_HEADS, dq=LANE,
                                         dv=LANE, kcol=lambda hh: SB_HEADS + hh, vcol=lambda hh: 2 * SB_HEADS + hh,
                                         side_exchange=partials, name=tag + "_sb_bwd")
    du, dgl, dsgu_w, db_t = _spatial_bwd(do_sg, gl, u, sgu_w, bt, name=tag + "_sgu_bwd")
    dz, dln_g, dln_b = _gelu_ln_bwd(z, du, dgl, ln_g, name=tag + "_geluln_bwd")
    dh = _mm(dz, wz, tb=True, name=tag + "_dh_z")
    dh = _mm(dq, wqkv, tb=True, residual=dh, name=tag + "_dh_q")
    dws = [_unpad_cols(_mm(h, dq, ta=True, scale=SB_SCALE, name=tag + "_dw_q"), SB_HEADS, SB_HD)]
    for i, (d_t, nm) in enumerate(((dk_t, "k"), (dv_t, "v")), start=1):
        dh = _mm(d_t, wqkv, ta=True, tb=True, b_off=(0, i * SB_HEADS * LANE), residual=dh,
                 norm_bwd=(x, g, dy) if nm == "v" else None, name=tag + "_dh_" + nm)
        dws.append(_unpad_rows(_mm(d_t, h, name=tag + "_dw_" + nm), SB_HEADS, SB_HD).T)
    dws.append(_mm(h, dz, ta=True, name=tag + "_dw_z"))
    dx, dg = dh
    return dx, dict(mix_norm=dg, sbg_w_in=jnp.concatenate(dws, axis=1), sgu_ln_gain=dln_g, sgu_ln_bias=dln_b,
                    sgu_w=dsgu_w, sgu_b=db_t[:, :SG_GROUPS].T, sbg_w_out=dwo), received


def _mla_weights(w_in, w_uq, w_ukv, w_out, q_gain, k_gain):
    d = w_in.shape[0]
    lat = MLA_QL + MLA_KVL
    w_in_ext = jnp.concatenate([w_in[:, :lat], jnp.zeros((d, MLA_NOPE), w_in.dtype), w_in[:, lat:],
                                jnp.zeros((d, LANE - MLA_QK), w_in.dtype)], axis=1)
    kv = w_ukv.reshape(MLA_KVL, MLA_HEADS, MLA_NOPE + MLA_V)
    wk = _pad_cols(kv[:, :, :MLA_NOPE].reshape(MLA_KVL, -1), MLA_HEADS, MLA_NOPE)
    wv = _pad_cols(kv[:, :, MLA_NOPE:].reshape(MLA_KVL, -1), MLA_HEADS, MLA_V)
    pad_gain = lambda gn: jnp.pad(gn.reshape(1, MLA_QK), ((0, 0), (0, LANE - MLA_QK)))
    return (w_in_ext, _pad_cols(w_uq, MLA_HEADS, MLA_QK), wk, wv, _pad_rows(w_out, MLA_HEADS, MLA_V),
            pad_gain(q_gain), pad_gain(k_gain))


def _mla_fwd(x, g, wts, qlg, kvlg, tabs, tag):
    w_in, w_uq, wk, wv, w_out, qg, kg = wts
    h = _norm_rows(x, g, name=tag + "_norm")
    P = _mm(h, w_in, name=tag + "_in")
    cqn, ckvn = _mla_lora(P, qlg, kvlg, name=tag + "_lora")
    q_pre = _mm(cqn, w_uq, name=tag + "_uq")
    k_pre = _mm(ckvn, wk, name=tag + "_uk")
    ones_lane = jnp.tile((jnp.arange(LANE) == MLA_V).astype(F32), MLA_HEADS)[None, :]
    v = _mm(ckvn, wv, out_dtype=BF16, bias=ones_lane, name=tag + "_uv")
    q, k = _mla_qk(q_pre, k_pre, P, tabs, qg, kg, name=tag + "_qk")
    o, lse = _attn_fwd(q, k, v, sb=False, causal=True, heads=MLA_HEADS, dq=LANE, dv=LANE, group=FWD_GROUP,
                       sum_lane=MLA_V, name=tag + "_attn")
    y = _mm(o, w_out, residual=x, name=tag + "_out")
    return y, (x, h, P, cqn, ckvn, q_pre, k_pre, q, k, v, o, lse)


def _mla_bwd(dy, saved, g, wts, qlg, kvlg, tabs, tag):
    w_in, w_uq, wk, wv, w_out, qg, kg = wts
    x, h, P, cqn, ckvn, q_pre, k_pre, q, k, v, o, lse = saved
    do = _mm(dy, w_out, tb=True, name=tag + "_do")
    dw_out = _unpad_rows(_mm(o, dy, ta=True, name=tag + "_dwo"), MLA_HEADS, MLA_V)
    dq, dk_t, dv_t = _attn_bwd(q, k, v, o, do, lse, sb=False, causal=True, heads=MLA_HEADS, dq=LANE, dv=LANE,
                               name=tag + "_attn_bwd")
    dq_pre, dk_pre, dkr, dqg, dkg = _mla_qk_bwd(dq, dk_t, q_pre, k_pre, P, tabs, qg, kg, name=tag + "_qk_bwd")
    dcqn = _mm(dq_pre, w_uq, tb=True, name=tag + "_dcq")
    dckvn = _mm(dk_pre, wk, tb=True, name=tag + "_dckv_k")
    dckvn = _mm(dv_t, wv, ta=True, tb=True, residual=dckvn, name=tag + "_dckv_v")
    dw_uq = _unpad_cols(_mm(cqn, dq_pre, ta=True, name=tag + "_dwuq"), MLA_HEADS, MLA_QK)
    dwk = _unpad_cols(_mm(ckvn, dk_pre, ta=True, name=tag + "_dwk"), MLA_HEADS, MLA_NOPE)
    dwv = _unpad_rows(_mm(dv_t, ckvn, name=tag + "_dwv"), MLA_HEADS, MLA_V).T
    dw_ukv = jnp.concatenate([dwk.reshape(MLA_KVL, MLA_HEADS, MLA_NOPE), dwv.reshape(MLA_KVL, MLA_HEADS, MLA_V)],
                             axis=2).reshape(MLA_KVL, -1)
    dP, dqlg, dkvlg = _mla_lora_bwd(dcqn, dckvn, dkr, P, qlg, kvlg, name=tag + "_lora_bwd")
    dx, dg = _mm(dP, w_in, tb=True, norm_bwd=(x, g, dy), name=tag + "_dh")
    dw_in_ext = _mm(h, dP, ta=True, name=tag + "_dwin")
    lat = MLA_QL + MLA_KVL
    dw_in = jnp.concatenate([dw_in_ext[:, :lat], dw_in_ext[:, lat + MLA_NOPE:lat + MLA_QK]], axis=1)
    return dx, dict(mix_norm=dg, mla_w_in=dw_in, mla_q_lora_gain=dqlg, mla_kv_lora_gain=dkvlg, mla_w_uq=dw_uq,
                    mla_w_ukv=dw_ukv, mla_q_gain=dqg[:, :MLA_QK], mla_k_gain=dkg[:, :MLA_QK], mla_w_out=dw_out)


def _xmem_fwd(x, mem, g, gm, wq, wkv, qg, kg, wo, tag):
    hq = _norm_rows(x, g, name=tag + "_norm")
    hm = _norm_rows(mem, gm, name=tag + "_mnorm")
    qp = _mm(hq, wq, name=tag + "_q")
    kv = _mm(hm, wkv, name=tag + "_kv")
    q = _head_norm(qp, qg, heads=MEM_HEADS, width=MEM_HD, scale=MEM_SCALE, name=tag + "_qn")
    kn = _head_norm(kv, kg, heads=MEM_HEADS, width=MEM_HD, colfn=lambda hh: 2 * hh, name=tag + "_kn")
    kvb = kv.reshape(-1, MEM_HEADS, 2, MEM_HD)[:, :, 1].reshape(-1, MEM_HEADS * MEM_HD).astype(BF16)
    o, lse = _attn_fwd(q, kn, kvb, sb=False, causal=False, heads=MEM_HEADS, dq=MEM_HD, dv=MEM_HD, group=MEM_HEADS,
                       name=tag + "_attn")
    y = _mm(o, wo, residual=x, name=tag + "_out")
    return y, (x, hq, hm, qp, kv, q, kn, kvb, o, lse)


def _xmem_bwd(dy, saved, mem, g, gm, wq, wkv, qg, kg, wo, tag):
    x, hq, hm, qp, kv, q, kn, kvb, o, lse = saved
    m = mem.shape[0]
    do = _mm(dy, wo, tb=True, name=tag + "_do")
    dwo = _mm(o, dy, ta=True, name=tag + "_dwo")
    dq, dk_t, dv_t = _attn_bwd(q, kn, kvb, o, do, lse, sb=False, causal=False, heads=MEM_HEADS, dq=MEM_HD, dv=MEM_HD,
                               name=tag + "_attn_bwd")
    dk, dv = dk_t.T, dv_t.T
    dqp, dqg = _head_norm_bwd(dq, qp, qg, heads=MEM_HEADS, width=MEM_HD, scale=MEM_SCALE, out_dtype=BF16,
                              name=tag + "_qn_bwd")
    dkp, dkg = _head_norm_bwd(dk, kv, kg, heads=MEM_HEADS, width=MEM_HD, colfn=lambda hh: 2 * hh, out_dtype=F32,
                              name=tag + "_kn_bwd")
    dkv = jnp.concatenate([dkp.reshape(m, MEM_HEADS, MEM_HD), dv.reshape(m, MEM_HEADS, MEM_HD)], axis=2).reshape(m, -1)
    dwkv = _mm(hm, dkv, ta=True, name=tag + "_dwkv")
    dhm = _mm(dkv, wkv, tb=True, name=tag + "_dhm")
    _, dgm = _norm_rows_bwd(dhm, mem, gm, None, name=tag + "_dmnorm")
    dwq = _mm(hq, dqp, ta=True, name=tag + "_dwq")
    dx, dg = _mm(dqp, wq, tb=True, norm_bwd=(x, g, dy), name=tag + "_dhq")
    return dx, dict(xmem_norm=dg, xmem_mem_norm=dgm, xmem_wq=dwq, xmem_wkv=dwkv, xmem_q_gain=dqg, xmem_k_gain=dkg,
                    xmem_wo=dwo)


def _local_step(x, mem, positions, tgt, w, late_shard, finish_late, partials_so_far):
    tabs = _rope_tables(positions)
    even = _even_weights(w["sbg_w_in"][0], w["sbg_w_out"][0])
    bt = jnp.repeat(w["sgu_b"][0].T, SG_GD, axis=1)
    saved = []
    for l in range(2):
        t = f"l{l}"
        x, s_pre = _ffn_fwd(x, w["ffn_pre_norm"][l], w["ffn_pre_w_gu"][l], w["ffn_pre_w_down"][l], t + "_pre")
        if l == 0:
            x, s_mix, late = _even_fwd(x, w["mix_norm"][0], even, w["sgu_ln_gain"][0], w["sgu_ln_bias"][0], w["sgu_w"][0],
                                       bt, late_shard, t + "_even")
            finish_late(late)
            mla = _mla_weights(w["mla_w_in"][0], w["mla_w_uq"][0], w["mla_w_ukv"][0], w["mla_w_out"][0],
                               w["mla_q_gain"][0], w["mla_k_gain"][0])
        else:
            x, s_mix = _mla_fwd(x, w["mix_norm"][1], mla, w["mla_q_lora_gain"][0], w["mla_kv_lora_gain"][0], tabs,
                                t + "_mla")
        x, s_xm = _xmem_fwd(x, mem, w["xmem_norm"][l], w["xmem_mem_norm"][l], w["xmem_wq"][l], w["xmem_wkv"][l],
                            w["xmem_q_gain"][l], w["xmem_k_gain"][l], w["xmem_wo"][l], t + "_xm")
        x, s_post = _ffn_fwd(x, w["ffn_post_norm"][l], w["ffn_post_w_gu"][l], w["ffn_post_w_down"][l], t + "_post")
        saved.append((s_pre, s_mix, s_xm, s_post))
    loss, dx = _loss_grad(x, tgt, name="loss")
    grads = {}

    def put(name, l, val):
        grads.setdefault(name, {})[l] = val

    for l in (1, 0):
        t = f"l{l}"
        s_pre, s_mix, s_xm, s_post = saved[l]
        dx, dg, dwgu, dwd = _ffn_bwd(dx, s_post, w["ffn_post_norm"][l], w["ffn_post_w_gu"][l], w["ffn_post_w_down"][l],
                                     t + "_post")
        put("ffn_post_norm", l, dg), put("ffn_post_w_gu", l, dwgu), put("ffn_post_w_down", l, dwd)
        dx, gx = _xmem_bwd(dx, s_xm, mem, w["xmem_norm"][l], w["xmem_mem_norm"][l], w["xmem_wq"][l], w["xmem_wkv"][l],
                           w["xmem_q_gain"][l], w["xmem_k_gain"][l], w["xmem_wo"][l], t + "_xm")
        for k_, v_ in gx.items():
            put(k_, l, v_)
        if l == 0:
            dx, gm, received = _even_bwd(dx, s_mix, w["mix_norm"][0], even, w["sgu_ln_gain"][0], w["sgu_w"][0], bt,
                                         partials_so_far(grads, last=False), t + "_even")
        else:
            dx, gm = _mla_bwd(dx, s_mix, w["mix_norm"][1], mla, w["mla_q_lora_gain"][0], w["mla_kv_lora_gain"][0], tabs,
                              t + "_mla")
        for k_, v_ in gm.items():
            put(k_, l if k_ == "mix_norm" else 0, v_)
        def last_partials(dwgu, dwd, l=l):
            put("ffn_pre_w_gu", l, dwgu), put("ffn_pre_w_down", l, dwd)
            return partials_so_far(grads, last=True)

        dx, dg, dwgu, dwd, *received_last = _ffn_bwd(dx, s_pre, w["ffn_pre_norm"][l], w["ffn_pre_w_gu"][l],
                                                     w["ffn_pre_w_down"][l], t + "_pre", last_partials if l == 0 else None)
        put("ffn_pre_norm", l, dg), put("ffn_pre_w_gu", l, dwgu), put("ffn_pre_w_down", l, dwd)
    return loss, dx, {k_: [v_[l] for l in sorted(v_)] for k_, v_ in grads.items()}, received, received_last[0]


N_CHIPS = 4
PACK_COLS = 1024
PACK_ROW_MULTIPLE = 512


def _place():
    x, y, c = lax.axis_index("x"), lax.axis_index("y"), lax.axis_index("c")
    return x, y, c, [(1 - x, y), (x, 1 - y), (1 - x, 1 - y)]


def _hops(x, y, c):
    return ((x + 1 - c) % 2, (y + c) % 2), ((x + c) % 2, (y + 1 - c) % 2), (1 - x, 1 - y)


GATHER_COPIES = 6
GATHER_STAGES = 4


def _gather_stages(x_ref, out_ref, send_sems, recv_sems):
    Rh = x_ref.shape[0] // 2
    x, y, c = lax.axis_index("x"), lax.axis_index("y"), lax.axis_index("c")
    n1, n2, nd = _hops(x, y, c)
    me, q1, q2, qd = 2 * x + y, 2 * n1[0] + n1[1], 2 * n2[0] + n2[1], 2 * nd[0] + nd[1]
    sibling = (x, y, 1 - c)

    def half(chip, core):
        return out_ref.at[chip, pl.ds(core * Rh, Rh), :]

    def copy(k, chip, core, to, own=False):
        return pltpu.make_async_remote_copy(src_ref=x_ref.at[pl.ds(c * Rh, Rh), :] if own else half(chip, core),
                                            dst_ref=half(chip, core), send_sem=send_sems.at[k], recv_sem=recv_sems.at[k],
                                            device_id=to, device_id_type=MESH)

    sends = [lambda: copy(0, me, c, (*n1, c), own=True), lambda: copy(1, me, c, (*n2, c), own=True),
             lambda: copy(2, q1, c, (*n2, c)), lambda: copy(3, q1, c, sibling), lambda: copy(4, q2, c, sibling),
             lambda: copy(5, qd, c, sibling)]

    def own_halves_out():
        sends[0]().start()
        sends[1]().start()

    def first_neighbours_on():
        copy(0, q1, c, sibling).wait_recv()
        sends[2]().start()
        sends[3]().start()

    def others_to_sibling():
        copy(1, q2, c, sibling).wait_recv()
        sends[4]().start()
        copy(2, qd, c, sibling).wait_recv()
        sends[5]().start()

    def all_landed():
        copy(3, q2, 1 - c, sibling).wait_recv()
        copy(4, q1, 1 - c, sibling).wait_recv()
        copy(5, qd, 1 - c, sibling).wait_recv()
        for send in sends:
            send().wait_send()

    return own_halves_out, first_neighbours_on, others_to_sibling, all_landed


N_DEVICES = 8


def _partials_exchange(g_ref, recv_ref, send_sems, recv_sems):
    Rh = g_ref.shape[1] // 2
    x, y, c = lax.axis_index("x"), lax.axis_index("y"), lax.axis_index("c")

    def copy(k):
        tx, ty, tc = (x + (k >> 2)) % 2, (y + ((k >> 1) & 1)) % 2, (c + (k & 1)) % 2
        return pltpu.make_async_remote_copy(src_ref=g_ref.at[2 * tx + ty, pl.ds(tc * Rh, Rh), :],
                                            dst_ref=recv_ref.at[4 * x + 2 * y + c], send_sem=send_sems.at[k],
                                            recv_sem=recv_sems.at[k], device_id=(tx, ty, tc), device_id_type=MESH)

    def start():
        for k in range(1, N_DEVICES):
            copy(k).start()

    def finish():
        for k in range(1, N_DEVICES):
            copy(k).wait_recv()
            copy(k).wait_send()

    return start, finish


def _sum_partials(g, recv, *, name):
    _, R, C = g.shape
    Rh = R // 2
    tr = _row_tile(Rh, 512)
    nt = Rh // tr
    x, y, c = lax.axis_index("x"), lax.axis_index("y"), lax.axis_index("c")
    where = jnp.stack([2 * x + y, c, 4 * x + 2 * y + c]).astype(jnp.int32)

    def body(where_ref, g_ref, r_ref, o_ref):
        own, mine = g_ref[0], where_ref[2]
        total = None
        for d in range(N_DEVICES):
            term = jnp.where(mine == d, own, r_ref[d].astype(F32))
            total = term if total is None else total + term
        o_ref[...] = total

    spec = pltpu.PrefetchScalarGridSpec(
        num_scalar_prefetch=1, grid=(nt,),
        in_specs=[pl.BlockSpec((1, tr, C), lambda i, wh: (wh[0], wh[1] * nt + i, 0)),
                  pl.BlockSpec((N_DEVICES, tr, C), lambda i, wh: (0, i, 0))],
        out_specs=pl.BlockSpec((tr, C), lambda i, wh: (i, 0)))
    return pl.pallas_call(body, name=name, grid_spec=spec, out_shape=jax.ShapeDtypeStruct((Rh, C), F32),
                          compiler_params=_params(1))(where, g, recv)


def _place_own_slot(others, shard):
    return lax.dynamic_update_slice(others, shard[None], (2 * lax.axis_index("x") + lax.axis_index("y"), 0, 0))


def _gather_chips(shard):
    def body(x_ref, out_ref, send_sems, recv_sems):
        for stage in _gather_stages(x_ref, out_ref, send_sems, recv_sems):
            stage()

    others = pl.pallas_call(
        body, name="gather_weights", out_shape=jax.ShapeDtypeStruct((N_CHIPS,) + shard.shape, shard.dtype),
        in_specs=[ANY], out_specs=ANY,
        scratch_shapes=[pltpu.SemaphoreType.DMA((GATHER_COPIES,)), pltpu.SemaphoreType.DMA((GATHER_COPIES,))])(shard)
    return _place_own_slot(others, shard)


def _gather_devices(block):
    M, N = block.shape

    def body(x_ref, out_ref, send_sems, recv_sems, local_sem):
        x, y, c, chips = _place()
        me, sibling = (x, y, c), (x, y, 1 - c)

        def rows(px, py, pc):
            return out_ref.at[pl.ds((4 * px + 2 * py + pc) * M, M), :]

        def copy(k, blk, to, src=None):
            return pltpu.make_async_remote_copy(src_ref=rows(*blk) if src is None else src, dst_ref=rows(*blk),
                                                send_sem=send_sems.at[k], recv_sem=recv_sems.at[k], device_id=to,
                                                device_id_type=MESH)

        mine = pltpu.make_async_copy(x_ref, rows(*me), local_sem)
        mine.start()
        first = [copy(0, me, sibling, src=x_ref)]
        first += [copy(1 + j, me, (*chip, c), src=x_ref) for j, chip in enumerate(chips)]
        for cp in first:
            cp.start()
        passed = [copy(4 + j, (*chip, c), sibling) for j, chip in enumerate(chips)]
        for j, chip in enumerate(chips):
            copy(1 + j, (*chip, c), me).wait_recv()
            passed[j].start()
        copy(0, sibling, me).wait_recv()
        for j, chip in enumerate(chips):
            copy(4 + j, (*chip, 1 - c), me).wait_recv()
        for cp in first + passed:
            cp.wait_send()
        mine.wait()

    vmem = pl.BlockSpec(memory_space=pltpu.VMEM)
    return pl.pallas_call(
        body, name=f"gather_devices_{M}", out_shape=jax.ShapeDtypeStruct((8 * M, N), block.dtype),
        in_specs=[vmem], out_specs=vmem,
        scratch_shapes=[pltpu.SemaphoreType.DMA((7,)), pltpu.SemaphoreType.DMA((7,)), pltpu.SemaphoreType.DMA],
        compiler_params=pltpu.CompilerParams(vmem_limit_bytes=VMEM_LIMIT))(block)


def _sum_slots(b, *, name):
    n, R, C = b.shape
    tr = _row_tile(R, 512)

    def body(b_ref, o_ref):
        acc = b_ref[0]
        for q in range(1, n):
            acc = acc + b_ref[q]
        o_ref[...] = acc

    return pl.pallas_call(body, name=name, grid=(R // tr,), in_specs=[pl.BlockSpec((n, tr, C), lambda i: (0, i, 0))],
                          out_specs=pl.BlockSpec((tr, C), lambda i: (i, 0)), out_shape=jax.ShapeDtypeStruct((R, C), F32),
                          compiler_params=_params(1))(b)


def _join_halves(r):
    Rh, C = r.shape

    def body(r_ref, o_ref, send_sem, recv_sem):
        x, y, c, _ = _place()
        own, other = o_ref.at[pl.ds(c * Rh, Rh), :], o_ref.at[pl.ds((1 - c) * Rh, Rh), :]
        cp = pltpu.make_async_remote_copy(src_ref=r_ref, dst_ref=own, send_sem=send_sem, recv_sem=recv_sem,
                                          device_id=(x, y, 1 - c), device_id_type=MESH)
        cp.start()
        pltpu.make_async_remote_copy(src_ref=r_ref, dst_ref=other, send_sem=send_sem, recv_sem=recv_sem,
                                     device_id=(x, y, 1 - c), device_id_type=MESH).wait_recv()
        cp.wait_send()

    theirs = pl.pallas_call(
        body, name="grad_join_halves", out_shape=jax.ShapeDtypeStruct((2 * Rh, C), r.dtype), in_specs=[ANY], out_specs=ANY,
        scratch_shapes=[pltpu.SemaphoreType.DMA, pltpu.SemaphoreType.DMA])(r)
    return lax.dynamic_update_slice(theirs, r, (lax.axis_index("c") * Rh, 0))


def _size(shape):
    size = 1
    for d in shape:
        size *= d
    return size


def _pack(pieces, cols, row_multiple, dtype):
    if any(p.size % cols for p in pieces):
        flat = jnp.concatenate([p.reshape(-1).astype(dtype) for p in pieces])
        pieces = [jnp.pad(flat, (0, -flat.shape[0] % cols))]
    rows = [p.reshape(-1, cols).astype(dtype) for p in pieces]
    pad = -sum(r.shape[0] for r in rows) % row_multiple
    return jnp.concatenate(rows + ([jnp.zeros((pad, cols), dtype)] if pad else []), axis=0)


def _unpack(buf, shapes):
    cols = buf.shape[1]
    if any(_size(s) % cols for s in shapes):
        flat, out, at = buf.reshape(-1), [], 0
        for shp in shapes:
            out.append(flat[at:at + _size(shp)].reshape(shp))
            at += _size(shp)
        return out
    out, at = [], 0
    for shp in shapes:
        out.append(buf[at:at + _size(shp) // cols].reshape(shp))
        at += _size(shp) // cols
    return out


SHARDED = (("ffn_pre_w_gu", 2), ("ffn_pre_w_down", 1), ("sbg_w_in", 2), ("sbg_w_out", 1), ("mla_w_in", 1),
           ("mla_w_uq", 2), ("mla_w_ukv", 2), ("mla_w_out", 1), ("xmem_wq", 1), ("xmem_wkv", 2), ("xmem_wo", 1),
           ("ffn_post_w_gu", 2), ("ffn_post_w_down", 1))
EARLY = (("ffn_pre_w_gu", 0), ("ffn_pre_w_down", 0), ("sbg_w_in", 0), ("sbg_w_out", 0))
LORA_GAINS = ("mla_q_lora_gain", "mla_kv_lora_gain")
REPLICATED = ("ffn_pre_norm", "mix_norm", "sgu_ln_gain", "sgu_ln_bias", "sgu_w", "sgu_b", "mla_q_gain", "mla_k_gain",
              "xmem_norm", "xmem_mem_norm", "xmem_q_gain", "xmem_k_gain", "ffn_post_norm")
WEIGHTS = ("ffn_pre_norm", "ffn_pre_w_gu", "ffn_pre_w_down", "mix_norm", "sbg_w_in", "sgu_ln_gain", "sgu_ln_bias", "sgu_w",
           "sgu_b", "sbg_w_out", "mla_w_in", "mla_q_lora_gain", "mla_kv_lora_gain", "mla_w_uq", "mla_w_ukv", "mla_q_gain",
           "mla_k_gain", "mla_w_out", "xmem_norm", "xmem_mem_norm", "xmem_wq", "xmem_wkv", "xmem_q_gain", "xmem_k_gain",
           "xmem_wo", "ffn_post_norm", "ffn_post_w_gu", "ffn_post_w_down")
INPUTS = ("x", "mem", "positions") + WEIGHTS + ("loss_target",) + tuple("m_" + n for n in WEIGHTS) + tuple(
    "v_" + n for n in WEIGHTS)


def _step(a):
    x, y, c, _ = _place()
    chip = 2 * x + y
    w = {n: [None] * a[n].shape[0] for n, _ in SHARDED}
    lots = {early: [(n, l, ax) for n, ax in SHARDED for l in range(a[n].shape[0]) if ((n, l) in EARLY) == early]
            for early in (True, False)}

    def packed(lot):
        return _pack([a[n][l] for n, l, _ in lot], PACK_COLS, PACK_ROW_MULTIPLE, BF16)

    def unpack(gathered, lot):
        at = 0
        for n, l, ax in lot:
            shp = a[n].shape[1:]
            rows = _size(shp) // PACK_COLS
            per_chip = gathered[:, at:at + rows].reshape((N_CHIPS,) + shp)
            at += rows
            w[n][l] = jnp.moveaxis(per_chip, 0, ax - 1).reshape(shp[:ax - 1] + (N_CHIPS * shp[ax - 1],) + shp[ax:])

    unpack(_gather_chips(packed(lots[True])), lots[True])
    gains = jnp.zeros((8, LANE), F32)
    for r, n in enumerate(LORA_GAINS):
        gains = gains.at[r, :a[n].shape[1]].set(a[n][0])
    gains = _gather_devices(gains)
    for r, n in enumerate(LORA_GAINS):
        w[n] = jnp.concatenate([gains[16 * q + r, :a[n].shape[1]] for q in range(N_CHIPS)])[None, :]
    for n in REPLICATED:
        w[n] = a[n]

    def packed_grads(lot, grads):
        def part(n, l, ax, q):
            size = a[n].shape[ax]
            return lax.slice_in_dim(grads[n][l], q * size, (q + 1) * size, axis=ax - 1)

        return jnp.stack([_pack([part(n, l, ax, q) for n, l, ax in lot], PACK_COLS, PACK_ROW_MULTIPLE, F32)
                          for q in range(N_CHIPS)])

    sent = {}

    def partials_so_far(grads, last):
        sent[last] = packed_grads(lots[last], grads)
        return sent[last].astype(BF16)

    loss, dx, grads, *received = _local_step(a["x"][0], a["mem"][0], a["positions"][0], a["loss_target"][0], w,
                                             packed(lots[False]), lambda gathered: unpack(gathered, lots[False]),
                                             partials_so_far)
    loss = lax.psum(loss, ("x", "y", "c"))
    small_names = REPLICATED + LORA_GAINS
    full = {n: jnp.stack(grads[n]).reshape(w[n].shape) for n in small_names}

    reduced = {last: _join_halves(_sum_partials(sent[last], received[last], name=f"grad_sum_partials_{int(last)}"))
               for last in (False, True)}
    layers = {n: [None] * a[n].shape[0] for n, _ in SHARDED}
    for early, lot in lots.items():
        for (n, l, _), piece in zip(lot, _unpack(reduced[early], [a[n].shape[1:] for n, _, _ in lot])):
            layers[n][l] = piece
    gw = {n: jnp.stack(layers[n]) for n, _ in SHARDED}

    small = _pack([full[n] for n in small_names], LANE, 256, F32)
    rows = small.shape[0]
    summed = _sum_slots(_gather_devices(small).reshape(8, rows, LANE), name="grad_sum_devices")
    for n, val in zip(small_names, _unpack(summed, [full[n].shape for n in small_names])):
        if n in LORA_GAINS:
            size = a[n].shape[1]
            val = lax.dynamic_slice_in_dim(val, chip * size, size, axis=1)
        gw[n] = val

    upd = {n: _adamw(a[n], gw[n], a["m_" + n], a["v_" + n], name="adamw_" + n) for n in WEIGHTS}
    return (loss, dx[None], *[gw[n] for n in WEIGHTS], *[upd[n][0] for n in WEIGHTS], *[upd[n][1] for n in WEIGHTS],
            *[upd[n][2] for n in WEIGHTS])


def kernel(x, mem, positions, ffn_pre_norm, ffn_pre_w_gu, ffn_pre_w_down, mix_norm, sbg_w_in, sgu_ln_gain,
           sgu_ln_bias, sgu_w, sgu_b, sbg_w_out, mla_w_in, mla_q_lora_gain, mla_kv_lora_gain, mla_w_uq, mla_w_ukv,
           mla_q_gain, mla_k_gain, mla_w_out, xmem_norm, xmem_mem_norm, xmem_wq, xmem_wkv, xmem_q_gain, xmem_k_gain,
           xmem_wo, ffn_post_norm, ffn_post_w_gu, ffn_post_w_down, loss_target, m_ffn_pre_norm, m_ffn_pre_w_gu,
           m_ffn_pre_w_down, m_mix_norm, m_sbg_w_in, m_sgu_ln_gain, m_sgu_ln_bias, m_sgu_w, m_sgu_b, m_sbg_w_out,
           m_mla_w_in, m_mla_q_lora_gain, m_mla_kv_lora_gain, m_mla_w_uq, m_mla_w_ukv, m_mla_q_gain, m_mla_k_gain,
           m_mla_w_out, m_xmem_norm, m_xmem_mem_norm, m_xmem_wq, m_xmem_wkv, m_xmem_q_gain, m_xmem_k_gain,
           m_xmem_wo, m_ffn_post_norm, m_ffn_post_w_gu, m_ffn_post_w_down, v_ffn_pre_norm, v_ffn_pre_w_gu,
           v_ffn_pre_w_down, v_mix_norm, v_sbg_w_in, v_sgu_ln_gain, v_sgu_ln_bias, v_sgu_w, v_sgu_b, v_sbg_w_out,
           v_mla_w_in, v_mla_q_lora_gain, v_mla_kv_lora_gain, v_mla_w_uq, v_mla_w_ukv, v_mla_q_gain, v_mla_k_gain,
           v_mla_w_out, v_xmem_norm, v_xmem_mem_norm, v_xmem_wq, v_xmem_wkv, v_xmem_q_gain, v_xmem_k_gain,
           v_xmem_wo, v_ffn_post_norm, v_ffn_post_w_gu, v_ffn_post_w_down):
    given = locals()
    return _step({n: given[n] for n in INPUTS})
```

```python
import jax
import jax.numpy as jnp
from jax import lax
from jax.experimental import pallas as pl
from jax.experimental.pallas import tpu as pltpu

F32, BF16 = jnp.float32, jnp.bfloat16
LANE = 128
VMEM_LIMIT = 56 * 1024 * 1024
EPS = 1e-6
SB_HEADS, SB_HD = 8, 64
SG_GROUPS, SG_GD, SG_CHUNK = 8, 64, 128
SB_W, SG_W = SB_HEADS * SB_HD, SG_GROUPS * SG_GD
MLA_HEADS, MLA_NOPE, MLA_ROPE, MLA_V = 16, 64, 32, 64
MLA_QK = MLA_NOPE + MLA_ROPE
MLA_QL, MLA_KVL = 512, 256
ROPE_THETA = 10000.0
MEM_HEADS, MEM_HD = 4, 256
SB_SCALE, MLA_SCALE, MEM_SCALE = SB_HD ** -0.5, MLA_QK ** -0.5, MEM_HD ** -0.5
ADAM_LR, ADAM_B1, ADAM_B2, ADAM_EPS, ADAM_WD, ADAM_STEP = 0.001, 0.9, 0.999, 1e-08, 0.01, 10
MESH = pl.DeviceIdType.MESH
ANY = pl.BlockSpec(memory_space=pl.ANY)


def _params(n_axes):
    return pltpu.CompilerParams(dimension_semantics=("arbitrary",) * n_axes, vmem_limit_bytes=VMEM_LIMIT)


MM_TILE_CAP = 1408
MM_VMEM_BUDGET = 40 * 1024 * 1024


def _tile(dim, cap):
    if dim <= cap:
        return dim
    best = max(t for t in range(LANE, cap + 1, LANE) if dim % t == 0)
    return best


def _k_scratch(count, tile, nk):
    return [pltpu.VMEM(tile, F32)] * count if nk > 1 else []


def _over_k_steps(prods, acc_refs, nk, finish):
    if nk == 1:
        finish(prods)
        return
    kk = pl.program_id(2)

    @pl.when(kk == 0)
    def _():
        for ref, p in zip(acc_refs, prods):
            ref[...] = p

    @pl.when(kk > 0)
    def _():
        for ref, p in zip(acc_refs, prods):
            ref[...] += p

    @pl.when(kk == nk - 1)
    def _():
        finish([ref[...] for ref in acc_refs])


def _mm(a, b, *, ta=False, tb=False, out_dtype=F32, scale=1.0, residual=None, bias=None, norm_bwd=None, a_off=(0, 0),
        b_off=(0, 0), m=None, n=None, k=None, name):
    am, ak = (a.shape[1], a.shape[0]) if ta else a.shape
    bk, bn = (b.shape[1], b.shape[0]) if tb else b.shape
    M, N, K = m or am, n or bn, k or ak
    tm, tn = _tile(M, MM_TILE_CAP if norm_bwd is None else MM_TILE_CAP // 2), _tile(N, MM_TILE_CAP)
    n_full = (residual is not None) + (2 if norm_bwd is not None else 0)
    fixed = tm * tn * (4 + 2 * jnp.dtype(out_dtype).itemsize + 8 * n_full)
    per_k = (tm * (2 * a.dtype.itemsize + 2) + tn * (2 * b.dtype.itemsize + 2))
    tk = _tile(K, max(LANE, (MM_VMEM_BUDGET - fixed) // per_k))
    nm, nn, nk = M // tm, N // tn, K // tk
    assert norm_bwd is None or nn == 1
    a_off = (a_off[0] // (tk if ta else tm), a_off[1] // (tm if ta else tk))
    b_off = (b_off[0] // (tn if tb else tk), b_off[1] // (tk if tb else tn))
    dims = (((0 if ta else 1,), (1 if tb else 0,)), ((), ()))
    n_out = 1 if norm_bwd is None else 2

    def body(*refs):
        a_ref, b_ref = refs[0], refs[1]
        n_in = len(ins)
        o_ref, extras, acc_refs = refs[n_in], refs[2:n_in], refs[n_in + n_out:]
        first_rows = pl.program_id(0) == 0

        def finish(total):
            out = total * scale
            for extra in (extras if norm_bwd is None else extras[:-3]):
                out = out + extra[...].astype(F32)
            if norm_bwd is not None:
                x_ref, g_ref, dres_ref = extras[-3:]
                dg_ref = refs[n_in + 1]
                dx, dg = _rmsnorm_bwd(out, x_ref[...], g_ref[...])
                out = dx + dres_ref[...]

                @pl.when(first_rows)
                def _():
                    dg_ref[...] = jnp.zeros_like(dg_ref)

                dg_ref[...] += dg
            o_ref[...] = out.astype(o_ref.dtype)

        prod = lax.dot_general(a_ref[...].astype(BF16), b_ref[...].astype(BF16), dims, preferred_element_type=F32)
        _over_k_steps([prod], acc_refs, nk, lambda totals: finish(totals[0]))

    (ao0, ao1), (bo0, bo1) = a_off, b_off
    a_spec = (pl.BlockSpec((tk, tm), lambda i, j, kk: (kk + ao0, i + ao1)) if ta
              else pl.BlockSpec((tm, tk), lambda i, j, kk: (i + ao0, kk + ao1)))
    b_spec = (pl.BlockSpec((tn, tk), lambda i, j, kk: (j + bo0, kk + bo1)) if tb
              else pl.BlockSpec((tk, tn), lambda i, j, kk: (kk + bo0, j + bo1)))
    o_spec = pl.BlockSpec((tm, tn), lambda i, j, kk: (i, j))
    ins, in_specs = [a, b], [a_spec, b_spec]
    if residual is not None:
        ins.append(residual)
        in_specs.append(o_spec)
    if bias is not None:
        ins.append(bias)
        in_specs.append(pl.BlockSpec((1, tn), lambda i, j, kk: (0, j)))
    out_specs, out_shape = o_spec, jax.ShapeDtypeStruct((M, N), out_dtype)
    if norm_bwd is not None:
        x, gain, dres = norm_bwd
        row = pl.BlockSpec((1, tn), lambda i, j, kk: (0, 0))
        ins += [x, gain.reshape(1, N), dres]
        in_specs += [o_spec, row, o_spec]
        out_specs, out_shape = [o_spec, row], [out_shape, jax.ShapeDtypeStruct((1, N), F32)]
    return pl.pallas_call(
        body, name=name, grid=(nm, nn, nk), in_specs=in_specs, out_specs=out_specs, out_shape=out_shape,
        scratch_shapes=_k_scratch(1, (tm, tn), nk), compiler_params=_params(3))(*ins)


def _mm_swiglu(h, wgu, *, name):
    M, K = h.shape
    F = wgu.shape[1] // 2
    tm, tn, tk = _tile(M, 512), _tile(F, MM_TILE_CAP), _tile(K, 1024)
    nm, nf, nk = M // tm, F // tn, K // tk

    def body(h_ref, wg_ref, wu_ref, g_ref, u_ref, a_ref, *acc_refs):
        def finish(totals):
            g, u = totals
            g_ref[...] = g.astype(BF16)
            u_ref[...] = u.astype(BF16)
            a_ref[...] = (g * jax.nn.sigmoid(g) * u).astype(BF16)

        hb = h_ref[...]
        _over_k_steps([jnp.dot(hb, wg_ref[...], preferred_element_type=F32),
                       jnp.dot(hb, wu_ref[...], preferred_element_type=F32)], acc_refs, nk, finish)

    o_spec = pl.BlockSpec((tm, tn), lambda j, i, kk: (i, j))
    shp = jax.ShapeDtypeStruct((M, F), BF16)
    return pl.pallas_call(
        body, name=name, grid=(nf, nm, nk),
        in_specs=[pl.BlockSpec((tm, tk), lambda j, i, kk: (i, kk)),
                  pl.BlockSpec((tk, tn), lambda j, i, kk: (kk, j)),
                  pl.BlockSpec((tk, tn), lambda j, i, kk: (kk, j + nf))],
        out_specs=[o_spec, o_spec, o_spec], out_shape=[shp, shp, shp],
        scratch_shapes=_k_scratch(2, (tm, tn), nk), compiler_params=_params(3))(h, wgu, wgu)


def _mm_dswiglu(dy, wd, gate, up, *, scale, name):
    M, K = dy.shape
    F = wd.shape[0]
    tm, tn, tk = _tile(M, 512), _tile(F, MM_TILE_CAP), _tile(K, 1024)
    nm, nf, nk = M // tm, F // tn, K // tk

    def body(dy_ref, wd_ref, g_ref, u_ref, dg_ref, du_ref, *acc_refs):
        def finish(totals):
            da = totals[0] * scale
            g, u = g_ref[...].astype(F32), u_ref[...].astype(F32)
            sg = jax.nn.sigmoid(g)
            du_ref[...] = (da * g * sg).astype(BF16)
            dg_ref[...] = (da * u * sg * (1.0 + g * (1.0 - sg))).astype(BF16)

        _over_k_steps([_nt(dy_ref[...].astype(BF16), wd_ref[...])], acc_refs, nk, finish)

    o_spec = pl.BlockSpec((tm, tn), lambda j, i, kk: (i, j))
    shp = jax.ShapeDtypeStruct((M, F), BF16)
    return pl.pallas_call(
        body, name=name, grid=(nf, nm, nk),
        in_specs=[pl.BlockSpec((tm, tk), lambda j, i, kk: (i, kk)),
                  pl.BlockSpec((tn, tk), lambda j, i, kk: (j, kk)), o_spec, o_spec],
        out_specs=[o_spec, o_spec], out_shape=[shp, shp],
        scratch_shapes=_k_scratch(1, (tm, tn), nk), compiler_params=_params(3))(dy, wd, gate, up)


HEAD_ROWS = 2048


def _row_tile(rows, cap):
    t = cap
    while t >= 8:
        if rows % t == 0:
            return t
        t //= 2
    return rows


def _rowwise(fn, rows, consts, outs, sums=(), hsums=(), *, heads=None, tm=512, name):
    rows = [r if isinstance(r, tuple) else (r, r.shape[1], None) for r in rows]
    rows = [r if len(r) == 4 else (*r, False) for r in rows]
    S = rows[0][0].shape[0]
    tm = _row_tile(S, tm)
    nh = heads or 1
    n_r, n_c, n_o, n_h, n_s = len(rows), len(consts), len(outs), len(hsums), len(sums)

    def body(*refs):
        r = [x[...].T if row[3] else x[...] for x, row in zip(refs, rows)]
        c = [x[...] for x in refs[n_r:n_r + n_c]]
        o_refs = refs[n_r + n_c:n_r + n_c + n_o]
        h_refs = refs[n_r + n_c + n_o:n_r + n_c + n_o + n_h]
        s_refs = refs[n_r + n_c + n_o + n_h:]
        res = fn(*r, *c)
        res = res if isinstance(res, (tuple, list)) else (res,)
        for ref, val in zip(o_refs, res[:n_o]):
            ref[...] = val.astype(ref.dtype)
        if n_h:
            @pl.when(pl.program_id(1) == 0)
            def _():
                for ref in h_refs:
                    ref[...] = jnp.zeros_like(ref)
            for ref, val in zip(h_refs, res[n_o:n_o + n_h]):
                ref[...] += val
        if n_s:
            @pl.when((pl.program_id(0) == 0) & (pl.program_id(1) == 0))
            def _():
                for ref in s_refs:
                    ref[...] = jnp.zeros_like(ref)
            for ref, val in zip(s_refs, res[n_o + n_h:]):
                ref[...] += val

    def col(colfn):
        return (lambda i, h: (i, 0)) if colfn is None else (lambda i, h: (i, colfn(h)))

    in_specs = [pl.BlockSpec((w, tm), lambda i, h, cf=cf: (cf(h), i)) if flipped else pl.BlockSpec((tm, w), col(cf))
                for _, w, cf, flipped in rows]
    in_specs += [pl.BlockSpec(a.shape, lambda i, h, nd=a.ndim: (0,) * nd) for a in consts]
    out_specs = [pl.BlockSpec((tm, w // nh), (lambda i, h: (i, h)) if heads else (lambda i, h: (i, 0))) for w, _ in outs]
    out_specs += [pl.BlockSpec((tm, w), lambda i, h: (i, 0)) for w in hsums]
    out_specs += [pl.BlockSpec(sh, lambda i, h, nd=len(sh): (0,) * nd) for sh in sums]
    out_shape = [jax.ShapeDtypeStruct((S, w), dt) for w, dt in outs]
    out_shape += [jax.ShapeDtypeStruct((S, w), F32) for w in hsums]
    out_shape += [jax.ShapeDtypeStruct(sh, F32) for sh in sums]
    return pl.pallas_call(body, name=name, grid=(S // tm, nh), in_specs=in_specs, out_specs=out_specs,
                          out_shape=out_shape, compiler_params=_params(2))(*[row[0] for row in rows], *consts)


def _rms(x, width=None):
    width = width or x.shape[-1]
    return lax.rsqrt(jnp.sum(x * x, axis=-1, keepdims=True) * (1.0 / width) + EPS)


def _rmsnorm_fwd(x, g, width=None):
    return x * _rms(x, width) * g


def _rmsnorm_bwd(dy, x, g, width=None):
    width = width or x.shape[-1]
    r = _rms(x, width)
    xn = x * r
    dxn = dy * g
    dx = r * (dxn - xn * (jnp.sum(dxn * xn, axis=-1, keepdims=True) * (1.0 / width)))
    return dx, jnp.sum(dy * xn, axis=0, keepdims=True)


def _norm_rows(x, g, *, name, out_dtype=BF16):
    D = x.shape[1]
    return _rowwise(lambda xv, gv: _rmsnorm_fwd(xv.astype(F32), gv), [x], [g.reshape(1, D)], [(D, out_dtype)],
                    name=name)[0]


def _norm_rows_bwd(dh, x, g, dres, *, name):
    D = x.shape[1]

    def fn(dhv, xv, *rest):
        dx, dg = _rmsnorm_bwd(dhv.astype(F32), xv, rest[-1])
        return (dx + rest[0] if dres is not None else dx), dg

    rows = [dh, x] + ([dres] if dres is not None else [])
    return _rowwise(fn, rows, [g.reshape(1, D)], [(D, F32)], [(1, D)], name=name)


def _softplus(z):
    return jnp.where(z > 20.0, z, jnp.log(1.0 + jnp.exp(z)))


def _running_sum(v, u, split=True):
    if not split:
        return jnp.dot(v.astype(BF16), u, preferred_element_type=F32)
    hi = lax.bitcast_convert_type(lax.bitcast_convert_type(v, jnp.uint32) & jnp.uint32(0xFFFF0000), F32)
    return (jnp.dot(hi.astype(BF16), u, preferred_element_type=F32)
            + jnp.dot((v - hi).astype(BF16), u, preferred_element_type=F32))


def _triangle(tk, inclusive_prefix):
    j, s = lax.broadcasted_iota(jnp.int32, (tk, tk), 0), lax.broadcasted_iota(jnp.int32, (tk, tk), 1)
    return ((j <= s) if inclusive_prefix else (j > s)).astype(BF16)


def _nt(a, b):
    return lax.dot_general(a, b, (((1,), (1,)), ((), ())), preferred_element_type=F32)


def _tn(a, b):
    return lax.dot_general(a, b, (((0,), (0,)), ((), ())), preferred_element_type=F32)


ATT_TQ, ATT_TK = 512, 512
SB_SUB = 256
FWD_GROUP = 2


def _attn_fwd(q, k, v, *, sb, causal, heads, dq, dv, group=1, kcol=None, vcol=None, sum_lane=None, side_gather=None,
              name):
    S, Sk = q.shape[0], k.shape[0]
    tq, tk = min(ATT_TQ, S), min(ATT_TK, Sk)
    sub = min(SB_SUB, tk) if sb else tk
    assert tq % sub == 0 or not causal
    kcol = kcol or (lambda h: h)
    vcol = vcol or (lambda h: h)
    members = range(group)
    assert side_gather is None or heads // group >= GATHER_STAGES

    def body(*refs):
        if side_gather is not None:
            n_in = 4 if sb else 3
            stages = _gather_stages(refs[n_in], refs[n_in + 3], *refs[-2:])
            for n, stage in enumerate(stages):
                pl.when((pl.program_id(0) == n) & (pl.program_id(1) == 0))(stage)
            refs = refs[:n_in] + refs[n_in + 1:n_in + 3] + refs[n_in + 4:-2]
        if sb:
            q_ref, k_ref, v_ref, u_ref, o_ref, lse_ref, acc_ref, r_ref = refs
            r_ref[...] = jnp.zeros_like(r_ref)
        else:
            q_ref, k_ref, v_ref, o_ref, lse_ref, acc_ref, m_ref, l_ref = refs
            m_ref[...] = jnp.full_like(m_ref, -1e30)
            l_ref[...] = jnp.zeros_like(l_ref)
        first_row = pl.program_id(1) * tq
        qb = [q_ref[:, hh * dq:(hh + 1) * dq] for hh in members]
        acc_ref[...] = jnp.zeros_like(acc_ref)
        nblk = (first_row + tq) // sub if causal else Sk // sub
        nfull = (first_row + (0 if sb else 1)) // sub if causal else nblk
        n_cut = tq // sub if causal else 0

        def scores(jj):
            off = pl.multiple_of(jj * sub, sub)
            return tuple(_nt(qb[hh], k_ref[pl.ds(off, sub), hh * dq:(hh + 1) * dq]) for hh in members)

        def weigh(jj, scores_now, masked):
            off = pl.multiple_of(jj * sub, sub)
            if masked:
                kpos = off + lax.broadcasted_iota(jnp.int32, (tq, sub), 1)
                qpos = first_row + lax.broadcasted_iota(jnp.int32, (tq, sub), 0)
                valid = (kpos < qpos) if sb else (kpos <= qpos)
            for hh in members:
                vb = v_ref[pl.ds(off, sub), hh * dv:(hh + 1) * dv]
                s = scores_now[hh]
                if sb:
                    sp = _softplus(s)
                    ls = jnp.where(valid, -sp, 0.0) if masked else -sp
                    w = jnp.exp(s - sp + r_ref[hh] + _running_sum(ls, u_ref[...]))
                    if masked:
                        w = jnp.where(valid, w, 0.0)
                    acc_ref[hh] += jnp.dot(w.astype(BF16), vb, preferred_element_type=F32)
                    r_ref[hh] += jnp.sum(ls, axis=1, keepdims=True)
                else:
                    if masked:
                        s = jnp.where(valid, s, -1e30)
                    m_old = m_ref[hh]
                    m_new = jnp.maximum(m_old, jnp.max(s, axis=1, keepdims=True))
                    p = jnp.exp(s - m_new)
                    alpha = jnp.exp(m_old - m_new)
                    if sum_lane is None:
                        l_ref[hh] = alpha * l_ref[hh] + jnp.sum(p, axis=1, keepdims=True)
                    acc_ref[hh] = alpha * acc_ref[hh] + jnp.dot(p.astype(BF16), vb, preferred_element_type=F32)
                    m_ref[hh] = m_new

        if sb:
            s_cur = scores(nblk - 1)
            for cut in range(n_cut):
                s_next = scores(jnp.maximum(nblk - 2 - cut, 0))
                weigh(nblk - 1 - cut, s_cur, True)
                s_cur = s_next

            def step(t, s_now):
                s_next = scores(jnp.maximum(nfull - 2 - t, 0))
                weigh(nfull - 1 - t, s_now, False)
                return s_next

            lax.fori_loop(0, nfull, step, s_cur)
        else:
            n_loop = nfull if causal else nblk - 1

            def step(t, s_now):
                s_next = scores(jnp.minimum(t + 1, nblk - 1))
                weigh(t, s_now, False)
                return s_next

            s_cur = lax.fori_loop(0, n_loop, step, scores(0))
            tail = n_cut if causal else 1
            for last in range(tail):
                s_next = scores(n_loop + last + 1) if last + 1 < tail else None
                weigh(n_loop + last, s_cur, causal)
                s_cur = s_next
        for hh in members:
            cols = slice(hh * dv, (hh + 1) * dv)
            if sb:
                o_ref[:, cols] = acc_ref[hh]
                lse_ref[hh] = r_ref[hh]
            else:
                acc = acc_ref[hh]
                l = l_ref[hh] if sum_lane is None else acc[:, sum_lane:sum_lane + 1]
                o_ref[:, cols] = acc / l
                lse_ref[hh] = m_ref[hh] + jnp.log(l)

    in_specs = [pl.BlockSpec((tq, group * dq), lambda g, i: (i, g)),
                pl.BlockSpec((Sk, group * dq), lambda g, i: (0, kcol(g))),
                pl.BlockSpec((Sk, group * dv), lambda g, i: (0, vcol(g)))]
    ins = [q, k, v]
    scratch = [pltpu.VMEM((group, tq, dv), F32), pltpu.VMEM((group, tq, 1), F32)]
    if sb:
        ins.append(_triangle(sub, inclusive_prefix=False))
        in_specs.append(pl.BlockSpec((sub, sub), lambda g, i: (0, 0)))
    else:
        scratch.append(pltpu.VMEM((group, tq, 1), F32))
    out_specs = [pl.BlockSpec((tq, group * dv), lambda g, i: (i, g)), pl.BlockSpec((group, tq, 1), lambda g, i: (g, i, 0))]
    out_shape = [jax.ShapeDtypeStruct((S, heads * dv), F32), jax.ShapeDtypeStruct((heads, S, 1), F32)]
    if side_gather is not None:
        ins.append(side_gather)
        in_specs.append(ANY)
        out_specs.append(ANY)
        out_shape.append(jax.ShapeDtypeStruct((N_CHIPS,) + side_gather.shape, side_gather.dtype))
        scratch += [pltpu.SemaphoreType.DMA((GATHER_COPIES,)), pltpu.SemaphoreType.DMA((GATHER_COPIES,))]
    outs = pl.pallas_call(body, name=name, grid=(heads // group, S // tq), in_specs=in_specs, out_specs=out_specs,
                          out_shape=out_shape, scratch_shapes=scratch, compiler_params=_params(2))(*ins)
    return outs if side_gather is None else (outs[0], outs[1], _place_own_slot(outs[2], side_gather))


def _attn_bwd(q, k, v, o, do, lse, *, sb, causal, heads, dq, dv, kcol=None, vcol=None, side_exchange=None, name):
    S, Sk = q.shape[0], k.shape[0]
    tq, tk = min(ATT_TQ, S), min(ATT_TK, Sk)
    sub = min(SB_SUB, tk) if sb else tk
    assert tq % sub == 0 or not causal
    nq = S // tq
    kcol = kcol or (lambda h: h)
    vcol = vcol or (lambda h: h)
    n_in = 7 if sb else 6

    def body(*refs):
        if side_exchange is not None:
            start, finish = _partials_exchange(refs[n_in], refs[n_in + 4], *refs[-2:])
            pl.when((pl.program_id(0) == 0) & (pl.program_id(1) == 0))(start)
            pl.when((pl.program_id(0) == heads - 1) & (pl.program_id(1) == 0))(finish)
            refs = refs[:n_in] + refs[n_in + 1:n_in + 4] + refs[n_in + 5:-2]
        if sb:
            q_ref, k_ref, v_ref, o_ref, do_ref, lse_ref, u_ref, dq_ref, dk_ref, dv_ref, acc_ref, r_ref, re_ref = refs
            r_ref[...] = jnp.zeros_like(r_ref)
            re_ref[...] = jnp.zeros_like(re_ref)
        else:
            q_ref, k_ref, v_ref, o_ref, do_ref, lse_ref, dq_ref, dk_ref, dv_ref, acc_ref = refs
        first_row = pl.program_id(1) * tq

        @pl.when(first_row == 0)
        def _():
            dk_ref[...] = jnp.zeros_like(dk_ref)
            dv_ref[...] = jnp.zeros_like(dv_ref)

        qb = q_ref[...]
        dof = do_ref[...].astype(F32)
        dob = dof.astype(BF16)
        q_t, do_t = qb.T, dob.T
        if not sb:
            dlt = jnp.sum(dof * o_ref[...], axis=1, keepdims=True)
        acc_ref[...] = jnp.zeros_like(acc_ref)
        nblk = (first_row + tq) // sub if causal else Sk // sub
        nfull = (first_row + (0 if sb else 1)) // sub if causal else nblk
        n_cut = tq // sub if causal else 0

        def products(jj):
            off = pl.multiple_of(jj * sub, sub)
            return _nt(qb, k_ref[pl.ds(off, sub), :]), _nt(dob, v_ref[pl.ds(off, sub), :])

        def piece(jj, now, masked):
            off = pl.multiple_of(jj * sub, sub)
            kb = k_ref[pl.ds(off, sub), :]
            s, dp = now
            if masked:
                qpos = first_row + lax.broadcasted_iota(jnp.int32, (tq, sub), 0)
                kpos = off + lax.broadcasted_iota(jnp.int32, (tq, sub), 1)
                valid = (kpos < qpos) if sb else (kpos <= qpos)
            if sb:
                u = u_ref[...]
                sp = _softplus(s)
                ls = jnp.where(valid, -sp, 0.0) if masked else -sp
                lb = s - sp
                w = jnp.exp(lb + (lse_ref[0] - (r_ref[...] + _running_sum(ls, u))))
                if masked:
                    w = jnp.where(valid, w, 0.0)
                e = dp * w
                ds = e - jnp.exp(lb) * (re_ref[...] + _running_sum(e, u, split=False))
                if masked:
                    ds = jnp.where(valid, ds, 0.0)
                r_ref[...] += jnp.sum(ls, axis=1, keepdims=True)
                re_ref[...] += jnp.sum(e, axis=1, keepdims=True)
            else:
                w = jnp.exp(s - lse_ref[0])
                if masked:
                    w = jnp.where(valid, w, 0.0)
                ds = w * (dp - dlt)
            dsb = ds.astype(BF16)
            dv_ref[:, pl.ds(off, sub)] += jnp.dot(do_t, w.astype(BF16), preferred_element_type=F32)
            dk_ref[:, pl.ds(off, sub)] += jnp.dot(q_t, dsb, preferred_element_type=F32)
            acc_ref[...] += jnp.dot(dsb, kb, preferred_element_type=F32)

        n_loop = nfull if causal else nblk - 1
        per_trip = tk // sub

        def steps(first, count, masked):
            ready = [products(first + c) for c in range(count)]
            for c in range(count):
                piece(first + c, ready[c], masked)

        def trip(t, carry):
            steps(t * per_trip, per_trip, False)
            return carry

        lax.fori_loop(0, n_loop // per_trip, trip, 0)
        steps(n_loop, n_cut if causal else 1, causal)
        dq_ref[...] = acc_ref[...]

    ins = [q, k, v, o, do]
    in_specs = [pl.BlockSpec((tq, dq), lambda h, i: (i, h)),
                pl.BlockSpec((Sk, dq), lambda h, i: (0, kcol(h))),
                pl.BlockSpec((Sk, dv), lambda h, i: (0, vcol(h))),
                pl.BlockSpec((tq, dv), lambda h, i: (i, h)),
                pl.BlockSpec((tq, dv), lambda h, i: (i, h))]
    scratch = [pltpu.VMEM((tq, dq), F32)]
    ins.append(lse)
    in_specs.append(pl.BlockSpec((1, tq, 1), lambda h, i: (h, i, 0)))
    if sb:
        ins.append(_triangle(sub, inclusive_prefix=True))
        in_specs.append(pl.BlockSpec((sub, sub), lambda h, i: (0, 0)))
        scratch += [pltpu.VMEM((tq, 1), F32), pltpu.VMEM((tq, 1), F32)]
    out_specs = [pl.BlockSpec((tq, dq), lambda h, i: (i, h)),
                 pl.BlockSpec((dq, Sk), lambda h, i: (h, 0)),
                 pl.BlockSpec((dv, Sk), lambda h, i: (h, 0))]
    out_shape = [jax.ShapeDtypeStruct((S, heads * dq), F32), jax.ShapeDtypeStruct((heads * dq, Sk), F32),
                 jax.ShapeDtypeStruct((heads * dv, Sk), F32)]
    if side_exchange is not None:
        _, R, C = side_exchange.shape
        ins.append(side_exchange)
        in_specs.append(ANY)
        out_specs.append(ANY)
        out_shape.append(jax.ShapeDtypeStruct((N_DEVICES, R // 2, C), side_exchange.dtype))
        scratch += [pltpu.SemaphoreType.DMA((N_DEVICES,)), pltpu.SemaphoreType.DMA((N_DEVICES,))]
    return pl.pallas_call(body, name=name, grid=(heads, nq), in_specs=in_specs, out_specs=out_specs,
                          out_shape=out_shape, scratch_shapes=scratch, compiler_params=_params(2))(*ins)


GELU_C = 0.7978845608028654
assert 2 * SG_GD == LANE and SG_CHUNK == LANE


def _gelu(z):
    t = jnp.tanh(GELU_C * (z + 0.044715 * z * z * z))
    return 0.5 * z * (1.0 + t), t


def _gelu_grad(z, t):
    return 0.5 * (1.0 + t) + 0.5 * z * (1.0 - t * t) * GELU_C * (1.0 + 3.0 * 0.044715 * z * z)


def _layernorm_parts(g):
    d = g - jnp.mean(g, axis=-1, keepdims=True)
    rstd = lax.rsqrt(jnp.mean(d * d, axis=-1, keepdims=True) + EPS)
    return d * rstd, rstd


def _gelu_ln(z, gain, bias, *, name):
    def fn(zv, gn, bs):
        a, _ = _gelu(zv)
        y, _ = _layernorm_parts(a[:, SG_W:])
        return a[:, :SG_W], y * gn + bs

    return _rowwise(fn, [z], [gain.reshape(1, SG_W), bias.reshape(1, SG_W)], [(SG_W, F32), (SG_W, BF16)], name=name)


def _gelu_ln_bwd(z, du, dgl, gain, *, name):
    def fn(zv, duv, dglv, gn):
        a, t = _gelu(zv)
        y, rstd = _layernorm_parts(a[:, SG_W:])
        dy = dglv * gn
        dgg = rstd * (dy - jnp.mean(dy, axis=-1, keepdims=True) - y * jnp.mean(dy * y, axis=-1, keepdims=True))
        dz = jnp.concatenate([duv, dgg], axis=1) * _gelu_grad(zv, t)
        return dz, jnp.sum(dglv * y, axis=0, keepdims=True), jnp.sum(dglv, axis=0, keepdims=True)

    return _rowwise(fn, [z, du, dgl], [gain.reshape(1, SG_W)], [(2 * SG_W, BF16)], [(1, SG_W), (1, SG_W)], name=name)


def _sg_masks():
    tri = lax.broadcasted_iota(jnp.int32, (SG_CHUNK, SG_CHUNK), 0) >= lax.broadcasted_iota(jnp.int32, (SG_CHUNK, SG_CHUNK), 1)
    first = lax.broadcasted_iota(jnp.int32, (SG_CHUNK, LANE), 1) < SG_GD
    return tri, first


def _spatial(gl, u, w, bt, *, name):
    S = gl.shape[0]
    tm = _row_tile(S, 512)
    nch = tm // SG_CHUNK

    def body(gl_ref, u_ref, w_ref, bt_ref, o_ref):
        tri, first = _sg_masks()
        for p in range(SG_W // LANE):
            cols = slice(p * LANE, (p + 1) * LANE)
            wa = jnp.where(tri, w_ref[2 * p], 0.0).astype(BF16)
            wb = jnp.where(tri, w_ref[2 * p + 1], 0.0).astype(BF16)
            for ci in range(nch):
                rws = slice(ci * SG_CHUNK, (ci + 1) * SG_CHUNK)
                g = gl_ref[rws, cols]
                zero = jnp.zeros_like(g)
                mixed = (jnp.dot(wa, jnp.where(first, g, zero), preferred_element_type=F32)
                         + jnp.dot(wb, jnp.where(first, zero, g), preferred_element_type=F32) + bt_ref[:, cols])
                o_ref[rws, cols] = u_ref[rws, cols] * mixed

    row = pl.BlockSpec((tm, SG_W), lambda i: (i, 0))
    return pl.pallas_call(
        body, name=name, grid=(S // tm,),
        in_specs=[row, row, pl.BlockSpec(w.shape, lambda i: (0, 0, 0)), pl.BlockSpec(bt.shape, lambda i: (0, 0))],
        out_specs=row, out_shape=jax.ShapeDtypeStruct((S, SG_W), F32), compiler_params=_params(1))(gl, u, w, bt)


def _spatial_bwd(d_o, gl, u, w, bt, *, name):
    S = gl.shape[0]
    tm = _row_tile(S, 512)
    nch = tm // SG_CHUNK
    nsteps = S // tm

    def body(do_ref, gl_ref, u_ref, w_ref, bt_ref, du_ref, dgl_ref, dw_ref, db_ref, dbt_ref):
        tri, first = _sg_masks()
        step = pl.program_id(0)

        @pl.when(step == 0)
        def _():
            dw_ref[...] = jnp.zeros_like(dw_ref)
            dbt_ref[...] = jnp.zeros_like(dbt_ref)

        for p in range(SG_W // LANE):
            cols = slice(p * LANE, (p + 1) * LANE)
            wa = jnp.where(tri, w_ref[2 * p], 0.0).astype(BF16)
            wb = jnp.where(tri, w_ref[2 * p + 1], 0.0).astype(BF16)
            for ci in range(nch):
                rws = slice(ci * SG_CHUNK, (ci + 1) * SG_CHUNK)
                g = gl_ref[rws, cols]
                zero = jnp.zeros_like(g)
                mixed = (jnp.dot(wa, jnp.where(first, g, zero), preferred_element_type=F32)
                         + jnp.dot(wb, jnp.where(first, zero, g), preferred_element_type=F32) + bt_ref[:, cols])
                dov = do_ref[rws, cols]
                du_ref[rws, cols] = dov * mixed
                dm = dov * u_ref[rws, cols]
                dbt_ref[:, cols] += dm
                dma = jnp.where(first, dm, 0.0).astype(BF16)
                dmb = jnp.where(first, 0.0, dm).astype(BF16)
                dw_ref[2 * p] += jnp.where(tri, _nt(dma, g), 0.0)
                dw_ref[2 * p + 1] += jnp.where(tri, _nt(dmb, g), 0.0)
                dgl_ref[rws, cols] = _tn(wa, dma) + _tn(wb, dmb)

        @pl.when(step == nsteps - 1)
        def _():
            lane = lax.broadcasted_iota(jnp.int32, (SG_CHUNK, LANE), 1)
            acc = jnp.zeros((SG_CHUNK, LANE), F32)
            for p in range(SG_W // LANE):
                blk = dbt_ref[:, p * LANE:(p + 1) * LANE]
                sa = jnp.sum(jnp.where(first, blk, 0.0), axis=1, keepdims=True)
                sb_ = jnp.sum(jnp.where(first, 0.0, blk), axis=1, keepdims=True)
                acc = acc + jnp.where(lane == 2 * p, sa, 0.0) + jnp.where(lane == 2 * p + 1, sb_, 0.0)
            db_ref[...] = acc

    row = pl.BlockSpec((tm, SG_W), lambda i: (i, 0))
    return pl.pallas_call(
        body, name=name, grid=(nsteps,),
        in_specs=[row, row, row, pl.BlockSpec(w.shape, lambda i: (0, 0, 0)), pl.BlockSpec(bt.shape, lambda i: (0, 0))],
        out_specs=[row, row, pl.BlockSpec(w.shape, lambda i: (0, 0, 0)), pl.BlockSpec((SG_CHUNK, LANE), lambda i: (0, 0))],
        out_shape=[jax.ShapeDtypeStruct((S, SG_W), F32), jax.ShapeDtypeStruct((S, SG_W), F32),
                   jax.ShapeDtypeStruct(w.shape, F32), jax.ShapeDtypeStruct((SG_CHUNK, LANE), F32)],
        scratch_shapes=[pltpu.VMEM((SG_CHUNK, SG_W), F32)], compiler_params=_params(1))(d_o, gl, u, w, bt)


ROPE_HALF = MLA_ROPE // 2
KR_COL = (MLA_QL + MLA_KVL) // LANE
MLA_IN_PAD = MLA_QL + MLA_KVL + LANE


def _rope_tables(positions):
    inv_freq = ROPE_THETA ** (-jnp.arange(ROPE_HALF, dtype=F32) / ROPE_HALF)
    ang = positions.astype(F32)[:, None] * inv_freq
    cos, sin = jnp.cos(ang), jnp.sin(ang)
    S = positions.shape[0]
    z16, tail = jnp.zeros((S, ROPE_HALF), F32), jnp.zeros((S, LANE - MLA_QK), F32)
    ones = jnp.ones((S, MLA_NOPE), F32)
    zeros = jnp.zeros((S, MLA_NOPE), F32)
    return (jnp.concatenate([ones, cos, cos, tail], axis=1), jnp.concatenate([zeros, z16, sin, tail], axis=1),
            jnp.concatenate([zeros, -sin, z16, tail], axis=1))


def _rope(x, cos, sa, sb):
    return x * cos + pltpu.roll(x, ROPE_HALF, 1) * sa + pltpu.roll(x, LANE - ROPE_HALF, 1) * sb


def _rope_t(dy, cos, sa, sb):
    return dy * cos + pltpu.roll(dy * sa, LANE - ROPE_HALF, 1) + pltpu.roll(dy * sb, ROPE_HALF, 1)


def _mla_lora(P, qlg, kvlg, *, name):
    def fn(pv, a, b):
        return _rmsnorm_fwd(pv[:, :MLA_QL], a), _rmsnorm_fwd(pv[:, MLA_QL:MLA_QL + MLA_KVL], b)

    return _rowwise(fn, [P], [qlg.reshape(1, MLA_QL), kvlg.reshape(1, MLA_KVL)], [(MLA_QL, BF16), (MLA_KVL, BF16)], name=name)


def _mla_lora_bwd(dcq, dckv, dkr, P, qlg, kvlg, *, name):
    def fn(d1, d2, d3, pv, a, b):
        x1, g1 = _rmsnorm_bwd(d1, pv[:, :MLA_QL], a)
        x2, g2 = _rmsnorm_bwd(d2, pv[:, MLA_QL:MLA_QL + MLA_KVL], b)
        return jnp.concatenate([x1, x2, d3], axis=1), g1, g2

    return _rowwise(fn, [dcq, dckv, dkr, P], [qlg.reshape(1, MLA_QL), kvlg.reshape(1, MLA_KVL)], [(MLA_IN_PAD, BF16)],
                    [(1, MLA_QL), (1, MLA_KVL)], name=name)


def _mla_qk(q_pre, k_pre, P, tabs, qg, kg, *, name):
    def fn(qp, kp, kr, c, a, b, qgv, kgv):
        return (_rope(_rmsnorm_fwd(qp, qgv, MLA_QK), c, a, b) * MLA_SCALE,
                _rope(_rmsnorm_fwd(kp + kr, kgv, MLA_QK), c, a, b))

    hcol = lambda h: h
    rows = [(q_pre, LANE, hcol), (k_pre, LANE, hcol), (P, LANE, lambda h: KR_COL), *tabs]
    w = MLA_HEADS * LANE
    return _rowwise(fn, rows, [qg, kg], [(w, BF16), (w, BF16)], heads=MLA_HEADS, tm=HEAD_ROWS, name=name)


def _mla_qk_bwd(dq, dk_t, q_pre, k_pre, P, tabs, qg, kg, *, name):
    def fn(dqv, dkv, qp, kp, kr, c, a, b, qgv, kgv):
        dqp, dqg = _rmsnorm_bwd(_rope_t(dqv * MLA_SCALE, c, a, b), qp, qgv, MLA_QK)
        dkp, dkg = _rmsnorm_bwd(_rope_t(dkv, c, a, b), kp + kr, kgv, MLA_QK)
        lane = lax.broadcasted_iota(jnp.int32, (1, LANE), 1)
        return dqp, dkp, jnp.where((lane >= MLA_NOPE) & (lane < MLA_QK), dkp, 0.0), dqg, dkg

    hcol = lambda h: h
    rows = [(dq, LANE, hcol), (dk_t, LANE, hcol, True), (q_pre, LANE, hcol), (k_pre, LANE, hcol),
            (P, LANE, lambda h: KR_COL), *tabs]
    w = MLA_HEADS * LANE
    return _rowwise(fn, rows, [qg, kg], [(w, BF16), (w, BF16)], [(1, LANE), (1, LANE)], [LANE], heads=MLA_HEADS,
                    tm=HEAD_ROWS, name=name)


def _head_norm(x, g, *, heads, width, colfn=None, scale=1.0, name):
    return _rowwise(lambda xv, gv: _rmsnorm_fwd(xv, gv) * scale, [(x, width, colfn or (lambda h: h))],
                    [g.reshape(1, width)], [(heads * width, BF16)], heads=heads, tm=HEAD_ROWS, name=name)[0]


def _head_norm_bwd(dy, x, g, *, heads, width, colfn=None, scale=1.0, out_dtype, name):
    return _rowwise(lambda dv_, xv, gv: _rmsnorm_bwd(dv_ * scale, xv, gv),
                    [(dy, width, lambda h: h), (x, width, colfn or (lambda h: h))],
                    [g.reshape(1, width)], [(heads * width, out_dtype)], [(1, width)], heads=heads, tm=HEAD_ROWS,
                    name=name)


def _loss_grad(y, tgt, *, name):
    D = y.shape[1]

    def fn(yv, tv):
        d = yv - tv
        return d * (1.0 / D), jnp.sum(d * d, axis=0, keepdims=True) * (0.5 / D)

    dy, part = _rowwise(fn, [y, tgt], [], [(D, F32)], [(1, D)], name=name)
    return jnp.sum(part), dy


def _adamw(w, g, m, v, *, name):
    shape = w.shape
    two_d = (-1, shape[-1])

    def fn(wv, gv, mv, vv):
        m2 = ADAM_B1 * mv + (1.0 - ADAM_B1) * gv
        v2 = ADAM_B2 * vv + (1.0 - ADAM_B2) * (gv * gv)
        m_hat = m2 / (1.0 - ADAM_B1 ** ADAM_STEP)
        v_hat = v2 / (1.0 - ADAM_B2 ** ADAM_STEP)
        return -ADAM_LR * (m_hat / (jnp.sqrt(v_hat) + ADAM_EPS) + ADAM_WD * wv), m2, v2

    outs = _rowwise(fn, [t.reshape(two_d) for t in (w, g, m, v)], [], [(shape[-1], F32)] * 3, tm=256, name=name)
    return [o.reshape(shape) for o in outs]


def _pad_cols(w, heads, hd):
    k = w.shape[0]
    return jnp.pad(w.reshape(k, heads, hd), ((0, 0), (0, 0), (0, LANE - hd))).reshape(k, heads * LANE)


def _unpad_cols(w, heads, hd):
    k = w.shape[0]
    return w.reshape(k, heads, LANE)[:, :, :hd].reshape(k, heads * hd)


def _pad_rows(w, heads, hd):
    n = w.shape[1]
    return jnp.pad(w.reshape(heads, hd, n), ((0, 0), (0, LANE - hd), (0, 0))).reshape(heads * LANE, n)


def _unpad_rows(w, heads, hd):
    n = w.shape[1]
    return w.reshape(heads, LANE, n)[:, :hd, :].reshape(heads * hd, n)


def _ffn_fwd(x, g, wgu, wd, tag):
    h = _norm_rows(x, g, name=tag + "_norm")
    gate, up, act = _mm_swiglu(h, wgu, name=tag + "_gu")
    y = _mm(act, wd, scale=0.5, residual=x, name=tag + "_down")
    return y, (x, h, gate, up, act)


def _ffn_bwd(dy, saved, g, wgu, wd, tag):
    x, h, gate, up, act = saved
    F = wd.shape[0]
    dwd = _mm(act, dy, ta=True, scale=0.5, name=tag + "_dwd")
    dgate, dup = _mm_dswiglu(dy, wd, gate, up, scale=0.5, name=tag + "_dact")
    dh = _mm(dgate, wgu, tb=True, name=tag + "_dh_g")
    dx, dg = _mm(dup, wgu, tb=True, b_off=(0, F), residual=dh, norm_bwd=(x, g, dy), name=tag + "_dh_u")
    dwgu = jnp.concatenate([_mm(h, dgate, ta=True, name=tag + "_dwg"), _mm(h, dup, ta=True, name=tag + "_dwu")], axis=1)
    return dx, dg, dwgu, dwd


def _even_weights(w_in, w_out):
    parts = [w_in[:, :SB_W] * SB_SCALE, w_in[:, SB_W:2 * SB_W], w_in[:, 2 * SB_W:3 * SB_W]]
    wqkv = jnp.concatenate([_pad_cols(p, SB_HEADS, SB_HD) for p in parts], axis=1)
    return wqkv, w_in[:, 3 * SB_W:], _pad_rows(w_out[:SB_W], SB_HEADS, SB_HD), w_out[SB_W:]


def _even_fwd(x, g, wts, ln_g, ln_b, sgu_w, bt, late_shard, tag):
    wqkv, wz, wo_sb, wo_sg = wts
    h = _norm_rows(x, g, name=tag + "_norm")
    qkv = _mm(h, wqkv, out_dtype=BF16, name=tag + "_qkv")
    z = _mm(h, wz, name=tag + "_z")
    o_sb, tot, late = _attn_fwd(qkv, qkv, qkv, sb=True, causal=True, heads=SB_HEADS, dq=LANE, dv=LANE, group=FWD_GROUP,
                                kcol=lambda g: SB_HEADS // FWD_GROUP + g, vcol=lambda g: 2 * SB_HEADS // FWD_GROUP + g,
                                side_gather=late_shard, name=tag + "_sb")
    u, gl = _gelu_ln(z, ln_g, ln_b, name=tag + "_geluln")
    o_sg = _spatial(gl, u, sgu_w, bt, name=tag + "_sgu")
    y = _mm(o_sb, wo_sb, residual=x, name=tag + "_out_sb")
    y = _mm(o_sg, wo_sg, residual=y, name=tag + "_out_sg")
    return y, (x, h, qkv, z, o_sb, tot, u, gl, o_sg), late


def _even_bwd(dy, saved, g, wts, ln_g, sgu_w, bt, partials, tag):
    wqkv, wz, wo_sb, wo_sg = wts
    x, h, qkv, z, o_sb, tot, u, gl, o_sg = saved
    do_sb = _mm(dy, wo_sb, tb=True, name=tag + "_do_sb")
    do_sg = _mm(dy, wo_sg, tb=True, name=tag + "_do_sg")
    dwo = jnp.concatenate([_unpad_rows(_mm(o_sb, dy, ta=True, name=tag + "_dwo_sb"), SB_HEADS, SB_HD),
                           _mm(o_sg, dy, ta=True, name=tag + "_dwo_sg")], axis=0)
    dq, dk_t, dv_t, received = _attn_bwd(qkv, qkv, qkv, o_sb, do_sb, tot, sb=True, causal=True, heads=SB_HEADS, dq=LANE,
                                         dv=LANE, kcol=lambda hh: SB_HEADS + hh, vcol=lambda hh: 2 * SB_HEADS + hh,
                                         side_exchange=partials, name=tag + "_sb_bwd")
    du, dgl, dsgu_w, db_t = _spatial_bwd(do_sg, gl, u, sgu_w, bt, name=tag + "_sgu_bwd")
    dz, dln_g, dln_b = _gelu_ln_bwd(z, du, dgl, ln_g, name=tag + "_geluln_bwd")
    dh = _mm(dz, wz, tb=True, name=tag + "_dh_z")
    dh = _mm(dq, wqkv, tb=True, residual=dh, name=tag + "_dh_q")
    dws = [_unpad_cols(_mm(h, dq, ta=True, scale=SB_SCALE, name=tag + "_dw_q"), SB_HEADS, SB_HD)]
    for i, (d_t, nm) in enumerate(((dk_t, "k"), (dv_t, "v")), start=1):
        dh = _mm(d_t, wqkv, ta=True, tb=True, b_off=(0, i * SB_HEADS * LANE), residual=dh,
                 norm_bwd=(x, g, dy) if nm == "v" else None, name=tag + "_dh_" + nm)
        dws.append(_unpad_rows(_mm(d_t, h, name=tag + "_dw_" + nm), SB_HEADS, SB_HD).T)
    dws.append(_mm(h, dz, ta=True, name=tag + "_dw_z"))
    dx, dg = dh
    return dx, dict(mix_norm=dg, sbg_w_in=jnp.concatenate(dws, axis=1), sgu_ln_gain=dln_g, sgu_ln_bias=dln_b,
                    sgu_w=dsgu_w, sgu_b=db_t[:, :SG_GROUPS].T, sbg_w_out=dwo), received


def _mla_weights(w_in, w_uq, w_ukv, w_out, q_gain, k_gain):
    d = w_in.shape[0]
    lat = MLA_QL + MLA_KVL
    w_in_ext = jnp.concatenate([w_in[:, :lat], jnp.zeros((d, MLA_NOPE), w_in.dtype), w_in[:, lat:],
                                jnp.zeros((d, LANE - MLA_QK), w_in.dtype)], axis=1)
    kv = w_ukv.reshape(MLA_KVL, MLA_HEADS, MLA_NOPE + MLA_V)
    wk = _pad_cols(kv[:, :, :MLA_NOPE].reshape(MLA_KVL, -1), MLA_HEADS, MLA_NOPE)
    wv = _pad_cols(kv[:, :, MLA_NOPE:].reshape(MLA_KVL, -1), MLA_HEADS, MLA_V)
    pad_gain = lambda gn: jnp.pad(gn.reshape(1, MLA_QK), ((0, 0), (0, LANE - MLA_QK)))
    return (w_in_ext, _pad_cols(w_uq, MLA_HEADS, MLA_QK), wk, wv, _pad_rows(w_out, MLA_HEADS, MLA_V),
            pad_gain(q_gain), pad_gain(k_gain))


def _mla_fwd(x, g, wts, qlg, kvlg, tabs, tag):
    w_in, w_uq, wk, wv, w_out, qg, kg = wts
    h = _norm_rows(x, g, name=tag + "_norm")
    P = _mm(h, w_in, name=tag + "_in")
    cqn, ckvn = _mla_lora(P, qlg, kvlg, name=tag + "_lora")
    q_pre = _mm(cqn, w_uq, name=tag + "_uq")
    k_pre = _mm(ckvn, wk, name=tag + "_uk")
    ones_lane = jnp.tile((jnp.arange(LANE) == MLA_V).astype(F32), MLA_HEADS)[None, :]
    v = _mm(ckvn, wv, out_dtype=BF16, bias=ones_lane, name=tag + "_uv")
    q, k = _mla_qk(q_pre, k_pre, P, tabs, qg, kg, name=tag + "_qk")
    o, lse = _attn_fwd(q, k, v, sb=False, causal=True, heads=MLA_HEADS, dq=LANE, dv=LANE, group=FWD_GROUP,
                       sum_lane=MLA_V, name=tag + "_attn")
    y = _mm(o, w_out, residual=x, name=tag + "_out")
    return y, (x, h, P, cqn, ckvn, q_pre, k_pre, q, k, v, o, lse)


def _mla_bwd(dy, saved, g, wts, qlg, kvlg, tabs, tag):
    w_in, w_uq, wk, wv, w_out, qg, kg = wts
    x, h, P, cqn, ckvn, q_pre, k_pre, q, k, v, o, lse = saved
    do = _mm(dy, w_out, tb=True, name=tag + "_do")
    dw_out = _unpad_rows(_mm(o, dy, ta=True, name=tag + "_dwo"), MLA_HEADS, MLA_V)
    dq, dk_t, dv_t = _attn_bwd(q, k, v, o, do, lse, sb=False, causal=True, heads=MLA_HEADS, dq=LANE, dv=LANE,
                               name=tag + "_attn_bwd")
    dq_pre, dk_pre, dkr, dqg, dkg = _mla_qk_bwd(dq, dk_t, q_pre, k_pre, P, tabs, qg, kg, name=tag + "_qk_bwd")
    dcqn = _mm(dq_pre, w_uq, tb=True, name=tag + "_dcq")
    dckvn = _mm(dk_pre, wk, tb=True, name=tag + "_dckv_k")
    dckvn = _mm(dv_t, wv, ta=True, tb=True, residual=dckvn, name=tag + "_dckv_v")
    dw_uq = _unpad_cols(_mm(cqn, dq_pre, ta=True, name=tag + "_dwuq"), MLA_HEADS, MLA_QK)
    dwk = _unpad_cols(_mm(ckvn, dk_pre, ta=True, name=tag + "_dwk"), MLA_HEADS, MLA_NOPE)
    dwv = _unpad_rows(_mm(dv_t, ckvn, name=tag + "_dwv"), MLA_HEADS, MLA_V).T
    dw_ukv = jnp.concatenate([dwk.reshape(MLA_KVL, MLA_HEADS, MLA_NOPE), dwv.reshape(MLA_KVL, MLA_HEADS, MLA_V)],
                             axis=2).reshape(MLA_KVL, -1)
    dP, dqlg, dkvlg = _mla_lora_bwd(dcqn, dckvn, dkr, P, qlg, kvlg, name=tag + "_lora_bwd")
    dx, dg = _mm(dP, w_in, tb=True, norm_bwd=(x, g, dy), name=tag + "_dh")
    dw_in_ext = _mm(h, dP, ta=True, name=tag + "_dwin")
    lat = MLA_QL + MLA_KVL
    dw_in = jnp.concatenate([dw_in_ext[:, :lat], dw_in_ext[:, lat + MLA_NOPE:lat + MLA_QK]], axis=1)
    return dx, dict(mix_norm=dg, mla_w_in=dw_in, mla_q_lora_gain=dqlg, mla_kv_lora_gain=dkvlg, mla_w_uq=dw_uq,
                    mla_w_ukv=dw_ukv, mla_q_gain=dqg[:, :MLA_QK], mla_k_gain=dkg[:, :MLA_QK], mla_w_out=dw_out)


def _xmem_fwd(x, mem, g, gm, wq, wkv, qg, kg, wo, tag):
    hq = _norm_rows(x, g, name=tag + "_norm")
    hm = _norm_rows(mem, gm, name=tag + "_mnorm")
    qp = _mm(hq, wq, name=tag + "_q")
    kv = _mm(hm, wkv, name=tag + "_kv")
    q = _head_norm(qp, qg, heads=MEM_HEADS, width=MEM_HD, scale=MEM_SCALE, name=tag + "_qn")
    kn = _head_norm(kv, kg, heads=MEM_HEADS, width=MEM_HD, colfn=lambda hh: 2 * hh, name=tag + "_kn")
    kvb = kv.reshape(-1, MEM_HEADS, 2, MEM_HD)[:, :, 1].reshape(-1, MEM_HEADS * MEM_HD).astype(BF16)
    o, lse = _attn_fwd(q, kn, kvb, sb=False, causal=False, heads=MEM_HEADS, dq=MEM_HD, dv=MEM_HD, group=MEM_HEADS,
                       name=tag + "_attn")
    y = _mm(o, wo, residual=x, name=tag + "_out")
    return y, (x, hq, hm, qp, kv, q, kn, kvb, o, lse)


def _xmem_bwd(dy, saved, mem, g, gm, wq, wkv, qg, kg, wo, tag):
    x, hq, hm, qp, kv, q, kn, kvb, o, lse = saved
    m = mem.shape[0]
    do = _mm(dy, wo, tb=True, name=tag + "_do")
    dwo = _mm(o, dy, ta=True, name=tag + "_dwo")
    dq, dk_t, dv_t = _attn_bwd(q, kn, kvb, o, do, lse, sb=False, causal=False, heads=MEM_HEADS, dq=MEM_HD, dv=MEM_HD,
                               name=tag + "_attn_bwd")
    dk, dv = dk_t.T, dv_t.T
    dqp, dqg = _head_norm_bwd(dq, qp, qg, heads=MEM_HEADS, width=MEM_HD, scale=MEM_SCALE, out_dtype=BF16,
                              name=tag + "_qn_bwd")
    dkp, dkg = _head_norm_bwd(dk, kv, kg, heads=MEM_HEADS, width=MEM_HD, colfn=lambda hh: 2 * hh, out_dtype=F32,
                              name=tag + "_kn_bwd")
    dkv = jnp.concatenate([dkp.reshape(m, MEM_HEADS, MEM_HD), dv.reshape(m, MEM_HEADS, MEM_HD)], axis=2).reshape(m, -1)
    dwkv = _mm(hm, dkv, ta=True, name=tag + "_dwkv")
    dhm = _mm(dkv, wkv, tb=True, name=tag + "_dhm")
    _, dgm = _norm_rows_bwd(dhm, mem, gm, None, name=tag + "_dmnorm")
    dwq = _mm(hq, dqp, ta=True, name=tag + "_dwq")
    dx, dg = _mm(dqp, wq, tb=True, norm_bwd=(x, g, dy), name=tag + "_dhq")
    return dx, dict(xmem_norm=dg, xmem_mem_norm=dgm, xmem_wq=dwq, xmem_wkv=dwkv, xmem_q_gain=dqg, xmem_k_gain=dkg,
                    xmem_wo=dwo)


def _local_step(x, mem, positions, tgt, w, late_shard, finish_late, partials_so_far):
    tabs = _rope_tables(positions)
    even = _even_weights(w["sbg_w_in"][0], w["sbg_w_out"][0])
    bt = jnp.repeat(w["sgu_b"][0].T, SG_GD, axis=1)
    saved = []
    for l in range(2):
        t = f"l{l}"
        x, s_pre = _ffn_fwd(x, w["ffn_pre_norm"][l], w["ffn_pre_w_gu"][l], w["ffn_pre_w_down"][l], t + "_pre")
        if l == 0:
            x, s_mix, late = _even_fwd(x, w["mix_norm"][0], even, w["sgu_ln_gain"][0], w["sgu_ln_bias"][0], w["sgu_w"][0],
                                       bt, late_shard, t + "_even")
            finish_late(late)
            mla = _mla_weights(w["mla_w_in"][0], w["mla_w_uq"][0], w["mla_w_ukv"][0], w["mla_w_out"][0],
                               w["mla_q_gain"][0], w["mla_k_gain"][0])
        else:
            x, s_mix = _mla_fwd(x, w["mix_norm"][1], mla, w["mla_q_lora_gain"][0], w["mla_kv_lora_gain"][0], tabs,
                                t + "_mla")
        x, s_xm = _xmem_fwd(x, mem, w["xmem_norm"][l], w["xmem_mem_norm"][l], w["xmem_wq"][l], w["xmem_wkv"][l],
                            w["xmem_q_gain"][l], w["xmem_k_gain"][l], w["xmem_wo"][l], t + "_xm")
        x, s_post = _ffn_fwd(x, w["ffn_post_norm"][l], w["ffn_post_w_gu"][l], w["ffn_post_w_down"][l], t + "_post")
        saved.append((s_pre, s_mix, s_xm, s_post))
    loss, dx = _loss_grad(x, tgt, name="loss")
    grads = {}

    def put(name, l, val):
        grads.setdefault(name, {})[l] = val

    for l in (1, 0):
        t = f"l{l}"
        s_pre, s_mix, s_xm, s_post = saved[l]
        dx, dg, dwgu, dwd = _ffn_bwd(dx, s_post, w["ffn_post_norm"][l], w["ffn_post_w_gu"][l], w["ffn_post_w_down"][l],
                                     t + "_post")
        put("ffn_post_norm", l, dg), put("ffn_post_w_gu", l, dwgu), put("ffn_post_w_down", l, dwd)
        dx, gx = _xmem_bwd(dx, s_xm, mem, w["xmem_norm"][l], w["xmem_mem_norm"][l], w["xmem_wq"][l], w["xmem_wkv"][l],
                           w["xmem_q_gain"][l], w["xmem_k_gain"][l], w["xmem_wo"][l], t + "_xm")
        for k_, v_ in gx.items():
            put(k_, l, v_)
        if l == 0:
            dx, gm, received = _even_bwd(dx, s_mix, w["mix_norm"][0], even, w["sgu_ln_gain"][0], w["sgu_w"][0], bt,
                                         partials_so_far(grads), t + "_even")
        else:
            dx, gm = _mla_bwd(dx, s_mix, w["mix_norm"][1], mla, w["mla_q_lora_gain"][0], w["mla_kv_lora_gain"][0], tabs,
                              t + "_mla")
        for k_, v_ in gm.items():
            put(k_, l if k_ == "mix_norm" else 0, v_)
        dx, dg, dwgu, dwd = _ffn_bwd(dx, s_pre, w["ffn_pre_norm"][l], w["ffn_pre_w_gu"][l], w["ffn_pre_w_down"][l],
                                     t + "_pre")
        put("ffn_pre_norm", l, dg), put("ffn_pre_w_gu", l, dwgu), put("ffn_pre_w_down", l, dwd)
    return loss, dx, {k_: [v_[l] for l in sorted(v_)] for k_, v_ in grads.items()}, received


N_CHIPS = 4
PACK_COLS = 1024
PACK_ROW_MULTIPLE = 512


def _place():
    x, y, c = lax.axis_index("x"), lax.axis_index("y"), lax.axis_index("c")
    return x, y, c, [(1 - x, y), (x, 1 - y), (1 - x, 1 - y)]


def _hops(x, y, c):
    return ((x + 1 - c) % 2, (y + c) % 2), ((x + c) % 2, (y + 1 - c) % 2), (1 - x, 1 - y)


GATHER_COPIES = 6
GATHER_STAGES = 4


def _gather_stages(x_ref, out_ref, send_sems, recv_sems):
    Rh = x_ref.shape[0] // 2
    x, y, c = lax.axis_index("x"), lax.axis_index("y"), lax.axis_index("c")
    n1, n2, nd = _hops(x, y, c)
    me, q1, q2, qd = 2 * x + y, 2 * n1[0] + n1[1], 2 * n2[0] + n2[1], 2 * nd[0] + nd[1]
    sibling = (x, y, 1 - c)

    def half(chip, core):
        return out_ref.at[chip, pl.ds(core * Rh, Rh), :]

    def copy(k, chip, core, to, own=False):
        return pltpu.make_async_remote_copy(src_ref=x_ref.at[pl.ds(c * Rh, Rh), :] if own else half(chip, core),
                                            dst_ref=half(chip, core), send_sem=send_sems.at[k], recv_sem=recv_sems.at[k],
                                            device_id=to, device_id_type=MESH)

    sends = [lambda: copy(0, me, c, (*n1, c), own=True), lambda: copy(1, me, c, (*n2, c), own=True),
             lambda: copy(2, q1, c, (*n2, c)), lambda: copy(3, q1, c, sibling), lambda: copy(4, q2, c, sibling),
             lambda: copy(5, qd, c, sibling)]

    def own_halves_out():
        sends[0]().start()
        sends[1]().start()

    def first_neighbours_on():
        copy(0, q1, c, sibling).wait_recv()
        sends[2]().start()
        sends[3]().start()

    def others_to_sibling():
        copy(1, q2, c, sibling).wait_recv()
        sends[4]().start()
        copy(2, qd, c, sibling).wait_recv()
        sends[5]().start()

    def all_landed():
        copy(3, q2, 1 - c, sibling).wait_recv()
        copy(4, q1, 1 - c, sibling).wait_recv()
        copy(5, qd, 1 - c, sibling).wait_recv()
        for send in sends:
            send().wait_send()

    return own_halves_out, first_neighbours_on, others_to_sibling, all_landed


N_DEVICES = 8


def _partials_exchange(g_ref, recv_ref, send_sems, recv_sems):
    Rh = g_ref.shape[1] // 2
    x, y, c = lax.axis_index("x"), lax.axis_index("y"), lax.axis_index("c")

    def copy(k):
        tx, ty, tc = (x + (k >> 2)) % 2, (y + ((k >> 1) & 1)) % 2, (c + (k & 1)) % 2
        return pltpu.make_async_remote_copy(src_ref=g_ref.at[2 * tx + ty, pl.ds(tc * Rh, Rh), :],
                                            dst_ref=recv_ref.at[4 * x + 2 * y + c], send_sem=send_sems.at[k],
                                            recv_sem=recv_sems.at[k], device_id=(tx, ty, tc), device_id_type=MESH)

    def start():
        for k in range(1, N_DEVICES):
            copy(k).start()

    def finish():
        for k in range(1, N_DEVICES):
            copy(k).wait_recv()
            copy(k).wait_send()

    return start, finish


def _sum_partials(g, recv, *, name):
    _, R, C = g.shape
    Rh = R // 2
    tr = _row_tile(Rh, 512)
    nt = Rh // tr
    x, y, c = lax.axis_index("x"), lax.axis_index("y"), lax.axis_index("c")
    where = jnp.stack([2 * x + y, c, 4 * x + 2 * y + c]).astype(jnp.int32)

    def body(where_ref, g_ref, r_ref, o_ref):
        own, mine = g_ref[0], where_ref[2]
        total = None
        for d in range(N_DEVICES):
            term = jnp.where(mine == d, own, r_ref[d].astype(F32))
            total = term if total is None else total + term
        o_ref[...] = total

    spec = pltpu.PrefetchScalarGridSpec(
        num_scalar_prefetch=1, grid=(nt,),
        in_specs=[pl.BlockSpec((1, tr, C), lambda i, wh: (wh[0], wh[1] * nt + i, 0)),
                  pl.BlockSpec((N_DEVICES, tr, C), lambda i, wh: (0, i, 0))],
        out_specs=pl.BlockSpec((tr, C), lambda i, wh: (i, 0)))
    return pl.pallas_call(body, name=name, grid_spec=spec, out_shape=jax.ShapeDtypeStruct((Rh, C), F32),
                          compiler_params=_params(1))(where, g, recv)


def _place_own_slot(others, shard):
    return lax.dynamic_update_slice(others, shard[None], (2 * lax.axis_index("x") + lax.axis_index("y"), 0, 0))


def _gather_chips(shard):
    def body(x_ref, out_ref, send_sems, recv_sems):
        for stage in _gather_stages(x_ref, out_ref, send_sems, recv_sems):
            stage()

    others = pl.pallas_call(
        body, name="gather_weights", out_shape=jax.ShapeDtypeStruct((N_CHIPS,) + shard.shape, shard.dtype),
        in_specs=[ANY], out_specs=ANY,
        scratch_shapes=[pltpu.SemaphoreType.DMA((GATHER_COPIES,)), pltpu.SemaphoreType.DMA((GATHER_COPIES,))])(shard)
    return _place_own_slot(others, shard)


def _gather_devices(block):
    M, N = block.shape

    def body(x_ref, out_ref, send_sems, recv_sems, local_sem):
        x, y, c, chips = _place()
        me, sibling = (x, y, c), (x, y, 1 - c)

        def rows(px, py, pc):
            return out_ref.at[pl.ds((4 * px + 2 * py + pc) * M, M), :]

        def copy(k, blk, to, src=None):
            return pltpu.make_async_remote_copy(src_ref=rows(*blk) if src is None else src, dst_ref=rows(*blk),
                                                send_sem=send_sems.at[k], recv_sem=recv_sems.at[k], device_id=to,
                                                device_id_type=MESH)

        mine = pltpu.make_async_copy(x_ref, rows(*me), local_sem)
        mine.start()
        first = [copy(0, me, sibling, src=x_ref)]
        first += [copy(1 + j, me, (*chip, c), src=x_ref) for j, chip in enumerate(chips)]
        for cp in first:
            cp.start()
        passed = [copy(4 + j, (*chip, c), sibling) for j, chip in enumerate(chips)]
        for j, chip in enumerate(chips):
            copy(1 + j, (*chip, c), me).wait_recv()
            passed[j].start()
        copy(0, sibling, me).wait_recv()
        for j, chip in enumerate(chips):
            copy(4 + j, (*chip, 1 - c), me).wait_recv()
        for cp in first + passed:
            cp.wait_send()
        mine.wait()

    vmem = pl.BlockSpec(memory_space=pltpu.VMEM)
    return pl.pallas_call(
        body, name=f"gather_devices_{M}", out_shape=jax.ShapeDtypeStruct((8 * M, N), block.dtype),
        in_specs=[vmem], out_specs=vmem,
        scratch_shapes=[pltpu.SemaphoreType.DMA((7,)), pltpu.SemaphoreType.DMA((7,)), pltpu.SemaphoreType.DMA],
        compiler_params=pltpu.CompilerParams(vmem_limit_bytes=VMEM_LIMIT))(block)


def _swap_halves(g):
    n, R, C = g.shape
    Rh = R // 2

    def body(g_ref, a_ref, send_sem, recv_sem):
        x, y, c, _ = _place()
        cp = pltpu.make_async_remote_copy(src_ref=g_ref.at[:, pl.ds((1 - c) * Rh, Rh), :], dst_ref=a_ref,
                                          send_sem=send_sem, recv_sem=recv_sem, device_id=(x, y, 1 - c),
                                          device_id_type=MESH)
        cp.start()
        cp.wait()

    return pl.pallas_call(body, name="grad_swap_halves", out_shape=jax.ShapeDtypeStruct((n, Rh, C), g.dtype),
                          in_specs=[ANY], out_specs=ANY,
                          scratch_shapes=[pltpu.SemaphoreType.DMA, pltpu.SemaphoreType.DMA])(g)


def _add_picked(a, b, picks, *, a_row_half=None, out_dtype, name):
    n_out = picks.shape[0]
    _, rows, C = b.shape
    tr = _row_tile(rows, 512)
    nt = rows // tr
    half = jnp.zeros((1,), jnp.int32) if a_row_half is None else a_row_half

    def body(pick_ref, half_ref, a_ref, b_ref, o_ref):
        o_ref[...] = (a_ref[...].astype(F32) + b_ref[...].astype(F32)).astype(o_ref.dtype)

    spec = pltpu.PrefetchScalarGridSpec(
        num_scalar_prefetch=2, grid=(n_out, nt),
        in_specs=[pl.BlockSpec((1, tr, C), lambda j, i, pick, hf: (pick[j], hf[0] * nt + i, 0)),
                  pl.BlockSpec((1, tr, C), lambda j, i, pick, hf: (pick[j], i, 0))],
        out_specs=pl.BlockSpec((1, tr, C), lambda j, i, pick, hf: (j, i, 0)))
    return pl.pallas_call(body, name=name, grid_spec=spec, out_shape=jax.ShapeDtypeStruct((n_out, rows, C), out_dtype),
                          compiler_params=_params(2))(picks.astype(jnp.int32), half.astype(jnp.int32), a, b)


def _hop_exchange(src, hop, *, name):
    def body(s_ref, d_ref, send_sem, recv_sem):
        x, y, c = lax.axis_index("x"), lax.axis_index("y"), lax.axis_index("c")
        cp = pltpu.make_async_remote_copy(src_ref=s_ref, dst_ref=d_ref, send_sem=send_sem, recv_sem=recv_sem,
                                          device_id=(*_hops(x, y, c)[hop], c), device_id_type=MESH)
        cp.start()
        cp.wait()

    return pl.pallas_call(body, name=name, out_shape=jax.ShapeDtypeStruct(src.shape, src.dtype), in_specs=[ANY],
                          out_specs=ANY, scratch_shapes=[pltpu.SemaphoreType.DMA, pltpu.SemaphoreType.DMA])(src)


def _reduce_over_chips(g):
    x, y, c = lax.axis_index("x"), lax.axis_index("y"), lax.axis_index("c")
    n1, n2, _ = _hops(x, y, c)
    chip = lambda p: 2 * p[0] + p[1]
    near = jnp.stack([chip((x, y)), chip(n2)])
    far = jnp.stack([chip(n1), chip((1 - x, 1 - y))])
    half = c.reshape(1)
    sib = _swap_halves(g)
    kept = _add_picked(g, sib, near, a_row_half=half, out_dtype=F32, name="grad_add_near")
    sent = _add_picked(g, sib, far, a_row_half=half, out_dtype=BF16, name="grad_add_far")
    got = _hop_exchange(sent, 0, name="grad_hop_first")
    mine = _add_picked(kept, got, jnp.zeros((1,), jnp.int32), out_dtype=F32, name="grad_add_mine")
    theirs = _add_picked(kept, got, jnp.ones((1,), jnp.int32), out_dtype=BF16, name="grad_add_theirs")
    got = _hop_exchange(theirs, 1, name="grad_hop_second")
    total = _add_picked(mine, got, jnp.zeros((1,), jnp.int32), out_dtype=F32, name="grad_add_total")
    return _join_halves(total[0])


def _sum_slots(b, *, name):
    n, R, C = b.shape
    tr = _row_tile(R, 512)

    def body(b_ref, o_ref):
        acc = b_ref[0]
        for q in range(1, n):
            acc = acc + b_ref[q]
        o_ref[...] = acc

    return pl.pallas_call(body, name=name, grid=(R // tr,), in_specs=[pl.BlockSpec((n, tr, C), lambda i: (0, i, 0))],
                          out_specs=pl.BlockSpec((tr, C), lambda i: (i, 0)), out_shape=jax.ShapeDtypeStruct((R, C), F32),
                          compiler_params=_params(1))(b)


def _join_halves(r):
    Rh, C = r.shape

    def body(r_ref, o_ref, send_sem, recv_sem):
        x, y, c, _ = _place()
        own, other = o_ref.at[pl.ds(c * Rh, Rh), :], o_ref.at[pl.ds((1 - c) * Rh, Rh), :]
        cp = pltpu.make_async_remote_copy(src_ref=r_ref, dst_ref=own, send_sem=send_sem, recv_sem=recv_sem,
                                          device_id=(x, y, 1 - c), device_id_type=MESH)
        cp.start()
        pltpu.make_async_remote_copy(src_ref=r_ref, dst_ref=other, send_sem=send_sem, recv_sem=recv_sem,
                                     device_id=(x, y, 1 - c), device_id_type=MESH).wait_recv()
        cp.wait_send()

    theirs = pl.pallas_call(
        body, name="grad_join_halves", out_shape=jax.ShapeDtypeStruct((2 * Rh, C), r.dtype), in_specs=[ANY], out_specs=ANY,
        scratch_shapes=[pltpu.SemaphoreType.DMA, pltpu.SemaphoreType.DMA])(r)
    return lax.dynamic_update_slice(theirs, r, (lax.axis_index("c") * Rh, 0))


def _size(shape):
    size = 1
    for d in shape:
        size *= d
    return size


def _pack(pieces, cols, row_multiple, dtype):
    if any(p.size % cols for p in pieces):
        flat = jnp.concatenate([p.reshape(-1).astype(dtype) for p in pieces])
        pieces = [jnp.pad(flat, (0, -flat.shape[0] % cols))]
    rows = [p.reshape(-1, cols).astype(dtype) for p in pieces]
    pad = -sum(r.shape[0] for r in rows) % row_multiple
    return jnp.concatenate(rows + ([jnp.zeros((pad, cols), dtype)] if pad else []), axis=0)


def _unpack(buf, shapes):
    cols = buf.shape[1]
    if any(_size(s) % cols for s in shapes):
        flat, out, at = buf.reshape(-1), [], 0
        for shp in shapes:
            out.append(flat[at:at + _size(shp)].reshape(shp))
            at += _size(shp)
        return out
    out, at = [], 0
    for shp in shapes:
        out.append(buf[at:at + _size(shp) // cols].reshape(shp))
        at += _size(shp) // cols
    return out


SHARDED = (("ffn_pre_w_gu", 2), ("ffn_pre_w_down", 1), ("sbg_w_in", 2), ("sbg_w_out", 1), ("mla_w_in", 1),
           ("mla_w_uq", 2), ("mla_w_ukv", 2), ("mla_w_out", 1), ("xmem_wq", 1), ("xmem_wkv", 2), ("xmem_wo", 1),
           ("ffn_post_w_gu", 2), ("ffn_post_w_down", 1))
EARLY = (("ffn_pre_w_gu", 0), ("ffn_pre_w_down", 0), ("sbg_w_in", 0), ("sbg_w_out", 0))
LORA_GAINS = ("mla_q_lora_gain", "mla_kv_lora_gain")
REPLICATED = ("ffn_pre_norm", "mix_norm", "sgu_ln_gain", "sgu_ln_bias", "sgu_w", "sgu_b", "mla_q_gain", "mla_k_gain",
              "xmem_norm", "xmem_mem_norm", "xmem_q_gain", "xmem_k_gain", "ffn_post_norm")
WEIGHTS = ("ffn_pre_norm", "ffn_pre_w_gu", "ffn_pre_w_down", "mix_norm", "sbg_w_in", "sgu_ln_gain", "sgu_ln_bias", "sgu_w",
           "sgu_b", "sbg_w_out", "mla_w_in", "mla_q_lora_gain", "mla_kv_lora_gain", "mla_w_uq", "mla_w_ukv", "mla_q_gain",
           "mla_k_gain", "mla_w_out", "xmem_norm", "xmem_mem_norm", "xmem_wq", "xmem_wkv", "xmem_q_gain", "xmem_k_gain",
           "xmem_wo", "ffn_post_norm", "ffn_post_w_gu", "ffn_post_w_down")
INPUTS = ("x", "mem", "positions") + WEIGHTS + ("loss_target",) + tuple("m_" + n for n in WEIGHTS) + tuple(
    "v_" + n for n in WEIGHTS)


def _step(a):
    x, y, c, _ = _place()
    chip = 2 * x + y
    w = {n: [None] * a[n].shape[0] for n, _ in SHARDED}
    lots = {early: [(n, l, ax) for n, ax in SHARDED for l in range(a[n].shape[0]) if ((n, l) in EARLY) == early]
            for early in (True, False)}

    def packed(lot):
        return _pack([a[n][l] for n, l, _ in lot], PACK_COLS, PACK_ROW_MULTIPLE, BF16)

    def unpack(gathered, lot):
        at = 0
        for n, l, ax in lot:
            shp = a[n].shape[1:]
            rows = _size(shp) // PACK_COLS
            per_chip = gathered[:, at:at + rows].reshape((N_CHIPS,) + shp)
            at += rows
            w[n][l] = jnp.moveaxis(per_chip, 0, ax - 1).reshape(shp[:ax - 1] + (N_CHIPS * shp[ax - 1],) + shp[ax:])

    unpack(_gather_chips(packed(lots[True])), lots[True])
    gains = jnp.zeros((8, LANE), F32)
    for r, n in enumerate(LORA_GAINS):
        gains = gains.at[r, :a[n].shape[1]].set(a[n][0])
    gains = _gather_devices(gains)
    for r, n in enumerate(LORA_GAINS):
        w[n] = jnp.concatenate([gains[16 * q + r, :a[n].shape[1]] for q in range(N_CHIPS)])[None, :]
    for n in REPLICATED:
        w[n] = a[n]

    def packed_grads(lot, grads):
        def part(n, l, ax, q):
            size = a[n].shape[ax]
            return lax.slice_in_dim(grads[n][l], q * size, (q + 1) * size, axis=ax - 1)

        return jnp.stack([_pack([part(n, l, ax, q) for n, l, ax in lot], PACK_COLS, PACK_ROW_MULTIPLE, F32)
                          for q in range(N_CHIPS)])

    sent = {}

    def partials_so_far(grads):
        sent["f32"] = packed_grads(lots[False], grads)
        return sent["f32"].astype(BF16)

    loss, dx, grads, received = _local_step(a["x"][0], a["mem"][0], a["positions"][0], a["loss_target"][0], w,
                                            packed(lots[False]), lambda gathered: unpack(gathered, lots[False]),
                                            partials_so_far)
    loss = lax.psum(loss, ("x", "y", "c"))
    small_names = REPLICATED + LORA_GAINS
    full = {n: jnp.stack(grads[n]).reshape(w[n].shape) for n in small_names}

    reduced = {False: _join_halves(_sum_partials(sent["f32"], received, name="grad_sum_partials")),
               True: _reduce_over_chips(packed_grads(lots[True], grads))}
    layers = {n: [None] * a[n].shape[0] for n, _ in SHARDED}
    for early, lot in lots.items():
        for (n, l, _), piece in zip(lot, _unpack(reduced[early], [a[n].shape[1:] for n, _, _ in lot])):
            layers[n][l] = piece
    gw = {n: jnp.stack(layers[n]) for n, _ in SHARDED}

    small = _pack([full[n] for n in small_names], LANE, 256, F32)
    rows = small.shape[0]
    summed = _sum_slots(_gather_devices(small).reshape(8, rows, LANE), name="grad_sum_devices")
    for n, val in zip(small_names, _unpack(summed, [full[n].shape for n in small_names])):
        if n in LORA_GAINS:
            size = a[n].shape[1]
            val = lax.dynamic_slice_in_dim(val, chip * size, size, axis=1)
        gw[n] = val

    upd = {n: _adamw(a[n], gw[n], a["m_" + n], a["v_" + n], name="adamw_" + n) for n in WEIGHTS}
    return (loss, dx[None], *[gw[n] for n in WEIGHTS], *[upd[n][0] for n in WEIGHTS], *[upd[n][1] for n in WEIGHTS],
            *[upd[n][2] for n in WEIGHTS])


def kernel(x, mem, positions, ffn_pre_norm, ffn_pre_w_gu, ffn_pre_w_down, mix_norm, sbg_w_in, sgu_ln_gain,
           sgu_ln_bias, sgu_w, sgu_b, sbg_w_out, mla_w_in, mla_q_lora_gain, mla_kv_lora_gain, mla_w_uq, mla_w_ukv,
           mla_q_gain, mla_k_gain, mla_w_out, xmem_norm, xmem_mem_norm, xmem_wq, xmem_wkv, xmem_q_gain, xmem_k_gain,
           xmem_wo, ffn_post_norm, ffn_post_w_gu, ffn_post_w_down, loss_target, m_ffn_pre_norm, m_ffn_pre_w_gu,
           m_ffn_pre_w_down, m_mix_norm, m_sbg_w_in, m_sgu_ln_gain, m_sgu_ln_bias, m_sgu_w, m_sgu_b, m_sbg_w_out,
           m_mla_w_in, m_mla_q_lora_gain, m_mla_kv_lora_gain, m_mla_w_uq, m_mla_w_ukv, m_mla_q_gain, m_mla_k_gain,
           m_mla_w_out, m_xmem_norm, m_xmem_mem_norm, m_xmem_wq, m_xmem_wkv, m_xmem_q_gain, m_xmem_k_gain,
           m_xmem_wo, m_ffn_post_norm, m_ffn_post_w_gu, m_ffn_post_w_down, v_ffn_pre_norm, v_ffn_pre_w_gu,
           v_ffn_pre_w_down, v_mix_norm, v_sbg_w_in, v_sgu_ln_gain, v_sgu_ln_bias, v_sgu_w, v_sgu_b, v_sbg_w_out,
           v_mla_w_in, v_mla_q_lora_gain, v_mla_kv_lora_gain, v_mla_w_uq, v_mla_w_ukv, v_mla_q_gain, v_mla_k_gain,
           v_mla_w_out, v_xmem_norm, v_xmem_mem_norm, v_xmem_wq, v_xmem_wkv, v_xmem_q_gain, v_xmem_k_gain,
           v_xmem_wo, v_ffn_post_norm, v_ffn_post_w_gu, v_ffn_post_w_down):
    given = locals()
    return _step({n: given[n] for n in INPUTS})
```

```python
import jax
import jax.numpy as jnp
from jax import lax
from jax.experimental import pallas as pl
from jax.experimental.pallas import tpu as pltpu

F32, BF16 = jnp.float32, jnp.bfloat16
LANE = 128
VMEM_LIMIT = 56 * 1024 * 1024
EPS = 1e-6
SB_HEADS, SB_HD = 8, 64
SG_GROUPS, SG_GD, SG_CHUNK = 8, 64, 128
SB_W, SG_W = SB_HEADS * SB_HD, SG_GROUPS * SG_GD
MLA_HEADS, MLA_NOPE, MLA_ROPE, MLA_V = 16, 64, 32, 64
MLA_QK = MLA_NOPE + MLA_ROPE
MLA_QL, MLA_KVL = 512, 256
ROPE_THETA = 10000.0
MEM_HEADS, MEM_HD = 4, 256
SB_SCALE, MLA_SCALE, MEM_SCALE = SB_HD ** -0.5, MLA_QK ** -0.5, MEM_HD ** -0.5
ADAM_LR, ADAM_B1, ADAM_B2, ADAM_EPS, ADAM_WD, ADAM_STEP = 0.001, 0.9, 0.999, 1e-08, 0.01, 10
MESH = pl.DeviceIdType.MESH
ANY = pl.BlockSpec(memory_space=pl.ANY)


def _params(n_axes):
    return pltpu.CompilerParams(dimension_semantics=("arbitrary",) * n_axes, vmem_limit_bytes=VMEM_LIMIT)


MM_TILE_CAP = 1408
MM_VMEM_BUDGET = 40 * 1024 * 1024


def _tile(dim, cap):
    if dim <= cap:
        return dim
    best = max(t for t in range(LANE, cap + 1, LANE) if dim % t == 0)
    return best


def _k_scratch(count, tile, nk):
    return [pltpu.VMEM(tile, F32)] * count if nk > 1 else []


def _over_k_steps(prods, acc_refs, nk, finish):
    if nk == 1:
        finish(prods)
        return
    kk = pl.program_id(2)

    @pl.when(kk == 0)
    def _():
        for ref, p in zip(acc_refs, prods):
            ref[...] = p

    @pl.when(kk > 0)
    def _():
        for ref, p in zip(acc_refs, prods):
            ref[...] += p

    @pl.when(kk == nk - 1)
    def _():
        finish([ref[...] for ref in acc_refs])


def _mm(a, b, *, ta=False, tb=False, out_dtype=F32, scale=1.0, residual=None, bias=None, norm_bwd=None, a_off=(0, 0),
        b_off=(0, 0), m=None, n=None, k=None, name):
    am, ak = (a.shape[1], a.shape[0]) if ta else a.shape
    bk, bn = (b.shape[1], b.shape[0]) if tb else b.shape
    M, N, K = m or am, n or bn, k or ak
    tm, tn = _tile(M, MM_TILE_CAP if norm_bwd is None else MM_TILE_CAP // 2), _tile(N, MM_TILE_CAP)
    n_full = (residual is not None) + (2 if norm_bwd is not None else 0)
    fixed = tm * tn * (4 + 2 * jnp.dtype(out_dtype).itemsize + 8 * n_full)
    per_k = (tm * (2 * a.dtype.itemsize + 2) + tn * (2 * b.dtype.itemsize + 2))
    tk = _tile(K, max(LANE, (MM_VMEM_BUDGET - fixed) // per_k))
    nm, nn, nk = M // tm, N // tn, K // tk
    assert norm_bwd is None or nn == 1
    a_off = (a_off[0] // (tk if ta else tm), a_off[1] // (tm if ta else tk))
    b_off = (b_off[0] // (tn if tb else tk), b_off[1] // (tk if tb else tn))
    dims = (((0 if ta else 1,), (1 if tb else 0,)), ((), ()))
    n_out = 1 if norm_bwd is None else 2

    def body(*refs):
        a_ref, b_ref = refs[0], refs[1]
        n_in = len(ins)
        o_ref, extras, acc_refs = refs[n_in], refs[2:n_in], refs[n_in + n_out:]
        first_rows = pl.program_id(0) == 0

        def finish(total):
            out = total * scale
            for extra in (extras if norm_bwd is None else extras[:-3]):
                out = out + extra[...].astype(F32)
            if norm_bwd is not None:
                x_ref, g_ref, dres_ref = extras[-3:]
                dg_ref = refs[n_in + 1]
                dx, dg = _rmsnorm_bwd(out, x_ref[...], g_ref[...])
                out = dx + dres_ref[...]

                @pl.when(first_rows)
                def _():
                    dg_ref[...] = jnp.zeros_like(dg_ref)

                dg_ref[...] += dg
            o_ref[...] = out.astype(o_ref.dtype)

        prod = lax.dot_general(a_ref[...].astype(BF16), b_ref[...].astype(BF16), dims, preferred_element_type=F32)
        _over_k_steps([prod], acc_refs, nk, lambda totals: finish(totals[0]))

    (ao0, ao1), (bo0, bo1) = a_off, b_off
    a_spec = (pl.BlockSpec((tk, tm), lambda i, j, kk: (kk + ao0, i + ao1)) if ta
              else pl.BlockSpec((tm, tk), lambda i, j, kk: (i + ao0, kk + ao1)))
    b_spec = (pl.BlockSpec((tn, tk), lambda i, j, kk: (j + bo0, kk + bo1)) if tb
              else pl.BlockSpec((tk, tn), lambda i, j, kk: (kk + bo0, j + bo1)))
    o_spec = pl.BlockSpec((tm, tn), lambda i, j, kk: (i, j))
    ins, in_specs = [a, b], [a_spec, b_spec]
    if residual is not None:
        ins.append(residual)
        in_specs.append(o_spec)
    if bias is not None:
        ins.append(bias)
        in_specs.append(pl.BlockSpec((1, tn), lambda i, j, kk: (0, j)))
    out_specs, out_shape = o_spec, jax.ShapeDtypeStruct((M, N), out_dtype)
    if norm_bwd is not None:
        x, gain, dres = norm_bwd
        row = pl.BlockSpec((1, tn), lambda i, j, kk: (0, 0))
        ins += [x, gain.reshape(1, N), dres]
        in_specs += [o_spec, row, o_spec]
        out_specs, out_shape = [o_spec, row], [out_shape, jax.ShapeDtypeStruct((1, N), F32)]
    return pl.pallas_call(
        body, name=name, grid=(nm, nn, nk), in_specs=in_specs, out_specs=out_specs, out_shape=out_shape,
        scratch_shapes=_k_scratch(1, (tm, tn), nk), compiler_params=_params(3))(*ins)


def _mm_swiglu(h, wgu, *, name):
    M, K = h.shape
    F = wgu.shape[1] // 2
    tm, tn, tk = _tile(M, 512), _tile(F, MM_TILE_CAP), _tile(K, 1024)
    nm, nf, nk = M // tm, F // tn, K // tk

    def body(h_ref, wg_ref, wu_ref, g_ref, u_ref, a_ref, *acc_refs):
        def finish(totals):
            g, u = totals
            g_ref[...] = g.astype(BF16)
            u_ref[...] = u.astype(BF16)
            a_ref[...] = (g * jax.nn.sigmoid(g) * u).astype(BF16)

        hb = h_ref[...]
        _over_k_steps([jnp.dot(hb, wg_ref[...], preferred_element_type=F32),
                       jnp.dot(hb, wu_ref[...], preferred_element_type=F32)], acc_refs, nk, finish)

    o_spec = pl.BlockSpec((tm, tn), lambda j, i, kk: (i, j))
    shp = jax.ShapeDtypeStruct((M, F), BF16)
    return pl.pallas_call(
        body, name=name, grid=(nf, nm, nk),
        in_specs=[pl.BlockSpec((tm, tk), lambda j, i, kk: (i, kk)),
                  pl.BlockSpec((tk, tn), lambda j, i, kk: (kk, j)),
                  pl.BlockSpec((tk, tn), lambda j, i, kk: (kk, j + nf))],
        out_specs=[o_spec, o_spec, o_spec], out_shape=[shp, shp, shp],
        scratch_shapes=_k_scratch(2, (tm, tn), nk), compiler_params=_params(3))(h, wgu, wgu)


def _mm_dswiglu(dy, wd, gate, up, *, scale, name):
    M, K = dy.shape
    F = wd.shape[0]
    tm, tn, tk = _tile(M, 512), _tile(F, MM_TILE_CAP), _tile(K, 1024)
    nm, nf, nk = M // tm, F // tn, K // tk

    def body(dy_ref, wd_ref, g_ref, u_ref, dg_ref, du_ref, *acc_refs):
        def finish(totals):
            da = totals[0] * scale
            g, u = g_ref[...].astype(F32), u_ref[...].astype(F32)
            sg = jax.nn.sigmoid(g)
            du_ref[...] = (da * g * sg).astype(BF16)
            dg_ref[...] = (da * u * sg * (1.0 + g * (1.0 - sg))).astype(BF16)

        _over_k_steps([_nt(dy_ref[...].astype(BF16), wd_ref[...])], acc_refs, nk, finish)

    o_spec = pl.BlockSpec((tm, tn), lambda j, i, kk: (i, j))
    shp = jax.ShapeDtypeStruct((M, F), BF16)
    return pl.pallas_call(
        body, name=name, grid=(nf, nm, nk),
        in_specs=[pl.BlockSpec((tm, tk), lambda j, i, kk: (i, kk)),
                  pl.BlockSpec((tn, tk), lambda j, i, kk: (j, kk)), o_spec, o_spec],
        out_specs=[o_spec, o_spec], out_shape=[shp, shp],
        scratch_shapes=_k_scratch(1, (tm, tn), nk), compiler_params=_params(3))(dy, wd, gate, up)


HEAD_ROWS = 2048


def _row_tile(rows, cap):
    t = cap
    while t >= 8:
        if rows % t == 0:
            return t
        t //= 2
    return rows


def _rowwise(fn, rows, consts, outs, sums=(), hsums=(), *, heads=None, tm=512, name):
    rows = [r if isinstance(r, tuple) else (r, r.shape[1], None) for r in rows]
    rows = [r if len(r) == 4 else (*r, False) for r in rows]
    S = rows[0][0].shape[0]
    tm = _row_tile(S, tm)
    nh = heads or 1
    n_r, n_c, n_o, n_h, n_s = len(rows), len(consts), len(outs), len(hsums), len(sums)

    def body(*refs):
        r = [x[...].T if row[3] else x[...] for x, row in zip(refs, rows)]
        c = [x[...] for x in refs[n_r:n_r + n_c]]
        o_refs = refs[n_r + n_c:n_r + n_c + n_o]
        h_refs = refs[n_r + n_c + n_o:n_r + n_c + n_o + n_h]
        s_refs = refs[n_r + n_c + n_o + n_h:]
        res = fn(*r, *c)
        res = res if isinstance(res, (tuple, list)) else (res,)
        for ref, val in zip(o_refs, res[:n_o]):
            ref[...] = val.astype(ref.dtype)
        if n_h:
            @pl.when(pl.program_id(1) == 0)
            def _():
                for ref in h_refs:
                    ref[...] = jnp.zeros_like(ref)
            for ref, val in zip(h_refs, res[n_o:n_o + n_h]):
                ref[...] += val
        if n_s:
            @pl.when((pl.program_id(0) == 0) & (pl.program_id(1) == 0))
            def _():
                for ref in s_refs:
                    ref[...] = jnp.zeros_like(ref)
            for ref, val in zip(s_refs, res[n_o + n_h:]):
                ref[...] += val

    def col(colfn):
        return (lambda i, h: (i, 0)) if colfn is None else (lambda i, h: (i, colfn(h)))

    in_specs = [pl.BlockSpec((w, tm), lambda i, h, cf=cf: (cf(h), i)) if flipped else pl.BlockSpec((tm, w), col(cf))
                for _, w, cf, flipped in rows]
    in_specs += [pl.BlockSpec(a.shape, lambda i, h, nd=a.ndim: (0,) * nd) for a in consts]
    out_specs = [pl.BlockSpec((tm, w // nh), (lambda i, h: (i, h)) if heads else (lambda i, h: (i, 0))) for w, _ in outs]
    out_specs += [pl.BlockSpec((tm, w), lambda i, h: (i, 0)) for w in hsums]
    out_specs += [pl.BlockSpec(sh, lambda i, h, nd=len(sh): (0,) * nd) for sh in sums]
    out_shape = [jax.ShapeDtypeStruct((S, w), dt) for w, dt in outs]
    out_shape += [jax.ShapeDtypeStruct((S, w), F32) for w in hsums]
    out_shape += [jax.ShapeDtypeStruct(sh, F32) for sh in sums]
    return pl.pallas_call(body, name=name, grid=(S // tm, nh), in_specs=in_specs, out_specs=out_specs,
                          out_shape=out_shape, compiler_params=_params(2))(*[row[0] for row in rows], *consts)


def _rms(x, width=None):
    width = width or x.shape[-1]
    return lax.rsqrt(jnp.sum(x * x, axis=-1, keepdims=True) * (1.0 / width) + EPS)


def _rmsnorm_fwd(x, g, width=None):
    return x * _rms(x, width) * g


def _rmsnorm_bwd(dy, x, g, width=None):
    width = width or x.shape[-1]
    r = _rms(x, width)
    xn = x * r
    dxn = dy * g
    dx = r * (dxn - xn * (jnp.sum(dxn * xn, axis=-1, keepdims=True) * (1.0 / width)))
    return dx, jnp.sum(dy * xn, axis=0, keepdims=True)


def _norm_rows(x, g, *, name, out_dtype=BF16):
    D = x.shape[1]
    return _rowwise(lambda xv, gv: _rmsnorm_fwd(xv.astype(F32), gv), [x], [g.reshape(1, D)], [(D, out_dtype)],
                    tm=1024, name=name)[0]


def _norm_rows_bwd(dh, x, g, dres, *, name):
    D = x.shape[1]

    def fn(dhv, xv, *rest):
        dx, dg = _rmsnorm_bwd(dhv.astype(F32), xv, rest[-1])
        return (dx + rest[0] if dres is not None else dx), dg

    rows = [dh, x] + ([dres] if dres is not None else [])
    return _rowwise(fn, rows, [g.reshape(1, D)], [(D, F32)], [(1, D)], name=name)


def _softplus(z):
    return jnp.where(z > 20.0, z, jnp.log(1.0 + jnp.exp(z)))


def _running_sum(v, u, split=True):
    if not split:
        return jnp.dot(v.astype(BF16), u, preferred_element_type=F32)
    hi = lax.bitcast_convert_type(lax.bitcast_convert_type(v, jnp.uint32) & jnp.uint32(0xFFFF0000), F32)
    return (jnp.dot(hi.astype(BF16), u, preferred_element_type=F32)
            + jnp.dot((v - hi).astype(BF16), u, preferred_element_type=F32))


def _triangle(tk, inclusive_prefix):
    j, s = lax.broadcasted_iota(jnp.int32, (tk, tk), 0), lax.broadcasted_iota(jnp.int32, (tk, tk), 1)
    return ((j <= s) if inclusive_prefix else (j > s)).astype(BF16)


def _nt(a, b):
    return lax.dot_general(a, b, (((1,), (1,)), ((), ())), preferred_element_type=F32)


def _tn(a, b):
    return lax.dot_general(a, b, (((0,), (0,)), ((), ())), preferred_element_type=F32)


ATT_TQ, ATT_TK = 512, 512
SB_SUB = 256
FWD_GROUP = 2


def _attn_fwd(q, k, v, *, sb, causal, heads, dq, dv, group=1, kcol=None, vcol=None, sum_lane=None, side_gather=None,
              name):
    S, Sk = q.shape[0], k.shape[0]
    tq, tk = min(ATT_TQ, S), min(ATT_TK, Sk)
    sub = min(SB_SUB, tk) if sb else tk
    assert tq % sub == 0 or not causal
    kcol = kcol or (lambda h: h)
    vcol = vcol or (lambda h: h)
    members = range(group)
    assert side_gather is None or heads // group >= GATHER_STAGES

    def body(*refs):
        if side_gather is not None:
            n_in = 4 if sb else 3
            stages = _gather_stages(refs[n_in], refs[n_in + 3], *refs[-2:])
            for n, stage in enumerate(stages):
                pl.when((pl.program_id(0) == n) & (pl.program_id(1) == 0))(stage)
            refs = refs[:n_in] + refs[n_in + 1:n_in + 3] + refs[n_in + 4:-2]
        if sb:
            q_ref, k_ref, v_ref, u_ref, o_ref, lse_ref, acc_ref, r_ref = refs
            r_ref[...] = jnp.zeros_like(r_ref)
        else:
            q_ref, k_ref, v_ref, o_ref, lse_ref, acc_ref, m_ref, l_ref = refs
            m_ref[...] = jnp.full_like(m_ref, -1e30)
            l_ref[...] = jnp.zeros_like(l_ref)
        first_row = pl.program_id(1) * tq
        qb = [q_ref[:, hh * dq:(hh + 1) * dq] for hh in members]
        acc_ref[...] = jnp.zeros_like(acc_ref)
        nblk = (first_row + tq) // sub if causal else Sk // sub
        nfull = (first_row + (0 if sb else 1)) // sub if causal else nblk
        n_cut = tq // sub if causal else 0

        def scores(jj):
            off = pl.multiple_of(jj * sub, sub)
            return tuple(_nt(qb[hh], k_ref[pl.ds(off, sub), hh * dq:(hh + 1) * dq]) for hh in members)

        def weigh(jj, scores_now, masked):
            off = pl.multiple_of(jj * sub, sub)
            if masked:
                kpos = off + lax.broadcasted_iota(jnp.int32, (tq, sub), 1)
                qpos = first_row + lax.broadcasted_iota(jnp.int32, (tq, sub), 0)
                valid = (kpos < qpos) if sb else (kpos <= qpos)
            for hh in members:
                vb = v_ref[pl.ds(off, sub), hh * dv:(hh + 1) * dv]
                s = scores_now[hh]
                if sb:
                    sp = _softplus(s)
                    ls = jnp.where(valid, -sp, 0.0) if masked else -sp
                    w = jnp.exp(s - sp + r_ref[hh] + _running_sum(ls, u_ref[...]))
                    if masked:
                        w = jnp.where(valid, w, 0.0)
                    acc_ref[hh] += jnp.dot(w.astype(BF16), vb, preferred_element_type=F32)
                    r_ref[hh] += jnp.sum(ls, axis=1, keepdims=True)
                else:
                    if masked:
                        s = jnp.where(valid, s, -1e30)
                    m_old = m_ref[hh]
                    m_new = jnp.maximum(m_old, jnp.max(s, axis=1, keepdims=True))
                    p = jnp.exp(s - m_new)
                    alpha = jnp.exp(m_old - m_new)
                    if sum_lane is None:
                        l_ref[hh] = alpha * l_ref[hh] + jnp.sum(p, axis=1, keepdims=True)
                    acc_ref[hh] = alpha * acc_ref[hh] + jnp.dot(p.astype(BF16), vb, preferred_element_type=F32)
                    m_ref[hh] = m_new

        if sb:
            s_cur = scores(nblk - 1)
            for cut in range(n_cut):
                s_next = scores(jnp.maximum(nblk - 2 - cut, 0))
                weigh(nblk - 1 - cut, s_cur, True)
                s_cur = s_next

            def step(t, s_now):
                s_next = scores(jnp.maximum(nfull - 2 - t, 0))
                weigh(nfull - 1 - t, s_now, False)
                return s_next

            lax.fori_loop(0, nfull, step, s_cur)
        else:
            n_loop = nfull if causal else nblk - 1

            def step(t, s_now):
                s_next = scores(jnp.minimum(t + 1, nblk - 1))
                weigh(t, s_now, False)
                return s_next

            s_cur = lax.fori_loop(0, n_loop, step, scores(0))
            tail = n_cut if causal else 1
            for last in range(tail):
                s_next = scores(n_loop + last + 1) if last + 1 < tail else None
                weigh(n_loop + last, s_cur, causal)
                s_cur = s_next
        for hh in members:
            cols = slice(hh * dv, (hh + 1) * dv)
            if sb:
                o_ref[:, cols] = acc_ref[hh]
                lse_ref[hh] = r_ref[hh]
            else:
                acc = acc_ref[hh]
                l = l_ref[hh] if sum_lane is None else acc[:, sum_lane:sum_lane + 1]
                o_ref[:, cols] = acc / l
                lse_ref[hh] = m_ref[hh] + jnp.log(l)

    in_specs = [pl.BlockSpec((tq, group * dq), lambda g, i: (i, g)),
                pl.BlockSpec((Sk, group * dq), lambda g, i: (0, kcol(g))),
                pl.BlockSpec((Sk, group * dv), lambda g, i: (0, vcol(g)))]
    ins = [q, k, v]
    scratch = [pltpu.VMEM((group, tq, dv), F32), pltpu.VMEM((group, tq, 1), F32)]
    if sb:
        ins.append(_triangle(sub, inclusive_prefix=False))
        in_specs.append(pl.BlockSpec((sub, sub), lambda g, i: (0, 0)))
    else:
        scratch.append(pltpu.VMEM((group, tq, 1), F32))
    out_specs = [pl.BlockSpec((tq, group * dv), lambda g, i: (i, g)), pl.BlockSpec((group, tq, 1), lambda g, i: (g, i, 0))]
    out_shape = [jax.ShapeDtypeStruct((S, heads * dv), F32), jax.ShapeDtypeStruct((heads, S, 1), F32)]
    if side_gather is not None:
        ins.append(side_gather)
        in_specs.append(ANY)
        out_specs.append(ANY)
        out_shape.append(jax.ShapeDtypeStruct((N_CHIPS,) + side_gather.shape, side_gather.dtype))
        scratch += [pltpu.SemaphoreType.DMA((GATHER_COPIES,)), pltpu.SemaphoreType.DMA((GATHER_COPIES,))]
    outs = pl.pallas_call(body, name=name, grid=(heads // group, S // tq), in_specs=in_specs, out_specs=out_specs,
                          out_shape=out_shape, scratch_shapes=scratch, compiler_params=_params(2))(*ins)
    return outs if side_gather is None else (outs[0], outs[1], _place_own_slot(outs[2], side_gather))


def _attn_bwd(q, k, v, o, do, lse, *, sb, causal, heads, dq, dv, kcol=None, vcol=None, side_exchange=None, name):
    S, Sk = q.shape[0], k.shape[0]
    tq, tk = min(ATT_TQ, S), min(ATT_TK, Sk)
    sub = min(SB_SUB, tk) if sb else tk
    assert tq % sub == 0 or not causal
    nq = S // tq
    kcol = kcol or (lambda h: h)
    vcol = vcol or (lambda h: h)
    n_in = 7 if sb else 6

    def body(*refs):
        if side_exchange is not None:
            start, finish = _partials_exchange(refs[n_in], refs[n_in + 4], *refs[-2:])
            pl.when((pl.program_id(0) == 0) & (pl.program_id(1) == 0))(start)
            pl.when((pl.program_id(0) == heads - 1) & (pl.program_id(1) == 0))(finish)
            refs = refs[:n_in] + refs[n_in + 1:n_in + 4] + refs[n_in + 5:-2]
        if sb:
            q_ref, k_ref, v_ref, o_ref, do_ref, lse_ref, u_ref, dq_ref, dk_ref, dv_ref, acc_ref, r_ref, re_ref = refs
            r_ref[...] = jnp.zeros_like(r_ref)
            re_ref[...] = jnp.zeros_like(re_ref)
        else:
            q_ref, k_ref, v_ref, o_ref, do_ref, lse_ref, dq_ref, dk_ref, dv_ref, acc_ref = refs
        first_row = pl.program_id(1) * tq

        @pl.when(first_row == 0)
        def _():
            dk_ref[...] = jnp.zeros_like(dk_ref)
            dv_ref[...] = jnp.zeros_like(dv_ref)

        qb = q_ref[...]
        dof = do_ref[...].astype(F32)
        dob = dof.astype(BF16)
        q_t, do_t = qb.T, dob.T
        if not sb:
            dlt = jnp.sum(dof * o_ref[...], axis=1, keepdims=True)
        acc_ref[...] = jnp.zeros_like(acc_ref)
        nblk = (first_row + tq) // sub if causal else Sk // sub
        nfull = (first_row + (0 if sb else 1)) // sub if causal else nblk
        n_cut = tq // sub if causal else 0

        def products(jj):
            off = pl.multiple_of(jj * sub, sub)
            return _nt(qb, k_ref[pl.ds(off, sub), :]), _nt(dob, v_ref[pl.ds(off, sub), :])

        def piece(jj, now, masked):
            off = pl.multiple_of(jj * sub, sub)
            kb = k_ref[pl.ds(off, sub), :]
            s, dp = now
            if masked:
                qpos = first_row + lax.broadcasted_iota(jnp.int32, (tq, sub), 0)
                kpos = off + lax.broadcasted_iota(jnp.int32, (tq, sub), 1)
                valid = (kpos < qpos) if sb else (kpos <= qpos)
            if sb:
                u = u_ref[...]
                sp = _softplus(s)
                ls = jnp.where(valid, -sp, 0.0) if masked else -sp
                lb = s - sp
                w = jnp.exp(lb + (lse_ref[0] - (r_ref[...] + _running_sum(ls, u))))
                if masked:
                    w = jnp.where(valid, w, 0.0)
                e = dp * w
                ds = e - jnp.exp(lb) * (re_ref[...] + _running_sum(e, u, split=False))
                if masked:
                    ds = jnp.where(valid, ds, 0.0)
                r_ref[...] += jnp.sum(ls, axis=1, keepdims=True)
                re_ref[...] += jnp.sum(e, axis=1, keepdims=True)
            else:
                w = jnp.exp(s - lse_ref[0])
                if masked:
                    w = jnp.where(valid, w, 0.0)
                ds = w * (dp - dlt)
            dsb = ds.astype(BF16)
            dv_ref[:, pl.ds(off, sub)] += jnp.dot(do_t, w.astype(BF16), preferred_element_type=F32)
            dk_ref[:, pl.ds(off, sub)] += jnp.dot(q_t, dsb, preferred_element_type=F32)
            acc_ref[...] += jnp.dot(dsb, kb, preferred_element_type=F32)

        n_loop = nfull if causal else nblk - 1
        per_trip = tk // sub

        def steps(first, count, masked):
            ready = [products(first + c) for c in range(count)]
            for c in range(count):
                piece(first + c, ready[c], masked)

        def trip(t, carry):
            steps(t * per_trip, per_trip, False)
            return carry

        lax.fori_loop(0, n_loop // per_trip, trip, 0)
        steps(n_loop, n_cut if causal else 1, causal)
        dq_ref[...] = acc_ref[...]

    ins = [q, k, v, o, do]
    in_specs = [pl.BlockSpec((tq, dq), lambda h, i: (i, h)),
                pl.BlockSpec((Sk, dq), lambda h, i: (0, kcol(h))),
                pl.BlockSpec((Sk, dv), lambda h, i: (0, vcol(h))),
                pl.BlockSpec((tq, dv), lambda h, i: (i, h)),
                pl.BlockSpec((tq, dv), lambda h, i: (i, h))]
    scratch = [pltpu.VMEM((tq, dq), F32)]
    ins.append(lse)
    in_specs.append(pl.BlockSpec((1, tq, 1), lambda h, i: (h, i, 0)))
    if sb:
        ins.append(_triangle(sub, inclusive_prefix=True))
        in_specs.append(pl.BlockSpec((sub, sub), lambda h, i: (0, 0)))
        scratch += [pltpu.VMEM((tq, 1), F32), pltpu.VMEM((tq, 1), F32)]
    out_specs = [pl.BlockSpec((tq, dq), lambda h, i: (i, h)),
                 pl.BlockSpec((dq, Sk), lambda h, i: (h, 0)),
                 pl.BlockSpec((dv, Sk), lambda h, i: (h, 0))]
    out_shape = [jax.ShapeDtypeStruct((S, heads * dq), F32), jax.ShapeDtypeStruct((heads * dq, Sk), F32),
                 jax.ShapeDtypeStruct((heads * dv, Sk), F32)]
    if side_exchange is not None:
        _, R, C = side_exchange.shape
        ins.append(side_exchange)
        in_specs.append(ANY)
        out_specs.append(ANY)
        out_shape.append(jax.ShapeDtypeStruct((N_DEVICES, R // 2, C), side_exchange.dtype))
        scratch += [pltpu.SemaphoreType.DMA((N_DEVICES,)), pltpu.SemaphoreType.DMA((N_DEVICES,))]
    return pl.pallas_call(body, name=name, grid=(heads, nq), in_specs=in_specs, out_specs=out_specs,
                          out_shape=out_shape, scratch_shapes=scratch, compiler_params=_params(2))(*ins)


GELU_C = 0.7978845608028654
assert 2 * SG_GD == LANE and SG_CHUNK == LANE


def _gelu(z):
    t = jnp.tanh(GELU_C * (z + 0.044715 * z * z * z))
    return 0.5 * z * (1.0 + t), t


def _gelu_grad(z, t):
    return 0.5 * (1.0 + t) + 0.5 * z * (1.0 - t * t) * GELU_C * (1.0 + 3.0 * 0.044715 * z * z)


def _layernorm_parts(g):
    d = g - jnp.mean(g, axis=-1, keepdims=True)
    rstd = lax.rsqrt(jnp.mean(d * d, axis=-1, keepdims=True) + EPS)
    return d * rstd, rstd


def _gelu_ln(z, gain, bias, *, name):
    def fn(zv, gn, bs):
        a, _ = _gelu(zv)
        y, _ = _layernorm_parts(a[:, SG_W:])
        return a[:, :SG_W], y * gn + bs

    return _rowwise(fn, [z], [gain.reshape(1, SG_W), bias.reshape(1, SG_W)], [(SG_W, F32), (SG_W, BF16)], name=name)


def _gelu_ln_bwd(z, du, dgl, gain, *, name):
    def fn(zv, duv, dglv, gn):
        a, t = _gelu(zv)
        y, rstd = _layernorm_parts(a[:, SG_W:])
        dy = dglv * gn
        dgg = rstd * (dy - jnp.mean(dy, axis=-1, keepdims=True) - y * jnp.mean(dy * y, axis=-1, keepdims=True))
        dz = jnp.concatenate([duv, dgg], axis=1) * _gelu_grad(zv, t)
        return dz, jnp.sum(dglv * y, axis=0, keepdims=True), jnp.sum(dglv, axis=0, keepdims=True)

    return _rowwise(fn, [z, du, dgl], [gain.reshape(1, SG_W)], [(2 * SG_W, BF16)], [(1, SG_W), (1, SG_W)], name=name)


def _sg_masks():
    tri = lax.broadcasted_iota(jnp.int32, (SG_CHUNK, SG_CHUNK), 0) >= lax.broadcasted_iota(jnp.int32, (SG_CHUNK, SG_CHUNK), 1)
    first = lax.broadcasted_iota(jnp.int32, (SG_CHUNK, LANE), 1) < SG_GD
    return tri, first


def _spatial(gl, u, w, bt, *, name):
    S = gl.shape[0]
    tm = _row_tile(S, 512)
    nch = tm // SG_CHUNK

    def body(gl_ref, u_ref, w_ref, bt_ref, o_ref):
        tri, first = _sg_masks()
        for p in range(SG_W // LANE):
            cols = slice(p * LANE, (p + 1) * LANE)
            wa = jnp.where(tri, w_ref[2 * p], 0.0).astype(BF16)
            wb = jnp.where(tri, w_ref[2 * p + 1], 0.0).astype(BF16)
            for ci in range(nch):
                rws = slice(ci * SG_CHUNK, (ci + 1) * SG_CHUNK)
                g = gl_ref[rws, cols]
                zero = jnp.zeros_like(g)
                mixed = (jnp.dot(wa, jnp.where(first, g, zero), preferred_element_type=F32)
                         + jnp.dot(wb, jnp.where(first, zero, g), preferred_element_type=F32) + bt_ref[:, cols])
                o_ref[rws, cols] = u_ref[rws, cols] * mixed

    row = pl.BlockSpec((tm, SG_W), lambda i: (i, 0))
    return pl.pallas_call(
        body, name=name, grid=(S // tm,),
        in_specs=[row, row, pl.BlockSpec(w.shape, lambda i: (0, 0, 0)), pl.BlockSpec(bt.shape, lambda i: (0, 0))],
        out_specs=row, out_shape=jax.ShapeDtypeStruct((S, SG_W), F32), compiler_params=_params(1))(gl, u, w, bt)


def _spatial_bwd(d_o, gl, u, w, bt, *, name):
    S = gl.shape[0]
    tm = _row_tile(S, 512)
    nch = tm // SG_CHUNK
    nsteps = S // tm

    def body(do_ref, gl_ref, u_ref, w_ref, bt_ref, du_ref, dgl_ref, dw_ref, db_ref, dbt_ref):
        tri, first = _sg_masks()
        step = pl.program_id(0)

        @pl.when(step == 0)
        def _():
            dw_ref[...] = jnp.zeros_like(dw_ref)
            dbt_ref[...] = jnp.zeros_like(dbt_ref)

        for p in range(SG_W // LANE):
            cols = slice(p * LANE, (p + 1) * LANE)
            wa = jnp.where(tri, w_ref[2 * p], 0.0).astype(BF16)
            wb = jnp.where(tri, w_ref[2 * p + 1], 0.0).astype(BF16)
            for ci in range(nch):
                rws = slice(ci * SG_CHUNK, (ci + 1) * SG_CHUNK)
                g = gl_ref[rws, cols]
                zero = jnp.zeros_like(g)
                mixed = (jnp.dot(wa, jnp.where(first, g, zero), preferred_element_type=F32)
                         + jnp.dot(wb, jnp.where(first, zero, g), preferred_element_type=F32) + bt_ref[:, cols])
                dov = do_ref[rws, cols]
                du_ref[rws, cols] = dov * mixed
                dm = dov * u_ref[rws, cols]
                dbt_ref[:, cols] += dm
                dma = jnp.where(first, dm, 0.0).astype(BF16)
                dmb = jnp.where(first, 0.0, dm).astype(BF16)
                dw_ref[2 * p] += jnp.where(tri, _nt(dma, g), 0.0)
                dw_ref[2 * p + 1] += jnp.where(tri, _nt(dmb, g), 0.0)
                dgl_ref[rws, cols] = _tn(wa, dma) + _tn(wb, dmb)

        @pl.when(step == nsteps - 1)
        def _():
            lane = lax.broadcasted_iota(jnp.int32, (SG_CHUNK, LANE), 1)
            acc = jnp.zeros((SG_CHUNK, LANE), F32)
            for p in range(SG_W // LANE):
                blk = dbt_ref[:, p * LANE:(p + 1) * LANE]
                sa = jnp.sum(jnp.where(first, blk, 0.0), axis=1, keepdims=True)
                sb_ = jnp.sum(jnp.where(first, 0.0, blk), axis=1, keepdims=True)
                acc = acc + jnp.where(lane == 2 * p, sa, 0.0) + jnp.where(lane == 2 * p + 1, sb_, 0.0)
            db_ref[...] = acc

    row = pl.BlockSpec((tm, SG_W), lambda i: (i, 0))
    return pl.pallas_call(
        body, name=name, grid=(nsteps,),
        in_specs=[row, row, row, pl.BlockSpec(w.shape, lambda i: (0, 0, 0)), pl.BlockSpec(bt.shape, lambda i: (0, 0))],
        out_specs=[row, row, pl.BlockSpec(w.shape, lambda i: (0, 0, 0)), pl.BlockSpec((SG_CHUNK, LANE), lambda i: (0, 0))],
        out_shape=[jax.ShapeDtypeStruct((S, SG_W), F32), jax.ShapeDtypeStruct((S, SG_W), F32),
                   jax.ShapeDtypeStruct(w.shape, F32), jax.ShapeDtypeStruct((SG_CHUNK, LANE), F32)],
        scratch_shapes=[pltpu.VMEM((SG_CHUNK, SG_W), F32)], compiler_params=_params(1))(d_o, gl, u, w, bt)


ROPE_HALF = MLA_ROPE // 2
KR_COL = (MLA_QL + MLA_KVL) // LANE
MLA_IN_PAD = MLA_QL + MLA_KVL + LANE


def _rope_tables(positions):
    inv_freq = ROPE_THETA ** (-jnp.arange(ROPE_HALF, dtype=F32) / ROPE_HALF)
    ang = positions.astype(F32)[:, None] * inv_freq
    cos, sin = jnp.cos(ang), jnp.sin(ang)
    S = positions.shape[0]
    z16, tail = jnp.zeros((S, ROPE_HALF), F32), jnp.zeros((S, LANE - MLA_QK), F32)
    ones = jnp.ones((S, MLA_NOPE), F32)
    zeros = jnp.zeros((S, MLA_NOPE), F32)
    return (jnp.concatenate([ones, cos, cos, tail], axis=1), jnp.concatenate([zeros, z16, sin, tail], axis=1),
            jnp.concatenate([zeros, -sin, z16, tail], axis=1))


def _rope(x, cos, sa, sb):
    return x * cos + pltpu.roll(x, ROPE_HALF, 1) * sa + pltpu.roll(x, LANE - ROPE_HALF, 1) * sb


def _rope_t(dy, cos, sa, sb):
    return dy * cos + pltpu.roll(dy * sa, LANE - ROPE_HALF, 1) + pltpu.roll(dy * sb, ROPE_HALF, 1)


def _mla_lora(P, qlg, kvlg, *, name):
    def fn(pv, a, b):
        return _rmsnorm_fwd(pv[:, :MLA_QL], a), _rmsnorm_fwd(pv[:, MLA_QL:MLA_QL + MLA_KVL], b)

    return _rowwise(fn, [P], [qlg.reshape(1, MLA_QL), kvlg.reshape(1, MLA_KVL)], [(MLA_QL, BF16), (MLA_KVL, BF16)], name=name)


def _mla_lora_bwd(dcq, dckv, dkr, P, qlg, kvlg, *, name):
    def fn(d1, d2, d3, pv, a, b):
        x1, g1 = _rmsnorm_bwd(d1, pv[:, :MLA_QL], a)
        x2, g2 = _rmsnorm_bwd(d2, pv[:, MLA_QL:MLA_QL + MLA_KVL], b)
        return jnp.concatenate([x1, x2, d3], axis=1), g1, g2

    return _rowwise(fn, [dcq, dckv, dkr, P], [qlg.reshape(1, MLA_QL), kvlg.reshape(1, MLA_KVL)], [(MLA_IN_PAD, BF16)],
                    [(1, MLA_QL), (1, MLA_KVL)], name=name)


def _mla_qk(q_pre, k_pre, P, tabs, qg, kg, *, name):
    def fn(qp, kp, kr, c, a, b, qgv, kgv):
        return (_rope(_rmsnorm_fwd(qp, qgv, MLA_QK), c, a, b) * MLA_SCALE,
                _rope(_rmsnorm_fwd(kp + kr, kgv, MLA_QK), c, a, b))

    hcol = lambda h: h
    rows = [(q_pre, LANE, hcol), (k_pre, LANE, hcol), (P, LANE, lambda h: KR_COL), *tabs]
    w = MLA_HEADS * LANE
    return _rowwise(fn, rows, [qg, kg], [(w, BF16), (w, BF16)], heads=MLA_HEADS, tm=HEAD_ROWS, name=name)


def _mla_qk_bwd(dq, dk_t, q_pre, k_pre, P, tabs, qg, kg, *, name):
    def fn(dqv, dkv, qp, kp, kr, c, a, b, qgv, kgv):
        dqp, dqg = _rmsnorm_bwd(_rope_t(dqv * MLA_SCALE, c, a, b), qp, qgv, MLA_QK)
        dkp, dkg = _rmsnorm_bwd(_rope_t(dkv, c, a, b), kp + kr, kgv, MLA_QK)
        lane = lax.broadcasted_iota(jnp.int32, (1, LANE), 1)
        return dqp, dkp, jnp.where((lane >= MLA_NOPE) & (lane < MLA_QK), dkp, 0.0), dqg, dkg

    hcol = lambda h: h
    rows = [(dq, LANE, hcol), (dk_t, LANE, hcol, True), (q_pre, LANE, hcol), (k_pre, LANE, hcol),
            (P, LANE, lambda h: KR_COL), *tabs]
    w = MLA_HEADS * LANE
    return _rowwise(fn, rows, [qg, kg], [(w, BF16), (w, BF16)], [(1, LANE), (1, LANE)], [LANE], heads=MLA_HEADS,
                    tm=HEAD_ROWS, name=name)


def _head_norm(x, g, *, heads, width, colfn=None, scale=1.0, name):
    return _rowwise(lambda xv, gv: _rmsnorm_fwd(xv, gv) * scale, [(x, width, colfn or (lambda h: h))],
                    [g.reshape(1, width)], [(heads * width, BF16)], heads=heads, tm=HEAD_ROWS, name=name)[0]


def _head_norm_bwd(dy, x, g, *, heads, width, colfn=None, scale=1.0, out_dtype, name):
    return _rowwise(lambda dv_, xv, gv: _rmsnorm_bwd(dv_ * scale, xv, gv),
                    [(dy, width, lambda h: h), (x, width, colfn or (lambda h: h))],
                    [g.reshape(1, width)], [(heads * width, out_dtype)], [(1, width)], heads=heads, tm=HEAD_ROWS,
                    name=name)


def _loss_grad(y, tgt, *, name):
    D = y.shape[1]

    def fn(yv, tv):
        d = yv - tv
        return d * (1.0 / D), jnp.sum(d * d, axis=0, keepdims=True) * (0.5 / D)

    dy, part = _rowwise(fn, [y, tgt], [], [(D, F32)], [(1, D)], tm=1024, name=name)
    return jnp.sum(part), dy


def _adamw(w, g, m, v, *, name):
    shape = w.shape
    two_d = (-1, shape[-1])

    def fn(wv, gv, mv, vv):
        m2 = ADAM_B1 * mv + (1.0 - ADAM_B1) * gv
        v2 = ADAM_B2 * vv + (1.0 - ADAM_B2) * (gv * gv)
        m_hat = m2 / (1.0 - ADAM_B1 ** ADAM_STEP)
        v_hat = v2 / (1.0 - ADAM_B2 ** ADAM_STEP)
        return -ADAM_LR * (m_hat / (jnp.sqrt(v_hat) + ADAM_EPS) + ADAM_WD * wv), m2, v2

    outs = _rowwise(fn, [t.reshape(two_d) for t in (w, g, m, v)], [], [(shape[-1], F32)] * 3, tm=256, name=name)
    return [o.reshape(shape) for o in outs]


def _pad_cols(w, heads, hd):
    k = w.shape[0]
    return jnp.pad(w.reshape(k, heads, hd), ((0, 0), (0, 0), (0, LANE - hd))).reshape(k, heads * LANE)


def _unpad_cols(w, heads, hd):
    k = w.shape[0]
    return w.reshape(k, heads, LANE)[:, :, :hd].reshape(k, heads * hd)


def _pad_rows(w, heads, hd):
    n = w.shape[1]
    return jnp.pad(w.reshape(heads, hd, n), ((0, 0), (0, LANE - hd), (0, 0))).reshape(heads * LANE, n)


def _unpad_rows(w, heads, hd):
    n = w.shape[1]
    return w.reshape(heads, LANE, n)[:, :hd, :].reshape(heads * hd, n)


def _ffn_fwd(x, g, wgu, wd, tag):
    h = _norm_rows(x, g, name=tag + "_norm")
    gate, up, act = _mm_swiglu(h, wgu, name=tag + "_gu")
    y = _mm(act, wd, scale=0.5, residual=x, name=tag + "_down")
    return y, (x, h, gate, up, act)


def _ffn_bwd(dy, saved, g, wgu, wd, tag):
    x, h, gate, up, act = saved
    F = wd.shape[0]
    dwd = _mm(act, dy, ta=True, scale=0.5, name=tag + "_dwd")
    dgate, dup = _mm_dswiglu(dy, wd, gate, up, scale=0.5, name=tag + "_dact")
    dh = _mm(dgate, wgu, tb=True, name=tag + "_dh_g")
    dx, dg = _mm(dup, wgu, tb=True, b_off=(0, F), residual=dh, norm_bwd=(x, g, dy), name=tag + "_dh_u")
    dwgu = jnp.concatenate([_mm(h, dgate, ta=True, name=tag + "_dwg"), _mm(h, dup, ta=True, name=tag + "_dwu")], axis=1)
    return dx, dg, dwgu, dwd


def _even_weights(w_in, w_out):
    parts = [w_in[:, :SB_W] * SB_SCALE, w_in[:, SB_W:2 * SB_W], w_in[:, 2 * SB_W:3 * SB_W]]
    wqkv = jnp.concatenate([_pad_cols(p, SB_HEADS, SB_HD) for p in parts], axis=1)
    return wqkv, w_in[:, 3 * SB_W:], _pad_rows(w_out[:SB_W], SB_HEADS, SB_HD), w_out[SB_W:]


def _even_fwd(x, g, wts, ln_g, ln_b, sgu_w, bt, late_shard, tag):
    wqkv, wz, wo_sb, wo_sg = wts
    h = _norm_rows(x, g, name=tag + "_norm")
    qkv = _mm(h, wqkv, out_dtype=BF16, name=tag + "_qkv")
    z = _mm(h, wz, name=tag + "_z")
    o_sb, tot, late = _attn_fwd(qkv, qkv, qkv, sb=True, causal=True, heads=SB_HEADS, dq=LANE, dv=LANE, group=FWD_GROUP,
                                kcol=lambda g: SB_HEADS // FWD_GROUP + g, vcol=lambda g: 2 * SB_HEADS // FWD_GROUP + g,
                                side_gather=late_shard, name=tag + "_sb")
    u, gl = _gelu_ln(z, ln_g, ln_b, name=tag + "_geluln")
    o_sg = _spatial(gl, u, sgu_w, bt, name=tag + "_sgu")
    y = _mm(o_sb, wo_sb, residual=x, name=tag + "_out_sb")
    y = _mm(o_sg, wo_sg, residual=y, name=tag + "_out_sg")
    return y, (x, h, qkv, z, o_sb, tot, u, gl, o_sg), late


def _even_bwd(dy, saved, g, wts, ln_g, sgu_w, bt, partials, tag):
    wqkv, wz, wo_sb, wo_sg = wts
    x, h, qkv, z, o_sb, tot, u, gl, o_sg = saved
    do_sb = _mm(dy, wo_sb, tb=True, name=tag + "_do_sb")
    do_sg = _mm(dy, wo_sg, tb=True, name=tag + "_do_sg")
    dwo = jnp.concatenate([_unpad_rows(_mm(o_sb, dy, ta=True, name=tag + "_dwo_sb"), SB_HEADS, SB_HD),
                           _mm(o_sg, dy, ta=True, name=tag + "_dwo_sg")], axis=0)
    dq, dk_t, dv_t, received = _attn_bwd(qkv, qkv, qkv, o_sb, do_sb, tot, sb=True, causal=True, heads=SB_HEADS, dq=LANE,
                                         dv=LANE, kcol=lambda hh: SB_HEADS + hh, vcol=lambda hh: 2 * SB_HEADS + hh,
                                         side_exchange=partials, name=tag + "_sb_bwd")
    du, dgl, dsgu_w, db_t = _spatial_bwd(do_sg, gl, u, sgu_w, bt, name=tag + "_sgu_bwd")
    dz, dln_g, dln_b = _gelu_ln_bwd(z, du, dgl, ln_g, name=tag + "_geluln_bwd")
    dh = _mm(dz, wz, tb=True, name=tag + "_dh_z")
    dh = _mm(dq, wqkv, tb=True, residual=dh, name=tag + "_dh_q")
    dws = [_unpad_cols(_mm(h, dq, ta=True, scale=SB_SCALE, name=tag + "_dw_q"), SB_HEADS, SB_HD)]
    for i, (d_t, nm) in enumerate(((dk_t, "k"), (dv_t, "v")), start=1):
        dh = _mm(d_t, wqkv, ta=True, tb=True, b_off=(0, i * SB_HEADS * LANE), residual=dh,
                 norm_bwd=(x, g, dy) if nm == "v" else None, name=tag + "_dh_" + nm)
        dws.append(_unpad_rows(_mm(d_t, h, name=tag + "_dw_" + nm), SB_HEADS, SB_HD).T)
    dws.append(_mm(h, dz, ta=True, name=tag + "_dw_z"))
    dx, dg = dh
    return dx, dict(mix_norm=dg, sbg_w_in=jnp.concatenate(dws, axis=1), sgu_ln_gain=dln_g, sgu_ln_bias=dln_b,
                    sgu_w=dsgu_w, sgu_b=db_t[:, :SG_GROUPS].T, sbg_w_out=dwo), received


def _mla_weights(w_in, w_uq, w_ukv, w_out, q_gain, k_gain):
    d = w_in.shape[0]
    lat = MLA_QL + MLA_KVL
    w_in_ext = jnp.concatenate([w_in[:, :lat], jnp.zeros((d, MLA_NOPE), w_in.dtype), w_in[:, lat:],
                                jnp.zeros((d, LANE - MLA_QK), w_in.dtype)], axis=1)
    kv = w_ukv.reshape(MLA_KVL, MLA_HEADS, MLA_NOPE + MLA_V)
    wk = _pad_cols(kv[:, :, :MLA_NOPE].reshape(MLA_KVL, -1), MLA_HEADS, MLA_NOPE)
    wv = _pad_cols(kv[:, :, MLA_NOPE:].reshape(MLA_KVL, -1), MLA_HEADS, MLA_V)
    pad_gain = lambda gn: jnp.pad(gn.reshape(1, MLA_QK), ((0, 0), (0, LANE - MLA_QK)))
    return (w_in_ext, _pad_cols(w_uq, MLA_HEADS, MLA_QK), wk, wv, _pad_rows(w_out, MLA_HEADS, MLA_V),
            pad_gain(q_gain), pad_gain(k_gain))


def _mla_fwd(x, g, wts, qlg, kvlg, tabs, tag):
    w_in, w_uq, wk, wv, w_out, qg, kg = wts
    h = _norm_rows(x, g, name=tag + "_norm")
    P = _mm(h, w_in, name=tag + "_in")
    cqn, ckvn = _mla_lora(P, qlg, kvlg, name=tag + "_lora")
    q_pre = _mm(cqn, w_uq, name=tag + "_uq")
    k_pre = _mm(ckvn, wk, name=tag + "_uk")
    ones_lane = jnp.tile((jnp.arange(LANE) == MLA_V).astype(F32), MLA_HEADS)[None, :]
    v = _mm(ckvn, wv, out_dtype=BF16, bias=ones_lane, name=tag + "_uv")
    q, k = _mla_qk(q_pre, k_pre, P, tabs, qg, kg, name=tag + "_qk")
    o, lse = _attn_fwd(q, k, v, sb=False, causal=True, heads=MLA_HEADS, dq=LANE, dv=LANE, group=FWD_GROUP,
                       sum_lane=MLA_V, name=tag + "_attn")
    y = _mm(o, w_out, residual=x, name=tag + "_out")
    return y, (x, h, P, cqn, ckvn, q_pre, k_pre, q, k, v, o, lse)


def _mla_bwd(dy, saved, g, wts, qlg, kvlg, tabs, tag):
    w_in, w_uq, wk, wv, w_out, qg, kg = wts
    x, h, P, cqn, ckvn, q_pre, k_pre, q, k, v, o, lse = saved
    do = _mm(dy, w_out, tb=True, name=tag + "_do")
    dw_out = _unpad_rows(_mm(o, dy, ta=True, name=tag + "_dwo"), MLA_HEADS, MLA_V)
    dq, dk_t, dv_t = _attn_bwd(q, k, v, o, do, lse, sb=False, causal=True, heads=MLA_HEADS, dq=LANE, dv=LANE,
                               name=tag + "_attn_bwd")
    dq_pre, dk_pre, dkr, dqg, dkg = _mla_qk_bwd(dq, dk_t, q_pre, k_pre, P, tabs, qg, kg, name=tag + "_qk_bwd")
    dcqn = _mm(dq_pre, w_uq, tb=True, name=tag + "_dcq")
    dckvn = _mm(dk_pre, wk, tb=True, name=tag + "_dckv_k")
    dckvn = _mm(dv_t, wv, ta=True, tb=True, residual=dckvn, name=tag + "_dckv_v")
    dw_uq = _unpad_cols(_mm(cqn, dq_pre, ta=True, name=tag + "_dwuq"), MLA_HEADS, MLA_QK)
    dwk = _unpad_cols(_mm(ckvn, dk_pre, ta=True, name=tag + "_dwk"), MLA_HEADS, MLA_NOPE)
    dwv = _unpad_rows(_mm(dv_t, ckvn, name=tag + "_dwv"), MLA_HEADS, MLA_V).T
    dw_ukv = jnp.concatenate([dwk.reshape(MLA_KVL, MLA_HEADS, MLA_NOPE), dwv.reshape(MLA_KVL, MLA_HEADS, MLA_V)],
                             axis=2).reshape(MLA_KVL, -1)
    dP, dqlg, dkvlg = _mla_lora_bwd(dcqn, dckvn, dkr, P, qlg, kvlg, name=tag + "_lora_bwd")
    dx, dg = _mm(dP, w_in, tb=True, norm_bwd=(x, g, dy), name=tag + "_dh")
    dw_in_ext = _mm(h, dP, ta=True, name=tag + "_dwin")
    lat = MLA_QL + MLA_KVL
    dw_in = jnp.concatenate([dw_in_ext[:, :lat], dw_in_ext[:, lat + MLA_NOPE:lat + MLA_QK]], axis=1)
    return dx, dict(mix_norm=dg, mla_w_in=dw_in, mla_q_lora_gain=dqlg, mla_kv_lora_gain=dkvlg, mla_w_uq=dw_uq,
                    mla_w_ukv=dw_ukv, mla_q_gain=dqg[:, :MLA_QK], mla_k_gain=dkg[:, :MLA_QK], mla_w_out=dw_out)


def _xmem_fwd(x, mem, g, gm, wq, wkv, qg, kg, wo, tag):
    hq = _norm_rows(x, g, name=tag + "_norm")
    hm = _norm_rows(mem, gm, name=tag + "_mnorm")
    qp = _mm(hq, wq, name=tag + "_q")
    kv = _mm(hm, wkv, name=tag + "_kv")
    q = _head_norm(qp, qg, heads=MEM_HEADS, width=MEM_HD, scale=MEM_SCALE, name=tag + "_qn")
    kn = _head_norm(kv, kg, heads=MEM_HEADS, width=MEM_HD, colfn=lambda hh: 2 * hh, name=tag + "_kn")
    kvb = kv.reshape(-1, MEM_HEADS, 2, MEM_HD)[:, :, 1].reshape(-1, MEM_HEADS * MEM_HD).astype(BF16)
    o, lse = _attn_fwd(q, kn, kvb, sb=False, causal=False, heads=MEM_HEADS, dq=MEM_HD, dv=MEM_HD, group=MEM_HEADS,
                       name=tag + "_attn")
    y = _mm(o, wo, residual=x, name=tag + "_out")
    return y, (x, hq, hm, qp, kv, q, kn, kvb, o, lse)


def _xmem_bwd(dy, saved, mem, g, gm, wq, wkv, qg, kg, wo, tag):
    x, hq, hm, qp, kv, q, kn, kvb, o, lse = saved
    m = mem.shape[0]
    do = _mm(dy, wo, tb=True, name=tag + "_do")
    dwo = _mm(o, dy, ta=True, name=tag + "_dwo")
    dq, dk_t, dv_t = _attn_bwd(q, kn, kvb, o, do, lse, sb=False, causal=False, heads=MEM_HEADS, dq=MEM_HD, dv=MEM_HD,
                               name=tag + "_attn_bwd")
    dk, dv = dk_t.T, dv_t.T
    dqp, dqg = _head_norm_bwd(dq, qp, qg, heads=MEM_HEADS, width=MEM_HD, scale=MEM_SCALE, out_dtype=BF16,
                              name=tag + "_qn_bwd")
    dkp, dkg = _head_norm_bwd(dk, kv, kg, heads=MEM_HEADS, width=MEM_HD, colfn=lambda hh: 2 * hh, out_dtype=F32,
                              name=tag + "_kn_bwd")
    dkv = jnp.concatenate([dkp.reshape(m, MEM_HEADS, MEM_HD), dv.reshape(m, MEM_HEADS, MEM_HD)], axis=2).reshape(m, -1)
    dwkv = _mm(hm, dkv, ta=True, name=tag + "_dwkv")
    dhm = _mm(dkv, wkv, tb=True, name=tag + "_dhm")
    _, dgm = _norm_rows_bwd(dhm, mem, gm, None, name=tag + "_dmnorm")
    dwq = _mm(hq, dqp, ta=True, name=tag + "_dwq")
    dx, dg = _mm(dqp, wq, tb=True, norm_bwd=(x, g, dy), name=tag + "_dhq")
    return dx, dict(xmem_norm=dg, xmem_mem_norm=dgm, xmem_wq=dwq, xmem_wkv=dwkv, xmem_q_gain=dqg, xmem_k_gain=dkg,
                    xmem_wo=dwo)


def _local_step(x, mem, positions, tgt, w, late_shard, finish_late, partials_so_far):
    tabs = _rope_tables(positions)
    even = _even_weights(w["sbg_w_in"][0], w["sbg_w_out"][0])
    bt = jnp.repeat(w["sgu_b"][0].T, SG_GD, axis=1)
    saved = []
    for l in range(2):
        t = f"l{l}"
        x, s_pre = _ffn_fwd(x, w["ffn_pre_norm"][l], w["ffn_pre_w_gu"][l], w["ffn_pre_w_down"][l], t + "_pre")
        if l == 0:
            x, s_mix, late = _even_fwd(x, w["mix_norm"][0], even, w["sgu_ln_gain"][0], w["sgu_ln_bias"][0], w["sgu_w"][0],
                                       bt, late_shard, t + "_even")
            finish_late(late)
            mla = _mla_weights(w["mla_w_in"][0], w["mla_w_uq"][0], w["mla_w_ukv"][0], w["mla_w_out"][0],
                               w["mla_q_gain"][0], w["mla_k_gain"][0])
        else:
            x, s_mix = _mla_fwd(x, w["mix_norm"][1], mla, w["mla_q_lora_gain"][0], w["mla_kv_lora_gain"][0], tabs,
                                t + "_mla")
        x, s_xm = _xmem_fwd(x, mem, w["xmem_norm"][l], w["xmem_mem_norm"][l], w["xmem_wq"][l], w["xmem_wkv"][l],
                            w["xmem_q_gain"][l], w["xmem_k_gain"][l], w["xmem_wo"][l], t + "_xm")
        x, s_post = _ffn_fwd(x, w["ffn_post_norm"][l], w["ffn_post_w_gu"][l], w["ffn_post_w_down"][l], t + "_post")
        saved.append((s_pre, s_mix, s_xm, s_post))
    loss, dx = _loss_grad(x, tgt, name="loss")
    grads = {}

    def put(name, l, val):
        grads.setdefault(name, {})[l] = val

    for l in (1, 0):
        t = f"l{l}"
        s_pre, s_mix, s_xm, s_post = saved[l]
        dx, dg, dwgu, dwd = _ffn_bwd(dx, s_post, w["ffn_post_norm"][l], w["ffn_post_w_gu"][l], w["ffn_post_w_down"][l],
                                     t + "_post")
        put("ffn_post_norm", l, dg), put("ffn_post_w_gu", l, dwgu), put("ffn_post_w_down", l, dwd)
        dx, gx = _xmem_bwd(dx, s_xm, mem, w["xmem_norm"][l], w["xmem_mem_norm"][l], w["xmem_wq"][l], w["xmem_wkv"][l],
                           w["xmem_q_gain"][l], w["xmem_k_gain"][l], w["xmem_wo"][l], t + "_xm")
        for k_, v_ in gx.items():
            put(k_, l, v_)
        if l == 0:
            dx, gm, received = _even_bwd(dx, s_mix, w["mix_norm"][0], even, w["sgu_ln_gain"][0], w["sgu_w"][0], bt,
                                         partials_so_far(grads), t + "_even")
        else:
            dx, gm = _mla_bwd(dx, s_mix, w["mix_norm"][1], mla, w["mla_q_lora_gain"][0], w["mla_kv_lora_gain"][0], tabs,
                              t + "_mla")
        for k_, v_ in gm.items():
            put(k_, l if k_ == "mix_norm" else 0, v_)
        dx, dg, dwgu, dwd = _ffn_bwd(dx, s_pre, w["ffn_pre_norm"][l], w["ffn_pre_w_gu"][l], w["ffn_pre_w_down"][l],
                                     t + "_pre")
        put("ffn_pre_norm", l, dg), put("ffn_pre_w_gu", l, dwgu), put("ffn_pre_w_down", l, dwd)
    return loss, dx, {k_: [v_[l] for l in sorted(v_)] for k_, v_ in grads.items()}, received


N_CHIPS = 4
PACK_COLS = 1024
PACK_ROW_MULTIPLE = 512


def _place():
    x, y, c = lax.axis_index("x"), lax.axis_index("y"), lax.axis_index("c")
    return x, y, c, [(1 - x, y), (x, 1 - y), (1 - x, 1 - y)]


def _hops(x, y, c):
    return ((x + 1 - c) % 2, (y + c) % 2), ((x + c) % 2, (y + 1 - c) % 2), (1 - x, 1 - y)


GATHER_COPIES = 6
GATHER_STAGES = 4


def _gather_stages(x_ref, out_ref, send_sems, recv_sems):
    Rh = x_ref.shape[0] // 2
    x, y, c = lax.axis_index("x"), lax.axis_index("y"), lax.axis_index("c")
    n1, n2, nd = _hops(x, y, c)
    me, q1, q2, qd = 2 * x + y, 2 * n1[0] + n1[1], 2 * n2[0] + n2[1], 2 * nd[0] + nd[1]
    sibling = (x, y, 1 - c)

    def half(chip, core):
        return out_ref.at[chip, pl.ds(core * Rh, Rh), :]

    def copy(k, chip, core, to, own=False):
        return pltpu.make_async_remote_copy(src_ref=x_ref.at[pl.ds(c * Rh, Rh), :] if own else half(chip, core),
                                            dst_ref=half(chip, core), send_sem=send_sems.at[k], recv_sem=recv_sems.at[k],
                                            device_id=to, device_id_type=MESH)

    sends = [lambda: copy(0, me, c, (*n1, c), own=True), lambda: copy(1, me, c, (*n2, c), own=True),
             lambda: copy(2, q1, c, (*n2, c)), lambda: copy(3, q1, c, sibling), lambda: copy(4, q2, c, sibling),
             lambda: copy(5, qd, c, sibling)]

    def own_halves_out():
        sends[0]().start()
        sends[1]().start()

    def first_neighbours_on():
        copy(0, q1, c, sibling).wait_recv()
        sends[2]().start()
        sends[3]().start()

    def others_to_sibling():
        copy(1, q2, c, sibling).wait_recv()
        sends[4]().start()
        copy(2, qd, c, sibling).wait_recv()
        sends[5]().start()

    def all_landed():
        copy(3, q2, 1 - c, sibling).wait_recv()
        copy(4, q1, 1 - c, sibling).wait_recv()
        copy(5, qd, 1 - c, sibling).wait_recv()
        for send in sends:
            send().wait_send()

    return own_halves_out, first_neighbours_on, others_to_sibling, all_landed


N_DEVICES = 8


def _partials_exchange(g_ref, recv_ref, send_sems, recv_sems):
    Rh = g_ref.shape[1] // 2
    x, y, c = lax.axis_index("x"), lax.axis_index("y"), lax.axis_index("c")

    def copy(k):
        tx, ty, tc = (x + (k >> 2)) % 2, (y + ((k >> 1) & 1)) % 2, (c + (k & 1)) % 2
        return pltpu.make_async_remote_copy(src_ref=g_ref.at[2 * tx + ty, pl.ds(tc * Rh, Rh), :],
                                            dst_ref=recv_ref.at[4 * x + 2 * y + c], send_sem=send_sems.at[k],
                                            recv_sem=recv_sems.at[k], device_id=(tx, ty, tc), device_id_type=MESH)

    def start():
        for k in range(1, N_DEVICES):
            copy(k).start()

    def finish():
        for k in range(1, N_DEVICES):
            copy(k).wait_recv()
            copy(k).wait_send()

    return start, finish


def _sum_partials(g, recv, *, name):
    _, R, C = g.shape
    Rh = R // 2
    tr = _row_tile(Rh, 512)
    nt = Rh // tr
    x, y, c = lax.axis_index("x"), lax.axis_index("y"), lax.axis_index("c")
    where = jnp.stack([2 * x + y, c, 4 * x + 2 * y + c]).astype(jnp.int32)

    def body(where_ref, g_ref, r_ref, o_ref):
        own, mine = g_ref[0], where_ref[2]
        total = None
        for d in range(N_DEVICES):
            term = jnp.where(mine == d, own, r_ref[d].astype(F32))
            total = term if total is None else total + term
        o_ref[...] = total

    spec = pltpu.PrefetchScalarGridSpec(
        num_scalar_prefetch=1, grid=(nt,),
        in_specs=[pl.BlockSpec((1, tr, C), lambda i, wh: (wh[0], wh[1] * nt + i, 0)),
                  pl.BlockSpec((N_DEVICES, tr, C), lambda i, wh: (0, i, 0))],
        out_specs=pl.BlockSpec((tr, C), lambda i, wh: (i, 0)))
    return pl.pallas_call(body, name=name, grid_spec=spec, out_shape=jax.ShapeDtypeStruct((Rh, C), F32),
                          compiler_params=_params(1))(where, g, recv)


def _place_own_slot(others, shard):
    return lax.dynamic_update_slice(others, shard[None], (2 * lax.axis_index("x") + lax.axis_index("y"), 0, 0))


def _gather_chips(shard):
    def body(x_ref, out_ref, send_sems, recv_sems):
        for stage in _gather_stages(x_ref, out_ref, send_sems, recv_sems):
            stage()

    others = pl.pallas_call(
        body, name="gather_weights", out_shape=jax.ShapeDtypeStruct((N_CHIPS,) + shard.shape, shard.dtype),
        in_specs=[ANY], out_specs=ANY,
        scratch_shapes=[pltpu.SemaphoreType.DMA((GATHER_COPIES,)), pltpu.SemaphoreType.DMA((GATHER_COPIES,))])(shard)
    return _place_own_slot(others, shard)


def _gather_devices(block):
    M, N = block.shape

    def body(x_ref, out_ref, send_sems, recv_sems, local_sem):
        x, y, c, chips = _place()
        me, sibling = (x, y, c), (x, y, 1 - c)

        def rows(px, py, pc):
            return out_ref.at[pl.ds((4 * px + 2 * py + pc) * M, M), :]

        def copy(k, blk, to, src=None):
            return pltpu.make_async_remote_copy(src_ref=rows(*blk) if src is None else src, dst_ref=rows(*blk),
                                                send_sem=send_sems.at[k], recv_sem=recv_sems.at[k], device_id=to,
                                                device_id_type=MESH)

        mine = pltpu.make_async_copy(x_ref, rows(*me), local_sem)
        mine.start()
        first = [copy(0, me, sibling, src=x_ref)]
        first += [copy(1 + j, me, (*chip, c), src=x_ref) for j, chip in enumerate(chips)]
        for cp in first:
            cp.start()
        passed = [copy(4 + j, (*chip, c), sibling) for j, chip in enumerate(chips)]
        for j, chip in enumerate(chips):
            copy(1 + j, (*chip, c), me).wait_recv()
            passed[j].start()
        copy(0, sibling, me).wait_recv()
        for j, chip in enumerate(chips):
            copy(4 + j, (*chip, 1 - c), me).wait_recv()
        for cp in first + passed:
            cp.wait_send()
        mine.wait()

    vmem = pl.BlockSpec(memory_space=pltpu.VMEM)
    return pl.pallas_call(
        body, name=f"gather_devices_{M}", out_shape=jax.ShapeDtypeStruct((8 * M, N), block.dtype),
        in_specs=[vmem], out_specs=vmem,
        scratch_shapes=[pltpu.SemaphoreType.DMA((7,)), pltpu.SemaphoreType.DMA((7,)), pltpu.SemaphoreType.DMA],
        compiler_params=pltpu.CompilerParams(vmem_limit_bytes=VMEM_LIMIT))(block)


def _swap_halves(g):
    n, R, C = g.shape
    Rh = R // 2

    def body(g_ref, a_ref, send_sem, recv_sem):
        x, y, c, _ = _place()
        cp = pltpu.make_async_remote_copy(src_ref=g_ref.at[:, pl.ds((1 - c) * Rh, Rh), :], dst_ref=a_ref,
                                          send_sem=send_sem, recv_sem=recv_sem, device_id=(x, y, 1 - c),
                                          device_id_type=MESH)
        cp.start()
        cp.wait()

    return pl.pallas_call(body, name="grad_swap_halves", out_shape=jax.ShapeDtypeStruct((n, Rh, C), g.dtype),
                          in_specs=[ANY], out_specs=ANY,
                          scratch_shapes=[pltpu.SemaphoreType.DMA, pltpu.SemaphoreType.DMA])(g)


def _add_picked(a, b, picks, *, a_row_half=None, out_dtype, name):
    n_out = picks.shape[0]
    _, rows, C = b.shape
    tr = _row_tile(rows, 512)
    nt = rows // tr
    half = jnp.zeros((1,), jnp.int32) if a_row_half is None else a_row_half

    def body(pick_ref, half_ref, a_ref, b_ref, o_ref):
        o_ref[...] = (a_ref[...].astype(F32) + b_ref[...].astype(F32)).astype(o_ref.dtype)

    spec = pltpu.PrefetchScalarGridSpec(
        num_scalar_prefetch=2, grid=(n_out, nt),
        in_specs=[pl.BlockSpec((1, tr, C), lambda j, i, pick, hf: (pick[j], hf[0] * nt + i, 0)),
                  pl.BlockSpec((1, tr, C), lambda j, i, pick, hf: (pick[j], i, 0))],
        out_specs=pl.BlockSpec((1, tr, C), lambda j, i, pick, hf: (j, i, 0)))
    return pl.pallas_call(body, name=name, grid_spec=spec, out_shape=jax.ShapeDtypeStruct((n_out, rows, C), out_dtype),
                          compiler_params=_params(2))(picks.astype(jnp.int32), half.astype(jnp.int32), a, b)


def _hop_exchange(src, hop, *, name):
    def body(s_ref, d_ref, send_sem, recv_sem):
        x, y, c = lax.axis_index("x"), lax.axis_index("y"), lax.axis_index("c")
        cp = pltpu.make_async_remote_copy(src_ref=s_ref, dst_ref=d_ref, send_sem=send_sem, recv_sem=recv_sem,
                                          device_id=(*_hops(x, y, c)[hop], c), device_id_type=MESH)
        cp.start()
        cp.wait()

    return pl.pallas_call(body, name=name, out_shape=jax.ShapeDtypeStruct(src.shape, src.dtype), in_specs=[ANY],
                          out_specs=ANY, scratch_shapes=[pltpu.SemaphoreType.DMA, pltpu.SemaphoreType.DMA])(src)


def _reduce_over_chips(g):
    x, y, c = lax.axis_index("x"), lax.axis_index("y"), lax.axis_index("c")
    n1, n2, _ = _hops(x, y, c)
    chip = lambda p: 2 * p[0] + p[1]
    near = jnp.stack([chip((x, y)), chip(n2)])
    far = jnp.stack([chip(n1), chip((1 - x, 1 - y))])
    half = c.reshape(1)
    sib = _swap_halves(g)
    kept = _add_picked(g, sib, near, a_row_half=half, out_dtype=F32, name="grad_add_near")
    sent = _add_picked(g, sib, far, a_row_half=half, out_dtype=BF16, name="grad_add_far")
    got = _hop_exchange(sent, 0, name="grad_hop_first")
    mine = _add_picked(kept, got, jnp.zeros((1,), jnp.int32), out_dtype=F32, name="grad_add_mine")
    theirs = _add_picked(kept, got, jnp.ones((1,), jnp.int32), out_dtype=BF16, name="grad_add_theirs")
    got = _hop_exchange(theirs, 1, name="grad_hop_second")
    total = _add_picked(mine, got, jnp.zeros((1,), jnp.int32), out_dtype=F32, name="grad_add_total")
    return _join_halves(total[0])


def _sum_slots(b, *, name):
    n, R, C = b.shape
    tr = _row_tile(R, 512)

    def body(b_ref, o_ref):
        acc = b_ref[0]
        for q in range(1, n):
            acc = acc + b_ref[q]
        o_ref[...] = acc

    return pl.pallas_call(body, name=name, grid=(R // tr,), in_specs=[pl.BlockSpec((n, tr, C), lambda i: (0, i, 0))],
                          out_specs=pl.BlockSpec((tr, C), lambda i: (i, 0)), out_shape=jax.ShapeDtypeStruct((R, C), F32),
                          compiler_params=_params(1))(b)


def _join_halves(r):
    Rh, C = r.shape

    def body(r_ref, o_ref, send_sem, recv_sem):
        x, y, c, _ = _place()
        own, other = o_ref.at[pl.ds(c * Rh, Rh), :], o_ref.at[pl.ds((1 - c) * Rh, Rh), :]
        cp = pltpu.make_async_remote_copy(src_ref=r_ref, dst_ref=own, send_sem=send_sem, recv_sem=recv_sem,
                                          device_id=(x, y, 1 - c), device_id_type=MESH)
        cp.start()
        pltpu.make_async_remote_copy(src_ref=r_ref, dst_ref=other, send_sem=send_sem, recv_sem=recv_sem,
                                     device_id=(x, y, 1 - c), device_id_type=MESH).wait_recv()
        cp.wait_send()

    theirs = pl.pallas_call(
        body, name="grad_join_halves", out_shape=jax.ShapeDtypeStruct((2 * Rh, C), r.dtype), in_specs=[ANY], out_specs=ANY,
        scratch_shapes=[pltpu.SemaphoreType.DMA, pltpu.SemaphoreType.DMA])(r)
    return lax.dynamic_update_slice(theirs, r, (lax.axis_index("c") * Rh, 0))


def _size(shape):
    size = 1
    for d in shape:
        size *= d
    return size


def _pack(pieces, cols, row_multiple, dtype):
    if any(p.size % cols for p in pieces):
        flat = jnp.concatenate([p.reshape(-1).astype(dtype) for p in pieces])
        pieces = [jnp.pad(flat, (0, -flat.shape[0] % cols))]
    rows = [p.reshape(-1, cols).astype(dtype) for p in pieces]
    pad = -sum(r.shape[0] for r in rows) % row_multiple
    return jnp.concatenate(rows + ([jnp.zeros((pad, cols), dtype)] if pad else []), axis=0)


def _unpack(buf, shapes):
    cols = buf.shape[1]
    if any(_size(s) % cols for s in shapes):
        flat, out, at = buf.reshape(-1), [], 0
        for shp in shapes:
            out.append(flat[at:at + _size(shp)].reshape(shp))
            at += _size(shp)
        return out
    out, at = [], 0
    for shp in shapes:
        out.append(buf[at:at + _size(shp) // cols].reshape(shp))
        at += _size(shp) // cols
    return out


SHARDED = (("ffn_pre_w_gu", 2), ("ffn_pre_w_down", 1), ("sbg_w_in", 2), ("sbg_w_out", 1), ("mla_w_in", 1),
           ("mla_w_uq", 2), ("mla_w_ukv", 2), ("mla_w_out", 1), ("xmem_wq", 1), ("xmem_wkv", 2), ("xmem_wo", 1),
           ("ffn_post_w_gu", 2), ("ffn_post_w_down", 1))
EARLY = (("ffn_pre_w_gu", 0), ("ffn_pre_w_down", 0), ("sbg_w_in", 0), ("sbg_w_out", 0))
LORA_GAINS = ("mla_q_lora_gain", "mla_kv_lora_gain")
REPLICATED = ("ffn_pre_norm", "mix_norm", "sgu_ln_gain", "sgu_ln_bias", "sgu_w", "sgu_b", "mla_q_gain", "mla_k_gain",
              "xmem_norm", "xmem_mem_norm", "xmem_q_gain", "xmem_k_gain", "ffn_post_norm")
WEIGHTS = ("ffn_pre_norm", "ffn_pre_w_gu", "ffn_pre_w_down", "mix_norm", "sbg_w_in", "sgu_ln_gain", "sgu_ln_bias", "sgu_w",
           "sgu_b", "sbg_w_out", "mla_w_in", "mla_q_lora_gain", "mla_kv_lora_gain", "mla_w_uq", "mla_w_ukv", "mla_q_gain",
           "mla_k_gain", "mla_w_out", "xmem_norm", "xmem_mem_norm", "xmem_wq", "xmem_wkv", "xmem_q_gain", "xmem_k_gain",
           "xmem_wo", "ffn_post_norm", "ffn_post_w_gu", "ffn_post_w_down")
INPUTS = ("x", "mem", "positions") + WEIGHTS + ("loss_target",) + tuple("m_" + n for n in WEIGHTS) + tuple(
    "v_" + n for n in WEIGHTS)


def _step(a):
    x, y, c, _ = _place()
    chip = 2 * x + y
    w = {n: [None] * a[n].shape[0] for n, _ in SHARDED}
    lots = {early: [(n, l, ax) for n, ax in SHARDED for l in range(a[n].shape[0]) if ((n, l) in EARLY) == early]
            for early in (True, False)}

    def packed(lot):
        return _pack([a[n][l] for n, l, _ in lot], PACK_COLS, PACK_ROW_MULTIPLE, BF16)

    def unpack(gathered, lot):
        at = 0
        for n, l, ax in lot:
            shp = a[n].shape[1:]
            rows = _size(shp) // PACK_COLS
            per_chip = gathered[:, at:at + rows].reshape((N_CHIPS,) + shp)
            at += rows
            w[n][l] = jnp.moveaxis(per_chip, 0, ax - 1).reshape(shp[:ax - 1] + (N_CHIPS * shp[ax - 1],) + shp[ax:])

    unpack(_gather_chips(packed(lots[True])), lots[True])
    gains = jnp.zeros((8, LANE), F32)
    for r, n in enumerate(LORA_GAINS):
        gains = gains.at[r, :a[n].shape[1]].set(a[n][0])
    gains = _gather_devices(gains)
    for r, n in enumerate(LORA_GAINS):
        w[n] = jnp.concatenate([gains[16 * q + r, :a[n].shape[1]] for q in range(N_CHIPS)])[None, :]
    for n in REPLICATED:
        w[n] = a[n]

    def packed_grads(lot, grads):
        def part(n, l, ax, q):
            size = a[n].shape[ax]
            return lax.slice_in_dim(grads[n][l], q * size, (q + 1) * size, axis=ax - 1)

        return jnp.stack([_pack([part(n, l, ax, q) for n, l, ax in lot], PACK_COLS, PACK_ROW_MULTIPLE, F32)
                          for q in range(N_CHIPS)])

    sent = {}

    def partials_so_far(grads):
        sent["f32"] = packed_grads(lots[False], grads)
        return sent["f32"].astype(BF16)

    loss, dx, grads, received = _local_step(a["x"][0], a["mem"][0], a["positions"][0], a["loss_target"][0], w,
                                            packed(lots[False]), lambda gathered: unpack(gathered, lots[False]),
                                            partials_so_far)
    loss = lax.psum(loss, ("x", "y", "c"))
    small_names = REPLICATED + LORA_GAINS
    full = {n: jnp.stack(grads[n]).reshape(w[n].shape) for n in small_names}

    reduced = {False: _join_halves(_sum_partials(sent["f32"], received, name="grad_sum_partials")),
               True: _reduce_over_chips(packed_grads(lots[True], grads))}
    layers = {n: [None] * a[n].shape[0] for n, _ in SHARDED}
    for early, lot in lots.items():
        for (n, l, _), piece in zip(lot, _unpack(reduced[early], [a[n].shape[1:] for n, _, _ in lot])):
            layers[n][l] = piece
    gw = {n: jnp.stack(layers[n]) for n, _ in SHARDED}

    small = _pack([full[n] for n in small_names], LANE, 256, F32)
    rows = small.shape[0]
    summed = _sum_slots(_gather_devices(small).reshape(8, rows, LANE), name="grad_sum_devices")
    for n, val in zip(small_names, _unpack(summed, [full[n].shape for n in small_names])):
        if n in LORA_GAINS:
            size = a[n].shape[1]
            val = lax.dynamic_slice_in_dim(val, chip * size, size, axis=1)
        gw[n] = val

    upd = {n: _adamw(a[n], gw[n], a["m_" + n], a["v_" + n], name="adamw_" + n) for n in WEIGHTS}
    return (loss, dx[None], *[gw[n] for n in WEIGHTS], *[upd[n][0] for n in WEIGHTS], *[upd[n][1] for n in WEIGHTS],
            *[upd[n][2] for n in WEIGHTS])


def kernel(x, mem, positions, ffn_pre_norm, ffn_pre_w_gu, ffn_pre_w_down, mix_norm, sbg_w_in, sgu_ln_gain,
           sgu_ln_bias, sgu_w, sgu_b, sbg_w_out, mla_w_in, mla_q_lora_gain, mla_kv_lora_gain, mla_w_uq, mla_w_ukv,
           mla_q_gain, mla_k_gain, mla_w_out, xmem_norm, xmem_mem_norm, xmem_wq, xmem_wkv, xmem_q_gain, xmem_k_gain,
           xmem_wo, ffn_post_norm, ffn_post_w_gu, ffn_post_w_down, loss_target, m_ffn_pre_norm, m_ffn_pre_w_gu,
           m_ffn_pre_w_down, m_mix_norm, m_sbg_w_in, m_sgu_ln_gain, m_sgu_ln_bias, m_sgu_w, m_sgu_b, m_sbg_w_out,
           m_mla_w_in, m_mla_q_lora_gain, m_mla_kv_lora_gain, m_mla_w_uq, m_mla_w_ukv, m_mla_q_gain, m_mla_k_gain,
           m_mla_w_out, m_xmem_norm, m_xmem_mem_norm, m_xmem_wq, m_xmem_wkv, m_xmem_q_gain, m_xmem_k_gain,
           m_xmem_wo, m_ffn_post_norm, m_ffn_post_w_gu, m_ffn_post_w_down, v_ffn_pre_norm, v_ffn_pre_w_gu,
           v_ffn_pre_w_down, v_mix_norm, v_sbg_w_in, v_sgu_ln_gain, v_sgu_ln_bias, v_sgu_w, v_sgu_b, v_sbg_w_out,
           v_mla_w_in, v_mla_q_lora_gain, v_mla_kv_lora_gain, v_mla_w_uq, v_mla_w_ukv, v_mla_q_gain, v_mla_k_gain,
           v_mla_w_out, v_xmem_norm, v_xmem_mem_norm, v_xmem_wq, v_xmem_wkv, v_xmem_q_gain, v_xmem_k_gain,
           v_xmem_wo, v_ffn_post_norm, v_ffn_post_w_gu, v_ffn_post_w_down):
    given = locals()
    return _step({n: given[n] for n in INPUTS})
```

```python
import jax
import jax.numpy as jnp
from jax import lax
from jax.experimental import pallas as pl
from jax.experimental.pallas import tpu as pltpu

F32, BF16 = jnp.float32, jnp.bfloat16
LANE = 128
VMEM_LIMIT = 56 * 1024 * 1024
EPS = 1e-6
SB_HEADS, SB_HD = 8, 64
SG_GROUPS, SG_GD, SG_CHUNK = 8, 64, 128
SB_W, SG_W = SB_HEADS * SB_HD, SG_GROUPS * SG_GD
MLA_HEADS, MLA_NOPE, MLA_ROPE, MLA_V = 16, 64, 32, 64
MLA_QK = MLA_NOPE + MLA_ROPE
MLA_QL, MLA_KVL = 512, 256
ROPE_THETA = 10000.0
MEM_HEADS, MEM_HD = 4, 256
SB_SCALE, MLA_SCALE, MEM_SCALE = SB_HD ** -0.5, MLA_QK ** -0.5, MEM_HD ** -0.5
ADAM_LR, ADAM_B1, ADAM_B2, ADAM_EPS, ADAM_WD, ADAM_STEP = 0.001, 0.9, 0.999, 1e-08, 0.01, 10
MESH = pl.DeviceIdType.MESH
ANY = pl.BlockSpec(memory_space=pl.ANY)


def _params(n_axes):
    return pltpu.CompilerParams(dimension_semantics=("arbitrary",) * n_axes, vmem_limit_bytes=VMEM_LIMIT)


MM_TILE_CAP = 1408
MM_VMEM_BUDGET = 40 * 1024 * 1024


def _tile(dim, cap):
    if dim <= cap:
        return dim
    best = max(t for t in range(LANE, cap + 1, LANE) if dim % t == 0)
    return best


def _k_scratch(count, tile, nk):
    return [pltpu.VMEM(tile, F32)] * count if nk > 1 else []


def _over_k_steps(prods, acc_refs, nk, finish):
    if nk == 1:
        finish(prods)
        return
    kk = pl.program_id(2)

    @pl.when(kk == 0)
    def _():
        for ref, p in zip(acc_refs, prods):
            ref[...] = p

    @pl.when(kk > 0)
    def _():
        for ref, p in zip(acc_refs, prods):
            ref[...] += p

    @pl.when(kk == nk - 1)
    def _():
        finish([ref[...] for ref in acc_refs])


def _mm(a, b, *, ta=False, tb=False, out_dtype=F32, scale=1.0, residual=None, bias=None, norm_bwd=None, a_off=(0, 0),
        b_off=(0, 0), m=None, n=None, k=None, name):
    am, ak = (a.shape[1], a.shape[0]) if ta else a.shape
    bk, bn = (b.shape[1], b.shape[0]) if tb else b.shape
    M, N, K = m or am, n or bn, k or ak
    tm, tn = _tile(M, MM_TILE_CAP if norm_bwd is None else MM_TILE_CAP // 2), _tile(N, MM_TILE_CAP)
    n_full = (residual is not None) + (2 if norm_bwd is not None else 0)
    fixed = tm * tn * (4 + 2 * jnp.dtype(out_dtype).itemsize + 8 * n_full)
    per_k = (tm * (2 * a.dtype.itemsize + 2) + tn * (2 * b.dtype.itemsize + 2))
    tk = _tile(K, max(LANE, (MM_VMEM_BUDGET - fixed) // per_k))
    nm, nn, nk = M // tm, N // tn, K // tk
    assert norm_bwd is None or nn == 1
    a_off = (a_off[0] // (tk if ta else tm), a_off[1] // (tm if ta else tk))
    b_off = (b_off[0] // (tn if tb else tk), b_off[1] // (tk if tb else tn))
    dims = (((0 if ta else 1,), (1 if tb else 0,)), ((), ()))
    n_out = 1 if norm_bwd is None else 3

    def body(*refs):
        a_ref, b_ref = refs[0], refs[1]
        n_in = len(ins)
        o_ref, extras, acc_refs = refs[n_in], refs[2:n_in], refs[n_in + n_out:]
        first_rows = pl.program_id(0) == 0

        def finish(total):
            out = total * scale
            for extra in (extras if norm_bwd is None else extras[:-3]):
                out = out + extra[...].astype(F32)
            if norm_bwd is not None:
                x_ref, g_ref, dres_ref = extras[-3:]
                dg_ref = refs[n_in + 1]
                dx, dg = _rmsnorm_bwd(out, x_ref[...], g_ref[...])
                out = dx + dres_ref[...]

                @pl.when(first_rows)
                def _():
                    dg_ref[...] = jnp.zeros_like(dg_ref)

                dg_ref[...] += dg
                refs[n_in + 2][...] = out.astype(BF16)
            o_ref[...] = out.astype(o_ref.dtype)

        prod = lax.dot_general(a_ref[...].astype(BF16), b_ref[...].astype(BF16), dims, preferred_element_type=F32)
        _over_k_steps([prod], acc_refs, nk, lambda totals: finish(totals[0]))

    (ao0, ao1), (bo0, bo1) = a_off, b_off
    a_spec = (pl.BlockSpec((tk, tm), lambda i, j, kk: (kk + ao0, i + ao1)) if ta
              else pl.BlockSpec((tm, tk), lambda i, j, kk: (i + ao0, kk + ao1)))
    b_spec = (pl.BlockSpec((tn, tk), lambda i, j, kk: (j + bo0, kk + bo1)) if tb
              else pl.BlockSpec((tk, tn), lambda i, j, kk: (kk + bo0, j + bo1)))
    o_spec = pl.BlockSpec((tm, tn), lambda i, j, kk: (i, j))
    ins, in_specs = [a, b], [a_spec, b_spec]
    if residual is not None:
        ins.append(residual)
        in_specs.append(o_spec)
    if bias is not None:
        ins.append(bias)
        in_specs.append(pl.BlockSpec((1, tn), lambda i, j, kk: (0, j)))
    out_specs, out_shape = o_spec, jax.ShapeDtypeStruct((M, N), out_dtype)
    if norm_bwd is not None:
        x, gain, dres = norm_bwd
        row = pl.BlockSpec((1, tn), lambda i, j, kk: (0, 0))
        ins += [x, gain.reshape(1, N), dres]
        in_specs += [o_spec, row, o_spec]
        out_specs = [o_spec, row, o_spec]
        out_shape = [out_shape, jax.ShapeDtypeStruct((1, N), F32), jax.ShapeDtypeStruct((M, N), BF16)]
    return pl.pallas_call(
        body, name=name, grid=(nm, nn, nk), in_specs=in_specs, out_specs=out_specs, out_shape=out_shape,
        scratch_shapes=_k_scratch(1, (tm, tn), nk), compiler_params=_params(3))(*ins)


def _mm_swiglu(h, wgu, *, name):
    M, K = h.shape
    F = wgu.shape[1] // 2
    tm, tn, tk = _tile(M, 512), _tile(F, MM_TILE_CAP), _tile(K, 1024)
    nm, nf, nk = M // tm, F // tn, K // tk

    def body(h_ref, wg_ref, wu_ref, g_ref, u_ref, a_ref, *acc_refs):
        def finish(totals):
            g, u = totals
            g_ref[...] = g.astype(BF16)
            u_ref[...] = u.astype(BF16)
            a_ref[...] = (g * jax.nn.sigmoid(g) * u).astype(BF16)

        hb = h_ref[...]
        _over_k_steps([jnp.dot(hb, wg_ref[...], preferred_element_type=F32),
                       jnp.dot(hb, wu_ref[...], preferred_element_type=F32)], acc_refs, nk, finish)

    o_spec = pl.BlockSpec((tm, tn), lambda j, i, kk: (i, j))
    shp = jax.ShapeDtypeStruct((M, F), BF16)
    return pl.pallas_call(
        body, name=name, grid=(nf, nm, nk),
        in_specs=[pl.BlockSpec((tm, tk), lambda j, i, kk: (i, kk)),
                  pl.BlockSpec((tk, tn), lambda j, i, kk: (kk, j)),
                  pl.BlockSpec((tk, tn), lambda j, i, kk: (kk, j + nf))],
        out_specs=[o_spec, o_spec, o_spec], out_shape=[shp, shp, shp],
        scratch_shapes=_k_scratch(2, (tm, tn), nk), compiler_params=_params(3))(h, wgu, wgu)


def _mm_dswiglu(dy, wd, gate, up, *, scale, name):
    M, K = dy.shape
    F = wd.shape[0]
    tm, tn, tk = _tile(M, 512), _tile(F, MM_TILE_CAP), _tile(K, 1024)
    nm, nf, nk = M // tm, F // tn, K // tk

    def body(dy_ref, wd_ref, g_ref, u_ref, dg_ref, du_ref, *acc_refs):
        def finish(totals):
            da = totals[0] * scale
            g, u = g_ref[...].astype(F32), u_ref[...].astype(F32)
            sg = jax.nn.sigmoid(g)
            du_ref[...] = (da * g * sg).astype(BF16)
            dg_ref[...] = (da * u * sg * (1.0 + g * (1.0 - sg))).astype(BF16)

        _over_k_steps([_nt(dy_ref[...].astype(BF16), wd_ref[...])], acc_refs, nk, finish)

    o_spec = pl.BlockSpec((tm, tn), lambda j, i, kk: (i, j))
    shp = jax.ShapeDtypeStruct((M, F), BF16)
    return pl.pallas_call(
        body, name=name, grid=(nf, nm, nk),
        in_specs=[pl.BlockSpec((tm, tk), lambda j, i, kk: (i, kk)),
                  pl.BlockSpec((tn, tk), lambda j, i, kk: (j, kk)), o_spec, o_spec],
        out_specs=[o_spec, o_spec], out_shape=[shp, shp],
        scratch_shapes=_k_scratch(1, (tm, tn), nk), compiler_params=_params(3))(dy, wd, gate, up)


HEAD_ROWS = 2048


def _row_tile(rows, cap):
    t = cap
    while t >= 8:
        if rows % t == 0:
            return t
        t //= 2
    return rows


def _rowwise(fn, rows, consts, outs, sums=(), hsums=(), *, heads=None, tm=512, name):
    rows = [r if isinstance(r, tuple) else (r, r.shape[1], None) for r in rows]
    rows = [r if len(r) == 4 else (*r, False) for r in rows]
    S = rows[0][0].shape[0]
    tm = _row_tile(S, tm)
    nh = heads or 1
    n_r, n_c, n_o, n_h, n_s = len(rows), len(consts), len(outs), len(hsums), len(sums)

    def body(*refs):
        r = [x[...].T if row[3] else x[...] for x, row in zip(refs, rows)]
        c = [x[...] for x in refs[n_r:n_r + n_c]]
        o_refs = refs[n_r + n_c:n_r + n_c + n_o]
        h_refs = refs[n_r + n_c + n_o:n_r + n_c + n_o + n_h]
        s_refs = refs[n_r + n_c + n_o + n_h:]
        res = fn(*r, *c)
        res = res if isinstance(res, (tuple, list)) else (res,)
        for ref, val in zip(o_refs, res[:n_o]):
            ref[...] = val.astype(ref.dtype)
        if n_h:
            @pl.when(pl.program_id(1) == 0)
            def _():
                for ref in h_refs:
                    ref[...] = jnp.zeros_like(ref)
            for ref, val in zip(h_refs, res[n_o:n_o + n_h]):
                ref[...] += val
        if n_s:
            @pl.when((pl.program_id(0) == 0) & (pl.program_id(1) == 0))
            def _():
                for ref in s_refs:
                    ref[...] = jnp.zeros_like(ref)
            for ref, val in zip(s_refs, res[n_o + n_h:]):
                ref[...] += val

    def col(colfn):
        return (lambda i, h: (i, 0)) if colfn is None else (lambda i, h: (i, colfn(h)))

    in_specs = [pl.BlockSpec((w, tm), lambda i, h, cf=cf: (cf(h), i)) if flipped else pl.BlockSpec((tm, w), col(cf))
                for _, w, cf, flipped in rows]
    in_specs += [pl.BlockSpec(a.shape, lambda i, h, nd=a.ndim: (0,) * nd) for a in consts]
    out_specs = [pl.BlockSpec((tm, w // nh), (lambda i, h: (i, h)) if heads else (lambda i, h: (i, 0))) for w, _ in outs]
    out_specs += [pl.BlockSpec((tm, w), lambda i, h: (i, 0)) for w in hsums]
    out_specs += [pl.BlockSpec(sh, lambda i, h, nd=len(sh): (0,) * nd) for sh in sums]
    out_shape = [jax.ShapeDtypeStruct((S, w), dt) for w, dt in outs]
    out_shape += [jax.ShapeDtypeStruct((S, w), F32) for w in hsums]
    out_shape += [jax.ShapeDtypeStruct(sh, F32) for sh in sums]
    return pl.pallas_call(body, name=name, grid=(S // tm, nh), in_specs=in_specs, out_specs=out_specs,
                          out_shape=out_shape, compiler_params=_params(2))(*[row[0] for row in rows], *consts)


def _rms(x, width=None):
    width = width or x.shape[-1]
    return lax.rsqrt(jnp.sum(x * x, axis=-1, keepdims=True) * (1.0 / width) + EPS)


def _rmsnorm_fwd(x, g, width=None):
    return x * _rms(x, width) * g


def _rmsnorm_bwd(dy, x, g, width=None):
    width = width or x.shape[-1]
    r = _rms(x, width)
    xn = x * r
    dxn = dy * g
    dx = r * (dxn - xn * (jnp.sum(dxn * xn, axis=-1, keepdims=True) * (1.0 / width)))
    return dx, jnp.sum(dy * xn, axis=0, keepdims=True)


def _norm_rows(x, g, *, name, out_dtype=BF16):
    D = x.shape[1]
    return _rowwise(lambda xv, gv: _rmsnorm_fwd(xv.astype(F32), gv), [x], [g.reshape(1, D)], [(D, out_dtype)],
                    tm=1024, name=name)[0]


def _norm_rows_bwd(dh, x, g, dres, *, name):
    D = x.shape[1]

    def fn(dhv, xv, *rest):
        dx, dg = _rmsnorm_bwd(dhv.astype(F32), xv, rest[-1])
        return (dx + rest[0] if dres is not None else dx), dg

    rows = [dh, x] + ([dres] if dres is not None else [])
    return _rowwise(fn, rows, [g.reshape(1, D)], [(D, F32)], [(1, D)], name=name)


def _softplus(z):
    return jnp.where(z > 20.0, z, jnp.log(1.0 + jnp.exp(z)))


def _running_sum(v, u, split=True):
    if not split:
        return jnp.dot(v.astype(BF16), u, preferred_element_type=F32)
    hi = lax.bitcast_convert_type(lax.bitcast_convert_type(v, jnp.uint32) & jnp.uint32(0xFFFF0000), F32)
    return (jnp.dot(hi.astype(BF16), u, preferred_element_type=F32)
            + jnp.dot((v - hi).astype(BF16), u, preferred_element_type=F32))


def _triangle(tk, inclusive_prefix):
    j, s = lax.broadcasted_iota(jnp.int32, (tk, tk), 0), lax.broadcasted_iota(jnp.int32, (tk, tk), 1)
    return ((j <= s) if inclusive_prefix else (j > s)).astype(BF16)


def _nt(a, b):
    return lax.dot_general(a, b, (((1,), (1,)), ((), ())), preferred_element_type=F32)


def _tn(a, b):
    return lax.dot_general(a, b, (((0,), (0,)), ((), ())), preferred_element_type=F32)


ATT_TQ, ATT_TK = 512, 512
SB_SUB = 256
FWD_GROUP = 2


def _attn_fwd(q, k, v, *, sb, causal, heads, dq, dv, group=1, kcol=None, vcol=None, sum_lane=None, side_gather=None,
              name):
    S, Sk = q.shape[0], k.shape[0]
    tq, tk = min(ATT_TQ, S), min(ATT_TK, Sk)
    sub = min(SB_SUB, tk) if sb else tk
    assert tq % sub == 0 or not causal
    kcol = kcol or (lambda h: h)
    vcol = vcol or (lambda h: h)
    members = range(group)
    assert side_gather is None or heads // group >= GATHER_STAGES

    def body(*refs):
        if side_gather is not None:
            n_in = 4 if sb else 3
            stages = _gather_stages(refs[n_in], refs[n_in + 3], *refs[-2:])
            for n, stage in enumerate(stages):
                pl.when((pl.program_id(0) == n) & (pl.program_id(1) == 0))(stage)
            refs = refs[:n_in] + refs[n_in + 1:n_in + 3] + refs[n_in + 4:-2]
        if sb:
            q_ref, k_ref, v_ref, u_ref, o_ref, lse_ref, acc_ref, r_ref = refs
            r_ref[...] = jnp.zeros_like(r_ref)
        else:
            q_ref, k_ref, v_ref, o_ref, lse_ref, acc_ref, m_ref, l_ref = refs
            m_ref[...] = jnp.full_like(m_ref, -1e30)
            l_ref[...] = jnp.zeros_like(l_ref)
        first_row = pl.program_id(1) * tq
        qb = [q_ref[:, hh * dq:(hh + 1) * dq] for hh in members]
        acc_ref[...] = jnp.zeros_like(acc_ref)
        nblk = (first_row + tq) // sub if causal else Sk // sub
        nfull = (first_row + (0 if sb else 1)) // sub if causal else nblk
        n_cut = tq // sub if causal else 0

        def scores(jj):
            off = pl.multiple_of(jj * sub, sub)
            return tuple(_nt(qb[hh], k_ref[pl.ds(off, sub), hh * dq:(hh + 1) * dq]) for hh in members)

        def weigh(jj, scores_now, masked):
            off = pl.multiple_of(jj * sub, sub)
            if masked:
                kpos = off + lax.broadcasted_iota(jnp.int32, (tq, sub), 1)
                qpos = first_row + lax.broadcasted_iota(jnp.int32, (tq, sub), 0)
                valid = (kpos < qpos) if sb else (kpos <= qpos)
            for hh in members:
                vb = v_ref[pl.ds(off, sub), hh * dv:(hh + 1) * dv]
                s = scores_now[hh]
                if sb:
                    sp = _softplus(s)
                    ls = jnp.where(valid, -sp, 0.0) if masked else -sp
                    w = jnp.exp(s - sp + r_ref[hh] + _running_sum(ls, u_ref[...]))
                    if masked:
                        w = jnp.where(valid, w, 0.0)
                    acc_ref[hh] += jnp.dot(w.astype(BF16), vb, preferred_element_type=F32)
                    r_ref[hh] += jnp.sum(ls, axis=1, keepdims=True)
                else:
                    if masked:
                        s = jnp.where(valid, s, -1e30)
                    m_old = m_ref[hh]
                    m_new = jnp.maximum(m_old, jnp.max(s, axis=1, keepdims=True))
                    p = jnp.exp(s - m_new)
                    alpha = jnp.exp(m_old - m_new)
                    if sum_lane is None:
                        l_ref[hh] = alpha * l_ref[hh] + jnp.sum(p, axis=1, keepdims=True)
                    acc_ref[hh] = alpha * acc_ref[hh] + jnp.dot(p.astype(BF16), vb, preferred_element_type=F32)
                    m_ref[hh] = m_new

        if sb:
            s_cur = scores(nblk - 1)
            for cut in range(n_cut):
                s_next = scores(jnp.maximum(nblk - 2 - cut, 0))
                weigh(nblk - 1 - cut, s_cur, True)
                s_cur = s_next

            def step(t, s_now):
                s_next = scores(jnp.maximum(nfull - 2 - t, 0))
                weigh(nfull - 1 - t, s_now, False)
                return s_next

            lax.fori_loop(0, nfull, step, s_cur)
        else:
            n_loop = nfull if causal else nblk - 1

            def step(t, s_now):
                s_next = scores(jnp.minimum(t + 1, nblk - 1))
                weigh(t, s_now, False)
                return s_next

            s_cur = lax.fori_loop(0, n_loop, step, scores(0))
            tail = n_cut if causal else 1
            for last in range(tail):
                s_next = scores(n_loop + last + 1) if last + 1 < tail else None
                weigh(n_loop + last, s_cur, causal)
                s_cur = s_next
        for hh in members:
            cols = slice(hh * dv, (hh + 1) * dv)
            if sb:
                o_ref[:, cols] = acc_ref[hh]
                lse_ref[hh] = r_ref[hh]
            else:
                acc = acc_ref[hh]
                l = l_ref[hh] if sum_lane is None else acc[:, sum_lane:sum_lane + 1]
                o_ref[:, cols] = acc / l
                lse_ref[hh] = m_ref[hh] + jnp.log(l)

    in_specs = [pl.BlockSpec((tq, group * dq), lambda g, i: (i, g)),
                pl.BlockSpec((Sk, group * dq), lambda g, i: (0, kcol(g))),
                pl.BlockSpec((Sk, group * dv), lambda g, i: (0, vcol(g)))]
    ins = [q, k, v]
    scratch = [pltpu.VMEM((group, tq, dv), F32), pltpu.VMEM((group, tq, 1), F32)]
    if sb:
        ins.append(_triangle(sub, inclusive_prefix=False))
        in_specs.append(pl.BlockSpec((sub, sub), lambda g, i: (0, 0)))
    else:
        scratch.append(pltpu.VMEM((group, tq, 1), F32))
    out_specs = [pl.BlockSpec((tq, group * dv), lambda g, i: (i, g)), pl.BlockSpec((group, tq, 1), lambda g, i: (g, i, 0))]
    out_shape = [jax.ShapeDtypeStruct((S, heads * dv), F32), jax.ShapeDtypeStruct((heads, S, 1), F32)]
    if side_gather is not None:
        ins.append(side_gather)
        in_specs.append(ANY)
        out_specs.append(ANY)
        out_shape.append(jax.ShapeDtypeStruct((N_CHIPS,) + side_gather.shape, side_gather.dtype))
        scratch += [pltpu.SemaphoreType.DMA((GATHER_COPIES,)), pltpu.SemaphoreType.DMA((GATHER_COPIES,))]
    outs = pl.pallas_call(body, name=name, grid=(heads // group, S // tq), in_specs=in_specs, out_specs=out_specs,
                          out_shape=out_shape, scratch_shapes=scratch, compiler_params=_params(2))(*ins)
    return outs if side_gather is None else (outs[0], outs[1], _place_own_slot(outs[2], side_gather))


def _attn_bwd(q, k, v, o, do, lse, *, sb, causal, heads, dq, dv, kcol=None, vcol=None, side_exchange=None, name):
    S, Sk = q.shape[0], k.shape[0]
    tq, tk = min(ATT_TQ, S), min(ATT_TK, Sk)
    sub = min(SB_SUB, tk) if sb else tk
    assert tq % sub == 0 or not causal
    nq = S // tq
    kcol = kcol or (lambda h: h)
    vcol = vcol or (lambda h: h)
    n_in = 7 if sb else 6

    def body(*refs):
        if side_exchange is not None:
            start, finish = _partials_exchange(refs[n_in], refs[n_in + 4], *refs[-2:])
            pl.when((pl.program_id(0) == 0) & (pl.program_id(1) == 0))(start)
            pl.when((pl.program_id(0) == heads - 1) & (pl.program_id(1) == 0))(finish)
            refs = refs[:n_in] + refs[n_in + 1:n_in + 4] + refs[n_in + 5:-2]
        if sb:
            q_ref, k_ref, v_ref, o_ref, do_ref, lse_ref, u_ref, dq_ref, dk_ref, dv_ref, acc_ref, r_ref, re_ref = refs
            r_ref[...] = jnp.zeros_like(r_ref)
            re_ref[...] = jnp.zeros_like(re_ref)
        else:
            q_ref, k_ref, v_ref, o_ref, do_ref, lse_ref, dq_ref, dk_ref, dv_ref, acc_ref = refs
        first_row = pl.program_id(1) * tq

        @pl.when(first_row == 0)
        def _():
            dk_ref[...] = jnp.zeros_like(dk_ref)
            dv_ref[...] = jnp.zeros_like(dv_ref)

        qb = q_ref[...]
        dof = do_ref[...].astype(F32)
        dob = dof.astype(BF16)
        q_t, do_t = qb.T, dob.T
        if not sb:
            dlt = jnp.sum(dof * o_ref[...], axis=1, keepdims=True)
        acc_ref[...] = jnp.zeros_like(acc_ref)
        nblk = (first_row + tq) // sub if causal else Sk // sub
        nfull = (first_row + (0 if sb else 1)) // sub if causal else nblk
        n_cut = tq // sub if causal else 0

        def products(jj):
            off = pl.multiple_of(jj * sub, sub)
            return _nt(qb, k_ref[pl.ds(off, sub), :]), _nt(dob, v_ref[pl.ds(off, sub), :])

        def piece(jj, now, masked):
            off = pl.multiple_of(jj * sub, sub)
            kb = k_ref[pl.ds(off, sub), :]
            s, dp = now
            if masked:
                qpos = first_row + lax.broadcasted_iota(jnp.int32, (tq, sub), 0)
                kpos = off + lax.broadcasted_iota(jnp.int32, (tq, sub), 1)
                valid = (kpos < qpos) if sb else (kpos <= qpos)
            if sb:
                u = u_ref[...]
                sp = _softplus(s)
                ls = jnp.where(valid, -sp, 0.0) if masked else -sp
                lb = s - sp
                w = jnp.exp(lb + (lse_ref[0] - (r_ref[...] + _running_sum(ls, u))))
                if masked:
                    w = jnp.where(valid, w, 0.0)
                e = dp * w
                ds = e - jnp.exp(lb) * (re_ref[...] + _running_sum(e, u, split=False))
                if masked:
                    ds = jnp.where(valid, ds, 0.0)
                r_ref[...] += jnp.sum(ls, axis=1, keepdims=True)
                re_ref[...] += jnp.sum(e, axis=1, keepdims=True)
            else:
                w = jnp.exp(s - lse_ref[0])
                if masked:
                    w = jnp.where(valid, w, 0.0)
                ds = w * (dp - dlt)
            dsb = ds.astype(BF16)
            dv_ref[:, pl.ds(off, sub)] += jnp.dot(do_t, w.astype(BF16), preferred_element_type=F32)
            dk_ref[:, pl.ds(off, sub)] += jnp.dot(q_t, dsb, preferred_element_type=F32)
            acc_ref[...] += jnp.dot(dsb, kb, preferred_element_type=F32)

        n_loop = nfull if causal else nblk - 1
        per_trip = tk // sub

        def steps(first, count, masked):
            ready = [products(first + c) for c in range(count)]
            for c in range(count):
                piece(first + c, ready[c], masked)

        def trip(t, carry):
            steps(t * per_trip, per_trip, False)
            return carry

        lax.fori_loop(0, n_loop // per_trip, trip, 0)
        steps(n_loop, n_cut if causal else 1, causal)
        dq_ref[...] = acc_ref[...]

    ins = [q, k, v, o, do]
    in_specs = [pl.BlockSpec((tq, dq), lambda h, i: (i, h)),
                pl.BlockSpec((Sk, dq), lambda h, i: (0, kcol(h))),
                pl.BlockSpec((Sk, dv), lambda h, i: (0, vcol(h))),
                pl.BlockSpec((tq, dv), lambda h, i: (i, h)),
                pl.BlockSpec((tq, dv), lambda h, i: (i, h))]
    scratch = [pltpu.VMEM((tq, dq), F32)]
    ins.append(lse)
    in_specs.append(pl.BlockSpec((1, tq, 1), lambda h, i: (h, i, 0)))
    if sb:
        ins.append(_triangle(sub, inclusive_prefix=True))
        in_specs.append(pl.BlockSpec((sub, sub), lambda h, i: (0, 0)))
        scratch += [pltpu.VMEM((tq, 1), F32), pltpu.VMEM((tq, 1), F32)]
    out_specs = [pl.BlockSpec((tq, dq), lambda h, i: (i, h)),
                 pl.BlockSpec((dq, Sk), lambda h, i: (h, 0)),
                 pl.BlockSpec((dv, Sk), lambda h, i: (h, 0))]
    out_shape = [jax.ShapeDtypeStruct((S, heads * dq), F32), jax.ShapeDtypeStruct((heads * dq, Sk), F32),
                 jax.ShapeDtypeStruct((heads * dv, Sk), F32)]
    if side_exchange is not None:
        _, R, C = side_exchange.shape
        ins.append(side_exchange)
        in_specs.append(ANY)
        out_specs.append(ANY)
        out_shape.append(jax.ShapeDtypeStruct((N_DEVICES, R // 2, C), side_exchange.dtype))
        scratch += [pltpu.SemaphoreType.DMA((N_DEVICES,)), pltpu.SemaphoreType.DMA((N_DEVICES,))]
    return pl.pallas_call(body, name=name, grid=(heads, nq), in_specs=in_specs, out_specs=out_specs,
                          out_shape=out_shape, scratch_shapes=scratch, compiler_params=_params(2))(*ins)


GELU_C = 0.7978845608028654
assert 2 * SG_GD == LANE and SG_CHUNK == LANE


def _gelu(z):
    t = jnp.tanh(GELU_C * (z + 0.044715 * z * z * z))
    return 0.5 * z * (1.0 + t), t


def _gelu_grad(z, t):
    return 0.5 * (1.0 + t) + 0.5 * z * (1.0 - t * t) * GELU_C * (1.0 + 3.0 * 0.044715 * z * z)


def _layernorm_parts(g):
    d = g - jnp.mean(g, axis=-1, keepdims=True)
    rstd = lax.rsqrt(jnp.mean(d * d, axis=-1, keepdims=True) + EPS)
    return d * rstd, rstd


def _gelu_ln(z, gain, bias, *, name):
    def fn(zv, gn, bs):
        a, _ = _gelu(zv)
        y, _ = _layernorm_parts(a[:, SG_W:])
        return a[:, :SG_W], y * gn + bs

    return _rowwise(fn, [z], [gain.reshape(1, SG_W), bias.reshape(1, SG_W)], [(SG_W, F32), (SG_W, BF16)], name=name)


def _gelu_ln_bwd(z, du, dgl, gain, *, name):
    def fn(zv, duv, dglv, gn):
        a, t = _gelu(zv)
        y, rstd = _layernorm_parts(a[:, SG_W:])
        dy = dglv * gn
        dgg = rstd * (dy - jnp.mean(dy, axis=-1, keepdims=True) - y * jnp.mean(dy * y, axis=-1, keepdims=True))
        dz = jnp.concatenate([duv, dgg], axis=1) * _gelu_grad(zv, t)
        return dz, jnp.sum(dglv * y, axis=0, keepdims=True), jnp.sum(dglv, axis=0, keepdims=True)

    return _rowwise(fn, [z, du, dgl], [gain.reshape(1, SG_W)], [(2 * SG_W, BF16)], [(1, SG_W), (1, SG_W)], name=name)


def _sg_masks():
    tri = lax.broadcasted_iota(jnp.int32, (SG_CHUNK, SG_CHUNK), 0) >= lax.broadcasted_iota(jnp.int32, (SG_CHUNK, SG_CHUNK), 1)
    first = lax.broadcasted_iota(jnp.int32, (SG_CHUNK, LANE), 1) < SG_GD
    return tri, first


def _spatial(gl, u, w, bt, *, name):
    S = gl.shape[0]
    tm = _row_tile(S, 512)
    nch = tm // SG_CHUNK

    def body(gl_ref, u_ref, w_ref, bt_ref, o_ref):
        tri, first = _sg_masks()
        for p in range(SG_W // LANE):
            cols = slice(p * LANE, (p + 1) * LANE)
            wa = jnp.where(tri, w_ref[2 * p], 0.0).astype(BF16)
            wb = jnp.where(tri, w_ref[2 * p + 1], 0.0).astype(BF16)
            for ci in range(nch):
                rws = slice(ci * SG_CHUNK, (ci + 1) * SG_CHUNK)
                g = gl_ref[rws, cols]
                zero = jnp.zeros_like(g)
                mixed = (jnp.dot(wa, jnp.where(first, g, zero), preferred_element_type=F32)
                         + jnp.dot(wb, jnp.where(first, zero, g), preferred_element_type=F32) + bt_ref[:, cols])
                o_ref[rws, cols] = u_ref[rws, cols] * mixed

    row = pl.BlockSpec((tm, SG_W), lambda i: (i, 0))
    return pl.pallas_call(
        body, name=name, grid=(S // tm,),
        in_specs=[row, row, pl.BlockSpec(w.shape, lambda i: (0, 0, 0)), pl.BlockSpec(bt.shape, lambda i: (0, 0))],
        out_specs=row, out_shape=jax.ShapeDtypeStruct((S, SG_W), F32), compiler_params=_params(1))(gl, u, w, bt)


def _spatial_bwd(d_o, gl, u, w, bt, *, name):
    S = gl.shape[0]
    tm = _row_tile(S, 512)
    nch = tm // SG_CHUNK
    nsteps = S // tm

    def body(do_ref, gl_ref, u_ref, w_ref, bt_ref, du_ref, dgl_ref, dw_ref, db_ref, dbt_ref):
        tri, first = _sg_masks()
        step = pl.program_id(0)

        @pl.when(step == 0)
        def _():
            dw_ref[...] = jnp.zeros_like(dw_ref)
            dbt_ref[...] = jnp.zeros_like(dbt_ref)

        for p in range(SG_W // LANE):
            cols = slice(p * LANE, (p + 1) * LANE)
            wa = jnp.where(tri, w_ref[2 * p], 0.0).astype(BF16)
            wb = jnp.where(tri, w_ref[2 * p + 1], 0.0).astype(BF16)
            for ci in range(nch):
                rws = slice(ci * SG_CHUNK, (ci + 1) * SG_CHUNK)
                g = gl_ref[rws, cols]
                zero = jnp.zeros_like(g)
                mixed = (jnp.dot(wa, jnp.where(first, g, zero), preferred_element_type=F32)
                         + jnp.dot(wb, jnp.where(first, zero, g), preferred_element_type=F32) + bt_ref[:, cols])
                dov = do_ref[rws, cols]
                du_ref[rws, cols] = dov * mixed
                dm = dov * u_ref[rws, cols]
                dbt_ref[:, cols] += dm
                dma = jnp.where(first, dm, 0.0).astype(BF16)
                dmb = jnp.where(first, 0.0, dm).astype(BF16)
                dw_ref[2 * p] += jnp.where(tri, _nt(dma, g), 0.0)
                dw_ref[2 * p + 1] += jnp.where(tri, _nt(dmb, g), 0.0)
                dgl_ref[rws, cols] = _tn(wa, dma) + _tn(wb, dmb)

        @pl.when(step == nsteps - 1)
        def _():
            lane = lax.broadcasted_iota(jnp.int32, (SG_CHUNK, LANE), 1)
            acc = jnp.zeros((SG_CHUNK, LANE), F32)
            for p in range(SG_W // LANE):
                blk = dbt_ref[:, p * LANE:(p + 1) * LANE]
                sa = jnp.sum(jnp.where(first, blk, 0.0), axis=1, keepdims=True)
                sb_ = jnp.sum(jnp.where(first, 0.0, blk), axis=1, keepdims=True)
                acc = acc + jnp.where(lane == 2 * p, sa, 0.0) + jnp.where(lane == 2 * p + 1, sb_, 0.0)
            db_ref[...] = acc

    row = pl.BlockSpec((tm, SG_W), lambda i: (i, 0))
    return pl.pallas_call(
        body, name=name, grid=(nsteps,),
        in_specs=[row, row, row, pl.BlockSpec(w.shape, lambda i: (0, 0, 0)), pl.BlockSpec(bt.shape, lambda i: (0, 0))],
        out_specs=[row, row, pl.BlockSpec(w.shape, lambda i: (0, 0, 0)), pl.BlockSpec((SG_CHUNK, LANE), lambda i: (0, 0))],
        out_shape=[jax.ShapeDtypeStruct((S, SG_W), F32), jax.ShapeDtypeStruct((S, SG_W), F32),
                   jax.ShapeDtypeStruct(w.shape, F32), jax.ShapeDtypeStruct((SG_CHUNK, LANE), F32)],
        scratch_shapes=[pltpu.VMEM((SG_CHUNK, SG_W), F32)], compiler_params=_params(1))(d_o, gl, u, w, bt)


ROPE_HALF = MLA_ROPE // 2
KR_COL = (MLA_QL + MLA_KVL) // LANE
MLA_IN_PAD = MLA_QL + MLA_KVL + LANE


def _rope_tables(positions):
    inv_freq = ROPE_THETA ** (-jnp.arange(ROPE_HALF, dtype=F32) / ROPE_HALF)
    ang = positions.astype(F32)[:, None] * inv_freq
    cos, sin = jnp.cos(ang), jnp.sin(ang)
    S = positions.shape[0]
    z16, tail = jnp.zeros((S, ROPE_HALF), F32), jnp.zeros((S, LANE - MLA_QK), F32)
    ones = jnp.ones((S, MLA_NOPE), F32)
    zeros = jnp.zeros((S, MLA_NOPE), F32)
    return (jnp.concatenate([ones, cos, cos, tail], axis=1), jnp.concatenate([zeros, z16, sin, tail], axis=1),
            jnp.concatenate([zeros, -sin, z16, tail], axis=1))


def _rope(x, cos, sa, sb):
    return x * cos + pltpu.roll(x, ROPE_HALF, 1) * sa + pltpu.roll(x, LANE - ROPE_HALF, 1) * sb


def _rope_t(dy, cos, sa, sb):
    return dy * cos + pltpu.roll(dy * sa, LANE - ROPE_HALF, 1) + pltpu.roll(dy * sb, ROPE_HALF, 1)


def _mla_lora(P, qlg, kvlg, *, name):
    def fn(pv, a, b):
        return _rmsnorm_fwd(pv[:, :MLA_QL], a), _rmsnorm_fwd(pv[:, MLA_QL:MLA_QL + MLA_KVL], b)

    return _rowwise(fn, [P], [qlg.reshape(1, MLA_QL), kvlg.reshape(1, MLA_KVL)], [(MLA_QL, BF16), (MLA_KVL, BF16)], name=name)


def _mla_lora_bwd(dcq, dckv, dkr, P, qlg, kvlg, *, name):
    def fn(d1, d2, d3, pv, a, b):
        x1, g1 = _rmsnorm_bwd(d1, pv[:, :MLA_QL], a)
        x2, g2 = _rmsnorm_bwd(d2, pv[:, MLA_QL:MLA_QL + MLA_KVL], b)
        return jnp.concatenate([x1, x2, d3], axis=1), g1, g2

    return _rowwise(fn, [dcq, dckv, dkr, P], [qlg.reshape(1, MLA_QL), kvlg.reshape(1, MLA_KVL)], [(MLA_IN_PAD, BF16)],
                    [(1, MLA_QL), (1, MLA_KVL)], name=name)


def _mla_qk(q_pre, k_pre, P, tabs, qg, kg, *, name):
    def fn(qp, kp, kr, c, a, b, qgv, kgv):
        return (_rope(_rmsnorm_fwd(qp, qgv, MLA_QK), c, a, b) * MLA_SCALE,
                _rope(_rmsnorm_fwd(kp + kr, kgv, MLA_QK), c, a, b))

    hcol = lambda h: h
    rows = [(q_pre, LANE, hcol), (k_pre, LANE, hcol), (P, LANE, lambda h: KR_COL), *tabs]
    w = MLA_HEADS * LANE
    return _rowwise(fn, rows, [qg, kg], [(w, BF16), (w, BF16)], heads=MLA_HEADS, tm=HEAD_ROWS, name=name)


def _mla_qk_bwd(dq, dk_t, q_pre, k_pre, P, tabs, qg, kg, *, name):
    def fn(dqv, dkv, qp, kp, kr, c, a, b, qgv, kgv):
        dqp, dqg = _rmsnorm_bwd(_rope_t(dqv * MLA_SCALE, c, a, b), qp, qgv, MLA_QK)
        dkp, dkg = _rmsnorm_bwd(_rope_t(dkv, c, a, b), kp + kr, kgv, MLA_QK)
        lane = lax.broadcasted_iota(jnp.int32, (1, LANE), 1)
        return dqp, dkp, jnp.where((lane >= MLA_NOPE) & (lane < MLA_QK), dkp, 0.0), dqg, dkg

    hcol = lambda h: h
    rows = [(dq, LANE, hcol), (dk_t, LANE, hcol, True), (q_pre, LANE, hcol), (k_pre, LANE, hcol),
            (P, LANE, lambda h: KR_COL), *tabs]
    w = MLA_HEADS * LANE
    return _rowwise(fn, rows, [qg, kg], [(w, BF16), (w, BF16)], [(1, LANE), (1, LANE)], [LANE], heads=MLA_HEADS,
                    tm=HEAD_ROWS, name=name)


def _head_norm(x, g, *, heads, width, colfn=None, scale=1.0, name):
    return _rowwise(lambda xv, gv: _rmsnorm_fwd(xv, gv) * scale, [(x, width, colfn or (lambda h: h))],
                    [g.reshape(1, width)], [(heads * width, BF16)], heads=heads, tm=HEAD_ROWS, name=name)[0]


def _head_norm_bwd(dy, x, g, *, heads, width, colfn=None, scale=1.0, out_dtype, name):
    return _rowwise(lambda dv_, xv, gv: _rmsnorm_bwd(dv_ * scale, xv, gv),
                    [(dy, width, lambda h: h), (x, width, colfn or (lambda h: h))],
                    [g.reshape(1, width)], [(heads * width, out_dtype)], [(1, width)], heads=heads, tm=HEAD_ROWS,
                    name=name)


def _loss_grad(y, tgt, *, name):
    D = y.shape[1]

    def fn(yv, tv):
        d = yv - tv
        return d * (1.0 / D), d * (1.0 / D), jnp.sum(d * d, axis=0, keepdims=True) * (0.5 / D)

    dy, dyb, part = _rowwise(fn, [y, tgt], [], [(D, F32), (D, BF16)], [(1, D)], tm=1024, name=name)
    return jnp.sum(part), (dy, dyb)


def _adamw(w, g, m, v, *, name):
    shape = w.shape
    two_d = (-1, shape[-1])

    def fn(wv, gv, mv, vv):
        m2 = ADAM_B1 * mv + (1.0 - ADAM_B1) * gv
        v2 = ADAM_B2 * vv + (1.0 - ADAM_B2) * (gv * gv)
        m_hat = m2 / (1.0 - ADAM_B1 ** ADAM_STEP)
        v_hat = v2 / (1.0 - ADAM_B2 ** ADAM_STEP)
        return -ADAM_LR * (m_hat / (jnp.sqrt(v_hat) + ADAM_EPS) + ADAM_WD * wv), m2, v2

    outs = _rowwise(fn, [t.reshape(two_d) for t in (w, g, m, v)], [], [(shape[-1], F32)] * 3, tm=256, name=name)
    return [o.reshape(shape) for o in outs]


def _pad_cols(w, heads, hd):
    k = w.shape[0]
    return jnp.pad(w.reshape(k, heads, hd), ((0, 0), (0, 0), (0, LANE - hd))).reshape(k, heads * LANE)


def _unpad_cols(w, heads, hd):
    k = w.shape[0]
    return w.reshape(k, heads, LANE)[:, :, :hd].reshape(k, heads * hd)


def _pad_rows(w, heads, hd):
    n = w.shape[1]
    return jnp.pad(w.reshape(heads, hd, n), ((0, 0), (0, LANE - hd), (0, 0))).reshape(heads * LANE, n)


def _unpad_rows(w, heads, hd):
    n = w.shape[1]
    return w.reshape(heads, LANE, n)[:, :hd, :].reshape(heads * hd, n)


def _ffn_fwd(x, g, wgu, wd, tag):
    h = _norm_rows(x, g, name=tag + "_norm")
    gate, up, act = _mm_swiglu(h, wgu, name=tag + "_gu")
    y = _mm(act, wd, scale=0.5, residual=x, name=tag + "_down")
    return y, (x, h, gate, up, act)


def _ffn_bwd(dy, saved, g, wgu, wd, tag):
    x, h, gate, up, act = saved
    F = wd.shape[0]
    dy, dyb = dy
    dwd = _mm(act, dyb, ta=True, scale=0.5, name=tag + "_dwd")
    dgate, dup = _mm_dswiglu(dyb, wd, gate, up, scale=0.5, name=tag + "_dact")
    dh = _mm(dgate, wgu, tb=True, name=tag + "_dh_g")
    dx, dg, dxb = _mm(dup, wgu, tb=True, b_off=(0, F), residual=dh, norm_bwd=(x, g, dy), name=tag + "_dh_u")
    dwgu = jnp.concatenate([_mm(h, dgate, ta=True, name=tag + "_dwg"), _mm(h, dup, ta=True, name=tag + "_dwu")], axis=1)
    return (dx, dxb), dg, dwgu, dwd


def _even_weights(w_in, w_out):
    parts = [w_in[:, :SB_W] * SB_SCALE, w_in[:, SB_W:2 * SB_W], w_in[:, 2 * SB_W:3 * SB_W]]
    wqkv = jnp.concatenate([_pad_cols(p, SB_HEADS, SB_HD) for p in parts], axis=1)
    return wqkv, w_in[:, 3 * SB_W:], _pad_rows(w_out[:SB_W], SB_HEADS, SB_HD), w_out[SB_W:]


def _even_fwd(x, g, wts, ln_g, ln_b, sgu_w, bt, late_shard, tag):
    wqkv, wz, wo_sb, wo_sg = wts
    h = _norm_rows(x, g, name=tag + "_norm")
    qkv = _mm(h, wqkv, out_dtype=BF16, name=tag + "_qkv")
    z = _mm(h, wz, name=tag + "_z")
    o_sb, tot, late = _attn_fwd(qkv, qkv, qkv, sb=True, causal=True, heads=SB_HEADS, dq=LANE, dv=LANE, group=FWD_GROUP,
                                kcol=lambda g: SB_HEADS // FWD_GROUP + g, vcol=lambda g: 2 * SB_HEADS // FWD_GROUP + g,
                                side_gather=late_shard, name=tag + "_sb")
    u, gl = _gelu_ln(z, ln_g, ln_b, name=tag + "_geluln")
    o_sg = _spatial(gl, u, sgu_w, bt, name=tag + "_sgu")
    y = _mm(o_sb, wo_sb, residual=x, name=tag + "_out_sb")
    y = _mm(o_sg, wo_sg, residual=y, name=tag + "_out_sg")
    return y, (x, h, qkv, z, o_sb, tot, u, gl, o_sg), late


def _even_bwd(dy, saved, g, wts, ln_g, sgu_w, bt, partials, tag):
    wqkv, wz, wo_sb, wo_sg = wts
    x, h, qkv, z, o_sb, tot, u, gl, o_sg = saved
    dy, dyb = dy
    do_sb = _mm(dyb, wo_sb, tb=True, name=tag + "_do_sb")
    do_sg = _mm(dyb, wo_sg, tb=True, name=tag + "_do_sg")
    dwo = jnp.concatenate([_unpad_rows(_mm(o_sb, dyb, ta=True, name=tag + "_dwo_sb"), SB_HEADS, SB_HD),
                           _mm(o_sg, dyb, ta=True, name=tag + "_dwo_sg")], axis=0)
    dq, dk_t, dv_t, received = _attn_bwd(qkv, qkv, qkv, o_sb, do_sb, tot, sb=True, causal=True, heads=SB_HEADS, dq=LANE,
                                         dv=LANE, kcol=lambda hh: SB_HEADS + hh, vcol=lambda hh: 2 * SB_HEADS + hh,
                                         side_exchange=partials, name=tag + "_sb_bwd")
    du, dgl, dsgu_w, db_t = _spatial_bwd(do_sg, gl, u, sgu_w, bt, name=tag + "_sgu_bwd")
    dz, dln_g, dln_b = _gelu_ln_bwd(z, du, dgl, ln_g, name=tag + "_geluln_bwd")
    dh = _mm(dz, wz, tb=True, name=tag + "_dh_z")
    dh = _mm(dq, wqkv, tb=True, residual=dh, name=tag + "_dh_q")
    dws = [_unpad_cols(_mm(h, dq, ta=True, scale=SB_SCALE, name=tag + "_dw_q"), SB_HEADS, SB_HD)]
    for i, (d_t, nm) in enumerate(((dk_t, "k"), (dv_t, "v")), start=1):
        dh = _mm(d_t, wqkv, ta=True, tb=True, b_off=(0, i * SB_HEADS * LANE), residual=dh,
                 norm_bwd=(x, g, dy) if nm == "v" else None, name=tag + "_dh_" + nm)
        dws.append(_unpad_rows(_mm(d_t, h, name=tag + "_dw_" + nm), SB_HEADS, SB_HD).T)
    dws.append(_mm(h, dz, ta=True, name=tag + "_dw_z"))
    dx, dg, dxb = dh
    return (dx, dxb), dict(mix_norm=dg, sbg_w_in=jnp.concatenate(dws, axis=1), sgu_ln_gain=dln_g, sgu_ln_bias=dln_b,
                    sgu_w=dsgu_w, sgu_b=db_t[:, :SG_GROUPS].T, sbg_w_out=dwo), received


def _mla_weights(w_in, w_uq, w_ukv, w_out, q_gain, k_gain):
    d = w_in.shape[0]
    lat = MLA_QL + MLA_KVL
    w_in_ext = jnp.concatenate([w_in[:, :lat], jnp.zeros((d, MLA_NOPE), w_in.dtype), w_in[:, lat:],
                                jnp.zeros((d, LANE - MLA_QK), w_in.dtype)], axis=1)
    kv = w_ukv.reshape(MLA_KVL, MLA_HEADS, MLA_NOPE + MLA_V)
    wk = _pad_cols(kv[:, :, :MLA_NOPE].reshape(MLA_KVL, -1), MLA_HEADS, MLA_NOPE)
    wv = _pad_cols(kv[:, :, MLA_NOPE:].reshape(MLA_KVL, -1), MLA_HEADS, MLA_V)
    pad_gain = lambda gn: jnp.pad(gn.reshape(1, MLA_QK), ((0, 0), (0, LANE - MLA_QK)))
    return (w_in_ext, _pad_cols(w_uq, MLA_HEADS, MLA_QK), wk, wv, _pad_rows(w_out, MLA_HEADS, MLA_V),
            pad_gain(q_gain), pad_gain(k_gain))


def _mla_fwd(x, g, wts, qlg, kvlg, tabs, tag):
    w_in, w_uq, wk, wv, w_out, qg, kg = wts
    h = _norm_rows(x, g, name=tag + "_norm")
    P = _mm(h, w_in, name=tag + "_in")
    cqn, ckvn = _mla_lora(P, qlg, kvlg, name=tag + "_lora")
    q_pre = _mm(cqn, w_uq, name=tag + "_uq")
    k_pre = _mm(ckvn, wk, name=tag + "_uk")
    ones_lane = jnp.tile((jnp.arange(LANE) == MLA_V).astype(F32), MLA_HEADS)[None, :]
    v = _mm(ckvn, wv, out_dtype=BF16, bias=ones_lane, name=tag + "_uv")
    q, k = _mla_qk(q_pre, k_pre, P, tabs, qg, kg, name=tag + "_qk")
    o, lse = _attn_fwd(q, k, v, sb=False, causal=True, heads=MLA_HEADS, dq=LANE, dv=LANE, group=FWD_GROUP,
                       sum_lane=MLA_V, name=tag + "_attn")
    y = _mm(o, w_out, residual=x, name=tag + "_out")
    return y, (x, h, P, cqn, ckvn, q_pre, k_pre, q, k, v, o, lse)


def _mla_bwd(dy, saved, g, wts, qlg, kvlg, tabs, tag):
    w_in, w_uq, wk, wv, w_out, qg, kg = wts
    x, h, P, cqn, ckvn, q_pre, k_pre, q, k, v, o, lse = saved
    dy, dyb = dy
    do = _mm(dyb, w_out, tb=True, name=tag + "_do")
    dw_out = _unpad_rows(_mm(o, dyb, ta=True, name=tag + "_dwo"), MLA_HEADS, MLA_V)
    dq, dk_t, dv_t = _attn_bwd(q, k, v, o, do, lse, sb=False, causal=True, heads=MLA_HEADS, dq=LANE, dv=LANE,
                               name=tag + "_attn_bwd")
    dq_pre, dk_pre, dkr, dqg, dkg = _mla_qk_bwd(dq, dk_t, q_pre, k_pre, P, tabs, qg, kg, name=tag + "_qk_bwd")
    dcqn = _mm(dq_pre, w_uq, tb=True, name=tag + "_dcq")
    dckvn = _mm(dk_pre, wk, tb=True, name=tag + "_dckv_k")
    dckvn = _mm(dv_t, wv, ta=True, tb=True, residual=dckvn, name=tag + "_dckv_v")
    dw_uq = _unpad_cols(_mm(cqn, dq_pre, ta=True, name=tag + "_dwuq"), MLA_HEADS, MLA_QK)
    dwk = _unpad_cols(_mm(ckvn, dk_pre, ta=True, name=tag + "_dwk"), MLA_HEADS, MLA_NOPE)
    dwv = _unpad_rows(_mm(dv_t, ckvn, name=tag + "_dwv"), MLA_HEADS, MLA_V).T
    dw_ukv = jnp.concatenate([dwk.reshape(MLA_KVL, MLA_HEADS, MLA_NOPE), dwv.reshape(MLA_KVL, MLA_HEADS, MLA_V)],
                             axis=2).reshape(MLA_KVL, -1)
    dP, dqlg, dkvlg = _mla_lora_bwd(dcqn, dckvn, dkr, P, qlg, kvlg, name=tag + "_lora_bwd")
    dx, dg, dxb = _mm(dP, w_in, tb=True, norm_bwd=(x, g, dy), name=tag + "_dh")
    dx = (dx, dxb)
    dw_in_ext = _mm(h, dP, ta=True, name=tag + "_dwin")
    lat = MLA_QL + MLA_KVL
    dw_in = jnp.concatenate([dw_in_ext[:, :lat], dw_in_ext[:, lat + MLA_NOPE:lat + MLA_QK]], axis=1)
    return dx, dict(mix_norm=dg, mla_w_in=dw_in, mla_q_lora_gain=dqlg, mla_kv_lora_gain=dkvlg, mla_w_uq=dw_uq,
                    mla_w_ukv=dw_ukv, mla_q_gain=dqg[:, :MLA_QK], mla_k_gain=dkg[:, :MLA_QK], mla_w_out=dw_out)


def _xmem_fwd(x, mem, g, gm, wq, wkv, qg, kg, wo, tag):
    hq = _norm_rows(x, g, name=tag + "_norm")
    hm = _norm_rows(mem, gm, name=tag + "_mnorm")
    qp = _mm(hq, wq, name=tag + "_q")
    kv = _mm(hm, wkv, name=tag + "_kv")
    q = _head_norm(qp, qg, heads=MEM_HEADS, width=MEM_HD, scale=MEM_SCALE, name=tag + "_qn")
    kn = _head_norm(kv, kg, heads=MEM_HEADS, width=MEM_HD, colfn=lambda hh: 2 * hh, name=tag + "_kn")
    kvb = kv.reshape(-1, MEM_HEADS, 2, MEM_HD)[:, :, 1].reshape(-1, MEM_HEADS * MEM_HD).astype(BF16)
    o, lse = _attn_fwd(q, kn, kvb, sb=False, causal=False, heads=MEM_HEADS, dq=MEM_HD, dv=MEM_HD, group=MEM_HEADS,
                       name=tag + "_attn")
    y = _mm(o, wo, residual=x, name=tag + "_out")
    return y, (x, hq, hm, qp, kv, q, kn, kvb, o, lse)


def _xmem_bwd(dy, saved, mem, g, gm, wq, wkv, qg, kg, wo, tag):
    x, hq, hm, qp, kv, q, kn, kvb, o, lse = saved
    m = mem.shape[0]
    dy, dyb = dy
    do = _mm(dyb, wo, tb=True, name=tag + "_do")
    dwo = _mm(o, dyb, ta=True, name=tag + "_dwo")
    dq, dk_t, dv_t = _attn_bwd(q, kn, kvb, o, do, lse, sb=False, causal=False, heads=MEM_HEADS, dq=MEM_HD, dv=MEM_HD,
                               name=tag + "_attn_bwd")
    dk, dv = dk_t.T, dv_t.T
    dqp, dqg = _head_norm_bwd(dq, qp, qg, heads=MEM_HEADS, width=MEM_HD, scale=MEM_SCALE, out_dtype=BF16,
                              name=tag + "_qn_bwd")
    dkp, dkg = _head_norm_bwd(dk, kv, kg, heads=MEM_HEADS, width=MEM_HD, colfn=lambda hh: 2 * hh, out_dtype=F32,
                              name=tag + "_kn_bwd")
    dkv = jnp.concatenate([dkp.reshape(m, MEM_HEADS, MEM_HD), dv.reshape(m, MEM_HEADS, MEM_HD)], axis=2).reshape(m, -1)
    dwkv = _mm(hm, dkv, ta=True, name=tag + "_dwkv")
    dhm = _mm(dkv, wkv, tb=True, name=tag + "_dhm")
    _, dgm = _norm_rows_bwd(dhm, mem, gm, None, name=tag + "_dmnorm")
    dwq = _mm(hq, dqp, ta=True, name=tag + "_dwq")
    dx, dg, dxb = _mm(dqp, wq, tb=True, norm_bwd=(x, g, dy), name=tag + "_dhq")
    dx = (dx, dxb)
    return dx, dict(xmem_norm=dg, xmem_mem_norm=dgm, xmem_wq=dwq, xmem_wkv=dwkv, xmem_q_gain=dqg, xmem_k_gain=dkg,
                    xmem_wo=dwo)


def _local_step(x, mem, positions, tgt, w, late_shard, finish_late, partials_so_far):
    tabs = _rope_tables(positions)
    even = _even_weights(w["sbg_w_in"][0], w["sbg_w_out"][0])
    bt = jnp.repeat(w["sgu_b"][0].T, SG_GD, axis=1)
    saved = []
    for l in range(2):
        t = f"l{l}"
        x, s_pre = _ffn_fwd(x, w["ffn_pre_norm"][l], w["ffn_pre_w_gu"][l], w["ffn_pre_w_down"][l], t + "_pre")
        if l == 0:
            x, s_mix, late = _even_fwd(x, w["mix_norm"][0], even, w["sgu_ln_gain"][0], w["sgu_ln_bias"][0], w["sgu_w"][0],
                                       bt, late_shard, t + "_even")
            finish_late(late)
            mla = _mla_weights(w["mla_w_in"][0], w["mla_w_uq"][0], w["mla_w_ukv"][0], w["mla_w_out"][0],
                               w["mla_q_gain"][0], w["mla_k_gain"][0])
        else:
            x, s_mix = _mla_fwd(x, w["mix_norm"][1], mla, w["mla_q_lora_gain"][0], w["mla_kv_lora_gain"][0], tabs,
                                t + "_mla")
        x, s_xm = _xmem_fwd(x, mem, w["xmem_norm"][l], w["xmem_mem_norm"][l], w["xmem_wq"][l], w["xmem_wkv"][l],
                            w["xmem_q_gain"][l], w["xmem_k_gain"][l], w["xmem_wo"][l], t + "_xm")
        x, s_post = _ffn_fwd(x, w["ffn_post_norm"][l], w["ffn_post_w_gu"][l], w["ffn_post_w_down"][l], t + "_post")
        saved.append((s_pre, s_mix, s_xm, s_post))
    loss, dx = _loss_grad(x, tgt, name="loss")
    grads = {}

    def put(name, l, val):
        grads.setdefault(name, {})[l] = val

    for l in (1, 0):
        t = f"l{l}"
        s_pre, s_mix, s_xm, s_post = saved[l]
        dx, dg, dwgu, dwd = _ffn_bwd(dx, s_post, w["ffn_post_norm"][l], w["ffn_post_w_gu"][l], w["ffn_post_w_down"][l],
                                     t + "_post")
        put("ffn_post_norm", l, dg), put("ffn_post_w_gu", l, dwgu), put("ffn_post_w_down", l, dwd)
        dx, gx = _xmem_bwd(dx, s_xm, mem, w["xmem_norm"][l], w["xmem_mem_norm"][l], w["xmem_wq"][l], w["xmem_wkv"][l],
                           w["xmem_q_gain"][l], w["xmem_k_gain"][l], w["xmem_wo"][l], t + "_xm")
        for k_, v_ in gx.items():
            put(k_, l, v_)
        if l == 0:
            dx, gm, received = _even_bwd(dx, s_mix, w["mix_norm"][0], even, w["sgu_ln_gain"][0], w["sgu_w"][0], bt,
                                         partials_so_far(grads), t + "_even")
        else:
            dx, gm = _mla_bwd(dx, s_mix, w["mix_norm"][1], mla, w["mla_q_lora_gain"][0], w["mla_kv_lora_gain"][0], tabs,
                              t + "_mla")
        for k_, v_ in gm.items():
            put(k_, l if k_ == "mix_norm" else 0, v_)
        dx, dg, dwgu, dwd = _ffn_bwd(dx, s_pre, w["ffn_pre_norm"][l], w["ffn_pre_w_gu"][l], w["ffn_pre_w_down"][l],
                                     t + "_pre")
        put("ffn_pre_norm", l, dg), put("ffn_pre_w_gu", l, dwgu), put("ffn_pre_w_down", l, dwd)
    return loss, dx[0], {k_: [v_[l] for l in sorted(v_)] for k_, v_ in grads.items()}, received


N_CHIPS = 4
PACK_COLS = 1024
PACK_ROW_MULTIPLE = 512


def _place():
    x, y, c = lax.axis_index("x"), lax.axis_index("y"), lax.axis_index("c")
    return x, y, c, [(1 - x, y), (x, 1 - y), (1 - x, 1 - y)]


def _hops(x, y, c):
    return ((x + 1 - c) % 2, (y + c) % 2), ((x + c) % 2, (y + 1 - c) % 2), (1 - x, 1 - y)


GATHER_COPIES = 6
GATHER_STAGES = 4


def _gather_stages(x_ref, out_ref, send_sems, recv_sems):
    Rh = x_ref.shape[0] // 2
    x, y, c = lax.axis_index("x"), lax.axis_index("y"), lax.axis_index("c")
    n1, n2, nd = _hops(x, y, c)
    me, q1, q2, qd = 2 * x + y, 2 * n1[0] + n1[1], 2 * n2[0] + n2[1], 2 * nd[0] + nd[1]
    sibling = (x, y, 1 - c)

    def half(chip, core):
        return out_ref.at[chip, pl.ds(core * Rh, Rh), :]

    def copy(k, chip, core, to, own=False):
        return pltpu.make_async_remote_copy(src_ref=x_ref.at[pl.ds(c * Rh, Rh), :] if own else half(chip, core),
                                            dst_ref=half(chip, core), send_sem=send_sems.at[k], recv_sem=recv_sems.at[k],
                                            device_id=to, device_id_type=MESH)

    sends = [lambda: copy(0, me, c, (*n1, c), own=True), lambda: copy(1, me, c, (*n2, c), own=True),
             lambda: copy(2, q1, c, (*n2, c)), lambda: copy(3, q1, c, sibling), lambda: copy(4, q2, c, sibling),
             lambda: copy(5, qd, c, sibling)]

    def own_halves_out():
        sends[0]().start()
        sends[1]().start()

    def first_neighbours_on():
        copy(0, q1, c, sibling).wait_recv()
        sends[2]().start()
        sends[3]().start()

    def others_to_sibling():
        copy(1, q2, c, sibling).wait_recv()
        sends[4]().start()
        copy(2, qd, c, sibling).wait_recv()
        sends[5]().start()

    def all_landed():
        copy(3, q2, 1 - c, sibling).wait_recv()
        copy(4, q1, 1 - c, sibling).wait_recv()
        copy(5, qd, 1 - c, sibling).wait_recv()
        for send in sends:
            send().wait_send()

    return own_halves_out, first_neighbours_on, others_to_sibling, all_landed


N_DEVICES = 8


def _partials_exchange(g_ref, recv_ref, send_sems, recv_sems):
    Rh = g_ref.shape[1] // 2
    x, y, c = lax.axis_index("x"), lax.axis_index("y"), lax.axis_index("c")

    def copy(k):
        tx, ty, tc = (x + (k >> 2)) % 2, (y + ((k >> 1) & 1)) % 2, (c + (k & 1)) % 2
        return pltpu.make_async_remote_copy(src_ref=g_ref.at[2 * tx + ty, pl.ds(tc * Rh, Rh), :],
                                            dst_ref=recv_ref.at[4 * x + 2 * y + c], send_sem=send_sems.at[k],
                                            recv_sem=recv_sems.at[k], device_id=(tx, ty, tc), device_id_type=MESH)

    def start():
        for k in range(1, N_DEVICES):
            copy(k).start()

    def finish():
        for k in range(1, N_DEVICES):
            copy(k).wait_recv()
            copy(k).wait_send()

    return start, finish


def _sum_partials(g, recv, *, name):
    _, R, C = g.shape
    Rh = R // 2
    tr = _row_tile(Rh, 512)
    nt = Rh // tr
    x, y, c = lax.axis_index("x"), lax.axis_index("y"), lax.axis_index("c")
    where = jnp.stack([2 * x + y, c, 4 * x + 2 * y + c]).astype(jnp.int32)

    def body(where_ref, g_ref, r_ref, o_ref):
        own, mine = g_ref[0], where_ref[2]
        total = None
        for d in range(N_DEVICES):
            term = jnp.where(mine == d, own, r_ref[d].astype(F32))
            total = term if total is None else total + term
        o_ref[...] = total

    spec = pltpu.PrefetchScalarGridSpec(
        num_scalar_prefetch=1, grid=(nt,),
        in_specs=[pl.BlockSpec((1, tr, C), lambda i, wh: (wh[0], wh[1] * nt + i, 0)),
                  pl.BlockSpec((N_DEVICES, tr, C), lambda i, wh: (0, i, 0))],
        out_specs=pl.BlockSpec((tr, C), lambda i, wh: (i, 0)))
    return pl.pallas_call(body, name=name, grid_spec=spec, out_shape=jax.ShapeDtypeStruct((Rh, C), F32),
                          compiler_params=_params(1))(where, g, recv)


def _place_own_slot(others, shard):
    return lax.dynamic_update_slice(others, shard[None], (2 * lax.axis_index("x") + lax.axis_index("y"), 0, 0))


def _gather_chips(shard):
    def body(x_ref, out_ref, send_sems, recv_sems):
        for stage in _gather_stages(x_ref, out_ref, send_sems, recv_sems):
            stage()

    others = pl.pallas_call(
        body, name="gather_weights", out_shape=jax.ShapeDtypeStruct((N_CHIPS,) + shard.shape, shard.dtype),
        in_specs=[ANY], out_specs=ANY,
        scratch_shapes=[pltpu.SemaphoreType.DMA((GATHER_COPIES,)), pltpu.SemaphoreType.DMA((GATHER_COPIES,))])(shard)
    return _place_own_slot(others, shard)


def _gather_devices(block):
    M, N = block.shape

    def body(x_ref, out_ref, send_sems, recv_sems, local_sem):
        x, y, c, chips = _place()
        me, sibling = (x, y, c), (x, y, 1 - c)

        def rows(px, py, pc):
            return out_ref.at[pl.ds((4 * px + 2 * py + pc) * M, M), :]

        def copy(k, blk, to, src=None):
            return pltpu.make_async_remote_copy(src_ref=rows(*blk) if src is None else src, dst_ref=rows(*blk),
                                                send_sem=send_sems.at[k], recv_sem=recv_sems.at[k], device_id=to,
                                                device_id_type=MESH)

        mine = pltpu.make_async_copy(x_ref, rows(*me), local_sem)
        mine.start()
        first = [copy(0, me, sibling, src=x_ref)]
        first += [copy(1 + j, me, (*chip, c), src=x_ref) for j, chip in enumerate(chips)]
        for cp in first:
            cp.start()
        passed = [copy(4 + j, (*chip, c), sibling) for j, chip in enumerate(chips)]
        for j, chip in enumerate(chips):
            copy(1 + j, (*chip, c), me).wait_recv()
            passed[j].start()
        copy(0, sibling, me).wait_recv()
        for j, chip in enumerate(chips):
            copy(4 + j, (*chip, 1 - c), me).wait_recv()
        for cp in first + passed:
            cp.wait_send()
        mine.wait()

    vmem = pl.BlockSpec(memory_space=pltpu.VMEM)
    return pl.pallas_call(
        body, name=f"gather_devices_{M}", out_shape=jax.ShapeDtypeStruct((8 * M, N), block.dtype),
        in_specs=[vmem], out_specs=vmem,
        scratch_shapes=[pltpu.SemaphoreType.DMA((7,)), pltpu.SemaphoreType.DMA((7,)), pltpu.SemaphoreType.DMA],
        compiler_params=pltpu.CompilerParams(vmem_limit_bytes=VMEM_LIMIT))(block)


def _swap_halves(g):
    n, R, C = g.shape
    Rh = R // 2

    def body(g_ref, a_ref, send_sem, recv_sem):
        x, y, c, _ = _place()
        cp = pltpu.make_async_remote_copy(src_ref=g_ref.at[:, pl.ds((1 - c) * Rh, Rh), :], dst_ref=a_ref,
                                          send_sem=send_sem, recv_sem=recv_sem, device_id=(x, y, 1 - c),
                                          device_id_type=MESH)
        cp.start()
        cp.wait()

    return pl.pallas_call(body, name="grad_swap_halves", out_shape=jax.ShapeDtypeStruct((n, Rh, C), g.dtype),
                          in_specs=[ANY], out_specs=ANY,
                          scratch_shapes=[pltpu.SemaphoreType.DMA, pltpu.SemaphoreType.DMA])(g)


def _add_picked(a, b, picks, *, a_row_half=None, out_dtype, name):
    n_out = picks.shape[0]
    _, rows, C = b.shape
    tr = _row_tile(rows, 512)
    nt = rows // tr
    half = jnp.zeros((1,), jnp.int32) if a_row_half is None else a_row_half

    def body(pick_ref, half_ref, a_ref, b_ref, o_ref):
        o_ref[...] = (a_ref[...].astype(F32) + b_ref[...].astype(F32)).astype(o_ref.dtype)

    spec = pltpu.PrefetchScalarGridSpec(
        num_scalar_prefetch=2, grid=(n_out, nt),
        in_specs=[pl.BlockSpec((1, tr, C), lambda j, i, pick, hf: (pick[j], hf[0] * nt + i, 0)),
                  pl.BlockSpec((1, tr, C), lambda j, i, pick, hf: (pick[j], i, 0))],
        out_specs=pl.BlockSpec((1, tr, C), lambda j, i, pick, hf: (j, i, 0)))
    return pl.pallas_call(body, name=name, grid_spec=spec, out_shape=jax.ShapeDtypeStruct((n_out, rows, C), out_dtype),
                          compiler_params=_params(2))(picks.astype(jnp.int32), half.astype(jnp.int32), a, b)


def _hop_exchange(src, hop, *, name):
    def body(s_ref, d_ref, send_sem, recv_sem):
        x, y, c = lax.axis_index("x"), lax.axis_index("y"), lax.axis_index("c")
        cp = pltpu.make_async_remote_copy(src_ref=s_ref, dst_ref=d_ref, send_sem=send_sem, recv_sem=recv_sem,
                                          device_id=(*_hops(x, y, c)[hop], c), device_id_type=MESH)
        cp.start()
        cp.wait()

    return pl.pallas_call(body, name=name, out_shape=jax.ShapeDtypeStruct(src.shape, src.dtype), in_specs=[ANY],
                          out_specs=ANY, scratch_shapes=[pltpu.SemaphoreType.DMA, pltpu.SemaphoreType.DMA])(src)


def _reduce_over_chips(g):
    x, y, c = lax.axis_index("x"), lax.axis_index("y"), lax.axis_index("c")
    n1, n2, _ = _hops(x, y, c)
    chip = lambda p: 2 * p[0] + p[1]
    near = jnp.stack([chip((x, y)), chip(n2)])
    far = jnp.stack([chip(n1), chip((1 - x, 1 - y))])
    half = c.reshape(1)
    sib = _swap_halves(g)
    kept = _add_picked(g, sib, near, a_row_half=half, out_dtype=F32, name="grad_add_near")
    sent = _add_picked(g, sib, far, a_row_half=half, out_dtype=BF16, name="grad_add_far")
    got = _hop_exchange(sent, 0, name="grad_hop_first")
    mine = _add_picked(kept, got, jnp.zeros((1,), jnp.int32), out_dtype=F32, name="grad_add_mine")
    theirs = _add_picked(kept, got, jnp.ones((1,), jnp.int32), out_dtype=BF16, name="grad_add_theirs")
    got = _hop_exchange(theirs, 1, name="grad_hop_second")
    total = _add_picked(mine, got, jnp.zeros((1,), jnp.int32), out_dtype=F32, name="grad_add_total")
    return _join_halves(total[0])


def _sum_slots(b, *, name):
    n, R, C = b.shape
    tr = _row_tile(R, 512)

    def body(b_ref, o_ref):
        acc = b_ref[0]
        for q in range(1, n):
            acc = acc + b_ref[q]
        o_ref[...] = acc

    return pl.pallas_call(body, name=name, grid=(R // tr,), in_specs=[pl.BlockSpec((n, tr, C), lambda i: (0, i, 0))],
                          out_specs=pl.BlockSpec((tr, C), lambda i: (i, 0)), out_shape=jax.ShapeDtypeStruct((R, C), F32),
                          compiler_params=_params(1))(b)


def _join_halves(r):
    Rh, C = r.shape

    def body(r_ref, o_ref, send_sem, recv_sem):
        x, y, c, _ = _place()
        own, other = o_ref.at[pl.ds(c * Rh, Rh), :], o_ref.at[pl.ds((1 - c) * Rh, Rh), :]
        cp = pltpu.make_async_remote_copy(src_ref=r_ref, dst_ref=own, send_sem=send_sem, recv_sem=recv_sem,
                                          device_id=(x, y, 1 - c), device_id_type=MESH)
        cp.start()
        pltpu.make_async_remote_copy(src_ref=r_ref, dst_ref=other, send_sem=send_sem, recv_sem=recv_sem,
                                     device_id=(x, y, 1 - c), device_id_type=MESH).wait_recv()
        cp.wait_send()

    theirs = pl.pallas_call(
        body, name="grad_join_halves", out_shape=jax.ShapeDtypeStruct((2 * Rh, C), r.dtype), in_specs=[ANY], out_specs=ANY,
        scratch_shapes=[pltpu.SemaphoreType.DMA, pltpu.SemaphoreType.DMA])(r)
    return lax.dynamic_update_slice(theirs, r, (lax.axis_index("c") * Rh, 0))


def _size(shape):
    size = 1
    for d in shape:
        size *= d
    return size


def _pack(pieces, cols, row_multiple, dtype):
    if any(p.size % cols for p in pieces):
        flat = jnp.concatenate([p.reshape(-1).astype(dtype) for p in pieces])
        pieces = [jnp.pad(flat, (0, -flat.shape[0] % cols))]
    rows = [p.reshape(-1, cols).astype(dtype) for p in pieces]
    pad = -sum(r.shape[0] for r in rows) % row_multiple
    return jnp.concatenate(rows + ([jnp.zeros((pad, cols), dtype)] if pad else []), axis=0)


def _unpack(buf, shapes):
    cols = buf.shape[1]
    if any(_size(s) % cols for s in shapes):
        flat, out, at = buf.reshape(-1), [], 0
        for shp in shapes:
            out.append(flat[at:at + _size(shp)].reshape(shp))
            at += _size(shp)
        return out
    out, at = [], 0
    for shp in shapes:
        out.append(buf[at:at + _size(shp) // cols].reshape(shp))
        at += _size(shp) // cols
    return out


SHARDED = (("ffn_pre_w_gu", 2), ("ffn_pre_w_down", 1), ("sbg_w_in", 2), ("sbg_w_out", 1), ("mla_w_in", 1),
           ("mla_w_uq", 2), ("mla_w_ukv", 2), ("mla_w_out", 1), ("xmem_wq", 1), ("xmem_wkv", 2), ("xmem_wo", 1),
           ("ffn_post_w_gu", 2), ("ffn_post_w_down", 1))
EARLY = (("ffn_pre_w_gu", 0), ("ffn_pre_w_down", 0), ("sbg_w_in", 0), ("sbg_w_out", 0))
LORA_GAINS = ("mla_q_lora_gain", "mla_kv_lora_gain")
REPLICATED = ("ffn_pre_norm", "mix_norm", "sgu_ln_gain", "sgu_ln_bias", "sgu_w", "sgu_b", "mla_q_gain", "mla_k_gain",
              "xmem_norm", "xmem_mem_norm", "xmem_q_gain", "xmem_k_gain", "ffn_post_norm")
WEIGHTS = ("ffn_pre_norm", "ffn_pre_w_gu", "ffn_pre_w_down", "mix_norm", "sbg_w_in", "sgu_ln_gain", "sgu_ln_bias", "sgu_w",
           "sgu_b", "sbg_w_out", "mla_w_in", "mla_q_lora_gain", "mla_kv_lora_gain", "mla_w_uq", "mla_w_ukv", "mla_q_gain",
           "mla_k_gain", "mla_w_out", "xmem_norm", "xmem_mem_norm", "xmem_wq", "xmem_wkv", "xmem_q_gain", "xmem_k_gain",
           "xmem_wo", "ffn_post_norm", "ffn_post_w_gu", "ffn_post_w_down")
INPUTS = ("x", "mem", "positions") + WEIGHTS + ("loss_target",) + tuple("m_" + n for n in WEIGHTS) + tuple(
    "v_" + n for n in WEIGHTS)


def _step(a):
    x, y, c, _ = _place()
    chip = 2 * x + y
    w = {n: [None] * a[n].shape[0] for n, _ in SHARDED}
    lots = {early: [(n, l, ax) for n, ax in SHARDED for l in range(a[n].shape[0]) if ((n, l) in EARLY) == early]
            for early in (True, False)}

    def packed(lot):
        return _pack([a[n][l] for n, l, _ in lot], PACK_COLS, PACK_ROW_MULTIPLE, BF16)

    def unpack(gathered, lot):
        at = 0
        for n, l, ax in lot:
            shp = a[n].shape[1:]
            rows = _size(shp) // PACK_COLS
            per_chip = gathered[:, at:at + rows].reshape((N_CHIPS,) + shp)
            at += rows
            w[n][l] = jnp.moveaxis(per_chip, 0, ax - 1).reshape(shp[:ax - 1] + (N_CHIPS * shp[ax - 1],) + shp[ax:])

    unpack(_gather_chips(packed(lots[True])), lots[True])
    gains = jnp.zeros((8, LANE), F32)
    for r, n in enumerate(LORA_GAINS):
        gains = gains.at[r, :a[n].shape[1]].set(a[n][0])
    gains = _gather_devices(gains)
    for r, n in enumerate(LORA_GAINS):
        w[n] = jnp.concatenate([gains[16 * q + r, :a[n].shape[1]] for q in range(N_CHIPS)])[None, :]
    for n in REPLICATED:
        w[n] = a[n]

    def packed_grads(lot, grads):
        def part(n, l, ax, q):
            size = a[n].shape[ax]
            return lax.slice_in_dim(grads[n][l], q * size, (q + 1) * size, axis=ax - 1)

        return jnp.stack([_pack([part(n, l, ax, q) for n, l, ax in lot], PACK_COLS, PACK_ROW_MULTIPLE, F32)
                          for q in range(N_CHIPS)])

    sent = {}

    def partials_so_far(grads):
        sent["f32"] = packed_grads(lots[False], grads)
        return sent["f32"].astype(BF16)

    loss, dx, grads, received = _local_step(a["x"][0], a["mem"][0], a["positions"][0], a["loss_target"][0], w,
                                            packed(lots[False]), lambda gathered: unpack(gathered, lots[False]),
                                            partials_so_far)
    loss = lax.psum(loss, ("x", "y", "c"))
    small_names = REPLICATED + LORA_GAINS
    full = {n: jnp.stack(grads[n]).reshape(w[n].shape) for n in small_names}

    reduced = {False: _join_halves(_sum_partials(sent["f32"], received, name="grad_sum_partials")),
               True: _reduce_over_chips(packed_grads(lots[True], grads))}
    layers = {n: [None] * a[n].shape[0] for n, _ in SHARDED}
    for early, lot in lots.items():
        for (n, l, _), piece in zip(lot, _unpack(reduced[early], [a[n].shape[1:] for n, _, _ in lot])):
            layers[n][l] = piece
    gw = {n: jnp.stack(layers[n]) for n, _ in SHARDED}

    small = _pack([full[n] for n in small_names], LANE, 256, F32)
    rows = small.shape[0]
    summed = _sum_slots(_gather_devices(small).reshape(8, rows, LANE), name="grad_sum_devices")
    for n, val in zip(small_names, _unpack(summed, [full[n].shape for n in small_names])):
        if n in LORA_GAINS:
            size = a[n].shape[1]
            val = lax.dynamic_slice_in_dim(val, chip * size, size, axis=1)
        gw[n] = val

    upd = {n: _adamw(a[n], gw[n], a["m_" + n], a["v_" + n], name="adamw_" + n) for n in WEIGHTS}
    return (loss, dx[None], *[gw[n] for n in WEIGHTS], *[upd[n][0] for n in WEIGHTS], *[upd[n][1] for n in WEIGHTS],
            *[upd[n][2] for n in WEIGHTS])


def kernel(x, mem, positions, ffn_pre_norm, ffn_pre_w_gu, ffn_pre_w_down, mix_norm, sbg_w_in, sgu_ln_gain,
           sgu_ln_bias, sgu_w, sgu_b, sbg_w_out, mla_w_in, mla_q_lora_gain, mla_kv_lora_gain, mla_w_uq, mla_w_ukv,
           mla_q_gain, mla_k_gain, mla_w_out, xmem_norm, xmem_mem_norm, xmem_wq, xmem_wkv, xmem_q_gain, xmem_k_gain,
           xmem_wo, ffn_post_norm, ffn_post_w_gu, ffn_post_w_down, loss_target, m_ffn_pre_norm, m_ffn_pre_w_gu,
           m_ffn_pre_w_down, m_mix_norm, m_sbg_w_in, m_sgu_ln_gain, m_sgu_ln_bias, m_sgu_w, m_sgu_b, m_sbg_w_out,
           m_mla_w_in, m_mla_q_lora_gain, m_mla_kv_lora_gain, m_mla_w_uq, m_mla_w_ukv, m_mla_q_gain, m_mla_k_gain,
           m_mla_w_out, m_xmem_norm, m_xmem_mem_norm, m_xmem_wq, m_xmem_wkv, m_xmem_q_gain, m_xmem_k_gain,
           m_xmem_wo, m_ffn_post_norm, m_ffn_post_w_gu, m_ffn_post_w_down, v_ffn_pre_norm, v_ffn_pre_w_gu,
           v_ffn_pre_w_down, v_mix_norm, v_sbg_w_in, v_sgu_ln_gain, v_sgu_ln_bias, v_sgu_w, v_sgu_b, v_sbg_w_out,
           v_mla_w_in, v_mla_q_lora_gain, v_mla_kv_lora_gain, v_mla_w_uq, v_mla_w_ukv, v_mla_q_gain, v_mla_k_gain,
           v_mla_w_out, v_xmem_norm, v_xmem_mem_norm, v_xmem_wq, v_xmem_wkv, v_xmem_q_gain, v_xmem_k_gain,
           v_xmem_wo, v_ffn_post_norm, v_ffn_post_w_gu, v_ffn_post_w_down):
    given = locals()
    return _step({n: given[n] for n in INPUTS})
```
